```python
import math
import jax
import jax.numpy as jnp
from jax import lax
import numpy as np

D_MODEL = 1024
BATCH = 8
SEQ = 2048
DEPTH = 2

HEAD_DIM = 64
N_HEADS = D_MODEL // HEAD_DIM
N_MIXERS = 4
GROUP_HEADS = N_HEADS // N_MIXERS
GROUP_WIDTH = GROUP_HEADS * HEAD_DIM
D_FF = 4 * D_MODEL
NORM_EPS = 1e-6
Q_BLOCK = 128
NEG_INF = -1e30
FORCE = 1e30
TINY = 1e-30
N_BUCKETS = 32
MAX_DISTANCE = 128
N_BIAS_HEADS = 3 * GROUP_HEADS
MOBA_BLOCK = 256
MOBA_TOPK = 3
MOBA_Q_CHUNK = 64
NSA_KV_DIM = HEAD_DIM
CMP_LEN = 32
CMP_STRIDE = 16
CMP_HIDDEN = 256
SLC_LEN = 64
SLC_TOPN = 4
WINDOW = 512
DIFF_HALF = HEAD_DIM // 2
SPLIT_SIZES = ((GROUP_WIDTH,) * 3
               + (GROUP_WIDTH,) * 3
               + (GROUP_WIDTH,)
               + (NSA_KV_DIM,) * 6
               + (3 * GROUP_HEADS,)
               + (GROUP_WIDTH,) * 3)
D_IN = sum(SPLIT_SIZES)

kernel_name = "hybrid_sb_moba_nsa_diff_block"


def _rmsnorm(x, g):
    xf = x.astype(jnp.float32)
    y = xf * lax.rsqrt(jnp.mean(xf * xf, axis=-1, keepdims=True) + NORM_EPS)
    return (y * g.astype(jnp.float32)).astype(x.dtype)


def _masked_softmax(logits, mask):
    s = jnp.where(mask, logits, NEG_INF)
    m = jnp.max(s, axis=-1, keepdims=True)
    e = jnp.where(mask, jnp.exp(s - m), 0.0)
    return e / jnp.maximum(jnp.sum(e, axis=-1, keepdims=True), TINY)


def _t5_bucket(dist):
    n = jnp.maximum(dist, 0)
    max_exact = N_BUCKETS // 2
    nf = jnp.maximum(n, 1).astype(jnp.float32)
    large = max_exact + (jnp.log(nf / max_exact) / math.log(MAX_DISTANCE / max_exact)
                         * (N_BUCKETS - max_exact)).astype(jnp.int32)
    large = jnp.minimum(large, N_BUCKETS - 1)
    return jnp.where(n < max_exact, n, large)


def _rel_bias(dist, table):
    return jnp.moveaxis(table[_t5_bucket(dist)], -1, 0).astype(jnp.float32)


def _rel_bias_per_head(dist, table):
    h = table.shape[1]
    h_idx = jnp.arange(h).reshape((1, h) + (1,) * (dist.ndim - 2))
    return table.T[h_idx, _t5_bucket(dist)].astype(jnp.float32)


def _heads(t, n_heads):
    b, s, _ = t.shape
    return t.reshape(b, s, n_heads, -1).transpose(0, 2, 1, 3)


def _merge_heads(t):
    b, h, s, d = t.shape
    return t.transpose(0, 2, 1, 3).reshape(b, s, h * d)


def _q_blocks(t, block):
    b, h, s = t.shape[:3]
    t = t.reshape((b, h, s // block, block) + t.shape[3:])
    return jnp.moveaxis(t, 2, 0)


def _unblock(t):
    t = jnp.moveaxis(t, 0, 2)
    b, h, n, blk = t.shape[:4]
    return t.reshape((b, h, n * blk) + t.shape[4:])


def stick_breaking_attention(q, k, v):
    b, h, s, d = q.shape
    scale = d ** -0.5
    kpos = jnp.arange(s)

    def one_block(args):
        qb, i = args
        qpos = i * Q_BLOCK + jnp.arange(Q_BLOCK)
        z = jnp.einsum('bhqd,bhkd->bhqk', qb, k).astype(jnp.float32) * scale
        mask = kpos[None, :] < qpos[:, None]
        log_keep = jnp.where(mask, -jax.nn.softplus(z), 0.0)
        suffix = lax.cumsum(log_keep, axis=3, reverse=True) - log_keep
        a = jnp.where(mask, jnp.exp(jax.nn.log_sigmoid(z) + suffix), 0.0)
        return jnp.einsum('bhqk,bhkd->bhqd', a.astype(v.dtype), v)

    out = lax.map(one_block, (_q_blocks(q, Q_BLOCK), jnp.arange(s // Q_BLOCK)))
    return _unblock(out)


def moba_attention(q, k, v, table):
    b, h, s, d = q.shape
    scale = d ** -0.5
    n_blk = -(-s // MOBA_BLOCK)
    pad = n_blk * MOBA_BLOCK - s
    k_blk = jnp.pad(k, ((0, 0), (0, 0), (0, pad), (0, 0))).reshape(b, h, n_blk, MOBA_BLOCK, d)
    v_blk = jnp.pad(v, ((0, 0), (0, 0), (0, pad), (0, 0))).reshape(b, h, n_blk, MOBA_BLOCK, d)
    k_mean = jnp.mean(k_blk.astype(jnp.float32), axis=3)
    n_sel = min(MOBA_TOPK, n_blk - 1)
    blk_ids = jnp.arange(n_blk)
    in_blk = jnp.arange(MOBA_BLOCK)
    b_idx = jnp.arange(b)[:, None, None, None]
    h_idx = jnp.arange(h)[None, :, None, None]

    def one_chunk(args):
        qc, i = args
        qpos = i * MOBA_Q_CHUNK + jnp.arange(MOBA_Q_CHUNK)
        own = (i * MOBA_Q_CHUNK) // MOBA_BLOCK
        k_own = lax.dynamic_index_in_dim(k_blk, own, axis=2, keepdims=False)
        v_own = lax.dynamic_index_in_dim(v_blk, own, axis=2, keepdims=False)
        dist_own = qpos[:, None] - (own * MOBA_BLOCK + in_blk)[None, :]
        s_own = (jnp.einsum('bhqd,bhkd->bhqk', qc, k_own).astype(jnp.float32) * scale
                 + _rel_bias(dist_own, table)[None])
        m_own = jnp.broadcast_to(dist_own >= 0, s_own.shape)
        if n_sel == 0:
            p = _masked_softmax(s_own, m_own)
            return jnp.einsum('bhqk,bhkd->bhqd', p.astype(v.dtype), v_own)
        gate = jnp.einsum('bhqd,bhnd->bhqn', qc.astype(jnp.float32), k_mean)
        gate = jnp.where(blk_ids < own, gate, NEG_INF)
        _, sel = lax.top_k(gate, n_sel)
        k_sel = k_blk[b_idx, h_idx, sel]
        v_sel = v_blk[b_idx, h_idx, sel]
        sel_pos = sel[..., None] * MOBA_BLOCK + in_blk
        dist_sel = qpos[:, None, None] - sel_pos
        s_sel = (jnp.einsum('bhqd,bhqnkd->bhqnk', qc, k_sel).astype(jnp.float32) * scale
                 + _rel_bias_per_head(dist_sel, table))
        m_sel = jnp.broadcast_to((sel < own)[..., None], s_sel.shape)
        n_flat = n_sel * MOBA_BLOCK
        s_all = jnp.concatenate([s_sel.reshape(b, h, MOBA_Q_CHUNK, n_flat), s_own], axis=-1)
        m_all = jnp.concatenate([m_sel.reshape(b, h, MOBA_Q_CHUNK, n_flat), m_own], axis=-1)
        p = _masked_softmax(s_all, m_all)
        p_sel = p[..., :n_flat].reshape(s_sel.shape)
        p_own = p[..., n_flat:]
        return (jnp.einsum('bhqnk,bhqnkd->bhqd', p_sel.astype(v.dtype), v_sel)
                + jnp.einsum('bhqk,bhkd->bhqd', p_own.astype(v.dtype), v_own))

    out = lax.map(one_chunk, (_q_blocks(q, MOBA_Q_CHUNK), jnp.arange(s // MOBA_Q_CHUNK)))
    return _unblock(out)


def nsa_attention(q, k_c, v_c, k_s, v_s, k_w, v_w, gates, pos_k, pos_v, wk1, wk2, wv1, wv2, table):
    b, h, s, d = q.shape
    scale = d ** -0.5
    tpos = jnp.arange(s)
    n_cmp = (s - CMP_LEN) // CMP_STRIDE + 1
    cmp_idx = np.arange(n_cmp)[:, None] * CMP_STRIDE + np.arange(CMP_LEN)[None, :]
    cmp_end = jnp.asarray(cmp_idx[:, -1])

    def compress(t, pos, w1, w2):
        blocks = t[:, cmp_idx] + pos
        return jax.nn.gelu(blocks.reshape(b, n_cmp, CMP_LEN * d) @ w1) @ w2

    kc = compress(k_c, pos_k, wk1, wk2)
    vc = compress(v_c, pos_v, wv1, wv2)
    dist_c = tpos[:, None] - cmp_end[None, :]
    s_c = (jnp.einsum('bhtd,bcd->bhtc', q, kc).astype(jnp.float32) * scale
           + _rel_bias(dist_c, table)[None])
    p_c = _masked_softmax(s_c, dist_c >= 0)
    o_cmp = jnp.einsum('bhtc,bcd->bhtd', p_c.astype(vc.dtype), vc)
    n_slc = s // SLC_LEN
    n_top = min(SLC_TOPN, n_slc)
    s_start = np.arange(n_slc) * SLC_LEN
    cover = np.clip(np.minimum(cmp_idx[:, -1][:, None], (s_start + SLC_LEN - 1)[None, :])
                    - np.maximum(cmp_idx[:, 0][:, None], s_start[None, :]) + 1, 0, None) / CMP_LEN
    importance = jnp.einsum('btc,cj->btj', jnp.sum(p_c, axis=1), jnp.asarray(cover, jnp.float32))
    own = tpos // SLC_LEN
    blk = jnp.arange(n_slc)
    imp = jnp.where(blk[None, :] == own[:, None], FORCE,
                    jnp.where(blk[None, :] < own[:, None], importance, NEG_INF))
    _, sel = lax.top_k(imp, n_top)
    n_qb = s // Q_BLOCK
    sel_b = sel.reshape(b, n_qb, Q_BLOCK, n_top).transpose(1, 0, 2, 3)
    ks_blk = k_s.reshape(b, n_slc, SLC_LEN, d)
    vs_blk = v_s.reshape(b, n_slc, SLC_LEN, d)
    kw_pad = jnp.pad(k_w, ((0, 0), (WINDOW, 0), (0, 0)))
    vw_pad = jnp.pad(v_w, ((0, 0), (WINDOW, 0), (0, 0)))
    b_idx = jnp.arange(b)[:, None, None]
    in_slc = jnp.arange(SLC_LEN)
    in_win = jnp.arange(WINDOW + Q_BLOCK)

    def one_block(args):
        qb, selc, i = args
        qpos = i * Q_BLOCK + jnp.arange(Q_BLOCK)
        k_sel = ks_blk[b_idx, selc]
        v_sel = vs_blk[b_idx, selc]
        dist_s = qpos[:, None, None] - (selc[..., None] * SLC_LEN + in_slc)
        s_s = (jnp.einsum('bhqd,bqnkd->bhqnk', qb, k_sel).astype(jnp.float32) * scale
               + jnp.moveaxis(table[_t5_bucket(dist_s)], -1, 1).astype(jnp.float32))
        m_s = jnp.broadcast_to((dist_s >= 0)[:, None], s_s.shape)
        flat = (b, h, Q_BLOCK, n_top * SLC_LEN)
        p_s = _masked_softmax(s_s.reshape(flat), m_s.reshape(flat)).reshape(s_s.shape)
        o_s = jnp.einsum('bhqnk,bqnkd->bhqd', p_s.astype(v_sel.dtype), v_sel)
        kw = lax.dynamic_slice_in_dim(kw_pad, i * Q_BLOCK, WINDOW + Q_BLOCK, axis=1)
        vw = lax.dynamic_slice_in_dim(vw_pad, i * Q_BLOCK, WINDOW + Q_BLOCK, axis=1)
        kpos = i * Q_BLOCK - WINDOW + in_win
        dist_w = qpos[:, None] - kpos[None, :]
        m_w = (dist_w >= 0) & (dist_w < WINDOW) & (kpos[None, :] >= 0)
        s_w = (jnp.einsum('bhqd,bkd->bhqk', qb, kw).astype(jnp.float32) * scale
               + _rel_bias(dist_w, table)[None])
        p_w = _masked_softmax(s_w, m_w)
        o_w = jnp.einsum('bhqk,bkd->bhqd', p_w.astype(vw.dtype), vw)
        return o_s, o_w

    o_slc, o_win = lax.map(one_block, (_q_blocks(q, Q_BLOCK), sel_b, jnp.arange(n_qb)))
    o_slc, o_win = _unblock(o_slc), _unblock(o_win)
    g = jax.nn.sigmoid(gates.astype(jnp.float32)).reshape(b, s, 3, h).transpose(2, 0, 3, 1)[..., None]
    g = g.astype(q.dtype)
    return g[0] * o_cmp + g[1] * o_slc + g[2] * o_win


def diff_attention(q, k, v, lam, table):
    s = q.shape[2]
    scale = q.shape[-1] ** -0.5
    kpos = jnp.arange(s)

    def one_block(args):
        qb, i = args
        qpos = i * Q_BLOCK + jnp.arange(Q_BLOCK)
        dist = qpos[:, None] - kpos[None, :]
        sc = (jnp.einsum('bhqcd,bhkcd->bhcqk', qb, k).astype(jnp.float32) * scale
              + _rel_bias(dist, table)[None, :, None])
        p = _masked_softmax(sc, dist >= 0)
        w = p[:, :, 0] - lam * p[:, :, 1]
        return jnp.einsum('bhqk,bhkd->bhqd', w.astype(v.dtype), v)

    out = lax.map(one_block, (_q_blocks(q, Q_BLOCK), jnp.arange(s // Q_BLOCK)))
    return _unblock(out)


def setup_inputs(seed: int = 0) -> dict:
    key = jax.random.key(seed)
    ks = jax.random.split(key, 17)
    f32 = jnp.float32

    def nrm(k, shape, scale):
        return jax.random.normal(k, shape, f32) * scale

    return {
        "x": nrm(ks[0], (BATCH, SEQ, D_MODEL), 1.0),
        "w_in": nrm(ks[1], (DEPTH, D_MODEL, D_IN), D_MODEL ** -0.5),
        "w_out": nrm(ks[2], (DEPTH, D_MODEL, D_MODEL), D_MODEL ** -0.5),
        "w_up": nrm(ks[3], (DEPTH, D_MODEL, D_FF), D_MODEL ** -0.5),
        "w_down": nrm(ks[4], (DEPTH, D_FF, D_MODEL), D_FF ** -0.5),
        "norm_attn": 1.0 + nrm(ks[5], (DEPTH, D_MODEL), 0.05),
        "norm_mlp": 1.0 + nrm(ks[6], (DEPTH, D_MODEL), 0.05),
        "cmp_pos_k": nrm(ks[7], (DEPTH, CMP_LEN, HEAD_DIM), 0.1),
        "cmp_pos_v": nrm(ks[8], (DEPTH, CMP_LEN, HEAD_DIM), 0.1),
        "cmp_k_w1": nrm(ks[9], (DEPTH, CMP_LEN * HEAD_DIM, CMP_HIDDEN), (CMP_LEN * HEAD_DIM) ** -0.5),
        "cmp_k_w2": nrm(ks[10], (DEPTH, CMP_HIDDEN, HEAD_DIM), CMP_HIDDEN ** -0.5),
        "cmp_v_w1": nrm(ks[11], (DEPTH, CMP_LEN * HEAD_DIM, CMP_HIDDEN), (CMP_LEN * HEAD_DIM) ** -0.5),
        "cmp_v_w2": nrm(ks[12], (DEPTH, CMP_HIDDEN, HEAD_DIM), CMP_HIDDEN ** -0.5),
        "diff_lambda": nrm(ks[13], (DEPTH, 4, DIFF_HALF), 0.1),
        "diff_subln": 1.0 + nrm(ks[14], (DEPTH, HEAD_DIM), 0.05),
        "rel_bias": nrm(ks[15], (N_BUCKETS, N_BIAS_HEADS), 0.2),
        "final_norm": 1.0 + nrm(ks[16], (D_MODEL,), 0.05),
    }


def reference(x, w_in, w_out, w_up, w_down, norm_attn, norm_mlp, cmp_pos_k, cmp_pos_v,
              cmp_k_w1, cmp_k_w2, cmp_v_w1, cmp_v_w2, diff_lambda, diff_subln, rel_bias, final_norm):
    bias_moba = rel_bias[:, :GROUP_HEADS]
    bias_nsa = rel_bias[:, GROUP_HEADS:2 * GROUP_HEADS]
    bias_diff = rel_bias[:, 2 * GROUP_HEADS:]
    split_at = np.cumsum(SPLIT_SIZES)[:-1].tolist()
    b, s, _ = x.shape
    for layer in range(DEPTH):
        h = _rmsnorm(x, norm_attn[layer])
        proj = h @ w_in[layer]
        (sb_q, sb_k, sb_v, mb_q, mb_k, mb_v, ns_q, ns_kc, ns_vc, ns_ks, ns_vs, ns_kw, ns_vw,
         ns_g, df_q, df_k, df_v) = jnp.split(proj, split_at, axis=-1)
        o_sb = stick_breaking_attention(_heads(sb_q, GROUP_HEADS), _heads(sb_k, GROUP_HEADS),
                                        _heads(sb_v, GROUP_HEADS))
        o_mb = moba_attention(_heads(mb_q, GROUP_HEADS), _heads(mb_k, GROUP_HEADS),
                              _heads(mb_v, GROUP_HEADS), bias_moba)
        o_ns = nsa_attention(_heads(ns_q, GROUP_HEADS), ns_kc, ns_vc, ns_ks, ns_vs, ns_kw, ns_vw, ns_g,
                             cmp_pos_k[layer], cmp_pos_v[layer], cmp_k_w1[layer], cmp_k_w2[layer],
                             cmp_v_w1[layer], cmp_v_w2[layer], bias_nsa)
        lambda_init = 0.8 - 0.6 * math.exp(-0.3 * layer)
        lv = diff_lambda[layer].astype(jnp.float32)
        lam = jnp.exp(jnp.sum(lv[0] * lv[1])) - jnp.exp(jnp.sum(lv[2] * lv[3])) + lambda_init
        dq = df_q.reshape(b, s, GROUP_HEADS, 2, DIFF_HALF).transpose(0, 2, 1, 3, 4)
        dk = df_k.reshape(b, s, GROUP_HEADS, 2, DIFF_HALF).transpose(0, 2, 1, 3, 4)
        o_df = diff_attention(dq, dk, _heads(df_v, GROUP_HEADS), lam, bias_diff)
        o_df = _rmsnorm(o_df, diff_subln[layer]) * (1.0 - lambda_init)
        mixed = jnp.concatenate([_merge_heads(o_sb), _merge_heads(o_mb),
                                 _merge_heads(o_ns), _merge_heads(o_df)], axis=-1)
        x = x + mixed @ w_out[layer]
        h2 = _rmsnorm(x, norm_mlp[layer])
        x = x + jnp.square(jax.nn.relu(h2 @ w_up[layer])) @ w_down[layer]
    return _rmsnorm(x, final_norm)
```

```python
import functools
import math

import numpy as np
import jax
import jax.numpy as jnp
from jax import lax
from jax.experimental import pallas as pl
from jax.experimental.pallas import tpu as pltpu

HEAD_DIM = 64
GROUP_HEADS = 4
GROUP_WIDTH = GROUP_HEADS * HEAD_DIM
NORM_EPS = 1e-6
NEG_INF = -1e30
FORCE = 1e30
TINY = 1e-30
PICKED = -3e38
PAD_SCORE = -2e38
N_BUCKETS = 32
MAX_DISTANCE = 128
MOBA_BLOCK = 256
MOBA_TOPK = 3
CMP_LEN = 32
CMP_STRIDE = 16
SLC_LEN = 64
SLC_TOPN = 4
WINDOW = 512
DIFF_HALF = HEAD_DIM // 2
LANES = 128
TILE = 256
N_GATES = 3 * GROUP_HEADS
COLS_BEFORE_PAD = 9 * GROUP_WIDTH - 2 * HEAD_DIM + N_GATES
PAD_COLS = 2 * HEAD_DIM - N_GATES
CB_SB_Q, CB_SB_K, CB_SB_V, CB_MB_Q, CB_MB_K, CB_MB_V, CB_NS_Q, CB_NS_A, CB_NS_B, CB_DF_Q, CB_DF_K, CB_DF_V = range(12)
D_IN_PAD = 12 * GROUP_WIDTH

_MXU = jnp.bfloat16
_VMEM_LIMIT = 56 * 1024 * 1024


def _dot(a, b):
    return jnp.dot(a, b, preferred_element_type=jnp.float32)


def _dot_nt(a, b, precision=None):
    return lax.dot_general(a, b, (((1,), (1,)), ((), ())), precision=precision,
                           preferred_element_type=jnp.float32)


def _rms(x, g):
    return x * lax.rsqrt(jnp.mean(x * x, axis=-1, keepdims=True) + NORM_EPS) * g


def _params(*sem):
    return pltpu.CompilerParams(dimension_semantics=sem, vmem_limit_bytes=_VMEM_LIMIT)


def _norm_matmul_kernel(x_ref, g_ref, w_ref, o_ref, h_ref):
    @pl.when(pl.program_id(1) == 0)
    def _():
        h_ref[...] = _rms(x_ref[...], g_ref[...]).astype(h_ref.dtype)

    o_ref[...] = _dot(h_ref[...], w_ref[...]).astype(o_ref.dtype)


def _norm_matmul(x, g, w, *, tm=512, tn=1024):
    m, d = x.shape
    n = w.shape[1]
    return pl.pallas_call(
        _norm_matmul_kernel,
        grid=(m // tm, n // tn),
        in_specs=[pl.BlockSpec((tm, d), lambda i, j: (i, 0)),
                  pl.BlockSpec((1, d), lambda i, j: (0, 0)),
                  pl.BlockSpec((d, tn), lambda i, j: (0, j))],
        out_specs=pl.BlockSpec((tm, tn), lambda i, j: (i, j)),
        out_shape=jax.ShapeDtypeStruct((m, n), jnp.float32),
        scratch_shapes=[pltpu.VMEM((tm, d), _MXU)],
        compiler_params=_params("parallel", "arbitrary"),
        name="norm_in_proj",
    )(x, g, w)


def _out_proj_kernel(x_ref, a_ref, b_ref, c_ref, d_ref, w_ref, o_ref):
    acc = x_ref[...]
    for g, ref in enumerate((a_ref, b_ref, c_ref, d_ref)):
        acc = acc + _dot(ref[...].astype(_MXU), w_ref[g * GROUP_WIDTH:(g + 1) * GROUP_WIDTH, :])
    o_ref[...] = acc


def _out_proj(x, groups, w, *, tm=512):
    m, d = x.shape
    gspec = pl.BlockSpec((tm, GROUP_WIDTH), lambda i: (i, 0))
    return pl.pallas_call(
        _out_proj_kernel,
        grid=(m // tm,),
        in_specs=[pl.BlockSpec((tm, d), lambda i: (i, 0)), gspec, gspec, gspec, gspec,
                  pl.BlockSpec((d, d), lambda i: (0, 0))],
        out_specs=pl.BlockSpec((tm, d), lambda i: (i, 0)),
        out_shape=jax.ShapeDtypeStruct((m, d), jnp.float32),
        compiler_params=_params("parallel"),
        name="out_proj_residual",
    )(x, *groups, w)


def _mlp_kernel(x_ref, g_ref, wu_ref, wd_ref, gf_ref, o_ref, h_ref, acc_ref, *, final_norm):
    c = pl.program_id(1)

    @pl.when(c == 0)
    def _():
        h_ref[...] = _rms(x_ref[...], g_ref[...]).astype(h_ref.dtype)
        acc_ref[...] = jnp.zeros_like(acc_ref)

    u = jnp.square(jnp.maximum(_dot(h_ref[...], wu_ref[...]), 0.0))
    acc_ref[...] += _dot(u.astype(_MXU), wd_ref[...])

    @pl.when(c == pl.num_programs(1) - 1)
    def _():
        y = x_ref[...] + acc_ref[...]
        if final_norm:
            y = _rms(y, gf_ref[...])
        o_ref[...] = y


def _mlp(x, g, w_up, w_down, g_final, *, final_norm, tm=512, tf=1024):
    m, d = x.shape
    f = w_up.shape[1]
    return pl.pallas_call(
        functools.partial(_mlp_kernel, final_norm=final_norm),
        grid=(m // tm, f // tf),
        in_specs=[pl.BlockSpec((tm, d), lambda i, c: (i, 0)),
                  pl.BlockSpec((1, d), lambda i, c: (0, 0)),
                  pl.BlockSpec((d, tf), lambda i, c: (0, c)),
                  pl.BlockSpec((tf, d), lambda i, c: (c, 0)),
                  pl.BlockSpec((1, d), lambda i, c: (0, 0))],
        out_specs=pl.BlockSpec((tm, d), lambda i, c: (i, 0)),
        out_shape=jax.ShapeDtypeStruct((m, d), jnp.float32),
        scratch_shapes=[pltpu.VMEM((tm, d), _MXU), pltpu.VMEM((tm, d), jnp.float32)],
        compiler_params=_params("parallel", "arbitrary"),
        name="norm_mlp_residual",
    )(x, g, w_up, w_down, g_final)


def _t5_bucket(dist):
    n = jnp.maximum(dist, 0)
    max_exact = N_BUCKETS // 2
    nf = jnp.maximum(n, 1).astype(jnp.float32)
    large = max_exact + (jnp.log(nf / max_exact) / math.log(MAX_DISTANCE / max_exact)
                         * (N_BUCKETS - max_exact)).astype(jnp.int32)
    large = jnp.minimum(large, N_BUCKETS - 1)
    return jnp.where(n < max_exact, n, large)


def _bias_kernel(tab_ref, o_ref, *, row_stride, col_stride, offset, head0):
    nh, tr, tc = o_ref.shape
    rows = lax.broadcasted_iota(jnp.int32, (tr, tc), 0) + pl.program_id(0) * tr
    cols = lax.broadcasted_iota(jnp.int32, (tr, tc), 1)
    bucket = _t5_bucket(rows * row_stride + cols * col_stride + offset)
    for h in range(nh):
        out = jnp.zeros((tr, tc), jnp.float32)
        for bkt in range(N_BUCKETS):
            out = jnp.where(bucket == bkt, tab_ref[bkt, head0 + h], out)
        o_ref[h] = out


def _bias_table(table, *, n_heads, head0, rows, cols, row_tile, row_stride, col_stride, offset):
    return pl.pallas_call(
        functools.partial(_bias_kernel, row_stride=row_stride, col_stride=col_stride,
                          offset=offset, head0=head0),
        grid=(rows // row_tile,),
        in_specs=[pl.BlockSpec(memory_space=pltpu.SMEM)],
        out_specs=pl.BlockSpec((n_heads, row_tile, cols), lambda i: (0, i, 0)),
        out_shape=jax.ShapeDtypeStruct((n_heads, rows, cols), jnp.float32),
        compiler_params=_params("parallel"),
        name="t5_bias_tiles",
    )(table)


def _softmax_init(t):
    return (jnp.full((t, 1), NEG_INF, jnp.float32), jnp.zeros((t, 1), jnp.float32),
            jnp.zeros((t, HEAD_DIM), jnp.float32))


def _softmax_tile(state, s, mask, vt):
    m, l, acc = state
    if mask is not None:
        s = jnp.where(mask, s, NEG_INF)
    m_new = jnp.maximum(m, jnp.max(s, axis=-1, keepdims=True))
    p = jnp.exp(s - m_new)
    if mask is not None:
        p = jnp.where(mask, p, 0.0)
    alpha = jnp.exp(m - m_new)
    l = alpha * l + jnp.sum(p, axis=-1, keepdims=True)
    acc = alpha * acc + _dot(p.astype(_MXU), vt)
    return m_new, l, acc


def _softmax_out(state):
    _, l, acc = state
    return acc / jnp.maximum(l, TINY)


def _top_k_mask(score, lane_f, k):
    sel = jnp.zeros(score.shape, jnp.float32)
    for _ in range(k):
        mx = jnp.max(score, axis=1, keepdims=True)
        idx = jnp.min(jnp.where(score == mx, lane_f, float(LANES)), axis=1, keepdims=True)
        pick = lane_f == idx
        sel = jnp.where(pick, 1.0, sel)
        score = jnp.where(pick, PICKED, score)
    return sel


def _tile_iotas(t):
    return (lax.broadcasted_iota(jnp.int32, (t, t), 0), lax.broadcasted_iota(jnp.int32, (t, t), 1))


def _rows(ref, start, t, lo, hi):
    return ref[0, pl.ds(pl.multiple_of(start, t), t), lo:hi]


def _sb_kernel(q_ref, k_ref, v_ref, o_ref):
    t = q_ref.shape[1]
    i = pl.program_id(1)
    row, col = _tile_iotas(t)
    strict = col < row
    later = jnp.where(row > col, 1.0, 0.0).astype(_MXU)
    scale = HEAD_DIM ** -0.5

    for h in range(GROUP_HEADS):
        lo, hi = h * HEAD_DIM, (h + 1) * HEAD_DIM
        q = (q_ref[0, :, lo:hi] * scale).astype(_MXU)

        def tile(j, carry, mask):
            acc, run = carry
            kt = _rows(k_ref, j * t, t, lo, hi).astype(_MXU)
            vt = _rows(v_ref, j * t, t, lo, hi).astype(_MXU)
            z = _dot_nt(q, kt)
            log_keep = -(jnp.maximum(z, 0.0) + jnp.log1p(jnp.exp(-jnp.abs(z))))
            if mask is not None:
                log_keep = jnp.where(mask, log_keep, 0.0)
            piece0 = log_keep.astype(_MXU)
            piece1 = (log_keep - piece0.astype(jnp.float32)).astype(_MXU)
            suffix = _dot(piece0, later) + _dot(piece1, later) + run
            a = jnp.exp(z + log_keep + suffix)
            if mask is not None:
                a = jnp.where(mask, a, 0.0)
            acc = acc + _dot(a.astype(_MXU), vt)
            run = run + jnp.sum(log_keep, axis=-1, keepdims=True)
            return acc, run

        carry = (jnp.zeros((t, HEAD_DIM), jnp.float32), jnp.zeros((t, 1), jnp.float32))
        carry = tile(i, carry, strict)
        carry = lax.fori_loop(0, i, lambda jj, c: tile(i - 1 - jj, c, None), carry)
        o_ref[0, :, lo:hi] = carry[0]


def _group_specs(t, s, cb_q, cb_k, cb_v):
    return [pl.BlockSpec((1, t, GROUP_WIDTH), lambda b, i: (b, i, cb_q)),
            pl.BlockSpec((1, s, GROUP_WIDTH), lambda b, i: (b, 0, cb_k)),
            pl.BlockSpec((1, s, GROUP_WIDTH), lambda b, i: (b, 0, cb_v))]


def _group_out(bsz, s, t):
    return dict(out_specs=pl.BlockSpec((1, t, GROUP_WIDTH), lambda b, i: (b, i, 0)),
                out_shape=jax.ShapeDtypeStruct((bsz, s, GROUP_WIDTH), jnp.float32))


def _stick_breaking(proj, t=TILE):
    bsz, s, _ = proj.shape
    return pl.pallas_call(
        _sb_kernel,
        grid=(bsz, s // t),
        in_specs=_group_specs(t, s, CB_SB_Q, CB_SB_K, CB_SB_V),
        compiler_params=_params("parallel", "arbitrary"),
        name="stick_breaking",
        **_group_out(bsz, s, t),
    )(proj, proj, proj)


def _moba_kernel(tab_ref, q_ref, k_ref, v_ref, bd_ref, bs_ref, o_ref, km_ref, *, head0):
    t = q_ref.shape[1]
    s_len = k_ref.shape[1]
    n_blk = s_len // t
    i = pl.program_id(1)
    row, col = _tile_iotas(t)
    causal = col <= row
    lane = lax.broadcasted_iota(jnp.int32, (t, LANES), 1)
    lane_f = lane.astype(jnp.float32)
    scale = HEAD_DIM ** -0.5

    @pl.when(i == 0)
    def _():
        km_ref[...] = jnp.zeros_like(km_ref)
        for h in range(GROUP_HEADS):
            for n in range(n_blk):
                blk = k_ref[0, n * t:(n + 1) * t, h * HEAD_DIM:(h + 1) * HEAD_DIM]
                km_ref[h, n:n + 1, :] = jnp.mean(blk, axis=0, keepdims=True)

    for h in range(GROUP_HEADS):
        lo, hi = h * HEAD_DIM, (h + 1) * HEAD_DIM
        qf = q_ref[0, :, lo:hi]
        q = (qf * scale).astype(_MXU)
        gate = _dot_nt(qf, km_ref[h], precision=lax.Precision.HIGHEST)
        gate = jnp.where(lane < i, gate, NEG_INF)
        sel = _top_k_mask(gate, lane_f, min(MOBA_TOPK, n_blk - 1))
        sel = jnp.where(lane < i, sel, 0.0)
        far = tab_ref[N_BUCKETS - 1, head0 + h]

        def past(n, state):
            kt = _rows(k_ref, n * t, t, lo, hi).astype(_MXU)
            vt = _rows(v_ref, n * t, t, lo, hi).astype(_MXU)
            sc = _dot_nt(q, kt) + jnp.where(n == i - 1, bs_ref[h], far)
            chosen = jnp.max(jnp.where(lane == n, sel, 0.0), axis=1, keepdims=True) > 0.5
            return _softmax_tile(state, sc, jnp.broadcast_to(chosen, (t, t)), vt)

        state = lax.fori_loop(0, i, past, _softmax_init(t))
        kt = _rows(k_ref, i * t, t, lo, hi).astype(_MXU)
        vt = _rows(v_ref, i * t, t, lo, hi).astype(_MXU)
        state = _softmax_tile(state, _dot_nt(q, kt) + bd_ref[h], causal, vt)
        o_ref[0, :, lo:hi] = _softmax_out(state)


def _bias_specs(t, head_group):
    spec = pl.BlockSpec((GROUP_HEADS, t, t), lambda b, i: (head_group, 0, 0))
    return [spec, spec]


def _moba(proj, table, bias_diag, bias_sub, t=MOBA_BLOCK):
    bsz, s, _ = proj.shape
    return pl.pallas_call(
        functools.partial(_moba_kernel, head0=0),
        grid=(bsz, s // t),
        in_specs=[pl.BlockSpec(memory_space=pltpu.SMEM)]
        + _group_specs(t, s, CB_MB_Q, CB_MB_K, CB_MB_V) + _bias_specs(t, 0),
        scratch_shapes=[pltpu.VMEM((GROUP_HEADS, LANES, HEAD_DIM), jnp.float32)],
        compiler_params=_params("parallel", "arbitrary"),
        name="moba",
        **_group_out(bsz, s, t),
    )(table, proj, proj, proj, bias_diag, bias_sub)


def _diff_kernel(tab_ref, lam_ref, g_ref, q_ref, k_ref, v_ref, bd_ref, bs_ref, o_ref, *, head0, lambda_init):
    t = q_ref.shape[1]
    i = pl.program_id(1)
    row, col = _tile_iotas(t)
    causal = col <= row
    scale = DIFF_HALF ** -0.5
    lv = lam_ref[...]
    lam = (jnp.exp(jnp.sum(lv[0:1] * lv[1:2], keepdims=True))
           - jnp.exp(jnp.sum(lv[2:3] * lv[3:4], keepdims=True)) + lambda_init)

    for h in range(GROUP_HEADS):
        lo, mid, hi = h * HEAD_DIM, h * HEAD_DIM + DIFF_HALF, (h + 1) * HEAD_DIM
        q1 = q_ref[0, :, lo:mid].astype(_MXU)
        q2 = q_ref[0, :, mid:hi].astype(_MXU)
        far = tab_ref[N_BUCKETS - 1, head0 + h]

        def tile(j, states, bias, mask):
            k1 = _rows(k_ref, j * t, t, lo, mid).astype(_MXU)
            k2 = _rows(k_ref, j * t, t, mid, hi).astype(_MXU)
            vt = _rows(v_ref, j * t, t, lo, hi).astype(_MXU)
            s1 = _dot_nt(q1, k1) * scale + bias
            s2 = _dot_nt(q2, k2) * scale + bias
            return (_softmax_tile(states[0], s1, mask, vt), _softmax_tile(states[1], s2, mask, vt))

        def past(j, states):
            return tile(j, states, jnp.where(j == i - 1, bs_ref[h], far), None)

        states = lax.fori_loop(0, i, past, (_softmax_init(t), _softmax_init(t)))
        states = tile(i, states, bd_ref[h], causal)
        o = _softmax_out(states[0]) - lam * _softmax_out(states[1])
        o_ref[0, :, lo:hi] = _rms(o, g_ref[...]) * (1.0 - lambda_init)


def _diff(proj, table, lam_params, subln, bias_diag, bias_sub, lambda_init, t=TILE):
    bsz, s, _ = proj.shape
    return pl.pallas_call(
        functools.partial(_diff_kernel, head0=2 * GROUP_HEADS, lambda_init=lambda_init),
        grid=(bsz, s // t),
        in_specs=[pl.BlockSpec(memory_space=pltpu.SMEM),
                  pl.BlockSpec(lam_params.shape, lambda b, i: (0, 0)),
                  pl.BlockSpec(subln.shape, lambda b, i: (0, 0))]
        + _group_specs(t, s, CB_DF_Q, CB_DF_K, CB_DF_V) + _bias_specs(t, 2),
        compiler_params=_params("parallel", "arbitrary"),
        name="diff_attention",
        **_group_out(bsz, s, t),
    )(table, lam_params, subln, proj, proj, proj, bias_diag, bias_sub)


def _compress_kernel(ck_ref, cv_ref, pk_ref, pv_ref, wk1_ref, wk2_ref, wv1_ref, wv2_ref, kc_ref, vc_ref):
    half = ck_ref.shape[2]

    def compress(c_ref, p_ref, w1_ref, w2_ref):
        chunks = c_ref[0]
        top = _dot((chunks + p_ref[:, :half]).astype(_MXU), w1_ref[:half, :])
        bot = _dot((chunks + p_ref[:, half:]).astype(_MXU), w1_ref[half:, :])
        pre = top + pltpu.roll(bot, bot.shape[0] - 1, axis=0)
        return _dot(jax.nn.gelu(pre).astype(_MXU), w2_ref[...])

    kc_ref[0] = compress(ck_ref, pk_ref, wk1_ref, wk2_ref)
    vc_ref[0] = compress(cv_ref, pv_ref, wv1_ref, wv2_ref)


def _compress(chunks_k, chunks_v, pos_k, pos_v, wk1, wk2, wv1, wv2):
    bsz, n_chunk, width = chunks_k.shape
    cspec = pl.BlockSpec((1, n_chunk, width), lambda b: (b, 0, 0))

    def full(a):
        return pl.BlockSpec(a.shape, lambda b: (0, 0))

    ospec = pl.BlockSpec((1, n_chunk, HEAD_DIM), lambda b: (b, 0, 0))
    oshape = jax.ShapeDtypeStruct((bsz, n_chunk, HEAD_DIM), jnp.float32)
    return pl.pallas_call(
        _compress_kernel,
        grid=(bsz,),
        in_specs=[cspec, cspec, full(pos_k), full(pos_v), full(wk1), full(wk2), full(wv1), full(wv2)],
        out_specs=[ospec, ospec],
        out_shape=[oshape, oshape],
        compiler_params=_params("parallel"),
        name="nsa_compress",
    )(chunks_k, chunks_v, pos_k, pos_v, wk1, wk2, wv1, wv2)


def _nsa_kernel(tab_ref, q_ref, kva_ref, kvb_ref, gq_ref, kc_ref, vc_ref, bd_ref, bs_ref, bc_ref,
                cover_ref, et_ref, o_ref, *, head0):
    t = q_ref.shape[1]
    i = pl.program_id(1)
    row, col = _tile_iotas(t)
    causal = col <= row
    lane = lax.broadcasted_iota(jnp.int32, (t, LANES), 1)
    lane_f = lane.astype(jnp.float32)
    tpos = lax.broadcasted_iota(jnp.int32, (t, LANES), 0) + i * t
    scale = HEAD_DIM ** -0.5
    heads = range(GROUP_HEADS)
    qs = [(q_ref[0, :, h * HEAD_DIM:(h + 1) * HEAD_DIM] * scale).astype(_MXU) for h in heads]
    far = [tab_ref[N_BUCKETS - 1, head0 + h] for h in heads]
    ks_cols, vs_cols = (2 * HEAD_DIM, 3 * HEAD_DIM), (3 * HEAD_DIM, 4 * HEAD_DIM)
    kw_cols, vw_cols = (0, HEAD_DIM), (HEAD_DIM, 2 * HEAD_DIM)

    kc = kc_ref[0].astype(_MXU)
    vc = vc_ref[0].astype(_MXU)
    visible = tpos >= lane * CMP_STRIDE + (CMP_LEN - 1)
    o_cmp = []
    p_sum = jnp.zeros((t, LANES), jnp.float32)
    for h in heads:
        sc = jnp.where(visible, _dot_nt(qs[h], kc) + bc_ref[h], NEG_INF)
        e = jnp.where(visible, jnp.exp(sc - jnp.max(sc, axis=-1, keepdims=True)), 0.0)
        p = e / jnp.maximum(jnp.sum(e, axis=-1, keepdims=True), TINY)
        o_cmp.append(_dot(p.astype(_MXU), vc))
        p_sum = p_sum + p

    importance = jnp.dot(p_sum, cover_ref[...], precision=lax.Precision.HIGHEST,
                         preferred_element_type=jnp.float32)
    own = jnp.right_shift(tpos, int(math.log2(SLC_LEN)))
    score = jnp.where(lane == own, FORCE, jnp.where(lane < own, importance, NEG_INF))
    n_slc = et_ref.shape[0] // SLC_LEN
    score = jnp.where(lane < n_slc, score, PAD_SCORE)
    sel = _top_k_mask(score, lane_f, min(SLC_TOPN, n_slc)).astype(_MXU)

    def expand(j):
        return _dot_nt(sel, _rows_2d(et_ref, j * t, t)) > 0.5

    def slc_tile(j, states, bias_of, mask):
        kt = _rows(kva_ref, j * t, t, *ks_cols).astype(_MXU)
        vt = _rows(kva_ref, j * t, t, *vs_cols).astype(_MXU)
        return tuple(_softmax_tile(states[h], _dot_nt(qs[h], kt) + bias_of(h), mask, vt) for h in heads)

    def slc_past(j, states):
        return slc_tile(j, states, lambda h: jnp.where(j == i - 1, bs_ref[h], far[h]), expand(j))

    states = lax.fori_loop(0, i, slc_past, tuple(_softmax_init(t) for _ in heads))
    states = slc_tile(i, states, lambda h: bd_ref[h], jnp.logical_and(expand(i), causal))
    o_slc = [_softmax_out(st) for st in states]

    def win_tile(j, states, bias_of, mask):
        kt = _rows(kvb_ref, j * t, t, *kw_cols).astype(_MXU)
        vt = _rows(kvb_ref, j * t, t, *vw_cols).astype(_MXU)
        return tuple(_softmax_tile(states[h], _dot_nt(qs[h], kt) + bias_of(h), mask, vt) for h in heads)

    n_back = WINDOW // t
    states = tuple(_softmax_init(t) for _ in heads)
    states = win_tile(jnp.maximum(i - n_back, 0), states, lambda h: far[h],
                      jnp.logical_and(col > row, i >= n_back))
    for back in range(n_back - 1, 0, -1):
        bias_of = (lambda h: bs_ref[h]) if back == 1 else (lambda h: far[h])
        states = win_tile(jnp.maximum(i - back, 0), states, bias_of,
                          jnp.broadcast_to(i >= back, (t, t)))
    states = win_tile(i, states, lambda h: bd_ref[h], causal)
    o_win = [_softmax_out(st) for st in states]

    gates = 1.0 / (1.0 + jnp.exp(-gq_ref[0, :, 2 * HEAD_DIM:2 * HEAD_DIM + N_GATES]))
    for h in heads:
        g = [gates[:, br * GROUP_HEADS + h:br * GROUP_HEADS + h + 1] for br in range(3)]
        o_ref[0, :, h * HEAD_DIM:(h + 1) * HEAD_DIM] = g[0] * o_cmp[h] + g[1] * o_slc[h] + g[2] * o_win[h]


def _rows_2d(ref, start, t):
    return ref[pl.ds(pl.multiple_of(start, t), t), :]


def _nsa(proj, table, kc, vc, bias_diag, bias_sub, bias_cmp, cover, expand_t, t=TILE):
    bsz, s, _ = proj.shape
    n_cmp = kc.shape[1]
    cmp_spec = pl.BlockSpec((1, n_cmp, HEAD_DIM), lambda b, i: (b, 0, 0))
    return pl.pallas_call(
        functools.partial(_nsa_kernel, head0=GROUP_HEADS),
        grid=(bsz, s // t),
        in_specs=[pl.BlockSpec(memory_space=pltpu.SMEM)]
        + _group_specs(t, s, CB_NS_Q, CB_NS_A, CB_NS_B)
        + [pl.BlockSpec((1, t, GROUP_WIDTH), lambda b, i: (b, i, CB_NS_B)), cmp_spec, cmp_spec]
        + _bias_specs(t, 1)
        + [pl.BlockSpec((GROUP_HEADS, t, LANES), lambda b, i: (0, i, 0)),
           pl.BlockSpec(cover.shape, lambda b, i: (0, 0)),
           pl.BlockSpec(expand_t.shape, lambda b, i: (0, 0))],
        compiler_params=_params("parallel", "arbitrary"),
        name="nsa",
        **_group_out(bsz, s, t),
    )(table, proj, proj, proj, proj, kc, vc, bias_diag, bias_sub, bias_cmp, cover, expand_t)


def _nsa_constants(s):
    n_cmp = (s - CMP_LEN) // CMP_STRIDE + 1
    n_slc = s // SLC_LEN
    assert n_cmp <= LANES and n_slc <= LANES and WINDOW % TILE == 0 and s % TILE == 0
    c_start = np.arange(n_cmp) * CMP_STRIDE
    s_start = np.arange(n_slc) * SLC_LEN
    cover = np.clip(np.minimum((c_start + CMP_LEN - 1)[:, None], (s_start + SLC_LEN - 1)[None, :])
                    - np.maximum(c_start[:, None], s_start[None, :]) + 1, 0, None) / CMP_LEN
    cover_pad = np.zeros((LANES, LANES), np.float32)
    cover_pad[:n_cmp, :n_slc] = cover
    expand_t = (np.arange(s)[:, None] // SLC_LEN == np.arange(LANES)[None, :]).astype(np.float32)
    return jnp.asarray(cover_pad), jnp.asarray(expand_t, _MXU)


def kernel(x, w_in, w_out, w_up, w_down, norm_attn, norm_mlp, cmp_pos_k, cmp_pos_v, cmp_k_w1, cmp_k_w2,
           cmp_v_w1, cmp_v_w2, diff_lambda, diff_subln, rel_bias, final_norm):
    bsz, s, d = x.shape
    depth = w_in.shape[0]
    t = TILE
    n_chunk = s // CMP_STRIDE

    w_in_p = jnp.concatenate([w_in[:, :, :COLS_BEFORE_PAD],
                              jnp.zeros((depth, d, PAD_COLS), w_in.dtype),
                              w_in[:, :, COLS_BEFORE_PAD:]], axis=2).astype(_MXU)
    assert w_in_p.shape[2] == D_IN_PAD
    w_out_c, w_up_c, w_down_c = w_out.astype(_MXU), w_up.astype(_MXU), w_down.astype(_MXU)
    wk1, wk2 = cmp_k_w1.astype(_MXU), cmp_k_w2.astype(_MXU)
    wv1, wv2 = cmp_v_w1.astype(_MXU), cmp_v_w2.astype(_MXU)

    tiles = dict(n_heads=rel_bias.shape[1], head0=0, rows=t, cols=t, row_tile=t, row_stride=1, col_stride=-1)
    bias_diag = _bias_table(rel_bias, offset=0, **tiles)
    bias_sub = _bias_table(rel_bias, offset=t, **tiles)
    bias_cmp = _bias_table(rel_bias, n_heads=GROUP_HEADS, head0=GROUP_HEADS, rows=s, cols=LANES,
                           row_tile=t, row_stride=1, col_stride=-CMP_STRIDE, offset=-(CMP_LEN - 1))
    cover, expand_t = _nsa_constants(s)

    x2 = x.reshape(bsz * s, d)
    for layer in range(depth):
        proj = _norm_matmul(x2, norm_attn[layer][None], w_in_p[layer]).reshape(bsz, s, D_IN_PAD)
        o_sb = _stick_breaking(proj)
        o_mb = _moba(proj, rel_bias, bias_diag, bias_sub)
        col0 = CB_NS_A * GROUP_WIDTH
        chunks_k = proj[:, :, col0:col0 + HEAD_DIM].reshape(bsz, n_chunk, CMP_STRIDE * HEAD_DIM)
        chunks_v = proj[:, :, col0 + HEAD_DIM:col0 + 2 * HEAD_DIM].reshape(bsz, n_chunk, CMP_STRIDE * HEAD_DIM)
        kc, vc = _compress(chunks_k, chunks_v,
                           cmp_pos_k[layer].reshape(2, CMP_STRIDE * HEAD_DIM).reshape(1, -1),
                           cmp_pos_v[layer].reshape(1, -1), wk1[layer], wk2[layer], wv1[layer], wv2[layer])
        o_ns = _nsa(proj, rel_bias, kc, vc, bias_diag, bias_sub, bias_cmp, cover, expand_t)
        lambda_init = 0.8 - 0.6 * math.exp(-0.3 * layer)
        o_df = _diff(proj, rel_bias, diff_lambda[layer], diff_subln[layer][None], bias_diag, bias_sub,
                     lambda_init)
        groups = [o.reshape(bsz * s, GROUP_WIDTH) for o in (o_sb, o_mb, o_ns, o_df)]
        x2 = _out_proj(x2, groups, w_out_c[layer])
        x2 = _mlp(x2, norm_mlp[layer][None], w_up_c[layer], w_down_c[layer], final_norm[None],
                  final_norm=(layer == depth - 1))
    return x2.reshape(bsz, s, d)
```

```python
import functools
import math

import numpy as np
import jax
import jax.numpy as jnp
from jax import lax
from jax.experimental import pallas as pl
from jax.experimental.pallas import tpu as pltpu

HEAD_DIM = 64
GROUP_HEADS = 4
GROUP_WIDTH = GROUP_HEADS * HEAD_DIM
NORM_EPS = 1e-6
NEG_INF = -1e30
BIG = 1e30
FORCE = 1e30
TINY = 1e-30
PICKED = -3e38
LOG2E = math.log2(math.e)
N_BUCKETS = 32
MAX_DISTANCE = 128
MOBA_BLOCK = 256
MOBA_TOPK = 3
CMP_LEN = 32
CMP_STRIDE = 16
SLC_LEN = 64
SLC_TOPN = 4
WINDOW = 512
DIFF_HALF = HEAD_DIM // 2
LANES = 128
SUBLANES = 8
TILE = 256
ACC_ROWS = 2 * HEAD_DIM
N_GATES = 3 * GROUP_HEADS
COLS_BEFORE_PAD = 9 * GROUP_WIDTH - 2 * HEAD_DIM + N_GATES
PAD_COLS = 2 * HEAD_DIM - N_GATES
CB_SB_Q, CB_SB_K, CB_SB_V, CB_MB_Q, CB_MB_K, CB_MB_V, CB_NS_Q, CB_NS_A, CB_NS_B, CB_DF_Q, CB_DF_K, CB_DF_V = range(12)
D_IN_PAD = 12 * GROUP_WIDTH

_MXU = jnp.bfloat16
_VMEM_LIMIT = 56 * 1024 * 1024
_HEADS = range(GROUP_HEADS)


def _dot(a, b, precision=None):
    return jnp.dot(a, b, precision=precision, preferred_element_type=jnp.float32)


def _dot_nt(a, b, precision=None):
    return lax.dot_general(a, b, (((1,), (1,)), ((), ())), precision=precision,
                           preferred_element_type=jnp.float32)


def _rms(x, g):
    return x * lax.rsqrt(jnp.mean(x * x, axis=-1, keepdims=True) + NORM_EPS) * g


def _params(*sem):
    return pltpu.CompilerParams(dimension_semantics=sem, vmem_limit_bytes=_VMEM_LIMIT)


def _norm_matmul_kernel(x_ref, g_ref, w_ref, o_ref, h_ref):
    @pl.when(pl.program_id(1) == 0)
    def _():
        h_ref[...] = _rms(x_ref[...], g_ref[...]).astype(h_ref.dtype)

    o_ref[...] = _dot(h_ref[...], w_ref[...]).astype(o_ref.dtype)


def _norm_matmul(x, g, w, *, tm=512, tn=1024):
    m, d = x.shape
    n = w.shape[1]
    return pl.pallas_call(
        _norm_matmul_kernel,
        grid=(m // tm, n // tn),
        in_specs=[pl.BlockSpec((tm, d), lambda i, j: (i, 0)),
                  pl.BlockSpec((1, d), lambda i, j: (0, 0)),
                  pl.BlockSpec((d, tn), lambda i, j: (0, j))],
        out_specs=pl.BlockSpec((tm, tn), lambda i, j: (i, j)),
        out_shape=jax.ShapeDtypeStruct((m, n), jnp.float32),
        scratch_shapes=[pltpu.VMEM((tm, d), _MXU)],
        compiler_params=_params("parallel", "arbitrary"),
        name="norm_in_proj",
    )(x, g, w)


def _out_proj_kernel(x_ref, a_ref, b_ref, c_ref, d_ref, w_ref, o_ref):
    acc = x_ref[...]
    for g, ref in enumerate((a_ref, b_ref, c_ref, d_ref)):
        acc = acc + _dot(ref[...].astype(_MXU), w_ref[g * GROUP_WIDTH:(g + 1) * GROUP_WIDTH, :])
    o_ref[...] = acc


def _out_proj(x, groups, w, *, tm=512):
    m, d = x.shape
    gspec = pl.BlockSpec((tm, GROUP_WIDTH), lambda i: (i, 0))
    return pl.pallas_call(
        _out_proj_kernel,
        grid=(m // tm,),
        in_specs=[pl.BlockSpec((tm, d), lambda i: (i, 0)), gspec, gspec, gspec, gspec,
                  pl.BlockSpec((d, d), lambda i: (0, 0))],
        out_specs=pl.BlockSpec((tm, d), lambda i: (i, 0)),
        out_shape=jax.ShapeDtypeStruct((m, d), jnp.float32),
        compiler_params=_params("parallel"),
        name="out_proj_residual",
    )(x, *groups, w)


def _mlp_kernel(x_ref, g_ref, wu_ref, wd_ref, gf_ref, o_ref, h_ref, acc_ref, *, final_norm):
    c = pl.program_id(1)

    @pl.when(c == 0)
    def _():
        h_ref[...] = _rms(x_ref[...], g_ref[...]).astype(h_ref.dtype)
        acc_ref[...] = jnp.zeros_like(acc_ref)

    u = jnp.square(jnp.maximum(_dot(h_ref[...], wu_ref[...]), 0.0))
    acc_ref[...] += _dot(u.astype(_MXU), wd_ref[...])

    @pl.when(c == pl.num_programs(1) - 1)
    def _():
        y = x_ref[...] + acc_ref[...]
        if final_norm:
            y = _rms(y, gf_ref[...])
        o_ref[...] = y


def _mlp(x, g, w_up, w_down, g_final, *, final_norm, tm=512, tf=1024):
    m, d = x.shape
    f = w_up.shape[1]
    return pl.pallas_call(
        functools.partial(_mlp_kernel, final_norm=final_norm),
        grid=(m // tm, f // tf),
        in_specs=[pl.BlockSpec((tm, d), lambda i, c: (i, 0)),
                  pl.BlockSpec((1, d), lambda i, c: (0, 0)),
                  pl.BlockSpec((d, tf), lambda i, c: (0, c)),
                  pl.BlockSpec((tf, d), lambda i, c: (c, 0)),
                  pl.BlockSpec((1, d), lambda i, c: (0, 0))],
        out_specs=pl.BlockSpec((tm, d), lambda i, c: (i, 0)),
        out_shape=jax.ShapeDtypeStruct((m, d), jnp.float32),
        scratch_shapes=[pltpu.VMEM((tm, d), _MXU), pltpu.VMEM((tm, d), jnp.float32)],
        compiler_params=_params("parallel", "arbitrary"),
        name="norm_mlp_residual",
    )(x, g, w_up, w_down, g_final)


def _t5_bucket(dist):
    n = jnp.maximum(dist, 0)
    max_exact = N_BUCKETS // 2
    nf = jnp.maximum(n, 1).astype(jnp.float32)
    large = max_exact + (jnp.log(nf / max_exact) / math.log(MAX_DISTANCE / max_exact)
                         * (N_BUCKETS - max_exact)).astype(jnp.int32)
    large = jnp.minimum(large, N_BUCKETS - 1)
    return jnp.where(n < max_exact, n, large)


def _bias_kernel(tab_ref, o_ref, *, row_stride, col_stride, offset, head0):
    nh, tr, tc = o_ref.shape
    rows = lax.broadcasted_iota(jnp.int32, (tr, tc), 0)
    cols = lax.broadcasted_iota(jnp.int32, (tr, tc), 1) + pl.program_id(0) * tc
    bucket = _t5_bucket(rows * row_stride + cols * col_stride + offset)
    for h in range(nh):
        out = jnp.zeros((tr, tc), jnp.float32)
        for bkt in range(N_BUCKETS):
            out = jnp.where(bucket == bkt, tab_ref[bkt, head0 + h], out)
        o_ref[h] = (out - tab_ref[N_BUCKETS - 1, head0 + h]) * LOG2E


def _bias_table(table, *, n_heads, head0, rows, cols, col_tile, row_stride, col_stride, offset):
    return pl.pallas_call(
        functools.partial(_bias_kernel, row_stride=row_stride, col_stride=col_stride,
                          offset=offset, head0=head0),
        grid=(cols // col_tile,),
        in_specs=[pl.BlockSpec(memory_space=pltpu.SMEM)],
        out_specs=pl.BlockSpec((n_heads, rows, col_tile), lambda i: (0, 0, i)),
        out_shape=jax.ShapeDtypeStruct((n_heads, rows, cols), jnp.float32),
        compiler_params=_params("parallel"),
        name="t5_bias_tiles",
    )(table)


def _softmax_init(t):
    return (jnp.full((1, t), NEG_INF, jnp.float32), jnp.zeros((ACC_ROWS, t), jnp.float32))


def _softmax_tiles(states, qs, kts, vts, biases=None, emasks=None, qmasks=None):
    n = len(states)
    biases, emasks, qmasks = (x if x is not None else [None] * n for x in (biases, emasks, qmasks))
    scores = []
    for c in range(n):
        s = _dot_nt(kts[c], qs[c])
        if biases[c] is not None:
            s = s + biases[c]
        if emasks[c] is not None:
            s = jnp.where(emasks[c], s, NEG_INF)
        scores.append(s)
    m_news, probs, alphas = [], [], []
    for c in range(n):
        m = states[c][0]
        m_new = jnp.maximum(m, jnp.max(scores[c], axis=0, keepdims=True))
        seen = m_new > 0.5 * NEG_INF
        if qmasks[c] is not None:
            m_new = jnp.where(qmasks[c], m_new, m)
            seen = jnp.logical_and(seen, qmasks[c])
        probs.append(jnp.exp2(scores[c] - jnp.where(seen, m_new, BIG)).astype(_MXU))
        alphas.append(jnp.exp2(m - m_new))
        m_news.append(m_new)
    return tuple((m_news[c], alphas[c] * states[c][1] + _dot(vts[c], probs[c])) for c in range(n))


def _softmax_out(state):
    acc = state[1]
    return acc[:HEAD_DIM] / jnp.maximum(acc[HEAD_DIM:HEAD_DIM + 1], TINY)


def _top_k_rows(score, row_f, k):
    sel = jnp.zeros(score.shape, jnp.float32)
    for _ in range(k):
        mx = jnp.max(score, axis=0, keepdims=True)
        idx = jnp.min(jnp.where(score == mx, row_f, float(score.shape[0])), axis=0, keepdims=True)
        pick = row_f == idx
        sel = jnp.where(pick, 1.0, sel)
        score = jnp.where(pick, PICKED, score)
    return sel


def _tile_iotas(t):
    return (lax.broadcasted_iota(jnp.int32, (t, t), 0), lax.broadcasted_iota(jnp.int32, (t, t), 1))


def _key_rows(ref, j, t):
    return ref[pl.ds(pl.multiple_of(j * t, t), t), :]


def _fill_values_t(vt_ref, v_ref, col0, t):
    n_tiles, rows, _ = vt_ref.shape
    lane_block = (col0 // LANES) * LANES
    for c in range(n_tiles):
        blk = v_ref[0, c * t:(c + 1) * t, lane_block:lane_block + LANES].T
        vt_ref[c, 0:HEAD_DIM, :] = blk[col0 - lane_block:col0 - lane_block + HEAD_DIM].astype(vt_ref.dtype)
        if rows == ACC_ROWS:
            first = lax.broadcasted_iota(jnp.int32, (rows - HEAD_DIM, t), 0) == 0
            vt_ref[c, HEAD_DIM:rows, :] = jnp.where(first, 1.0, 0.0).astype(vt_ref.dtype)


def _group_specs(t, s, cb_q, cb_k, cb_v):
    return [pl.BlockSpec((1, t, GROUP_WIDTH), lambda b, i: (b, i, cb_q)),
            pl.BlockSpec((1, s, GROUP_WIDTH), lambda b, i: (b, 0, cb_k)),
            pl.BlockSpec((1, s, GROUP_WIDTH), lambda b, i: (b, 0, cb_v))]


def _group_out(bsz, s, t):
    return dict(out_specs=pl.BlockSpec((1, t, GROUP_WIDTH), lambda b, i: (b, i, 0)),
                out_shape=jax.ShapeDtypeStruct((bsz, s, GROUP_WIDTH), jnp.float32))


def _bias_specs(t, head_group):
    spec = pl.BlockSpec((GROUP_HEADS, t, t), lambda b, i: (head_group, 0, 0))
    return [spec, spec]


def _store_heads(o_ref, outs_t):
    o_ref[0] = jnp.concatenate(outs_t, axis=0).T


def _sb_kernel(q_ref, k_ref, v_ref, o_ref, kb_ref, vt_ref):
    t = q_ref.shape[1]
    i = pl.program_id(1)
    key, qry = _tile_iotas(t)
    strict = key < qry
    later = jnp.where(qry > key, 1.0, 0.0).astype(_MXU)
    scale = HEAD_DIM ** -0.5

    @pl.when(i == 0)
    def _():
        for h in _HEADS:
            kb_ref[h] = k_ref[0, :, h * HEAD_DIM:(h + 1) * HEAD_DIM].astype(kb_ref.dtype)
            _fill_values_t(vt_ref.at[h], v_ref, h * HEAD_DIM, t)

    qs = [(q_ref[0, :, h * HEAD_DIM:(h + 1) * HEAD_DIM] * scale).astype(_MXU) for h in _HEADS]

    def tile(j, carry, mask):
        zs = [_dot_nt(_key_rows(kb_ref.at[h], j, t), qs[h]) for h in _HEADS]
        log_keeps, suffixes = [], []
        for h in _HEADS:
            z = zs[h]
            log_keep = -(jnp.maximum(z, 0.0) + jnp.log(1.0 + jnp.exp(-jnp.abs(z))))
            if mask is not None:
                log_keep = jnp.where(mask, log_keep, 0.0)
            piece0 = log_keep.astype(_MXU)
            piece1 = (log_keep - piece0.astype(jnp.float32)).astype(_MXU)
            log_keeps.append(log_keep)
            suffixes.append(_dot(later, piece0) + _dot(later, piece1))
        weights = []
        for h in _HEADS:
            a = jnp.exp(zs[h] + log_keeps[h] + suffixes[h] + carry[h][1])
            if mask is not None:
                a = jnp.where(mask, a, 0.0)
            weights.append(a.astype(_MXU))
        return tuple((carry[h][0] + _dot(vt_ref[h, j], weights[h]),
                      carry[h][1] + jnp.sum(log_keeps[h], axis=0, keepdims=True)) for h in _HEADS)

    init = (jnp.zeros((HEAD_DIM, t), jnp.float32), jnp.zeros((1, t), jnp.float32))
    carry = tile(i, (init,) * GROUP_HEADS, strict)
    carry = lax.fori_loop(0, i, lambda jj, c: tile(i - 1 - jj, c, None), carry)
    _store_heads(o_ref, [c[0] for c in carry])


def _stick_breaking(proj, t=TILE):
    bsz, s, _ = proj.shape
    return pl.pallas_call(
        _sb_kernel,
        grid=(bsz, s // t),
        in_specs=_group_specs(t, s, CB_SB_Q, CB_SB_K, CB_SB_V),
        scratch_shapes=[pltpu.VMEM((GROUP_HEADS, s, HEAD_DIM), _MXU),
                        pltpu.VMEM((GROUP_HEADS, s // t, HEAD_DIM, t), _MXU)],
        compiler_params=_params("parallel", "arbitrary"),
        name="stick_breaking",
        **_group_out(bsz, s, t),
    )(proj, proj, proj)


def _moba_kernel(q_ref, k_ref, v_ref, bd_ref, bs_ref, o_ref, kb_ref, vt_ref, km_ref):
    t = q_ref.shape[1]
    n_blk = k_ref.shape[1] // MOBA_BLOCK
    tiles_per_blk = MOBA_BLOCK // t
    blk_shift = int(math.log2(tiles_per_blk))
    i = pl.program_id(1)
    own = jnp.right_shift(i, blk_shift)
    key, qry = _tile_iotas(t)
    causal = key <= qry
    blk_row = lax.broadcasted_iota(jnp.int32, (km_ref.shape[1], t), 0)
    blk_row_f = blk_row.astype(jnp.float32)

    @pl.when(i == 0)
    def _():
        km_ref[...] = jnp.zeros_like(km_ref)
        for h in _HEADS:
            lo, hi = h * HEAD_DIM, (h + 1) * HEAD_DIM
            kb_ref[h] = k_ref[0, :, lo:hi].astype(kb_ref.dtype)
            _fill_values_t(vt_ref.at[h], v_ref, lo, t)
            for n in range(n_blk):
                blk = k_ref[0, n * MOBA_BLOCK:(n + 1) * MOBA_BLOCK, lo:hi]
                km_ref[h, n:n + 1, :] = jnp.mean(blk, axis=0, keepdims=True)

    qs, sels = [], []
    for h in _HEADS:
        qf = q_ref[0, :, h * HEAD_DIM:(h + 1) * HEAD_DIM]
        qs.append((qf * (HEAD_DIM ** -0.5 * LOG2E)).astype(_MXU))
        gate = _dot_nt(km_ref[h], qf, precision=lax.Precision.HIGHEST)
        gate = jnp.where(blk_row < own, gate, NEG_INF)
        sel = _top_k_rows(gate, blk_row_f, min(MOBA_TOPK, n_blk - 1))
        sels.append(jnp.where(blk_row < own, sel, 0.0))

    def tile(j, states, biases=None, emasks=None, live=None):
        qmasks = None
        if live is not None:
            n = jnp.right_shift(j, blk_shift)
            qmasks = [jnp.logical_and(
                jnp.logical_or(jnp.max(jnp.where(blk_row == n, sels[h], 0.0), axis=0, keepdims=True) > 0.5,
                               n == own), live) for h in _HEADS]
        return _softmax_tiles(states, qs, [_key_rows(kb_ref.at[h], j, t) for h in _HEADS],
                              [vt_ref[h, j] for h in _HEADS], biases, emasks, qmasks)

    states = tuple(_softmax_init(t) for _ in _HEADS)
    states = lax.fori_loop(0, jnp.maximum(i - 1, 0), lambda j, st: tile(j, st, live=True), states)
    states = tile(jnp.maximum(i - 1, 0), states, biases=[bs_ref[h] for h in _HEADS], live=i >= 1)
    states = tile(i, states, biases=[bd_ref[h] for h in _HEADS], emasks=[causal] * GROUP_HEADS)
    _store_heads(o_ref, [_softmax_out(st) for st in states])


def _kv_scratch(s, t, key_dim=HEAD_DIM, n_keys=GROUP_HEADS):
    return [pltpu.VMEM((n_keys, s, key_dim), _MXU), pltpu.VMEM((GROUP_HEADS, s // t, ACC_ROWS, t), _MXU)]


def _moba(proj, bias_diag, bias_sub, t=TILE):
    bsz, s, _ = proj.shape
    n_blk_pad = -(-(s // MOBA_BLOCK) // SUBLANES) * SUBLANES
    return pl.pallas_call(
        _moba_kernel,
        grid=(bsz, s // t),
        in_specs=_group_specs(t, s, CB_MB_Q, CB_MB_K, CB_MB_V) + _bias_specs(t, 0),
        scratch_shapes=_kv_scratch(s, t) + [pltpu.VMEM((GROUP_HEADS, n_blk_pad, HEAD_DIM), jnp.float32)],
        compiler_params=_params("parallel", "arbitrary"),
        name="moba",
        **_group_out(bsz, s, t),
    )(proj, proj, proj, bias_diag, bias_sub)


def _diff_kernel(lam_ref, g_ref, q_ref, k_ref, v_ref, bd_ref, bs_ref, o_ref, kb_ref, vt_ref, *, lambda_init):
    t = q_ref.shape[1]
    i = pl.program_id(1)
    key, qry = _tile_iotas(t)
    causal = key <= qry
    lv = lam_ref[...]
    lam = (jnp.exp(jnp.sum(lv[0:1] * lv[1:2], keepdims=True))
           - jnp.exp(jnp.sum(lv[2:3] * lv[3:4], keepdims=True)) + lambda_init)
    halves = range(2 * GROUP_HEADS)

    @pl.when(i == 0)
    def _():
        for c in halves:
            kb_ref[c] = k_ref[0, :, c * DIFF_HALF:(c + 1) * DIFF_HALF].astype(kb_ref.dtype)
        for h in _HEADS:
            _fill_values_t(vt_ref.at[h], v_ref, h * HEAD_DIM, t)

    qs = [(q_ref[0, :, c * DIFF_HALF:(c + 1) * DIFF_HALF] * (DIFF_HALF ** -0.5 * LOG2E)).astype(_MXU)
          for c in halves]

    def tile(j, states, bias_ref=None, emask=None, live=None):
        n = len(halves)
        return _softmax_tiles(states, qs, [_key_rows(kb_ref.at[c], j, t) for c in halves],
                              [vt_ref[c // 2, j] for c in halves],
                              None if bias_ref is None else [bias_ref[c // 2] for c in halves],
                              None if emask is None else [emask] * n,
                              None if live is None else [live] * n)

    states = tuple(_softmax_init(t) for _ in halves)
    states = lax.fori_loop(0, jnp.maximum(i - 1, 0), tile, states)
    states = tile(jnp.maximum(i - 1, 0), states, bs_ref, live=jnp.broadcast_to(i >= 1, (1, t)))
    states = tile(i, states, bd_ref, emask=causal)
    outs = []
    for h in _HEADS:
        o = _softmax_out(states[2 * h]) - lam * _softmax_out(states[2 * h + 1])
        o = o * lax.rsqrt(jnp.mean(o * o, axis=0, keepdims=True) + NORM_EPS) * g_ref[...]
        outs.append(o * (1.0 - lambda_init))
    _store_heads(o_ref, outs)


def _diff(proj, lam_params, subln, bias_diag, bias_sub, lambda_init, t=TILE):
    bsz, s, _ = proj.shape
    return pl.pallas_call(
        functools.partial(_diff_kernel, lambda_init=lambda_init),
        grid=(bsz, s // t),
        in_specs=[pl.BlockSpec(lam_params.shape, lambda b, i: (0, 0)),
                  pl.BlockSpec(subln.shape, lambda b, i: (0, 0))]
        + _group_specs(t, s, CB_DF_Q, CB_DF_K, CB_DF_V) + _bias_specs(t, 2),
        scratch_shapes=_kv_scratch(s, t, key_dim=DIFF_HALF, n_keys=2 * GROUP_HEADS),
        compiler_params=_params("parallel", "arbitrary"),
        name="diff_attention",
        **_group_out(bsz, s, t),
    )(lam_params, subln, proj, proj, proj, bias_diag, bias_sub)


def _compress_kernel(ck_ref, cv_ref, pk_ref, pv_ref, wk1_ref, wk2_ref, wv1_ref, wv2t_ref, kc_ref, vct_ref):
    half = ck_ref.shape[2]

    def hidden(c_ref, p_ref, w1_ref):
        chunks = c_ref[0]
        top = _dot((chunks + p_ref[:, :half]).astype(_MXU), w1_ref[:half, :])
        bot = _dot((chunks + p_ref[:, half:]).astype(_MXU), w1_ref[half:, :])
        pre = top + pltpu.roll(bot, bot.shape[0] - 1, axis=0)
        return jax.nn.gelu(pre).astype(_MXU)

    kc_ref[0] = _dot(hidden(ck_ref, pk_ref, wk1_ref), wk2_ref[...])
    vct_ref[0] = _dot_nt(wv2t_ref[...], hidden(cv_ref, pv_ref, wv1_ref))


def _compress(chunks_k, chunks_v, pos_k, pos_v, wk1, wk2, wv1, wv2t):
    bsz, n_chunk, width = chunks_k.shape
    cspec = pl.BlockSpec((1, n_chunk, width), lambda b: (b, 0, 0))

    def full(a):
        return pl.BlockSpec(a.shape, lambda b: (0, 0))

    return pl.pallas_call(
        _compress_kernel,
        grid=(bsz,),
        in_specs=[cspec, cspec, full(pos_k), full(pos_v), full(wk1), full(wk2), full(wv1), full(wv2t)],
        out_specs=[pl.BlockSpec((1, n_chunk, HEAD_DIM), lambda b: (b, 0, 0)),
                   pl.BlockSpec((1, HEAD_DIM, n_chunk), lambda b: (b, 0, 0))],
        out_shape=[jax.ShapeDtypeStruct((bsz, n_chunk, HEAD_DIM), jnp.float32),
                   jax.ShapeDtypeStruct((bsz, HEAD_DIM, n_chunk), jnp.float32)],
        compiler_params=_params("parallel"),
        name="nsa_compress",
    )(chunks_k, chunks_v, pos_k, pos_v, wk1, wk2, wv1, wv2t)


def _nsa_kernel(q_ref, kva_ref, kvb_ref, gq_ref, kc_ref, vct_ref, bd_ref, bs_ref, bc_ref, cover_ref, e_ref,
                o_ref, ks_ref, vst_ref, kw_ref, vwt_ref):
    t = q_ref.shape[1]
    i = pl.program_id(1)
    key, qry = _tile_iotas(t)
    causal = key <= qry
    ks_col, vs_col, kw_col, vw_col, gate_col = 2 * HEAD_DIM, 3 * HEAD_DIM, 0, HEAD_DIM, 2 * HEAD_DIM

    @pl.when(i == 0)
    def _():
        ks_ref[...] = kva_ref[0, :, ks_col:ks_col + HEAD_DIM].astype(ks_ref.dtype)
        kw_ref[...] = kvb_ref[0, :, kw_col:kw_col + HEAD_DIM].astype(kw_ref.dtype)
        _fill_values_t(vst_ref, kva_ref, vs_col, t)
        _fill_values_t(vwt_ref, kvb_ref, vw_col, t)

    qs = [(q_ref[0, :, h * HEAD_DIM:(h + 1) * HEAD_DIM] * (HEAD_DIM ** -0.5 * LOG2E)).astype(_MXU)
          for h in _HEADS]

    n_cmp = kc_ref.shape[1]
    kc = kc_ref[0].astype(_MXU)
    vct = vct_ref[0].astype(_MXU)
    c_row = lax.broadcasted_iota(jnp.int32, (n_cmp, t), 0)
    c_qpos = lax.broadcasted_iota(jnp.int32, (n_cmp, t), 1) + i * t
    visible = c_qpos >= c_row * CMP_STRIDE + (CMP_LEN - 1)
    o_cmp = []
    p_sum = jnp.zeros((n_cmp, t), jnp.float32)
    for h in _HEADS:
        sc = jnp.where(visible, _dot_nt(kc, qs[h]) + bc_ref[h], NEG_INF)
        e = jnp.where(visible, jnp.exp2(sc - jnp.max(sc, axis=0, keepdims=True)), 0.0)
        p = e / jnp.maximum(jnp.sum(e, axis=0, keepdims=True), TINY)
        o_cmp.append(_dot(vct, p.astype(_MXU)))
        p_sum = p_sum + p

    n_slc = cover_ref.shape[0]
    importance = _dot(cover_ref[...], p_sum, precision=lax.Precision.HIGHEST)
    s_row = lax.broadcasted_iota(jnp.int32, (n_slc, t), 0)
    own = jnp.right_shift(lax.broadcasted_iota(jnp.int32, (n_slc, t), 1) + i * t, int(math.log2(SLC_LEN)))
    score = jnp.where(s_row == own, FORCE, jnp.where(s_row < own, importance, NEG_INF))
    sel = _top_k_rows(score, s_row.astype(jnp.float32), min(SLC_TOPN, n_slc)).astype(_MXU)

    def chosen(j):
        return _dot(_key_rows(e_ref, j, t), sel) > 0.5

    def sweep(k_ref, vt_ref, j, states, bias_ref, emask, live):
        n = GROUP_HEADS
        return _softmax_tiles(states, qs, [_key_rows(k_ref, j, t)] * n, [vt_ref[j]] * n,
                              None if bias_ref is None else [bias_ref[h] for h in _HEADS],
                              None if emask is None else [emask] * n,
                              None if live is None else [live] * n)

    def alive(back):
        return jnp.broadcast_to(i >= back, (1, t))

    prev = jnp.maximum(i - 1, 0)
    states = tuple(_softmax_init(t) for _ in _HEADS)
    states = lax.fori_loop(0, prev, lambda j, st: sweep(ks_ref, vst_ref, j, st, None, chosen(j), None), states)
    states = sweep(ks_ref, vst_ref, prev, states, bs_ref, chosen(prev), alive(1))
    states = sweep(ks_ref, vst_ref, i, states, bd_ref, jnp.logical_and(chosen(i), causal), None)
    o_slc = [_softmax_out(st) for st in states]

    n_back = WINDOW // t
    states = tuple(_softmax_init(t) for _ in _HEADS)
    states = sweep(kw_ref, vwt_ref, jnp.maximum(i - n_back, 0), states, None, qry < key, alive(n_back))
    for back in range(n_back - 1, 0, -1):
        states = sweep(kw_ref, vwt_ref, jnp.maximum(i - back, 0), states, bs_ref if back == 1 else None,
                       None, alive(back))
    states = sweep(kw_ref, vwt_ref, i, states, bd_ref, causal, None)
    o_win = [_softmax_out(st) for st in states]

    gates = gq_ref[0, :, (gate_col // LANES) * LANES:(gate_col // LANES + 1) * LANES].T
    gates = 1.0 / (1.0 + jnp.exp(-gates[gate_col % LANES:gate_col % LANES + N_GATES + 4]))
    outs = []
    for h in _HEADS:
        g = [gates[br * GROUP_HEADS + h:br * GROUP_HEADS + h + 1] for br in range(3)]
        outs.append(g[0] * o_cmp[h] + g[1] * o_slc[h] + g[2] * o_win[h])
    _store_heads(o_ref, outs)


def _nsa(proj, kc, vc_t, bias_diag, bias_sub, bias_cmp, cover_t, expand, t=TILE):
    bsz, s, _ = proj.shape
    n_cmp = kc.shape[1]
    kv_scratch = [pltpu.VMEM((s, HEAD_DIM), _MXU), pltpu.VMEM((s // t, ACC_ROWS, t), _MXU)]
    return pl.pallas_call(
        _nsa_kernel,
        grid=(bsz, s // t),
        in_specs=_group_specs(t, s, CB_NS_Q, CB_NS_A, CB_NS_B)
        + [pl.BlockSpec((1, t, GROUP_WIDTH), lambda b, i: (b, i, CB_NS_B)),
           pl.BlockSpec((1, n_cmp, HEAD_DIM), lambda b, i: (b, 0, 0)),
           pl.BlockSpec((1, HEAD_DIM, n_cmp), lambda b, i: (b, 0, 0))]
        + _bias_specs(t, 1)
        + [pl.BlockSpec((GROUP_HEADS, n_cmp, t), lambda b, i: (0, 0, i)),
           pl.BlockSpec(cover_t.shape, lambda b, i: (0, 0)),
           pl.BlockSpec(expand.shape, lambda b, i: (0, 0))],
        scratch_shapes=kv_scratch + kv_scratch,
        compiler_params=_params("parallel", "arbitrary"),
        name="nsa",
        **_group_out(bsz, s, t),
    )(proj, proj, proj, proj, kc, vc_t, bias_diag, bias_sub, bias_cmp, cover_t, expand)


def _nsa_constants(s):
    n_cmp = (s - CMP_LEN) // CMP_STRIDE + 1
    n_slc = s // SLC_LEN
    assert n_cmp + 1 == s // CMP_STRIDE and n_slc % SUBLANES == 0
    c_start = np.arange(n_cmp) * CMP_STRIDE
    s_start = np.arange(n_slc) * SLC_LEN
    cover = np.clip(np.minimum((c_start + CMP_LEN - 1)[:, None], (s_start + SLC_LEN - 1)[None, :])
                    - np.maximum(c_start[:, None], s_start[None, :]) + 1, 0, None) / CMP_LEN
    cover_t = np.zeros((n_slc, n_cmp + 1), np.float32)
    cover_t[:, :n_cmp] = cover.T
    expand = (np.arange(s)[:, None] // SLC_LEN == np.arange(n_slc)[None, :]).astype(np.float32)
    return jnp.asarray(cover_t), jnp.asarray(expand, _MXU)


def kernel(x, w_in, w_out, w_up, w_down, norm_attn, norm_mlp, cmp_pos_k, cmp_pos_v, cmp_k_w1, cmp_k_w2,
           cmp_v_w1, cmp_v_w2, diff_lambda, diff_subln, rel_bias, final_norm):
    bsz, s, d = x.shape
    depth = w_in.shape[0]
    t = TILE
    n_chunk = s // CMP_STRIDE
    assert s % MOBA_BLOCK == 0 and MOBA_BLOCK % t == 0 and WINDOW % t == 0 and t >= MAX_DISTANCE

    w_in_p = jnp.concatenate([w_in[:, :, :COLS_BEFORE_PAD],
                              jnp.zeros((depth, d, PAD_COLS), w_in.dtype),
                              w_in[:, :, COLS_BEFORE_PAD:]], axis=2).astype(_MXU)
    assert w_in_p.shape[2] == D_IN_PAD
    w_out_c, w_up_c, w_down_c = w_out.astype(_MXU), w_up.astype(_MXU), w_down.astype(_MXU)
    wk1, wk2 = cmp_k_w1.astype(_MXU), cmp_k_w2.astype(_MXU)
    wv1, wv2t = cmp_v_w1.astype(_MXU), jnp.swapaxes(cmp_v_w2, 1, 2).astype(_MXU)

    tiles = dict(n_heads=rel_bias.shape[1], head0=0, rows=t, cols=t, col_tile=t, row_stride=-1, col_stride=1)
    bias_diag = _bias_table(rel_bias, offset=0, **tiles)
    bias_sub = _bias_table(rel_bias, offset=t, **tiles)
    bias_cmp = _bias_table(rel_bias, n_heads=GROUP_HEADS, head0=GROUP_HEADS, rows=n_chunk, cols=s,
                           col_tile=t, row_stride=-CMP_STRIDE, col_stride=1, offset=-(CMP_LEN - 1))
    cover_t, expand = _nsa_constants(s)

    x2 = x.reshape(bsz * s, d)
    for layer in range(depth):
        proj = _norm_matmul(x2, norm_attn[layer][None], w_in_p[layer]).reshape(bsz, s, D_IN_PAD)
        o_sb = _stick_breaking(proj)
        o_mb = _moba(proj, bias_diag, bias_sub)
        col0 = CB_NS_A * GROUP_WIDTH
        chunks_k = proj[:, :, col0:col0 + HEAD_DIM].reshape(bsz, n_chunk, CMP_STRIDE * HEAD_DIM)
        chunks_v = proj[:, :, col0 + HEAD_DIM:col0 + 2 * HEAD_DIM].reshape(bsz, n_chunk, CMP_STRIDE * HEAD_DIM)
        kc, vc_t = _compress(chunks_k, chunks_v, cmp_pos_k[layer].reshape(1, -1), cmp_pos_v[layer].reshape(1, -1),
                             wk1[layer], wk2[layer], wv1[layer], wv2t[layer])
        o_ns = _nsa(proj, kc, vc_t, bias_diag, bias_sub, bias_cmp, cover_t, expand)
        lambda_init = 0.8 - 0.6 * math.exp(-0.3 * layer)
        o_df = _diff(proj, diff_lambda[layer], diff_subln[layer][:, None], bias_diag, bias_sub, lambda_init)
        groups = [o.reshape(bsz * s, GROUP_WIDTH) for o in (o_sb, o_mb, o_ns, o_df)]
        x2 = _out_proj(x2, groups, w_out_c[layer])
        x2 = _mlp(x2, norm_mlp[layer][None], w_up_c[layer], w_down_c[layer], final_norm[None],
                  final_norm=(layer == depth - 1))
    return x2.reshape(bsz, s, d)
```

```python
import functools
import math
from typing import Any, NamedTuple, Optional, Sequence

import numpy as np
import jax
import jax.numpy as jnp
from jax import lax
from jax.experimental import pallas as pl
from jax.experimental.pallas import tpu as pltpu

HEAD_DIM = 64
GROUP_HEADS = 4
GROUP_WIDTH = GROUP_HEADS * HEAD_DIM
NORM_EPS = 1e-6
NEG_INF = -1e30
BIG = 1e30
FORCE = 1e30
TINY = 1e-30
PICKED = -3e38
LOG2E = math.log2(math.e)
N_BUCKETS = 32
MAX_DISTANCE = 128
MOBA_BLOCK = 256
MOBA_TOPK = 3
CMP_LEN = 32
CMP_STRIDE = 16
SLC_LEN = 64
SLC_TOPN = 4
WINDOW = 512
DIFF_HALF = HEAD_DIM // 2
LANES = 128
SUBLANES = 8
TILE = 256
BF16_ROWS = 16
ACC_ROWS = HEAD_DIM + BF16_ROWS
N_GATES = 3 * GROUP_HEADS
COLS_BEFORE_PAD = 9 * GROUP_WIDTH - 2 * HEAD_DIM + N_GATES
PAD_COLS = 2 * HEAD_DIM - N_GATES
CB_SB_Q, CB_SB_K, CB_SB_V, CB_MB_Q, CB_MB_K, CB_MB_V, CB_NS_Q, CB_NS_A, CB_NS_B, CB_DF_Q, CB_DF_K, CB_DF_V = range(12)
D_IN_PAD = 12 * GROUP_WIDTH

_MXU = jnp.bfloat16
_VMEM_LIMIT = 56 * 1024 * 1024
_HEADS = range(GROUP_HEADS)
LOOKAHEAD = 2


def _dot(a, b, precision=None):
    return jnp.dot(a, b, precision=precision, preferred_element_type=jnp.float32)


def _dot_nt(a, b, precision=None):
    return lax.dot_general(a, b, (((1,), (1,)), ((), ())), precision=precision,
                           preferred_element_type=jnp.float32)


def _rms(x, g):
    return x * lax.rsqrt(jnp.mean(x * x, axis=-1, keepdims=True) + NORM_EPS) * g


def _params(*sem):
    return pltpu.CompilerParams(dimension_semantics=sem, vmem_limit_bytes=_VMEM_LIMIT)


def _norm_matmul_kernel(x_ref, g_ref, w_ref, o_ref, h_ref):
    @pl.when(pl.program_id(1) == 0)
    def _():
        h_ref[...] = _rms(x_ref[...], g_ref[...]).astype(h_ref.dtype)

    o_ref[...] = _dot(h_ref[...], w_ref[...]).astype(o_ref.dtype)


def _norm_matmul(x, g, w, *, tm=512, tn=1024):
    m, d = x.shape
    n = w.shape[1]
    return pl.pallas_call(
        _norm_matmul_kernel,
        grid=(m // tm, n // tn),
        in_specs=[pl.BlockSpec((tm, d), lambda i, j: (i, 0)),
                  pl.BlockSpec((1, d), lambda i, j: (0, 0)),
                  pl.BlockSpec((d, tn), lambda i, j: (0, j))],
        out_specs=pl.BlockSpec((tm, tn), lambda i, j: (i, j)),
        out_shape=jax.ShapeDtypeStruct((m, n), jnp.float32),
        scratch_shapes=[pltpu.VMEM((tm, d), _MXU)],
        compiler_params=_params("parallel", "arbitrary"),
        name="norm_in_proj",
    )(x, g, w)


def _out_proj_kernel(x_ref, a_ref, b_ref, c_ref, d_ref, w_ref, o_ref):
    acc = x_ref[...]
    for g, ref in enumerate((a_ref, b_ref, c_ref, d_ref)):
        acc = acc + _dot(ref[...].astype(_MXU), w_ref[g * GROUP_WIDTH:(g + 1) * GROUP_WIDTH, :])
    o_ref[...] = acc


def _out_proj(x, groups, w, *, tm=512):
    m, d = x.shape
    gspec = pl.BlockSpec((tm, GROUP_WIDTH), lambda i: (i, 0))
    return pl.pallas_call(
        _out_proj_kernel,
        grid=(m // tm,),
        in_specs=[pl.BlockSpec((tm, d), lambda i: (i, 0)), gspec, gspec, gspec, gspec,
                  pl.BlockSpec((d, d), lambda i: (0, 0))],
        out_specs=pl.BlockSpec((tm, d), lambda i: (i, 0)),
        out_shape=jax.ShapeDtypeStruct((m, d), jnp.float32),
        compiler_params=_params("parallel"),
        name="out_proj_residual",
    )(x, *groups, w)


def _mlp_kernel(x_ref, g_ref, wu_ref, wd_ref, gf_ref, o_ref, h_ref, acc_ref, *, final_norm):
    c = pl.program_id(1)

    @pl.when(c == 0)
    def _():
        h_ref[...] = _rms(x_ref[...], g_ref[...]).astype(h_ref.dtype)
        acc_ref[...] = jnp.zeros_like(acc_ref)

    u = jnp.square(jnp.maximum(_dot(h_ref[...], wu_ref[...]), 0.0))
    acc_ref[...] += _dot(u.astype(_MXU), wd_ref[...])

    @pl.when(c == pl.num_programs(1) - 1)
    def _():
        y = x_ref[...] + acc_ref[...]
        if final_norm:
            y = _rms(y, gf_ref[...])
        o_ref[...] = y


def _mlp(x, g, w_up, w_down, g_final, *, final_norm, tm=512, tf=1024):
    m, d = x.shape
    f = w_up.shape[1]
    return pl.pallas_call(
        functools.partial(_mlp_kernel, final_norm=final_norm),
        grid=(m // tm, f // tf),
        in_specs=[pl.BlockSpec((tm, d), lambda i, c: (i, 0)),
                  pl.BlockSpec((1, d), lambda i, c: (0, 0)),
                  pl.BlockSpec((d, tf), lambda i, c: (0, c)),
                  pl.BlockSpec((tf, d), lambda i, c: (c, 0)),
                  pl.BlockSpec((1, d), lambda i, c: (0, 0))],
        out_specs=pl.BlockSpec((tm, d), lambda i, c: (i, 0)),
        out_shape=jax.ShapeDtypeStruct((m, d), jnp.float32),
        scratch_shapes=[pltpu.VMEM((tm, d), _MXU), pltpu.VMEM((tm, d), jnp.float32)],
        compiler_params=_params("parallel", "arbitrary"),
        name="norm_mlp_residual",
    )(x, g, w_up, w_down, g_final)


def _t5_bucket(dist):
    n = jnp.maximum(dist, 0)
    max_exact = N_BUCKETS // 2
    nf = jnp.maximum(n, 1).astype(jnp.float32)
    large = max_exact + (jnp.log(nf / max_exact) / math.log(MAX_DISTANCE / max_exact)
                         * (N_BUCKETS - max_exact)).astype(jnp.int32)
    large = jnp.minimum(large, N_BUCKETS - 1)
    return jnp.where(n < max_exact, n, large)


def _bias_kernel(tab_ref, o_ref, *, row_stride, col_stride, offset, head0):
    nh, tr, tc = o_ref.shape
    rows = lax.broadcasted_iota(jnp.int32, (tr, tc), 0)
    cols = lax.broadcasted_iota(jnp.int32, (tr, tc), 1) + pl.program_id(0) * tc
    bucket = _t5_bucket(rows * row_stride + cols * col_stride + offset)
    for h in range(nh):
        out = jnp.zeros((tr, tc), jnp.float32)
        for bkt in range(N_BUCKETS):
            out = jnp.where(bucket == bkt, tab_ref[bkt, head0 + h], out)
        o_ref[h] = (out - tab_ref[N_BUCKETS - 1, head0 + h]) * LOG2E


def _bias_table(table, *, n_heads, head0, rows, cols, col_tile, row_stride, col_stride, offset):
    return pl.pallas_call(
        functools.partial(_bias_kernel, row_stride=row_stride, col_stride=col_stride,
                          offset=offset, head0=head0),
        grid=(cols // col_tile,),
        in_specs=[pl.BlockSpec(memory_space=pltpu.SMEM)],
        out_specs=pl.BlockSpec((n_heads, rows, col_tile), lambda i: (0, 0, i)),
        out_shape=jax.ShapeDtypeStruct((n_heads, rows, cols), jnp.float32),
        compiler_params=_params("parallel"),
        name="t5_bias_tiles",
    )(table)


def _softmax_init(t):
    return (jnp.full((1, t), NEG_INF, jnp.float32), jnp.zeros((ACC_ROWS, t), jnp.float32))


class _KeyTile(NamedTuple):
    kts: Sequence[Any]
    vts: Sequence[Any]
    biases: Optional[Sequence[Any]] = None
    emasks: Optional[Sequence[Any]] = None
    qmasks: Optional[Sequence[Any]] = None


def _softmax_tiles(states, qts, tiles):
    n = len(states)

    def scores_of(c):
        row = []
        for tile in tiles:
            s = _dot(tile.kts[c], qts[c])
            if tile.biases is not None:
                s = s + tile.biases[c]
            if tile.emasks is not None:
                s = jnp.where(tile.emasks[c], s, NEG_INF)
            row.append(s)
        return row

    def update(c, scores):
        m, acc = states[c]
        m_new = m
        for tile, s in zip(tiles, scores):
            tile_max = jnp.max(s, axis=0, keepdims=True)
            if tile.qmasks is not None:
                tile_max = jnp.where(tile.qmasks[c], tile_max, NEG_INF)
            m_new = jnp.maximum(m_new, tile_max)
        seen = m_new > 0.5 * NEG_INF
        acc = jnp.exp2(m - m_new) * acc
        for tile, s in zip(tiles, scores):
            ok = seen if tile.qmasks is None else jnp.logical_and(seen, tile.qmasks[c])
            acc = acc + _dot(tile.vts[c], jnp.exp2(s - jnp.where(ok, m_new, BIG)).astype(_MXU))
        return m_new, acc

    pending = {c: scores_of(c) for c in range(min(LOOKAHEAD, n))}
    out = []
    for c in range(n):
        if c + LOOKAHEAD < n:
            pending[c + LOOKAHEAD] = scores_of(c + LOOKAHEAD)
        out.append(update(c, pending.pop(c)))
    return tuple(out)


def _pair_loop(n_tiles, states, body):
    last = jnp.maximum(n_tiles - 1, 0)

    def step(p, st):
        j1 = 2 * p + 1
        return body(2 * p, jnp.minimum(j1, last), j1 < n_tiles, st)

    return lax.fori_loop(0, jnp.right_shift(n_tiles + 1, 1), step, states)


def _softmax_out(state):
    acc = state[1]
    return acc[:HEAD_DIM] / jnp.maximum(acc[HEAD_DIM:HEAD_DIM + 1], TINY)


def _top_k_rows(score, row_f, k):
    sel = jnp.zeros(score.shape, jnp.float32)
    for _ in range(k):
        mx = jnp.max(score, axis=0, keepdims=True)
        idx = jnp.min(jnp.where(score == mx, row_f, float(score.shape[0])), axis=0, keepdims=True)
        pick = row_f == idx
        sel = jnp.where(pick, 1.0, sel)
        score = jnp.where(pick, PICKED, score)
    return sel


def _tile_iotas(t):
    return (lax.broadcasted_iota(jnp.int32, (t, t), 0), lax.broadcasted_iota(jnp.int32, (t, t), 1))


def _key_rows(ref, j, t):
    return ref[pl.ds(pl.multiple_of(j * t, t), t), :]


def _queries_t(q_ref, width, scale):
    qt = q_ref[0].T
    return [(qt[c * width:(c + 1) * width] * scale).astype(_MXU) for c in range(GROUP_WIDTH // width)]


def _fill_values_t(vt_ref, v_ref, col0, t):
    n_tiles, rows, _ = vt_ref.shape
    lane_block = (col0 // LANES) * LANES
    for c in range(n_tiles):
        blk = v_ref[0, c * t:(c + 1) * t, lane_block:lane_block + LANES].T
        vt_ref[c, 0:HEAD_DIM, :] = blk[col0 - lane_block:col0 - lane_block + HEAD_DIM].astype(vt_ref.dtype)
        if rows == ACC_ROWS:
            first = lax.broadcasted_iota(jnp.int32, (rows - HEAD_DIM, t), 0) == 0
            vt_ref[c, HEAD_DIM:rows, :] = jnp.where(first, 1.0, 0.0).astype(vt_ref.dtype)


def _group_specs(t, s, cb_q, cb_k, cb_v):
    return [pl.BlockSpec((1, t, GROUP_WIDTH), lambda b, i: (b, i, cb_q)),
            pl.BlockSpec((1, s, GROUP_WIDTH), lambda b, i: (b, 0, cb_k)),
            pl.BlockSpec((1, s, GROUP_WIDTH), lambda b, i: (b, 0, cb_v))]


def _group_out(bsz, s, t):
    return dict(out_specs=pl.BlockSpec((1, t, GROUP_WIDTH), lambda b, i: (b, i, 0)),
                out_shape=jax.ShapeDtypeStruct((bsz, s, GROUP_WIDTH), jnp.float32))


def _bias_specs(t, head_group):
    spec = pl.BlockSpec((GROUP_HEADS, t, t), lambda b, i: (head_group, 0, 0))
    return [spec, spec]


def _store_heads(o_ref, outs_t):
    o_ref[0] = jnp.concatenate(outs_t, axis=0).T


def _sb_kernel(q_ref, k_ref, v_ref, o_ref, kb_ref, vt_ref):
    t = q_ref.shape[1]
    i = pl.program_id(1)
    key, qry = _tile_iotas(t)
    strict = key < qry
    later = jnp.where(qry > key, 1.0, 0.0).astype(_MXU)
    later2 = jnp.concatenate([later, later], axis=1)
    scale = HEAD_DIM ** -0.5 * LOG2E

    @pl.when(i == 0)
    def _():
        for h in _HEADS:
            kb_ref[h] = k_ref[0, :, h * HEAD_DIM:(h + 1) * HEAD_DIM].astype(kb_ref.dtype)
            _fill_values_t(vt_ref.at[h], v_ref, h * HEAD_DIM, t)

    qts = _queries_t(q_ref, HEAD_DIM, scale)

    def sweep(tiles, carry):
        zs = [[_dot(_key_rows(kb_ref.at[h], j, t), qts[h]) for h in _HEADS] for j, _, _ in tiles]
        log_keeps, suffixes = [], []
        for g, (_, mask, _) in enumerate(tiles):
            lk_row, sf_row = [], []
            for h in _HEADS:
                z = zs[g][h]
                neg = -z
                log_keep = jnp.minimum(neg, 0.0) - jnp.log2(1.0 + jnp.exp2(jnp.minimum(z, neg)))
                if mask is not None:
                    log_keep = jnp.where(mask, log_keep, 0.0)
                piece0 = log_keep.astype(_MXU)
                piece1 = (log_keep - piece0.astype(jnp.float32)).astype(_MXU)
                lk_row.append(log_keep)
                sf_row.append(_dot(later2, jnp.concatenate([piece0, piece1], axis=0)))
            log_keeps.append(lk_row)
            suffixes.append(sf_row)
        out = []
        for h in _HEADS:
            acc, run = carry[h]
            weights = []
            for g, (_, mask, live) in enumerate(tiles):
                a = jnp.exp2(zs[g][h] + log_keeps[g][h] + suffixes[g][h] + run)
                keep = mask if live is None else (live if mask is None else jnp.logical_and(mask, live))
                if keep is not None:
                    a = jnp.where(keep, a, 0.0)
                weights.append(a.astype(_MXU))
                run = run + jnp.sum(log_keeps[g][h], axis=0, keepdims=True)
            for g, (j, _, _) in enumerate(tiles):
                acc = acc + _dot(vt_ref[h, j], weights[g])
            out.append((acc, run))
        return tuple(out)

    init = (jnp.zeros((HEAD_DIM, t), jnp.float32), jnp.zeros((1, t), jnp.float32))
    carry = sweep([(i, strict, None), (jnp.maximum(i - 1, 0), None, i >= 1)], (init,) * GROUP_HEADS)
    n_rest = jnp.maximum(i - 1, 0)
    carry = _pair_loop(n_rest, carry, lambda r0, r1, live1, c: sweep(
        [(n_rest - 1 - r0, None, None), (n_rest - 1 - r1, None, live1)], c))
    _store_heads(o_ref, [c[0] for c in carry])


def _stick_breaking(proj, t=TILE):
    bsz, s, _ = proj.shape
    return pl.pallas_call(
        _sb_kernel,
        grid=(bsz, s // t),
        in_specs=_group_specs(t, s, CB_SB_Q, CB_SB_K, CB_SB_V),
        scratch_shapes=[pltpu.VMEM((GROUP_HEADS, s, HEAD_DIM), _MXU),
                        pltpu.VMEM((GROUP_HEADS, s // t, HEAD_DIM, t), _MXU)],
        compiler_params=_params("parallel", "arbitrary"),
        name="stick_breaking",
        **_group_out(bsz, s, t),
    )(proj, proj, proj)


def _moba_kernel(q_ref, k_ref, v_ref, bd_ref, bs_ref, o_ref, kb_ref, vt_ref, km_ref):
    t = q_ref.shape[1]
    n_blk = k_ref.shape[1] // MOBA_BLOCK
    tiles_per_blk = MOBA_BLOCK // t
    blk_shift = int(math.log2(tiles_per_blk))
    i = pl.program_id(1)
    own = jnp.right_shift(i, blk_shift)
    key, qry = _tile_iotas(t)
    causal = key <= qry
    blk_row = lax.broadcasted_iota(jnp.int32, (km_ref.shape[1], t), 0)
    blk_row_f = blk_row.astype(jnp.float32)

    @pl.when(i == 0)
    def _():
        km_ref[...] = jnp.zeros_like(km_ref)
        for h in _HEADS:
            lo, hi = h * HEAD_DIM, (h + 1) * HEAD_DIM
            kb_ref[h] = k_ref[0, :, lo:hi].astype(kb_ref.dtype)
            _fill_values_t(vt_ref.at[h], v_ref, lo, t)
            for n in range(n_blk):
                blk = k_ref[0, n * MOBA_BLOCK:(n + 1) * MOBA_BLOCK, lo:hi]
                km_ref[h, n:n + 1, :] = jnp.mean(blk, axis=0, keepdims=True)

    qt = q_ref[0].T
    qts, sels = [], []
    for h in _HEADS:
        qf = qt[h * HEAD_DIM:(h + 1) * HEAD_DIM]
        qts.append((qf * (HEAD_DIM ** -0.5 * LOG2E)).astype(_MXU))
        gate = _dot(km_ref[h], qf, precision=lax.Precision.HIGHEST)
        gate = jnp.where(blk_row < own, gate, NEG_INF)
        sel = _top_k_rows(gate, blk_row_f, min(MOBA_TOPK, n_blk - 1))
        sels.append(jnp.where(blk_row < own, sel, 0.0))

    def tile(j, live, biases=None, emasks=None):
        qmasks = None
        if live is not None:
            n = jnp.right_shift(j, blk_shift)
            qmasks = [jnp.logical_and(
                jnp.logical_or(jnp.max(jnp.where(blk_row == n, sels[h], 0.0), axis=0, keepdims=True) > 0.5,
                               n == own), live) for h in _HEADS]
        return _KeyTile([_key_rows(kb_ref.at[h], j, t) for h in _HEADS], [vt_ref[h, j] for h in _HEADS],
                        biases, emasks, qmasks)

    states = tuple(_softmax_init(t) for _ in _HEADS)
    states = _pair_loop(jnp.maximum(i - 1, 0), states, lambda j0, j1, live1, st: _softmax_tiles(
        st, qts, [tile(j0, True), tile(j1, live1)]))
    states = _softmax_tiles(states, qts, [
        tile(jnp.maximum(i - 1, 0), i >= 1, biases=[bs_ref[h] for h in _HEADS]),
        tile(i, None, biases=[bd_ref[h] for h in _HEADS], emasks=[causal] * GROUP_HEADS)])
    _store_heads(o_ref, [_softmax_out(st) for st in states])


def _kv_scratch(s, t, key_dim=HEAD_DIM, n_keys=GROUP_HEADS):
    return [pltpu.VMEM((n_keys, s, key_dim), _MXU), pltpu.VMEM((GROUP_HEADS, s // t, ACC_ROWS, t), _MXU)]


def _moba(proj, bias_diag, bias_sub, t=TILE):
    bsz, s, _ = proj.shape
    n_blk_pad = -(-(s // MOBA_BLOCK) // SUBLANES) * SUBLANES
    return pl.pallas_call(
        _moba_kernel,
        grid=(bsz, s // t),
        in_specs=_group_specs(t, s, CB_MB_Q, CB_MB_K, CB_MB_V) + _bias_specs(t, 0),
        scratch_shapes=_kv_scratch(s, t) + [pltpu.VMEM((GROUP_HEADS, n_blk_pad, HEAD_DIM), jnp.float32)],
        compiler_params=_params("parallel", "arbitrary"),
        name="moba",
        **_group_out(bsz, s, t),
    )(proj, proj, proj, bias_diag, bias_sub)


def _diff_kernel(lam_ref, g_ref, q_ref, k_ref, v_ref, bd_ref, bs_ref, o_ref, kb_ref, vt_ref, *, lambda_init):
    t = q_ref.shape[1]
    i = pl.program_id(1)
    key, qry = _tile_iotas(t)
    causal = key <= qry
    lv = lam_ref[...]
    lam = (jnp.exp(jnp.sum(lv[0:1] * lv[1:2], keepdims=True))
           - jnp.exp(jnp.sum(lv[2:3] * lv[3:4], keepdims=True)) + lambda_init)
    halves = range(2 * GROUP_HEADS)

    @pl.when(i == 0)
    def _():
        for c in halves:
            kb_ref[c] = k_ref[0, :, c * DIFF_HALF:(c + 1) * DIFF_HALF].astype(kb_ref.dtype)
        for h in _HEADS:
            _fill_values_t(vt_ref.at[h], v_ref, h * HEAD_DIM, t)

    qts = _queries_t(q_ref, DIFF_HALF, DIFF_HALF ** -0.5 * LOG2E)

    def tile(j, live=None, bias_ref=None, emask=None):
        n = len(halves)
        return _KeyTile([_key_rows(kb_ref.at[c], j, t) for c in halves], [vt_ref[c // 2, j] for c in halves],
                        None if bias_ref is None else [bias_ref[c // 2] for c in halves],
                        None if emask is None else [emask] * n,
                        None if live is None else [jnp.broadcast_to(live, (1, t))] * n)

    states = tuple(_softmax_init(t) for _ in halves)
    states = _pair_loop(jnp.maximum(i - 1, 0), states, lambda j0, j1, live1, st: _softmax_tiles(
        st, qts, [tile(j0), tile(j1, live1)]))
    states = _softmax_tiles(states, qts, [tile(jnp.maximum(i - 1, 0), i >= 1, bs_ref),
                                          tile(i, None, bd_ref, causal)])
    outs = []
    for h in _HEADS:
        o = _softmax_out(states[2 * h]) - lam * _softmax_out(states[2 * h + 1])
        o = o * lax.rsqrt(jnp.mean(o * o, axis=0, keepdims=True) + NORM_EPS) * g_ref[...]
        outs.append(o * (1.0 - lambda_init))
    _store_heads(o_ref, outs)


def _diff(proj, lam_params, subln, bias_diag, bias_sub, lambda_init, t=TILE):
    bsz, s, _ = proj.shape
    return pl.pallas_call(
        functools.partial(_diff_kernel, lambda_init=lambda_init),
        grid=(bsz, s // t),
        in_specs=[pl.BlockSpec(lam_params.shape, lambda b, i: (0, 0)),
                  pl.BlockSpec(subln.shape, lambda b, i: (0, 0))]
        + _group_specs(t, s, CB_DF_Q, CB_DF_K, CB_DF_V) + _bias_specs(t, 2),
        scratch_shapes=_kv_scratch(s, t, key_dim=DIFF_HALF, n_keys=2 * GROUP_HEADS),
        compiler_params=_params("parallel", "arbitrary"),
        name="diff_attention",
        **_group_out(bsz, s, t),
    )(lam_params, subln, proj, proj, proj, bias_diag, bias_sub)


def _compress_kernel(ck_ref, cv_ref, pk_ref, pv_ref, wk1_ref, wk2_ref, wv1_ref, wv2t_ref, kc_ref, vct_ref):
    half = ck_ref.shape[2]

    def hidden(c_ref, p_ref, w1_ref):
        chunks = c_ref[0]
        top = _dot((chunks + p_ref[:, :half]).astype(_MXU), w1_ref[:half, :])
        bot = _dot((chunks + p_ref[:, half:]).astype(_MXU), w1_ref[half:, :])
        pre = top + pltpu.roll(bot, bot.shape[0] - 1, axis=0)
        return jax.nn.gelu(pre).astype(_MXU)

    kc_ref[0] = _dot(hidden(ck_ref, pk_ref, wk1_ref), wk2_ref[...])
    vct_ref[0] = _dot_nt(wv2t_ref[...], hidden(cv_ref, pv_ref, wv1_ref))


def _compress(chunks_k, chunks_v, pos_k, pos_v, wk1, wk2, wv1, wv2t):
    bsz, n_chunk, width = chunks_k.shape
    cspec = pl.BlockSpec((1, n_chunk, width), lambda b: (b, 0, 0))

    def full(a):
        return pl.BlockSpec(a.shape, lambda b: (0, 0))

    return pl.pallas_call(
        _compress_kernel,
        grid=(bsz,),
        in_specs=[cspec, cspec, full(pos_k), full(pos_v), full(wk1), full(wk2), full(wv1), full(wv2t)],
        out_specs=[pl.BlockSpec((1, n_chunk, HEAD_DIM), lambda b: (b, 0, 0)),
                   pl.BlockSpec((1, HEAD_DIM, n_chunk), lambda b: (b, 0, 0))],
        out_shape=[jax.ShapeDtypeStruct((bsz, n_chunk, HEAD_DIM), jnp.float32),
                   jax.ShapeDtypeStruct((bsz, HEAD_DIM, n_chunk), jnp.float32)],
        compiler_params=_params("parallel"),
        name="nsa_compress",
    )(chunks_k, chunks_v, pos_k, pos_v, wk1, wk2, wv1, wv2t)


def _nsa_kernel(q_ref, kva_ref, kvb_ref, gq_ref, kc_ref, vct_ref, bd_ref, bs_ref, bc_ref, cover_ref, e_ref,
                o_ref, ks_ref, vst_ref, kw_ref, vwt_ref):
    t = q_ref.shape[1]
    i = pl.program_id(1)
    key, qry = _tile_iotas(t)
    causal = key <= qry
    ks_col, vs_col, kw_col, vw_col, gate_col = 2 * HEAD_DIM, 3 * HEAD_DIM, 0, HEAD_DIM, 2 * HEAD_DIM

    @pl.when(i == 0)
    def _():
        ks_ref[...] = kva_ref[0, :, ks_col:ks_col + HEAD_DIM].astype(ks_ref.dtype)
        kw_ref[...] = kvb_ref[0, :, kw_col:kw_col + HEAD_DIM].astype(kw_ref.dtype)
        _fill_values_t(vst_ref, kva_ref, vs_col, t)
        _fill_values_t(vwt_ref, kvb_ref, vw_col, t)

    qts = _queries_t(q_ref, HEAD_DIM, HEAD_DIM ** -0.5 * LOG2E)

    n_cmp = kc_ref.shape[1]
    kc = kc_ref[0].astype(_MXU)
    vct = vct_ref[0].astype(_MXU)
    c_row = lax.broadcasted_iota(jnp.int32, (n_cmp, t), 0)
    c_qpos = lax.broadcasted_iota(jnp.int32, (n_cmp, t), 1) + i * t
    visible = c_qpos >= c_row * CMP_STRIDE + (CMP_LEN - 1)
    cmp_scores = [_dot(kc, qts[h]) for h in _HEADS]
    cmp_probs = []
    p_sum = jnp.zeros((n_cmp, t), jnp.float32)
    for h in _HEADS:
        sc = jnp.where(visible, cmp_scores[h] + bc_ref[h], NEG_INF)
        e = jnp.where(visible, jnp.exp2(sc - jnp.max(sc, axis=0, keepdims=True)), 0.0)
        p = e / jnp.maximum(jnp.sum(e, axis=0, keepdims=True), TINY)
        cmp_probs.append(p.astype(_MXU))
        p_sum = p_sum + p
    o_cmp = [_dot(vct, cmp_probs[h]) for h in _HEADS]

    n_slc = cover_ref.shape[0]
    importance = _dot(cover_ref[...], p_sum, precision=lax.Precision.HIGHEST)
    s_row = lax.broadcasted_iota(jnp.int32, (n_slc, t), 0)
    own = jnp.right_shift(lax.broadcasted_iota(jnp.int32, (n_slc, t), 1) + i * t, int(math.log2(SLC_LEN)))
    score = jnp.where(s_row == own, FORCE, jnp.where(s_row < own, importance, NEG_INF))
    sel = _top_k_rows(score, s_row.astype(jnp.float32), min(SLC_TOPN, n_slc)).astype(_MXU)

    def chosen(j):
        return _dot(_key_rows(e_ref, j, t), sel) > 0.5

    def tile(k_ref, vt_ref, j, live=None, bias_ref=None, emask=None):
        n = GROUP_HEADS
        return _KeyTile([_key_rows(k_ref, j, t)] * n, [vt_ref[j]] * n,
                        None if bias_ref is None else [bias_ref[h] for h in _HEADS],
                        None if emask is None else [emask] * n,
                        None if live is None else [jnp.broadcast_to(live, (1, t))] * n)

    prev = jnp.maximum(i - 1, 0)
    states = tuple(_softmax_init(t) for _ in _HEADS)
    states = _pair_loop(prev, states, lambda j0, j1, live1, st: _softmax_tiles(
        st, qts, [tile(ks_ref, vst_ref, j0, emask=chosen(j0)), tile(ks_ref, vst_ref, j1, live1, emask=chosen(j1))]))
    states = _softmax_tiles(states, qts, [
        tile(ks_ref, vst_ref, prev, i >= 1, bs_ref, chosen(prev)),
        tile(ks_ref, vst_ref, i, None, bd_ref, jnp.logical_and(chosen(i), causal))])
    o_slc = [_softmax_out(st) for st in states]

    n_back = WINDOW // t
    window = [tile(kw_ref, vwt_ref, jnp.maximum(i - n_back, 0), i >= n_back, emask=qry < key)]
    for back in range(n_back - 1, 0, -1):
        window.append(tile(kw_ref, vwt_ref, jnp.maximum(i - back, 0), i >= back, bs_ref if back == 1 else None))
    window.append(tile(kw_ref, vwt_ref, i, None, bd_ref, causal))
    states = _softmax_tiles(tuple(_softmax_init(t) for _ in _HEADS), qts, window)
    o_win = [_softmax_out(st) for st in states]

    gates = gq_ref[0, :, (gate_col // LANES) * LANES:(gate_col // LANES + 1) * LANES].T
    gates = 1.0 / (1.0 + jnp.exp(-gates[gate_col % LANES:gate_col % LANES + N_GATES + 4]))
    outs = []
    for h in _HEADS:
        g = [gates[br * GROUP_HEADS + h:br * GROUP_HEADS + h + 1] for br in range(3)]
        outs.append(g[0] * o_cmp[h] + g[1] * o_slc[h] + g[2] * o_win[h])
    _store_heads(o_ref, outs)


def _nsa(proj, kc, vc_t, bias_diag, bias_sub, bias_cmp, cover_t, expand, t=TILE):
    bsz, s, _ = proj.shape
    n_cmp = kc.shape[1]
    kv_scratch = [pltpu.VMEM((s, HEAD_DIM), _MXU), pltpu.VMEM((s // t, ACC_ROWS, t), _MXU)]
    return pl.pallas_call(
        _nsa_kernel,
        grid=(bsz, s // t),
        in_specs=_group_specs(t, s, CB_NS_Q, CB_NS_A, CB_NS_B)
        + [pl.BlockSpec((1, t, GROUP_WIDTH), lambda b, i: (b, i, CB_NS_B)),
           pl.BlockSpec((1, n_cmp, HEAD_DIM), lambda b, i: (b, 0, 0)),
           pl.BlockSpec((1, HEAD_DIM, n_cmp), lambda b, i: (b, 0, 0))]
        + _bias_specs(t, 1)
        + [pl.BlockSpec((GROUP_HEADS, n_cmp, t), lambda b, i: (0, 0, i)),
           pl.BlockSpec(cover_t.shape, lambda b, i: (0, 0)),
           pl.BlockSpec(expand.shape, lambda b, i: (0, 0))],
        scratch_shapes=kv_scratch + kv_scratch,
        compiler_params=_params("parallel", "arbitrary"),
        name="nsa",
        **_group_out(bsz, s, t),
    )(proj, proj, proj, proj, kc, vc_t, bias_diag, bias_sub, bias_cmp, cover_t, expand)


def _nsa_constants(s):
    n_cmp = (s - CMP_LEN) // CMP_STRIDE + 1
    n_slc = s // SLC_LEN
    assert n_cmp + 1 == s // CMP_STRIDE and n_slc % SUBLANES == 0
    c_start = np.arange(n_cmp) * CMP_STRIDE
    s_start = np.arange(n_slc) * SLC_LEN
    cover = np.clip(np.minimum((c_start + CMP_LEN - 1)[:, None], (s_start + SLC_LEN - 1)[None, :])
                    - np.maximum(c_start[:, None], s_start[None, :]) + 1, 0, None) / CMP_LEN
    cover_t = np.zeros((n_slc, n_cmp + 1), np.float32)
    cover_t[:, :n_cmp] = cover.T
    expand = (np.arange(s)[:, None] // SLC_LEN == np.arange(n_slc)[None, :]).astype(np.float32)
    return jnp.asarray(cover_t), jnp.asarray(expand, _MXU)


def kernel(x, w_in, w_out, w_up, w_down, norm_attn, norm_mlp, cmp_pos_k, cmp_pos_v, cmp_k_w1, cmp_k_w2,
           cmp_v_w1, cmp_v_w2, diff_lambda, diff_subln, rel_bias, final_norm):
    bsz, s, d = x.shape
    depth = w_in.shape[0]
    t = TILE
    n_chunk = s // CMP_STRIDE
    assert s % MOBA_BLOCK == 0 and MOBA_BLOCK % t == 0 and WINDOW % t == 0 and t >= MAX_DISTANCE

    w_in_p = jnp.concatenate([w_in[:, :, :COLS_BEFORE_PAD],
                              jnp.zeros((depth, d, PAD_COLS), w_in.dtype),
                              w_in[:, :, COLS_BEFORE_PAD:]], axis=2).astype(_MXU)
    assert w_in_p.shape[2] == D_IN_PAD
    w_out_c, w_up_c, w_down_c = w_out.astype(_MXU), w_up.astype(_MXU), w_down.astype(_MXU)
    wk1, wk2 = cmp_k_w1.astype(_MXU), cmp_k_w2.astype(_MXU)
    wv1, wv2t = cmp_v_w1.astype(_MXU), jnp.swapaxes(cmp_v_w2, 1, 2).astype(_MXU)

    tiles = dict(n_heads=rel_bias.shape[1], head0=0, rows=t, cols=t, col_tile=t, row_stride=-1, col_stride=1)
    bias_diag = _bias_table(rel_bias, offset=0, **tiles)
    bias_sub = _bias_table(rel_bias, offset=t, **tiles)
    bias_cmp = _bias_table(rel_bias, n_heads=GROUP_HEADS, head0=GROUP_HEADS, rows=n_chunk, cols=s,
                           col_tile=t, row_stride=-CMP_STRIDE, col_stride=1, offset=-(CMP_LEN - 1))
    cover_t, expand = _nsa_constants(s)

    x2 = x.reshape(bsz * s, d)
    for layer in range(depth):
        proj = _norm_matmul(x2, norm_attn[layer][None], w_in_p[layer]).reshape(bsz, s, D_IN_PAD)
        o_sb = _stick_breaking(proj)
        o_mb = _moba(proj, bias_diag, bias_sub)
        col0 = CB_NS_A * GROUP_WIDTH
        chunks_k = proj[:, :, col0:col0 + HEAD_DIM].reshape(bsz, n_chunk, CMP_STRIDE * HEAD_DIM)
        chunks_v = proj[:, :, col0 + HEAD_DIM:col0 + 2 * HEAD_DIM].reshape(bsz, n_chunk, CMP_STRIDE * HEAD_DIM)
        kc, vc_t = _compress(chunks_k, chunks_v, cmp_pos_k[layer].reshape(1, -1), cmp_pos_v[layer].reshape(1, -1),
                             wk1[layer], wk2[layer], wv1[layer], wv2t[layer])
        o_ns = _nsa(proj, kc, vc_t, bias_diag, bias_sub, bias_cmp, cover_t, expand)
        lambda_init = 0.8 - 0.6 * math.exp(-0.3 * layer)
        o_df = _diff(proj, diff_lambda[layer], diff_subln[layer][:, None], bias_diag, bias_sub, lambda_init)
        groups = [o.reshape(bsz * s, GROUP_WIDTH) for o in (o_sb, o_mb, o_ns, o_df)]
        x2 = _out_proj(x2, groups, w_out_c[layer])
        x2 = _mlp(x2, norm_mlp[layer][None], w_up_c[layer], w_down_c[layer], final_norm[None],
                  final_norm=(layer == depth - 1))
    return x2.reshape(bsz, s, d)
```

```python
import functools
import math
from typing import Any, NamedTuple, Optional, Sequence

import numpy as np
import jax
import jax.numpy as jnp
from jax import lax
from jax.experimental import pallas as pl
from jax.experimental.pallas import tpu as pltpu

HEAD_DIM = 64
GROUP_HEADS = 4
GROUP_WIDTH = GROUP_HEADS * HEAD_DIM
NORM_EPS = 1e-6
NEG_INF = -1e30
BIG = 1e30
FORCE = 1e30
TINY = 1e-30
PICKED = -3e38
LOG2E = math.log2(math.e)
N_BUCKETS = 32
MAX_DISTANCE = 128
MOBA_BLOCK = 256
MOBA_TOPK = 3
CMP_LEN = 32
CMP_STRIDE = 16
SLC_LEN = 64
SLC_TOPN = 4
WINDOW = 512
DIFF_HALF = HEAD_DIM // 2
LANES = 128
SUBLANES = 8
TILE = 256
BF16_ROWS = 16
ACC_ROWS = HEAD_DIM + BF16_ROWS
N_GATES = 3 * GROUP_HEADS
COLS_BEFORE_PAD = 9 * GROUP_WIDTH - 2 * HEAD_DIM + N_GATES
PAD_COLS = 2 * HEAD_DIM - N_GATES
CB_SB_Q, CB_SB_K, CB_SB_V, CB_MB_Q, CB_MB_K, CB_MB_V, CB_NS_Q, CB_NS_A, CB_NS_B, CB_DF_Q, CB_DF_K, CB_DF_V = range(12)
D_IN_PAD = 12 * GROUP_WIDTH

_MXU = jnp.bfloat16
_VMEM_LIMIT = 56 * 1024 * 1024
_HEADS = range(GROUP_HEADS)
LOOKAHEAD = 2


def _dot(a, b, precision=None):
    return jnp.dot(a, b, precision=precision, preferred_element_type=jnp.float32)


def _dot_nt(a, b, precision=None):
    return lax.dot_general(a, b, (((1,), (1,)), ((), ())), precision=precision,
                           preferred_element_type=jnp.float32)


def _rms(x, g):
    return x * lax.rsqrt(jnp.mean(x * x, axis=-1, keepdims=True) + NORM_EPS) * g


def _params(*sem):
    return pltpu.CompilerParams(dimension_semantics=sem, vmem_limit_bytes=_VMEM_LIMIT)


def _norm_matmul_kernel(x_ref, g_ref, w_ref, scale_ref, o_ref, *, tn):
    h = _rms(x_ref[...], g_ref[...]).astype(_MXU)
    for j in range(w_ref.shape[1] // tn):
        cols = slice(j * tn, (j + 1) * tn)
        o_ref[:, cols] = (_dot(h, w_ref[:, cols]) * scale_ref[:, cols]).astype(o_ref.dtype)


def _norm_matmul(x, g, w, col_scale, *, tm=512, tn=1024):
    m, d = x.shape
    n = w.shape[1]
    return pl.pallas_call(
        functools.partial(_norm_matmul_kernel, tn=tn),
        grid=(m // tm,),
        in_specs=[pl.BlockSpec((tm, d), lambda i: (i, 0)),
                  pl.BlockSpec((1, d), lambda i: (0, 0)),
                  pl.BlockSpec((d, n), lambda i: (0, 0)),
                  pl.BlockSpec((1, n), lambda i: (0, 0))],
        out_specs=pl.BlockSpec((tm, n), lambda i: (i, 0)),
        out_shape=jax.ShapeDtypeStruct((m, n), _MXU),
        compiler_params=_params("parallel"),
        name="norm_in_proj",
    )(x, g, w, col_scale)


def _out_mlp_kernel(x_ref, a_ref, b_ref, c_ref, d_ref, wo_ref, g_ref, wu_ref, wd_ref, gf_ref, o_ref,
                    h_ref, acc_ref, *, final_norm):
    c = pl.program_id(1)

    @pl.when(c == 0)
    def _():
        y = x_ref[...]
        for g, ref in enumerate((a_ref, b_ref, c_ref, d_ref)):
            y = y + _dot(ref[...], wo_ref[g * GROUP_WIDTH:(g + 1) * GROUP_WIDTH, :])
        acc_ref[...] = y
        h_ref[...] = _rms(y, g_ref[...]).astype(h_ref.dtype)

    u = jnp.square(jnp.maximum(_dot(h_ref[...], wu_ref[...]), 0.0))
    acc_ref[...] += _dot(u.astype(_MXU), wd_ref[...])

    @pl.when(c == pl.num_programs(1) - 1)
    def _():
        y = acc_ref[...]
        if final_norm:
            y = _rms(y, gf_ref[...])
        o_ref[...] = y


def _out_mlp(x, groups, w_out, g, w_up, w_down, g_final, *, final_norm, tm=512, tf=1024):
    m, d = x.shape
    f = w_up.shape[1]
    gspec = pl.BlockSpec((tm, GROUP_WIDTH), lambda i, c: (i, 0))
    row = pl.BlockSpec((1, d), lambda i, c: (0, 0))
    return pl.pallas_call(
        functools.partial(_out_mlp_kernel, final_norm=final_norm),
        grid=(m // tm, f // tf),
        in_specs=[pl.BlockSpec((tm, d), lambda i, c: (i, 0)), gspec, gspec, gspec, gspec,
                  pl.BlockSpec((d, d), lambda i, c: (0, 0)), row,
                  pl.BlockSpec((d, tf), lambda i, c: (0, c)),
                  pl.BlockSpec((tf, d), lambda i, c: (c, 0)), row],
        out_specs=pl.BlockSpec((tm, d), lambda i, c: (i, 0)),
        out_shape=jax.ShapeDtypeStruct((m, d), jnp.float32),
        scratch_shapes=[pltpu.VMEM((tm, d), _MXU), pltpu.VMEM((tm, d), jnp.float32)],
        compiler_params=_params("parallel", "arbitrary"),
        name="out_proj_mlp_residual",
    )(x, *groups, w_out, g, w_up, w_down, g_final)


def _t5_bucket(dist):
    n = jnp.maximum(dist, 0)
    max_exact = N_BUCKETS // 2
    nf = jnp.maximum(n, 1).astype(jnp.float32)
    large = max_exact + (jnp.log(nf / max_exact) / math.log(MAX_DISTANCE / max_exact)
                         * (N_BUCKETS - max_exact)).astype(jnp.int32)
    large = jnp.minimum(large, N_BUCKETS - 1)
    return jnp.where(n < max_exact, n, large)


def _bias_kernel(tab_ref, o_ref, *, row_stride, col_stride, offset, head0):
    nh, tr, tc = o_ref.shape
    rows = lax.broadcasted_iota(jnp.int32, (tr, tc), 0)
    cols = lax.broadcasted_iota(jnp.int32, (tr, tc), 1) + pl.program_id(0) * tc
    bucket = _t5_bucket(rows * row_stride + cols * col_stride + offset)
    for h in range(nh):
        out = jnp.zeros((tr, tc), jnp.float32)
        for bkt in range(N_BUCKETS):
            out = jnp.where(bucket == bkt, tab_ref[bkt, head0 + h], out)
        o_ref[h] = (out - tab_ref[N_BUCKETS - 1, head0 + h]) * LOG2E


def _bias_table(table, *, n_heads, head0, rows, cols, col_tile, row_stride, col_stride, offset):
    return pl.pallas_call(
        functools.partial(_bias_kernel, row_stride=row_stride, col_stride=col_stride,
                          offset=offset, head0=head0),
        grid=(cols // col_tile,),
        in_specs=[pl.BlockSpec(memory_space=pltpu.SMEM)],
        out_specs=pl.BlockSpec((n_heads, rows, col_tile), lambda i: (0, 0, i)),
        out_shape=jax.ShapeDtypeStruct((n_heads, rows, cols), jnp.float32),
        compiler_params=_params("parallel"),
        name="t5_bias_tiles",
    )(table)


def _softmax_init(t):
    return (jnp.full((1, t), NEG_INF, jnp.float32), jnp.zeros((ACC_ROWS, t), jnp.float32))


class _KeyTile(NamedTuple):
    kts: Sequence[Any]
    vts: Sequence[Any]
    biases: Optional[Sequence[Any]] = None
    emasks: Optional[Sequence[Any]] = None
    qmasks: Optional[Sequence[Any]] = None


def _softmax_tiles(states, qts, tiles):
    n = len(states)

    def scores_of(c):
        row = []
        for tile in tiles:
            s = _dot(tile.kts[c], qts[c])
            if tile.biases is not None:
                s = s + tile.biases[c]
            if tile.emasks is not None:
                s = jnp.where(tile.emasks[c], s, NEG_INF)
            row.append(s)
        return row

    def update(c, scores):
        m, acc = states[c]
        m_new = m
        for tile, s in zip(tiles, scores):
            tile_max = jnp.max(s, axis=0, keepdims=True)
            if tile.qmasks is not None:
                tile_max = jnp.where(tile.qmasks[c], tile_max, NEG_INF)
            m_new = jnp.maximum(m_new, tile_max)
        seen = m_new > 0.5 * NEG_INF
        acc = jnp.exp2(m - m_new) * acc
        for tile, s in zip(tiles, scores):
            ok = seen if tile.qmasks is None else jnp.logical_and(seen, tile.qmasks[c])
            acc = acc + _dot(tile.vts[c], jnp.exp2(s - jnp.where(ok, m_new, BIG)).astype(_MXU))
        return m_new, acc

    pending = {c: scores_of(c) for c in range(min(LOOKAHEAD, n))}
    out = []
    for c in range(n):
        if c + LOOKAHEAD < n:
            pending[c + LOOKAHEAD] = scores_of(c + LOOKAHEAD)
        out.append(update(c, pending.pop(c)))
    return tuple(out)


def _pair_loop(n_tiles, states, body):
    last = jnp.maximum(n_tiles - 1, 0)

    def step(p, st):
        j1 = 2 * p + 1
        return body(2 * p, jnp.minimum(j1, last), j1 < n_tiles, st)

    return lax.fori_loop(0, jnp.right_shift(n_tiles + 1, 1), step, states)


def _softmax_out(state):
    acc = state[1]
    return acc[:HEAD_DIM] / jnp.maximum(acc[HEAD_DIM:HEAD_DIM + 1], TINY)


def _top_k_rows(score, row_f, k):
    sel = jnp.zeros(score.shape, jnp.float32)
    for _ in range(k):
        mx = jnp.max(score, axis=0, keepdims=True)
        idx = jnp.min(jnp.where(score == mx, row_f, float(score.shape[0])), axis=0, keepdims=True)
        pick = row_f == idx
        sel = jnp.where(pick, 1.0, sel)
        score = jnp.where(pick, PICKED, score)
    return sel


def _tile_iotas(t):
    return (lax.broadcasted_iota(jnp.int32, (t, t), 0), lax.broadcasted_iota(jnp.int32, (t, t), 1))


def _key_rows(ref, j, t):
    return ref[pl.ds(pl.multiple_of(j * t, t), t), :]


def _transposed(ref_block):
    return ref_block.astype(jnp.float32).T


def _queries_t(q_ref, width):
    qt = _transposed(q_ref[0])
    return [qt[c * width:(c + 1) * width].astype(_MXU) for c in range(GROUP_WIDTH // width)]


def _fill_values_t(vt_ref, v_ref, col0, t):
    n_tiles, rows, _ = vt_ref.shape
    lane_block = (col0 // LANES) * LANES
    for c in range(n_tiles):
        blk = _transposed(v_ref[0, c * t:(c + 1) * t, lane_block:lane_block + LANES])
        vt_ref[c, 0:HEAD_DIM, :] = blk[col0 - lane_block:col0 - lane_block + HEAD_DIM].astype(vt_ref.dtype)
        if rows == ACC_ROWS:
            first = lax.broadcasted_iota(jnp.int32, (rows - HEAD_DIM, t), 0) == 0
            vt_ref[c, HEAD_DIM:rows, :] = jnp.where(first, 1.0, 0.0).astype(vt_ref.dtype)


def _group_specs(t, s, cb_q, cb_k, cb_v):
    return [pl.BlockSpec((1, t, GROUP_WIDTH), lambda b, i: (b, i, cb_q)),
            pl.BlockSpec((1, s, GROUP_WIDTH), lambda b, i: (b, 0, cb_k)),
            pl.BlockSpec((1, s, GROUP_WIDTH), lambda b, i: (b, 0, cb_v))]


def _group_out(bsz, s, t):
    return dict(out_specs=pl.BlockSpec((1, t, GROUP_WIDTH), lambda b, i: (b, i, 0)),
                out_shape=jax.ShapeDtypeStruct((bsz, s, GROUP_WIDTH), _MXU))


def _bias_specs(t, head_group):
    spec = pl.BlockSpec((GROUP_HEADS, t, t), lambda b, i: (head_group, 0, 0))
    return [spec, spec]


def _store_heads(o_ref, outs_t):
    o_ref[0] = jnp.concatenate(outs_t, axis=0).T.astype(o_ref.dtype)


def _sb_kernel(q_ref, k_ref, v_ref, o_ref, kb_ref, vt_ref):
    t = q_ref.shape[1]
    i = pl.program_id(1)
    key, qry = _tile_iotas(t)
    strict = key < qry
    later = jnp.where(qry > key, 1.0, 0.0).astype(_MXU)
    later2 = jnp.concatenate([later, later], axis=1)

    @pl.when(i == 0)
    def _():
        for h in _HEADS:
            kb_ref[h] = k_ref[0, :, h * HEAD_DIM:(h + 1) * HEAD_DIM].astype(kb_ref.dtype)
            _fill_values_t(vt_ref.at[h], v_ref, h * HEAD_DIM, t)

    qts = _queries_t(q_ref, HEAD_DIM)

    def sweep(tiles, carry):
        zs = [[_dot(_key_rows(kb_ref.at[h], j, t), qts[h]) for h in _HEADS] for j, _, _ in tiles]
        log_keeps, suffixes = [], []
        for g, (_, mask, _) in enumerate(tiles):
            lk_row, sf_row = [], []
            for h in _HEADS:
                z = zs[g][h]
                neg = -z
                log_keep = jnp.minimum(neg, 0.0) - jnp.log2(1.0 + jnp.exp2(jnp.minimum(z, neg)))
                if mask is not None:
                    log_keep = jnp.where(mask, log_keep, 0.0)
                piece0 = log_keep.astype(_MXU)
                piece1 = (log_keep - piece0.astype(jnp.float32)).astype(_MXU)
                lk_row.append(log_keep)
                sf_row.append(_dot(later2, jnp.concatenate([piece0, piece1], axis=0)))
            log_keeps.append(lk_row)
            suffixes.append(sf_row)
        out = []
        for h in _HEADS:
            acc, run = carry[h]
            weights = []
            for g, (_, mask, live) in enumerate(tiles):
                a = jnp.exp2(zs[g][h] + log_keeps[g][h] + suffixes[g][h] + run)
                keep = mask if live is None else (live if mask is None else jnp.logical_and(mask, live))
                if keep is not None:
                    a = jnp.where(keep, a, 0.0)
                weights.append(a.astype(_MXU))
                run = run + jnp.sum(log_keeps[g][h], axis=0, keepdims=True)
            for g, (j, _, _) in enumerate(tiles):
                acc = acc + _dot(vt_ref[h, j], weights[g])
            out.append((acc, run))
        return tuple(out)

    init = (jnp.zeros((HEAD_DIM, t), jnp.float32), jnp.zeros((1, t), jnp.float32))
    carry = sweep([(i, strict, None), (jnp.maximum(i - 1, 0), None, i >= 1)], (init,) * GROUP_HEADS)
    n_rest = jnp.maximum(i - 1, 0)
    carry = _pair_loop(n_rest, carry, lambda r0, r1, live1, c: sweep(
        [(n_rest - 1 - r0, None, None), (n_rest - 1 - r1, None, live1)], c))
    _store_heads(o_ref, [c[0] for c in carry])


def _stick_breaking(proj, t=TILE):
    bsz, s, _ = proj.shape
    return pl.pallas_call(
        _sb_kernel,
        grid=(bsz, s // t),
        in_specs=_group_specs(t, s, CB_SB_Q, CB_SB_K, CB_SB_V),
        scratch_shapes=[pltpu.VMEM((GROUP_HEADS, s, HEAD_DIM), _MXU),
                        pltpu.VMEM((GROUP_HEADS, s // t, HEAD_DIM, t), _MXU)],
        compiler_params=_params("parallel", "arbitrary"),
        name="stick_breaking",
        **_group_out(bsz, s, t),
    )(proj, proj, proj)


def _moba_kernel(q_ref, k_ref, v_ref, bd_ref, bs_ref, o_ref, kb_ref, vt_ref, km_ref):
    t = q_ref.shape[1]
    n_blk = k_ref.shape[1] // MOBA_BLOCK
    tiles_per_blk = MOBA_BLOCK // t
    blk_shift = int(math.log2(tiles_per_blk))
    i = pl.program_id(1)
    own = jnp.right_shift(i, blk_shift)
    key, qry = _tile_iotas(t)
    causal = key <= qry
    blk_row = lax.broadcasted_iota(jnp.int32, (km_ref.shape[1], t), 0)
    blk_row_f = blk_row.astype(jnp.float32)

    @pl.when(i == 0)
    def _():
        km_ref[...] = jnp.zeros_like(km_ref)
        for h in _HEADS:
            lo, hi = h * HEAD_DIM, (h + 1) * HEAD_DIM
            kb_ref[h] = k_ref[0, :, lo:hi].astype(kb_ref.dtype)
            _fill_values_t(vt_ref.at[h], v_ref, lo, t)
            for n in range(n_blk):
                blk = k_ref[0, n * MOBA_BLOCK:(n + 1) * MOBA_BLOCK, lo:hi]
                km_ref[h, n:n + 1, :] = jnp.mean(blk.astype(jnp.float32), axis=0, keepdims=True)

    qt = _transposed(q_ref[0])
    qts, sels = [], []
    for h in _HEADS:
        qf = qt[h * HEAD_DIM:(h + 1) * HEAD_DIM]
        qts.append(qf.astype(_MXU))
        gate = _dot(km_ref[h], qf, precision=lax.Precision.HIGHEST)
        gate = jnp.where(blk_row < own, gate, NEG_INF)
        sel = _top_k_rows(gate, blk_row_f, min(MOBA_TOPK, n_blk - 1))
        sels.append(jnp.where(blk_row < own, sel, 0.0))

    def tile(j, live, biases=None, emasks=None):
        qmasks = None
        if live is not None:
            n = jnp.right_shift(j, blk_shift)
            qmasks = [jnp.logical_and(
                jnp.logical_or(jnp.max(jnp.where(blk_row == n, sels[h], 0.0), axis=0, keepdims=True) > 0.5,
                               n == own), live) for h in _HEADS]
        return _KeyTile([_key_rows(kb_ref.at[h], j, t) for h in _HEADS], [vt_ref[h, j] for h in _HEADS],
                        biases, emasks, qmasks)

    states = tuple(_softmax_init(t) for _ in _HEADS)
    states = _pair_loop(jnp.maximum(i - 1, 0), states, lambda j0, j1, live1, st: _softmax_tiles(
        st, qts, [tile(j0, True), tile(j1, live1)]))
    states = _softmax_tiles(states, qts, [
        tile(jnp.maximum(i - 1, 0), i >= 1, biases=[bs_ref[h] for h in _HEADS]),
        tile(i, None, biases=[bd_ref[h] for h in _HEADS], emasks=[causal] * GROUP_HEADS)])
    _store_heads(o_ref, [_softmax_out(st) for st in states])


def _kv_scratch(s, t, key_dim=HEAD_DIM, n_keys=GROUP_HEADS):
    return [pltpu.VMEM((n_keys, s, key_dim), _MXU), pltpu.VMEM((GROUP_HEADS, s // t, ACC_ROWS, t), _MXU)]


def _moba(proj, bias_diag, bias_sub, t=TILE):
    bsz, s, _ = proj.shape
    n_blk_pad = -(-(s // MOBA_BLOCK) // SUBLANES) * SUBLANES
    return pl.pallas_call(
        _moba_kernel,
        grid=(bsz, s // t),
        in_specs=_group_specs(t, s, CB_MB_Q, CB_MB_K, CB_MB_V) + _bias_specs(t, 0),
        scratch_shapes=_kv_scratch(s, t) + [pltpu.VMEM((GROUP_HEADS, n_blk_pad, HEAD_DIM), jnp.float32)],
        compiler_params=_params("parallel", "arbitrary"),
        name="moba",
        **_group_out(bsz, s, t),
    )(proj, proj, proj, bias_diag, bias_sub)


def _diff_kernel(lam_ref, g_ref, q_ref, k_ref, v_ref, bd_ref, bs_ref, o_ref, kb_ref, vt_ref, *, lambda_init):
    t = q_ref.shape[1]
    i = pl.program_id(1)
    key, qry = _tile_iotas(t)
    causal = key <= qry
    lv = lam_ref[...]
    lam = (jnp.exp(jnp.sum(lv[0:1] * lv[1:2], keepdims=True))
           - jnp.exp(jnp.sum(lv[2:3] * lv[3:4], keepdims=True)) + lambda_init)
    halves = range(2 * GROUP_HEADS)

    @pl.when(i == 0)
    def _():
        for c in halves:
            kb_ref[c] = k_ref[0, :, c * DIFF_HALF:(c + 1) * DIFF_HALF].astype(kb_ref.dtype)
        for h in _HEADS:
            _fill_values_t(vt_ref.at[h], v_ref, h * HEAD_DIM, t)

    qts = _queries_t(q_ref, DIFF_HALF)

    def tile(j, live=None, bias_ref=None, emask=None):
        n = len(halves)
        return _KeyTile([_key_rows(kb_ref.at[c], j, t) for c in halves], [vt_ref[c // 2, j] for c in halves],
                        None if bias_ref is None else [bias_ref[c // 2] for c in halves],
                        None if emask is None else [emask] * n,
                        None if live is None else [jnp.broadcast_to(live, (1, t))] * n)

    states = tuple(_softmax_init(t) for _ in halves)
    states = _pair_loop(jnp.maximum(i - 1, 0), states, lambda j0, j1, live1, st: _softmax_tiles(
        st, qts, [tile(j0), tile(j1, live1)]))
    states = _softmax_tiles(states, qts, [tile(jnp.maximum(i - 1, 0), i >= 1, bs_ref),
                                          tile(i, None, bd_ref, causal)])
    outs = []
    for h in _HEADS:
        o = _softmax_out(states[2 * h]) - lam * _softmax_out(states[2 * h + 1])
        o = o * lax.rsqrt(jnp.mean(o * o, axis=0, keepdims=True) + NORM_EPS) * g_ref[...]
        outs.append(o * (1.0 - lambda_init))
    _store_heads(o_ref, outs)


def _diff(proj, lam_params, subln, bias_diag, bias_sub, lambda_init, t=TILE):
    bsz, s, _ = proj.shape
    return pl.pallas_call(
        functools.partial(_diff_kernel, lambda_init=lambda_init),
        grid=(bsz, s // t),
        in_specs=[pl.BlockSpec(lam_params.shape, lambda b, i: (0, 0)),
                  pl.BlockSpec(subln.shape, lambda b, i: (0, 0))]
        + _group_specs(t, s, CB_DF_Q, CB_DF_K, CB_DF_V) + _bias_specs(t, 2),
        scratch_shapes=_kv_scratch(s, t, key_dim=DIFF_HALF, n_keys=2 * GROUP_HEADS),
        compiler_params=_params("parallel", "arbitrary"),
        name="diff_attention",
        **_group_out(bsz, s, t),
    )(lam_params, subln, proj, proj, proj, bias_diag, bias_sub)


def _compress_kernel(ck_ref, cv_ref, pk_ref, pv_ref, wk1_ref, wk2_ref, wv1_ref, wv2t_ref, kc_ref, vct_ref):
    half = ck_ref.shape[2]

    def hidden(c_ref, p_ref, w1_ref):
        chunks = c_ref[0]
        top = _dot((chunks + p_ref[:, :half]).astype(_MXU), w1_ref[:half, :])
        bot = _dot((chunks + p_ref[:, half:]).astype(_MXU), w1_ref[half:, :])
        pre = top + pltpu.roll(bot, bot.shape[0] - 1, axis=0)
        return jax.nn.gelu(pre).astype(_MXU)

    kc_ref[0] = _dot(hidden(ck_ref, pk_ref, wk1_ref), wk2_ref[...])
    vct_ref[0] = _dot_nt(wv2t_ref[...], hidden(cv_ref, pv_ref, wv1_ref))


def _compress(chunks_k, chunks_v, pos_k, pos_v, wk1, wk2, wv1, wv2t):
    bsz, n_chunk, width = chunks_k.shape
    cspec = pl.BlockSpec((1, n_chunk, width), lambda b: (b, 0, 0))

    def full(a):
        return pl.BlockSpec(a.shape, lambda b: (0, 0))

    return pl.pallas_call(
        _compress_kernel,
        grid=(bsz,),
        in_specs=[cspec, cspec, full(pos_k), full(pos_v), full(wk1), full(wk2), full(wv1), full(wv2t)],
        out_specs=[pl.BlockSpec((1, n_chunk, HEAD_DIM), lambda b: (b, 0, 0)),
                   pl.BlockSpec((1, HEAD_DIM, n_chunk), lambda b: (b, 0, 0))],
        out_shape=[jax.ShapeDtypeStruct((bsz, n_chunk, HEAD_DIM), jnp.float32),
                   jax.ShapeDtypeStruct((bsz, HEAD_DIM, n_chunk), jnp.float32)],
        compiler_params=_params("parallel"),
        name="nsa_compress",
    )(chunks_k, chunks_v, pos_k, pos_v, wk1, wk2, wv1, wv2t)


def _nsa_kernel(q_ref, kva_ref, kvb_ref, gq_ref, kc_ref, vct_ref, bd_ref, bs_ref, bc_ref, cover_ref, e_ref,
                o_ref, ks_ref, vst_ref, kw_ref, vwt_ref):
    t = q_ref.shape[1]
    i = pl.program_id(1)
    key, qry = _tile_iotas(t)
    causal = key <= qry
    ks_col, vs_col, kw_col, vw_col, gate_col = 2 * HEAD_DIM, 3 * HEAD_DIM, 0, HEAD_DIM, 2 * HEAD_DIM

    @pl.when(i == 0)
    def _():
        ks_ref[...] = kva_ref[0, :, ks_col:ks_col + HEAD_DIM].astype(ks_ref.dtype)
        kw_ref[...] = kvb_ref[0, :, kw_col:kw_col + HEAD_DIM].astype(kw_ref.dtype)
        _fill_values_t(vst_ref, kva_ref, vs_col, t)
        _fill_values_t(vwt_ref, kvb_ref, vw_col, t)

    qts = _queries_t(q_ref, HEAD_DIM)

    n_cmp = kc_ref.shape[1]
    kc = kc_ref[0].astype(_MXU)
    vct = vct_ref[0].astype(_MXU)
    c_row = lax.broadcasted_iota(jnp.int32, (n_cmp, t), 0)
    c_qpos = lax.broadcasted_iota(jnp.int32, (n_cmp, t), 1) + i * t
    visible = c_qpos >= c_row * CMP_STRIDE + (CMP_LEN - 1)
    cmp_scores = [_dot(kc, qts[h]) for h in _HEADS]
    cmp_probs = []
    p_sum = jnp.zeros((n_cmp, t), jnp.float32)
    for h in _HEADS:
        sc = jnp.where(visible, cmp_scores[h] + bc_ref[h], NEG_INF)
        e = jnp.where(visible, jnp.exp2(sc - jnp.max(sc, axis=0, keepdims=True)), 0.0)
        p = e / jnp.maximum(jnp.sum(e, axis=0, keepdims=True), TINY)
        cmp_probs.append(p.astype(_MXU))
        p_sum = p_sum + p
    o_cmp = [_dot(vct, cmp_probs[h]) for h in _HEADS]

    n_slc = cover_ref.shape[0]
    importance = _dot(cover_ref[...], p_sum, precision=lax.Precision.HIGHEST)
    s_row = lax.broadcasted_iota(jnp.int32, (n_slc, t), 0)
    own = jnp.right_shift(lax.broadcasted_iota(jnp.int32, (n_slc, t), 1) + i * t, int(math.log2(SLC_LEN)))
    score = jnp.where(s_row == own, FORCE, jnp.where(s_row < own, importance, NEG_INF))
    sel = _top_k_rows(score, s_row.astype(jnp.float32), min(SLC_TOPN, n_slc)).astype(_MXU)

    def chosen(j):
        return _dot(_key_rows(e_ref, j, t), sel) > 0.5

    def tile(k_ref, vt_ref, j, live=None, bias_ref=None, emask=None):
        n = GROUP_HEADS
        return _KeyTile([_key_rows(k_ref, j, t)] * n, [vt_ref[j]] * n,
                        None if bias_ref is None else [bias_ref[h] for h in _HEADS],
                        None if emask is None else [emask] * n,
                        None if live is None else [jnp.broadcast_to(live, (1, t))] * n)

    prev = jnp.maximum(i - 1, 0)
    states = tuple(_softmax_init(t) for _ in _HEADS)
    states = _pair_loop(prev, states, lambda j0, j1, live1, st: _softmax_tiles(
        st, qts, [tile(ks_ref, vst_ref, j0, emask=chosen(j0)), tile(ks_ref, vst_ref, j1, live1, emask=chosen(j1))]))
    states = _softmax_tiles(states, qts, [
        tile(ks_ref, vst_ref, prev, i >= 1, bs_ref, chosen(prev)),
        tile(ks_ref, vst_ref, i, None, bd_ref, jnp.logical_and(chosen(i), causal))])
    o_slc = [_softmax_out(st) for st in states]

    n_back = WINDOW // t
    window = [tile(kw_ref, vwt_ref, jnp.maximum(i - n_back, 0), i >= n_back, emask=qry < key)]
    for back in range(n_back - 1, 0, -1):
        window.append(tile(kw_ref, vwt_ref, jnp.maximum(i - back, 0), i >= back, bs_ref if back == 1 else None))
    window.append(tile(kw_ref, vwt_ref, i, None, bd_ref, causal))
    states = _softmax_tiles(tuple(_softmax_init(t) for _ in _HEADS), qts, window)
    o_win = [_softmax_out(st) for st in states]

    gates = _transposed(gq_ref[0, :, (gate_col // LANES) * LANES:(gate_col // LANES + 1) * LANES])
    gates = 1.0 / (1.0 + jnp.exp(-gates[gate_col % LANES:gate_col % LANES + N_GATES + 4]))
    outs = []
    for h in _HEADS:
        g = [gates[br * GROUP_HEADS + h:br * GROUP_HEADS + h + 1] for br in range(3)]
        outs.append(g[0] * o_cmp[h] + g[1] * o_slc[h] + g[2] * o_win[h])
    _store_heads(o_ref, outs)


def _nsa(proj, kc, vc_t, bias_diag, bias_sub, bias_cmp, cover_t, expand, t=TILE):
    bsz, s, _ = proj.shape
    n_cmp = kc.shape[1]
    kv_scratch = [pltpu.VMEM((s, HEAD_DIM), _MXU), pltpu.VMEM((s // t, ACC_ROWS, t), _MXU)]
    return pl.pallas_call(
        _nsa_kernel,
        grid=(bsz, s // t),
        in_specs=_group_specs(t, s, CB_NS_Q, CB_NS_A, CB_NS_B)
        + [pl.BlockSpec((1, t, GROUP_WIDTH), lambda b, i: (b, i, CB_NS_B)),
           pl.BlockSpec((1, n_cmp, HEAD_DIM), lambda b, i: (b, 0, 0)),
           pl.BlockSpec((1, HEAD_DIM, n_cmp), lambda b, i: (b, 0, 0))]
        + _bias_specs(t, 1)
        + [pl.BlockSpec((GROUP_HEADS, n_cmp, t), lambda b, i: (0, 0, i)),
           pl.BlockSpec(cover_t.shape, lambda b, i: (0, 0)),
           pl.BlockSpec(expand.shape, lambda b, i: (0, 0))],
        scratch_shapes=kv_scratch + kv_scratch,
        compiler_params=_params("parallel", "arbitrary"),
        name="nsa",
        **_group_out(bsz, s, t),
    )(proj, proj, proj, proj, kc, vc_t, bias_diag, bias_sub, bias_cmp, cover_t, expand)


def _nsa_constants(s):
    n_cmp = (s - CMP_LEN) // CMP_STRIDE + 1
    n_slc = s // SLC_LEN
    assert n_cmp + 1 == s // CMP_STRIDE and n_slc % SUBLANES == 0
    c_start = np.arange(n_cmp) * CMP_STRIDE
    s_start = np.arange(n_slc) * SLC_LEN
    cover = np.clip(np.minimum((c_start + CMP_LEN - 1)[:, None], (s_start + SLC_LEN - 1)[None, :])
                    - np.maximum(c_start[:, None], s_start[None, :]) + 1, 0, None) / CMP_LEN
    cover_t = np.zeros((n_slc, n_cmp + 1), np.float32)
    cover_t[:, :n_cmp] = cover.T
    expand = (np.arange(s)[:, None] // SLC_LEN == np.arange(n_slc)[None, :]).astype(np.float32)
    return jnp.asarray(cover_t), jnp.asarray(expand, _MXU)


def kernel(x, w_in, w_out, w_up, w_down, norm_attn, norm_mlp, cmp_pos_k, cmp_pos_v, cmp_k_w1, cmp_k_w2,
           cmp_v_w1, cmp_v_w2, diff_lambda, diff_subln, rel_bias, final_norm):
    bsz, s, d = x.shape
    depth = w_in.shape[0]
    t = TILE
    n_chunk = s // CMP_STRIDE
    assert s % MOBA_BLOCK == 0 and MOBA_BLOCK % t == 0 and WINDOW % t == 0 and t >= MAX_DISTANCE

    w_in_p = jnp.concatenate([w_in[:, :, :COLS_BEFORE_PAD],
                              jnp.zeros((depth, d, PAD_COLS), w_in.dtype),
                              w_in[:, :, COLS_BEFORE_PAD:]], axis=2).astype(_MXU)
    assert w_in_p.shape[2] == D_IN_PAD
    w_out_c, w_up_c, w_down_c = w_out.astype(_MXU), w_up.astype(_MXU), w_down.astype(_MXU)
    wk1, wk2 = cmp_k_w1.astype(_MXU), cmp_k_w2.astype(_MXU)
    wv1, wv2t = cmp_v_w1.astype(_MXU), jnp.swapaxes(cmp_v_w2, 1, 2).astype(_MXU)

    tiles = dict(n_heads=rel_bias.shape[1], head0=0, rows=t, cols=t, col_tile=t, row_stride=-1, col_stride=1)
    bias_diag = _bias_table(rel_bias, offset=0, **tiles)
    bias_sub = _bias_table(rel_bias, offset=t, **tiles)
    bias_cmp = _bias_table(rel_bias, n_heads=GROUP_HEADS, head0=GROUP_HEADS, rows=n_chunk, cols=s,
                           col_tile=t, row_stride=-CMP_STRIDE, col_stride=1, offset=-(CMP_LEN - 1))
    cover_t, expand = _nsa_constants(s)
    col_scale = np.ones((1, D_IN_PAD), np.float32)
    for cb, width in ((CB_SB_Q, HEAD_DIM), (CB_MB_Q, HEAD_DIM), (CB_NS_Q, HEAD_DIM), (CB_DF_Q, DIFF_HALF)):
        col_scale[:, cb * GROUP_WIDTH:(cb + 1) * GROUP_WIDTH] = width ** -0.5 * LOG2E
    col_scale = jnp.asarray(col_scale)

    x2 = x.reshape(bsz * s, d)
    for layer in range(depth):
        proj = _norm_matmul(x2, norm_attn[layer][None], w_in_p[layer], col_scale).reshape(bsz, s, D_IN_PAD)
        o_sb = _stick_breaking(proj)
        o_mb = _moba(proj, bias_diag, bias_sub)
        col0 = CB_NS_A * GROUP_WIDTH
        chunks_k = proj[:, :, col0:col0 + HEAD_DIM].reshape(bsz, n_chunk, CMP_STRIDE * HEAD_DIM)
        chunks_v = proj[:, :, col0 + HEAD_DIM:col0 + 2 * HEAD_DIM].reshape(bsz, n_chunk, CMP_STRIDE * HEAD_DIM)
        kc, vc_t = _compress(chunks_k, chunks_v, cmp_pos_k[layer].reshape(1, -1), cmp_pos_v[layer].reshape(1, -1),
                             wk1[layer], wk2[layer], wv1[layer], wv2t[layer])
        o_ns = _nsa(proj, kc, vc_t, bias_diag, bias_sub, bias_cmp, cover_t, expand)
        lambda_init = 0.8 - 0.6 * math.exp(-0.3 * layer)
        o_df = _diff(proj, diff_lambda[layer], diff_subln[layer][:, None], bias_diag, bias_sub, lambda_init)
        groups = [o.reshape(bsz * s, GROUP_WIDTH) for o in (o_sb, o_mb, o_ns, o_df)]
        x2 = _out_mlp(x2, groups, w_out_c[layer], norm_mlp[layer][None], w_up_c[layer], w_down_c[layer],
                      final_norm[None], final_norm=(layer == depth - 1))
    return x2.reshape(bsz, s, d)
```

```python
import functools
import math
from typing import Any, NamedTuple, Optional, Sequence

import numpy as np
import jax
import jax.numpy as jnp
from jax import lax
from jax.experimental import pallas as pl
from jax.experimental.pallas import tpu as pltpu

HEAD_DIM = 64
GROUP_HEADS = 4
GROUP_WIDTH = GROUP_HEADS * HEAD_DIM
NORM_EPS = 1e-6
NEG_INF = -1e30
BIG = 1e30
FORCE = 1e30
TINY = 1e-30
PICKED = -3e38
LOG2E = math.log2(math.e)
N_BUCKETS = 32
MAX_DISTANCE = 128
MOBA_BLOCK = 256
MOBA_TOPK = 3
CMP_LEN = 32
CMP_STRIDE = 16
SLC_LEN = 64
SLC_TOPN = 4
WINDOW = 512
DIFF_HALF = HEAD_DIM // 2
LANES = 128
SUBLANES = 8
TILE = 256
BF16_ROWS = 16
ACC_ROWS = HEAD_DIM + BF16_ROWS
N_GATES = 3 * GROUP_HEADS
COLS_BEFORE_PAD = 9 * GROUP_WIDTH - 2 * HEAD_DIM + N_GATES
PAD_COLS = 2 * HEAD_DIM - N_GATES
CB_SB_Q, CB_SB_K, CB_SB_V, CB_MB_Q, CB_MB_K, CB_MB_V, CB_NS_Q, CB_NS_A, CB_NS_B, CB_DF_Q, CB_DF_K, CB_DF_V = range(12)
D_IN_PAD = 12 * GROUP_WIDTH

_MXU = jnp.bfloat16
_VMEM_LIMIT = 56 * 1024 * 1024
_HEADS = range(GROUP_HEADS)
LOOKAHEAD = 2


def _dot(a, b, precision=None):
    return jnp.dot(a, b, precision=precision, preferred_element_type=jnp.float32)


def _dot_nt(a, b, precision=None):
    return lax.dot_general(a, b, (((1,), (1,)), ((), ())), precision=precision,
                           preferred_element_type=jnp.float32)


def _rms(x, g):
    return x * lax.rsqrt(jnp.mean(x * x, axis=-1, keepdims=True) + NORM_EPS) * g


def _params(*sem):
    return pltpu.CompilerParams(dimension_semantics=sem, vmem_limit_bytes=_VMEM_LIMIT)


def _norm_matmul_kernel(x_ref, g_ref, w_ref, scale_ref, o_ref, *, tn):
    h = _rms(x_ref[...], g_ref[...]).astype(_MXU)
    for j in range(w_ref.shape[1] // tn):
        cols = slice(j * tn, (j + 1) * tn)
        o_ref[:, cols] = (_dot(h, w_ref[:, cols]) * scale_ref[:, cols]).astype(o_ref.dtype)


def _norm_matmul(x, g, w, col_scale, *, tm=512, tn=1024):
    m, d = x.shape
    n = w.shape[1]
    return pl.pallas_call(
        functools.partial(_norm_matmul_kernel, tn=tn),
        grid=(m // tm,),
        in_specs=[pl.BlockSpec((tm, d), lambda i: (i, 0)),
                  pl.BlockSpec((1, d), lambda i: (0, 0)),
                  pl.BlockSpec((d, n), lambda i: (0, 0)),
                  pl.BlockSpec((1, n), lambda i: (0, 0))],
        out_specs=pl.BlockSpec((tm, n), lambda i: (i, 0)),
        out_shape=jax.ShapeDtypeStruct((m, n), _MXU),
        compiler_params=_params("parallel"),
        name="norm_in_proj",
    )(x, g, w, col_scale)


def _out_mlp_kernel(x_ref, a_ref, b_ref, c_ref, d_ref, wo_ref, g_ref, wu_ref, wd_ref, gf_ref, o_ref,
                    h_ref, acc_ref, *, final_norm):
    c = pl.program_id(1)

    @pl.when(c == 0)
    def _():
        y = x_ref[...]
        for g, ref in enumerate((a_ref, b_ref, c_ref, d_ref)):
            y = y + _dot(ref[...], wo_ref[g * GROUP_WIDTH:(g + 1) * GROUP_WIDTH, :])
        acc_ref[...] = y
        h_ref[...] = _rms(y, g_ref[...]).astype(h_ref.dtype)

    u = jnp.square(jnp.maximum(_dot(h_ref[...], wu_ref[...]), 0.0))
    acc_ref[...] += _dot(u.astype(_MXU), wd_ref[...])

    @pl.when(c == pl.num_programs(1) - 1)
    def _():
        y = acc_ref[...]
        if final_norm:
            y = _rms(y, gf_ref[...])
        o_ref[...] = y


def _out_mlp(x, groups, w_out, g, w_up, w_down, g_final, *, final_norm, tm=512, tf=2048):
    m, d = x.shape
    f = w_up.shape[1]
    gspec = pl.BlockSpec((tm, GROUP_WIDTH), lambda i, c: (i, 0))
    row = pl.BlockSpec((1, d), lambda i, c: (0, 0))
    return pl.pallas_call(
        functools.partial(_out_mlp_kernel, final_norm=final_norm),
        grid=(m // tm, f // tf),
        in_specs=[pl.BlockSpec((tm, d), lambda i, c: (i, 0)), gspec, gspec, gspec, gspec,
                  pl.BlockSpec((d, d), lambda i, c: (0, 0)), row,
                  pl.BlockSpec((d, tf), lambda i, c: (0, c)),
                  pl.BlockSpec((tf, d), lambda i, c: (c, 0)), row],
        out_specs=pl.BlockSpec((tm, d), lambda i, c: (i, 0)),
        out_shape=jax.ShapeDtypeStruct((m, d), jnp.float32),
        scratch_shapes=[pltpu.VMEM((tm, d), _MXU), pltpu.VMEM((tm, d), jnp.float32)],
        compiler_params=_params("parallel", "arbitrary"),
        name="out_proj_mlp_residual",
    )(x, *groups, w_out, g, w_up, w_down, g_final)


def _t5_bucket(dist):
    n = jnp.maximum(dist, 0)
    max_exact = N_BUCKETS // 2
    nf = jnp.maximum(n, 1).astype(jnp.float32)
    large = max_exact + (jnp.log(nf / max_exact) / math.log(MAX_DISTANCE / max_exact)
                         * (N_BUCKETS - max_exact)).astype(jnp.int32)
    large = jnp.minimum(large, N_BUCKETS - 1)
    return jnp.where(n < max_exact, n, large)


def _bias_kernel(tab_ref, o_ref, *, row_stride, col_stride, offset, head0):
    nh, tr, tc = o_ref.shape
    rows = lax.broadcasted_iota(jnp.int32, (tr, tc), 0)
    cols = lax.broadcasted_iota(jnp.int32, (tr, tc), 1) + pl.program_id(0) * tc
    bucket = _t5_bucket(rows * row_stride + cols * col_stride + offset)
    for h in range(nh):
        out = jnp.zeros((tr, tc), jnp.float32)
        for bkt in range(N_BUCKETS):
            out = jnp.where(bucket == bkt, tab_ref[bkt, head0 + h], out)
        o_ref[h] = (out - tab_ref[N_BUCKETS - 1, head0 + h]) * LOG2E


def _bias_table(table, *, n_heads, head0, rows, cols, col_tile, row_stride, col_stride, offset):
    return pl.pallas_call(
        functools.partial(_bias_kernel, row_stride=row_stride, col_stride=col_stride,
                          offset=offset, head0=head0),
        grid=(cols // col_tile,),
        in_specs=[pl.BlockSpec(memory_space=pltpu.SMEM)],
        out_specs=pl.BlockSpec((n_heads, rows, col_tile), lambda i: (0, 0, i)),
        out_shape=jax.ShapeDtypeStruct((n_heads, rows, cols), jnp.float32),
        compiler_params=_params("parallel"),
        name="t5_bias_tiles",
    )(table)


def _softmax_init(t):
    return (jnp.full((1, t), NEG_INF, jnp.float32), jnp.zeros((ACC_ROWS, t), jnp.float32))


class _KeyTile(NamedTuple):
    kts: Sequence[Any]
    vts: Sequence[Any]
    biases: Optional[Sequence[Any]] = None
    emasks: Optional[Sequence[Any]] = None
    qmasks: Optional[Sequence[Any]] = None


def _softmax_groups(n, t, qts, groups):
    built = {}

    def tiles_of(g):
        if g not in built:
            built[g] = [make() for make in groups[g]]
        return built[g]

    def scores_of(g, c):
        row = []
        for tile in tiles_of(g):
            s = _dot(tile.kts[c], qts[c])
            if tile.biases is not None:
                s = s + tile.biases[c]
            if tile.emasks is not None:
                s = jnp.where(tile.emasks[c], s, NEG_INF)
            row.append(s)
        return row

    def update(state, g, c, scores):
        m, acc = state
        m_new = m
        for tile, s in zip(tiles_of(g), scores):
            tile_max = jnp.max(s, axis=0, keepdims=True)
            if tile.qmasks is not None:
                tile_max = jnp.where(tile.qmasks[c], tile_max, NEG_INF)
            m_new = jnp.maximum(m_new, tile_max)
        seen = m_new > 0.5 * NEG_INF
        acc = jnp.exp2(m - m_new) * acc
        for tile, s in zip(tiles_of(g), scores):
            ok = seen if tile.qmasks is None else jnp.logical_and(seen, tile.qmasks[c])
            acc = acc + _dot(tile.vts[c], jnp.exp2(s - jnp.where(ok, m_new, BIG)).astype(_MXU))
        return m_new, acc

    units = [(g, c) for g in range(len(groups)) for c in range(n)]
    states = [_softmax_init(t) for _ in range(n)]
    pending = {k: scores_of(*units[k]) for k in range(min(LOOKAHEAD, len(units)))}
    for k, (g, c) in enumerate(units):
        if k + LOOKAHEAD < len(units):
            pending[k + LOOKAHEAD] = scores_of(*units[k + LOOKAHEAD])
        states[c] = update(states[c], g, c, pending.pop(k))
    return states


def _pairs(items):
    return [items[p:p + 2] for p in range(0, len(items), 2)]


def _per_query_tile(i, n_tiles, body):
    for k in range(n_tiles):
        pl.when(i == k)(functools.partial(body, k))


def _softmax_out(state):
    acc = state[1]
    return acc[:HEAD_DIM] / jnp.maximum(acc[HEAD_DIM:HEAD_DIM + 1], TINY)


def _top_k_rows(score, row_f, k):
    sel = jnp.zeros(score.shape, jnp.float32)
    for _ in range(k):
        mx = jnp.max(score, axis=0, keepdims=True)
        idx = jnp.min(jnp.where(score == mx, row_f, float(score.shape[0])), axis=0, keepdims=True)
        pick = row_f == idx
        sel = jnp.where(pick, 1.0, sel)
        score = jnp.where(pick, PICKED, score)
    return sel


def _tile_iotas(t):
    return (lax.broadcasted_iota(jnp.int32, (t, t), 0), lax.broadcasted_iota(jnp.int32, (t, t), 1))


def _key_rows(ref, j, t):
    if isinstance(j, int):
        return ref[j * t:(j + 1) * t, :]
    return ref[pl.ds(pl.multiple_of(j * t, t), t), :]


def _transposed(ref_block):
    return ref_block.astype(jnp.float32).T


def _queries_t(q_ref, width):
    qt = _transposed(q_ref[0])
    return [qt[c * width:(c + 1) * width].astype(_MXU) for c in range(GROUP_WIDTH // width)]


def _fill_values_t(vt_ref, v_ref, col0, t):
    n_tiles, rows, _ = vt_ref.shape
    lane_block = (col0 // LANES) * LANES
    for c in range(n_tiles):
        blk = _transposed(v_ref[0, c * t:(c + 1) * t, lane_block:lane_block + LANES])
        vt_ref[c, 0:HEAD_DIM, :] = blk[col0 - lane_block:col0 - lane_block + HEAD_DIM].astype(vt_ref.dtype)
        if rows == ACC_ROWS:
            first = lax.broadcasted_iota(jnp.int32, (rows - HEAD_DIM, t), 0) == 0
            vt_ref[c, HEAD_DIM:rows, :] = jnp.where(first, 1.0, 0.0).astype(vt_ref.dtype)


def _group_specs(t, s, cb_q, cb_k, cb_v):
    return [pl.BlockSpec((1, t, GROUP_WIDTH), lambda b, i: (b, i, cb_q)),
            pl.BlockSpec((1, s, GROUP_WIDTH), lambda b, i: (b, 0, cb_k)),
            pl.BlockSpec((1, s, GROUP_WIDTH), lambda b, i: (b, 0, cb_v))]


def _group_out(bsz, s, t):
    return dict(out_specs=pl.BlockSpec((1, t, GROUP_WIDTH), lambda b, i: (b, i, 0)),
                out_shape=jax.ShapeDtypeStruct((bsz, s, GROUP_WIDTH), _MXU))


def _bias_specs(t, head_group):
    spec = pl.BlockSpec((GROUP_HEADS, t, t), lambda b, i: (head_group, 0, 0))
    return [spec, spec]


def _store_heads(o_ref, outs_t):
    o_ref[0] = jnp.concatenate(outs_t, axis=0).T.astype(o_ref.dtype)


def _sb_kernel(q_ref, k_ref, v_ref, o_ref, kb_ref, vt_ref):
    t = q_ref.shape[1]
    i = pl.program_id(1)
    key, qry = _tile_iotas(t)
    strict = key < qry
    later = jnp.where(qry > key, 1.0, 0.0).astype(_MXU)
    later2 = jnp.concatenate([later, later], axis=1)

    @pl.when(i == 0)
    def _():
        for h in _HEADS:
            kb_ref[h] = k_ref[0, :, h * HEAD_DIM:(h + 1) * HEAD_DIM].astype(kb_ref.dtype)
            _fill_values_t(vt_ref.at[h], v_ref, h * HEAD_DIM, t)

    def query_tile(k):
        qts = _queries_t(q_ref, HEAD_DIM)
        units = [(pair, h) for pair in _pairs(list(range(k, -1, -1))) for h in _HEADS]
        carry = [(jnp.zeros((HEAD_DIM, t), jnp.float32), jnp.zeros((1, t), jnp.float32))] * GROUP_HEADS
        zs, log_keeps, suffixes = {}, {}, {}

        def scores(u):
            pair, h = units[u]
            zs[u] = [_dot(_key_rows(kb_ref.at[h], j, t), qts[h]) for j in pair]

        def keeps(u):
            pair, _ = units[u]
            log_keeps[u], suffixes[u] = [], []
            for j, z in zip(pair, zs[u]):
                neg = -z
                log_keep = jnp.minimum(neg, 0.0) - jnp.log2(1.0 + jnp.exp2(jnp.minimum(z, neg)))
                if j == k:
                    log_keep = jnp.where(strict, log_keep, 0.0)
                piece0 = log_keep.astype(_MXU)
                piece1 = (log_keep - piece0.astype(jnp.float32)).astype(_MXU)
                log_keeps[u].append(log_keep)
                suffixes[u].append(_dot(later2, jnp.concatenate([piece0, piece1], axis=0)))

        def values(u):
            pair, h = units[u]
            acc, run = carry[h]
            weights = []
            for j, z, log_keep, suffix in zip(pair, zs.pop(u), log_keeps.pop(u), suffixes.pop(u)):
                a = jnp.exp2(z + log_keep + suffix + run)
                if j == k:
                    a = jnp.where(strict, a, 0.0)
                weights.append(a.astype(_MXU))
                run = run + jnp.sum(log_keep, axis=0, keepdims=True)
            for j, w in zip(pair, weights):
                acc = acc + _dot(vt_ref[h, j], w)
            carry[h] = (acc, run)

        stages = (scores, keeps, values)
        for step in range(len(units) + len(stages) - 1):
            for lag, stage in enumerate(stages):
                if 0 <= step - lag < len(units):
                    stage(step - lag)
        _store_heads(o_ref, [c[0] for c in carry])

    _per_query_tile(i, k_ref.shape[1] // t, query_tile)


def _stick_breaking(proj, t=TILE):
    bsz, s, _ = proj.shape
    return pl.pallas_call(
        _sb_kernel,
        grid=(bsz, s // t),
        in_specs=_group_specs(t, s, CB_SB_Q, CB_SB_K, CB_SB_V),
        scratch_shapes=[pltpu.VMEM((GROUP_HEADS, s, HEAD_DIM), _MXU),
                        pltpu.VMEM((GROUP_HEADS, s // t, HEAD_DIM, t), _MXU)],
        compiler_params=_params("parallel", "arbitrary"),
        name="stick_breaking",
        **_group_out(bsz, s, t),
    )(proj, proj, proj)


def _moba_kernel(q_ref, k_ref, v_ref, bd_ref, bs_ref, o_ref, kb_ref, vt_ref, km_ref):
    t = q_ref.shape[1]
    n_blk = k_ref.shape[1] // MOBA_BLOCK
    tiles_per_blk = MOBA_BLOCK // t
    blk_shift = int(math.log2(tiles_per_blk))
    i = pl.program_id(1)

    @pl.when(i == 0)
    def _():
        km_ref[...] = jnp.zeros_like(km_ref)
        for h in _HEADS:
            lo, hi = h * HEAD_DIM, (h + 1) * HEAD_DIM
            kb_ref[h] = k_ref[0, :, lo:hi].astype(kb_ref.dtype)
            _fill_values_t(vt_ref.at[h], v_ref, lo, t)
            for n in range(n_blk):
                blk = k_ref[0, n * MOBA_BLOCK:(n + 1) * MOBA_BLOCK, lo:hi]
                km_ref[h, n:n + 1, :] = jnp.mean(blk.astype(jnp.float32), axis=0, keepdims=True)

    def query_tile(k):
        own = k >> blk_shift
        key, qry = _tile_iotas(t)
        causal = key <= qry
        blk_row = lax.broadcasted_iota(jnp.int32, (km_ref.shape[1], t), 0)
        qt = _transposed(q_ref[0])
        qts, sels = [], []
        for h in _HEADS:
            qf = qt[h * HEAD_DIM:(h + 1) * HEAD_DIM]
            qts.append(qf.astype(_MXU))
            gate = _dot(km_ref[h], qf, precision=lax.Precision.HIGHEST)
            gate = jnp.where(blk_row < own, gate, NEG_INF)
            sel = _top_k_rows(gate, blk_row.astype(jnp.float32), min(MOBA_TOPK, n_blk - 1))
            sels.append(jnp.where(blk_row < own, sel, 0.0))

        def tile(j, bias_ref=None, emask=None):
            n = j >> blk_shift
            return _KeyTile([_key_rows(kb_ref.at[h], j, t) for h in _HEADS], [vt_ref[h, j] for h in _HEADS],
                            None if bias_ref is None else [bias_ref[h] for h in _HEADS],
                            None if emask is None else [emask] * GROUP_HEADS,
                            None if n == own else [sels[h][n:n + 1] > 0.5 for h in _HEADS])

        groups = _pairs([functools.partial(tile, j) for j in range(k - 1)])
        groups.append(([functools.partial(tile, k - 1, bs_ref)] if k else [])
                      + [functools.partial(tile, k, bd_ref, causal)])
        _store_heads(o_ref, [_softmax_out(st) for st in _softmax_groups(GROUP_HEADS, t, qts, groups)])

    _per_query_tile(i, k_ref.shape[1] // t, query_tile)


def _kv_scratch(s, t, key_dim=HEAD_DIM, n_keys=GROUP_HEADS):
    return [pltpu.VMEM((n_keys, s, key_dim), _MXU), pltpu.VMEM((GROUP_HEADS, s // t, ACC_ROWS, t), _MXU)]


def _moba(proj, bias_diag, bias_sub, t=TILE):
    bsz, s, _ = proj.shape
    n_blk_pad = -(-(s // MOBA_BLOCK) // SUBLANES) * SUBLANES
    return pl.pallas_call(
        _moba_kernel,
        grid=(bsz, s // t),
        in_specs=_group_specs(t, s, CB_MB_Q, CB_MB_K, CB_MB_V) + _bias_specs(t, 0),
        scratch_shapes=_kv_scratch(s, t) + [pltpu.VMEM((GROUP_HEADS, n_blk_pad, HEAD_DIM), jnp.float32)],
        compiler_params=_params("parallel", "arbitrary"),
        name="moba",
        **_group_out(bsz, s, t),
    )(proj, proj, proj, bias_diag, bias_sub)


def _diff_kernel(lam_ref, g_ref, q_ref, k_ref, v_ref, bd_ref, bs_ref, o_ref, kb_ref, vt_ref, *, lambda_init):
    t = q_ref.shape[1]
    i = pl.program_id(1)
    key, qry = _tile_iotas(t)
    causal = key <= qry
    lv = lam_ref[...]
    lam = (jnp.exp(jnp.sum(lv[0:1] * lv[1:2], keepdims=True))
           - jnp.exp(jnp.sum(lv[2:3] * lv[3:4], keepdims=True)) + lambda_init)
    halves = range(2 * GROUP_HEADS)

    @pl.when(i == 0)
    def _():
        for c in halves:
            kb_ref[c] = k_ref[0, :, c * DIFF_HALF:(c + 1) * DIFF_HALF].astype(kb_ref.dtype)
        for h in _HEADS:
            _fill_values_t(vt_ref.at[h], v_ref, h * HEAD_DIM, t)

    def tile(j, bias_ref=None, emask=None):
        n = len(halves)
        return _KeyTile([_key_rows(kb_ref.at[c], j, t) for c in halves], [vt_ref[c // 2, j] for c in halves],
                        None if bias_ref is None else [bias_ref[c // 2] for c in halves],
                        None if emask is None else [emask] * n)

    def query_tile(k):
        groups = _pairs([functools.partial(tile, j) for j in range(k - 1)])
        groups.append(([functools.partial(tile, k - 1, bs_ref)] if k else [])
                      + [functools.partial(tile, k, bd_ref, causal)])
        states = _softmax_groups(len(halves), t, _queries_t(q_ref, DIFF_HALF), groups)
        outs = []
        for h in _HEADS:
            o = _softmax_out(states[2 * h]) - lam * _softmax_out(states[2 * h + 1])
            o = o * lax.rsqrt(jnp.mean(o * o, axis=0, keepdims=True) + NORM_EPS) * g_ref[...]
            outs.append(o * (1.0 - lambda_init))
        _store_heads(o_ref, outs)

    _per_query_tile(i, k_ref.shape[1] // t, query_tile)


def _diff(proj, lam_params, subln, bias_diag, bias_sub, lambda_init, t=TILE):
    bsz, s, _ = proj.shape
    return pl.pallas_call(
        functools.partial(_diff_kernel, lambda_init=lambda_init),
        grid=(bsz, s // t),
        in_specs=[pl.BlockSpec(lam_params.shape, lambda b, i: (0, 0)),
                  pl.BlockSpec(subln.shape, lambda b, i: (0, 0))]
        + _group_specs(t, s, CB_DF_Q, CB_DF_K, CB_DF_V) + _bias_specs(t, 2),
        scratch_shapes=_kv_scratch(s, t, key_dim=DIFF_HALF, n_keys=2 * GROUP_HEADS),
        compiler_params=_params("parallel", "arbitrary"),
        name="diff_attention",
        **_group_out(bsz, s, t),
    )(lam_params, subln, proj, proj, proj, bias_diag, bias_sub)


def _compress_kernel(ck_ref, cv_ref, pk_ref, pv_ref, wk1_ref, wk2_ref, wv1_ref, wv2t_ref, kc_ref, vct_ref):
    half = ck_ref.shape[2]

    def hidden(c_ref, p_ref, w1_ref):
        chunks = c_ref[0]
        top = _dot((chunks + p_ref[:, :half]).astype(_MXU), w1_ref[:half, :])
        bot = _dot((chunks + p_ref[:, half:]).astype(_MXU), w1_ref[half:, :])
        pre = top + pltpu.roll(bot, bot.shape[0] - 1, axis=0)
        return jax.nn.gelu(pre).astype(_MXU)

    kc_ref[0] = _dot(hidden(ck_ref, pk_ref, wk1_ref), wk2_ref[...])
    vct_ref[0] = _dot_nt(wv2t_ref[...], hidden(cv_ref, pv_ref, wv1_ref))


def _compress(chunks_k, chunks_v, pos_k, pos_v, wk1, wk2, wv1, wv2t):
    bsz, n_chunk, width = chunks_k.shape
    cspec = pl.BlockSpec((1, n_chunk, width), lambda b: (b, 0, 0))

    def full(a):
        return pl.BlockSpec(a.shape, lambda b: (0, 0))

    return pl.pallas_call(
        _compress_kernel,
        grid=(bsz,),
        in_specs=[cspec, cspec, full(pos_k), full(pos_v), full(wk1), full(wk2), full(wv1), full(wv2t)],
        out_specs=[pl.BlockSpec((1, n_chunk, HEAD_DIM), lambda b: (b, 0, 0)),
                   pl.BlockSpec((1, HEAD_DIM, n_chunk), lambda b: (b, 0, 0))],
        out_shape=[jax.ShapeDtypeStruct((bsz, n_chunk, HEAD_DIM), jnp.float32),
                   jax.ShapeDtypeStruct((bsz, HEAD_DIM, n_chunk), jnp.float32)],
        compiler_params=_params("parallel"),
        name="nsa_compress",
    )(chunks_k, chunks_v, pos_k, pos_v, wk1, wk2, wv1, wv2t)


def _nsa_kernel(q_ref, kva_ref, kvb_ref, gq_ref, kc_ref, vct_ref, bd_ref, bs_ref, bc_ref, cover_ref, e_ref,
                o_ref, ks_ref, vst_ref, kw_ref, vwt_ref):
    t = q_ref.shape[1]
    i = pl.program_id(1)
    key, qry = _tile_iotas(t)
    causal = key <= qry
    ks_col, vs_col, kw_col, vw_col, gate_col = 2 * HEAD_DIM, 3 * HEAD_DIM, 0, HEAD_DIM, 2 * HEAD_DIM

    @pl.when(i == 0)
    def _():
        ks_ref[...] = kva_ref[0, :, ks_col:ks_col + HEAD_DIM].astype(ks_ref.dtype)
        kw_ref[...] = kvb_ref[0, :, kw_col:kw_col + HEAD_DIM].astype(kw_ref.dtype)
        _fill_values_t(vst_ref, kva_ref, vs_col, t)
        _fill_values_t(vwt_ref, kvb_ref, vw_col, t)

    def tile(k_ref, vt_ref, j, bias_ref=None, emask=None):
        n = GROUP_HEADS
        emask = emask() if callable(emask) else emask
        return _KeyTile([_key_rows(k_ref, j, t)] * n, [vt_ref[j]] * n,
                        None if bias_ref is None else [bias_ref[h] for h in _HEADS],
                        None if emask is None else [emask] * n)

    def query_tile(k):
        qts = _queries_t(q_ref, HEAD_DIM)

        n_back = WINDOW // t
        window = [functools.partial(tile, kw_ref, vwt_ref, k - n_back, None, qry < key)] if k >= n_back else []
        for back in range(min(n_back - 1, k), 0, -1):
            window.append(functools.partial(tile, kw_ref, vwt_ref, k - back, bs_ref if back == 1 else None))
        window.append(functools.partial(tile, kw_ref, vwt_ref, k, bd_ref, causal))
        o_win = [_softmax_out(st) for st in _softmax_groups(GROUP_HEADS, t, qts, [window])]

        n_cmp = kc_ref.shape[1]
        kc = kc_ref[0].astype(_MXU)
        vct = vct_ref[0].astype(_MXU)
        c_row = lax.broadcasted_iota(jnp.int32, (n_cmp, t), 0)
        c_qpos = lax.broadcasted_iota(jnp.int32, (n_cmp, t), 1) + k * t
        visible = c_qpos >= c_row * CMP_STRIDE + (CMP_LEN - 1)
        cmp_scores = [_dot(kc, qts[h]) for h in _HEADS]
        cmp_probs = []
        p_sum = jnp.zeros((n_cmp, t), jnp.float32)
        for h in _HEADS:
            sc = jnp.where(visible, cmp_scores[h] + bc_ref[h], NEG_INF)
            e = jnp.where(visible, jnp.exp2(sc - jnp.max(sc, axis=0, keepdims=True)), 0.0)
            p = e / jnp.maximum(jnp.sum(e, axis=0, keepdims=True), TINY)
            cmp_probs.append(p.astype(_MXU))
            p_sum = p_sum + p
        o_cmp = [_dot(vct, cmp_probs[h]) for h in _HEADS]

        n_slc = cover_ref.shape[0]
        importance = _dot(cover_ref[...], p_sum, precision=lax.Precision.HIGHEST)
        s_row = lax.broadcasted_iota(jnp.int32, (n_slc, t), 0)
        own = jnp.right_shift(lax.broadcasted_iota(jnp.int32, (n_slc, t), 1) + k * t, int(math.log2(SLC_LEN)))
        score = jnp.where(s_row == own, FORCE, jnp.where(s_row < own, importance, NEG_INF))
        sel = _top_k_rows(score, s_row.astype(jnp.float32), min(SLC_TOPN, n_slc)).astype(_MXU)

        def chosen(j):
            return _dot(_key_rows(e_ref, j, t), sel) > 0.5

        groups = _pairs([functools.partial(tile, ks_ref, vst_ref, j, None, functools.partial(chosen, j))
                         for j in range(k - 1)])
        groups.append(([functools.partial(tile, ks_ref, vst_ref, k - 1, bs_ref, functools.partial(chosen, k - 1))]
                       if k else [])
                      + [functools.partial(tile, ks_ref, vst_ref, k, bd_ref,
                                           lambda: jnp.logical_and(chosen(k), causal))])
        o_slc = [_softmax_out(st) for st in _softmax_groups(GROUP_HEADS, t, qts, groups)]

        gates = _transposed(gq_ref[0, :, (gate_col // LANES) * LANES:(gate_col // LANES + 1) * LANES])
        gates = 1.0 / (1.0 + jnp.exp(-gates[gate_col % LANES:gate_col % LANES + N_GATES + 4]))
        outs = []
        for h in _HEADS:
            g = [gates[br * GROUP_HEADS + h:br * GROUP_HEADS + h + 1] for br in range(3)]
            outs.append(g[0] * o_cmp[h] + g[1] * o_slc[h] + g[2] * o_win[h])
        _store_heads(o_ref, outs)

    _per_query_tile(i, kva_ref.shape[1] // t, query_tile)


def _nsa(proj, kc, vc_t, bias_diag, bias_sub, bias_cmp, cover_t, expand, t=TILE):
    bsz, s, _ = proj.shape
    n_cmp = kc.shape[1]
    kv_scratch = [pltpu.VMEM((s, HEAD_DIM), _MXU), pltpu.VMEM((s // t, ACC_ROWS, t), _MXU)]
    return pl.pallas_call(
        _nsa_kernel,
        grid=(bsz, s // t),
        in_specs=_group_specs(t, s, CB_NS_Q, CB_NS_A, CB_NS_B)
        + [pl.BlockSpec((1, t, GROUP_WIDTH), lambda b, i: (b, i, CB_NS_B)),
           pl.BlockSpec((1, n_cmp, HEAD_DIM), lambda b, i: (b, 0, 0)),
           pl.BlockSpec((1, HEAD_DIM, n_cmp), lambda b, i: (b, 0, 0))]
        + _bias_specs(t, 1)
        + [pl.BlockSpec((GROUP_HEADS, n_cmp, t), lambda b, i: (0, 0, i)),
           pl.BlockSpec(cover_t.shape, lambda b, i: (0, 0)),
           pl.BlockSpec(expand.shape, lambda b, i: (0, 0))],
        scratch_shapes=kv_scratch + kv_scratch,
        compiler_params=_params("parallel", "arbitrary"),
        name="nsa",
        **_group_out(bsz, s, t),
    )(proj, proj, proj, proj, kc, vc_t, bias_diag, bias_sub, bias_cmp, cover_t, expand)


def _nsa_constants(s):
    n_cmp = (s - CMP_LEN) // CMP_STRIDE + 1
    n_slc = s // SLC_LEN
    assert n_cmp + 1 == s // CMP_STRIDE and n_slc % SUBLANES == 0
    c_start = np.arange(n_cmp) * CMP_STRIDE
    s_start = np.arange(n_slc) * SLC_LEN
    cover = np.clip(np.minimum((c_start + CMP_LEN - 1)[:, None], (s_start + SLC_LEN - 1)[None, :])
                    - np.maximum(c_start[:, None], s_start[None, :]) + 1, 0, None) / CMP_LEN
    cover_t = np.zeros((n_slc, n_cmp + 1), np.float32)
    cover_t[:, :n_cmp] = cover.T
    expand = (np.arange(s)[:, None] // SLC_LEN == np.arange(n_slc)[None, :]).astype(np.float32)
    return jnp.asarray(cover_t), jnp.asarray(expand, _MXU)


def kernel(x, w_in, w_out, w_up, w_down, norm_attn, norm_mlp, cmp_pos_k, cmp_pos_v, cmp_k_w1, cmp_k_w2,
           cmp_v_w1, cmp_v_w2, diff_lambda, diff_subln, rel_bias, final_norm):
    bsz, s, d = x.shape
    depth = w_in.shape[0]
    t = TILE
    n_chunk = s // CMP_STRIDE
    assert s % MOBA_BLOCK == 0 and MOBA_BLOCK % t == 0 and WINDOW % t == 0 and t >= MAX_DISTANCE

    w_in_p = jnp.concatenate([w_in[:, :, :COLS_BEFORE_PAD],
                              jnp.zeros((depth, d, PAD_COLS), w_in.dtype),
                              w_in[:, :, COLS_BEFORE_PAD:]], axis=2).astype(_MXU)
    assert w_in_p.shape[2] == D_IN_PAD
    w_out_c, w_up_c, w_down_c = w_out.astype(_MXU), w_up.astype(_MXU), w_down.astype(_MXU)
    wk1, wk2 = cmp_k_w1.astype(_MXU), cmp_k_w2.astype(_MXU)
    wv1, wv2t = cmp_v_w1.astype(_MXU), jnp.swapaxes(cmp_v_w2, 1, 2).astype(_MXU)

    tiles = dict(n_heads=rel_bias.shape[1], head0=0, rows=t, cols=t, col_tile=t, row_stride=-1, col_stride=1)
    bias_diag = _bias_table(rel_bias, offset=0, **tiles)
    bias_sub = _bias_table(rel_bias, offset=t, **tiles)
    bias_cmp = _bias_table(rel_bias, n_heads=GROUP_HEADS, head0=GROUP_HEADS, rows=n_chunk, cols=s,
                           col_tile=t, row_stride=-CMP_STRIDE, col_stride=1, offset=-(CMP_LEN - 1))
    cover_t, expand = _nsa_constants(s)
    col_scale = np.ones((1, D_IN_PAD), np.float32)
    for cb, width in ((CB_SB_Q, HEAD_DIM), (CB_MB_Q, HEAD_DIM), (CB_NS_Q, HEAD_DIM), (CB_DF_Q, DIFF_HALF)):
        col_scale[:, cb * GROUP_WIDTH:(cb + 1) * GROUP_WIDTH] = width ** -0.5 * LOG2E
    col_scale = jnp.asarray(col_scale)

    x2 = x.reshape(bsz * s, d)
    for layer in range(depth):
        proj = _norm_matmul(x2, norm_attn[layer][None], w_in_p[layer], col_scale).reshape(bsz, s, D_IN_PAD)
        o_sb = _stick_breaking(proj)
        o_mb = _moba(proj, bias_diag, bias_sub)
        col0 = CB_NS_A * GROUP_WIDTH
        chunks_k = proj[:, :, col0:col0 + HEAD_DIM].reshape(bsz, n_chunk, CMP_STRIDE * HEAD_DIM)
        chunks_v = proj[:, :, col0 + HEAD_DIM:col0 + 2 * HEAD_DIM].reshape(bsz, n_chunk, CMP_STRIDE * HEAD_DIM)
        kc, vc_t = _compress(chunks_k, chunks_v, cmp_pos_k[layer].reshape(1, -1), cmp_pos_v[layer].reshape(1, -1),
                             wk1[layer], wk2[layer], wv1[layer], wv2t[layer])
        o_ns = _nsa(proj, kc, vc_t, bias_diag, bias_sub, bias_cmp, cover_t, expand)
        lambda_init = 0.8 - 0.6 * math.exp(-0.3 * layer)
        o_df = _diff(proj, diff_lambda[layer], diff_subln[layer][:, None], bias_diag, bias_sub, lambda_init)
        groups = [o.reshape(bsz * s, GROUP_WIDTH) for o in (o_sb, o_mb, o_ns, o_df)]
        x2 = _out_mlp(x2, groups, w_out_c[layer], norm_mlp[layer][None], w_up_c[layer], w_down_c[layer],
                      final_norm[None], final_norm=(layer == depth - 1))
    return x2.reshape(bsz, s, d)
```

```python
import functools
import math
from typing import Any, NamedTuple, Optional, Sequence

import numpy as np
import jax
import jax.numpy as jnp
from jax import lax
from jax.experimental import pallas as pl
from jax.experimental.pallas import tpu as pltpu

HEAD_DIM = 64
GROUP_HEADS = 4
GROUP_WIDTH = GROUP_HEADS * HEAD_DIM
NORM_EPS = 1e-6
NEG_INF = -1e30
BIG = 1e30
FORCE = 1e30
TINY = 1e-30
PICKED = -3e38
LOG2E = math.log2(math.e)
N_BUCKETS = 32
MAX_DISTANCE = 128
MOBA_BLOCK = 256
MOBA_TOPK = 3
CMP_LEN = 32
CMP_STRIDE = 16
SLC_LEN = 64
SLC_TOPN = 4
WINDOW = 512
DIFF_HALF = HEAD_DIM // 2
LANES = 128
SUBLANES = 8
TILE = 256
BF16_ROWS = 16
ACC_ROWS = HEAD_DIM + BF16_ROWS
N_GATES = 3 * GROUP_HEADS
COLS_BEFORE_PAD = 9 * GROUP_WIDTH - 2 * HEAD_DIM + N_GATES
PAD_COLS = 2 * HEAD_DIM - N_GATES
CB_SB_Q, CB_SB_K, CB_SB_V, CB_MB_Q, CB_MB_K, CB_MB_V, CB_NS_Q, CB_NS_A, CB_NS_B, CB_DF_Q, CB_DF_K, CB_DF_V = range(12)
D_IN_PAD = 12 * GROUP_WIDTH

_MXU = jnp.bfloat16
_VMEM_LIMIT = 56 * 1024 * 1024
_HEADS = range(GROUP_HEADS)
LOOKAHEAD = 2


def _dot(a, b, precision=None):
    return jnp.dot(a, b, precision=precision, preferred_element_type=jnp.float32)


def _dot_nt(a, b, precision=None):
    return lax.dot_general(a, b, (((1,), (1,)), ((), ())), precision=precision,
                           preferred_element_type=jnp.float32)


def _rms(x, g):
    return x * lax.rsqrt(jnp.mean(x * x, axis=-1, keepdims=True) + NORM_EPS) * g


def _params(*sem):
    return pltpu.CompilerParams(dimension_semantics=sem, vmem_limit_bytes=_VMEM_LIMIT)


def _norm_matmul_kernel(x_ref, g_ref, w_ref, scale_ref, o_ref, *, tn):
    h = _rms(x_ref[...], g_ref[...]).astype(_MXU)
    for j in range(w_ref.shape[1] // tn):
        cols = slice(j * tn, (j + 1) * tn)
        o_ref[:, cols] = (_dot(h, w_ref[:, cols]) * scale_ref[:, cols]).astype(o_ref.dtype)


def _norm_matmul(x, g, w, col_scale, *, tm=512, tn=1024):
    m, d = x.shape
    n = w.shape[1]
    return pl.pallas_call(
        functools.partial(_norm_matmul_kernel, tn=tn),
        grid=(m // tm,),
        in_specs=[pl.BlockSpec((tm, d), lambda i: (i, 0)),
                  pl.BlockSpec((1, d), lambda i: (0, 0)),
                  pl.BlockSpec((d, n), lambda i: (0, 0)),
                  pl.BlockSpec((1, n), lambda i: (0, 0))],
        out_specs=pl.BlockSpec((tm, n), lambda i: (i, 0)),
        out_shape=jax.ShapeDtypeStruct((m, n), _MXU),
        compiler_params=_params("parallel"),
        name="norm_in_proj",
    )(x, g, w, col_scale)


def _out_mlp_kernel(x_ref, a_ref, b_ref, c_ref, d_ref, wo_ref, g_ref, wu_ref, wd_ref, gf_ref, o_ref,
                    h_ref, acc_ref, *, final_norm):
    c = pl.program_id(1)

    @pl.when(c == 0)
    def _():
        y = x_ref[...]
        for g, ref in enumerate((a_ref, b_ref, c_ref, d_ref)):
            y = y + _dot(ref[...], wo_ref[g * GROUP_WIDTH:(g + 1) * GROUP_WIDTH, :])
        acc_ref[...] = y
        h_ref[...] = _rms(y, g_ref[...]).astype(h_ref.dtype)

    u = jnp.square(jnp.maximum(_dot(h_ref[...], wu_ref[...]), 0.0))
    acc_ref[...] += _dot(u.astype(_MXU), wd_ref[...])

    @pl.when(c == pl.num_programs(1) - 1)
    def _():
        y = acc_ref[...]
        if final_norm:
            y = _rms(y, gf_ref[...])
        o_ref[...] = y


def _out_mlp(x, groups, w_out, g, w_up, w_down, g_final, *, final_norm, tm=512, tf=2048):
    m, d = x.shape
    f = w_up.shape[1]
    gspec = pl.BlockSpec((tm, GROUP_WIDTH), lambda i, c: (i, 0))
    row = pl.BlockSpec((1, d), lambda i, c: (0, 0))
    return pl.pallas_call(
        functools.partial(_out_mlp_kernel, final_norm=final_norm),
        grid=(m // tm, f // tf),
        in_specs=[pl.BlockSpec((tm, d), lambda i, c: (i, 0)), gspec, gspec, gspec, gspec,
                  pl.BlockSpec((d, d), lambda i, c: (0, 0)), row,
                  pl.BlockSpec((d, tf), lambda i, c: (0, c)),
                  pl.BlockSpec((tf, d), lambda i, c: (c, 0)), row],
        out_specs=pl.BlockSpec((tm, d), lambda i, c: (i, 0)),
        out_shape=jax.ShapeDtypeStruct((m, d), jnp.float32),
        scratch_shapes=[pltpu.VMEM((tm, d), _MXU), pltpu.VMEM((tm, d), jnp.float32)],
        compiler_params=_params("parallel", "arbitrary"),
        name="out_proj_mlp_residual",
    )(x, *groups, w_out, g, w_up, w_down, g_final)


def _t5_bucket(dist):
    n = jnp.maximum(dist, 0)
    max_exact = N_BUCKETS // 2
    nf = jnp.maximum(n, 1).astype(jnp.float32)
    large = max_exact + (jnp.log(nf / max_exact) / math.log(MAX_DISTANCE / max_exact)
                         * (N_BUCKETS - max_exact)).astype(jnp.int32)
    large = jnp.minimum(large, N_BUCKETS - 1)
    return jnp.where(n < max_exact, n, large)


def _bias_kernel(tab_ref, o_ref, *, row_stride, col_stride, offset, head0):
    nh, tr, tc = o_ref.shape
    rows = lax.broadcasted_iota(jnp.int32, (tr, tc), 0)
    cols = lax.broadcasted_iota(jnp.int32, (tr, tc), 1) + pl.program_id(0) * tc
    bucket = _t5_bucket(rows * row_stride + cols * col_stride + offset)
    for h in range(nh):
        out = jnp.zeros((tr, tc), jnp.float32)
        for bkt in range(N_BUCKETS):
            out = jnp.where(bucket == bkt, tab_ref[bkt, head0 + h], out)
        o_ref[h] = (out - tab_ref[N_BUCKETS - 1, head0 + h]) * LOG2E


def _bias_table(table, *, n_heads, head0, rows, cols, col_tile, row_stride, col_stride, offset):
    return pl.pallas_call(
        functools.partial(_bias_kernel, row_stride=row_stride, col_stride=col_stride,
                          offset=offset, head0=head0),
        grid=(cols // col_tile,),
        in_specs=[pl.BlockSpec(memory_space=pltpu.SMEM)],
        out_specs=pl.BlockSpec((n_heads, rows, col_tile), lambda i: (0, 0, i)),
        out_shape=jax.ShapeDtypeStruct((n_heads, rows, cols), jnp.float32),
        compiler_params=_params("parallel"),
        name="t5_bias_tiles",
    )(table)


def _softmax_init(t):
    return (jnp.full((1, t), NEG_INF, jnp.float32), jnp.zeros((ACC_ROWS, t), jnp.float32))


class _KeyTile(NamedTuple):
    kts: Sequence[Any]
    vts: Sequence[Any]
    biases: Optional[Sequence[Any]] = None
    emasks: Optional[Sequence[Any]] = None
    qmasks: Optional[Sequence[Any]] = None


def _softmax_groups(n, t, qts, groups):
    built = {}

    def tiles_of(g):
        if g not in built:
            built[g] = [make() for make in groups[g]]
        return built[g]

    def scores_of(g, c):
        row = []
        for tile in tiles_of(g):
            s = _dot(tile.kts[c], qts[c])
            if tile.biases is not None:
                s = s + tile.biases[c]
            if tile.emasks is not None:
                s = jnp.where(tile.emasks[c], s, NEG_INF)
            row.append(s.astype(_MXU))
        return row

    def update(state, g, c, scores):
        m, acc = state
        m_new = m
        for tile, s in zip(tiles_of(g), scores):
            tile_max = jnp.max(s, axis=0, keepdims=True).astype(jnp.float32)
            if tile.qmasks is not None:
                tile_max = jnp.where(tile.qmasks[c], tile_max, NEG_INF)
            m_new = jnp.maximum(m_new, tile_max)
        seen = m_new > 0.5 * NEG_INF
        acc = jnp.exp2(m - m_new) * acc
        for tile, s in zip(tiles_of(g), scores):
            ok = seen if tile.qmasks is None else jnp.logical_and(seen, tile.qmasks[c])
            acc = acc + _dot(tile.vts[c], jnp.exp2(s - jnp.where(ok, m_new, BIG).astype(_MXU)))
        return m_new, acc

    units = [(g, c) for g in range(len(groups)) for c in range(n)]
    states = [_softmax_init(t) for _ in range(n)]
    pending = {k: scores_of(*units[k]) for k in range(min(LOOKAHEAD, len(units)))}
    for k, (g, c) in enumerate(units):
        if k + LOOKAHEAD < len(units):
            pending[k + LOOKAHEAD] = scores_of(*units[k + LOOKAHEAD])
        states[c] = update(states[c], g, c, pending.pop(k))
    return states


def _pairs(items):
    return [items[p:p + 2] for p in range(0, len(items), 2)]


def _per_query_tile(i, n_tiles, body):
    for k in range(n_tiles):
        pl.when(i == k)(functools.partial(body, k))


def _softmax_out(state):
    acc = state[1]
    return acc[:HEAD_DIM] / jnp.maximum(acc[HEAD_DIM:HEAD_DIM + 1], TINY)


def _top_k_rows(score, row_f, k):
    sel = jnp.zeros(score.shape, jnp.float32)
    for _ in range(k):
        mx = jnp.max(score, axis=0, keepdims=True)
        idx = jnp.min(jnp.where(score == mx, row_f, float(score.shape[0])), axis=0, keepdims=True)
        pick = row_f == idx
        sel = jnp.where(pick, 1.0, sel)
        score = jnp.where(pick, PICKED, score)
    return sel


def _tile_iotas(t):
    return (lax.broadcasted_iota(jnp.int32, (t, t), 0), lax.broadcasted_iota(jnp.int32, (t, t), 1))


def _key_rows(ref, j, t):
    if isinstance(j, int):
        return ref[j * t:(j + 1) * t, :]
    return ref[pl.ds(pl.multiple_of(j * t, t), t), :]


def _transposed(ref_block):
    return ref_block.astype(jnp.float32).T


def _queries_t(q_ref, width):
    qt = _transposed(q_ref[0])
    return [qt[c * width:(c + 1) * width].astype(_MXU) for c in range(GROUP_WIDTH // width)]


def _fill_values_t(vt_ref, v_ref, col0, t):
    n_tiles, rows, _ = vt_ref.shape
    lane_block = (col0 // LANES) * LANES
    for c in range(n_tiles):
        blk = _transposed(v_ref[0, c * t:(c + 1) * t, lane_block:lane_block + LANES])
        vt_ref[c, 0:HEAD_DIM, :] = blk[col0 - lane_block:col0 - lane_block + HEAD_DIM].astype(vt_ref.dtype)
        if rows == ACC_ROWS:
            first = lax.broadcasted_iota(jnp.int32, (rows - HEAD_DIM, t), 0) == 0
            vt_ref[c, HEAD_DIM:rows, :] = jnp.where(first, 1.0, 0.0).astype(vt_ref.dtype)


def _group_specs(t, s, cb_q, cb_k, cb_v):
    return [pl.BlockSpec((1, t, GROUP_WIDTH), lambda b, i: (b, i, cb_q)),
            pl.BlockSpec((1, s, GROUP_WIDTH), lambda b, i: (b, 0, cb_k)),
            pl.BlockSpec((1, s, GROUP_WIDTH), lambda b, i: (b, 0, cb_v))]


def _group_out(bsz, s, t):
    return dict(out_specs=pl.BlockSpec((1, t, GROUP_WIDTH), lambda b, i: (b, i, 0)),
                out_shape=jax.ShapeDtypeStruct((bsz, s, GROUP_WIDTH), _MXU))


def _bias_specs(t, head_group):
    spec = pl.BlockSpec((GROUP_HEADS, t, t), lambda b, i: (head_group, 0, 0))
    return [spec, spec]


def _store_heads(o_ref, outs_t):
    o_ref[0] = jnp.concatenate(outs_t, axis=0).T.astype(o_ref.dtype)


def _sb_kernel(q_ref, k_ref, v_ref, o_ref, kb_ref, vt_ref):
    t = q_ref.shape[1]
    i = pl.program_id(1)
    key, qry = _tile_iotas(t)
    strict = key < qry
    later = jnp.where(qry > key, 1.0, 0.0).astype(_MXU)
    later2 = jnp.concatenate([later, later], axis=1)

    @pl.when(i == 0)
    def _():
        for h in _HEADS:
            kb_ref[h] = k_ref[0, :, h * HEAD_DIM:(h + 1) * HEAD_DIM].astype(kb_ref.dtype)
            _fill_values_t(vt_ref.at[h], v_ref, h * HEAD_DIM, t)

    def query_tile(k):
        qts = _queries_t(q_ref, HEAD_DIM)
        units = [(pair, h) for pair in _pairs(list(range(k, -1, -1))) for h in _HEADS]
        carry = [(jnp.zeros((HEAD_DIM, t), jnp.float32), jnp.zeros((1, t), jnp.float32))] * GROUP_HEADS
        zs, log_keeps, suffixes = {}, {}, {}

        def scores(u):
            pair, h = units[u]
            zs[u] = [_dot(_key_rows(kb_ref.at[h], j, t), qts[h]) for j in pair]

        def keeps(u):
            pair, _ = units[u]
            log_keeps[u], suffixes[u] = [], []
            for j, z in zip(pair, zs[u]):
                neg = -z
                log_keep = jnp.minimum(neg, 0.0) - jnp.log2(1.0 + jnp.exp2(jnp.minimum(z, neg)))
                if j == k:
                    log_keep = jnp.where(strict, log_keep, 0.0)
                piece0 = log_keep.astype(_MXU)
                piece1 = (log_keep - piece0.astype(jnp.float32)).astype(_MXU)
                log_keeps[u].append(log_keep)
                suffixes[u].append(_dot(later2, jnp.concatenate([piece0, piece1], axis=0)))

        def values(u):
            pair, h = units[u]
            acc, run = carry[h]
            weights = []
            for j, z, log_keep, suffix in zip(pair, zs.pop(u), log_keeps.pop(u), suffixes.pop(u)):
                a = jnp.exp2(z + log_keep + suffix + run)
                if j == k:
                    a = jnp.where(strict, a, 0.0)
                weights.append(a.astype(_MXU))
                run = run + (suffix[0:1] + log_keep[0:1])
            for j, w in zip(pair, weights):
                acc = acc + _dot(vt_ref[h, j], w)
            carry[h] = (acc, run)

        stages = (scores, keeps, values)
        for step in range(len(units) + len(stages) - 1):
            for lag, stage in enumerate(stages):
                if 0 <= step - lag < len(units):
                    stage(step - lag)
        _store_heads(o_ref, [c[0] for c in carry])

    _per_query_tile(i, k_ref.shape[1] // t, query_tile)


def _stick_breaking(proj, t=TILE):
    bsz, s, _ = proj.shape
    return pl.pallas_call(
        _sb_kernel,
        grid=(bsz, s // t),
        in_specs=_group_specs(t, s, CB_SB_Q, CB_SB_K, CB_SB_V),
        scratch_shapes=[pltpu.VMEM((GROUP_HEADS, s, HEAD_DIM), _MXU),
                        pltpu.VMEM((GROUP_HEADS, s // t, HEAD_DIM, t), _MXU)],
        compiler_params=_params("parallel", "arbitrary"),
        name="stick_breaking",
        **_group_out(bsz, s, t),
    )(proj, proj, proj)


def _moba_kernel(q_ref, k_ref, v_ref, bd_ref, bs_ref, o_ref, kb_ref, vt_ref, km_ref):
    t = q_ref.shape[1]
    n_blk = k_ref.shape[1] // MOBA_BLOCK
    tiles_per_blk = MOBA_BLOCK // t
    blk_shift = int(math.log2(tiles_per_blk))
    i = pl.program_id(1)

    @pl.when(i == 0)
    def _():
        km_ref[...] = jnp.zeros_like(km_ref)
        for h in _HEADS:
            lo, hi = h * HEAD_DIM, (h + 1) * HEAD_DIM
            kb_ref[h] = k_ref[0, :, lo:hi].astype(kb_ref.dtype)
            _fill_values_t(vt_ref.at[h], v_ref, lo, t)
            for n in range(n_blk):
                blk = k_ref[0, n * MOBA_BLOCK:(n + 1) * MOBA_BLOCK, lo:hi]
                km_ref[h, n:n + 1, :] = jnp.mean(blk.astype(jnp.float32), axis=0, keepdims=True)

    def query_tile(k):
        own = k >> blk_shift
        key, qry = _tile_iotas(t)
        causal = key <= qry
        blk_row = lax.broadcasted_iota(jnp.int32, (km_ref.shape[1], t), 0)
        qt = _transposed(q_ref[0])
        qts, sels = [], []
        for h in _HEADS:
            qf = qt[h * HEAD_DIM:(h + 1) * HEAD_DIM]
            qts.append(qf.astype(_MXU))
            gate = _dot(km_ref[h], qf, precision=lax.Precision.HIGHEST)
            gate = jnp.where(blk_row < own, gate, NEG_INF)
            sel = _top_k_rows(gate, blk_row.astype(jnp.float32), min(MOBA_TOPK, n_blk - 1))
            sels.append(jnp.where(blk_row < own, sel, 0.0))

        def tile(j, bias_ref=None, emask=None):
            n = j >> blk_shift
            return _KeyTile([_key_rows(kb_ref.at[h], j, t) for h in _HEADS], [vt_ref[h, j] for h in _HEADS],
                            None if bias_ref is None else [bias_ref[h] for h in _HEADS],
                            None if emask is None else [emask] * GROUP_HEADS,
                            None if n == own else [sels[h][n:n + 1] > 0.5 for h in _HEADS])

        groups = _pairs([functools.partial(tile, j) for j in range(k - 1)])
        groups.append(([functools.partial(tile, k - 1, bs_ref)] if k else [])
                      + [functools.partial(tile, k, bd_ref, causal)])
        _store_heads(o_ref, [_softmax_out(st) for st in _softmax_groups(GROUP_HEADS, t, qts, groups)])

    _per_query_tile(i, k_ref.shape[1] // t, query_tile)


def _kv_scratch(s, t, key_dim=HEAD_DIM, n_keys=GROUP_HEADS):
    return [pltpu.VMEM((n_keys, s, key_dim), _MXU), pltpu.VMEM((GROUP_HEADS, s // t, ACC_ROWS, t), _MXU)]


def _moba(proj, bias_diag, bias_sub, t=TILE):
    bsz, s, _ = proj.shape
    n_blk_pad = -(-(s // MOBA_BLOCK) // SUBLANES) * SUBLANES
    return pl.pallas_call(
        _moba_kernel,
        grid=(bsz, s // t),
        in_specs=_group_specs(t, s, CB_MB_Q, CB_MB_K, CB_MB_V) + _bias_specs(t, 0),
        scratch_shapes=_kv_scratch(s, t) + [pltpu.VMEM((GROUP_HEADS, n_blk_pad, HEAD_DIM), jnp.float32)],
        compiler_params=_params("parallel", "arbitrary"),
        name="moba",
        **_group_out(bsz, s, t),
    )(proj, proj, proj, bias_diag, bias_sub)


def _diff_kernel(lam_ref, g_ref, q_ref, k_ref, v_ref, bd_ref, bs_ref, o_ref, kb_ref, vt_ref, *, lambda_init):
    t = q_ref.shape[1]
    i = pl.program_id(1)
    key, qry = _tile_iotas(t)
    causal = key <= qry
    lv = lam_ref[...]
    lam = (jnp.exp(jnp.sum(lv[0:1] * lv[1:2], keepdims=True))
           - jnp.exp(jnp.sum(lv[2:3] * lv[3:4], keepdims=True)) + lambda_init)
    halves = range(2 * GROUP_HEADS)

    @pl.when(i == 0)
    def _():
        for c in halves:
            kb_ref[c] = k_ref[0, :, c * DIFF_HALF:(c + 1) * DIFF_HALF].astype(kb_ref.dtype)
        for h in _HEADS:
            _fill_values_t(vt_ref.at[h], v_ref, h * HEAD_DIM, t)

    def tile(j, bias_ref=None, emask=None):
        n = len(halves)
        return _KeyTile([_key_rows(kb_ref.at[c], j, t) for c in halves], [vt_ref[c // 2, j] for c in halves],
                        None if bias_ref is None else [bias_ref[c // 2] for c in halves],
                        None if emask is None else [emask] * n)

    def query_tile(k):
        groups = _pairs([functools.partial(tile, j) for j in range(k - 1)])
        groups.append(([functools.partial(tile, k - 1, bs_ref)] if k else [])
                      + [functools.partial(tile, k, bd_ref, causal)])
        states = _softmax_groups(len(halves), t, _queries_t(q_ref, DIFF_HALF), groups)
        outs = []
        for h in _HEADS:
            o = _softmax_out(states[2 * h]) - lam * _softmax_out(states[2 * h + 1])
            o = o * lax.rsqrt(jnp.mean(o * o, axis=0, keepdims=True) + NORM_EPS) * g_ref[...]
            outs.append(o * (1.0 - lambda_init))
        _store_heads(o_ref, outs)

    _per_query_tile(i, k_ref.shape[1] // t, query_tile)


def _diff(proj, lam_params, subln, bias_diag, bias_sub, lambda_init, t=TILE):
    bsz, s, _ = proj.shape
    return pl.pallas_call(
        functools.partial(_diff_kernel, lambda_init=lambda_init),
        grid=(bsz, s // t),
        in_specs=[pl.BlockSpec(lam_params.shape, lambda b, i: (0, 0)),
                  pl.BlockSpec(subln.shape, lambda b, i: (0, 0))]
        + _group_specs(t, s, CB_DF_Q, CB_DF_K, CB_DF_V) + _bias_specs(t, 2),
        scratch_shapes=_kv_scratch(s, t, key_dim=DIFF_HALF, n_keys=2 * GROUP_HEADS),
        compiler_params=_params("parallel", "arbitrary"),
        name="diff_attention",
        **_group_out(bsz, s, t),
    )(lam_params, subln, proj, proj, proj, bias_diag, bias_sub)


def _compress_kernel(ck_ref, cv_ref, pk_ref, pv_ref, wk1_ref, wk2_ref, wv1_ref, wv2t_ref, kc_ref, vct_ref):
    half = ck_ref.shape[2]

    def hidden(c_ref, p_ref, w1_ref):
        chunks = c_ref[0]
        top = _dot((chunks + p_ref[:, :half]).astype(_MXU), w1_ref[:half, :])
        bot = _dot((chunks + p_ref[:, half:]).astype(_MXU), w1_ref[half:, :])
        pre = top + pltpu.roll(bot, bot.shape[0] - 1, axis=0)
        return jax.nn.gelu(pre).astype(_MXU)

    kc_ref[0] = _dot(hidden(ck_ref, pk_ref, wk1_ref), wk2_ref[...])
    vct_ref[0] = _dot_nt(wv2t_ref[...], hidden(cv_ref, pv_ref, wv1_ref))


def _compress(chunks_k, chunks_v, pos_k, pos_v, wk1, wk2, wv1, wv2t):
    bsz, n_chunk, width = chunks_k.shape
    cspec = pl.BlockSpec((1, n_chunk, width), lambda b: (b, 0, 0))

    def full(a):
        return pl.BlockSpec(a.shape, lambda b: (0, 0))

    return pl.pallas_call(
        _compress_kernel,
        grid=(bsz,),
        in_specs=[cspec, cspec, full(pos_k), full(pos_v), full(wk1), full(wk2), full(wv1), full(wv2t)],
        out_specs=[pl.BlockSpec((1, n_chunk, HEAD_DIM), lambda b: (b, 0, 0)),
                   pl.BlockSpec((1, HEAD_DIM, n_chunk), lambda b: (b, 0, 0))],
        out_shape=[jax.ShapeDtypeStruct((bsz, n_chunk, HEAD_DIM), jnp.float32),
                   jax.ShapeDtypeStruct((bsz, HEAD_DIM, n_chunk), jnp.float32)],
        compiler_params=_params("parallel"),
        name="nsa_compress",
    )(chunks_k, chunks_v, pos_k, pos_v, wk1, wk2, wv1, wv2t)


def _nsa_kernel(q_ref, kva_ref, kvb_ref, gq_ref, kc_ref, vct_ref, bd_ref, bs_ref, bc_ref, cover_ref, e_ref,
                o_ref, ks_ref, vst_ref, kw_ref, vwt_ref):
    t = q_ref.shape[1]
    i = pl.program_id(1)
    key, qry = _tile_iotas(t)
    causal = key <= qry
    ks_col, vs_col, kw_col, vw_col, gate_col = 2 * HEAD_DIM, 3 * HEAD_DIM, 0, HEAD_DIM, 2 * HEAD_DIM

    @pl.when(i == 0)
    def _():
        ks_ref[...] = kva_ref[0, :, ks_col:ks_col + HEAD_DIM].astype(ks_ref.dtype)
        kw_ref[...] = kvb_ref[0, :, kw_col:kw_col + HEAD_DIM].astype(kw_ref.dtype)
        _fill_values_t(vst_ref, kva_ref, vs_col, t)
        _fill_values_t(vwt_ref, kvb_ref, vw_col, t)

    def tile(k_ref, vt_ref, j, bias_ref=None, emask=None):
        n = GROUP_HEADS
        emask = emask() if callable(emask) else emask
        return _KeyTile([_key_rows(k_ref, j, t)] * n, [vt_ref[j]] * n,
                        None if bias_ref is None else [bias_ref[h] for h in _HEADS],
                        None if emask is None else [emask] * n)

    def query_tile(k):
        qts = _queries_t(q_ref, HEAD_DIM)

        n_cmp = kc_ref.shape[1]
        kc = kc_ref[0].astype(_MXU)
        vct = vct_ref[0].astype(_MXU)
        c_row = lax.broadcasted_iota(jnp.int32, (n_cmp, t), 0)
        c_qpos = lax.broadcasted_iota(jnp.int32, (n_cmp, t), 1) + k * t
        visible = c_qpos >= c_row * CMP_STRIDE + (CMP_LEN - 1)
        cmp_scores = [_dot(kc, qts[h]) for h in _HEADS]
        cmp_probs = []
        p_sum = jnp.zeros((n_cmp, t), jnp.float32)
        for h in _HEADS:
            sc = jnp.where(visible, cmp_scores[h] + bc_ref[h], NEG_INF)
            e = jnp.where(visible, jnp.exp2(sc - jnp.max(sc, axis=0, keepdims=True)), 0.0)
            p = e / jnp.maximum(jnp.sum(e, axis=0, keepdims=True), TINY)
            cmp_probs.append(p.astype(_MXU))
            p_sum = p_sum + p
        o_cmp = [_dot(vct, cmp_probs[h]) for h in _HEADS]

        n_slc = cover_ref.shape[0]
        importance = _dot(cover_ref[...], p_sum, precision=lax.Precision.HIGHEST)

        n_back = WINDOW // t
        window = [functools.partial(tile, kw_ref, vwt_ref, k - n_back, None, qry < key)] if k >= n_back else []
        for back in range(min(n_back - 1, k), 0, -1):
            window.append(functools.partial(tile, kw_ref, vwt_ref, k - back, bs_ref if back == 1 else None))
        window.append(functools.partial(tile, kw_ref, vwt_ref, k, bd_ref, causal))
        o_win = [_softmax_out(st) for st in _softmax_groups(GROUP_HEADS, t, qts, [window])]

        s_row = lax.broadcasted_iota(jnp.int32, (n_slc, t), 0)
        own = jnp.right_shift(lax.broadcasted_iota(jnp.int32, (n_slc, t), 1) + k * t, int(math.log2(SLC_LEN)))
        score = jnp.where(s_row == own, FORCE, jnp.where(s_row < own, importance, NEG_INF))
        sel = _top_k_rows(score, s_row.astype(jnp.float32), min(SLC_TOPN, n_slc)).astype(_MXU)

        def chosen(j):
            return _dot(_key_rows(e_ref, j, t), sel) > 0.5

        groups = _pairs([functools.partial(tile, ks_ref, vst_ref, j, None, functools.partial(chosen, j))
                         for j in range(k - 1)])
        groups.append(([functools.partial(tile, ks_ref, vst_ref, k - 1, bs_ref, functools.partial(chosen, k - 1))]
                       if k else [])
                      + [functools.partial(tile, ks_ref, vst_ref, k, bd_ref,
                                           lambda: jnp.logical_and(chosen(k), causal))])
        o_slc = [_softmax_out(st) for st in _softmax_groups(GROUP_HEADS, t, qts, groups)]

        gates = _transposed(gq_ref[0, :, (gate_col // LANES) * LANES:(gate_col // LANES + 1) * LANES])
        gates = 1.0 / (1.0 + jnp.exp(-gates[gate_col % LANES:gate_col % LANES + N_GATES + 4]))
        outs = []
        for h in _HEADS:
            g = [gates[br * GROUP_HEADS + h:br * GROUP_HEADS + h + 1] for br in range(3)]
            outs.append(g[0] * o_cmp[h] + g[1] * o_slc[h] + g[2] * o_win[h])
        _store_heads(o_ref, outs)

    _per_query_tile(i, kva_ref.shape[1] // t, query_tile)


def _nsa(proj, kc, vc_t, bias_diag, bias_sub, bias_cmp, cover_t, expand, t=TILE):
    bsz, s, _ = proj.shape
    n_cmp = kc.shape[1]
    kv_scratch = [pltpu.VMEM((s, HEAD_DIM), _MXU), pltpu.VMEM((s // t, ACC_ROWS, t), _MXU)]
    return pl.pallas_call(
        _nsa_kernel,
        grid=(bsz, s // t),
        in_specs=_group_specs(t, s, CB_NS_Q, CB_NS_A, CB_NS_B)
        + [pl.BlockSpec((1, t, GROUP_WIDTH), lambda b, i: (b, i, CB_NS_B)),
           pl.BlockSpec((1, n_cmp, HEAD_DIM), lambda b, i: (b, 0, 0)),
           pl.BlockSpec((1, HEAD_DIM, n_cmp), lambda b, i: (b, 0, 0))]
        + _bias_specs(t, 1)
        + [pl.BlockSpec((GROUP_HEADS, n_cmp, t), lambda b, i: (0, 0, i)),
           pl.BlockSpec(cover_t.shape, lambda b, i: (0, 0)),
           pl.BlockSpec(expand.shape, lambda b, i: (0, 0))],
        scratch_shapes=kv_scratch + kv_scratch,
        compiler_params=_params("parallel", "arbitrary"),
        name="nsa",
        **_group_out(bsz, s, t),
    )(proj, proj, proj, proj, kc, vc_t, bias_diag, bias_sub, bias_cmp, cover_t, expand)


def _nsa_constants(s):
    n_cmp = (s - CMP_LEN) // CMP_STRIDE + 1
    n_slc = s // SLC_LEN
    assert n_cmp + 1 == s // CMP_STRIDE and n_slc % SUBLANES == 0
    c_start = np.arange(n_cmp) * CMP_STRIDE
    s_start = np.arange(n_slc) * SLC_LEN
    cover = np.clip(np.minimum((c_start + CMP_LEN - 1)[:, None], (s_start + SLC_LEN - 1)[None, :])
                    - np.maximum(c_start[:, None], s_start[None, :]) + 1, 0, None) / CMP_LEN
    cover_t = np.zeros((n_slc, n_cmp + 1), np.float32)
    cover_t[:, :n_cmp] = cover.T
    expand = (np.arange(s)[:, None] // SLC_LEN == np.arange(n_slc)[None, :]).astype(np.float32)
    return jnp.asarray(cover_t), jnp.asarray(expand, _MXU)


def kernel(x, w_in, w_out, w_up, w_down, norm_attn, norm_mlp, cmp_pos_k, cmp_pos_v, cmp_k_w1, cmp_k_w2,
           cmp_v_w1, cmp_v_w2, diff_lambda, diff_subln, rel_bias, final_norm):
    bsz, s, d = x.shape
    depth = w_in.shape[0]
    t = TILE
    n_chunk = s // CMP_STRIDE
    assert s % MOBA_BLOCK == 0 and MOBA_BLOCK % t == 0 and WINDOW % t == 0 and t >= MAX_DISTANCE

    w_in_p = jnp.concatenate([w_in[:, :, :COLS_BEFORE_PAD],
                              jnp.zeros((depth, d, PAD_COLS), w_in.dtype),
                              w_in[:, :, COLS_BEFORE_PAD:]], axis=2).astype(_MXU)
    assert w_in_p.shape[2] == D_IN_PAD
    w_out_c, w_up_c, w_down_c = w_out.astype(_MXU), w_up.astype(_MXU), w_down.astype(_MXU)
    wk1, wk2 = cmp_k_w1.astype(_MXU), cmp_k_w2.astype(_MXU)
    wv1, wv2t = cmp_v_w1.astype(_MXU), jnp.swapaxes(cmp_v_w2, 1, 2).astype(_MXU)

    tiles = dict(n_heads=rel_bias.shape[1], head0=0, rows=t, cols=t, col_tile=t, row_stride=-1, col_stride=1)
    bias_diag = _bias_table(rel_bias, offset=0, **tiles)
    bias_sub = _bias_table(rel_bias, offset=t, **tiles)
    bias_cmp = _bias_table(rel_bias, n_heads=GROUP_HEADS, head0=GROUP_HEADS, rows=n_chunk, cols=s,
                           col_tile=t, row_stride=-CMP_STRIDE, col_stride=1, offset=-(CMP_LEN - 1))
    cover_t, expand = _nsa_constants(s)
    col_scale = np.ones((1, D_IN_PAD), np.float32)
    for cb, width in ((CB_SB_Q, HEAD_DIM), (CB_MB_Q, HEAD_DIM), (CB_NS_Q, HEAD_DIM), (CB_DF_Q, DIFF_HALF)):
        col_scale[:, cb * GROUP_WIDTH:(cb + 1) * GROUP_WIDTH] = width ** -0.5 * LOG2E
    col_scale = jnp.asarray(col_scale)

    x2 = x.reshape(bsz * s, d)
    for layer in range(depth):
        proj = _norm_matmul(x2, norm_attn[layer][None], w_in_p[layer], col_scale).reshape(bsz, s, D_IN_PAD)
        o_sb = _stick_breaking(proj)
        o_mb = _moba(proj, bias_diag, bias_sub)
        col0 = CB_NS_A * GROUP_WIDTH
        chunks_k = proj[:, :, col0:col0 + HEAD_DIM].reshape(bsz, n_chunk, CMP_STRIDE * HEAD_DIM)
        chunks_v = proj[:, :, col0 + HEAD_DIM:col0 + 2 * HEAD_DIM].reshape(bsz, n_chunk, CMP_STRIDE * HEAD_DIM)
        kc, vc_t = _compress(chunks_k, chunks_v, cmp_pos_k[layer].reshape(1, -1), cmp_pos_v[layer].reshape(1, -1),
                             wk1[layer], wk2[layer], wv1[layer], wv2t[layer])
        o_ns = _nsa(proj, kc, vc_t, bias_diag, bias_sub, bias_cmp, cover_t, expand)
        lambda_init = 0.8 - 0.6 * math.exp(-0.3 * layer)
        o_df = _diff(proj, diff_lambda[layer], diff_subln[layer][:, None], bias_diag, bias_sub, lambda_init)
        groups = [o.reshape(bsz * s, GROUP_WIDTH) for o in (o_sb, o_mb, o_ns, o_df)]
        x2 = _out_mlp(x2, groups, w_out_c[layer], norm_mlp[layer][None], w_up_c[layer], w_down_c[layer],
                      final_norm[None], final_norm=(layer == depth - 1))
    return x2.reshape(bsz, s, d)
```

```python
import functools
import math
from typing import Any, NamedTuple, Optional, Sequence

import numpy as np
import jax
import jax.numpy as jnp
from jax import lax
from jax.experimental import pallas as pl
from jax.experimental.pallas import tpu as pltpu

HEAD_DIM = 64
GROUP_HEADS = 4
GROUP_WIDTH = GROUP_HEADS * HEAD_DIM
NORM_EPS = 1e-6
NEG_INF = -1e30
BIG = 1e30
FORCE = 1e30
TINY = 1e-30
PICKED = -3e38
LOG2E = math.log2(math.e)
N_BUCKETS = 32
MAX_DISTANCE = 128
MOBA_BLOCK = 256
MOBA_TOPK = 3
CMP_LEN = 32
CMP_STRIDE = 16
SLC_LEN = 64
SLC_TOPN = 4
WINDOW = 512
DIFF_HALF = HEAD_DIM // 2
LANES = 128
SUBLANES = 8
TILE = 256
BF16_ROWS = 16
ACC_ROWS = HEAD_DIM + BF16_ROWS
N_GATES = 3 * GROUP_HEADS
COLS_BEFORE_PAD = 9 * GROUP_WIDTH - 2 * HEAD_DIM + N_GATES
PAD_COLS = 2 * HEAD_DIM - N_GATES
CB_SB_Q, CB_SB_K, CB_SB_V, CB_MB_Q, CB_MB_K, CB_MB_V, CB_NS_Q, CB_NS_A, CB_NS_B, CB_DF_Q, CB_DF_K, CB_DF_V = range(12)
D_IN_PAD = 12 * GROUP_WIDTH

_MXU = jnp.bfloat16
_VMEM_LIMIT = 56 * 1024 * 1024
_HEADS = range(GROUP_HEADS)
Q_TILES = 2
LOOKAHEAD = 2


def _dot(a, b, precision=None):
    return jnp.dot(a, b, precision=precision, preferred_element_type=jnp.float32)


def _dot_nt(a, b, precision=None):
    return lax.dot_general(a, b, (((1,), (1,)), ((), ())), precision=precision,
                           preferred_element_type=jnp.float32)


def _rms(x, g):
    return x * lax.rsqrt(jnp.mean(x * x, axis=-1, keepdims=True) + NORM_EPS) * g


def _params(*sem):
    return pltpu.CompilerParams(dimension_semantics=sem, vmem_limit_bytes=_VMEM_LIMIT)


def _norm_matmul_kernel(x_ref, g_ref, w_ref, scale_ref, o_ref, *, tn):
    h = _rms(x_ref[...], g_ref[...]).astype(_MXU)
    for j in range(w_ref.shape[1] // tn):
        cols = slice(j * tn, (j + 1) * tn)
        o_ref[:, cols] = (_dot(h, w_ref[:, cols]) * scale_ref[:, cols]).astype(o_ref.dtype)


def _norm_matmul(x, g, w, col_scale, *, tm=512, tn=1024):
    m, d = x.shape
    n = w.shape[1]
    return pl.pallas_call(
        functools.partial(_norm_matmul_kernel, tn=tn),
        grid=(m // tm,),
        in_specs=[pl.BlockSpec((tm, d), lambda i: (i, 0)),
                  pl.BlockSpec((1, d), lambda i: (0, 0)),
                  pl.BlockSpec((d, n), lambda i: (0, 0)),
                  pl.BlockSpec((1, n), lambda i: (0, 0))],
        out_specs=pl.BlockSpec((tm, n), lambda i: (i, 0)),
        out_shape=jax.ShapeDtypeStruct((m, n), _MXU),
        compiler_params=_params("parallel"),
        name="norm_in_proj",
    )(x, g, w, col_scale)


def _out_mlp_kernel(x_ref, a_ref, b_ref, c_ref, d_ref, wo_ref, g_ref, wu_ref, wd_ref, gf_ref, o_ref,
                    h_ref, acc_ref, *, final_norm):
    c = pl.program_id(1)

    @pl.when(c == 0)
    def _():
        y = x_ref[...]
        for g, ref in enumerate((a_ref, b_ref, c_ref, d_ref)):
            y = y + _dot(ref[...], wo_ref[g * GROUP_WIDTH:(g + 1) * GROUP_WIDTH, :])
        acc_ref[...] = y
        h_ref[...] = _rms(y, g_ref[...]).astype(h_ref.dtype)

    u = jnp.square(jnp.maximum(_dot(h_ref[...], wu_ref[...]), 0.0))
    acc_ref[...] += _dot(u.astype(_MXU), wd_ref[...])

    @pl.when(c == pl.num_programs(1) - 1)
    def _():
        y = acc_ref[...]
        if final_norm:
            y = _rms(y, gf_ref[...])
        o_ref[...] = y


def _out_mlp(x, groups, w_out, g, w_up, w_down, g_final, *, final_norm, tm=512, tf=2048):
    m, d = x.shape
    f = w_up.shape[1]
    gspec = pl.BlockSpec((tm, GROUP_WIDTH), lambda i, c: (i, 0))
    row = pl.BlockSpec((1, d), lambda i, c: (0, 0))
    return pl.pallas_call(
        functools.partial(_out_mlp_kernel, final_norm=final_norm),
        grid=(m // tm, f // tf),
        in_specs=[pl.BlockSpec((tm, d), lambda i, c: (i, 0)), gspec, gspec, gspec, gspec,
                  pl.BlockSpec((d, d), lambda i, c: (0, 0)), row,
                  pl.BlockSpec((d, tf), lambda i, c: (0, c)),
                  pl.BlockSpec((tf, d), lambda i, c: (c, 0)), row],
        out_specs=pl.BlockSpec((tm, d), lambda i, c: (i, 0)),
        out_shape=jax.ShapeDtypeStruct((m, d), jnp.float32),
        scratch_shapes=[pltpu.VMEM((tm, d), _MXU), pltpu.VMEM((tm, d), jnp.float32)],
        compiler_params=_params("parallel", "arbitrary"),
        name="out_proj_mlp_residual",
    )(x, *groups, w_out, g, w_up, w_down, g_final)


def _t5_bucket(dist):
    n = jnp.maximum(dist, 0)
    max_exact = N_BUCKETS // 2
    nf = jnp.maximum(n, 1).astype(jnp.float32)
    large = max_exact + (jnp.log(nf / max_exact) / math.log(MAX_DISTANCE / max_exact)
                         * (N_BUCKETS - max_exact)).astype(jnp.int32)
    large = jnp.minimum(large, N_BUCKETS - 1)
    return jnp.where(n < max_exact, n, large)


def _bias_kernel(tab_ref, o_ref, *, row_stride, col_stride, offset, head0):
    nh, tr, tc = o_ref.shape
    rows = lax.broadcasted_iota(jnp.int32, (tr, tc), 0)
    cols = lax.broadcasted_iota(jnp.int32, (tr, tc), 1) + pl.program_id(0) * tc
    bucket = _t5_bucket(rows * row_stride + cols * col_stride + offset)
    for h in range(nh):
        out = jnp.zeros((tr, tc), jnp.float32)
        for bkt in range(N_BUCKETS):
            out = jnp.where(bucket == bkt, tab_ref[bkt, head0 + h], out)
        o_ref[h] = (out - tab_ref[N_BUCKETS - 1, head0 + h]) * LOG2E


def _bias_table(table, *, n_heads, head0, rows, cols, col_tile, row_stride, col_stride, offset):
    return pl.pallas_call(
        functools.partial(_bias_kernel, row_stride=row_stride, col_stride=col_stride,
                          offset=offset, head0=head0),
        grid=(cols // col_tile,),
        in_specs=[pl.BlockSpec(memory_space=pltpu.SMEM)],
        out_specs=pl.BlockSpec((n_heads, rows, col_tile), lambda i: (0, 0, i)),
        out_shape=jax.ShapeDtypeStruct((n_heads, rows, cols), jnp.float32),
        compiler_params=_params("parallel"),
        name="t5_bias_tiles",
    )(table)


def _softmax_init(t):
    return (jnp.full((1, t), NEG_INF, jnp.float32), jnp.zeros((ACC_ROWS, t), jnp.float32))


class _KeyTile(NamedTuple):
    kts: Sequence[Any]
    vts: Sequence[Any]
    biases: Optional[Sequence[Any]] = None
    emasks: Optional[Sequence[Any]] = None
    qmasks: Optional[Sequence[Any]] = None


def _round_robin(lists):
    out = []
    for rank in range(max(map(len, lists), default=0)):
        out.extend(items[rank] for items in lists if rank < len(items))
    return out


def _softmax_jobs(t, jobs):
    built = {}

    def tiles_of(job, g):
        if (job, g) not in built:
            built[job, g] = [make() for make in jobs[job][2][g]]
        return built[job, g]

    def scores_of(job, g, c):
        qts = jobs[job][1]
        row = []
        for tile in tiles_of(job, g):
            s = _dot(tile.kts[c], qts[c])
            if tile.biases is not None:
                s = s + tile.biases[c]
            if tile.emasks is not None:
                s = jnp.where(tile.emasks[c], s, NEG_INF)
            row.append(s.astype(_MXU))
        return row

    def update(state, job, g, c, scores):
        m, acc = state
        m_new = m
        for tile, s in zip(tiles_of(job, g), scores):
            tile_max = jnp.max(s, axis=0, keepdims=True).astype(jnp.float32)
            if tile.qmasks is not None:
                tile_max = jnp.where(tile.qmasks[c], tile_max, NEG_INF)
            m_new = jnp.maximum(m_new, tile_max)
        seen = m_new > 0.5 * NEG_INF
        acc = jnp.exp2(m - m_new) * acc
        for tile, s in zip(tiles_of(job, g), scores):
            ok = seen if tile.qmasks is None else jnp.logical_and(seen, tile.qmasks[c])
            acc = acc + _dot(tile.vts[c], jnp.exp2(s - jnp.where(ok, m_new, BIG).astype(_MXU)))
        return m_new, acc

    units = _round_robin([[(job, g, c) for g in range(len(groups)) for c in range(n)]
                          for job, (n, _, groups) in enumerate(jobs)])
    lookahead = LOOKAHEAD * len(jobs)
    states = [[_softmax_init(t) for _ in range(n)] for n, _, _ in jobs]
    pending = {k: scores_of(*units[k]) for k in range(min(lookahead, len(units)))}
    for k, (job, g, c) in enumerate(units):
        if k + lookahead < len(units):
            pending[k + lookahead] = scores_of(*units[k + lookahead])
        states[job][c] = update(states[job][c], job, g, c, pending.pop(k))
    return states


def _pairs(items):
    return [items[p:p + 2] for p in range(0, len(items), 2)]


def _per_step(i, n_steps, body):
    for step in range(n_steps):
        pl.when(i == step)(functools.partial(body, step))


def _softmax_out(state):
    acc = state[1]
    return acc[:HEAD_DIM] / jnp.maximum(acc[HEAD_DIM:HEAD_DIM + 1], TINY)


def _top_k_rows(score, row_f, k):
    sel = jnp.zeros(score.shape, jnp.float32)
    for _ in range(k):
        mx = jnp.max(score, axis=0, keepdims=True)
        idx = jnp.min(jnp.where(score == mx, row_f, float(score.shape[0])), axis=0, keepdims=True)
        pick = row_f == idx
        sel = jnp.where(pick, 1.0, sel)
        score = jnp.where(pick, PICKED, score)
    return sel


def _tile_iotas(t):
    return (lax.broadcasted_iota(jnp.int32, (t, t), 0), lax.broadcasted_iota(jnp.int32, (t, t), 1))


def _key_rows(ref, j, t):
    if isinstance(j, int):
        return ref[j * t:(j + 1) * t, :]
    return ref[pl.ds(pl.multiple_of(j * t, t), t), :]


def _transposed(ref_block):
    return ref_block.astype(jnp.float32).T


def _queries_t(q_ref, width, t):
    qt = _transposed(q_ref[0]).astype(_MXU)
    return [[qt[c * width:(c + 1) * width, r * t:(r + 1) * t] for c in range(GROUP_WIDTH // width)]
            for r in range(Q_TILES)]


def _fill_values_t(vt_ref, v_ref, col0, t):
    n_tiles, rows, _ = vt_ref.shape
    lane_block = (col0 // LANES) * LANES
    for c in range(n_tiles):
        blk = _transposed(v_ref[0, c * t:(c + 1) * t, lane_block:lane_block + LANES])
        vt_ref[c, 0:HEAD_DIM, :] = blk[col0 - lane_block:col0 - lane_block + HEAD_DIM].astype(vt_ref.dtype)
        if rows == ACC_ROWS:
            first = lax.broadcasted_iota(jnp.int32, (rows - HEAD_DIM, t), 0) == 0
            vt_ref[c, HEAD_DIM:rows, :] = jnp.where(first, 1.0, 0.0).astype(vt_ref.dtype)


def _group_specs(t, s, cb_q, cb_k, cb_v):
    return [pl.BlockSpec((1, Q_TILES * t, GROUP_WIDTH), lambda b, i: (b, i, cb_q)),
            pl.BlockSpec((1, s, GROUP_WIDTH), lambda b, i: (b, 0, cb_k)),
            pl.BlockSpec((1, s, GROUP_WIDTH), lambda b, i: (b, 0, cb_v))]


def _group_out(bsz, s, t):
    return dict(out_specs=pl.BlockSpec((1, Q_TILES * t, GROUP_WIDTH), lambda b, i: (b, i, 0)),
                out_shape=jax.ShapeDtypeStruct((bsz, s, GROUP_WIDTH), _MXU))


def _bias_specs(t, head_group):
    spec = pl.BlockSpec((GROUP_HEADS, t, t), lambda b, i: (head_group, 0, 0))
    return [spec, spec]


def _store_heads(o_ref, outs_t):
    tiles = [jnp.concatenate(heads, axis=0) for heads in outs_t]
    o_ref[0] = jnp.concatenate(tiles, axis=1).T.astype(o_ref.dtype)


def _sb_kernel(q_ref, k_ref, v_ref, o_ref, kb_ref, vt_ref):
    t = q_ref.shape[1] // Q_TILES
    i = pl.program_id(1)
    key, qry = _tile_iotas(t)
    strict = key < qry
    later = jnp.where(qry > key, 1.0, 0.0).astype(_MXU)

    @pl.when(i == 0)
    def _():
        for h in _HEADS:
            kb_ref[h] = k_ref[0, :, h * HEAD_DIM:(h + 1) * HEAD_DIM].astype(kb_ref.dtype)
            _fill_values_t(vt_ref.at[h], v_ref, h * HEAD_DIM, t)

    def query_step(step):
        qts = _queries_t(q_ref, HEAD_DIM, t)
        units = _round_robin([[(r, step * Q_TILES + r, pair, h)
                               for pair in _pairs(list(range(step * Q_TILES + r, -1, -1))) for h in _HEADS]
                              for r in range(Q_TILES)])
        zero = (jnp.zeros((HEAD_DIM, t), jnp.float32), jnp.zeros((1, t), jnp.float32))
        carry = [[zero] * GROUP_HEADS for _ in range(Q_TILES)]
        zs, log_keeps, suffixes = {}, {}, {}

        def scores(u):
            r, _, pair, h = units[u]
            zs[u] = [_dot(_key_rows(kb_ref.at[h], j, t), qts[r][h]) for j in pair]

        def keeps(u):
            _, k, pair, _ = units[u]
            log_keeps[u], suffixes[u] = [], []
            for j, z in zip(pair, zs[u]):
                neg = -z
                log_keep = jnp.minimum(neg, 0.0) - jnp.log2(1.0 + jnp.exp2(jnp.minimum(z, neg)))
                if j == k:
                    log_keep = jnp.where(strict, log_keep, 0.0)
                log_keeps[u].append(log_keep)
                suffixes[u].append(_dot(later, log_keep.astype(_MXU)))

        def values(u):
            r, k, pair, h = units[u]
            acc, run = carry[r][h]
            weights = []
            for j, z, log_keep, suffix in zip(pair, zs.pop(u), log_keeps.pop(u), suffixes.pop(u)):
                a = jnp.exp2(z + log_keep + suffix + run)
                if j == k:
                    a = jnp.where(strict, a, 0.0)
                weights.append(a.astype(_MXU))
                run = run + (suffix[0:1] + log_keep[0:1])
            for j, w in zip(pair, weights):
                acc = acc + _dot(vt_ref[h, j], w)
            carry[r][h] = (acc, run)

        stages = (scores, keeps, values)
        for tick in range(len(units) + len(stages) - 1):
            for lag, stage in enumerate(stages):
                if 0 <= tick - lag < len(units):
                    stage(tick - lag)
        _store_heads(o_ref, [[c[0] for c in tile_carry] for tile_carry in carry])

    _per_step(i, k_ref.shape[1] // (Q_TILES * t), query_step)


def _stick_breaking(proj, t=TILE):
    bsz, s, _ = proj.shape
    return pl.pallas_call(
        _sb_kernel,
        grid=(bsz, s // (Q_TILES * t)),
        in_specs=_group_specs(t, s, CB_SB_Q, CB_SB_K, CB_SB_V),
        scratch_shapes=[pltpu.VMEM((GROUP_HEADS, s, HEAD_DIM), _MXU),
                        pltpu.VMEM((GROUP_HEADS, s // t, HEAD_DIM, t), _MXU)],
        compiler_params=_params("parallel", "arbitrary"),
        name="stick_breaking",
        **_group_out(bsz, s, t),
    )(proj, proj, proj)


def _moba_kernel(q_ref, k_ref, v_ref, bd_ref, bs_ref, o_ref, kb_ref, vt_ref, km_ref):
    t = q_ref.shape[1] // Q_TILES
    n_blk = k_ref.shape[1] // MOBA_BLOCK
    tiles_per_blk = MOBA_BLOCK // t
    blk_shift = int(math.log2(tiles_per_blk))
    i = pl.program_id(1)

    @pl.when(i == 0)
    def _():
        km_ref[...] = jnp.zeros_like(km_ref)
        for h in _HEADS:
            lo, hi = h * HEAD_DIM, (h + 1) * HEAD_DIM
            kb_ref[h] = k_ref[0, :, lo:hi].astype(kb_ref.dtype)
            _fill_values_t(vt_ref.at[h], v_ref, lo, t)
            for n in range(n_blk):
                blk = k_ref[0, n * MOBA_BLOCK:(n + 1) * MOBA_BLOCK, lo:hi]
                km_ref[h, n:n + 1, :] = jnp.mean(blk.astype(jnp.float32), axis=0, keepdims=True)

    def query_step(step):
        key, qry = _tile_iotas(t)
        causal = key <= qry
        blk_row = lax.broadcasted_iota(jnp.int32, (km_ref.shape[1], t), 0)
        qt = _transposed(q_ref[0])
        jobs = []
        for r in range(Q_TILES):
            k = step * Q_TILES + r
            own = k >> blk_shift
            qts, sels = [], []
            for h in _HEADS:
                qf = qt[h * HEAD_DIM:(h + 1) * HEAD_DIM, r * t:(r + 1) * t]
                qts.append(qf.astype(_MXU))
                gate = _dot(km_ref[h], qf, precision=lax.Precision.HIGHEST)
                gate = jnp.where(blk_row < own, gate, NEG_INF)
                sel = _top_k_rows(gate, blk_row.astype(jnp.float32), min(MOBA_TOPK, n_blk - 1))
                sels.append(jnp.where(blk_row < own, sel, 0.0))

            def tile(own, sels, j, bias_ref=None, emask=None):
                n = j >> blk_shift
                return _KeyTile([_key_rows(kb_ref.at[h], j, t) for h in _HEADS], [vt_ref[h, j] for h in _HEADS],
                                None if bias_ref is None else [bias_ref[h] for h in _HEADS],
                                None if emask is None else [emask] * GROUP_HEADS,
                                None if n == own else [sels[h][n:n + 1] > 0.5 for h in _HEADS])

            tile = functools.partial(tile, own, sels)
            groups = _pairs([functools.partial(tile, j) for j in range(k - 1)])
            groups.append(([functools.partial(tile, k - 1, bs_ref)] if k else [])
                          + [functools.partial(tile, k, bd_ref, causal)])
            jobs.append((GROUP_HEADS, qts, groups))
        _store_heads(o_ref, [[_softmax_out(st) for st in states] for states in _softmax_jobs(t, jobs)])

    _per_step(i, k_ref.shape[1] // (Q_TILES * t), query_step)


def _kv_scratch(s, t, key_dim=HEAD_DIM, n_keys=GROUP_HEADS):
    return [pltpu.VMEM((n_keys, s, key_dim), _MXU), pltpu.VMEM((GROUP_HEADS, s // t, ACC_ROWS, t), _MXU)]


def _moba(proj, bias_diag, bias_sub, t=TILE):
    bsz, s, _ = proj.shape
    n_blk_pad = -(-(s // MOBA_BLOCK) // SUBLANES) * SUBLANES
    return pl.pallas_call(
        _moba_kernel,
        grid=(bsz, s // (Q_TILES * t)),
        in_specs=_group_specs(t, s, CB_MB_Q, CB_MB_K, CB_MB_V) + _bias_specs(t, 0),
        scratch_shapes=_kv_scratch(s, t) + [pltpu.VMEM((GROUP_HEADS, n_blk_pad, HEAD_DIM), jnp.float32)],
        compiler_params=_params("parallel", "arbitrary"),
        name="moba",
        **_group_out(bsz, s, t),
    )(proj, proj, proj, bias_diag, bias_sub)


def _diff_kernel(lam_ref, g_ref, q_ref, k_ref, v_ref, bd_ref, bs_ref, o_ref, kb_ref, vt_ref, *, lambda_init):
    t = q_ref.shape[1] // Q_TILES
    i = pl.program_id(1)
    key, qry = _tile_iotas(t)
    causal = key <= qry
    lv = lam_ref[...]
    lam = (jnp.exp(jnp.sum(lv[0:1] * lv[1:2], keepdims=True))
           - jnp.exp(jnp.sum(lv[2:3] * lv[3:4], keepdims=True)) + lambda_init)
    halves = range(2 * GROUP_HEADS)

    @pl.when(i == 0)
    def _():
        for c in halves:
            kb_ref[c] = k_ref[0, :, c * DIFF_HALF:(c + 1) * DIFF_HALF].astype(kb_ref.dtype)
        for h in _HEADS:
            _fill_values_t(vt_ref.at[h], v_ref, h * HEAD_DIM, t)

    def tile(j, bias_ref=None, emask=None):
        n = len(halves)
        return _KeyTile([_key_rows(kb_ref.at[c], j, t) for c in halves], [vt_ref[c // 2, j] for c in halves],
                        None if bias_ref is None else [bias_ref[c // 2] for c in halves],
                        None if emask is None else [emask] * n)

    def query_step(step):
        qts = _queries_t(q_ref, DIFF_HALF, t)
        jobs = []
        for r in range(Q_TILES):
            k = step * Q_TILES + r
            groups = _pairs([functools.partial(tile, j) for j in range(k - 1)])
            groups.append(([functools.partial(tile, k - 1, bs_ref)] if k else [])
                          + [functools.partial(tile, k, bd_ref, causal)])
            jobs.append((len(halves), qts[r], groups))
        outs = []
        for states in _softmax_jobs(t, jobs):
            heads = []
            for h in _HEADS:
                o = _softmax_out(states[2 * h]) - lam * _softmax_out(states[2 * h + 1])
                o = o * lax.rsqrt(jnp.mean(o * o, axis=0, keepdims=True) + NORM_EPS) * g_ref[...]
                heads.append(o * (1.0 - lambda_init))
            outs.append(heads)
        _store_heads(o_ref, outs)

    _per_step(i, k_ref.shape[1] // (Q_TILES * t), query_step)


def _diff(proj, lam_params, subln, bias_diag, bias_sub, lambda_init, t=TILE):
    bsz, s, _ = proj.shape
    return pl.pallas_call(
        functools.partial(_diff_kernel, lambda_init=lambda_init),
        grid=(bsz, s // (Q_TILES * t)),
        in_specs=[pl.BlockSpec(lam_params.shape, lambda b, i: (0, 0)),
                  pl.BlockSpec(subln.shape, lambda b, i: (0, 0))]
        + _group_specs(t, s, CB_DF_Q, CB_DF_K, CB_DF_V) + _bias_specs(t, 2),
        scratch_shapes=_kv_scratch(s, t, key_dim=DIFF_HALF, n_keys=2 * GROUP_HEADS),
        compiler_params=_params("parallel", "arbitrary"),
        name="diff_attention",
        **_group_out(bsz, s, t),
    )(lam_params, subln, proj, proj, proj, bias_diag, bias_sub)


def _compress_kernel(ck_ref, cv_ref, pk_ref, pv_ref, wk1_ref, wk2_ref, wv1_ref, wv2t_ref, kc_ref, vct_ref):
    half = ck_ref.shape[2]

    def hidden(c_ref, p_ref, w1_ref):
        chunks = c_ref[0]
        top = _dot((chunks + p_ref[:, :half]).astype(_MXU), w1_ref[:half, :])
        bot = _dot((chunks + p_ref[:, half:]).astype(_MXU), w1_ref[half:, :])
        pre = top + pltpu.roll(bot, bot.shape[0] - 1, axis=0)
        return jax.nn.gelu(pre).astype(_MXU)

    kc_ref[0] = _dot(hidden(ck_ref, pk_ref, wk1_ref), wk2_ref[...])
    vct_ref[0] = _dot_nt(wv2t_ref[...], hidden(cv_ref, pv_ref, wv1_ref))


def _compress(chunks_k, chunks_v, pos_k, pos_v, wk1, wk2, wv1, wv2t):
    bsz, n_chunk, width = chunks_k.shape
    cspec = pl.BlockSpec((1, n_chunk, width), lambda b: (b, 0, 0))

    def full(a):
        return pl.BlockSpec(a.shape, lambda b: (0, 0))

    return pl.pallas_call(
        _compress_kernel,
        grid=(bsz,),
        in_specs=[cspec, cspec, full(pos_k), full(pos_v), full(wk1), full(wk2), full(wv1), full(wv2t)],
        out_specs=[pl.BlockSpec((1, n_chunk, HEAD_DIM), lambda b: (b, 0, 0)),
                   pl.BlockSpec((1, HEAD_DIM, n_chunk), lambda b: (b, 0, 0))],
        out_shape=[jax.ShapeDtypeStruct((bsz, n_chunk, HEAD_DIM), jnp.float32),
                   jax.ShapeDtypeStruct((bsz, HEAD_DIM, n_chunk), jnp.float32)],
        compiler_params=_params("parallel"),
        name="nsa_compress",
    )(chunks_k, chunks_v, pos_k, pos_v, wk1, wk2, wv1, wv2t)


def _nsa_kernel(q_ref, kva_ref, kvb_ref, gq_ref, kc_ref, vct_ref, bd_ref, bs_ref, bc_ref, cover_ref, e_ref,
                o_ref, ks_ref, vst_ref, kw_ref, vwt_ref):
    t = q_ref.shape[1] // Q_TILES
    i = pl.program_id(1)
    key, qry = _tile_iotas(t)
    causal = key <= qry
    ks_col, vs_col, kw_col, vw_col, gate_col = 2 * HEAD_DIM, 3 * HEAD_DIM, 0, HEAD_DIM, 2 * HEAD_DIM

    @pl.when(i == 0)
    def _():
        ks_ref[...] = kva_ref[0, :, ks_col:ks_col + HEAD_DIM].astype(ks_ref.dtype)
        kw_ref[...] = kvb_ref[0, :, kw_col:kw_col + HEAD_DIM].astype(kw_ref.dtype)
        _fill_values_t(vst_ref, kva_ref, vs_col, t)
        _fill_values_t(vwt_ref, kvb_ref, vw_col, t)

    def tile(k_ref, vt_ref, j, bias_ref=None, emask=None):
        n = GROUP_HEADS
        emask = emask() if callable(emask) else emask
        return _KeyTile([_key_rows(k_ref, j, t)] * n, [vt_ref[j]] * n,
                        None if bias_ref is None else [bias_ref[h] for h in _HEADS],
                        None if emask is None else [emask] * n)

    def query_step(step):
        qts = _queries_t(q_ref, HEAD_DIM, t)
        n_cmp = kc_ref.shape[1]
        n_slc = cover_ref.shape[0]
        kc = kc_ref[0].astype(_MXU)
        vct = vct_ref[0].astype(_MXU)
        c_row = lax.broadcasted_iota(jnp.int32, (n_cmp, t), 0)
        c_col = lax.broadcasted_iota(jnp.int32, (n_cmp, t), 1)
        s_row = lax.broadcasted_iota(jnp.int32, (n_slc, t), 0)
        s_col = lax.broadcasted_iota(jnp.int32, (n_slc, t), 1)
        tiles = [step * Q_TILES + r for r in range(Q_TILES)]

        o_cmp, importance = [], []
        for r, k in enumerate(tiles):
            visible = c_col + k * t >= c_row * CMP_STRIDE + (CMP_LEN - 1)
            cmp_scores = [_dot(kc, qts[r][h]) for h in _HEADS]
            cmp_probs = []
            p_sum = jnp.zeros((n_cmp, t), jnp.float32)
            for h in _HEADS:
                sc = jnp.where(visible, cmp_scores[h] + bc_ref[h, :, r * t:(r + 1) * t], NEG_INF)
                e = jnp.where(visible, jnp.exp2(sc - jnp.max(sc, axis=0, keepdims=True)), 0.0)
                p = e / jnp.maximum(jnp.sum(e, axis=0, keepdims=True), TINY)
                cmp_probs.append(p.astype(_MXU))
                p_sum = p_sum + p
            o_cmp.append([_dot(vct, cmp_probs[h]) for h in _HEADS])
            importance.append(_dot(cover_ref[...], p_sum, precision=lax.Precision.HIGHEST))

        n_back = WINDOW // t
        jobs = []
        for r, k in enumerate(tiles):
            window = [functools.partial(tile, kw_ref, vwt_ref, k - n_back, None, qry < key)] if k >= n_back else []
            for back in range(min(n_back - 1, k), 0, -1):
                window.append(functools.partial(tile, kw_ref, vwt_ref, k - back, bs_ref if back == 1 else None))
            window.append(functools.partial(tile, kw_ref, vwt_ref, k, bd_ref, causal))
            jobs.append((GROUP_HEADS, qts[r], [window]))
        o_win = [[_softmax_out(st) for st in states] for states in _softmax_jobs(t, jobs)]

        jobs = []
        for r, k in enumerate(tiles):
            own = jnp.right_shift(s_col + k * t, int(math.log2(SLC_LEN)))
            score = jnp.where(s_row == own, FORCE, jnp.where(s_row < own, importance[r], NEG_INF))
            sel = _top_k_rows(score, s_row.astype(jnp.float32), min(SLC_TOPN, n_slc)).astype(_MXU)

            def chosen(sel, j, diagonal=False):
                mask = _dot(_key_rows(e_ref, j, t), sel) > 0.5
                return jnp.logical_and(mask, causal) if diagonal else mask

            chosen = functools.partial(chosen, sel)
            groups = _pairs([functools.partial(tile, ks_ref, vst_ref, j, None, functools.partial(chosen, j))
                             for j in range(k - 1)])
            groups.append(([functools.partial(tile, ks_ref, vst_ref, k - 1, bs_ref, functools.partial(chosen, k - 1))]
                           if k else [])
                          + [functools.partial(tile, ks_ref, vst_ref, k, bd_ref, functools.partial(chosen, k, True))])
            jobs.append((GROUP_HEADS, qts[r], groups))
        o_slc = [[_softmax_out(st) for st in states] for states in _softmax_jobs(t, jobs)]

        gates = _transposed(gq_ref[0, :, (gate_col // LANES) * LANES:(gate_col // LANES + 1) * LANES])
        gates = 1.0 / (1.0 + jnp.exp(-gates[gate_col % LANES:gate_col % LANES + N_GATES + 4]))
        outs = []
        for r in range(Q_TILES):
            heads = []
            for h in _HEADS:
                g = [gates[br * GROUP_HEADS + h:br * GROUP_HEADS + h + 1, r * t:(r + 1) * t] for br in range(3)]
                heads.append(g[0] * o_cmp[r][h] + g[1] * o_slc[r][h] + g[2] * o_win[r][h])
            outs.append(heads)
        _store_heads(o_ref, outs)

    _per_step(i, kva_ref.shape[1] // (Q_TILES * t), query_step)


def _nsa(proj, kc, vc_t, bias_diag, bias_sub, bias_cmp, cover_t, expand, t=TILE):
    bsz, s, _ = proj.shape
    n_cmp = kc.shape[1]
    kv_scratch = [pltpu.VMEM((s, HEAD_DIM), _MXU), pltpu.VMEM((s // t, ACC_ROWS, t), _MXU)]
    return pl.pallas_call(
        _nsa_kernel,
        grid=(bsz, s // (Q_TILES * t)),
        in_specs=_group_specs(t, s, CB_NS_Q, CB_NS_A, CB_NS_B)
        + [pl.BlockSpec((1, Q_TILES * t, GROUP_WIDTH), lambda b, i: (b, i, CB_NS_B)),
           pl.BlockSpec((1, n_cmp, HEAD_DIM), lambda b, i: (b, 0, 0)),
           pl.BlockSpec((1, HEAD_DIM, n_cmp), lambda b, i: (b, 0, 0))]
        + _bias_specs(t, 1)
        + [pl.BlockSpec((GROUP_HEADS, n_cmp, Q_TILES * t), lambda b, i: (0, 0, i)),
           pl.BlockSpec(cover_t.shape, lambda b, i: (0, 0)),
           pl.BlockSpec(expand.shape, lambda b, i: (0, 0))],
        scratch_shapes=kv_scratch + kv_scratch,
        compiler_params=_params("parallel", "arbitrary"),
        name="nsa",
        **_group_out(bsz, s, t),
    )(proj, proj, proj, proj, kc, vc_t, bias_diag, bias_sub, bias_cmp, cover_t, expand)


def _nsa_constants(s):
    n_cmp = (s - CMP_LEN) // CMP_STRIDE + 1
    n_slc = s // SLC_LEN
    assert n_cmp + 1 == s // CMP_STRIDE and n_slc % SUBLANES == 0
    c_start = np.arange(n_cmp) * CMP_STRIDE
    s_start = np.arange(n_slc) * SLC_LEN
    cover = np.clip(np.minimum((c_start + CMP_LEN - 1)[:, None], (s_start + SLC_LEN - 1)[None, :])
                    - np.maximum(c_start[:, None], s_start[None, :]) + 1, 0, None) / CMP_LEN
    cover_t = np.zeros((n_slc, n_cmp + 1), np.float32)
    cover_t[:, :n_cmp] = cover.T
    expand = (np.arange(s)[:, None] // SLC_LEN == np.arange(n_slc)[None, :]).astype(np.float32)
    return jnp.asarray(cover_t), jnp.asarray(expand, _MXU)


def kernel(x, w_in, w_out, w_up, w_down, norm_attn, norm_mlp, cmp_pos_k, cmp_pos_v, cmp_k_w1, cmp_k_w2,
           cmp_v_w1, cmp_v_w2, diff_lambda, diff_subln, rel_bias, final_norm):
    bsz, s, d = x.shape
    depth = w_in.shape[0]
    t = TILE
    n_chunk = s // CMP_STRIDE
    assert s % MOBA_BLOCK == 0 and MOBA_BLOCK % t == 0 and WINDOW % t == 0 and t >= MAX_DISTANCE

    w_in_p = jnp.concatenate([w_in[:, :, :COLS_BEFORE_PAD],
                              jnp.zeros((depth, d, PAD_COLS), w_in.dtype),
                              w_in[:, :, COLS_BEFORE_PAD:]], axis=2).astype(_MXU)
    assert w_in_p.shape[2] == D_IN_PAD
    w_out_c, w_up_c, w_down_c = w_out.astype(_MXU), w_up.astype(_MXU), w_down.astype(_MXU)
    wk1, wk2 = cmp_k_w1.astype(_MXU), cmp_k_w2.astype(_MXU)
    wv1, wv2t = cmp_v_w1.astype(_MXU), jnp.swapaxes(cmp_v_w2, 1, 2).astype(_MXU)

    tiles = dict(n_heads=rel_bias.shape[1], head0=0, rows=t, cols=t, col_tile=t, row_stride=-1, col_stride=1)
    bias_diag = _bias_table(rel_bias, offset=0, **tiles)
    bias_sub = _bias_table(rel_bias, offset=t, **tiles)
    bias_cmp = _bias_table(rel_bias, n_heads=GROUP_HEADS, head0=GROUP_HEADS, rows=n_chunk, cols=s,
                           col_tile=t, row_stride=-CMP_STRIDE, col_stride=1, offset=-(CMP_LEN - 1))
    cover_t, expand = _nsa_constants(s)
    col_scale = np.ones((1, D_IN_PAD), np.float32)
    for cb, width in ((CB_SB_Q, HEAD_DIM), (CB_MB_Q, HEAD_DIM), (CB_NS_Q, HEAD_DIM), (CB_DF_Q, DIFF_HALF)):
        col_scale[:, cb * GROUP_WIDTH:(cb + 1) * GROUP_WIDTH] = width ** -0.5 * LOG2E
    col_scale = jnp.asarray(col_scale)

    x2 = x.reshape(bsz * s, d)
    for layer in range(depth):
        proj = _norm_matmul(x2, norm_attn[layer][None], w_in_p[layer], col_scale).reshape(bsz, s, D_IN_PAD)
        o_sb = _stick_breaking(proj)
        o_mb = _moba(proj, bias_diag, bias_sub)
        col0 = CB_NS_A * GROUP_WIDTH
        chunks_k = proj[:, :, col0:col0 + HEAD_DIM].reshape(bsz, n_chunk, CMP_STRIDE * HEAD_DIM)
        chunks_v = proj[:, :, col0 + HEAD_DIM:col0 + 2 * HEAD_DIM].reshape(bsz, n_chunk, CMP_STRIDE * HEAD_DIM)
        kc, vc_t = _compress(chunks_k, chunks_v, cmp_pos_k[layer].reshape(1, -1), cmp_pos_v[layer].reshape(1, -1),
                             wk1[layer], wk2[layer], wv1[layer], wv2t[layer])
        o_ns = _nsa(proj, kc, vc_t, bias_diag, bias_sub, bias_cmp, cover_t, expand)
        lambda_init = 0.8 - 0.6 * math.exp(-0.3 * layer)
        o_df = _diff(proj, diff_lambda[layer], diff_subln[layer][:, None], bias_diag, bias_sub, lambda_init)
        groups = [o.reshape(bsz * s, GROUP_WIDTH) for o in (o_sb, o_mb, o_ns, o_df)]
        x2 = _out_mlp(x2, groups, w_out_c[layer], norm_mlp[layer][None], w_up_c[layer], w_down_c[layer],
                      final_norm[None], final_norm=(layer == depth - 1))
    return x2.reshape(bsz, s, d)
```

```python
import functools
import math
from typing import Any, NamedTuple, Optional, Sequence

import numpy as np
import jax
import jax.numpy as jnp
from jax import lax
from jax.experimental import pallas as pl
from jax.experimental.pallas import tpu as pltpu

HEAD_DIM = 64
GROUP_HEADS = 4
GROUP_WIDTH = GROUP_HEADS * HEAD_DIM
NORM_EPS = 1e-6
NEG_INF = -1e30
BIG = 1e30
FORCE = 1e30
TINY = 1e-30
PICKED = -3e38
LOG2E = math.log2(math.e)
N_BUCKETS = 32
MAX_DISTANCE = 128
MOBA_BLOCK = 256
MOBA_TOPK = 3
CMP_LEN = 32
CMP_STRIDE = 16
SLC_LEN = 64
SLC_TOPN = 4
WINDOW = 512
DIFF_HALF = HEAD_DIM // 2
LANES = 128
SUBLANES = 8
TILE = 256
BF16_ROWS = 16
ACC_ROWS = HEAD_DIM + BF16_ROWS
N_GATES = 3 * GROUP_HEADS
COLS_BEFORE_PAD = 9 * GROUP_WIDTH - 2 * HEAD_DIM + N_GATES
PAD_COLS = 2 * HEAD_DIM - N_GATES
CB_SB_Q, CB_SB_K, CB_SB_V, CB_MB_Q, CB_MB_K, CB_MB_V, CB_NS_Q, CB_NS_A, CB_NS_B, CB_DF_Q, CB_DF_K, CB_DF_V = range(12)
D_IN_PAD = 12 * GROUP_WIDTH

_MXU = jnp.bfloat16
_VMEM_LIMIT = 56 * 1024 * 1024
_HEADS = range(GROUP_HEADS)
Q_TILES = 4
LOOKAHEAD = 2


def _dot(a, b, precision=None):
    return jnp.dot(a, b, precision=precision, preferred_element_type=jnp.float32)


def _dot_nt(a, b, precision=None):
    return lax.dot_general(a, b, (((1,), (1,)), ((), ())), precision=precision,
                           preferred_element_type=jnp.float32)


def _rms(x, g):
    return x * lax.rsqrt(jnp.mean(x * x, axis=-1, keepdims=True) + NORM_EPS) * g


def _params(*sem):
    return pltpu.CompilerParams(dimension_semantics=sem, vmem_limit_bytes=_VMEM_LIMIT)


def _norm_matmul_kernel(x_ref, g_ref, w_ref, scale_ref, o_ref, side_ref, *, tn, side_col):
    h = _rms(x_ref[...], g_ref[...]).astype(_MXU)
    for j in range(w_ref.shape[1] // tn):
        cols = slice(j * tn, (j + 1) * tn)
        acc = _dot(h, w_ref[:, cols])
        o_ref[:, cols] = (acc * scale_ref[:, cols]).astype(o_ref.dtype)
        if j * tn <= side_col < (j + 1) * tn:
            side_ref[...] = acc[:, side_col - j * tn:side_col - j * tn + side_ref.shape[1]]


def _norm_matmul(x, g, w, col_scale, *, side_col, side_width, tm=512, tn=1024):
    m, d = x.shape
    n = w.shape[1]
    assert side_col % LANES == 0 and side_col // tn == (side_col + side_width - 1) // tn
    return pl.pallas_call(
        functools.partial(_norm_matmul_kernel, tn=tn, side_col=side_col),
        grid=(m // tm,),
        in_specs=[pl.BlockSpec((tm, d), lambda i: (i, 0)),
                  pl.BlockSpec((1, d), lambda i: (0, 0)),
                  pl.BlockSpec((d, n), lambda i: (0, 0)),
                  pl.BlockSpec((1, n), lambda i: (0, 0))],
        out_specs=[pl.BlockSpec((tm, n), lambda i: (i, 0)),
                   pl.BlockSpec((tm, side_width), lambda i: (i, 0))],
        out_shape=[jax.ShapeDtypeStruct((m, n), _MXU),
                   jax.ShapeDtypeStruct((m, side_width), jnp.float32)],
        compiler_params=_params("parallel"),
        name="norm_in_proj",
    )(x, g, w, col_scale)


def _out_mlp_kernel(x_ref, a_ref, b_ref, c_ref, d_ref, wo_ref, g_ref, wu_ref, wd_ref, gf_ref, o_ref,
                    h_ref, acc_ref, *, final_norm):
    c = pl.program_id(1)

    @pl.when(c == 0)
    def _():
        mixed = jnp.concatenate([a_ref[...], b_ref[...], c_ref[...], d_ref[...]], axis=1)
        y = x_ref[...] + _dot(mixed, wo_ref[...])
        acc_ref[...] = y
        h_ref[...] = _rms(y, g_ref[...]).astype(h_ref.dtype)

    u = jnp.square(jnp.maximum(_dot(h_ref[...], wu_ref[...]), 0.0))
    acc_ref[...] += _dot(u.astype(_MXU), wd_ref[...])

    @pl.when(c == pl.num_programs(1) - 1)
    def _():
        y = acc_ref[...]
        if final_norm:
            y = _rms(y, gf_ref[...])
        o_ref[...] = y


def _out_mlp(x, groups, w_out, g, w_up, w_down, g_final, *, final_norm, tm=512, tf=2048):
    m, d = x.shape
    f = w_up.shape[1]
    gspec = pl.BlockSpec((tm, GROUP_WIDTH), lambda i, c: (i, 0))
    row = pl.BlockSpec((1, d), lambda i, c: (0, 0))
    return pl.pallas_call(
        functools.partial(_out_mlp_kernel, final_norm=final_norm),
        grid=(m // tm, f // tf),
        in_specs=[pl.BlockSpec((tm, d), lambda i, c: (i, 0)), gspec, gspec, gspec, gspec,
                  pl.BlockSpec((d, d), lambda i, c: (0, 0)), row,
                  pl.BlockSpec((d, tf), lambda i, c: (0, c)),
                  pl.BlockSpec((tf, d), lambda i, c: (c, 0)), row],
        out_specs=pl.BlockSpec((tm, d), lambda i, c: (i, 0)),
        out_shape=jax.ShapeDtypeStruct((m, d), jnp.float32),
        scratch_shapes=[pltpu.VMEM((tm, d), _MXU), pltpu.VMEM((tm, d), jnp.float32)],
        compiler_params=_params("parallel", "arbitrary"),
        name="out_proj_mlp_residual",
    )(x, *groups, w_out, g, w_up, w_down, g_final)


def _t5_bucket(dist):
    n = jnp.maximum(dist, 0)
    max_exact = N_BUCKETS // 2
    nf = jnp.maximum(n, 1).astype(jnp.float32)
    large = max_exact + (jnp.log(nf / max_exact) / math.log(MAX_DISTANCE / max_exact)
                         * (N_BUCKETS - max_exact)).astype(jnp.int32)
    large = jnp.minimum(large, N_BUCKETS - 1)
    return jnp.where(n < max_exact, n, large)


def _bias_kernel(tab_ref, o_ref, *, row_stride, col_stride, offset, head0):
    nh, tr, tc = o_ref.shape
    col0 = pl.program_id(0) * tc

    def row_group(r, carry):
        start = pl.multiple_of(r * SUBLANES, SUBLANES)
        rows = lax.broadcasted_iota(jnp.int32, (SUBLANES, tc), 0) + start
        cols = lax.broadcasted_iota(jnp.int32, (SUBLANES, tc), 1) + col0
        bucket = _t5_bucket(rows * row_stride + cols * col_stride + offset)
        outs = [jnp.zeros((SUBLANES, tc), jnp.float32)] * nh
        for bkt in range(N_BUCKETS):
            hit = bucket == bkt
            outs = [jnp.where(hit, tab_ref[bkt, head0 + h], outs[h]) for h in range(nh)]
        for h in range(nh):
            o_ref[h, pl.ds(start, SUBLANES), :] = (outs[h] - tab_ref[N_BUCKETS - 1, head0 + h]) * LOG2E
        return carry

    lax.fori_loop(0, tr // SUBLANES, row_group, 0)


def _bias_table(table, *, n_heads, head0, rows, cols, col_tile, row_stride, col_stride, offset):
    return pl.pallas_call(
        functools.partial(_bias_kernel, row_stride=row_stride, col_stride=col_stride,
                          offset=offset, head0=head0),
        grid=(cols // col_tile,),
        in_specs=[pl.BlockSpec(memory_space=pltpu.SMEM)],
        out_specs=pl.BlockSpec((n_heads, rows, col_tile), lambda i: (0, 0, i)),
        out_shape=jax.ShapeDtypeStruct((n_heads, rows, cols), jnp.float32),
        compiler_params=_params("parallel"),
        name="t5_bias_tiles",
    )(table)


def _softmax_init(t):
    return (jnp.full((1, t), NEG_INF, jnp.float32), jnp.zeros((ACC_ROWS, t), jnp.float32))


class _KeyTile(NamedTuple):
    kts: Sequence[Any]
    vts: Sequence[Any]
    biases: Optional[Sequence[Any]] = None
    emasks: Optional[Sequence[Any]] = None
    qmasks: Optional[Sequence[Any]] = None


def _round_robin(lists):
    out = []
    for rank in range(max(map(len, lists), default=0)):
        out.extend(items[rank] for items in lists if rank < len(items))
    return out


def _softmax_jobs(t, jobs):
    built = {}

    def tiles_of(job, g):
        if (job, g) not in built:
            built[job, g] = [make() for make in jobs[job][2][g]]
        return built[job, g]

    def scores_of(job, g, c):
        qts = jobs[job][1]
        row = []
        for tile in tiles_of(job, g):
            s = _dot(tile.kts[c], qts[c])
            if tile.biases is not None:
                s = s + tile.biases[c]
            if tile.emasks is not None:
                s = jnp.where(tile.emasks[c], s, NEG_INF)
            row.append(s.astype(_MXU))
        return row

    def update(state, job, g, c, scores):
        m, acc = state
        m_new = m
        for tile, s in zip(tiles_of(job, g), scores):
            tile_max = jnp.max(s, axis=0, keepdims=True).astype(jnp.float32)
            if tile.qmasks is not None:
                tile_max = jnp.where(tile.qmasks[c], tile_max, NEG_INF)
            m_new = jnp.maximum(m_new, tile_max)
        seen = m_new > 0.5 * NEG_INF
        acc = jnp.exp2(m - m_new) * acc
        for tile, s in zip(tiles_of(job, g), scores):
            ok = seen if tile.qmasks is None else jnp.logical_and(seen, tile.qmasks[c])
            acc = acc + _dot(tile.vts[c], jnp.exp2(s - jnp.where(ok, m_new, BIG).astype(_MXU)))
        return m_new, acc

    units = _round_robin([[(job, g, c) for g in range(len(groups)) for c in range(n)]
                          for job, (n, _, groups) in enumerate(jobs)])
    lookahead = LOOKAHEAD * len(jobs)
    states = [[_softmax_init(t) for _ in range(n)] for n, _, _ in jobs]
    pending = {k: scores_of(*units[k]) for k in range(min(lookahead, len(units)))}
    for k, (job, g, c) in enumerate(units):
        if k + lookahead < len(units):
            pending[k + lookahead] = scores_of(*units[k + lookahead])
        states[job][c] = update(states[job][c], job, g, c, pending.pop(k))
    return states


def _pairs(items):
    return [items[p:p + 2] for p in range(0, len(items), 2)]


def _per_step(i, n_steps, body):
    for step in range(n_steps):
        pl.when(i == step)(functools.partial(body, step))


def _softmax_out(state):
    acc = state[1]
    return acc[:HEAD_DIM] / jnp.maximum(acc[HEAD_DIM:HEAD_DIM + 1], TINY)


def _top_k_rows(score, row_f, k):
    sel = jnp.zeros(score.shape, jnp.float32)
    for _ in range(k):
        mx = jnp.max(score, axis=0, keepdims=True)
        idx = jnp.min(jnp.where(score == mx, row_f, float(score.shape[0])), axis=0, keepdims=True)
        pick = row_f == idx
        sel = jnp.where(pick, 1.0, sel)
        score = jnp.where(pick, PICKED, score)
    return sel


def _tile_iotas(t):
    return (lax.broadcasted_iota(jnp.int32, (t, t), 0), lax.broadcasted_iota(jnp.int32, (t, t), 1))


def _key_rows(ref, j, t):
    if isinstance(j, int):
        return ref[j * t:(j + 1) * t, :]
    return ref[pl.ds(pl.multiple_of(j * t, t), t), :]


def _transposed(ref_block):
    return ref_block.astype(jnp.float32).T


def _queries_t(q_ref, width, t):
    qt = _transposed(q_ref[0]).astype(_MXU)
    return [[qt[c * width:(c + 1) * width, r * t:(r + 1) * t] for c in range(GROUP_WIDTH // width)]
            for r in range(Q_TILES)]


def _fill_values_t(vt_ref, v_ref, col0, t):
    n_tiles, rows, _ = vt_ref.shape
    lane_block = (col0 // LANES) * LANES
    for c in range(n_tiles):
        blk = _transposed(v_ref[0, c * t:(c + 1) * t, lane_block:lane_block + LANES])
        vt_ref[c, 0:HEAD_DIM, :] = blk[col0 - lane_block:col0 - lane_block + HEAD_DIM].astype(vt_ref.dtype)
        if rows == ACC_ROWS:
            first = lax.broadcasted_iota(jnp.int32, (rows - HEAD_DIM, t), 0) == 0
            vt_ref[c, HEAD_DIM:rows, :] = jnp.where(first, 1.0, 0.0).astype(vt_ref.dtype)


def _group_specs(t, s, cb_q, cb_k, cb_v):
    return [pl.BlockSpec((1, Q_TILES * t, GROUP_WIDTH), lambda b, i: (b, i, cb_q)),
            pl.BlockSpec((1, s, GROUP_WIDTH), lambda b, i: (b, 0, cb_k)),
            pl.BlockSpec((1, s, GROUP_WIDTH), lambda b, i: (b, 0, cb_v))]


def _group_out(bsz, s, t):
    return dict(out_specs=pl.BlockSpec((1, Q_TILES * t, GROUP_WIDTH), lambda b, i: (b, i, 0)),
                out_shape=jax.ShapeDtypeStruct((bsz, s, GROUP_WIDTH), _MXU))


def _bias_specs(t, head_group):
    spec = pl.BlockSpec((GROUP_HEADS, t, t), lambda b, i: (head_group, 0, 0))
    return [spec, spec]


def _store_heads(o_ref, outs_t):
    tiles = [jnp.concatenate(heads, axis=0) for heads in outs_t]
    o_ref[0] = jnp.concatenate(tiles, axis=1).T.astype(o_ref.dtype)


def _sb_kernel(q_ref, k_ref, v_ref, o_ref, kb_ref, vt_ref):
    t = q_ref.shape[1] // Q_TILES
    i = pl.program_id(1)
    key, qry = _tile_iotas(t)
    strict = key < qry
    later = jnp.where(qry > key, 1.0, 0.0).astype(_MXU)

    @pl.when(i == 0)
    def _():
        for h in _HEADS:
            kb_ref[h] = k_ref[0, :, h * HEAD_DIM:(h + 1) * HEAD_DIM].astype(kb_ref.dtype)
            _fill_values_t(vt_ref.at[h], v_ref, h * HEAD_DIM, t)

    def query_step(step):
        qts = _queries_t(q_ref, HEAD_DIM, t)
        units = _round_robin([[(r, step * Q_TILES + r, pair, h)
                               for pair in _pairs(list(range(step * Q_TILES + r, -1, -1))) for h in _HEADS]
                              for r in range(Q_TILES)])
        zero = (jnp.zeros((HEAD_DIM, t), jnp.float32), jnp.zeros((1, t), jnp.float32))
        carry = [[zero] * GROUP_HEADS for _ in range(Q_TILES)]
        zs, log_keeps, suffixes = {}, {}, {}

        def scores(u):
            r, _, pair, h = units[u]
            zs[u] = [_dot(_key_rows(kb_ref.at[h], j, t), qts[r][h]) for j in pair]

        def keeps(u):
            _, k, pair, _ = units[u]
            log_keeps[u], suffixes[u] = [], []
            for j, z in zip(pair, zs[u]):
                neg = -z
                log_keep = jnp.minimum(neg, 0.0) - jnp.log2(1.0 + jnp.exp2(jnp.minimum(z, neg)))
                if j == k:
                    log_keep = jnp.where(strict, log_keep, 0.0)
                log_keeps[u].append(log_keep)
                suffixes[u].append(_dot(later, log_keep.astype(_MXU)))

        def values(u):
            r, k, pair, h = units[u]
            acc, run = carry[r][h]
            weights = []
            for j, z, log_keep, suffix in zip(pair, zs.pop(u), log_keeps.pop(u), suffixes.pop(u)):
                a = jnp.exp2(z + log_keep + suffix + run)
                if j == k:
                    a = jnp.where(strict, a, 0.0)
                weights.append(a.astype(_MXU))
                run = run + (suffix[0:1] + log_keep[0:1])
            for j, w in zip(pair, weights):
                acc = acc + _dot(vt_ref[h, j], w)
            carry[r][h] = (acc, run)

        stages = (scores, keeps, values)
        for tick in range(len(units) + len(stages) - 1):
            for lag, stage in enumerate(stages):
                if 0 <= tick - lag < len(units):
                    stage(tick - lag)
        _store_heads(o_ref, [[c[0] for c in tile_carry] for tile_carry in carry])

    _per_step(i, k_ref.shape[1] // (Q_TILES * t), query_step)


def _stick_breaking(proj, t=TILE):
    bsz, s, _ = proj.shape
    return pl.pallas_call(
        _sb_kernel,
        grid=(bsz, s // (Q_TILES * t)),
        in_specs=_group_specs(t, s, CB_SB_Q, CB_SB_K, CB_SB_V),
        scratch_shapes=[pltpu.VMEM((GROUP_HEADS, s, HEAD_DIM), _MXU),
                        pltpu.VMEM((GROUP_HEADS, s // t, HEAD_DIM, t), _MXU)],
        compiler_params=_params("parallel", "arbitrary"),
        name="stick_breaking",
        **_group_out(bsz, s, t),
    )(proj, proj, proj)


def _moba_kernel(q_ref, k_ref, v_ref, bd_ref, bs_ref, o_ref, kb_ref, vt_ref, km_ref):
    t = q_ref.shape[1] // Q_TILES
    n_blk = k_ref.shape[1] // MOBA_BLOCK
    tiles_per_blk = MOBA_BLOCK // t
    blk_shift = int(math.log2(tiles_per_blk))
    i = pl.program_id(1)

    @pl.when(i == 0)
    def _():
        km_ref[...] = jnp.zeros_like(km_ref)
        for h in _HEADS:
            lo, hi = h * HEAD_DIM, (h + 1) * HEAD_DIM
            kb_ref[h] = k_ref[0, :, lo:hi].astype(kb_ref.dtype)
            _fill_values_t(vt_ref.at[h], v_ref, lo, t)
            for n in range(n_blk):
                blk = k_ref[0, n * MOBA_BLOCK:(n + 1) * MOBA_BLOCK, lo:hi]
                km_ref[h, n:n + 1, :] = jnp.mean(blk.astype(jnp.float32), axis=0, keepdims=True)

    def query_step(step):
        key, qry = _tile_iotas(t)
        causal = key <= qry
        blk_row = lax.broadcasted_iota(jnp.int32, (km_ref.shape[1], t), 0)
        qt = _transposed(q_ref[0])
        jobs = []
        for r in range(Q_TILES):
            k = step * Q_TILES + r
            own = k >> blk_shift
            qts, sels = [], []
            for h in _HEADS:
                qf = qt[h * HEAD_DIM:(h + 1) * HEAD_DIM, r * t:(r + 1) * t]
                qts.append(qf.astype(_MXU))
                gate = _dot(km_ref[h], qf, precision=lax.Precision.HIGHEST)
                gate = jnp.where(blk_row < own, gate, NEG_INF)
                sel = _top_k_rows(gate, blk_row.astype(jnp.float32), min(MOBA_TOPK, n_blk - 1))
                sels.append(jnp.where(blk_row < own, sel, 0.0))

            def tile(own, sels, j, bias_ref=None, emask=None):
                n = j >> blk_shift
                return _KeyTile([_key_rows(kb_ref.at[h], j, t) for h in _HEADS], [vt_ref[h, j] for h in _HEADS],
                                None if bias_ref is None else [bias_ref[h] for h in _HEADS],
                                None if emask is None else [emask] * GROUP_HEADS,
                                None if n == own else [sels[h][n:n + 1] > 0.5 for h in _HEADS])

            tile = functools.partial(tile, own, sels)
            groups = _pairs([functools.partial(tile, j) for j in range(k - 1)])
            groups.append(([functools.partial(tile, k - 1, bs_ref)] if k else [])
                          + [functools.partial(tile, k, bd_ref, causal)])
            jobs.append((GROUP_HEADS, qts, groups))
        _store_heads(o_ref, [[_softmax_out(st) for st in states] for states in _softmax_jobs(t, jobs)])

    _per_step(i, k_ref.shape[1] // (Q_TILES * t), query_step)


def _kv_scratch(s, t, key_dim=HEAD_DIM, n_keys=GROUP_HEADS):
    return [pltpu.VMEM((n_keys, s, key_dim), _MXU), pltpu.VMEM((GROUP_HEADS, s // t, ACC_ROWS, t), _MXU)]


def _moba(proj, bias_diag, bias_sub, t=TILE):
    bsz, s, _ = proj.shape
    n_blk_pad = -(-(s // MOBA_BLOCK) // SUBLANES) * SUBLANES
    return pl.pallas_call(
        _moba_kernel,
        grid=(bsz, s // (Q_TILES * t)),
        in_specs=_group_specs(t, s, CB_MB_Q, CB_MB_K, CB_MB_V) + _bias_specs(t, 0),
        scratch_shapes=_kv_scratch(s, t) + [pltpu.VMEM((GROUP_HEADS, n_blk_pad, HEAD_DIM), jnp.float32)],
        compiler_params=_params("parallel", "arbitrary"),
        name="moba",
        **_group_out(bsz, s, t),
    )(proj, proj, proj, bias_diag, bias_sub)


def _diff_kernel(lam_ref, g_ref, q_ref, k_ref, v_ref, bd_ref, bs_ref, o_ref, kb_ref, vt_ref, *, lambda_init):
    t = q_ref.shape[1] // Q_TILES
    i = pl.program_id(1)
    key, qry = _tile_iotas(t)
    causal = key <= qry
    lv = lam_ref[...]
    lam = (jnp.exp(jnp.sum(lv[0:1] * lv[1:2], keepdims=True))
           - jnp.exp(jnp.sum(lv[2:3] * lv[3:4], keepdims=True)) + lambda_init)
    halves = range(2 * GROUP_HEADS)

    @pl.when(i == 0)
    def _():
        for c in halves:
            kb_ref[c] = k_ref[0, :, c * DIFF_HALF:(c + 1) * DIFF_HALF].astype(kb_ref.dtype)
        for h in _HEADS:
            _fill_values_t(vt_ref.at[h], v_ref, h * HEAD_DIM, t)

    def tile(j, bias_ref=None, emask=None):
        n = len(halves)
        return _KeyTile([_key_rows(kb_ref.at[c], j, t) for c in halves], [vt_ref[c // 2, j] for c in halves],
                        None if bias_ref is None else [bias_ref[c // 2] for c in halves],
                        None if emask is None else [emask] * n)

    def query_step(step):
        qts = _queries_t(q_ref, DIFF_HALF, t)
        jobs = []
        for r in range(Q_TILES):
            k = step * Q_TILES + r
            groups = _pairs([functools.partial(tile, j) for j in range(k - 1)])
            groups.append(([functools.partial(tile, k - 1, bs_ref)] if k else [])
                          + [functools.partial(tile, k, bd_ref, causal)])
            jobs.append((len(halves), qts[r], groups))
        outs = []
        for states in _softmax_jobs(t, jobs):
            heads = []
            for h in _HEADS:
                o = _softmax_out(states[2 * h]) - lam * _softmax_out(states[2 * h + 1])
                o = o * lax.rsqrt(jnp.mean(o * o, axis=0, keepdims=True) + NORM_EPS) * g_ref[...]
                heads.append(o * (1.0 - lambda_init))
            outs.append(heads)
        _store_heads(o_ref, outs)

    _per_step(i, k_ref.shape[1] // (Q_TILES * t), query_step)


def _diff(proj, lam_params, subln, bias_diag, bias_sub, lambda_init, t=TILE):
    bsz, s, _ = proj.shape
    return pl.pallas_call(
        functools.partial(_diff_kernel, lambda_init=lambda_init),
        grid=(bsz, s // (Q_TILES * t)),
        in_specs=[pl.BlockSpec(lam_params.shape, lambda b, i: (0, 0)),
                  pl.BlockSpec(subln.shape, lambda b, i: (0, 0))]
        + _group_specs(t, s, CB_DF_Q, CB_DF_K, CB_DF_V) + _bias_specs(t, 2),
        scratch_shapes=_kv_scratch(s, t, key_dim=DIFF_HALF, n_keys=2 * GROUP_HEADS),
        compiler_params=_params("parallel", "arbitrary"),
        name="diff_attention",
        **_group_out(bsz, s, t),
    )(lam_params, subln, proj, proj, proj, bias_diag, bias_sub)


def _compress_kernel(kcv_ref, pk_ref, pv_ref, wk1_ref, wk2_ref, wv1_ref, wv2t_ref, kc_ref, vct_ref):
    n_chunk = kc_ref.shape[1]

    branches = ((0, pk_ref, wk1_ref), (HEAD_DIM, pv_ref, wv1_ref))
    tops = [jnp.zeros((n_chunk, w1_ref.shape[1]), jnp.float32) for _, _, w1_ref in branches]
    bots = list(tops)
    for l in range(CMP_STRIDE):
        tokens = kcv_ref[0, pl.ds(l, n_chunk, stride=CMP_STRIDE), :]
        l2 = CMP_STRIDE + l
        for n, (col0, p_ref, w1_ref) in enumerate(branches):
            x = tokens[:, col0:col0 + HEAD_DIM]
            tops[n] = tops[n] + _dot((x + p_ref[l:l + 1, :]).astype(_MXU),
                                     w1_ref[l * HEAD_DIM:(l + 1) * HEAD_DIM, :])
            bots[n] = bots[n] + _dot((x + p_ref[l2:l2 + 1, :]).astype(_MXU),
                                     w1_ref[l2 * HEAD_DIM:(l2 + 1) * HEAD_DIM, :])
    hidden = [jax.nn.gelu(top + pltpu.roll(bot, n_chunk - 1, axis=0)).astype(_MXU) for top, bot in zip(tops, bots)]
    kc_ref[0] = _dot(hidden[0], wk2_ref[...])
    vct_ref[0] = _dot_nt(wv2t_ref[...], hidden[1])


def _compress(kcv, pos_k, pos_v, wk1, wk2, wv1, wv2t):
    bsz, s, width = kcv.shape
    n_chunk = s // CMP_STRIDE

    def full(a):
        return pl.BlockSpec(a.shape, lambda b: (0, 0))

    return pl.pallas_call(
        _compress_kernel,
        grid=(bsz,),
        in_specs=[pl.BlockSpec((1, s, width), lambda b: (b, 0, 0)),
                  full(pos_k), full(pos_v), full(wk1), full(wk2), full(wv1), full(wv2t)],
        out_specs=[pl.BlockSpec((1, n_chunk, HEAD_DIM), lambda b: (b, 0, 0)),
                   pl.BlockSpec((1, HEAD_DIM, n_chunk), lambda b: (b, 0, 0))],
        out_shape=[jax.ShapeDtypeStruct((bsz, n_chunk, HEAD_DIM), jnp.float32),
                   jax.ShapeDtypeStruct((bsz, HEAD_DIM, n_chunk), jnp.float32)],
        compiler_params=_params("parallel"),
        name="nsa_compress",
    )(kcv, pos_k, pos_v, wk1, wk2, wv1, wv2t)


def _nsa_kernel(q_ref, kva_ref, kvb_ref, gq_ref, kc_ref, vct_ref, bd_ref, bs_ref, bc_ref, cover_ref, e_ref,
                o_ref, ks_ref, vst_ref, kw_ref, vwt_ref):
    t = q_ref.shape[1] // Q_TILES
    i = pl.program_id(1)
    key, qry = _tile_iotas(t)
    causal = key <= qry
    ks_col, vs_col, kw_col, vw_col, gate_col = 2 * HEAD_DIM, 3 * HEAD_DIM, 0, HEAD_DIM, 2 * HEAD_DIM

    @pl.when(i == 0)
    def _():
        ks_ref[...] = kva_ref[0, :, ks_col:ks_col + HEAD_DIM].astype(ks_ref.dtype)
        kw_ref[...] = kvb_ref[0, :, kw_col:kw_col + HEAD_DIM].astype(kw_ref.dtype)
        _fill_values_t(vst_ref, kva_ref, vs_col, t)
        _fill_values_t(vwt_ref, kvb_ref, vw_col, t)

    def tile(k_ref, vt_ref, j, bias_ref=None, emask=None):
        n = GROUP_HEADS
        emask = emask() if callable(emask) else emask
        return _KeyTile([_key_rows(k_ref, j, t)] * n, [vt_ref[j]] * n,
                        None if bias_ref is None else [bias_ref[h] for h in _HEADS],
                        None if emask is None else [emask] * n)

    def query_step(step):
        qts = _queries_t(q_ref, HEAD_DIM, t)
        n_cmp = kc_ref.shape[1]
        n_slc = cover_ref.shape[0]
        kc = kc_ref[0].astype(_MXU)
        vct = vct_ref[0].astype(_MXU)
        c_row = lax.broadcasted_iota(jnp.int32, (n_cmp, t), 0)
        c_col = lax.broadcasted_iota(jnp.int32, (n_cmp, t), 1)
        s_row = lax.broadcasted_iota(jnp.int32, (n_slc, t), 0)
        s_col = lax.broadcasted_iota(jnp.int32, (n_slc, t), 1)
        tiles = [step * Q_TILES + r for r in range(Q_TILES)]

        o_cmp, importance = [], []
        for r, k in enumerate(tiles):
            visible = c_col + k * t >= c_row * CMP_STRIDE + (CMP_LEN - 1)
            cmp_scores = [_dot(kc, qts[r][h]) for h in _HEADS]
            cmp_probs = []
            p_sum = jnp.zeros((n_cmp, t), jnp.float32)
            for h in _HEADS:
                sc = jnp.where(visible, cmp_scores[h] + bc_ref[h, :, r * t:(r + 1) * t], NEG_INF)
                e = jnp.where(visible, jnp.exp2(sc - jnp.max(sc, axis=0, keepdims=True)), 0.0)
                p = e / jnp.maximum(jnp.sum(e, axis=0, keepdims=True), TINY)
                cmp_probs.append(p.astype(_MXU))
                p_sum = p_sum + p
            o_cmp.append([_dot(vct, cmp_probs[h]) for h in _HEADS])
            importance.append(_dot(cover_ref[...], p_sum, precision=lax.Precision.HIGHEST))

        n_back = WINDOW // t
        jobs = []
        for r, k in enumerate(tiles):
            window = [functools.partial(tile, kw_ref, vwt_ref, k - n_back, None, qry < key)] if k >= n_back else []
            for back in range(min(n_back - 1, k), 0, -1):
                window.append(functools.partial(tile, kw_ref, vwt_ref, k - back, bs_ref if back == 1 else None))
            window.append(functools.partial(tile, kw_ref, vwt_ref, k, bd_ref, causal))
            jobs.append((GROUP_HEADS, qts[r], [window]))
        o_win = [[_softmax_out(st) for st in states] for states in _softmax_jobs(t, jobs)]

        jobs = []
        for r, k in enumerate(tiles):
            own = jnp.right_shift(s_col + k * t, int(math.log2(SLC_LEN)))
            score = jnp.where(s_row == own, FORCE, jnp.where(s_row < own, importance[r], NEG_INF))
            sel = _top_k_rows(score, s_row.astype(jnp.float32), min(SLC_TOPN, n_slc)).astype(_MXU)

            def chosen(sel, j, diagonal=False):
                mask = _dot(_key_rows(e_ref, j, t), sel) > 0.5
                return jnp.logical_and(mask, causal) if diagonal else mask

            chosen = functools.partial(chosen, sel)
            groups = _pairs([functools.partial(tile, ks_ref, vst_ref, j, None, functools.partial(chosen, j))
                             for j in range(k - 1)])
            groups.append(([functools.partial(tile, ks_ref, vst_ref, k - 1, bs_ref, functools.partial(chosen, k - 1))]
                           if k else [])
                          + [functools.partial(tile, ks_ref, vst_ref, k, bd_ref, functools.partial(chosen, k, True))])
            jobs.append((GROUP_HEADS, qts[r], groups))
        o_slc = [[_softmax_out(st) for st in states] for states in _softmax_jobs(t, jobs)]

        gates = _transposed(gq_ref[0, :, (gate_col // LANES) * LANES:(gate_col // LANES + 1) * LANES])
        gates = 1.0 / (1.0 + jnp.exp(-gates[gate_col % LANES:gate_col % LANES + N_GATES + 4]))
        outs = []
        for r in range(Q_TILES):
            heads = []
            for h in _HEADS:
                g = [gates[br * GROUP_HEADS + h:br * GROUP_HEADS + h + 1, r * t:(r + 1) * t] for br in range(3)]
                heads.append(g[0] * o_cmp[r][h] + g[1] * o_slc[r][h] + g[2] * o_win[r][h])
            outs.append(heads)
        _store_heads(o_ref, outs)

    _per_step(i, kva_ref.shape[1] // (Q_TILES * t), query_step)


def _nsa(proj, kc, vc_t, bias_diag, bias_sub, bias_cmp, cover_t, expand, t=TILE):
    bsz, s, _ = proj.shape
    n_cmp = kc.shape[1]
    kv_scratch = [pltpu.VMEM((s, HEAD_DIM), _MXU), pltpu.VMEM((s // t, ACC_ROWS, t), _MXU)]
    return pl.pallas_call(
        _nsa_kernel,
        grid=(bsz, s // (Q_TILES * t)),
        in_specs=_group_specs(t, s, CB_NS_Q, CB_NS_A, CB_NS_B)
        + [pl.BlockSpec((1, Q_TILES * t, GROUP_WIDTH), lambda b, i: (b, i, CB_NS_B)),
           pl.BlockSpec((1, n_cmp, HEAD_DIM), lambda b, i: (b, 0, 0)),
           pl.BlockSpec((1, HEAD_DIM, n_cmp), lambda b, i: (b, 0, 0))]
        + _bias_specs(t, 1)
        + [pl.BlockSpec((GROUP_HEADS, n_cmp, Q_TILES * t), lambda b, i: (0, 0, i)),
           pl.BlockSpec(cover_t.shape, lambda b, i: (0, 0)),
           pl.BlockSpec(expand.shape, lambda b, i: (0, 0))],
        scratch_shapes=kv_scratch + kv_scratch,
        compiler_params=_params("parallel", "arbitrary"),
        name="nsa",
        **_group_out(bsz, s, t),
    )(proj, proj, proj, proj, kc, vc_t, bias_diag, bias_sub, bias_cmp, cover_t, expand)


def _nsa_constants(s):
    n_cmp = (s - CMP_LEN) // CMP_STRIDE + 1
    n_slc = s // SLC_LEN
    assert n_cmp + 1 == s // CMP_STRIDE and n_slc % SUBLANES == 0
    c_start = np.arange(n_cmp) * CMP_STRIDE
    s_start = np.arange(n_slc) * SLC_LEN
    cover = np.clip(np.minimum((c_start + CMP_LEN - 1)[:, None], (s_start + SLC_LEN - 1)[None, :])
                    - np.maximum(c_start[:, None], s_start[None, :]) + 1, 0, None) / CMP_LEN
    cover_t = np.zeros((n_slc, n_cmp + 1), np.float32)
    cover_t[:, :n_cmp] = cover.T
    expand = (np.arange(s)[:, None] // SLC_LEN == np.arange(n_slc)[None, :]).astype(np.float32)
    return jnp.asarray(cover_t), jnp.asarray(expand, _MXU)


def kernel(x, w_in, w_out, w_up, w_down, norm_attn, norm_mlp, cmp_pos_k, cmp_pos_v, cmp_k_w1, cmp_k_w2,
           cmp_v_w1, cmp_v_w2, diff_lambda, diff_subln, rel_bias, final_norm):
    bsz, s, d = x.shape
    depth = w_in.shape[0]
    t = TILE
    n_chunk = s // CMP_STRIDE
    assert s % MOBA_BLOCK == 0 and MOBA_BLOCK % t == 0 and WINDOW % t == 0 and t >= MAX_DISTANCE

    w_in_p = jnp.concatenate([w_in[:, :, :COLS_BEFORE_PAD],
                              jnp.zeros((depth, d, PAD_COLS), w_in.dtype),
                              w_in[:, :, COLS_BEFORE_PAD:]], axis=2).astype(_MXU)
    assert w_in_p.shape[2] == D_IN_PAD
    w_out_c, w_up_c, w_down_c = w_out.astype(_MXU), w_up.astype(_MXU), w_down.astype(_MXU)
    wk1, wk2 = cmp_k_w1.astype(_MXU), cmp_k_w2.astype(_MXU)
    wv1, wv2t = cmp_v_w1.astype(_MXU), jnp.swapaxes(cmp_v_w2, 1, 2).astype(_MXU)

    tiles = dict(n_heads=rel_bias.shape[1], head0=0, rows=t, cols=t, col_tile=t, row_stride=-1, col_stride=1)
    bias_diag = _bias_table(rel_bias, offset=0, **tiles)
    bias_sub = _bias_table(rel_bias, offset=t, **tiles)
    bias_cmp = _bias_table(rel_bias, n_heads=GROUP_HEADS, head0=GROUP_HEADS, rows=n_chunk, cols=s,
                           col_tile=t, row_stride=-CMP_STRIDE, col_stride=1, offset=-(CMP_LEN - 1))
    cover_t, expand = _nsa_constants(s)
    col_scale = np.ones((1, D_IN_PAD), np.float32)
    for cb, width in ((CB_SB_Q, HEAD_DIM), (CB_MB_Q, HEAD_DIM), (CB_NS_Q, HEAD_DIM), (CB_DF_Q, DIFF_HALF)):
        col_scale[:, cb * GROUP_WIDTH:(cb + 1) * GROUP_WIDTH] = width ** -0.5 * LOG2E
    col_scale = jnp.asarray(col_scale)

    x2 = x.reshape(bsz * s, d)
    for layer in range(depth):
        proj, kcv = _norm_matmul(x2, norm_attn[layer][None], w_in_p[layer], col_scale,
                                 side_col=CB_NS_A * GROUP_WIDTH, side_width=2 * HEAD_DIM)
        proj = proj.reshape(bsz, s, D_IN_PAD)
        o_sb = _stick_breaking(proj)
        o_mb = _moba(proj, bias_diag, bias_sub)
        kc, vc_t = _compress(kcv.reshape(bsz, s, 2 * HEAD_DIM), cmp_pos_k[layer], cmp_pos_v[layer],
                             wk1[layer], wk2[layer], wv1[layer], wv2t[layer])
        o_ns = _nsa(proj, kc, vc_t, bias_diag, bias_sub, bias_cmp, cover_t, expand)
        lambda_init = 0.8 - 0.6 * math.exp(-0.3 * layer)
        o_df = _diff(proj, diff_lambda[layer], diff_subln[layer][:, None], bias_diag, bias_sub, lambda_init)
        groups = [o.reshape(bsz * s, GROUP_WIDTH) for o in (o_sb, o_mb, o_ns, o_df)]
        x2 = _out_mlp(x2, groups, w_out_c[layer], norm_mlp[layer][None], w_up_c[layer], w_down_c[layer],
                      final_norm[None], final_norm=(layer == depth - 1))
    return x2.reshape(bsz, s, d)
```

```python
import functools
import math
from typing import Any, NamedTuple, Optional, Sequence

import numpy as np
import jax
import jax.numpy as jnp
from jax import lax
from jax.experimental import pallas as pl
from jax.experimental.pallas import tpu as pltpu

HEAD_DIM = 64
GROUP_HEADS = 4
GROUP_WIDTH = GROUP_HEADS * HEAD_DIM
NORM_EPS = 1e-6
NEG_INF = -1e30
BIG = 1e30
FORCE = 1e30
TINY = 1e-30
PICKED = -3e38
LOG2E = math.log2(math.e)
N_BUCKETS = 32
MAX_DISTANCE = 128
MOBA_BLOCK = 256
MOBA_TOPK = 3
CMP_LEN = 32
CMP_STRIDE = 16
SLC_LEN = 64
SLC_TOPN = 4
WINDOW = 512
DIFF_HALF = HEAD_DIM // 2
LANES = 128
SUBLANES = 8
TILE = 256
BF16_ROWS = 16
ACC_ROWS = HEAD_DIM + BF16_ROWS
N_GATES = 3 * GROUP_HEADS
COLS_BEFORE_PAD = 9 * GROUP_WIDTH - 2 * HEAD_DIM + N_GATES
PAD_COLS = 2 * HEAD_DIM - N_GATES
CB_SB_Q, CB_SB_K, CB_SB_V, CB_MB_Q, CB_MB_K, CB_MB_V, CB_NS_Q, CB_NS_A, CB_NS_B, CB_DF_Q, CB_DF_K, CB_DF_V = range(12)
D_IN_PAD = 12 * GROUP_WIDTH

_MXU = jnp.bfloat16
_VMEM_LIMIT = 56 * 1024 * 1024
_HEADS = range(GROUP_HEADS)
Q_TILES = 4
LOOKAHEAD = 2


def _dot(a, b, precision=None):
    return jnp.dot(a, b, precision=precision, preferred_element_type=jnp.float32)


def _dot_nt(a, b, precision=None):
    return lax.dot_general(a, b, (((1,), (1,)), ((), ())), precision=precision,
                           preferred_element_type=jnp.float32)


def _rms(x, g):
    return x * lax.rsqrt(jnp.mean(x * x, axis=-1, keepdims=True) + NORM_EPS) * g


def _params(*sem):
    return pltpu.CompilerParams(dimension_semantics=sem, vmem_limit_bytes=_VMEM_LIMIT)


def _norm_matmul_kernel(x_ref, g_ref, w_ref, scale_ref, o_ref, side_ref, *, tn, side_col):
    h = _rms(x_ref[...], g_ref[...]).astype(_MXU)
    for j in range(w_ref.shape[1] // tn):
        cols = slice(j * tn, (j + 1) * tn)
        acc = _dot(h, w_ref[:, cols])
        o_ref[:, cols] = (acc * scale_ref[:, cols]).astype(o_ref.dtype)
        if j * tn <= side_col < (j + 1) * tn:
            side_ref[...] = acc[:, side_col - j * tn:side_col - j * tn + side_ref.shape[1]]


def _norm_matmul(x, g, w, layer, col_scale, *, side_col, side_width, tm=512, tn=1024):
    m, d = x.shape
    n = w.shape[2]
    assert side_col % LANES == 0 and side_col // tn == (side_col + side_width - 1) // tn
    return pl.pallas_call(
        functools.partial(_norm_matmul_kernel, tn=tn, side_col=side_col),
        grid=(m // tm,),
        in_specs=[pl.BlockSpec((tm, d), lambda i: (i, 0)),
                  pl.BlockSpec((1, d), lambda i: (0, 0)),
                  pl.BlockSpec((None, d, n), lambda i: (layer, 0, 0)),
                  pl.BlockSpec((1, n), lambda i: (0, 0))],
        out_specs=[pl.BlockSpec((tm, n), lambda i: (i, 0)),
                   pl.BlockSpec((tm, side_width), lambda i: (i, 0))],
        out_shape=[jax.ShapeDtypeStruct((m, n), _MXU),
                   jax.ShapeDtypeStruct((m, side_width), jnp.float32)],
        compiler_params=_params("parallel"),
        name="norm_in_proj",
    )(x, g, w, col_scale)


def _out_mlp_kernel(x_ref, a_ref, b_ref, c_ref, d_ref, wo_ref, g_ref, wu_ref, wd_ref, gf_ref, o_ref,
                    *, final_norm, tf):
    mixed = jnp.concatenate([a_ref[...], b_ref[...], c_ref[...], d_ref[...]], axis=1)
    y = x_ref[...] + _dot(mixed, wo_ref[...])
    h = _rms(y, g_ref[...]).astype(_MXU)
    for c in range(wu_ref.shape[1] // tf):
        u = jnp.square(jnp.maximum(_dot(h, wu_ref[:, c * tf:(c + 1) * tf]), 0.0))
        y = y + _dot(u.astype(_MXU), wd_ref[c * tf:(c + 1) * tf, :])
    if final_norm:
        y = _rms(y, gf_ref[...])
    o_ref[...] = y


def _out_mlp(x, groups, w_out, g, w_up, w_down, layer, g_final, *, final_norm, tm=512, tf=2048):
    m, d = x.shape
    f = w_up.shape[2]
    gspec = pl.BlockSpec((tm, GROUP_WIDTH), lambda i: (i, 0))
    row = pl.BlockSpec((1, d), lambda i: (0, 0))

    def resident(rows, cols):
        return pl.BlockSpec((None, rows, cols), lambda i: (layer, 0, 0), pipeline_mode=pl.Buffered(1))

    return pl.pallas_call(
        functools.partial(_out_mlp_kernel, final_norm=final_norm, tf=tf),
        grid=(m // tm,),
        in_specs=[pl.BlockSpec((tm, d), lambda i: (i, 0)), gspec, gspec, gspec, gspec,
                  resident(d, d), row, resident(d, f), resident(f, d), row],
        out_specs=pl.BlockSpec((tm, d), lambda i: (i, 0)),
        out_shape=jax.ShapeDtypeStruct((m, d), jnp.float32),
        compiler_params=_params("parallel"),
        name="out_proj_mlp_residual",
    )(x, *groups, w_out, g, w_up, w_down, g_final)


def _t5_bucket(dist):
    n = jnp.maximum(dist, 0)
    max_exact = N_BUCKETS // 2
    nf = jnp.maximum(n, 1).astype(jnp.float32)
    large = max_exact + (jnp.log(nf / max_exact) / math.log(MAX_DISTANCE / max_exact)
                         * (N_BUCKETS - max_exact)).astype(jnp.int32)
    large = jnp.minimum(large, N_BUCKETS - 1)
    return jnp.where(n < max_exact, n, large)


def _bias_kernel(tab_ref, o_ref, *, row_stride, col_stride, offset, head0):
    nh, tr, tc = o_ref.shape
    rows = lax.broadcasted_iota(jnp.int32, (tr, tc), 0)
    cols = lax.broadcasted_iota(jnp.int32, (tr, tc), 1) + pl.program_id(0) * tc
    bucket = _t5_bucket(rows * row_stride + cols * col_stride + offset)
    for h in range(nh):
        out = jnp.zeros((tr, tc), jnp.float32)
        for bkt in range(N_BUCKETS):
            out = jnp.where(bucket == bkt, tab_ref[bkt, head0 + h], out)
        o_ref[h] = (out - tab_ref[N_BUCKETS - 1, head0 + h]) * LOG2E


def _bias_table(table, *, n_heads, head0, rows, cols, col_tile, row_stride, col_stride, offset):
    return pl.pallas_call(
        functools.partial(_bias_kernel, row_stride=row_stride, col_stride=col_stride,
                          offset=offset, head0=head0),
        grid=(cols // col_tile,),
        in_specs=[pl.BlockSpec(memory_space=pltpu.SMEM)],
        out_specs=pl.BlockSpec((n_heads, rows, col_tile), lambda i: (0, 0, i)),
        out_shape=jax.ShapeDtypeStruct((n_heads, rows, cols), jnp.float32),
        compiler_params=_params("parallel"),
        name="t5_bias_tiles",
    )(table)


def _softmax_init(t):
    return (jnp.full((1, t), NEG_INF, jnp.float32), jnp.zeros((ACC_ROWS, t), jnp.float32))


class _KeyTile(NamedTuple):
    kts: Sequence[Any]
    vts: Sequence[Any]
    biases: Optional[Sequence[Any]] = None
    emasks: Optional[Sequence[Any]] = None
    qmasks: Optional[Sequence[Any]] = None


def _round_robin(lists):
    out = []
    for rank in range(max(map(len, lists), default=0)):
        out.extend(items[rank] for items in lists if rank < len(items))
    return out


def _softmax_jobs(t, jobs):
    built = {}

    def tiles_of(job, g):
        if (job, g) not in built:
            built[job, g] = [make() for make in jobs[job][2][g]]
        return built[job, g]

    def scores_of(job, g, c):
        qts = jobs[job][1]
        row = []
        for tile in tiles_of(job, g):
            s = _dot(tile.kts[c], qts[c])
            if tile.biases is not None:
                s = s + tile.biases[c]
            if tile.emasks is not None:
                s = jnp.where(tile.emasks[c], s, NEG_INF)
            row.append(s.astype(_MXU))
        return row

    def update(state, job, g, c, scores):
        m, acc = state
        m_new = m
        for tile, s in zip(tiles_of(job, g), scores):
            tile_max = jnp.max(s, axis=0, keepdims=True).astype(jnp.float32)
            if tile.qmasks is not None:
                tile_max = jnp.where(tile.qmasks[c], tile_max, NEG_INF)
            m_new = jnp.maximum(m_new, tile_max)
        seen = m_new > 0.5 * NEG_INF
        acc = jnp.exp2(m - m_new) * acc
        for tile, s in zip(tiles_of(job, g), scores):
            ok = seen if tile.qmasks is None else jnp.logical_and(seen, tile.qmasks[c])
            acc = acc + _dot(tile.vts[c], jnp.exp2(s - jnp.where(ok, m_new, BIG).astype(_MXU)))
        return m_new, acc

    units = _round_robin([[(job, g, c) for g in range(len(groups)) for c in range(n)]
                          for job, (n, _, groups) in enumerate(jobs)])
    lookahead = LOOKAHEAD * len(jobs)
    states = [[_softmax_init(t) for _ in range(n)] for n, _, _ in jobs]
    pending = {k: scores_of(*units[k]) for k in range(min(lookahead, len(units)))}
    for k, (job, g, c) in enumerate(units):
        if k + lookahead < len(units):
            pending[k + lookahead] = scores_of(*units[k + lookahead])
        states[job][c] = update(states[job][c], job, g, c, pending.pop(k))
    return states


def _pairs(items):
    return [items[p:p + 2] for p in range(0, len(items), 2)]


def _per_step(i, n_steps, body):
    for step in range(n_steps):
        pl.when(i == step)(functools.partial(body, step))


def _softmax_out(state):
    acc = state[1]
    return acc[:HEAD_DIM] / jnp.maximum(acc[HEAD_DIM:HEAD_DIM + 1], TINY)


def _top_k_rows(score, row_f, k):
    sel = jnp.zeros(score.shape, jnp.float32)
    for _ in range(k):
        mx = jnp.max(score, axis=0, keepdims=True)
        idx = jnp.min(jnp.where(score == mx, row_f, float(score.shape[0])), axis=0, keepdims=True)
        pick = row_f == idx
        sel = jnp.where(pick, 1.0, sel)
        score = jnp.where(pick, PICKED, score)
    return sel


def _tile_iotas(t):
    return (lax.broadcasted_iota(jnp.int32, (t, t), 0), lax.broadcasted_iota(jnp.int32, (t, t), 1))


def _key_rows(ref, j, t):
    if isinstance(j, int):
        return ref[j * t:(j + 1) * t, :]
    return ref[pl.ds(pl.multiple_of(j * t, t), t), :]


def _transposed(ref_block):
    return ref_block.astype(jnp.float32).T


def _queries_t(q_ref, width, t):
    qt = _transposed(q_ref[0]).astype(_MXU)
    return [[qt[c * width:(c + 1) * width, r * t:(r + 1) * t] for c in range(GROUP_WIDTH // width)]
            for r in range(Q_TILES)]


def _fill_values_t(vt_ref, v_ref, col0, t):
    n_tiles, rows, _ = vt_ref.shape
    lane_block = (col0 // LANES) * LANES
    for c in range(n_tiles):
        blk = _transposed(v_ref[0, c * t:(c + 1) * t, lane_block:lane_block + LANES])
        vt_ref[c, 0:HEAD_DIM, :] = blk[col0 - lane_block:col0 - lane_block + HEAD_DIM].astype(vt_ref.dtype)
        if rows == ACC_ROWS:
            first = lax.broadcasted_iota(jnp.int32, (rows - HEAD_DIM, t), 0) == 0
            vt_ref[c, HEAD_DIM:rows, :] = jnp.where(first, 1.0, 0.0).astype(vt_ref.dtype)


def _group_specs(t, s, cb_q, cb_k, cb_v):
    return [pl.BlockSpec((1, Q_TILES * t, GROUP_WIDTH), lambda b, i: (b, i, cb_q)),
            pl.BlockSpec((1, s, GROUP_WIDTH), lambda b, i: (b, 0, cb_k)),
            pl.BlockSpec((1, s, GROUP_WIDTH), lambda b, i: (b, 0, cb_v))]


def _group_out(bsz, s, t):
    return dict(out_specs=pl.BlockSpec((1, Q_TILES * t, GROUP_WIDTH), lambda b, i: (b, i, 0)),
                out_shape=jax.ShapeDtypeStruct((bsz, s, GROUP_WIDTH), _MXU))


def _bias_specs(t, head_group):
    spec = pl.BlockSpec((GROUP_HEADS, t, t), lambda b, i: (head_group, 0, 0))
    return [spec, spec]


def _store_heads(o_ref, outs_t):
    tiles = [jnp.concatenate(heads, axis=0) for heads in outs_t]
    o_ref[0] = jnp.concatenate(tiles, axis=1).T.astype(o_ref.dtype)


def _sb_kernel(q_ref, k_ref, v_ref, o_ref, kb_ref, vt_ref):
    t = q_ref.shape[1] // Q_TILES
    i = pl.program_id(1)
    key, qry = _tile_iotas(t)
    strict = key < qry
    later = jnp.where(qry > key, 1.0, 0.0).astype(_MXU)

    @pl.when(i == 0)
    def _():
        for h in _HEADS:
            kb_ref[h] = k_ref[0, :, h * HEAD_DIM:(h + 1) * HEAD_DIM].astype(kb_ref.dtype)
            _fill_values_t(vt_ref.at[h], v_ref, h * HEAD_DIM, t)

    def query_step(step):
        qts = _queries_t(q_ref, HEAD_DIM, t)
        units = _round_robin([[(r, step * Q_TILES + r, pair, h)
                               for pair in _pairs(list(range(step * Q_TILES + r, -1, -1))) for h in _HEADS]
                              for r in range(Q_TILES)])
        zero = (jnp.zeros((HEAD_DIM, t), jnp.float32), jnp.zeros((1, t), jnp.float32))
        carry = [[zero] * GROUP_HEADS for _ in range(Q_TILES)]
        zs, log_keeps, suffixes = {}, {}, {}

        def scores(u):
            r, _, pair, h = units[u]
            zs[u] = [_dot(_key_rows(kb_ref.at[h], j, t), qts[r][h]) for j in pair]

        def keeps(u):
            _, k, pair, _ = units[u]
            log_keeps[u], suffixes[u] = [], []
            for j, z in zip(pair, zs[u]):
                neg = -z
                log_keep = jnp.minimum(neg, 0.0) - jnp.log2(1.0 + jnp.exp2(jnp.minimum(z, neg)))
                if j == k:
                    log_keep = jnp.where(strict, log_keep, 0.0)
                log_keeps[u].append(log_keep)
                suffixes[u].append(_dot(later, log_keep.astype(_MXU)))

        def values(u):
            r, k, pair, h = units[u]
            acc, run = carry[r][h]
            weights = []
            for j, z, log_keep, suffix in zip(pair, zs.pop(u), log_keeps.pop(u), suffixes.pop(u)):
                a = jnp.exp2(z + log_keep + suffix + run)
                if j == k:
                    a = jnp.where(strict, a, 0.0)
                weights.append(a.astype(_MXU))
                run = run + (suffix[0:1] + log_keep[0:1])
            for j, w in zip(pair, weights):
                acc = acc + _dot(vt_ref[h, j], w)
            carry[r][h] = (acc, run)

        stages = (scores, keeps, values)
        for tick in range(len(units) + len(stages) - 1):
            for lag, stage in enumerate(stages):
                if 0 <= tick - lag < len(units):
                    stage(tick - lag)
        _store_heads(o_ref, [[c[0] for c in tile_carry] for tile_carry in carry])

    _per_step(i, k_ref.shape[1] // (Q_TILES * t), query_step)


def _stick_breaking(proj, t=TILE):
    bsz, s, _ = proj.shape
    return pl.pallas_call(
        _sb_kernel,
        grid=(bsz, s // (Q_TILES * t)),
        in_specs=_group_specs(t, s, CB_SB_Q, CB_SB_K, CB_SB_V),
        scratch_shapes=[pltpu.VMEM((GROUP_HEADS, s, HEAD_DIM), _MXU),
                        pltpu.VMEM((GROUP_HEADS, s // t, HEAD_DIM, t), _MXU)],
        compiler_params=_params("parallel", "arbitrary"),
        name="stick_breaking",
        **_group_out(bsz, s, t),
    )(proj, proj, proj)


def _moba_kernel(q_ref, k_ref, v_ref, bd_ref, bs_ref, o_ref, kb_ref, vt_ref, km_ref):
    t = q_ref.shape[1] // Q_TILES
    n_blk = k_ref.shape[1] // MOBA_BLOCK
    tiles_per_blk = MOBA_BLOCK // t
    blk_shift = int(math.log2(tiles_per_blk))
    i = pl.program_id(1)

    @pl.when(i == 0)
    def _():
        km_ref[...] = jnp.zeros_like(km_ref)
        for h in _HEADS:
            lo, hi = h * HEAD_DIM, (h + 1) * HEAD_DIM
            kb_ref[h] = k_ref[0, :, lo:hi].astype(kb_ref.dtype)
            _fill_values_t(vt_ref.at[h], v_ref, lo, t)
            for n in range(n_blk):
                blk = k_ref[0, n * MOBA_BLOCK:(n + 1) * MOBA_BLOCK, lo:hi]
                km_ref[h, n:n + 1, :] = jnp.mean(blk.astype(jnp.float32), axis=0, keepdims=True)

    def query_step(step):
        key, qry = _tile_iotas(t)
        causal = key <= qry
        blk_row = lax.broadcasted_iota(jnp.int32, (km_ref.shape[1], t), 0)
        qt = _transposed(q_ref[0])
        jobs = []
        for r in range(Q_TILES):
            k = step * Q_TILES + r
            own = k >> blk_shift
            qts, sels = [], []
            for h in _HEADS:
                qf = qt[h * HEAD_DIM:(h + 1) * HEAD_DIM, r * t:(r + 1) * t]
                qts.append(qf.astype(_MXU))
                gate = _dot(km_ref[h], qf, precision=lax.Precision.HIGHEST)
                gate = jnp.where(blk_row < own, gate, NEG_INF)
                sel = _top_k_rows(gate, blk_row.astype(jnp.float32), min(MOBA_TOPK, n_blk - 1))
                sels.append(jnp.where(blk_row < own, sel, 0.0))

            def tile(own, sels, j, bias_ref=None, emask=None):
                n = j >> blk_shift
                return _KeyTile([_key_rows(kb_ref.at[h], j, t) for h in _HEADS], [vt_ref[h, j] for h in _HEADS],
                                None if bias_ref is None else [bias_ref[h] for h in _HEADS],
                                None if emask is None else [emask] * GROUP_HEADS,
                                None if n == own else [sels[h][n:n + 1] > 0.5 for h in _HEADS])

            tile = functools.partial(tile, own, sels)
            groups = _pairs([functools.partial(tile, j) for j in range(k - 1)])
            groups.append(([functools.partial(tile, k - 1, bs_ref)] if k else [])
                          + [functools.partial(tile, k, bd_ref, causal)])
            jobs.append((GROUP_HEADS, qts, groups))
        _store_heads(o_ref, [[_softmax_out(st) for st in states] for states in _softmax_jobs(t, jobs)])

    _per_step(i, k_ref.shape[1] // (Q_TILES * t), query_step)


def _kv_scratch(s, t, key_dim=HEAD_DIM, n_keys=GROUP_HEADS):
    return [pltpu.VMEM((n_keys, s, key_dim), _MXU), pltpu.VMEM((GROUP_HEADS, s // t, ACC_ROWS, t), _MXU)]


def _moba(proj, bias_diag, bias_sub, t=TILE):
    bsz, s, _ = proj.shape
    n_blk_pad = -(-(s // MOBA_BLOCK) // SUBLANES) * SUBLANES
    return pl.pallas_call(
        _moba_kernel,
        grid=(bsz, s // (Q_TILES * t)),
        in_specs=_group_specs(t, s, CB_MB_Q, CB_MB_K, CB_MB_V) + _bias_specs(t, 0),
        scratch_shapes=_kv_scratch(s, t) + [pltpu.VMEM((GROUP_HEADS, n_blk_pad, HEAD_DIM), jnp.float32)],
        compiler_params=_params("parallel", "arbitrary"),
        name="moba",
        **_group_out(bsz, s, t),
    )(proj, proj, proj, bias_diag, bias_sub)


def _diff_kernel(lam_ref, g_ref, q_ref, k_ref, v_ref, bd_ref, bs_ref, o_ref, kb_ref, vt_ref, *, lambda_init):
    t = q_ref.shape[1] // Q_TILES
    i = pl.program_id(1)
    key, qry = _tile_iotas(t)
    causal = key <= qry
    lv = lam_ref[...]
    lam = (jnp.exp(jnp.sum(lv[0:1] * lv[1:2], keepdims=True))
           - jnp.exp(jnp.sum(lv[2:3] * lv[3:4], keepdims=True)) + lambda_init)
    halves = range(2 * GROUP_HEADS)

    @pl.when(i == 0)
    def _():
        for c in halves:
            kb_ref[c] = k_ref[0, :, c * DIFF_HALF:(c + 1) * DIFF_HALF].astype(kb_ref.dtype)
        for h in _HEADS:
            _fill_values_t(vt_ref.at[h], v_ref, h * HEAD_DIM, t)

    def tile(j, bias_ref=None, emask=None):
        n = len(halves)
        return _KeyTile([_key_rows(kb_ref.at[c], j, t) for c in halves], [vt_ref[c // 2, j] for c in halves],
                        None if bias_ref is None else [bias_ref[c // 2] for c in halves],
                        None if emask is None else [emask] * n)

    def query_step(step):
        qts = _queries_t(q_ref, DIFF_HALF, t)
        jobs = []
        for r in range(Q_TILES):
            k = step * Q_TILES + r
            groups = _pairs([functools.partial(tile, j) for j in range(k - 1)])
            groups.append(([functools.partial(tile, k - 1, bs_ref)] if k else [])
                          + [functools.partial(tile, k, bd_ref, causal)])
            jobs.append((len(halves), qts[r], groups))
        outs = []
        for states in _softmax_jobs(t, jobs):
            heads = []
            for h in _HEADS:
                o = _softmax_out(states[2 * h]) - lam * _softmax_out(states[2 * h + 1])
                o = o * lax.rsqrt(jnp.mean(o * o, axis=0, keepdims=True) + NORM_EPS) * g_ref[...]
                heads.append(o * (1.0 - lambda_init))
            outs.append(heads)
        _store_heads(o_ref, outs)

    _per_step(i, k_ref.shape[1] // (Q_TILES * t), query_step)


def _diff(proj, lam_params, subln, bias_diag, bias_sub, lambda_init, t=TILE):
    bsz, s, _ = proj.shape
    return pl.pallas_call(
        functools.partial(_diff_kernel, lambda_init=lambda_init),
        grid=(bsz, s // (Q_TILES * t)),
        in_specs=[pl.BlockSpec(lam_params.shape, lambda b, i: (0, 0)),
                  pl.BlockSpec(subln.shape, lambda b, i: (0, 0))]
        + _group_specs(t, s, CB_DF_Q, CB_DF_K, CB_DF_V) + _bias_specs(t, 2),
        scratch_shapes=_kv_scratch(s, t, key_dim=DIFF_HALF, n_keys=2 * GROUP_HEADS),
        compiler_params=_params("parallel", "arbitrary"),
        name="diff_attention",
        **_group_out(bsz, s, t),
    )(lam_params, subln, proj, proj, proj, bias_diag, bias_sub)


def _compress_kernel(kcv_ref, pk_ref, pv_ref, wk1_ref, wk2_ref, wv1_ref, wv2t_ref, kc_ref, vct_ref):
    n_chunk = kc_ref.shape[1]

    branches = ((0, pk_ref, wk1_ref), (HEAD_DIM, pv_ref, wv1_ref))
    tops = [jnp.zeros((n_chunk, w1_ref.shape[1]), jnp.float32) for _, _, w1_ref in branches]
    bots = list(tops)
    for l in range(CMP_STRIDE):
        tokens = kcv_ref[0, pl.ds(l, n_chunk, stride=CMP_STRIDE), :]
        l2 = CMP_STRIDE + l
        for n, (col0, p_ref, w1_ref) in enumerate(branches):
            x = tokens[:, col0:col0 + HEAD_DIM]
            tops[n] = tops[n] + _dot((x + p_ref[l:l + 1, :]).astype(_MXU),
                                     w1_ref[l * HEAD_DIM:(l + 1) * HEAD_DIM, :])
            bots[n] = bots[n] + _dot((x + p_ref[l2:l2 + 1, :]).astype(_MXU),
                                     w1_ref[l2 * HEAD_DIM:(l2 + 1) * HEAD_DIM, :])
    hidden = [jax.nn.gelu(top + pltpu.roll(bot, n_chunk - 1, axis=0)).astype(_MXU) for top, bot in zip(tops, bots)]
    kc_ref[0] = _dot(hidden[0], wk2_ref[...])
    vct_ref[0] = _dot_nt(wv2t_ref[...], hidden[1])


def _compress(kcv, pos_k, pos_v, wk1, wk2, wv1, wv2t, layer):
    bsz, s, width = kcv.shape
    n_chunk = s // CMP_STRIDE

    def full(a):
        return pl.BlockSpec((None,) + a.shape[1:], lambda b: (layer, 0, 0))

    return pl.pallas_call(
        _compress_kernel,
        grid=(bsz,),
        in_specs=[pl.BlockSpec((1, s, width), lambda b: (b, 0, 0)),
                  full(pos_k), full(pos_v), full(wk1), full(wk2), full(wv1), full(wv2t)],
        out_specs=[pl.BlockSpec((1, n_chunk, HEAD_DIM), lambda b: (b, 0, 0)),
                   pl.BlockSpec((1, HEAD_DIM, n_chunk), lambda b: (b, 0, 0))],
        out_shape=[jax.ShapeDtypeStruct((bsz, n_chunk, HEAD_DIM), jnp.float32),
                   jax.ShapeDtypeStruct((bsz, HEAD_DIM, n_chunk), jnp.float32)],
        compiler_params=_params("parallel"),
        name="nsa_compress",
    )(kcv, pos_k, pos_v, wk1, wk2, wv1, wv2t)


def _nsa_kernel(q_ref, kva_ref, kvb_ref, gq_ref, kc_ref, vct_ref, bd_ref, bs_ref, bc_ref, cover_ref, e_ref,
                o_ref, ks_ref, vst_ref, kw_ref, vwt_ref):
    t = q_ref.shape[1] // Q_TILES
    i = pl.program_id(1)
    key, qry = _tile_iotas(t)
    causal = key <= qry
    ks_col, vs_col, kw_col, vw_col, gate_col = 2 * HEAD_DIM, 3 * HEAD_DIM, 0, HEAD_DIM, 2 * HEAD_DIM

    @pl.when(i == 0)
    def _():
        ks_ref[...] = kva_ref[0, :, ks_col:ks_col + HEAD_DIM].astype(ks_ref.dtype)
        kw_ref[...] = kvb_ref[0, :, kw_col:kw_col + HEAD_DIM].astype(kw_ref.dtype)
        _fill_values_t(vst_ref, kva_ref, vs_col, t)
        _fill_values_t(vwt_ref, kvb_ref, vw_col, t)

    def tile(k_ref, vt_ref, j, bias_ref=None, emask=None):
        n = GROUP_HEADS
        emask = emask() if callable(emask) else emask
        return _KeyTile([_key_rows(k_ref, j, t)] * n, [vt_ref[j]] * n,
                        None if bias_ref is None else [bias_ref[h] for h in _HEADS],
                        None if emask is None else [emask] * n)

    def query_step(step):
        qts = _queries_t(q_ref, HEAD_DIM, t)
        n_cmp = kc_ref.shape[1]
        n_slc = cover_ref.shape[0]
        kc = kc_ref[0].astype(_MXU)
        vct = vct_ref[0].astype(_MXU)
        c_row = lax.broadcasted_iota(jnp.int32, (n_cmp, t), 0)
        c_col = lax.broadcasted_iota(jnp.int32, (n_cmp, t), 1)
        s_row = lax.broadcasted_iota(jnp.int32, (n_slc, t), 0)
        s_col = lax.broadcasted_iota(jnp.int32, (n_slc, t), 1)
        tiles = [step * Q_TILES + r for r in range(Q_TILES)]

        o_cmp, importance = [], []
        for r, k in enumerate(tiles):
            visible = c_col + k * t >= c_row * CMP_STRIDE + (CMP_LEN - 1)
            cmp_scores = [_dot(kc, qts[r][h]) for h in _HEADS]
            cmp_probs = []
            p_sum = jnp.zeros((n_cmp, t), jnp.float32)
            for h in _HEADS:
                sc = jnp.where(visible, cmp_scores[h] + bc_ref[h, :, r * t:(r + 1) * t], NEG_INF)
                e = jnp.where(visible, jnp.exp2(sc - jnp.max(sc, axis=0, keepdims=True)), 0.0)
                p = e / jnp.maximum(jnp.sum(e, axis=0, keepdims=True), TINY)
                cmp_probs.append(p.astype(_MXU))
                p_sum = p_sum + p
            o_cmp.append([_dot(vct, cmp_probs[h]) for h in _HEADS])
            importance.append(_dot(cover_ref[...], p_sum, precision=lax.Precision.HIGHEST))

        n_back = WINDOW // t
        jobs = []
        for r, k in enumerate(tiles):
            window = [functools.partial(tile, kw_ref, vwt_ref, k - n_back, None, qry < key)] if k >= n_back else []
            for back in range(min(n_back - 1, k), 0, -1):
                window.append(functools.partial(tile, kw_ref, vwt_ref, k - back, bs_ref if back == 1 else None))
            window.append(functools.partial(tile, kw_ref, vwt_ref, k, bd_ref, causal))
            jobs.append((GROUP_HEADS, qts[r], [window]))
        o_win = [[_softmax_out(st) for st in states] for states in _softmax_jobs(t, jobs)]

        jobs = []
        for r, k in enumerate(tiles):
            own = jnp.right_shift(s_col + k * t, int(math.log2(SLC_LEN)))
            score = jnp.where(s_row == own, FORCE, jnp.where(s_row < own, importance[r], NEG_INF))
            sel = _top_k_rows(score, s_row.astype(jnp.float32), min(SLC_TOPN, n_slc)).astype(_MXU)

            def chosen(sel, j, diagonal=False):
                mask = _dot(_key_rows(e_ref, j, t), sel) > 0.5
                return jnp.logical_and(mask, causal) if diagonal else mask

            chosen = functools.partial(chosen, sel)
            groups = _pairs([functools.partial(tile, ks_ref, vst_ref, j, None, functools.partial(chosen, j))
                             for j in range(k - 1)])
            groups.append(([functools.partial(tile, ks_ref, vst_ref, k - 1, bs_ref, functools.partial(chosen, k - 1))]
                           if k else [])
                          + [functools.partial(tile, ks_ref, vst_ref, k, bd_ref, functools.partial(chosen, k, True))])
            jobs.append((GROUP_HEADS, qts[r], groups))
        o_slc = [[_softmax_out(st) for st in states] for states in _softmax_jobs(t, jobs)]

        gates = _transposed(gq_ref[0, :, (gate_col // LANES) * LANES:(gate_col // LANES + 1) * LANES])
        gates = 1.0 / (1.0 + jnp.exp(-gates[gate_col % LANES:gate_col % LANES + N_GATES + 4]))
        outs = []
        for r in range(Q_TILES):
            heads = []
            for h in _HEADS:
                g = [gates[br * GROUP_HEADS + h:br * GROUP_HEADS + h + 1, r * t:(r + 1) * t] for br in range(3)]
                heads.append(g[0] * o_cmp[r][h] + g[1] * o_slc[r][h] + g[2] * o_win[r][h])
            outs.append(heads)
        _store_heads(o_ref, outs)

    _per_step(i, kva_ref.shape[1] // (Q_TILES * t), query_step)


def _nsa(proj, kc, vc_t, bias_diag, bias_sub, bias_cmp, cover_t, expand, t=TILE):
    bsz, s, _ = proj.shape
    n_cmp = kc.shape[1]
    kv_scratch = [pltpu.VMEM((s, HEAD_DIM), _MXU), pltpu.VMEM((s // t, ACC_ROWS, t), _MXU)]
    return pl.pallas_call(
        _nsa_kernel,
        grid=(bsz, s // (Q_TILES * t)),
        in_specs=_group_specs(t, s, CB_NS_Q, CB_NS_A, CB_NS_B)
        + [pl.BlockSpec((1, Q_TILES * t, GROUP_WIDTH), lambda b, i: (b, i, CB_NS_B)),
           pl.BlockSpec((1, n_cmp, HEAD_DIM), lambda b, i: (b, 0, 0)),
           pl.BlockSpec((1, HEAD_DIM, n_cmp), lambda b, i: (b, 0, 0))]
        + _bias_specs(t, 1)
        + [pl.BlockSpec((GROUP_HEADS, n_cmp, Q_TILES * t), lambda b, i: (0, 0, i)),
           pl.BlockSpec(cover_t.shape, lambda b, i: (0, 0)),
           pl.BlockSpec(expand.shape, lambda b, i: (0, 0))],
        scratch_shapes=kv_scratch + kv_scratch,
        compiler_params=_params("parallel", "arbitrary"),
        name="nsa",
        **_group_out(bsz, s, t),
    )(proj, proj, proj, proj, kc, vc_t, bias_diag, bias_sub, bias_cmp, cover_t, expand)


def _nsa_constants(s):
    n_cmp = (s - CMP_LEN) // CMP_STRIDE + 1
    n_slc = s // SLC_LEN
    assert n_cmp + 1 == s // CMP_STRIDE and n_slc % SUBLANES == 0
    c_start = np.arange(n_cmp) * CMP_STRIDE
    s_start = np.arange(n_slc) * SLC_LEN
    cover = np.clip(np.minimum((c_start + CMP_LEN - 1)[:, None], (s_start + SLC_LEN - 1)[None, :])
                    - np.maximum(c_start[:, None], s_start[None, :]) + 1, 0, None) / CMP_LEN
    cover_t = np.zeros((n_slc, n_cmp + 1), np.float32)
    cover_t[:, :n_cmp] = cover.T
    expand = (np.arange(s)[:, None] // SLC_LEN == np.arange(n_slc)[None, :]).astype(np.float32)
    return jnp.asarray(cover_t), jnp.asarray(expand, _MXU)


def kernel(x, w_in, w_out, w_up, w_down, norm_attn, norm_mlp, cmp_pos_k, cmp_pos_v, cmp_k_w1, cmp_k_w2,
           cmp_v_w1, cmp_v_w2, diff_lambda, diff_subln, rel_bias, final_norm):
    bsz, s, d = x.shape
    depth = w_in.shape[0]
    t = TILE
    n_chunk = s // CMP_STRIDE
    assert s % MOBA_BLOCK == 0 and MOBA_BLOCK % t == 0 and WINDOW % t == 0 and t >= MAX_DISTANCE

    w_in_p = jnp.concatenate([w_in[:, :, :COLS_BEFORE_PAD],
                              jnp.zeros((depth, d, PAD_COLS), w_in.dtype),
                              w_in[:, :, COLS_BEFORE_PAD:]], axis=2).astype(_MXU)
    assert w_in_p.shape[2] == D_IN_PAD
    w_out_c, w_up_c, w_down_c = w_out.astype(_MXU), w_up.astype(_MXU), w_down.astype(_MXU)
    wk1, wk2 = cmp_k_w1.astype(_MXU), cmp_k_w2.astype(_MXU)
    wv1, wv2t = cmp_v_w1.astype(_MXU), jnp.swapaxes(cmp_v_w2, 1, 2).astype(_MXU)

    tiles = dict(n_heads=rel_bias.shape[1], head0=0, rows=t, cols=t, col_tile=t, row_stride=-1, col_stride=1)
    bias_diag = _bias_table(rel_bias, offset=0, **tiles)
    bias_sub = _bias_table(rel_bias, offset=t, **tiles)
    bias_cmp = _bias_table(rel_bias, n_heads=GROUP_HEADS, head0=GROUP_HEADS, rows=n_chunk, cols=s,
                           col_tile=t, row_stride=-CMP_STRIDE, col_stride=1, offset=-(CMP_LEN - 1))
    cover_t, expand = _nsa_constants(s)
    col_scale = np.ones((1, D_IN_PAD), np.float32)
    for cb, width in ((CB_SB_Q, HEAD_DIM), (CB_MB_Q, HEAD_DIM), (CB_NS_Q, HEAD_DIM), (CB_DF_Q, DIFF_HALF)):
        col_scale[:, cb * GROUP_WIDTH:(cb + 1) * GROUP_WIDTH] = width ** -0.5 * LOG2E
    col_scale = jnp.asarray(col_scale)

    x2 = x.reshape(bsz * s, d)
    for layer in range(depth):
        proj, kcv = _norm_matmul(x2, norm_attn[layer][None], w_in_p, layer, col_scale,
                                 side_col=CB_NS_A * GROUP_WIDTH, side_width=2 * HEAD_DIM)
        proj = proj.reshape(bsz, s, D_IN_PAD)
        o_sb = _stick_breaking(proj)
        o_mb = _moba(proj, bias_diag, bias_sub)
        kc, vc_t = _compress(kcv.reshape(bsz, s, 2 * HEAD_DIM), cmp_pos_k, cmp_pos_v, wk1, wk2, wv1, wv2t, layer)
        o_ns = _nsa(proj, kc, vc_t, bias_diag, bias_sub, bias_cmp, cover_t, expand)
        lambda_init = 0.8 - 0.6 * math.exp(-0.3 * layer)
        o_df = _diff(proj, diff_lambda[layer], diff_subln[layer][:, None], bias_diag, bias_sub, lambda_init)
        groups = [o.reshape(bsz * s, GROUP_WIDTH) for o in (o_sb, o_mb, o_ns, o_df)]
        x2 = _out_mlp(x2, groups, w_out_c, norm_mlp[layer][None], w_up_c, w_down_c, layer,
                      final_norm[None], final_norm=(layer == depth - 1))
    return x2.reshape(bsz, s, d)
```

```python
import functools
import math
from typing import Any, NamedTuple, Optional, Sequence

import numpy as np
import jax
import jax.numpy as jnp
from jax import lax
from jax.experimental import pallas as pl
from jax.experimental.pallas import tpu as pltpu

HEAD_DIM = 64
GROUP_HEADS = 4
GROUP_WIDTH = GROUP_HEADS * HEAD_DIM
NORM_EPS = 1e-6
NEG_INF = -1e30
BIG = 1e30
FORCE = 1e30
TINY = 1e-30
PICKED = -3e38
LOG2E = math.log2(math.e)
N_BUCKETS = 32
MAX_DISTANCE = 128
MOBA_BLOCK = 256
MOBA_TOPK = 3
CMP_LEN = 32
CMP_STRIDE = 16
SLC_LEN = 64
SLC_TOPN = 4
WINDOW = 512
DIFF_HALF = HEAD_DIM // 2
LANES = 128
SUBLANES = 8
TILE = 256
BF16_ROWS = 16
ACC_ROWS = HEAD_DIM + BF16_ROWS
N_GATES = 3 * GROUP_HEADS
COLS_BEFORE_PAD = 9 * GROUP_WIDTH - 2 * HEAD_DIM + N_GATES
PAD_COLS = 2 * HEAD_DIM - N_GATES
CB_SB_Q, CB_SB_K, CB_SB_V, CB_MB_Q, CB_MB_K, CB_MB_V, CB_NS_Q, CB_NS_A, CB_NS_B, CB_DF_Q, CB_DF_K, CB_DF_V = range(12)
D_IN_PAD = 12 * GROUP_WIDTH

_MXU = jnp.bfloat16
_VMEM_LIMIT = 56 * 1024 * 1024
_HEADS = range(GROUP_HEADS)
Q_TILES = 4
LOOKAHEAD = 2


def _dot(a, b, precision=None):
    return jnp.dot(a, b, precision=precision, preferred_element_type=jnp.float32)


def _dot_nt(a, b, precision=None):
    return lax.dot_general(a, b, (((1,), (1,)), ((), ())), precision=precision,
                           preferred_element_type=jnp.float32)


def _rms(x, g):
    return x * lax.rsqrt(jnp.mean(x * x, axis=-1, keepdims=True) + NORM_EPS) * g


def _params(*sem):
    return pltpu.CompilerParams(dimension_semantics=sem, vmem_limit_bytes=_VMEM_LIMIT)


def _norm_matmul_kernel(x_ref, g_ref, w_ref, scale_ref, o_ref, side_ref, wp_ref, *, tn, side_col, gap):
    @pl.when(pl.program_id(0) == 0)
    def _():
        lo, hi = gap
        wp_ref[:, :lo] = w_ref[:, :lo]
        wp_ref[:, lo:hi] = jnp.zeros((wp_ref.shape[0], hi - lo), wp_ref.dtype)
        wp_ref[:, hi:] = w_ref[:, lo:]

    h = _rms(x_ref[...], g_ref[...]).astype(_MXU)
    for j in range(wp_ref.shape[1] // tn):
        cols = slice(j * tn, (j + 1) * tn)
        acc = _dot(h, wp_ref[:, cols])
        o_ref[:, cols] = (acc * scale_ref[:, cols]).astype(o_ref.dtype)
        if j * tn <= side_col < (j + 1) * tn:
            side_ref[...] = acc[:, side_col - j * tn:side_col - j * tn + side_ref.shape[1]]


def _norm_matmul(x, g, w, layer, col_scale, *, gap, side_col, side_width, tm=512, tn=1024):
    m, d = x.shape
    n = w.shape[2] + gap[1] - gap[0]
    assert side_col % LANES == 0 and side_col // tn == (side_col + side_width - 1) // tn
    return pl.pallas_call(
        functools.partial(_norm_matmul_kernel, tn=tn, side_col=side_col, gap=gap),
        grid=(m // tm,),
        in_specs=[pl.BlockSpec((tm, d), lambda i: (i, 0)),
                  pl.BlockSpec((1, d), lambda i: (0, 0)),
                  pl.BlockSpec((None, d, w.shape[2]), lambda i: (layer, 0, 0), pipeline_mode=pl.Buffered(1)),
                  pl.BlockSpec((1, n), lambda i: (0, 0))],
        out_specs=[pl.BlockSpec((tm, n), lambda i: (i, 0)),
                   pl.BlockSpec((tm, side_width), lambda i: (i, 0))],
        out_shape=[jax.ShapeDtypeStruct((m, n), _MXU),
                   jax.ShapeDtypeStruct((m, side_width), jnp.float32)],
        scratch_shapes=[pltpu.VMEM((d, n), _MXU)],
        compiler_params=_params("arbitrary"),
        name="norm_in_proj",
    )(x, g, w, col_scale)


def _out_mlp_kernel(x_ref, a_ref, b_ref, c_ref, d_ref, wo_ref, g_ref, wu_ref, wd_ref, gf_ref, o_ref,
                    *, final_norm, tf):
    mixed = jnp.concatenate([a_ref[...], b_ref[...], c_ref[...], d_ref[...]], axis=1)
    y = x_ref[...] + _dot(mixed, wo_ref[...])
    h = _rms(y, g_ref[...]).astype(_MXU)
    for c in range(wu_ref.shape[1] // tf):
        u = jnp.square(jnp.maximum(_dot(h, wu_ref[:, c * tf:(c + 1) * tf]), 0.0))
        y = y + _dot(u.astype(_MXU), wd_ref[c * tf:(c + 1) * tf, :])
    if final_norm:
        y = _rms(y, gf_ref[...])
    o_ref[...] = y


def _out_mlp(x, groups, w_out, g, w_up, w_down, layer, g_final, *, final_norm, tm=512, tf=2048):
    m, d = x.shape
    f = w_up.shape[2]
    gspec = pl.BlockSpec((tm, GROUP_WIDTH), lambda i: (i, 0))
    row = pl.BlockSpec((1, d), lambda i: (0, 0))

    def resident(rows, cols):
        return pl.BlockSpec((None, rows, cols), lambda i: (layer, 0, 0), pipeline_mode=pl.Buffered(1))

    return pl.pallas_call(
        functools.partial(_out_mlp_kernel, final_norm=final_norm, tf=tf),
        grid=(m // tm,),
        in_specs=[pl.BlockSpec((tm, d), lambda i: (i, 0)), gspec, gspec, gspec, gspec,
                  resident(d, d), row, resident(d, f), resident(f, d), row],
        out_specs=pl.BlockSpec((tm, d), lambda i: (i, 0)),
        out_shape=jax.ShapeDtypeStruct((m, d), jnp.float32),
        compiler_params=_params("parallel"),
        name="out_proj_mlp_residual",
    )(x, *groups, w_out, g, w_up, w_down, g_final)


def _t5_bucket(dist):
    n = jnp.maximum(dist, 0)
    max_exact = N_BUCKETS // 2
    nf = jnp.maximum(n, 1).astype(jnp.float32)
    large = max_exact + (jnp.log(nf / max_exact) / math.log(MAX_DISTANCE / max_exact)
                         * (N_BUCKETS - max_exact)).astype(jnp.int32)
    large = jnp.minimum(large, N_BUCKETS - 1)
    return jnp.where(n < max_exact, n, large)


def _bias_kernel(tab_ref, o_ref, *, row_stride, col_stride, offset, head0):
    nh, tr, tc = o_ref.shape
    for blk in range(tc // LANES):
        rows = lax.broadcasted_iota(jnp.int32, (tr, LANES), 0)
        cols = lax.broadcasted_iota(jnp.int32, (tr, LANES), 1) + (pl.program_id(0) * tc + blk * LANES)
        bucket = _t5_bucket(rows * row_stride + cols * col_stride + offset)
        for h in range(nh):
            row = tab_ref[head0 + h:head0 + h + 1, :]
            row = (row - row[:, N_BUCKETS - 1:N_BUCKETS]) * LOG2E
            o_ref[h, :, blk * LANES:(blk + 1) * LANES] = jnp.take_along_axis(
                jnp.broadcast_to(row, (tr, LANES)), bucket, axis=1, mode="promise_in_bounds")


def _bias_table(table_t, *, n_heads, head0, rows, cols, col_tile, row_stride, col_stride, offset):
    return pl.pallas_call(
        functools.partial(_bias_kernel, row_stride=row_stride, col_stride=col_stride,
                          offset=offset, head0=head0),
        grid=(cols // col_tile,),
        in_specs=[pl.BlockSpec(table_t.shape, lambda i: (0, 0))],
        out_specs=pl.BlockSpec((n_heads, rows, col_tile), lambda i: (0, 0, i)),
        out_shape=jax.ShapeDtypeStruct((n_heads, rows, cols), jnp.float32),
        compiler_params=_params("parallel"),
        name="t5_bias_tiles",
    )(table_t)


def _softmax_init(t):
    return (jnp.full((1, t), NEG_INF, jnp.float32), jnp.zeros((ACC_ROWS, t), jnp.float32))


class _KeyTile(NamedTuple):
    kts: Sequence[Any]
    vts: Sequence[Any]
    biases: Optional[Sequence[Any]] = None
    emasks: Optional[Sequence[Any]] = None
    qmasks: Optional[Sequence[Any]] = None


def _round_robin(lists):
    out = []
    for rank in range(max(map(len, lists), default=0)):
        out.extend(items[rank] for items in lists if rank < len(items))
    return out


def _softmax_jobs(t, jobs):
    built = {}

    def tiles_of(job, g):
        if (job, g) not in built:
            built[job, g] = [make() for make in jobs[job][2][g]]
        return built[job, g]

    def scores_of(job, g, c):
        qts = jobs[job][1]
        row = []
        for tile in tiles_of(job, g):
            s = _dot(tile.kts[c], qts[c])
            if tile.biases is not None:
                s = s + tile.biases[c]
            if tile.emasks is not None:
                s = jnp.where(tile.emasks[c], s, NEG_INF)
            row.append(s.astype(_MXU))
        return row

    def update(state, job, g, c, scores):
        m, acc = state
        m_new = m
        for tile, s in zip(tiles_of(job, g), scores):
            tile_max = jnp.max(s, axis=0, keepdims=True).astype(jnp.float32)
            if tile.qmasks is not None:
                tile_max = jnp.where(tile.qmasks[c], tile_max, NEG_INF)
            m_new = jnp.maximum(m_new, tile_max)
        seen = m_new > 0.5 * NEG_INF
        acc = jnp.exp2(m - m_new) * acc
        for tile, s in zip(tiles_of(job, g), scores):
            ok = seen if tile.qmasks is None else jnp.logical_and(seen, tile.qmasks[c])
            acc = acc + _dot(tile.vts[c], jnp.exp2(s - jnp.where(ok, m_new, BIG).astype(_MXU)))
        return m_new, acc

    units = _round_robin([[(job, g, c) for g in range(len(groups)) for c in range(n)]
                          for job, (n, _, groups) in enumerate(jobs)])
    lookahead = LOOKAHEAD * len(jobs)
    states = [[_softmax_init(t) for _ in range(n)] for n, _, _ in jobs]
    pending = {k: scores_of(*units[k]) for k in range(min(lookahead, len(units)))}
    for k, (job, g, c) in enumerate(units):
        if k + lookahead < len(units):
            pending[k + lookahead] = scores_of(*units[k + lookahead])
        states[job][c] = update(states[job][c], job, g, c, pending.pop(k))
    return states


def _pairs(items):
    return [items[p:p + 2] for p in range(0, len(items), 2)]


def _per_step(i, n_steps, body):
    for step in range(n_steps):
        pl.when(i == step)(functools.partial(body, step))


def _softmax_out(state):
    acc = state[1]
    return acc[:HEAD_DIM] / jnp.maximum(acc[HEAD_DIM:HEAD_DIM + 1], TINY)


def _top_k_rows(score, row_f, k):
    sel = jnp.zeros(score.shape, jnp.float32)
    for _ in range(k):
        mx = jnp.max(score, axis=0, keepdims=True)
        idx = jnp.min(jnp.where(score == mx, row_f, float(score.shape[0])), axis=0, keepdims=True)
        pick = row_f == idx
        sel = jnp.where(pick, 1.0, sel)
        score = jnp.where(pick, PICKED, score)
    return sel


def _tile_iotas(t):
    return (lax.broadcasted_iota(jnp.int32, (t, t), 0), lax.broadcasted_iota(jnp.int32, (t, t), 1))


def _key_rows(ref, j, t):
    if isinstance(j, int):
        return ref[j * t:(j + 1) * t, :]
    return ref[pl.ds(pl.multiple_of(j * t, t), t), :]


def _transposed(ref_block):
    return ref_block.astype(jnp.float32).T


def _queries_t(q_ref, width, t):
    qt = _transposed(q_ref[0]).astype(_MXU)
    return [[qt[c * width:(c + 1) * width, r * t:(r + 1) * t] for c in range(GROUP_WIDTH // width)]
            for r in range(Q_TILES)]


def _fill_values_t(vt_ref, v_ref, col0, t):
    n_tiles, rows, _ = vt_ref.shape
    lane_block = (col0 // LANES) * LANES
    for c in range(n_tiles):
        blk = _transposed(v_ref[0, c * t:(c + 1) * t, lane_block:lane_block + LANES])
        vt_ref[c, 0:HEAD_DIM, :] = blk[col0 - lane_block:col0 - lane_block + HEAD_DIM].astype(vt_ref.dtype)
        if rows == ACC_ROWS:
            first = lax.broadcasted_iota(jnp.int32, (rows - HEAD_DIM, t), 0) == 0
            vt_ref[c, HEAD_DIM:rows, :] = jnp.where(first, 1.0, 0.0).astype(vt_ref.dtype)


def _group_specs(t, s, cb_q, cb_k, cb_v):
    return [pl.BlockSpec((1, Q_TILES * t, GROUP_WIDTH), lambda b, i: (b, i, cb_q)),
            pl.BlockSpec((1, s, GROUP_WIDTH), lambda b, i: (b, 0, cb_k)),
            pl.BlockSpec((1, s, GROUP_WIDTH), lambda b, i: (b, 0, cb_v))]


def _group_out(bsz, s, t):
    return dict(out_specs=pl.BlockSpec((1, Q_TILES * t, GROUP_WIDTH), lambda b, i: (b, i, 0)),
                out_shape=jax.ShapeDtypeStruct((bsz, s, GROUP_WIDTH), _MXU))


def _bias_specs(t, head_group):
    spec = pl.BlockSpec((GROUP_HEADS, t, t), lambda b, i: (head_group, 0, 0))
    return [spec, spec]


def _store_heads(o_ref, outs_t):
    tiles = [jnp.concatenate(heads, axis=0) for heads in outs_t]
    o_ref[0] = jnp.concatenate(tiles, axis=1).T.astype(o_ref.dtype)


def _sb_kernel(q_ref, k_ref, v_ref, o_ref, kb_ref, vt_ref):
    t = q_ref.shape[1] // Q_TILES
    i = pl.program_id(1)
    key, qry = _tile_iotas(t)
    strict = key < qry
    later = jnp.where(qry > key, 1.0, 0.0).astype(_MXU)

    @pl.when(i == 0)
    def _():
        for h in _HEADS:
            kb_ref[h] = k_ref[0, :, h * HEAD_DIM:(h + 1) * HEAD_DIM].astype(kb_ref.dtype)
            _fill_values_t(vt_ref.at[h], v_ref, h * HEAD_DIM, t)

    def query_step(step):
        qts = _queries_t(q_ref, HEAD_DIM, t)
        units = _round_robin([[(r, step * Q_TILES + r, pair, h)
                               for pair in _pairs(list(range(step * Q_TILES + r, -1, -1))) for h in _HEADS]
                              for r in range(Q_TILES)])
        zero = (jnp.zeros((HEAD_DIM, t), jnp.float32), jnp.zeros((1, t), jnp.float32))
        carry = [[zero] * GROUP_HEADS for _ in range(Q_TILES)]
        zs, log_keeps, suffixes = {}, {}, {}

        def scores(u):
            r, _, pair, h = units[u]
            zs[u] = [_dot(_key_rows(kb_ref.at[h], j, t), qts[r][h]) for j in pair]

        def keeps(u):
            _, k, pair, _ = units[u]
            log_keeps[u], suffixes[u] = [], []
            for j, z in zip(pair, zs[u]):
                neg = -z
                log_keep = jnp.minimum(neg, 0.0) - jnp.log2(1.0 + jnp.exp2(jnp.minimum(z, neg)))
                if j == k:
                    log_keep = jnp.where(strict, log_keep, 0.0)
                log_keeps[u].append(log_keep)
                suffixes[u].append(_dot(later, log_keep.astype(_MXU)))

        def values(u):
            r, k, pair, h = units[u]
            acc, run = carry[r][h]
            weights = []
            for j, z, log_keep, suffix in zip(pair, zs.pop(u), log_keeps.pop(u), suffixes.pop(u)):
                a = jnp.exp2(z + log_keep + suffix + run)
                if j == k:
                    a = jnp.where(strict, a, 0.0)
                weights.append(a.astype(_MXU))
                run = run + (suffix[0:1] + log_keep[0:1])
            for j, w in zip(pair, weights):
                acc = acc + _dot(vt_ref[h, j], w)
            carry[r][h] = (acc, run)

        stages = (scores, keeps, values)
        for tick in range(len(units) + len(stages) - 1):
            for lag, stage in enumerate(stages):
                if 0 <= tick - lag < len(units):
                    stage(tick - lag)
        _store_heads(o_ref, [[c[0] for c in tile_carry] for tile_carry in carry])

    _per_step(i, k_ref.shape[1] // (Q_TILES * t), query_step)


def _stick_breaking(proj, t=TILE):
    bsz, s, _ = proj.shape
    return pl.pallas_call(
        _sb_kernel,
        grid=(bsz, s // (Q_TILES * t)),
        in_specs=_group_specs(t, s, CB_SB_Q, CB_SB_K, CB_SB_V),
        scratch_shapes=[pltpu.VMEM((GROUP_HEADS, s, HEAD_DIM), _MXU),
                        pltpu.VMEM((GROUP_HEADS, s // t, HEAD_DIM, t), _MXU)],
        compiler_params=_params("parallel", "arbitrary"),
        name="stick_breaking",
        **_group_out(bsz, s, t),
    )(proj, proj, proj)


def _moba_kernel(q_ref, k_ref, v_ref, bd_ref, bs_ref, o_ref, kb_ref, vt_ref, km_ref):
    t = q_ref.shape[1] // Q_TILES
    n_blk = k_ref.shape[1] // MOBA_BLOCK
    tiles_per_blk = MOBA_BLOCK // t
    blk_shift = int(math.log2(tiles_per_blk))
    i = pl.program_id(1)

    @pl.when(i == 0)
    def _():
        km_ref[...] = jnp.zeros_like(km_ref)
        for h in _HEADS:
            lo, hi = h * HEAD_DIM, (h + 1) * HEAD_DIM
            kb_ref[h] = k_ref[0, :, lo:hi].astype(kb_ref.dtype)
            _fill_values_t(vt_ref.at[h], v_ref, lo, t)
            for n in range(n_blk):
                blk = k_ref[0, n * MOBA_BLOCK:(n + 1) * MOBA_BLOCK, lo:hi]
                km_ref[h, n:n + 1, :] = jnp.mean(blk.astype(jnp.float32), axis=0, keepdims=True)

    def query_step(step):
        key, qry = _tile_iotas(t)
        causal = key <= qry
        blk_row = lax.broadcasted_iota(jnp.int32, (km_ref.shape[1], t), 0)
        qt = _transposed(q_ref[0])
        jobs = []
        for r in range(Q_TILES):
            k = step * Q_TILES + r
            own = k >> blk_shift
            qts, sels = [], []
            for h in _HEADS:
                qf = qt[h * HEAD_DIM:(h + 1) * HEAD_DIM, r * t:(r + 1) * t]
                qts.append(qf.astype(_MXU))
                gate = _dot(km_ref[h], qf, precision=lax.Precision.HIGHEST)
                gate = jnp.where(blk_row < own, gate, NEG_INF)
                sel = _top_k_rows(gate, blk_row.astype(jnp.float32), min(MOBA_TOPK, n_blk - 1))
                sels.append(jnp.where(blk_row < own, sel, 0.0))

            def tile(own, sels, j, bias_ref=None, emask=None):
                n = j >> blk_shift
                return _KeyTile([_key_rows(kb_ref.at[h], j, t) for h in _HEADS], [vt_ref[h, j] for h in _HEADS],
                                None if bias_ref is None else [bias_ref[h] for h in _HEADS],
                                None if emask is None else [emask] * GROUP_HEADS,
                                None if n == own else [sels[h][n:n + 1] > 0.5 for h in _HEADS])

            tile = functools.partial(tile, own, sels)
            groups = _pairs([functools.partial(tile, j) for j in range(k - 1)])
            groups.append(([functools.partial(tile, k - 1, bs_ref)] if k else [])
                          + [functools.partial(tile, k, bd_ref, causal)])
            jobs.append((GROUP_HEADS, qts, groups))
        _store_heads(o_ref, [[_softmax_out(st) for st in states] for states in _softmax_jobs(t, jobs)])

    _per_step(i, k_ref.shape[1] // (Q_TILES * t), query_step)


def _kv_scratch(s, t, key_dim=HEAD_DIM, n_keys=GROUP_HEADS):
    return [pltpu.VMEM((n_keys, s, key_dim), _MXU), pltpu.VMEM((GROUP_HEADS, s // t, ACC_ROWS, t), _MXU)]


def _moba(proj, bias_diag, bias_sub, t=TILE):
    bsz, s, _ = proj.shape
    n_blk_pad = -(-(s // MOBA_BLOCK) // SUBLANES) * SUBLANES
    return pl.pallas_call(
        _moba_kernel,
        grid=(bsz, s // (Q_TILES * t)),
        in_specs=_group_specs(t, s, CB_MB_Q, CB_MB_K, CB_MB_V) + _bias_specs(t, 0),
        scratch_shapes=_kv_scratch(s, t) + [pltpu.VMEM((GROUP_HEADS, n_blk_pad, HEAD_DIM), jnp.float32)],
        compiler_params=_params("parallel", "arbitrary"),
        name="moba",
        **_group_out(bsz, s, t),
    )(proj, proj, proj, bias_diag, bias_sub)


def _diff_kernel(lam_ref, g_ref, q_ref, k_ref, v_ref, bd_ref, bs_ref, o_ref, kb_ref, vt_ref, *, lambda_init):
    t = q_ref.shape[1] // Q_TILES
    i = pl.program_id(1)
    key, qry = _tile_iotas(t)
    causal = key <= qry
    lv = lam_ref[...]
    lam = (jnp.exp(jnp.sum(lv[0:1] * lv[1:2], keepdims=True))
           - jnp.exp(jnp.sum(lv[2:3] * lv[3:4], keepdims=True)) + lambda_init)
    halves = range(2 * GROUP_HEADS)

    @pl.when(i == 0)
    def _():
        for c in halves:
            kb_ref[c] = k_ref[0, :, c * DIFF_HALF:(c + 1) * DIFF_HALF].astype(kb_ref.dtype)
        for h in _HEADS:
            _fill_values_t(vt_ref.at[h], v_ref, h * HEAD_DIM, t)

    def tile(j, bias_ref=None, emask=None):
        n = len(halves)
        return _KeyTile([_key_rows(kb_ref.at[c], j, t) for c in halves], [vt_ref[c // 2, j] for c in halves],
                        None if bias_ref is None else [bias_ref[c // 2] for c in halves],
                        None if emask is None else [emask] * n)

    def query_step(step):
        qts = _queries_t(q_ref, DIFF_HALF, t)
        jobs = []
        for r in range(Q_TILES):
            k = step * Q_TILES + r
            groups = _pairs([functools.partial(tile, j) for j in range(k - 1)])
            groups.append(([functools.partial(tile, k - 1, bs_ref)] if k else [])
                          + [functools.partial(tile, k, bd_ref, causal)])
            jobs.append((len(halves), qts[r], groups))
        outs = []
        for states in _softmax_jobs(t, jobs):
            heads = []
            for h in _HEADS:
                o = _softmax_out(states[2 * h]) - lam * _softmax_out(states[2 * h + 1])
                o = o * lax.rsqrt(jnp.mean(o * o, axis=0, keepdims=True) + NORM_EPS) * g_ref[...]
                heads.append(o * (1.0 - lambda_init))
            outs.append(heads)
        _store_heads(o_ref, outs)

    _per_step(i, k_ref.shape[1] // (Q_TILES * t), query_step)


def _diff(proj, lam_params, subln, bias_diag, bias_sub, lambda_init, t=TILE):
    bsz, s, _ = proj.shape
    return pl.pallas_call(
        functools.partial(_diff_kernel, lambda_init=lambda_init),
        grid=(bsz, s // (Q_TILES * t)),
        in_specs=[pl.BlockSpec(lam_params.shape, lambda b, i: (0, 0)),
                  pl.BlockSpec(subln.shape, lambda b, i: (0, 0))]
        + _group_specs(t, s, CB_DF_Q, CB_DF_K, CB_DF_V) + _bias_specs(t, 2),
        scratch_shapes=_kv_scratch(s, t, key_dim=DIFF_HALF, n_keys=2 * GROUP_HEADS),
        compiler_params=_params("parallel", "arbitrary"),
        name="diff_attention",
        **_group_out(bsz, s, t),
    )(lam_params, subln, proj, proj, proj, bias_diag, bias_sub)


def _compress_kernel(kcv_ref, pk_ref, pv_ref, wk1_ref, wk2_ref, wv1_ref, wv2t_ref, kc_ref, vct_ref):
    n_chunk = kc_ref.shape[1]

    branches = ((0, pk_ref, wk1_ref), (HEAD_DIM, pv_ref, wv1_ref))
    tops = [jnp.zeros((n_chunk, w1_ref.shape[1]), jnp.float32) for _, _, w1_ref in branches]
    bots = list(tops)
    for l in range(CMP_STRIDE):
        tokens = kcv_ref[0, pl.ds(l, n_chunk, stride=CMP_STRIDE), :]
        l2 = CMP_STRIDE + l
        for n, (col0, p_ref, w1_ref) in enumerate(branches):
            x = tokens[:, col0:col0 + HEAD_DIM]
            tops[n] = tops[n] + _dot((x + p_ref[l:l + 1, :]).astype(_MXU),
                                     w1_ref[l * HEAD_DIM:(l + 1) * HEAD_DIM, :])
            bots[n] = bots[n] + _dot((x + p_ref[l2:l2 + 1, :]).astype(_MXU),
                                     w1_ref[l2 * HEAD_DIM:(l2 + 1) * HEAD_DIM, :])
    hidden = [jax.nn.gelu(top + pltpu.roll(bot, n_chunk - 1, axis=0)).astype(_MXU) for top, bot in zip(tops, bots)]
    kc_ref[0] = _dot(hidden[0], wk2_ref[...])
    vct_ref[0] = _dot_nt(wv2t_ref[...], hidden[1])


def _compress(kcv, pos_k, pos_v, wk1, wk2, wv1, wv2t, layer):
    bsz, s, width = kcv.shape
    n_chunk = s // CMP_STRIDE

    def full(a):
        return pl.BlockSpec((None,) + a.shape[1:], lambda b: (layer, 0, 0))

    return pl.pallas_call(
        _compress_kernel,
        grid=(bsz,),
        in_specs=[pl.BlockSpec((1, s, width), lambda b: (b, 0, 0)),
                  full(pos_k), full(pos_v), full(wk1), full(wk2), full(wv1), full(wv2t)],
        out_specs=[pl.BlockSpec((1, n_chunk, HEAD_DIM), lambda b: (b, 0, 0)),
                   pl.BlockSpec((1, HEAD_DIM, n_chunk), lambda b: (b, 0, 0))],
        out_shape=[jax.ShapeDtypeStruct((bsz, n_chunk, HEAD_DIM), jnp.float32),
                   jax.ShapeDtypeStruct((bsz, HEAD_DIM, n_chunk), jnp.float32)],
        compiler_params=_params("parallel"),
        name="nsa_compress",
    )(kcv, pos_k, pos_v, wk1, wk2, wv1, wv2t)


def _nsa_kernel(q_ref, kva_ref, kvb_ref, gq_ref, kc_ref, vct_ref, bd_ref, bs_ref, bc_ref, cover_ref, e_ref,
                o_ref, ks_ref, vst_ref, kw_ref, vwt_ref):
    t = q_ref.shape[1] // Q_TILES
    i = pl.program_id(1)
    key, qry = _tile_iotas(t)
    causal = key <= qry
    ks_col, vs_col, kw_col, vw_col, gate_col = 2 * HEAD_DIM, 3 * HEAD_DIM, 0, HEAD_DIM, 2 * HEAD_DIM

    @pl.when(i == 0)
    def _():
        ks_ref[...] = kva_ref[0, :, ks_col:ks_col + HEAD_DIM].astype(ks_ref.dtype)
        kw_ref[...] = kvb_ref[0, :, kw_col:kw_col + HEAD_DIM].astype(kw_ref.dtype)
        _fill_values_t(vst_ref, kva_ref, vs_col, t)
        _fill_values_t(vwt_ref, kvb_ref, vw_col, t)

    def tile(k_ref, vt_ref, j, bias_ref=None, emask=None):
        n = GROUP_HEADS
        emask = emask() if callable(emask) else emask
        return _KeyTile([_key_rows(k_ref, j, t)] * n, [vt_ref[j]] * n,
                        None if bias_ref is None else [bias_ref[h] for h in _HEADS],
                        None if emask is None else [emask] * n)

    def query_step(step):
        qts = _queries_t(q_ref, HEAD_DIM, t)
        n_cmp = kc_ref.shape[1]
        n_slc = cover_ref.shape[0]
        kc = kc_ref[0].astype(_MXU)
        vct = vct_ref[0].astype(_MXU)
        c_row = lax.broadcasted_iota(jnp.int32, (n_cmp, t), 0)
        c_col = lax.broadcasted_iota(jnp.int32, (n_cmp, t), 1)
        s_row = lax.broadcasted_iota(jnp.int32, (n_slc, t), 0)
        s_col = lax.broadcasted_iota(jnp.int32, (n_slc, t), 1)
        tiles = [step * Q_TILES + r for r in range(Q_TILES)]

        o_cmp, importance = [], []
        for r, k in enumerate(tiles):
            visible = c_col + k * t >= c_row * CMP_STRIDE + (CMP_LEN - 1)
            cmp_scores = [_dot(kc, qts[r][h]) for h in _HEADS]
            cmp_probs = []
            p_sum = jnp.zeros((n_cmp, t), jnp.float32)
            for h in _HEADS:
                sc = jnp.where(visible, cmp_scores[h] + bc_ref[h, :, r * t:(r + 1) * t], NEG_INF)
                e = jnp.where(visible, jnp.exp2(sc - jnp.max(sc, axis=0, keepdims=True)), 0.0)
                p = e / jnp.maximum(jnp.sum(e, axis=0, keepdims=True), TINY)
                cmp_probs.append(p.astype(_MXU))
                p_sum = p_sum + p
            o_cmp.append([_dot(vct, cmp_probs[h]) for h in _HEADS])
            importance.append(_dot(cover_ref[...], p_sum, precision=lax.Precision.HIGHEST))

        n_back = WINDOW // t
        jobs = []
        for r, k in enumerate(tiles):
            window = [functools.partial(tile, kw_ref, vwt_ref, k - n_back, None, qry < key)] if k >= n_back else []
            for back in range(min(n_back - 1, k), 0, -1):
                window.append(functools.partial(tile, kw_ref, vwt_ref, k - back, bs_ref if back == 1 else None))
            window.append(functools.partial(tile, kw_ref, vwt_ref, k, bd_ref, causal))
            jobs.append((GROUP_HEADS, qts[r], [window]))
        o_win = [[_softmax_out(st) for st in states] for states in _softmax_jobs(t, jobs)]

        jobs = []
        for r, k in enumerate(tiles):
            own = jnp.right_shift(s_col + k * t, int(math.log2(SLC_LEN)))
            score = jnp.where(s_row == own, FORCE, jnp.where(s_row < own, importance[r], NEG_INF))
            sel = _top_k_rows(score, s_row.astype(jnp.float32), min(SLC_TOPN, n_slc)).astype(_MXU)

            def chosen(sel, j, diagonal=False):
                mask = _dot(_key_rows(e_ref, j, t), sel) > 0.5
                return jnp.logical_and(mask, causal) if diagonal else mask

            chosen = functools.partial(chosen, sel)
            groups = _pairs([functools.partial(tile, ks_ref, vst_ref, j, None, functools.partial(chosen, j))
                             for j in range(k - 1)])
            groups.append(([functools.partial(tile, ks_ref, vst_ref, k - 1, bs_ref, functools.partial(chosen, k - 1))]
                           if k else [])
                          + [functools.partial(tile, ks_ref, vst_ref, k, bd_ref, functools.partial(chosen, k, True))])
            jobs.append((GROUP_HEADS, qts[r], groups))
        o_slc = [[_softmax_out(st) for st in states] for states in _softmax_jobs(t, jobs)]

        gates = _transposed(gq_ref[0, :, (gate_col // LANES) * LANES:(gate_col // LANES + 1) * LANES])
        gates = 1.0 / (1.0 + jnp.exp(-gates[gate_col % LANES:gate_col % LANES + N_GATES + 4]))
        outs = []
        for r in range(Q_TILES):
            heads = []
            for h in _HEADS:
                g = [gates[br * GROUP_HEADS + h:br * GROUP_HEADS + h + 1, r * t:(r + 1) * t] for br in range(3)]
                heads.append(g[0] * o_cmp[r][h] + g[1] * o_slc[r][h] + g[2] * o_win[r][h])
            outs.append(heads)
        _store_heads(o_ref, outs)

    _per_step(i, kva_ref.shape[1] // (Q_TILES * t), query_step)


def _nsa(proj, kc, vc_t, bias_diag, bias_sub, bias_cmp, cover_t, expand, t=TILE):
    bsz, s, _ = proj.shape
    n_cmp = kc.shape[1]
    kv_scratch = [pltpu.VMEM((s, HEAD_DIM), _MXU), pltpu.VMEM((s // t, ACC_ROWS, t), _MXU)]
    return pl.pallas_call(
        _nsa_kernel,
        grid=(bsz, s // (Q_TILES * t)),
        in_specs=_group_specs(t, s, CB_NS_Q, CB_NS_A, CB_NS_B)
        + [pl.BlockSpec((1, Q_TILES * t, GROUP_WIDTH), lambda b, i: (b, i, CB_NS_B)),
           pl.BlockSpec((1, n_cmp, HEAD_DIM), lambda b, i: (b, 0, 0)),
           pl.BlockSpec((1, HEAD_DIM, n_cmp), lambda b, i: (b, 0, 0))]
        + _bias_specs(t, 1)
        + [pl.BlockSpec((GROUP_HEADS, n_cmp, Q_TILES * t), lambda b, i: (0, 0, i)),
           pl.BlockSpec(cover_t.shape, lambda b, i: (0, 0)),
           pl.BlockSpec(expand.shape, lambda b, i: (0, 0))],
        scratch_shapes=kv_scratch + kv_scratch,
        compiler_params=_params("parallel", "arbitrary"),
        name="nsa",
        **_group_out(bsz, s, t),
    )(proj, proj, proj, proj, kc, vc_t, bias_diag, bias_sub, bias_cmp, cover_t, expand)


def _nsa_constants(s):
    n_cmp = (s - CMP_LEN) // CMP_STRIDE + 1
    n_slc = s // SLC_LEN
    assert n_cmp + 1 == s // CMP_STRIDE and n_slc % SUBLANES == 0
    c_start = np.arange(n_cmp) * CMP_STRIDE
    s_start = np.arange(n_slc) * SLC_LEN
    cover = np.clip(np.minimum((c_start + CMP_LEN - 1)[:, None], (s_start + SLC_LEN - 1)[None, :])
                    - np.maximum(c_start[:, None], s_start[None, :]) + 1, 0, None) / CMP_LEN
    cover_t = np.zeros((n_slc, n_cmp + 1), np.float32)
    cover_t[:, :n_cmp] = cover.T
    expand = (np.arange(s)[:, None] // SLC_LEN == np.arange(n_slc)[None, :]).astype(np.float32)
    return jnp.asarray(cover_t), jnp.asarray(expand, _MXU)


def kernel(x, w_in, w_out, w_up, w_down, norm_attn, norm_mlp, cmp_pos_k, cmp_pos_v, cmp_k_w1, cmp_k_w2,
           cmp_v_w1, cmp_v_w2, diff_lambda, diff_subln, rel_bias, final_norm):
    bsz, s, d = x.shape
    depth = w_in.shape[0]
    t = TILE
    n_chunk = s // CMP_STRIDE
    assert s % MOBA_BLOCK == 0 and MOBA_BLOCK % t == 0 and WINDOW % t == 0 and t >= MAX_DISTANCE

    assert w_in.shape[2] + PAD_COLS == D_IN_PAD
    w_in_c, w_out_c, w_up_c, w_down_c = (w.astype(_MXU) for w in (w_in, w_out, w_up, w_down))
    wk1, wk2 = cmp_k_w1.astype(_MXU), cmp_k_w2.astype(_MXU)
    wv1, wv2t = cmp_v_w1.astype(_MXU), jnp.swapaxes(cmp_v_w2, 1, 2).astype(_MXU)

    table_t = jnp.pad(rel_bias.T, ((0, 0), (0, LANES - N_BUCKETS)))
    tiles = dict(n_heads=rel_bias.shape[1], head0=0, rows=t, cols=t, col_tile=t, row_stride=-1, col_stride=1)
    bias_diag = _bias_table(table_t, offset=0, **tiles)
    bias_sub = _bias_table(table_t, offset=t, **tiles)
    bias_cmp = _bias_table(table_t, n_heads=GROUP_HEADS, head0=GROUP_HEADS, rows=n_chunk, cols=s,
                           col_tile=t, row_stride=-CMP_STRIDE, col_stride=1, offset=-(CMP_LEN - 1))
    cover_t, expand = _nsa_constants(s)
    col_scale = np.ones((1, D_IN_PAD), np.float32)
    for cb, width in ((CB_SB_Q, HEAD_DIM), (CB_MB_Q, HEAD_DIM), (CB_NS_Q, HEAD_DIM), (CB_DF_Q, DIFF_HALF)):
        col_scale[:, cb * GROUP_WIDTH:(cb + 1) * GROUP_WIDTH] = width ** -0.5 * LOG2E
    col_scale = jnp.asarray(col_scale)

    x2 = x.reshape(bsz * s, d)
    for layer in range(depth):
        proj, kcv = _norm_matmul(x2, norm_attn[layer][None], w_in_c, layer, col_scale,
                                 gap=(COLS_BEFORE_PAD, COLS_BEFORE_PAD + PAD_COLS),
                                 side_col=CB_NS_A * GROUP_WIDTH, side_width=2 * HEAD_DIM)
        proj = proj.reshape(bsz, s, D_IN_PAD)
        o_sb = _stick_breaking(proj)
        o_mb = _moba(proj, bias_diag, bias_sub)
        kc, vc_t = _compress(kcv.reshape(bsz, s, 2 * HEAD_DIM), cmp_pos_k, cmp_pos_v, wk1, wk2, wv1, wv2t, layer)
        o_ns = _nsa(proj, kc, vc_t, bias_diag, bias_sub, bias_cmp, cover_t, expand)
        lambda_init = 0.8 - 0.6 * math.exp(-0.3 * layer)
        o_df = _diff(proj, diff_lambda[layer], diff_subln[layer][:, None], bias_diag, bias_sub, lambda_init)
        groups = [o.reshape(bsz * s, GROUP_WIDTH) for o in (o_sb, o_mb, o_ns, o_df)]
        x2 = _out_mlp(x2, groups, w_out_c, norm_mlp[layer][None], w_up_c, w_down_c, layer,
                      final_norm[None], final_norm=(layer == depth - 1))
    return x2.reshape(bsz, s, d)
```

```python
import functools
import math
from typing import Any, NamedTuple, Optional, Sequence

import numpy as np
import jax
import jax.numpy as jnp
from jax import lax
from jax.experimental import pallas as pl
from jax.experimental.pallas import tpu as pltpu

HEAD_DIM = 64
GROUP_HEADS = 4
GROUP_WIDTH = GROUP_HEADS * HEAD_DIM
NORM_EPS = 1e-6
NEG_INF = -1e30
BIG = 1e30
FORCE = 1e30
TINY = 1e-30
SOFTPLUS_CLAMP = 64.0
PICKED = -3e38
LOG2E = math.log2(math.e)
N_BUCKETS = 32
MAX_DISTANCE = 128
MOBA_BLOCK = 256
MOBA_TOPK = 3
CMP_LEN = 32
CMP_STRIDE = 16
SLC_LEN = 64
SLC_TOPN = 4
WINDOW = 512
DIFF_HALF = HEAD_DIM // 2
LANES = 128
SUBLANES = 8
TILE = 256
BF16_ROWS = 16
ACC_ROWS = HEAD_DIM + BF16_ROWS
N_GATES = 3 * GROUP_HEADS
COLS_BEFORE_PAD = 9 * GROUP_WIDTH - 2 * HEAD_DIM + N_GATES
PAD_COLS = 2 * HEAD_DIM - N_GATES
CB_SB_Q, CB_SB_K, CB_SB_V, CB_MB_Q, CB_MB_K, CB_MB_V, CB_NS_Q, CB_NS_A, CB_NS_B, CB_DF_Q, CB_DF_K, CB_DF_V = range(12)
D_IN_PAD = 12 * GROUP_WIDTH

_MXU = jnp.bfloat16
_VMEM_LIMIT = 56 * 1024 * 1024
_HEADS = range(GROUP_HEADS)
Q_TILES = 8
LOOKAHEAD = 2


def _dot(a, b, precision=None):
    return jnp.dot(a, b, precision=precision, preferred_element_type=jnp.float32)


def _dot_nt(a, b, precision=None):
    return lax.dot_general(a, b, (((1,), (1,)), ((), ())), precision=precision,
                           preferred_element_type=jnp.float32)


def _rms(x, g):
    return x * lax.rsqrt(jnp.mean(x * x, axis=-1, keepdims=True) + NORM_EPS) * g


def _params(*sem):
    return pltpu.CompilerParams(dimension_semantics=sem, vmem_limit_bytes=_VMEM_LIMIT)


def _norm_matmul_kernel(x_ref, g_ref, w_ref, scale_ref, o_ref, side_ref, wp_ref, *, tn, side_col, gap):
    @pl.when(pl.program_id(0) == 0)
    def _():
        lo, hi = gap
        wp_ref[:, :lo] = w_ref[:, :lo]
        wp_ref[:, lo:hi] = jnp.zeros((wp_ref.shape[0], hi - lo), wp_ref.dtype)
        wp_ref[:, hi:] = w_ref[:, lo:lo + wp_ref.shape[1] - hi]

    h = _rms(x_ref[...], g_ref[...]).astype(_MXU)
    for j in range(wp_ref.shape[1] // tn):
        cols = slice(j * tn, (j + 1) * tn)
        acc = _dot(h, wp_ref[:, cols])
        o_ref[:, cols] = (acc * scale_ref[:, cols]).astype(o_ref.dtype)
        if j * tn <= side_col < (j + 1) * tn:
            side_ref[...] = acc[:, side_col - j * tn:side_col - j * tn + side_ref.shape[1]]


def _norm_matmul(x, g, w, layer, col_scale, *, gap, side_col, side_width, tm=512, tn=1024):
    m, d = x.shape
    n = w.shape[2]
    assert side_col % LANES == 0 and side_col // tn == (side_col + side_width - 1) // tn
    return pl.pallas_call(
        functools.partial(_norm_matmul_kernel, tn=tn, side_col=side_col, gap=gap),
        grid=(m // tm,),
        in_specs=[pl.BlockSpec((tm, d), lambda i: (i, 0)),
                  pl.BlockSpec((1, d), lambda i: (0, 0)),
                  pl.BlockSpec((None, d, w.shape[2]), lambda i: (layer, 0, 0), pipeline_mode=pl.Buffered(1)),
                  pl.BlockSpec((1, n), lambda i: (0, 0))],
        out_specs=[pl.BlockSpec((tm, n), lambda i: (i, 0)),
                   pl.BlockSpec((tm, side_width), lambda i: (i, 0))],
        out_shape=[jax.ShapeDtypeStruct((m, n), _MXU),
                   jax.ShapeDtypeStruct((m, side_width), jnp.float32)],
        scratch_shapes=[pltpu.VMEM((d, n), _MXU)],
        compiler_params=_params("arbitrary"),
        name="norm_in_proj",
    )(x, g, w, col_scale)


def _out_mlp_kernel(x_ref, a_ref, b_ref, c_ref, d_ref, wo_ref, g_ref, wu_ref, wd_ref, gf_ref, o_ref,
                    *, final_norm, tf):
    mixed = jnp.concatenate([a_ref[...], b_ref[...], c_ref[...], d_ref[...]], axis=1)
    y = x_ref[...] + _dot(mixed, wo_ref[...])
    h = _rms(y, g_ref[...]).astype(_MXU)
    for c in range(wu_ref.shape[1] // tf):
        u = jnp.square(jnp.maximum(_dot(h, wu_ref[:, c * tf:(c + 1) * tf]), 0.0))
        y = y + _dot(u.astype(_MXU), wd_ref[c * tf:(c + 1) * tf, :])
    if final_norm:
        y = _rms(y, gf_ref[...])
    o_ref[...] = y


def _out_mlp(x, groups, w_out, g, w_up, w_down, layer, g_final, *, final_norm, tm=512, tf=2048):
    m, d = x.shape
    f = w_up.shape[2]
    gspec = pl.BlockSpec((tm, GROUP_WIDTH), lambda i: (i, 0))
    row = pl.BlockSpec((1, d), lambda i: (0, 0))

    def resident(rows, cols):
        return pl.BlockSpec((None, rows, cols), lambda i: (layer, 0, 0), pipeline_mode=pl.Buffered(1))

    return pl.pallas_call(
        functools.partial(_out_mlp_kernel, final_norm=final_norm, tf=tf),
        grid=(m // tm,),
        in_specs=[pl.BlockSpec((tm, d), lambda i: (i, 0)), gspec, gspec, gspec, gspec,
                  resident(d, d), row, resident(d, f), resident(f, d), row],
        out_specs=pl.BlockSpec((tm, d), lambda i: (i, 0)),
        out_shape=jax.ShapeDtypeStruct((m, d), jnp.float32),
        compiler_params=_params("parallel"),
        name="out_proj_mlp_residual",
    )(x, *groups, w_out, g, w_up, w_down, g_final)


def _t5_bucket(dist):
    n = jnp.maximum(dist, 0)
    max_exact = N_BUCKETS // 2
    nf = jnp.maximum(n, 1).astype(jnp.float32)
    large = max_exact + (jnp.log(nf / max_exact) / math.log(MAX_DISTANCE / max_exact)
                         * (N_BUCKETS - max_exact)).astype(jnp.int32)
    large = jnp.minimum(large, N_BUCKETS - 1)
    return jnp.where(n < max_exact, n, large)


def _bias_kernel(tab_ref, o_ref, *, row_stride, col_stride, offset, head0):
    nh, tr, tc = o_ref.shape
    for blk in range(tc // LANES):
        rows = lax.broadcasted_iota(jnp.int32, (tr, LANES), 0)
        cols = lax.broadcasted_iota(jnp.int32, (tr, LANES), 1) + (pl.program_id(0) * tc + blk * LANES)
        bucket = _t5_bucket(rows * row_stride + cols * col_stride + offset)
        for h in range(nh):
            row = tab_ref[head0 + h:head0 + h + 1, :]
            row = (row - row[:, N_BUCKETS - 1:N_BUCKETS]) * LOG2E
            o_ref[h, :, blk * LANES:(blk + 1) * LANES] = jnp.take_along_axis(
                jnp.broadcast_to(row, (tr, LANES)), bucket, axis=1, mode="promise_in_bounds")


def _bias_table(table_t, *, n_heads, head0, rows, cols, col_tile, row_stride, col_stride, offset):
    return pl.pallas_call(
        functools.partial(_bias_kernel, row_stride=row_stride, col_stride=col_stride,
                          offset=offset, head0=head0),
        grid=(cols // col_tile,),
        in_specs=[pl.BlockSpec(table_t.shape, lambda i: (0, 0))],
        out_specs=pl.BlockSpec((n_heads, rows, col_tile), lambda i: (0, 0, i)),
        out_shape=jax.ShapeDtypeStruct((n_heads, rows, cols), jnp.float32),
        compiler_params=_params("parallel"),
        name="t5_bias_tiles",
    )(table_t)


def _softmax_init(t):
    return (jnp.full((1, t), NEG_INF, jnp.float32), jnp.zeros((ACC_ROWS, t), jnp.float32))


class _KeyTile(NamedTuple):
    kts: Sequence[Any]
    vts: Sequence[Any]
    biases: Optional[Sequence[Any]] = None
    emasks: Optional[Sequence[Any]] = None
    qmasks: Optional[Sequence[Any]] = None


def _round_robin(lists):
    out = []
    for rank in range(max(map(len, lists), default=0)):
        out.extend(items[rank] for items in lists if rank < len(items))
    return out


def _softmax_jobs(t, jobs):
    built = {}

    def tiles_of(job, g):
        if (job, g) not in built:
            built[job, g] = [make() for make in jobs[job][2][g]]
        return built[job, g]

    def scores_of(job, g, c):
        qts = jobs[job][1]
        row = []
        for tile in tiles_of(job, g):
            s = _dot(tile.kts[c], qts[c])
            if tile.biases is not None:
                s = s + tile.biases[c]
            if tile.emasks is not None:
                s = jnp.where(tile.emasks[c], s, NEG_INF)
            row.append(s.astype(_MXU))
        return row

    def update(state, job, g, c, scores):
        m, acc = state
        m_new = m
        for tile, s in zip(tiles_of(job, g), scores):
            tile_max = jnp.max(s, axis=0, keepdims=True).astype(jnp.float32)
            if tile.qmasks is not None:
                tile_max = jnp.where(tile.qmasks[c], tile_max, NEG_INF)
            m_new = jnp.maximum(m_new, tile_max)
        seen = m_new > 0.5 * NEG_INF
        acc = jnp.exp2(m - m_new) * acc
        for tile, s in zip(tiles_of(job, g), scores):
            ok = seen if tile.qmasks is None else jnp.logical_and(seen, tile.qmasks[c])
            acc = acc + _dot(tile.vts[c], jnp.exp2(s - jnp.where(ok, m_new, BIG).astype(_MXU)))
        return m_new, acc

    units = _round_robin([[(job, g, c) for g in range(len(groups)) for c in range(n)]
                          for job, (n, _, groups) in enumerate(jobs)])
    lookahead = LOOKAHEAD * len(jobs)
    states = [[_softmax_init(t) for _ in range(n)] for n, _, _ in jobs]
    pending = {k: scores_of(*units[k]) for k in range(min(lookahead, len(units)))}
    for k, (job, g, c) in enumerate(units):
        if k + lookahead < len(units):
            pending[k + lookahead] = scores_of(*units[k + lookahead])
        states[job][c] = update(states[job][c], job, g, c, pending.pop(k))
    return states


def _pairs(items):
    return [items[p:p + 2] for p in range(0, len(items), 2)]


def _per_step(i, n_steps, body):
    for step in range(n_steps):
        pl.when(i == step)(functools.partial(body, step))


def _softmax_out(state):
    acc = state[1]
    return acc[:HEAD_DIM] / jnp.maximum(acc[HEAD_DIM:HEAD_DIM + 1], TINY)


def _top_k_rows(score, row_f, k):
    sel = jnp.zeros(score.shape, jnp.float32)
    for _ in range(k):
        mx = jnp.max(score, axis=0, keepdims=True)
        idx = jnp.min(jnp.where(score == mx, row_f, float(score.shape[0])), axis=0, keepdims=True)
        pick = row_f == idx
        sel = jnp.where(pick, 1.0, sel)
        score = jnp.where(pick, PICKED, score)
    return sel


def _tile_iotas(t):
    return (lax.broadcasted_iota(jnp.int32, (t, t), 0), lax.broadcasted_iota(jnp.int32, (t, t), 1))


def _key_rows(ref, j, t):
    if isinstance(j, int):
        return ref[j * t:(j + 1) * t, :]
    return ref[pl.ds(pl.multiple_of(j * t, t), t), :]


def _transposed(ref_block):
    return ref_block.astype(jnp.float32).T


def _queries_t(q_ref, width, t):
    qt = _transposed(q_ref[0]).astype(_MXU)
    return [[qt[c * width:(c + 1) * width, r * t:(r + 1) * t] for c in range(GROUP_WIDTH // width)]
            for r in range(Q_TILES)]


def _fill_values_t(vt_ref, v_ref, col0, t):
    n_tiles, rows, _ = vt_ref.shape
    lane_block = (col0 // LANES) * LANES
    for c in range(n_tiles):
        blk = _transposed(v_ref[0, c * t:(c + 1) * t, lane_block:lane_block + LANES])
        vt_ref[c, 0:HEAD_DIM, :] = blk[col0 - lane_block:col0 - lane_block + HEAD_DIM].astype(vt_ref.dtype)
        if rows == ACC_ROWS:
            first = lax.broadcasted_iota(jnp.int32, (rows - HEAD_DIM, t), 0) == 0
            vt_ref[c, HEAD_DIM:rows, :] = jnp.where(first, 1.0, 0.0).astype(vt_ref.dtype)


def _group_specs(t, s, cb_q, cb_k, cb_v):
    return [pl.BlockSpec((1, Q_TILES * t, GROUP_WIDTH), lambda b, i: (b, i, cb_q)),
            pl.BlockSpec((1, s, GROUP_WIDTH), lambda b, i: (b, 0, cb_k)),
            pl.BlockSpec((1, s, GROUP_WIDTH), lambda b, i: (b, 0, cb_v))]


def _group_out(bsz, s, t):
    return dict(out_specs=pl.BlockSpec((1, Q_TILES * t, GROUP_WIDTH), lambda b, i: (b, i, 0)),
                out_shape=jax.ShapeDtypeStruct((bsz, s, GROUP_WIDTH), _MXU))


def _bias_specs(t, head_group):
    spec = pl.BlockSpec((GROUP_HEADS, t, t), lambda b, i: (head_group, 0, 0))
    return [spec, spec]


def _store_heads(o_ref, outs_t):
    tiles = [jnp.concatenate(heads, axis=0) for heads in outs_t]
    o_ref[0] = jnp.concatenate(tiles, axis=1).T.astype(o_ref.dtype)


def _sb_kernel(q_ref, k_ref, v_ref, o_ref, kb_ref, vt_ref):
    t = q_ref.shape[1] // Q_TILES
    i = pl.program_id(1)
    key, qry = _tile_iotas(t)
    strict = key < qry
    later = jnp.where(qry > key, 1.0, 0.0).astype(_MXU)

    @pl.when(i == 0)
    def _():
        for h in _HEADS:
            kb_ref[h] = k_ref[0, :, h * HEAD_DIM:(h + 1) * HEAD_DIM].astype(kb_ref.dtype)
            _fill_values_t(vt_ref.at[h], v_ref, h * HEAD_DIM, t)

    def query_step(step):
        qts = _queries_t(q_ref, HEAD_DIM, t)
        units = _round_robin([[(r, step * Q_TILES + r, pair, h)
                               for pair in _pairs(list(range(step * Q_TILES + r, -1, -1))) for h in _HEADS]
                              for r in range(Q_TILES)])
        zero = (jnp.zeros((HEAD_DIM, t), jnp.float32), jnp.zeros((1, t), jnp.float32))
        carry = [[zero] * GROUP_HEADS for _ in range(Q_TILES)]
        zs, log_keeps, suffixes = {}, {}, {}

        def scores(u):
            r, _, pair, h = units[u]
            zs[u] = [_dot(_key_rows(kb_ref.at[h], j, t), qts[r][h]) for j in pair]

        def keeps(u):
            _, k, pair, _ = units[u]
            log_keeps[u], suffixes[u] = [], []
            for j, z in zip(pair, zs[u]):
                drop = jnp.maximum(jnp.log2(1.0 + jnp.exp2(jnp.minimum(z, SOFTPLUS_CLAMP))), z)
                if j == k:
                    drop = jnp.where(strict, drop, 0.0)
                log_keeps[u].append(drop)
                suffixes[u].append(_dot(later, drop.astype(_MXU)))

        def values(u):
            r, k, pair, h = units[u]
            acc, run = carry[r][h]
            weights = []
            for j, z, drop, suffix in zip(pair, zs.pop(u), log_keeps.pop(u), suffixes.pop(u)):
                a = jnp.exp2(z - drop - suffix + run)
                if j == k:
                    a = jnp.where(strict, a, 0.0)
                weights.append(a.astype(_MXU))
                run = run - (suffix[0:1] + drop[0:1])
            for j, w in zip(pair, weights):
                acc = acc + _dot(vt_ref[h, j], w)
            carry[r][h] = (acc, run)

        stages = (scores, keeps, values)
        for tick in range(len(units) + len(stages) - 1):
            for lag, stage in enumerate(stages):
                if 0 <= tick - lag < len(units):
                    stage(tick - lag)
        _store_heads(o_ref, [[c[0] for c in tile_carry] for tile_carry in carry])

    _per_step(i, k_ref.shape[1] // (Q_TILES * t), query_step)


def _stick_breaking(proj, t=TILE):
    bsz, s, _ = proj.shape
    return pl.pallas_call(
        _sb_kernel,
        grid=(bsz, s // (Q_TILES * t)),
        in_specs=_group_specs(t, s, CB_SB_Q, CB_SB_K, CB_SB_V),
        scratch_shapes=[pltpu.VMEM((GROUP_HEADS, s, HEAD_DIM), _MXU),
                        pltpu.VMEM((GROUP_HEADS, s // t, HEAD_DIM, t), _MXU)],
        compiler_params=_params("parallel", "arbitrary"),
        name="stick_breaking",
        **_group_out(bsz, s, t),
    )(proj, proj, proj)


def _moba_kernel(q_ref, k_ref, v_ref, bd_ref, bs_ref, o_ref, kb_ref, vt_ref, km_ref):
    t = q_ref.shape[1] // Q_TILES
    n_blk = k_ref.shape[1] // MOBA_BLOCK
    tiles_per_blk = MOBA_BLOCK // t
    blk_shift = int(math.log2(tiles_per_blk))
    i = pl.program_id(1)

    @pl.when(i == 0)
    def _():
        km_ref[...] = jnp.zeros_like(km_ref)
        for h in _HEADS:
            lo, hi = h * HEAD_DIM, (h + 1) * HEAD_DIM
            kb_ref[h] = k_ref[0, :, lo:hi].astype(kb_ref.dtype)
            _fill_values_t(vt_ref.at[h], v_ref, lo, t)
            for n in range(n_blk):
                blk = k_ref[0, n * MOBA_BLOCK:(n + 1) * MOBA_BLOCK, lo:hi]
                km_ref[h, n:n + 1, :] = jnp.mean(blk.astype(jnp.float32), axis=0, keepdims=True)

    def query_step(step):
        key, qry = _tile_iotas(t)
        causal = key <= qry
        blk_row = lax.broadcasted_iota(jnp.int32, (km_ref.shape[1], t), 0)
        qt = _transposed(q_ref[0])
        jobs = []
        for r in range(Q_TILES):
            k = step * Q_TILES + r
            own = k >> blk_shift
            qts, sels = [], []
            for h in _HEADS:
                qf = qt[h * HEAD_DIM:(h + 1) * HEAD_DIM, r * t:(r + 1) * t]
                qts.append(qf.astype(_MXU))
                gate = _dot(km_ref[h], qf, precision=lax.Precision.HIGHEST)
                gate = jnp.where(blk_row < own, gate, NEG_INF)
                sel = _top_k_rows(gate, blk_row.astype(jnp.float32), min(MOBA_TOPK, n_blk - 1))
                sels.append(jnp.where(blk_row < own, sel, 0.0))

            def tile(own, sels, j, bias_ref=None, emask=None):
                n = j >> blk_shift
                return _KeyTile([_key_rows(kb_ref.at[h], j, t) for h in _HEADS], [vt_ref[h, j] for h in _HEADS],
                                None if bias_ref is None else [bias_ref[h] for h in _HEADS],
                                None if emask is None else [emask] * GROUP_HEADS,
                                None if n == own else [sels[h][n:n + 1] > 0.5 for h in _HEADS])

            tile = functools.partial(tile, own, sels)
            groups = _pairs([functools.partial(tile, j) for j in range(k - 1)])
            groups.append(([functools.partial(tile, k - 1, bs_ref)] if k else [])
                          + [functools.partial(tile, k, bd_ref, causal)])
            jobs.append((GROUP_HEADS, qts, groups))
        _store_heads(o_ref, [[_softmax_out(st) for st in states] for states in _softmax_jobs(t, jobs)])

    _per_step(i, k_ref.shape[1] // (Q_TILES * t), query_step)


def _kv_scratch(s, t, key_dim=HEAD_DIM, n_keys=GROUP_HEADS):
    return [pltpu.VMEM((n_keys, s, key_dim), _MXU), pltpu.VMEM((GROUP_HEADS, s // t, ACC_ROWS, t), _MXU)]


def _moba(proj, bias_diag, bias_sub, t=TILE):
    bsz, s, _ = proj.shape
    n_blk_pad = -(-(s // MOBA_BLOCK) // SUBLANES) * SUBLANES
    return pl.pallas_call(
        _moba_kernel,
        grid=(bsz, s // (Q_TILES * t)),
        in_specs=_group_specs(t, s, CB_MB_Q, CB_MB_K, CB_MB_V) + _bias_specs(t, 0),
        scratch_shapes=_kv_scratch(s, t) + [pltpu.VMEM((GROUP_HEADS, n_blk_pad, HEAD_DIM), jnp.float32)],
        compiler_params=_params("parallel", "arbitrary"),
        name="moba",
        **_group_out(bsz, s, t),
    )(proj, proj, proj, bias_diag, bias_sub)


def _diff_kernel(lam_ref, g_ref, q_ref, k_ref, v_ref, bd_ref, bs_ref, o_ref, kb_ref, vt_ref, *, lambda_init):
    t = q_ref.shape[1] // Q_TILES
    i = pl.program_id(1)
    key, qry = _tile_iotas(t)
    causal = key <= qry
    lv = lam_ref[...]
    lam = (jnp.exp(jnp.sum(lv[0:1] * lv[1:2], keepdims=True))
           - jnp.exp(jnp.sum(lv[2:3] * lv[3:4], keepdims=True)) + lambda_init)
    halves = range(2 * GROUP_HEADS)

    @pl.when(i == 0)
    def _():
        for c in halves:
            kb_ref[c] = k_ref[0, :, c * DIFF_HALF:(c + 1) * DIFF_HALF].astype(kb_ref.dtype)
        for h in _HEADS:
            _fill_values_t(vt_ref.at[h], v_ref, h * HEAD_DIM, t)

    def tile(j, bias_ref=None, emask=None):
        n = len(halves)
        return _KeyTile([_key_rows(kb_ref.at[c], j, t) for c in halves], [vt_ref[c // 2, j] for c in halves],
                        None if bias_ref is None else [bias_ref[c // 2] for c in halves],
                        None if emask is None else [emask] * n)

    def query_step(step):
        qts = _queries_t(q_ref, DIFF_HALF, t)
        jobs = []
        for r in range(Q_TILES):
            k = step * Q_TILES + r
            groups = _pairs([functools.partial(tile, j) for j in range(k - 1)])
            groups.append(([functools.partial(tile, k - 1, bs_ref)] if k else [])
                          + [functools.partial(tile, k, bd_ref, causal)])
            jobs.append((len(halves), qts[r], groups))
        outs = []
        for states in _softmax_jobs(t, jobs):
            heads = []
            for h in _HEADS:
                o = _softmax_out(states[2 * h]) - lam * _softmax_out(states[2 * h + 1])
                o = o * lax.rsqrt(jnp.mean(o * o, axis=0, keepdims=True) + NORM_EPS) * g_ref[...]
                heads.append(o * (1.0 - lambda_init))
            outs.append(heads)
        _store_heads(o_ref, outs)

    _per_step(i, k_ref.shape[1] // (Q_TILES * t), query_step)


def _diff(proj, lam_params, subln, bias_diag, bias_sub, lambda_init, t=TILE):
    bsz, s, _ = proj.shape
    return pl.pallas_call(
        functools.partial(_diff_kernel, lambda_init=lambda_init),
        grid=(bsz, s // (Q_TILES * t)),
        in_specs=[pl.BlockSpec(lam_params.shape, lambda b, i: (0, 0)),
                  pl.BlockSpec(subln.shape, lambda b, i: (0, 0))]
        + _group_specs(t, s, CB_DF_Q, CB_DF_K, CB_DF_V) + _bias_specs(t, 2),
        scratch_shapes=_kv_scratch(s, t, key_dim=DIFF_HALF, n_keys=2 * GROUP_HEADS),
        compiler_params=_params("parallel", "arbitrary"),
        name="diff_attention",
        **_group_out(bsz, s, t),
    )(lam_params, subln, proj, proj, proj, bias_diag, bias_sub)


def _compress_kernel(kcv_ref, pk_ref, pv_ref, wk1_ref, wk2_ref, wv1_ref, wv2t_ref, kc_ref, vct_ref):
    n_chunk = kc_ref.shape[1]

    branches = ((0, pk_ref, wk1_ref), (HEAD_DIM, pv_ref, wv1_ref))
    tops = [jnp.zeros((n_chunk, w1_ref.shape[1]), jnp.float32) for _, _, w1_ref in branches]
    bots = list(tops)
    for l in range(CMP_STRIDE):
        tokens = kcv_ref[0, pl.ds(l, n_chunk, stride=CMP_STRIDE), :]
        l2 = CMP_STRIDE + l
        for n, (col0, p_ref, w1_ref) in enumerate(branches):
            x = tokens[:, col0:col0 + HEAD_DIM]
            tops[n] = tops[n] + _dot((x + p_ref[l:l + 1, :]).astype(_MXU),
                                     w1_ref[l * HEAD_DIM:(l + 1) * HEAD_DIM, :])
            bots[n] = bots[n] + _dot((x + p_ref[l2:l2 + 1, :]).astype(_MXU),
                                     w1_ref[l2 * HEAD_DIM:(l2 + 1) * HEAD_DIM, :])
    hidden = [jax.nn.gelu(top + pltpu.roll(bot, n_chunk - 1, axis=0)).astype(_MXU) for top, bot in zip(tops, bots)]
    kc_ref[0] = _dot(hidden[0], wk2_ref[...])
    vct_ref[0] = _dot_nt(wv2t_ref[...], hidden[1])


def _compress(kcv, pos_k, pos_v, wk1, wk2, wv1, wv2t, layer):
    bsz, s, width = kcv.shape
    n_chunk = s // CMP_STRIDE

    def full(a):
        return pl.BlockSpec((None,) + a.shape[1:], lambda b: (layer, 0, 0))

    return pl.pallas_call(
        _compress_kernel,
        grid=(bsz,),
        in_specs=[pl.BlockSpec((1, s, width), lambda b: (b, 0, 0)),
                  full(pos_k), full(pos_v), full(wk1), full(wk2), full(wv1), full(wv2t)],
        out_specs=[pl.BlockSpec((1, n_chunk, HEAD_DIM), lambda b: (b, 0, 0)),
                   pl.BlockSpec((1, HEAD_DIM, n_chunk), lambda b: (b, 0, 0))],
        out_shape=[jax.ShapeDtypeStruct((bsz, n_chunk, HEAD_DIM), jnp.float32),
                   jax.ShapeDtypeStruct((bsz, HEAD_DIM, n_chunk), jnp.float32)],
        compiler_params=_params("parallel"),
        name="nsa_compress",
    )(kcv, pos_k, pos_v, wk1, wk2, wv1, wv2t)


def _nsa_kernel(q_ref, kva_ref, kvb_ref, gq_ref, kc_ref, vct_ref, bd_ref, bs_ref, bc_ref, cover_ref, e_ref,
                o_ref, ks_ref, vst_ref, kw_ref, vwt_ref):
    t = q_ref.shape[1] // Q_TILES
    i = pl.program_id(1)
    key, qry = _tile_iotas(t)
    causal = key <= qry
    ks_col, vs_col, kw_col, vw_col, gate_col = 2 * HEAD_DIM, 3 * HEAD_DIM, 0, HEAD_DIM, 2 * HEAD_DIM

    @pl.when(i == 0)
    def _():
        ks_ref[...] = kva_ref[0, :, ks_col:ks_col + HEAD_DIM].astype(ks_ref.dtype)
        kw_ref[...] = kvb_ref[0, :, kw_col:kw_col + HEAD_DIM].astype(kw_ref.dtype)
        _fill_values_t(vst_ref, kva_ref, vs_col, t)
        _fill_values_t(vwt_ref, kvb_ref, vw_col, t)

    def tile(k_ref, vt_ref, j, bias_ref=None, emask=None):
        n = GROUP_HEADS
        emask = emask() if callable(emask) else emask
        return _KeyTile([_key_rows(k_ref, j, t)] * n, [vt_ref[j]] * n,
                        None if bias_ref is None else [bias_ref[h] for h in _HEADS],
                        None if emask is None else [emask] * n)

    def query_step(step):
        qts = _queries_t(q_ref, HEAD_DIM, t)
        n_cmp = kc_ref.shape[1]
        n_slc = cover_ref.shape[0]
        kc = kc_ref[0].astype(_MXU)
        vct = vct_ref[0].astype(_MXU)
        c_row = lax.broadcasted_iota(jnp.int32, (n_cmp, t), 0)
        c_col = lax.broadcasted_iota(jnp.int32, (n_cmp, t), 1)
        s_row = lax.broadcasted_iota(jnp.int32, (n_slc, t), 0)
        s_col = lax.broadcasted_iota(jnp.int32, (n_slc, t), 1)
        tiles = [step * Q_TILES + r for r in range(Q_TILES)]

        o_cmp, importance = [], []
        for r, k in enumerate(tiles):
            visible = c_col + k * t >= c_row * CMP_STRIDE + (CMP_LEN - 1)
            cmp_scores = [_dot(kc, qts[r][h]) for h in _HEADS]
            cmp_probs = []
            p_sum = jnp.zeros((n_cmp, t), jnp.float32)
            for h in _HEADS:
                sc = jnp.where(visible, cmp_scores[h] + bc_ref[h, :, r * t:(r + 1) * t], NEG_INF)
                e = jnp.where(visible, jnp.exp2(sc - jnp.max(sc, axis=0, keepdims=True)), 0.0)
                p = e / jnp.maximum(jnp.sum(e, axis=0, keepdims=True), TINY)
                cmp_probs.append(p.astype(_MXU))
                p_sum = p_sum + p
            o_cmp.append([_dot(vct, cmp_probs[h]) for h in _HEADS])
            importance.append(_dot(cover_ref[...], p_sum, precision=lax.Precision.HIGHEST))

        n_back = WINDOW // t
        jobs = []
        for r, k in enumerate(tiles):
            window = [functools.partial(tile, kw_ref, vwt_ref, k - n_back, None, qry < key)] if k >= n_back else []
            for back in range(min(n_back - 1, k), 0, -1):
                window.append(functools.partial(tile, kw_ref, vwt_ref, k - back, bs_ref if back == 1 else None))
            window.append(functools.partial(tile, kw_ref, vwt_ref, k, bd_ref, causal))
            jobs.append((GROUP_HEADS, qts[r], [window]))
        o_win = [[_softmax_out(st) for st in states] for states in _softmax_jobs(t, jobs)]

        jobs = []
        for r, k in enumerate(tiles):
            own = jnp.right_shift(s_col + k * t, int(math.log2(SLC_LEN)))
            score = jnp.where(s_row == own, FORCE, jnp.where(s_row < own, importance[r], NEG_INF))
            sel = _top_k_rows(score, s_row.astype(jnp.float32), min(SLC_TOPN, n_slc)).astype(_MXU)

            def chosen(sel, j, diagonal=False):
                mask = _dot(_key_rows(e_ref, j, t), sel) > 0.5
                return jnp.logical_and(mask, causal) if diagonal else mask

            chosen = functools.partial(chosen, sel)
            groups = _pairs([functools.partial(tile, ks_ref, vst_ref, j, None, functools.partial(chosen, j))
                             for j in range(k - 1)])
            groups.append(([functools.partial(tile, ks_ref, vst_ref, k - 1, bs_ref, functools.partial(chosen, k - 1))]
                           if k else [])
                          + [functools.partial(tile, ks_ref, vst_ref, k, bd_ref, functools.partial(chosen, k, True))])
            jobs.append((GROUP_HEADS, qts[r], groups))
        o_slc = [[_softmax_out(st) for st in states] for states in _softmax_jobs(t, jobs)]

        gates = _transposed(gq_ref[0, :, (gate_col // LANES) * LANES:(gate_col // LANES + 1) * LANES])
        gates = 1.0 / (1.0 + jnp.exp(-gates[gate_col % LANES:gate_col % LANES + N_GATES + 4]))
        outs = []
        for r in range(Q_TILES):
            heads = []
            for h in _HEADS:
                g = [gates[br * GROUP_HEADS + h:br * GROUP_HEADS + h + 1, r * t:(r + 1) * t] for br in range(3)]
                heads.append(g[0] * o_cmp[r][h] + g[1] * o_slc[r][h] + g[2] * o_win[r][h])
            outs.append(heads)
        _store_heads(o_ref, outs)

    _per_step(i, kva_ref.shape[1] // (Q_TILES * t), query_step)


def _nsa(proj, kc, vc_t, bias_diag, bias_sub, bias_cmp, cover_t, expand, t=TILE):
    bsz, s, _ = proj.shape
    n_cmp = kc.shape[1]
    kv_scratch = [pltpu.VMEM((s, HEAD_DIM), _MXU), pltpu.VMEM((s // t, ACC_ROWS, t), _MXU)]
    return pl.pallas_call(
        _nsa_kernel,
        grid=(bsz, s // (Q_TILES * t)),
        in_specs=_group_specs(t, s, CB_NS_Q, CB_NS_A, CB_NS_B)
        + [pl.BlockSpec((1, Q_TILES * t, GROUP_WIDTH), lambda b, i: (b, i, CB_NS_B)),
           pl.BlockSpec((1, n_cmp, HEAD_DIM), lambda b, i: (b, 0, 0)),
           pl.BlockSpec((1, HEAD_DIM, n_cmp), lambda b, i: (b, 0, 0))]
        + _bias_specs(t, 1)
        + [pl.BlockSpec((GROUP_HEADS, n_cmp, Q_TILES * t), lambda b, i: (0, 0, i)),
           pl.BlockSpec(cover_t.shape, lambda b, i: (0, 0)),
           pl.BlockSpec(expand.shape, lambda b, i: (0, 0))],
        scratch_shapes=kv_scratch + kv_scratch,
        compiler_params=_params("parallel", "arbitrary"),
        name="nsa",
        **_group_out(bsz, s, t),
    )(proj, proj, proj, proj, kc, vc_t, bias_diag, bias_sub, bias_cmp, cover_t, expand)


def _nsa_constants(s):
    n_cmp = (s - CMP_LEN) // CMP_STRIDE + 1
    n_slc = s // SLC_LEN
    assert n_cmp + 1 == s // CMP_STRIDE and n_slc % SUBLANES == 0
    c_start = np.arange(n_cmp) * CMP_STRIDE
    s_start = np.arange(n_slc) * SLC_LEN
    cover = np.clip(np.minimum((c_start + CMP_LEN - 1)[:, None], (s_start + SLC_LEN - 1)[None, :])
                    - np.maximum(c_start[:, None], s_start[None, :]) + 1, 0, None) / CMP_LEN
    cover_t = np.zeros((n_slc, n_cmp + 1), np.float32)
    cover_t[:, :n_cmp] = cover.T
    expand = (np.arange(s)[:, None] // SLC_LEN == np.arange(n_slc)[None, :]).astype(np.float32)
    return jnp.asarray(cover_t), jnp.asarray(expand, _MXU)


def kernel(x, w_in, w_out, w_up, w_down, norm_attn, norm_mlp, cmp_pos_k, cmp_pos_v, cmp_k_w1, cmp_k_w2,
           cmp_v_w1, cmp_v_w2, diff_lambda, diff_subln, rel_bias, final_norm):
    bsz, s, d = x.shape
    depth = w_in.shape[0]
    t = TILE
    n_chunk = s // CMP_STRIDE
    assert s % MOBA_BLOCK == 0 and MOBA_BLOCK % t == 0 and WINDOW % t == 0 and t >= MAX_DISTANCE

    assert w_in.shape[2] + PAD_COLS == D_IN_PAD
    w_in_c = jnp.pad(w_in, ((0, 0), (0, 0), (0, PAD_COLS))).astype(_MXU)
    w_out_c, w_up_c, w_down_c = (w.astype(_MXU) for w in (w_out, w_up, w_down))
    wk1, wk2 = cmp_k_w1.astype(_MXU), cmp_k_w2.astype(_MXU)
    wv1, wv2t = cmp_v_w1.astype(_MXU), jnp.swapaxes(cmp_v_w2, 1, 2).astype(_MXU)

    table_t = jnp.pad(rel_bias.T, ((0, 0), (0, LANES - N_BUCKETS)))
    tiles = dict(n_heads=rel_bias.shape[1], head0=0, rows=t, cols=t, col_tile=t, row_stride=-1, col_stride=1)
    bias_diag = _bias_table(table_t, offset=0, **tiles)
    bias_sub = _bias_table(table_t, offset=t, **tiles)
    bias_cmp = _bias_table(table_t, n_heads=GROUP_HEADS, head0=GROUP_HEADS, rows=n_chunk, cols=s,
                           col_tile=t, row_stride=-CMP_STRIDE, col_stride=1, offset=-(CMP_LEN - 1))
    cover_t, expand = _nsa_constants(s)
    col_scale = np.ones((1, D_IN_PAD), np.float32)
    for cb, width in ((CB_SB_Q, HEAD_DIM), (CB_MB_Q, HEAD_DIM), (CB_NS_Q, HEAD_DIM), (CB_DF_Q, DIFF_HALF)):
        col_scale[:, cb * GROUP_WIDTH:(cb + 1) * GROUP_WIDTH] = width ** -0.5 * LOG2E
    col_scale = jnp.asarray(col_scale)

    x2 = x.reshape(bsz * s, d)
    for layer in range(depth):
        proj, kcv = _norm_matmul(x2, norm_attn[layer][None], w_in_c, layer, col_scale,
                                 gap=(COLS_BEFORE_PAD, COLS_BEFORE_PAD + PAD_COLS),
                                 side_col=CB_NS_A * GROUP_WIDTH, side_width=2 * HEAD_DIM)
        proj = proj.reshape(bsz, s, D_IN_PAD)
        o_sb = _stick_breaking(proj)
        o_mb = _moba(proj, bias_diag, bias_sub)
        kc, vc_t = _compress(kcv.reshape(bsz, s, 2 * HEAD_DIM), cmp_pos_k, cmp_pos_v, wk1, wk2, wv1, wv2t, layer)
        o_ns = _nsa(proj, kc, vc_t, bias_diag, bias_sub, bias_cmp, cover_t, expand)
        lambda_init = 0.8 - 0.6 * math.exp(-0.3 * layer)
        o_df = _diff(proj, diff_lambda[layer], diff_subln[layer][:, None], bias_diag, bias_sub, lambda_init)
        groups = [o.reshape(bsz * s, GROUP_WIDTH) for o in (o_sb, o_mb, o_ns, o_df)]
        x2 = _out_mlp(x2, groups, w_out_c, norm_mlp[layer][None], w_up_c, w_down_c, layer,
                      final_norm[None], final_norm=(layer == depth - 1))
    return x2.reshape(bsz, s, d)
```

```python
import functools
import math
from typing import Any, NamedTuple, Optional, Sequence

import numpy as np
import jax
import jax.numpy as jnp
from jax import lax
from jax.experimental import pallas as pl
from jax.experimental.pallas import tpu as pltpu

HEAD_DIM = 64
GROUP_HEADS = 4
GROUP_WIDTH = GROUP_HEADS * HEAD_DIM
NORM_EPS = 1e-6
NEG_INF = -1e30
BIG = 1e30
FORCE = 1e30
TINY = 1e-30
SOFTPLUS_CLAMP = 64.0
PICKED = -3e38
LOG2E = math.log2(math.e)
N_BUCKETS = 32
MAX_DISTANCE = 128
MOBA_BLOCK = 256
MOBA_TOPK = 3
CMP_LEN = 32
CMP_STRIDE = 16
SLC_LEN = 64
SLC_TOPN = 4
WINDOW = 512
DIFF_HALF = HEAD_DIM // 2
LANES = 128
SUBLANES = 8
TILE = 256
BF16_ROWS = 16
ACC_ROWS = HEAD_DIM + BF16_ROWS
N_GATES = 3 * GROUP_HEADS
COLS_BEFORE_PAD = 9 * GROUP_WIDTH - 2 * HEAD_DIM + N_GATES
PAD_COLS = 2 * HEAD_DIM - N_GATES
CB_SB_Q, CB_SB_K, CB_SB_V, CB_MB_Q, CB_MB_K, CB_MB_V, CB_NS_Q, CB_NS_A, CB_NS_B, CB_DF_Q, CB_DF_K, CB_DF_V = range(12)
D_IN_PAD = 12 * GROUP_WIDTH

_MXU = jnp.bfloat16
_VMEM_LIMIT = 56 * 1024 * 1024
_HEADS = range(GROUP_HEADS)
Q_TILES = 8
LOOKAHEAD = 2


def _dot(a, b, precision=None):
    return jnp.dot(a, b, precision=precision, preferred_element_type=jnp.float32)


def _dot_nt(a, b, precision=None):
    return lax.dot_general(a, b, (((1,), (1,)), ((), ())), precision=precision,
                           preferred_element_type=jnp.float32)


def _rms(x, g):
    return x * lax.rsqrt(jnp.mean(x * x, axis=-1, keepdims=True) + NORM_EPS) * g


def _params(*sem):
    return pltpu.CompilerParams(dimension_semantics=sem, vmem_limit_bytes=_VMEM_LIMIT)


def _norm_matmul_kernel(x_ref, g_ref, w_ref, scale_ref, o_ref, side_ref, wp_ref, *, tn, side_col, gap):
    @pl.when(pl.program_id(0) == 0)
    def _():
        lo, hi = gap
        wp_ref[:, :lo] = w_ref[:, :lo]
        wp_ref[:, lo:hi] = jnp.zeros((wp_ref.shape[0], hi - lo), wp_ref.dtype)
        wp_ref[:, hi:] = w_ref[:, lo:lo + wp_ref.shape[1] - hi]

    h = _rms(x_ref[...], g_ref[...]).astype(_MXU)
    for j in range(wp_ref.shape[1] // tn):
        cols = slice(j * tn, (j + 1) * tn)
        acc = _dot(h, wp_ref[:, cols])
        o_ref[:, cols] = (acc * scale_ref[:, cols]).astype(o_ref.dtype)
        if j * tn <= side_col < (j + 1) * tn:
            side_ref[...] = acc[:, side_col - j * tn:side_col - j * tn + side_ref.shape[1]]


def _norm_matmul(x, g, w, layer, col_scale, *, gap, side_col, side_width, tm=512, tn=1024):
    m, d = x.shape
    n = w.shape[2]
    assert side_col % LANES == 0 and side_col // tn == (side_col + side_width - 1) // tn
    return pl.pallas_call(
        functools.partial(_norm_matmul_kernel, tn=tn, side_col=side_col, gap=gap),
        grid=(m // tm,),
        in_specs=[pl.BlockSpec((tm, d), lambda i: (i, 0)),
                  pl.BlockSpec((1, d), lambda i: (0, 0)),
                  pl.BlockSpec((None, d, w.shape[2]), lambda i: (layer, 0, 0), pipeline_mode=pl.Buffered(1)),
                  pl.BlockSpec((1, n), lambda i: (0, 0))],
        out_specs=[pl.BlockSpec((tm, n), lambda i: (i, 0)),
                   pl.BlockSpec((tm, side_width), lambda i: (i, 0))],
        out_shape=[jax.ShapeDtypeStruct((m, n), _MXU),
                   jax.ShapeDtypeStruct((m, side_width), jnp.float32)],
        scratch_shapes=[pltpu.VMEM((d, n), _MXU)],
        compiler_params=_params("arbitrary"),
        name="norm_in_proj",
    )(x, g, w, col_scale)


def _out_mlp_kernel(x_ref, a_ref, b_ref, c_ref, d_ref, wo_ref, g_ref, wu_ref, wd_ref, gf_ref, o_ref,
                    *, final_norm, tf):
    mixed = jnp.concatenate([a_ref[...], b_ref[...], c_ref[...], d_ref[...]], axis=1)
    y = x_ref[...] + _dot(mixed, wo_ref[...])
    h = _rms(y, g_ref[...]).astype(_MXU)
    for c in range(wu_ref.shape[1] // tf):
        u = jnp.square(jnp.maximum(_dot(h, wu_ref[:, c * tf:(c + 1) * tf]), 0.0))
        y = y + _dot(u.astype(_MXU), wd_ref[c * tf:(c + 1) * tf, :])
    if final_norm:
        y = _rms(y, gf_ref[...])
    o_ref[...] = y


def _out_mlp(x, groups, w_out, g, w_up, w_down, layer, g_final, *, final_norm, tm=512, tf=2048):
    m, d = x.shape
    f = w_up.shape[2]
    gspec = pl.BlockSpec((tm, GROUP_WIDTH), lambda i: (i, 0))
    row = pl.BlockSpec((1, d), lambda i: (0, 0))

    def resident(rows, cols):
        return pl.BlockSpec((None, rows, cols), lambda i: (layer, 0, 0), pipeline_mode=pl.Buffered(1))

    return pl.pallas_call(
        functools.partial(_out_mlp_kernel, final_norm=final_norm, tf=tf),
        grid=(m // tm,),
        in_specs=[pl.BlockSpec((tm, d), lambda i: (i, 0)), gspec, gspec, gspec, gspec,
                  resident(d, d), row, resident(d, f), resident(f, d), row],
        out_specs=pl.BlockSpec((tm, d), lambda i: (i, 0)),
        out_shape=jax.ShapeDtypeStruct((m, d), jnp.float32),
        compiler_params=_params("parallel"),
        name="out_proj_mlp_residual",
    )(x, *groups, w_out, g, w_up, w_down, g_final)


def _t5_bucket(dist):
    n = jnp.maximum(dist, 0)
    max_exact = N_BUCKETS // 2
    nf = jnp.maximum(n, 1).astype(jnp.float32)
    large = max_exact + (jnp.log(nf / max_exact) / math.log(MAX_DISTANCE / max_exact)
                         * (N_BUCKETS - max_exact)).astype(jnp.int32)
    large = jnp.minimum(large, N_BUCKETS - 1)
    return jnp.where(n < max_exact, n, large)


def _bias_kernel(tab_ref, o_ref, *, row_stride, col_stride, offset, head0):
    nh, tr, tc = o_ref.shape
    for blk in range(tc // LANES):
        rows = lax.broadcasted_iota(jnp.int32, (tr, LANES), 0)
        cols = lax.broadcasted_iota(jnp.int32, (tr, LANES), 1) + (pl.program_id(0) * tc + blk * LANES)
        bucket = _t5_bucket(rows * row_stride + cols * col_stride + offset)
        for h in range(nh):
            row = tab_ref[head0 + h:head0 + h + 1, :]
            row = (row - row[:, N_BUCKETS - 1:N_BUCKETS]) * LOG2E
            o_ref[h, :, blk * LANES:(blk + 1) * LANES] = jnp.take_along_axis(
                jnp.broadcast_to(row, (tr, LANES)), bucket, axis=1, mode="promise_in_bounds")


def _bias_table(table_t, *, n_heads, head0, rows, cols, col_tile, row_stride, col_stride, offset):
    return pl.pallas_call(
        functools.partial(_bias_kernel, row_stride=row_stride, col_stride=col_stride,
                          offset=offset, head0=head0),
        grid=(cols // col_tile,),
        in_specs=[pl.BlockSpec(table_t.shape, lambda i: (0, 0))],
        out_specs=pl.BlockSpec((n_heads, rows, col_tile), lambda i: (0, 0, i)),
        out_shape=jax.ShapeDtypeStruct((n_heads, rows, cols), jnp.float32),
        compiler_params=_params("parallel"),
        name="t5_bias_tiles",
    )(table_t)


def _softmax_init(t):
    return (jnp.full((1, t), NEG_INF, jnp.float32), jnp.zeros((ACC_ROWS, t), jnp.float32))


class _KeyTile(NamedTuple):
    kts: Sequence[Any]
    vts: Sequence[Any]
    biases: Optional[Sequence[Any]] = None
    emasks: Optional[Sequence[Any]] = None
    qmasks: Optional[Sequence[Any]] = None


def _round_robin(lists):
    out = []
    for rank in range(max(map(len, lists), default=0)):
        out.extend(items[rank] for items in lists if rank < len(items))
    return out


def _softmax_jobs(t, jobs):
    built = {}

    def tiles_of(job, g):
        if (job, g) not in built:
            built[job, g] = [make() for make in jobs[job][2][g]]
        return built[job, g]

    def scores_of(job, g, c):
        qts = jobs[job][1]
        row = []
        for tile in tiles_of(job, g):
            s = _dot(tile.kts[c], qts[c])
            if tile.biases is not None:
                s = s + tile.biases[c]
            if tile.emasks is not None:
                s = jnp.where(tile.emasks[c], s, NEG_INF)
            row.append(s.astype(_MXU))
        return row

    def update(state, job, g, c, scores):
        m, acc = state
        m_new = m
        for tile, s in zip(tiles_of(job, g), scores):
            tile_max = jnp.max(s, axis=0, keepdims=True).astype(jnp.float32)
            if tile.qmasks is not None:
                tile_max = jnp.where(tile.qmasks[c], tile_max, NEG_INF)
            m_new = jnp.maximum(m_new, tile_max)
        seen = m_new > 0.5 * NEG_INF
        acc = jnp.exp2(m - m_new) * acc
        for tile, s in zip(tiles_of(job, g), scores):
            ok = seen if tile.qmasks is None else jnp.logical_and(seen, tile.qmasks[c])
            acc = acc + _dot(tile.vts[c], jnp.exp2(s - jnp.where(ok, m_new, BIG).astype(_MXU)))
        return m_new, acc

    units = _round_robin([[(job, g, c) for g in range(len(groups)) for c in range(n)]
                          for job, (n, _, groups) in enumerate(jobs)])
    lookahead = max(LOOKAHEAD, len(jobs))
    states = [[_softmax_init(t) for _ in range(n)] for n, _, _ in jobs]
    pending = {k: scores_of(*units[k]) for k in range(min(lookahead, len(units)))}
    for k, (job, g, c) in enumerate(units):
        if k + lookahead < len(units):
            pending[k + lookahead] = scores_of(*units[k + lookahead])
        states[job][c] = update(states[job][c], job, g, c, pending.pop(k))
    return states


def _pairs(items):
    return [items[p:p + 2] for p in range(0, len(items), 2)]


def _per_step(i, n_steps, body):
    for step in range(n_steps):
        pl.when(i == step)(functools.partial(body, step))


def _softmax_out(state):
    acc = state[1]
    return acc[:HEAD_DIM] / jnp.maximum(acc[HEAD_DIM:HEAD_DIM + 1], TINY)


def _top_k_rows(score, row_f, k):
    sel = jnp.zeros(score.shape, jnp.float32)
    for _ in range(k):
        mx = jnp.max(score, axis=0, keepdims=True)
        idx = jnp.min(jnp.where(score == mx, row_f, float(score.shape[0])), axis=0, keepdims=True)
        pick = row_f == idx
        sel = jnp.where(pick, 1.0, sel)
        score = jnp.where(pick, PICKED, score)
    return sel


def _tile_iotas(t):
    return (lax.broadcasted_iota(jnp.int32, (t, t), 0), lax.broadcasted_iota(jnp.int32, (t, t), 1))


def _key_rows(ref, j, t):
    if isinstance(j, int):
        return ref[j * t:(j + 1) * t, :]
    return ref[pl.ds(pl.multiple_of(j * t, t), t), :]


def _transposed(ref_block):
    return ref_block.astype(jnp.float32).T


def _queries_t(q_ref, width, t):
    qt = _transposed(q_ref[0]).astype(_MXU)
    return [[qt[c * width:(c + 1) * width, r * t:(r + 1) * t] for c in range(GROUP_WIDTH // width)]
            for r in range(Q_TILES)]


def _fill_values_t(vt_ref, v_ref, col0, t):
    n_tiles, rows, _ = vt_ref.shape
    lane_block = (col0 // LANES) * LANES
    for c in range(n_tiles):
        blk = _transposed(v_ref[0, c * t:(c + 1) * t, lane_block:lane_block + LANES])
        vt_ref[c, 0:HEAD_DIM, :] = blk[col0 - lane_block:col0 - lane_block + HEAD_DIM].astype(vt_ref.dtype)
        if rows == ACC_ROWS:
            first = lax.broadcasted_iota(jnp.int32, (rows - HEAD_DIM, t), 0) == 0
            vt_ref[c, HEAD_DIM:rows, :] = jnp.where(first, 1.0, 0.0).astype(vt_ref.dtype)


def _group_specs(t, s, cb_q, cb_k, cb_v):
    return [pl.BlockSpec((1, Q_TILES * t, GROUP_WIDTH), lambda b, i: (b, i, cb_q)),
            pl.BlockSpec((1, s, GROUP_WIDTH), lambda b, i: (b, 0, cb_k)),
            pl.BlockSpec((1, s, GROUP_WIDTH), lambda b, i: (b, 0, cb_v))]


def _group_out(bsz, s, t):
    return dict(out_specs=pl.BlockSpec((1, Q_TILES * t, GROUP_WIDTH), lambda b, i: (b, i, 0)),
                out_shape=jax.ShapeDtypeStruct((bsz, s, GROUP_WIDTH), _MXU))


def _bias_specs(t, head_group):
    spec = pl.BlockSpec((GROUP_HEADS, t, t), lambda b, i: (head_group, 0, 0))
    return [spec, spec]


def _store_heads(o_ref, outs_t):
    tiles = [jnp.concatenate(heads, axis=0) for heads in outs_t]
    o_ref[0] = jnp.concatenate(tiles, axis=1).T.astype(o_ref.dtype)


def _sb_kernel(q_ref, k_ref, v_ref, o_ref, kb_ref, vt_ref):
    t = q_ref.shape[1] // Q_TILES
    i = pl.program_id(1)
    key, qry = _tile_iotas(t)
    strict = key < qry
    later = jnp.where(qry > key, 1.0, 0.0).astype(_MXU)

    @pl.when(i == 0)
    def _():
        for h in _HEADS:
            kb_ref[h] = k_ref[0, :, h * HEAD_DIM:(h + 1) * HEAD_DIM].astype(kb_ref.dtype)
            _fill_values_t(vt_ref.at[h], v_ref, h * HEAD_DIM, t)

    def query_step(step):
        qts = _queries_t(q_ref, HEAD_DIM, t)
        units = _round_robin([[(r, step * Q_TILES + r, pair, h)
                               for pair in _pairs(list(range(step * Q_TILES + r, -1, -1))) for h in _HEADS]
                              for r in range(Q_TILES)])
        zero = (jnp.zeros((HEAD_DIM, t), jnp.float32), jnp.zeros((1, t), jnp.float32))
        carry = [[zero] * GROUP_HEADS for _ in range(Q_TILES)]
        zs, log_keeps, suffixes = {}, {}, {}

        def scores(u):
            r, _, pair, h = units[u]
            zs[u] = [_dot(_key_rows(kb_ref.at[h], j, t), qts[r][h]) for j in pair]

        def keeps(u):
            _, k, pair, _ = units[u]
            log_keeps[u], suffixes[u] = [], []
            for j, z in zip(pair, zs[u]):
                drop = jnp.maximum(jnp.log2(1.0 + jnp.exp2(jnp.minimum(z, SOFTPLUS_CLAMP))), z)
                if j == k:
                    drop = jnp.where(strict, drop, 0.0)
                log_keeps[u].append(drop)
                suffixes[u].append(_dot(later, drop.astype(_MXU)))

        def values(u):
            r, k, pair, h = units[u]
            acc, run = carry[r][h]
            weights = []
            for j, z, drop, suffix in zip(pair, zs.pop(u), log_keeps.pop(u), suffixes.pop(u)):
                a = jnp.exp2(z - drop - suffix + run)
                if j == k:
                    a = jnp.where(strict, a, 0.0)
                weights.append(a.astype(_MXU))
                run = run - (suffix[0:1] + drop[0:1])
            for j, w in zip(pair, weights):
                acc = acc + _dot(vt_ref[h, j], w)
            carry[r][h] = (acc, run)

        stages = (scores, keeps, values)
        for tick in range(len(units) + len(stages) - 1):
            for lag, stage in enumerate(stages):
                if 0 <= tick - lag < len(units):
                    stage(tick - lag)
        _store_heads(o_ref, [[c[0] for c in tile_carry] for tile_carry in carry])

    _per_step(i, k_ref.shape[1] // (Q_TILES * t), query_step)


def _stick_breaking(proj, t=TILE):
    bsz, s, _ = proj.shape
    return pl.pallas_call(
        _sb_kernel,
        grid=(bsz, s // (Q_TILES * t)),
        in_specs=_group_specs(t, s, CB_SB_Q, CB_SB_K, CB_SB_V),
        scratch_shapes=[pltpu.VMEM((GROUP_HEADS, s, HEAD_DIM), _MXU),
                        pltpu.VMEM((GROUP_HEADS, s // t, HEAD_DIM, t), _MXU)],
        compiler_params=_params("parallel", "arbitrary"),
        name="stick_breaking",
        **_group_out(bsz, s, t),
    )(proj, proj, proj)


def _moba_kernel(q_ref, k_ref, v_ref, bd_ref, bs_ref, o_ref, kb_ref, vt_ref, km_ref):
    t = q_ref.shape[1] // Q_TILES
    n_blk = k_ref.shape[1] // MOBA_BLOCK
    tiles_per_blk = MOBA_BLOCK // t
    blk_shift = int(math.log2(tiles_per_blk))
    i = pl.program_id(1)

    @pl.when(i == 0)
    def _():
        km_ref[...] = jnp.zeros_like(km_ref)
        for h in _HEADS:
            lo, hi = h * HEAD_DIM, (h + 1) * HEAD_DIM
            kb_ref[h] = k_ref[0, :, lo:hi].astype(kb_ref.dtype)
            _fill_values_t(vt_ref.at[h], v_ref, lo, t)
            for n in range(n_blk):
                blk = k_ref[0, n * MOBA_BLOCK:(n + 1) * MOBA_BLOCK, lo:hi]
                km_ref[h, n:n + 1, :] = jnp.mean(blk.astype(jnp.float32), axis=0, keepdims=True)

    def query_step(step):
        key, qry = _tile_iotas(t)
        causal = key <= qry
        blk_row = lax.broadcasted_iota(jnp.int32, (km_ref.shape[1], t), 0)
        qt = _transposed(q_ref[0])
        jobs = []
        for r in range(Q_TILES):
            k = step * Q_TILES + r
            own = k >> blk_shift
            qts, sels = [], []
            for h in _HEADS:
                qf = qt[h * HEAD_DIM:(h + 1) * HEAD_DIM, r * t:(r + 1) * t]
                qts.append(qf.astype(_MXU))
                gate = _dot(km_ref[h], qf, precision=lax.Precision.HIGHEST)
                gate = jnp.where(blk_row < own, gate, NEG_INF)
                sel = _top_k_rows(gate, blk_row.astype(jnp.float32), min(MOBA_TOPK, n_blk - 1))
                sels.append(jnp.where(blk_row < own, sel, 0.0))

            def tile(own, sels, j, bias_ref=None, emask=None):
                n = j >> blk_shift
                return _KeyTile([_key_rows(kb_ref.at[h], j, t) for h in _HEADS], [vt_ref[h, j] for h in _HEADS],
                                None if bias_ref is None else [bias_ref[h] for h in _HEADS],
                                None if emask is None else [emask] * GROUP_HEADS,
                                None if n == own else [sels[h][n:n + 1] > 0.5 for h in _HEADS])

            tile = functools.partial(tile, own, sels)
            groups = _pairs([functools.partial(tile, j) for j in range(k - 1)])
            groups.append(([functools.partial(tile, k - 1, bs_ref)] if k else [])
                          + [functools.partial(tile, k, bd_ref, causal)])
            jobs.append((GROUP_HEADS, qts, groups))
        _store_heads(o_ref, [[_softmax_out(st) for st in states] for states in _softmax_jobs(t, jobs)])

    _per_step(i, k_ref.shape[1] // (Q_TILES * t), query_step)


def _kv_scratch(s, t, key_dim=HEAD_DIM, n_keys=GROUP_HEADS):
    return [pltpu.VMEM((n_keys, s, key_dim), _MXU), pltpu.VMEM((GROUP_HEADS, s // t, ACC_ROWS, t), _MXU)]


def _moba(proj, bias_diag, bias_sub, t=TILE):
    bsz, s, _ = proj.shape
    n_blk_pad = -(-(s // MOBA_BLOCK) // SUBLANES) * SUBLANES
    return pl.pallas_call(
        _moba_kernel,
        grid=(bsz, s // (Q_TILES * t)),
        in_specs=_group_specs(t, s, CB_MB_Q, CB_MB_K, CB_MB_V) + _bias_specs(t, 0),
        scratch_shapes=_kv_scratch(s, t) + [pltpu.VMEM((GROUP_HEADS, n_blk_pad, HEAD_DIM), jnp.float32)],
        compiler_params=_params("parallel", "arbitrary"),
        name="moba",
        **_group_out(bsz, s, t),
    )(proj, proj, proj, bias_diag, bias_sub)


def _diff_kernel(lam_ref, g_ref, q_ref, k_ref, v_ref, bd_ref, bs_ref, o_ref, kb_ref, vt_ref, *, lambda_init):
    t = q_ref.shape[1] // Q_TILES
    i = pl.program_id(1)
    key, qry = _tile_iotas(t)
    causal = key <= qry
    lv = lam_ref[...]
    lam = (jnp.exp(jnp.sum(lv[0:1] * lv[1:2], keepdims=True))
           - jnp.exp(jnp.sum(lv[2:3] * lv[3:4], keepdims=True)) + lambda_init)
    halves = range(2 * GROUP_HEADS)

    @pl.when(i == 0)
    def _():
        for c in halves:
            kb_ref[c] = k_ref[0, :, c * DIFF_HALF:(c + 1) * DIFF_HALF].astype(kb_ref.dtype)
        for h in _HEADS:
            _fill_values_t(vt_ref.at[h], v_ref, h * HEAD_DIM, t)

    def tile(j, bias_ref=None, emask=None):
        n = len(halves)
        return _KeyTile([_key_rows(kb_ref.at[c], j, t) for c in halves], [vt_ref[c // 2, j] for c in halves],
                        None if bias_ref is None else [bias_ref[c // 2] for c in halves],
                        None if emask is None else [emask] * n)

    def query_step(step):
        qts = _queries_t(q_ref, DIFF_HALF, t)
        jobs = []
        for r in range(Q_TILES):
            k = step * Q_TILES + r
            groups = _pairs([functools.partial(tile, j) for j in range(k - 1)])
            groups.append(([functools.partial(tile, k - 1, bs_ref)] if k else [])
                          + [functools.partial(tile, k, bd_ref, causal)])
            jobs.append((len(halves), qts[r], groups))
        outs = []
        for states in _softmax_jobs(t, jobs):
            heads = []
            for h in _HEADS:
                o = _softmax_out(states[2 * h]) - lam * _softmax_out(states[2 * h + 1])
                o = o * lax.rsqrt(jnp.mean(o * o, axis=0, keepdims=True) + NORM_EPS) * g_ref[...]
                heads.append(o * (1.0 - lambda_init))
            outs.append(heads)
        _store_heads(o_ref, outs)

    _per_step(i, k_ref.shape[1] // (Q_TILES * t), query_step)


def _diff(proj, lam_params, subln, bias_diag, bias_sub, lambda_init, t=TILE):
    bsz, s, _ = proj.shape
    return pl.pallas_call(
        functools.partial(_diff_kernel, lambda_init=lambda_init),
        grid=(bsz, s // (Q_TILES * t)),
        in_specs=[pl.BlockSpec(lam_params.shape, lambda b, i: (0, 0)),
                  pl.BlockSpec(subln.shape, lambda b, i: (0, 0))]
        + _group_specs(t, s, CB_DF_Q, CB_DF_K, CB_DF_V) + _bias_specs(t, 2),
        scratch_shapes=_kv_scratch(s, t, key_dim=DIFF_HALF, n_keys=2 * GROUP_HEADS),
        compiler_params=_params("parallel", "arbitrary"),
        name="diff_attention",
        **_group_out(bsz, s, t),
    )(lam_params, subln, proj, proj, proj, bias_diag, bias_sub)


def _compress_kernel(kcv_ref, pk_ref, pv_ref, wk1_ref, wk2_ref, wv1_ref, wv2t_ref, kc_ref, vct_ref):
    n_chunk = kc_ref.shape[1]

    branches = ((0, pk_ref, wk1_ref), (HEAD_DIM, pv_ref, wv1_ref))
    tops = [jnp.zeros((n_chunk, w1_ref.shape[1]), jnp.float32) for _, _, w1_ref in branches]
    bots = list(tops)
    for l in range(CMP_STRIDE):
        tokens = kcv_ref[0, pl.ds(l, n_chunk, stride=CMP_STRIDE), :]
        l2 = CMP_STRIDE + l
        for n, (col0, p_ref, w1_ref) in enumerate(branches):
            x = tokens[:, col0:col0 + HEAD_DIM]
            tops[n] = tops[n] + _dot((x + p_ref[l:l + 1, :]).astype(_MXU),
                                     w1_ref[l * HEAD_DIM:(l + 1) * HEAD_DIM, :])
            bots[n] = bots[n] + _dot((x + p_ref[l2:l2 + 1, :]).astype(_MXU),
                                     w1_ref[l2 * HEAD_DIM:(l2 + 1) * HEAD_DIM, :])
    hidden = [jax.nn.gelu(top + pltpu.roll(bot, n_chunk - 1, axis=0)).astype(_MXU) for top, bot in zip(tops, bots)]
    kc_ref[0] = _dot(hidden[0], wk2_ref[...])
    vct_ref[0] = _dot_nt(wv2t_ref[...], hidden[1])


def _compress(kcv, pos_k, pos_v, wk1, wk2, wv1, wv2t, layer):
    bsz, s, width = kcv.shape
    n_chunk = s // CMP_STRIDE

    def full(a):
        return pl.BlockSpec((None,) + a.shape[1:], lambda b: (layer, 0, 0))

    return pl.pallas_call(
        _compress_kernel,
        grid=(bsz,),
        in_specs=[pl.BlockSpec((1, s, width), lambda b: (b, 0, 0)),
                  full(pos_k), full(pos_v), full(wk1), full(wk2), full(wv1), full(wv2t)],
        out_specs=[pl.BlockSpec((1, n_chunk, HEAD_DIM), lambda b: (b, 0, 0)),
                   pl.BlockSpec((1, HEAD_DIM, n_chunk), lambda b: (b, 0, 0))],
        out_shape=[jax.ShapeDtypeStruct((bsz, n_chunk, HEAD_DIM), jnp.float32),
                   jax.ShapeDtypeStruct((bsz, HEAD_DIM, n_chunk), jnp.float32)],
        compiler_params=_params("parallel"),
        name="nsa_compress",
    )(kcv, pos_k, pos_v, wk1, wk2, wv1, wv2t)


def _nsa_kernel(q_ref, kva_ref, kvb_ref, gq_ref, kc_ref, vct_ref, bd_ref, bs_ref, bc_ref, cover_ref, e_ref,
                o_ref, ks_ref, vst_ref, kw_ref, vwt_ref):
    t = q_ref.shape[1] // Q_TILES
    i = pl.program_id(1)
    key, qry = _tile_iotas(t)
    causal = key <= qry
    ks_col, vs_col, kw_col, vw_col, gate_col = 2 * HEAD_DIM, 3 * HEAD_DIM, 0, HEAD_DIM, 2 * HEAD_DIM

    @pl.when(i == 0)
    def _():
        ks_ref[:, :HEAD_DIM] = kva_ref[0, :, ks_col:ks_col + HEAD_DIM].astype(ks_ref.dtype)
        ks_ref[:, HEAD_DIM:] = e_ref[...]
        kw_ref[...] = kvb_ref[0, :, kw_col:kw_col + HEAD_DIM].astype(kw_ref.dtype)
        _fill_values_t(vst_ref, kva_ref, vs_col, t)
        _fill_values_t(vwt_ref, kvb_ref, vw_col, t)

    def tile(k_ref, vt_ref, j, bias_ref=None, emask=None):
        n = GROUP_HEADS
        return _KeyTile([_key_rows(k_ref, j, t)] * n, [vt_ref[j]] * n,
                        None if bias_ref is None else [bias_ref[h] for h in _HEADS],
                        None if emask is None else [emask] * n)

    def query_step(step):
        qts = _queries_t(q_ref, HEAD_DIM, t)
        n_cmp = kc_ref.shape[1]
        n_slc = cover_ref.shape[0]
        kc = kc_ref[0].astype(_MXU)
        vct = vct_ref[0].astype(_MXU)
        c_row = lax.broadcasted_iota(jnp.int32, (n_cmp, t), 0)
        c_col = lax.broadcasted_iota(jnp.int32, (n_cmp, t), 1)
        s_row = lax.broadcasted_iota(jnp.int32, (n_slc, t), 0)
        s_col = lax.broadcasted_iota(jnp.int32, (n_slc, t), 1)
        tiles = [step * Q_TILES + r for r in range(Q_TILES)]

        o_cmp, importance = [], []
        for r, k in enumerate(tiles):
            visible = c_col + k * t >= c_row * CMP_STRIDE + (CMP_LEN - 1)
            cmp_scores = [_dot(kc, qts[r][h]) for h in _HEADS]
            cmp_probs = []
            p_sum = jnp.zeros((n_cmp, t), jnp.float32)
            for h in _HEADS:
                sc = jnp.where(visible, cmp_scores[h] + bc_ref[h, :, r * t:(r + 1) * t], NEG_INF)
                e = jnp.where(visible, jnp.exp2(sc - jnp.max(sc, axis=0, keepdims=True)), 0.0)
                p = e / jnp.maximum(jnp.sum(e, axis=0, keepdims=True), TINY)
                cmp_probs.append(p.astype(_MXU))
                p_sum = p_sum + p
            o_cmp.append([_dot(vct, cmp_probs[h]) for h in _HEADS])
            importance.append(_dot(cover_ref[...], p_sum, precision=lax.Precision.HIGHEST))

        n_back = WINDOW // t
        jobs = []
        for r, k in enumerate(tiles):
            window = [functools.partial(tile, kw_ref, vwt_ref, k - n_back, None, qry < key)] if k >= n_back else []
            for back in range(min(n_back - 1, k), 0, -1):
                window.append(functools.partial(tile, kw_ref, vwt_ref, k - back, bs_ref if back == 1 else None))
            window.append(functools.partial(tile, kw_ref, vwt_ref, k, bd_ref, causal))
            jobs.append((GROUP_HEADS, qts[r], [window]))
        o_win = [[_softmax_out(st) for st in states] for states in _softmax_jobs(t, jobs)]

        jobs = []
        for r, k in enumerate(tiles):
            own = jnp.right_shift(s_col + k * t, int(math.log2(SLC_LEN)))
            score = jnp.where(s_row == own, FORCE, jnp.where(s_row < own, importance[r], NEG_INF))
            sel = _top_k_rows(score, s_row.astype(jnp.float32), min(SLC_TOPN, n_slc))
            penalty = jnp.where(sel > 0.5, 0.0, NEG_INF).astype(_MXU)
            q_aug = [jnp.concatenate([qts[r][h], penalty], axis=0) for h in _HEADS]
            groups = _pairs([functools.partial(tile, ks_ref, vst_ref, j) for j in range(k - 1)])
            groups.append(([functools.partial(tile, ks_ref, vst_ref, k - 1, bs_ref)] if k else [])
                          + [functools.partial(tile, ks_ref, vst_ref, k, bd_ref, causal)])
            jobs.append((GROUP_HEADS, q_aug, groups))
        o_slc = [[_softmax_out(st) for st in states] for states in _softmax_jobs(t, jobs)]

        gates = _transposed(gq_ref[0, :, (gate_col // LANES) * LANES:(gate_col // LANES + 1) * LANES])
        gates = 1.0 / (1.0 + jnp.exp(-gates[gate_col % LANES:gate_col % LANES + N_GATES + 4]))
        outs = []
        for r in range(Q_TILES):
            heads = []
            for h in _HEADS:
                g = [gates[br * GROUP_HEADS + h:br * GROUP_HEADS + h + 1, r * t:(r + 1) * t] for br in range(3)]
                heads.append(g[0] * o_cmp[r][h] + g[1] * o_slc[r][h] + g[2] * o_win[r][h])
            outs.append(heads)
        _store_heads(o_ref, outs)

    _per_step(i, kva_ref.shape[1] // (Q_TILES * t), query_step)


def _nsa(proj, kc, vc_t, bias_diag, bias_sub, bias_cmp, cover_t, expand, t=TILE):
    bsz, s, _ = proj.shape
    n_cmp = kc.shape[1]
    values_t = pltpu.VMEM((s // t, ACC_ROWS, t), _MXU)
    return pl.pallas_call(
        _nsa_kernel,
        grid=(bsz, s // (Q_TILES * t)),
        in_specs=_group_specs(t, s, CB_NS_Q, CB_NS_A, CB_NS_B)
        + [pl.BlockSpec((1, Q_TILES * t, GROUP_WIDTH), lambda b, i: (b, i, CB_NS_B)),
           pl.BlockSpec((1, n_cmp, HEAD_DIM), lambda b, i: (b, 0, 0)),
           pl.BlockSpec((1, HEAD_DIM, n_cmp), lambda b, i: (b, 0, 0))]
        + _bias_specs(t, 1)
        + [pl.BlockSpec((GROUP_HEADS, n_cmp, Q_TILES * t), lambda b, i: (0, 0, i)),
           pl.BlockSpec(cover_t.shape, lambda b, i: (0, 0)),
           pl.BlockSpec(expand.shape, lambda b, i: (0, 0))],
        scratch_shapes=[pltpu.VMEM((s, HEAD_DIM + expand.shape[1]), _MXU), values_t,
                        pltpu.VMEM((s, HEAD_DIM), _MXU), values_t],
        compiler_params=_params("parallel", "arbitrary"),
        name="nsa",
        **_group_out(bsz, s, t),
    )(proj, proj, proj, proj, kc, vc_t, bias_diag, bias_sub, bias_cmp, cover_t, expand)


def _nsa_constants(s):
    n_cmp = (s - CMP_LEN) // CMP_STRIDE + 1
    n_slc = s // SLC_LEN
    assert n_cmp + 1 == s // CMP_STRIDE and n_slc % SUBLANES == 0
    c_start = np.arange(n_cmp) * CMP_STRIDE
    s_start = np.arange(n_slc) * SLC_LEN
    cover = np.clip(np.minimum((c_start + CMP_LEN - 1)[:, None], (s_start + SLC_LEN - 1)[None, :])
                    - np.maximum(c_start[:, None], s_start[None, :]) + 1, 0, None) / CMP_LEN
    cover_t = np.zeros((n_slc, n_cmp + 1), np.float32)
    cover_t[:, :n_cmp] = cover.T
    expand = (np.arange(s)[:, None] // SLC_LEN == np.arange(n_slc)[None, :]).astype(np.float32)
    return jnp.asarray(cover_t), jnp.asarray(expand, _MXU)


def kernel(x, w_in, w_out, w_up, w_down, norm_attn, norm_mlp, cmp_pos_k, cmp_pos_v, cmp_k_w1, cmp_k_w2,
           cmp_v_w1, cmp_v_w2, diff_lambda, diff_subln, rel_bias, final_norm):
    bsz, s, d = x.shape
    depth = w_in.shape[0]
    t = TILE
    n_chunk = s // CMP_STRIDE
    assert s % MOBA_BLOCK == 0 and MOBA_BLOCK % t == 0 and WINDOW % t == 0 and t >= MAX_DISTANCE

    assert w_in.shape[2] + PAD_COLS == D_IN_PAD
    w_in_c = jnp.pad(w_in.astype(_MXU), ((0, 0), (0, 0), (0, PAD_COLS)))
    w_out_c, w_up_c, w_down_c = (w.astype(_MXU) for w in (w_out, w_up, w_down))
    wk1, wk2 = cmp_k_w1.astype(_MXU), cmp_k_w2.astype(_MXU)
    wv1, wv2t = cmp_v_w1.astype(_MXU), jnp.swapaxes(cmp_v_w2, 1, 2).astype(_MXU)

    table_t = jnp.pad(rel_bias.T, ((0, 0), (0, LANES - N_BUCKETS)))
    tiles = dict(n_heads=rel_bias.shape[1], head0=0, rows=t, cols=t, col_tile=t, row_stride=-1, col_stride=1)
    bias_diag = _bias_table(table_t, offset=0, **tiles)
    bias_sub = _bias_table(table_t, offset=t, **tiles)
    bias_cmp = _bias_table(table_t, n_heads=GROUP_HEADS, head0=GROUP_HEADS, rows=n_chunk, cols=s,
                           col_tile=t, row_stride=-CMP_STRIDE, col_stride=1, offset=-(CMP_LEN - 1))
    cover_t, expand = _nsa_constants(s)
    col_scale = np.ones((1, D_IN_PAD), np.float32)
    for cb, width in ((CB_SB_Q, HEAD_DIM), (CB_MB_Q, HEAD_DIM), (CB_NS_Q, HEAD_DIM), (CB_DF_Q, DIFF_HALF)):
        col_scale[:, cb * GROUP_WIDTH:(cb + 1) * GROUP_WIDTH] = width ** -0.5 * LOG2E
    col_scale = jnp.asarray(col_scale)

    x2 = x.reshape(bsz * s, d)
    for layer in range(depth):
        proj, kcv = _norm_matmul(x2, norm_attn[layer][None], w_in_c, layer, col_scale,
                                 gap=(COLS_BEFORE_PAD, COLS_BEFORE_PAD + PAD_COLS),
                                 side_col=CB_NS_A * GROUP_WIDTH, side_width=2 * HEAD_DIM)
        proj = proj.reshape(bsz, s, D_IN_PAD)
        o_sb = _stick_breaking(proj)
        o_mb = _moba(proj, bias_diag, bias_sub)
        kc, vc_t = _compress(kcv.reshape(bsz, s, 2 * HEAD_DIM), cmp_pos_k, cmp_pos_v, wk1, wk2, wv1, wv2t, layer)
        o_ns = _nsa(proj, kc, vc_t, bias_diag, bias_sub, bias_cmp, cover_t, expand)
        lambda_init = 0.8 - 0.6 * math.exp(-0.3 * layer)
        o_df = _diff(proj, diff_lambda[layer], diff_subln[layer][:, None], bias_diag, bias_sub, lambda_init)
        groups = [o.reshape(bsz * s, GROUP_WIDTH) for o in (o_sb, o_mb, o_ns, o_df)]
        x2 = _out_mlp(x2, groups, w_out_c, norm_mlp[layer][None], w_up_c, w_down_c, layer,
                      final_norm[None], final_norm=(layer == depth - 1))
    return x2.reshape(bsz, s, d)
```

```python
import functools
import math
from typing import Any, NamedTuple, Optional, Sequence

import numpy as np
import jax
import jax.numpy as jnp
from jax import lax
from jax.experimental import pallas as pl
from jax.experimental.pallas import tpu as pltpu

HEAD_DIM = 64
GROUP_HEADS = 4
GROUP_WIDTH = GROUP_HEADS * HEAD_DIM
NORM_EPS = 1e-6
NEG_INF = -1e30
BIG = 1e30
FORCE = 1e30
TINY = 1e-30
SOFTPLUS_CLAMP = 64.0
PICKED = -3e38
LOG2E = math.log2(math.e)
N_BUCKETS = 32
MAX_DISTANCE = 128
MOBA_BLOCK = 256
MOBA_TOPK = 3
CMP_LEN = 32
CMP_STRIDE = 16
SLC_LEN = 64
SLC_TOPN = 4
WINDOW = 512
DIFF_HALF = HEAD_DIM // 2
LANES = 128
SUBLANES = 8
TILE = 256
BF16_ROWS = 16
ACC_ROWS = HEAD_DIM + BF16_ROWS
N_GATES = 3 * GROUP_HEADS
COLS_BEFORE_PAD = 9 * GROUP_WIDTH - 2 * HEAD_DIM + N_GATES
PAD_COLS = 2 * HEAD_DIM - N_GATES
CB_SB_Q, CB_SB_K, CB_SB_V, CB_MB_Q, CB_MB_K, CB_MB_V, CB_NS_Q, CB_NS_A, CB_NS_B, CB_DF_Q, CB_DF_K, CB_DF_V = range(12)
D_IN_PAD = 12 * GROUP_WIDTH

_MXU = jnp.bfloat16
_VMEM_LIMIT = 56 * 1024 * 1024
_HEADS = range(GROUP_HEADS)
Q_TILES = 8
LOOKAHEAD = 2


def _dot(a, b, precision=None):
    return jnp.dot(a, b, precision=precision, preferred_element_type=jnp.float32)


def _dot_nt(a, b, precision=None):
    return lax.dot_general(a, b, (((1,), (1,)), ((), ())), precision=precision,
                           preferred_element_type=jnp.float32)


def _rms(x, g):
    return x * lax.rsqrt(jnp.mean(x * x, axis=-1, keepdims=True) + NORM_EPS) * g


def _params(*sem):
    return pltpu.CompilerParams(dimension_semantics=sem, vmem_limit_bytes=_VMEM_LIMIT)


def _norm_matmul_kernel(x_ref, g_ref, w_ref, scale_ref, o_ref, side_ref, wp_ref, *, tn, side_col, gap):
    @pl.when(pl.program_id(0) == 0)
    def _():
        lo, hi = gap
        wp_ref[:, :lo] = w_ref[:, :lo]
        wp_ref[:, lo:hi] = jnp.zeros((wp_ref.shape[0], hi - lo), wp_ref.dtype)
        wp_ref[:, hi:] = w_ref[:, lo:lo + wp_ref.shape[1] - hi]

    h = _rms(x_ref[...], g_ref[...]).astype(_MXU)
    for j in range(wp_ref.shape[1] // tn):
        cols = slice(j * tn, (j + 1) * tn)
        acc = _dot(h, wp_ref[:, cols])
        o_ref[:, cols] = (acc * scale_ref[:, cols]).astype(o_ref.dtype)
        if j * tn <= side_col < (j + 1) * tn:
            side_ref[...] = acc[:, side_col - j * tn:side_col - j * tn + side_ref.shape[1]]


def _norm_matmul(x, g, w, layer, col_scale, *, gap, side_col, side_width, tm=512, tn=1024):
    m, d = x.shape
    n = w.shape[2]
    assert side_col % LANES == 0 and side_col // tn == (side_col + side_width - 1) // tn
    return pl.pallas_call(
        functools.partial(_norm_matmul_kernel, tn=tn, side_col=side_col, gap=gap),
        grid=(m // tm,),
        in_specs=[pl.BlockSpec((tm, d), lambda i: (i, 0)),
                  pl.BlockSpec((1, d), lambda i: (0, 0)),
                  pl.BlockSpec((None, d, w.shape[2]), lambda i: (layer, 0, 0), pipeline_mode=pl.Buffered(1)),
                  pl.BlockSpec((1, n), lambda i: (0, 0))],
        out_specs=[pl.BlockSpec((tm, n), lambda i: (i, 0)),
                   pl.BlockSpec((tm, side_width), lambda i: (i, 0))],
        out_shape=[jax.ShapeDtypeStruct((m, n), _MXU),
                   jax.ShapeDtypeStruct((m, side_width), jnp.float32)],
        scratch_shapes=[pltpu.VMEM((d, n), _MXU)],
        compiler_params=_params("arbitrary"),
        name="norm_in_proj",
    )(x, g, w, col_scale)


def _out_mlp_kernel(x_ref, a_ref, b_ref, c_ref, d_ref, wo_ref, g_ref, wu_ref, wd_ref, gf_ref, o_ref,
                    *, final_norm, tf):
    mixed = jnp.concatenate([a_ref[...], b_ref[...], c_ref[...], d_ref[...]], axis=1)
    y = x_ref[...] + _dot(mixed, wo_ref[...])
    h = _rms(y, g_ref[...]).astype(_MXU)
    for c in range(wu_ref.shape[1] // tf):
        u = jnp.square(jnp.maximum(_dot(h, wu_ref[:, c * tf:(c + 1) * tf]), 0.0))
        y = y + _dot(u.astype(_MXU), wd_ref[c * tf:(c + 1) * tf, :])
    if final_norm:
        y = _rms(y, gf_ref[...])
    o_ref[...] = y


def _out_mlp(x, groups, w_out, g, w_up, w_down, layer, g_final, *, final_norm, tm=512, tf=2048):
    m, d = x.shape
    f = w_up.shape[2]
    gspec = pl.BlockSpec((tm, GROUP_WIDTH), lambda i: (i, 0))
    row = pl.BlockSpec((1, d), lambda i: (0, 0))

    def resident(rows, cols):
        return pl.BlockSpec((None, rows, cols), lambda i: (layer, 0, 0), pipeline_mode=pl.Buffered(1))

    return pl.pallas_call(
        functools.partial(_out_mlp_kernel, final_norm=final_norm, tf=tf),
        grid=(m // tm,),
        in_specs=[pl.BlockSpec((tm, d), lambda i: (i, 0)), gspec, gspec, gspec, gspec,
                  resident(d, d), row, resident(d, f), resident(f, d), row],
        out_specs=pl.BlockSpec((tm, d), lambda i: (i, 0)),
        out_shape=jax.ShapeDtypeStruct((m, d), jnp.float32),
        compiler_params=_params("parallel"),
        name="out_proj_mlp_residual",
    )(x, *groups, w_out, g, w_up, w_down, g_final)


def _t5_bucket(dist):
    n = jnp.maximum(dist, 0)
    max_exact = N_BUCKETS // 2
    nf = jnp.maximum(n, 1).astype(jnp.float32)
    large = max_exact + (jnp.log(nf / max_exact) / math.log(MAX_DISTANCE / max_exact)
                         * (N_BUCKETS - max_exact)).astype(jnp.int32)
    large = jnp.minimum(large, N_BUCKETS - 1)
    return jnp.where(n < max_exact, n, large)


def _bias_kernel(tab_ref, o_ref, *, row_stride, col_stride, offset, head0):
    nh, tr, tc = o_ref.shape
    for blk in range(tc // LANES):
        rows = lax.broadcasted_iota(jnp.int32, (tr, LANES), 0)
        cols = lax.broadcasted_iota(jnp.int32, (tr, LANES), 1) + (pl.program_id(0) * tc + blk * LANES)
        bucket = _t5_bucket(rows * row_stride + cols * col_stride + offset)
        for h in range(nh):
            row = tab_ref[head0 + h:head0 + h + 1, :]
            row = (row - row[:, N_BUCKETS - 1:N_BUCKETS]) * LOG2E
            o_ref[h, :, blk * LANES:(blk + 1) * LANES] = jnp.take_along_axis(
                jnp.broadcast_to(row, (tr, LANES)), bucket, axis=1, mode="promise_in_bounds")


def _bias_table(table_t, *, n_heads, head0, rows, cols, col_tile, row_stride, col_stride, offset):
    return pl.pallas_call(
        functools.partial(_bias_kernel, row_stride=row_stride, col_stride=col_stride,
                          offset=offset, head0=head0),
        grid=(cols // col_tile,),
        in_specs=[pl.BlockSpec(table_t.shape, lambda i: (0, 0))],
        out_specs=pl.BlockSpec((n_heads, rows, col_tile), lambda i: (0, 0, i)),
        out_shape=jax.ShapeDtypeStruct((n_heads, rows, cols), jnp.float32),
        compiler_params=_params("parallel"),
        name="t5_bias_tiles",
    )(table_t)


def _softmax_init(t):
    return (jnp.full((1, t), NEG_INF, jnp.float32), jnp.zeros((ACC_ROWS, t), jnp.float32))


class _KeyTile(NamedTuple):
    kts: Sequence[Any]
    vts: Sequence[Any]
    biases: Optional[Sequence[Any]] = None
    emasks: Optional[Sequence[Any]] = None
    qmasks: Optional[Sequence[Any]] = None


def _round_robin(lists):
    out = []
    for rank in range(max(map(len, lists), default=0)):
        out.extend(items[rank] for items in lists if rank < len(items))
    return out


def _softmax_jobs(t, jobs):
    built = {}

    def tiles_of(job, g):
        if (job, g) not in built:
            built[job, g] = [make() for make in jobs[job][2][g]]
        return built[job, g]

    def scores_of(job, g, c):
        qts = jobs[job][1]
        row = []
        for tile in tiles_of(job, g):
            s = _dot(tile.kts[c], qts[c])
            if tile.biases is not None:
                s = s + tile.biases[c]
            if tile.emasks is not None:
                s = jnp.where(tile.emasks[c], s, NEG_INF)
            row.append(s.astype(_MXU))
        return row

    def update(state, job, g, c, scores):
        m, acc = state
        m_new = m
        for tile, s in zip(tiles_of(job, g), scores):
            tile_max = jnp.max(s, axis=0, keepdims=True).astype(jnp.float32)
            if tile.qmasks is not None:
                tile_max = jnp.where(tile.qmasks[c], tile_max, NEG_INF)
            m_new = jnp.maximum(m_new, tile_max)
        seen = m_new > 0.5 * NEG_INF
        acc = jnp.exp2(m - m_new) * acc
        for tile, s in zip(tiles_of(job, g), scores):
            ok = seen if tile.qmasks is None else jnp.logical_and(seen, tile.qmasks[c])
            acc = acc + _dot(tile.vts[c], jnp.exp2(s - jnp.where(ok, m_new, BIG).astype(_MXU)))
        return m_new, acc

    units = _round_robin([[(job, g, c) for g in range(len(groups)) for c in range(n)]
                          for job, (n, _, groups) in enumerate(jobs)])
    lookahead = max(LOOKAHEAD, len(jobs))
    states = [[_softmax_init(t) for _ in range(n)] for n, _, _ in jobs]
    pending = {k: scores_of(*units[k]) for k in range(min(lookahead, len(units)))}
    for k, (job, g, c) in enumerate(units):
        if k + lookahead < len(units):
            pending[k + lookahead] = scores_of(*units[k + lookahead])
        states[job][c] = update(states[job][c], job, g, c, pending.pop(k))
    return states


def _pairs(items):
    return [items[p:p + 2] for p in range(0, len(items), 2)]


def _per_step(i, n_steps, body):
    if n_steps == 1:
        body(0)
        return
    for step in range(n_steps):
        pl.when(i == step)(functools.partial(body, step))


def _first_step(i, n_steps):
    return (lambda fn: fn()) if n_steps == 1 else pl.when(i == 0)


def _softmax_out(state):
    acc = state[1]
    return acc[:HEAD_DIM] / jnp.maximum(acc[HEAD_DIM:HEAD_DIM + 1], TINY)


def _top_k_rows(score, row_f, k):
    sel = jnp.zeros(score.shape, jnp.float32)
    for _ in range(k):
        mx = jnp.max(score, axis=0, keepdims=True)
        idx = jnp.min(jnp.where(score == mx, row_f, float(score.shape[0])), axis=0, keepdims=True)
        pick = row_f == idx
        sel = jnp.where(pick, 1.0, sel)
        score = jnp.where(pick, PICKED, score)
    return sel


def _tile_iotas(t):
    return (lax.broadcasted_iota(jnp.int32, (t, t), 0), lax.broadcasted_iota(jnp.int32, (t, t), 1))


def _key_rows(ref, j, t):
    if isinstance(j, int):
        return ref[j * t:(j + 1) * t, :]
    return ref[pl.ds(pl.multiple_of(j * t, t), t), :]


def _transposed(ref_block):
    return ref_block.astype(jnp.float32).T


def _queries_t(q_ref, width, t):
    qt = _transposed(q_ref[0]).astype(_MXU)
    return [[qt[c * width:(c + 1) * width, r * t:(r + 1) * t] for c in range(GROUP_WIDTH // width)]
            for r in range(Q_TILES)]


def _fill_values_t(vt_ref, v_ref, col0, t):
    n_tiles, rows, _ = vt_ref.shape
    lane_block = (col0 // LANES) * LANES
    for c in range(n_tiles):
        blk = _transposed(v_ref[0, c * t:(c + 1) * t, lane_block:lane_block + LANES])
        vt_ref[c, 0:HEAD_DIM, :] = blk[col0 - lane_block:col0 - lane_block + HEAD_DIM].astype(vt_ref.dtype)
        if rows == ACC_ROWS:
            first = lax.broadcasted_iota(jnp.int32, (rows - HEAD_DIM, t), 0) == 0
            vt_ref[c, HEAD_DIM:rows, :] = jnp.where(first, 1.0, 0.0).astype(vt_ref.dtype)


def _group_specs(t, s, cb_q, cb_k, cb_v):
    return [pl.BlockSpec((1, Q_TILES * t, GROUP_WIDTH), lambda b, i: (b, i, cb_q)),
            pl.BlockSpec((1, s, GROUP_WIDTH), lambda b, i: (b, 0, cb_k)),
            pl.BlockSpec((1, s, GROUP_WIDTH), lambda b, i: (b, 0, cb_v))]


def _group_out(bsz, s, t):
    return dict(out_specs=pl.BlockSpec((1, Q_TILES * t, GROUP_WIDTH), lambda b, i: (b, i, 0)),
                out_shape=jax.ShapeDtypeStruct((bsz, s, GROUP_WIDTH), _MXU))


def _bias_specs(t, head_group):
    spec = pl.BlockSpec((GROUP_HEADS, t, t), lambda b, i: (head_group, 0, 0))
    return [spec, spec]


def _store_heads(o_ref, outs_t):
    tiles = [jnp.concatenate(heads, axis=0) for heads in outs_t]
    o_ref[0] = jnp.concatenate(tiles, axis=1).T.astype(o_ref.dtype)


def _sb_kernel(q_ref, k_ref, v_ref, o_ref, kb_ref, vt_ref):
    t = q_ref.shape[1] // Q_TILES
    i = pl.program_id(1)
    key, qry = _tile_iotas(t)
    strict = key < qry
    later = jnp.where(qry > key, 1.0, 0.0).astype(_MXU)
    n_steps = k_ref.shape[1] // (Q_TILES * t)

    @_first_step(i, n_steps)
    def _():
        for h in _HEADS:
            kb_ref[h] = k_ref[0, :, h * HEAD_DIM:(h + 1) * HEAD_DIM].astype(kb_ref.dtype)
            _fill_values_t(vt_ref.at[h], v_ref, h * HEAD_DIM, t)

    def query_step(step):
        qts = _queries_t(q_ref, HEAD_DIM, t)
        units = _round_robin([[(r, step * Q_TILES + r, pair, h)
                               for pair in _pairs(list(range(step * Q_TILES + r, -1, -1))) for h in _HEADS]
                              for r in range(Q_TILES)])
        zero = (jnp.zeros((HEAD_DIM, t), jnp.float32), jnp.zeros((1, t), jnp.float32))
        carry = [[zero] * GROUP_HEADS for _ in range(Q_TILES)]
        zs, log_keeps, suffixes = {}, {}, {}

        def scores(u):
            r, _, pair, h = units[u]
            zs[u] = [_dot(_key_rows(kb_ref.at[h], j, t), qts[r][h]) for j in pair]

        def keeps(u):
            _, k, pair, _ = units[u]
            log_keeps[u], suffixes[u] = [], []
            for j, z in zip(pair, zs[u]):
                drop = jnp.maximum(jnp.log2(1.0 + jnp.exp2(jnp.minimum(z, SOFTPLUS_CLAMP))), z)
                if j == k:
                    drop = jnp.where(strict, drop, 0.0)
                log_keeps[u].append(drop)
                suffixes[u].append(_dot(later, drop.astype(_MXU)))

        def values(u):
            r, k, pair, h = units[u]
            acc, run = carry[r][h]
            weights = []
            for j, z, drop, suffix in zip(pair, zs.pop(u), log_keeps.pop(u), suffixes.pop(u)):
                a = jnp.exp2(z - drop - suffix + run)
                if j == k:
                    a = jnp.where(strict, a, 0.0)
                weights.append(a.astype(_MXU))
                run = run - (suffix[0:1] + drop[0:1])
            for j, w in zip(pair, weights):
                acc = acc + _dot(vt_ref[h, j], w)
            carry[r][h] = (acc, run)

        stages = (scores, keeps, values)
        for tick in range(len(units) + len(stages) - 1):
            for lag, stage in enumerate(stages):
                if 0 <= tick - lag < len(units):
                    stage(tick - lag)
        _store_heads(o_ref, [[c[0] for c in tile_carry] for tile_carry in carry])

    _per_step(i, n_steps, query_step)


def _stick_breaking(proj, t=TILE):
    bsz, s, _ = proj.shape
    return pl.pallas_call(
        _sb_kernel,
        grid=(bsz, s // (Q_TILES * t)),
        in_specs=_group_specs(t, s, CB_SB_Q, CB_SB_K, CB_SB_V),
        scratch_shapes=[pltpu.VMEM((GROUP_HEADS, s, HEAD_DIM), _MXU),
                        pltpu.VMEM((GROUP_HEADS, s // t, HEAD_DIM, t), _MXU)],
        compiler_params=_params("parallel", "arbitrary"),
        name="stick_breaking",
        **_group_out(bsz, s, t),
    )(proj, proj, proj)


def _moba_kernel(q_ref, k_ref, v_ref, bd_ref, bs_ref, o_ref, kb_ref, vt_ref, km_ref):
    t = q_ref.shape[1] // Q_TILES
    n_blk = k_ref.shape[1] // MOBA_BLOCK
    tiles_per_blk = MOBA_BLOCK // t
    blk_shift = int(math.log2(tiles_per_blk))
    i = pl.program_id(1)
    n_steps = k_ref.shape[1] // (Q_TILES * t)

    @_first_step(i, n_steps)
    def _():
        km_ref[...] = jnp.zeros_like(km_ref)
        for h in _HEADS:
            lo, hi = h * HEAD_DIM, (h + 1) * HEAD_DIM
            kb_ref[h] = k_ref[0, :, lo:hi].astype(kb_ref.dtype)
            _fill_values_t(vt_ref.at[h], v_ref, lo, t)
            for n in range(n_blk):
                blk = k_ref[0, n * MOBA_BLOCK:(n + 1) * MOBA_BLOCK, lo:hi]
                km_ref[h, n:n + 1, :] = jnp.mean(blk.astype(jnp.float32), axis=0, keepdims=True)

    def query_step(step):
        key, qry = _tile_iotas(t)
        causal = key <= qry
        blk_row = lax.broadcasted_iota(jnp.int32, (km_ref.shape[1], t), 0)
        qt = _transposed(q_ref[0])
        jobs = []
        for r in range(Q_TILES):
            k = step * Q_TILES + r
            own = k >> blk_shift
            qts, sels = [], []
            for h in _HEADS:
                qf = qt[h * HEAD_DIM:(h + 1) * HEAD_DIM, r * t:(r + 1) * t]
                qts.append(qf.astype(_MXU))
                gate = _dot(km_ref[h], qf, precision=lax.Precision.HIGHEST)
                gate = jnp.where(blk_row < own, gate, NEG_INF)
                sel = _top_k_rows(gate, blk_row.astype(jnp.float32), min(MOBA_TOPK, n_blk - 1))
                sels.append(jnp.where(blk_row < own, sel, 0.0))

            def tile(own, sels, j, bias_ref=None, emask=None):
                n = j >> blk_shift
                return _KeyTile([_key_rows(kb_ref.at[h], j, t) for h in _HEADS], [vt_ref[h, j] for h in _HEADS],
                                None if bias_ref is None else [bias_ref[h] for h in _HEADS],
                                None if emask is None else [emask] * GROUP_HEADS,
                                None if n == own else [sels[h][n:n + 1] > 0.5 for h in _HEADS])

            tile = functools.partial(tile, own, sels)
            groups = _pairs([functools.partial(tile, j) for j in range(k - 1)])
            groups.append(([functools.partial(tile, k - 1, bs_ref)] if k else [])
                          + [functools.partial(tile, k, bd_ref, causal)])
            jobs.append((GROUP_HEADS, qts, groups))
        _store_heads(o_ref, [[_softmax_out(st) for st in states] for states in _softmax_jobs(t, jobs)])

    _per_step(i, n_steps, query_step)


def _kv_scratch(s, t, key_dim=HEAD_DIM, n_keys=GROUP_HEADS):
    return [pltpu.VMEM((n_keys, s, key_dim), _MXU), pltpu.VMEM((GROUP_HEADS, s // t, ACC_ROWS, t), _MXU)]


def _moba(proj, bias_diag, bias_sub, t=TILE):
    bsz, s, _ = proj.shape
    n_blk_pad = -(-(s // MOBA_BLOCK) // SUBLANES) * SUBLANES
    return pl.pallas_call(
        _moba_kernel,
        grid=(bsz, s // (Q_TILES * t)),
        in_specs=_group_specs(t, s, CB_MB_Q, CB_MB_K, CB_MB_V) + _bias_specs(t, 0),
        scratch_shapes=_kv_scratch(s, t) + [pltpu.VMEM((GROUP_HEADS, n_blk_pad, HEAD_DIM), jnp.float32)],
        compiler_params=_params("parallel", "arbitrary"),
        name="moba",
        **_group_out(bsz, s, t),
    )(proj, proj, proj, bias_diag, bias_sub)


def _diff_kernel(lam_ref, g_ref, q_ref, k_ref, v_ref, bd_ref, bs_ref, o_ref, kb_ref, vt_ref, *, lambda_init):
    t = q_ref.shape[1] // Q_TILES
    i = pl.program_id(1)
    key, qry = _tile_iotas(t)
    causal = key <= qry
    lv = lam_ref[...]
    lam = (jnp.exp(jnp.sum(lv[0:1] * lv[1:2], keepdims=True))
           - jnp.exp(jnp.sum(lv[2:3] * lv[3:4], keepdims=True)) + lambda_init)
    halves = range(2 * GROUP_HEADS)
    n_steps = k_ref.shape[1] // (Q_TILES * t)

    @_first_step(i, n_steps)
    def _():
        for c in halves:
            kb_ref[c] = k_ref[0, :, c * DIFF_HALF:(c + 1) * DIFF_HALF].astype(kb_ref.dtype)
        for h in _HEADS:
            _fill_values_t(vt_ref.at[h], v_ref, h * HEAD_DIM, t)

    def tile(j, bias_ref=None, emask=None):
        n = len(halves)
        return _KeyTile([_key_rows(kb_ref.at[c], j, t) for c in halves], [vt_ref[c // 2, j] for c in halves],
                        None if bias_ref is None else [bias_ref[c // 2] for c in halves],
                        None if emask is None else [emask] * n)

    def query_step(step):
        qts = _queries_t(q_ref, DIFF_HALF, t)
        jobs = []
        for r in range(Q_TILES):
            k = step * Q_TILES + r
            groups = _pairs([functools.partial(tile, j) for j in range(k - 1)])
            groups.append(([functools.partial(tile, k - 1, bs_ref)] if k else [])
                          + [functools.partial(tile, k, bd_ref, causal)])
            jobs.append((len(halves), qts[r], groups))
        outs = []
        for states in _softmax_jobs(t, jobs):
            heads = []
            for h in _HEADS:
                o = _softmax_out(states[2 * h]) - lam * _softmax_out(states[2 * h + 1])
                o = o * lax.rsqrt(jnp.mean(o * o, axis=0, keepdims=True) + NORM_EPS) * g_ref[...]
                heads.append(o * (1.0 - lambda_init))
            outs.append(heads)
        _store_heads(o_ref, outs)

    _per_step(i, n_steps, query_step)


def _diff(proj, lam_params, subln, bias_diag, bias_sub, lambda_init, t=TILE):
    bsz, s, _ = proj.shape
    return pl.pallas_call(
        functools.partial(_diff_kernel, lambda_init=lambda_init),
        grid=(bsz, s // (Q_TILES * t)),
        in_specs=[pl.BlockSpec(lam_params.shape, lambda b, i: (0, 0)),
                  pl.BlockSpec(subln.shape, lambda b, i: (0, 0))]
        + _group_specs(t, s, CB_DF_Q, CB_DF_K, CB_DF_V) + _bias_specs(t, 2),
        scratch_shapes=_kv_scratch(s, t, key_dim=DIFF_HALF, n_keys=2 * GROUP_HEADS),
        compiler_params=_params("parallel", "arbitrary"),
        name="diff_attention",
        **_group_out(bsz, s, t),
    )(lam_params, subln, proj, proj, proj, bias_diag, bias_sub)


def _compress_kernel(kcv_ref, pk_ref, pv_ref, wk1_ref, wk2_ref, wv1_ref, wv2t_ref, kc_ref, vct_ref):
    n_chunk = kc_ref.shape[1]

    branches = ((0, pk_ref, wk1_ref), (HEAD_DIM, pv_ref, wv1_ref))
    tops = [jnp.zeros((n_chunk, w1_ref.shape[1]), jnp.float32) for _, _, w1_ref in branches]
    bots = list(tops)
    pack = 2 * LANES // HEAD_DIM
    for l0 in range(0, CMP_STRIDE, pack):
        tokens = [kcv_ref[0, pl.ds(l, n_chunk, stride=CMP_STRIDE), :] for l in range(l0, l0 + pack)]
        for n, (col0, p_ref, w1_ref) in enumerate(branches):
            for half, acc in ((0, tops), (CMP_STRIDE, bots)):
                x = jnp.concatenate([(tok[:, col0:col0 + HEAD_DIM] + p_ref[half + l:half + l + 1, :]).astype(_MXU)
                                     for l, tok in zip(range(l0, l0 + pack), tokens)], axis=1)
                acc[n] = acc[n] + _dot(x, w1_ref[(half + l0) * HEAD_DIM:(half + l0 + pack) * HEAD_DIM, :])
    hidden = [jax.nn.gelu(top + pltpu.roll(bot, n_chunk - 1, axis=0)).astype(_MXU) for top, bot in zip(tops, bots)]
    kc_ref[0] = _dot(hidden[0], wk2_ref[...])
    vct_ref[0] = _dot_nt(wv2t_ref[...], hidden[1])


def _compress(kcv, pos_k, pos_v, wk1, wk2, wv1, wv2t, layer):
    bsz, s, width = kcv.shape
    n_chunk = s // CMP_STRIDE

    def full(a):
        return pl.BlockSpec((None,) + a.shape[1:], lambda b: (layer, 0, 0))

    return pl.pallas_call(
        _compress_kernel,
        grid=(bsz,),
        in_specs=[pl.BlockSpec((1, s, width), lambda b: (b, 0, 0)),
                  full(pos_k), full(pos_v), full(wk1), full(wk2), full(wv1), full(wv2t)],
        out_specs=[pl.BlockSpec((1, n_chunk, HEAD_DIM), lambda b: (b, 0, 0)),
                   pl.BlockSpec((1, HEAD_DIM, n_chunk), lambda b: (b, 0, 0))],
        out_shape=[jax.ShapeDtypeStruct((bsz, n_chunk, HEAD_DIM), jnp.float32),
                   jax.ShapeDtypeStruct((bsz, HEAD_DIM, n_chunk), jnp.float32)],
        compiler_params=_params("parallel"),
        name="nsa_compress",
    )(kcv, pos_k, pos_v, wk1, wk2, wv1, wv2t)


def _nsa_kernel(q_ref, kva_ref, kvb_ref, gq_ref, kc_ref, vct_ref, bd_ref, bs_ref, bc_ref, cover_ref, e_ref,
                o_ref, ks_ref, vst_ref, kw_ref, vwt_ref):
    t = q_ref.shape[1] // Q_TILES
    i = pl.program_id(1)
    key, qry = _tile_iotas(t)
    causal = key <= qry
    ks_col, vs_col, kw_col, vw_col, gate_col = 2 * HEAD_DIM, 3 * HEAD_DIM, 0, HEAD_DIM, 2 * HEAD_DIM
    n_steps = kva_ref.shape[1] // (Q_TILES * t)

    @_first_step(i, n_steps)
    def _():
        ks_ref[:, :HEAD_DIM] = kva_ref[0, :, ks_col:ks_col + HEAD_DIM].astype(ks_ref.dtype)
        ks_ref[:, HEAD_DIM:] = e_ref[...]
        kw_ref[...] = kvb_ref[0, :, kw_col:kw_col + HEAD_DIM].astype(kw_ref.dtype)
        _fill_values_t(vst_ref, kva_ref, vs_col, t)
        _fill_values_t(vwt_ref, kvb_ref, vw_col, t)

    def tile(k_ref, vt_ref, j, bias_ref=None, emask=None):
        n = GROUP_HEADS
        return _KeyTile([_key_rows(k_ref, j, t)] * n, [vt_ref[j]] * n,
                        None if bias_ref is None else [bias_ref[h] for h in _HEADS],
                        None if emask is None else [emask] * n)

    def query_step(step):
        qts = _queries_t(q_ref, HEAD_DIM, t)
        n_cmp = kc_ref.shape[1]
        n_slc = cover_ref.shape[0]
        kc = kc_ref[0].astype(_MXU)
        vct = vct_ref[0].astype(_MXU)
        c_row = lax.broadcasted_iota(jnp.int32, (n_cmp, t), 0)
        c_col = lax.broadcasted_iota(jnp.int32, (n_cmp, t), 1)
        s_row = lax.broadcasted_iota(jnp.int32, (n_slc, t), 0)
        s_col = lax.broadcasted_iota(jnp.int32, (n_slc, t), 1)
        tiles = [step * Q_TILES + r for r in range(Q_TILES)]

        o_cmp, importance = [], []
        for r, k in enumerate(tiles):
            visible = c_col + k * t >= c_row * CMP_STRIDE + (CMP_LEN - 1)
            cmp_scores = [_dot(kc, qts[r][h]) for h in _HEADS]
            cmp_probs = []
            p_sum = jnp.zeros((n_cmp, t), jnp.float32)
            for h in _HEADS:
                sc = jnp.where(visible, cmp_scores[h] + bc_ref[h, :, r * t:(r + 1) * t], NEG_INF)
                e = jnp.where(visible, jnp.exp2(sc - jnp.max(sc, axis=0, keepdims=True)), 0.0)
                p = e / jnp.maximum(jnp.sum(e, axis=0, keepdims=True), TINY)
                cmp_probs.append(p.astype(_MXU))
                p_sum = p_sum + p
            o_cmp.append([_dot(vct, cmp_probs[h]) for h in _HEADS])
            importance.append(_dot(cover_ref[...], p_sum, precision=lax.Precision.HIGHEST))

        n_back = WINDOW // t
        jobs = []
        for r, k in enumerate(tiles):
            window = [functools.partial(tile, kw_ref, vwt_ref, k - n_back, None, qry < key)] if k >= n_back else []
            for back in range(min(n_back - 1, k), 0, -1):
                window.append(functools.partial(tile, kw_ref, vwt_ref, k - back, bs_ref if back == 1 else None))
            window.append(functools.partial(tile, kw_ref, vwt_ref, k, bd_ref, causal))
            jobs.append((GROUP_HEADS, qts[r], [window]))
        o_win = [[_softmax_out(st) for st in states] for states in _softmax_jobs(t, jobs)]

        jobs = []
        for r, k in enumerate(tiles):
            own = jnp.right_shift(s_col + k * t, int(math.log2(SLC_LEN)))
            score = jnp.where(s_row == own, FORCE, jnp.where(s_row < own, importance[r], NEG_INF))
            sel = _top_k_rows(score, s_row.astype(jnp.float32), min(SLC_TOPN, n_slc))
            penalty = jnp.where(sel > 0.5, 0.0, NEG_INF).astype(_MXU)
            q_aug = [jnp.concatenate([qts[r][h], penalty], axis=0) for h in _HEADS]
            groups = _pairs([functools.partial(tile, ks_ref, vst_ref, j) for j in range(k - 1)])
            groups.append(([functools.partial(tile, ks_ref, vst_ref, k - 1, bs_ref)] if k else [])
                          + [functools.partial(tile, ks_ref, vst_ref, k, bd_ref, causal)])
            jobs.append((GROUP_HEADS, q_aug, groups))
        o_slc = [[_softmax_out(st) for st in states] for states in _softmax_jobs(t, jobs)]

        gates = _transposed(gq_ref[0, :, (gate_col // LANES) * LANES:(gate_col // LANES + 1) * LANES])
        gates = 1.0 / (1.0 + jnp.exp(-gates[gate_col % LANES:gate_col % LANES + N_GATES + 4]))
        outs = []
        for r in range(Q_TILES):
            heads = []
            for h in _HEADS:
                g = [gates[br * GROUP_HEADS + h:br * GROUP_HEADS + h + 1, r * t:(r + 1) * t] for br in range(3)]
                heads.append(g[0] * o_cmp[r][h] + g[1] * o_slc[r][h] + g[2] * o_win[r][h])
            outs.append(heads)
        _store_heads(o_ref, outs)

    _per_step(i, n_steps, query_step)


def _nsa(proj, kc, vc_t, bias_diag, bias_sub, bias_cmp, cover_t, expand, t=TILE):
    bsz, s, _ = proj.shape
    n_cmp = kc.shape[1]
    values_t = pltpu.VMEM((s // t, ACC_ROWS, t), _MXU)
    return pl.pallas_call(
        _nsa_kernel,
        grid=(bsz, s // (Q_TILES * t)),
        in_specs=_group_specs(t, s, CB_NS_Q, CB_NS_A, CB_NS_B)
        + [pl.BlockSpec((1, Q_TILES * t, GROUP_WIDTH), lambda b, i: (b, i, CB_NS_B)),
           pl.BlockSpec((1, n_cmp, HEAD_DIM), lambda b, i: (b, 0, 0)),
           pl.BlockSpec((1, HEAD_DIM, n_cmp), lambda b, i: (b, 0, 0))]
        + _bias_specs(t, 1)
        + [pl.BlockSpec((GROUP_HEADS, n_cmp, Q_TILES * t), lambda b, i: (0, 0, i)),
           pl.BlockSpec(cover_t.shape, lambda b, i: (0, 0)),
           pl.BlockSpec(expand.shape, lambda b, i: (0, 0))],
        scratch_shapes=[pltpu.VMEM((s, HEAD_DIM + expand.shape[1]), _MXU), values_t,
                        pltpu.VMEM((s, HEAD_DIM), _MXU), values_t],
        compiler_params=_params("parallel", "arbitrary"),
        name="nsa",
        **_group_out(bsz, s, t),
    )(proj, proj, proj, proj, kc, vc_t, bias_diag, bias_sub, bias_cmp, cover_t, expand)


def _nsa_constants(s):
    n_cmp = (s - CMP_LEN) // CMP_STRIDE + 1
    n_slc = s // SLC_LEN
    assert n_cmp + 1 == s // CMP_STRIDE and n_slc % SUBLANES == 0
    c_start = np.arange(n_cmp) * CMP_STRIDE
    s_start = np.arange(n_slc) * SLC_LEN
    cover = np.clip(np.minimum((c_start + CMP_LEN - 1)[:, None], (s_start + SLC_LEN - 1)[None, :])
                    - np.maximum(c_start[:, None], s_start[None, :]) + 1, 0, None) / CMP_LEN
    cover_t = np.zeros((n_slc, n_cmp + 1), np.float32)
    cover_t[:, :n_cmp] = cover.T
    expand = (np.arange(s)[:, None] // SLC_LEN == np.arange(n_slc)[None, :]).astype(np.float32)
    return jnp.asarray(cover_t), jnp.asarray(expand, _MXU)


def kernel(x, w_in, w_out, w_up, w_down, norm_attn, norm_mlp, cmp_pos_k, cmp_pos_v, cmp_k_w1, cmp_k_w2,
           cmp_v_w1, cmp_v_w2, diff_lambda, diff_subln, rel_bias, final_norm):
    bsz, s, d = x.shape
    depth = w_in.shape[0]
    t = TILE
    n_chunk = s // CMP_STRIDE
    assert s % MOBA_BLOCK == 0 and MOBA_BLOCK % t == 0 and WINDOW % t == 0 and t >= MAX_DISTANCE

    assert w_in.shape[2] + PAD_COLS == D_IN_PAD
    w_in_c = jnp.pad(w_in.astype(_MXU), ((0, 0), (0, 0), (0, PAD_COLS)))
    w_out_c, w_up_c, w_down_c = (w.astype(_MXU) for w in (w_out, w_up, w_down))
    wk1, wk2 = cmp_k_w1.astype(_MXU), cmp_k_w2.astype(_MXU)
    wv1, wv2t = cmp_v_w1.astype(_MXU), jnp.swapaxes(cmp_v_w2, 1, 2).astype(_MXU)

    table_t = jnp.pad(rel_bias.T, ((0, 0), (0, LANES - N_BUCKETS)))
    tiles = dict(n_heads=rel_bias.shape[1], head0=0, rows=t, cols=t, col_tile=t, row_stride=-1, col_stride=1)
    bias_diag = _bias_table(table_t, offset=0, **tiles)
    bias_sub = _bias_table(table_t, offset=t, **tiles)
    bias_cmp = _bias_table(table_t, n_heads=GROUP_HEADS, head0=GROUP_HEADS, rows=n_chunk, cols=s,
                           col_tile=t, row_stride=-CMP_STRIDE, col_stride=1, offset=-(CMP_LEN - 1))
    cover_t, expand = _nsa_constants(s)
    col_scale = np.ones((1, D_IN_PAD), np.float32)
    for cb, width in ((CB_SB_Q, HEAD_DIM), (CB_MB_Q, HEAD_DIM), (CB_NS_Q, HEAD_DIM), (CB_DF_Q, DIFF_HALF)):
        col_scale[:, cb * GROUP_WIDTH:(cb + 1) * GROUP_WIDTH] = width ** -0.5 * LOG2E
    col_scale = jnp.asarray(col_scale)

    x2 = x.reshape(bsz * s, d)
    for layer in range(depth):
        proj, kcv = _norm_matmul(x2, norm_attn[layer][None], w_in_c, layer, col_scale,
                                 gap=(COLS_BEFORE_PAD, COLS_BEFORE_PAD + PAD_COLS),
                                 side_col=CB_NS_A * GROUP_WIDTH, side_width=2 * HEAD_DIM)
        proj = proj.reshape(bsz, s, D_IN_PAD)
        o_sb = _stick_breaking(proj)
        o_mb = _moba(proj, bias_diag, bias_sub)
        kc, vc_t = _compress(kcv.reshape(bsz, s, 2 * HEAD_DIM), cmp_pos_k, cmp_pos_v, wk1, wk2, wv1, wv2t, layer)
        o_ns = _nsa(proj, kc, vc_t, bias_diag, bias_sub, bias_cmp, cover_t, expand)
        lambda_init = 0.8 - 0.6 * math.exp(-0.3 * layer)
        o_df = _diff(proj, diff_lambda[layer], diff_subln[layer][:, None], bias_diag, bias_sub, lambda_init)
        groups = [o.reshape(bsz * s, GROUP_WIDTH) for o in (o_sb, o_mb, o_ns, o_df)]
        x2 = _out_mlp(x2, groups, w_out_c, norm_mlp[layer][None], w_up_c, w_down_c, layer,
                      final_norm[None], final_norm=(layer == depth - 1))
    return x2.reshape(bsz, s, d)
```

```python
import functools
import math
from typing import Any, Callable, NamedTuple, Optional, Sequence

import numpy as np
import jax
import jax.numpy as jnp
from jax import lax
from jax.experimental import pallas as pl
from jax.experimental.pallas import tpu as pltpu

HEAD_DIM = 64
GROUP_HEADS = 4
GROUP_WIDTH = GROUP_HEADS * HEAD_DIM
NORM_EPS = 1e-6
NEG_INF = -1e30
BIG = 1e30
FORCE = 1e30
TINY = 1e-30
SOFTPLUS_CLAMP = 64.0
PICKED = -3e38
LOG2E = math.log2(math.e)
N_BUCKETS = 32
MAX_DISTANCE = 128
MOBA_BLOCK = 256
MOBA_TOPK = 3
CMP_LEN = 32
CMP_STRIDE = 16
SLC_LEN = 64
SLC_TOPN = 4
WINDOW = 512
DIFF_HALF = HEAD_DIM // 2
LANES = 128
SUBLANES = 8
TILE = 256
BF16_ROWS = 16
ACC_ROWS = HEAD_DIM + BF16_ROWS
N_GATES = 3 * GROUP_HEADS
COLS_BEFORE_PAD = 9 * GROUP_WIDTH - 2 * HEAD_DIM + N_GATES
PAD_COLS = 2 * HEAD_DIM - N_GATES
CB_SB_Q, CB_SB_K, CB_SB_V, CB_MB_Q, CB_MB_K, CB_MB_V, CB_NS_Q, CB_NS_A, CB_NS_B, CB_DF_Q, CB_DF_K, CB_DF_V = range(12)
D_IN_PAD = 12 * GROUP_WIDTH

_MXU = jnp.bfloat16
_VMEM_LIMIT = 56 * 1024 * 1024
_HEADS = range(GROUP_HEADS)
Q_TILES = 8
LOOKAHEAD = 2


def _dot(a, b, precision=None):
    return jnp.dot(a, b, precision=precision, preferred_element_type=jnp.float32)


def _dot_nt(a, b, precision=None):
    return lax.dot_general(a, b, (((1,), (1,)), ((), ())), precision=precision,
                           preferred_element_type=jnp.float32)


def _rms(x, g):
    return x * lax.rsqrt(jnp.mean(x * x, axis=-1, keepdims=True) + NORM_EPS) * g


def _params(*sem):
    return pltpu.CompilerParams(dimension_semantics=sem, vmem_limit_bytes=_VMEM_LIMIT)


def _norm_matmul_kernel(x_ref, g_ref, w_ref, scale_ref, o_ref, side_ref, wp_ref, *, tn, side_col, gap):
    @pl.when(pl.program_id(0) == 0)
    def _():
        lo, hi = gap
        wp_ref[:, :lo] = w_ref[:, :lo]
        wp_ref[:, lo:hi] = jnp.zeros((wp_ref.shape[0], hi - lo), wp_ref.dtype)
        wp_ref[:, hi:] = w_ref[:, lo:lo + wp_ref.shape[1] - hi]

    h = _rms(x_ref[...], g_ref[...]).astype(_MXU)
    for j in range(wp_ref.shape[1] // tn):
        cols = slice(j * tn, (j + 1) * tn)
        acc = _dot(h, wp_ref[:, cols])
        o_ref[:, cols] = (acc * scale_ref[:, cols]).astype(o_ref.dtype)
        if j * tn <= side_col < (j + 1) * tn:
            side_ref[...] = acc[:, side_col - j * tn:side_col - j * tn + side_ref.shape[1]]


def _norm_matmul(x, g, w, layer, col_scale, *, gap, side_col, side_width, tm=512, tn=1024):
    m, d = x.shape
    n = w.shape[2]
    assert side_col % LANES == 0 and side_col // tn == (side_col + side_width - 1) // tn
    return pl.pallas_call(
        functools.partial(_norm_matmul_kernel, tn=tn, side_col=side_col, gap=gap),
        grid=(m // tm,),
        in_specs=[pl.BlockSpec((tm, d), lambda i: (i, 0)),
                  pl.BlockSpec((1, d), lambda i: (0, 0)),
                  pl.BlockSpec((None, d, w.shape[2]), lambda i: (layer, 0, 0), pipeline_mode=pl.Buffered(1)),
                  pl.BlockSpec((1, n), lambda i: (0, 0))],
        out_specs=[pl.BlockSpec((tm, n), lambda i: (i, 0)),
                   pl.BlockSpec((tm, side_width), lambda i: (i, 0))],
        out_shape=[jax.ShapeDtypeStruct((m, n), _MXU),
                   jax.ShapeDtypeStruct((m, side_width), jnp.float32)],
        scratch_shapes=[pltpu.VMEM((d, n), _MXU)],
        compiler_params=_params("arbitrary"),
        name="norm_in_proj",
    )(x, g, w, col_scale)


def _out_mlp_kernel(x_ref, a_ref, b_ref, c_ref, d_ref, wo_ref, g_ref, wu_ref, wd_ref, gf_ref, o_ref,
                    *, final_norm, tf):
    mixed = jnp.concatenate([a_ref[...], b_ref[...], c_ref[...], d_ref[...]], axis=1)
    y = x_ref[...] + _dot(mixed, wo_ref[...])
    h = _rms(y, g_ref[...]).astype(_MXU)
    for c in range(wu_ref.shape[1] // tf):
        u = jnp.square(jnp.maximum(_dot(h, wu_ref[:, c * tf:(c + 1) * tf]), 0.0))
        y = y + _dot(u.astype(_MXU), wd_ref[c * tf:(c + 1) * tf, :])
    if final_norm:
        y = _rms(y, gf_ref[...])
    o_ref[...] = y


def _out_mlp(x, groups, w_out, g, w_up, w_down, layer, g_final, *, final_norm, tm=512, tf=2048):
    m, d = x.shape
    f = w_up.shape[2]
    gspec = pl.BlockSpec((tm, GROUP_WIDTH), lambda i: (i, 0))
    row = pl.BlockSpec((1, d), lambda i: (0, 0))

    def resident(rows, cols):
        return pl.BlockSpec((None, rows, cols), lambda i: (layer, 0, 0), pipeline_mode=pl.Buffered(1))

    return pl.pallas_call(
        functools.partial(_out_mlp_kernel, final_norm=final_norm, tf=tf),
        grid=(m // tm,),
        in_specs=[pl.BlockSpec((tm, d), lambda i: (i, 0)), gspec, gspec, gspec, gspec,
                  resident(d, d), row, resident(d, f), resident(f, d), row],
        out_specs=pl.BlockSpec((tm, d), lambda i: (i, 0)),
        out_shape=jax.ShapeDtypeStruct((m, d), jnp.float32),
        compiler_params=_params("parallel"),
        name="out_proj_mlp_residual",
    )(x, *groups, w_out, g, w_up, w_down, g_final)


def _t5_bucket(dist):
    n = jnp.maximum(dist, 0)
    max_exact = N_BUCKETS // 2
    nf = jnp.maximum(n, 1).astype(jnp.float32)
    large = max_exact + (jnp.log(nf / max_exact) / math.log(MAX_DISTANCE / max_exact)
                         * (N_BUCKETS - max_exact)).astype(jnp.int32)
    large = jnp.minimum(large, N_BUCKETS - 1)
    return jnp.where(n < max_exact, n, large)


def _bias_kernel(tab_ref, o_ref, *, row_stride, col_stride, offset, head0):
    nh, tr, tc = o_ref.shape
    for blk in range(tc // LANES):
        rows = lax.broadcasted_iota(jnp.int32, (tr, LANES), 0)
        cols = lax.broadcasted_iota(jnp.int32, (tr, LANES), 1) + (pl.program_id(0) * tc + blk * LANES)
        bucket = _t5_bucket(rows * row_stride + cols * col_stride + offset)
        for h in range(nh):
            row = tab_ref[head0 + h:head0 + h + 1, :]
            row = (row - row[:, N_BUCKETS - 1:N_BUCKETS]) * LOG2E
            o_ref[h, :, blk * LANES:(blk + 1) * LANES] = jnp.take_along_axis(
                jnp.broadcast_to(row, (tr, LANES)), bucket, axis=1, mode="promise_in_bounds")


def _bias_table(table_t, *, n_heads, head0, rows, cols, col_tile, row_stride, col_stride, offset):
    return pl.pallas_call(
        functools.partial(_bias_kernel, row_stride=row_stride, col_stride=col_stride,
                          offset=offset, head0=head0),
        grid=(cols // col_tile,),
        in_specs=[pl.BlockSpec(table_t.shape, lambda i: (0, 0))],
        out_specs=pl.BlockSpec((n_heads, rows, col_tile), lambda i: (0, 0, i)),
        out_shape=jax.ShapeDtypeStruct((n_heads, rows, cols), jnp.float32),
        compiler_params=_params("parallel"),
        name="t5_bias_tiles",
    )(table_t)


def _softmax_init(t):
    return (jnp.full((1, t), NEG_INF, jnp.float32), jnp.zeros((ACC_ROWS, t), jnp.float32))


class _KeyTile(NamedTuple):
    kts: Sequence[Any]
    vts: Sequence[Any]
    biases: Optional[Sequence[Any]] = None
    emasks: Optional[Sequence[Any]] = None
    qmasks: Optional[Sequence[Any]] = None


def _round_robin(lists):
    out = []
    for rank in range(max(map(len, lists), default=0)):
        out.extend(items[rank] for items in lists if rank < len(items))
    return out


def _softmax_jobs(t, jobs):
    built = {}

    def tiles_of(job, g):
        if (job, g) not in built:
            built[job, g] = [make() for make in jobs[job][2][g]]
        return built[job, g]

    def scores_of(job, g, c):
        qts = jobs[job][1]
        row = []
        for tile in tiles_of(job, g):
            s = _dot(tile.kts[c], qts[c])
            if tile.biases is not None:
                s = s + tile.biases[c]
            if tile.emasks is not None:
                s = jnp.where(tile.emasks[c], s, NEG_INF)
            row.append(s.astype(_MXU))
        return row

    def update(state, job, g, c, scores):
        m, acc = state
        m_new = m
        for tile, s in zip(tiles_of(job, g), scores):
            tile_max = jnp.max(s, axis=0, keepdims=True).astype(jnp.float32)
            if tile.qmasks is not None:
                tile_max = jnp.where(tile.qmasks[c], tile_max, NEG_INF)
            m_new = jnp.maximum(m_new, tile_max)
        seen = m_new > 0.5 * NEG_INF
        acc = jnp.exp2(m - m_new) * acc
        for tile, s in zip(tiles_of(job, g), scores):
            ok = seen if tile.qmasks is None else jnp.logical_and(seen, tile.qmasks[c])
            acc = acc + _dot(tile.vts[c], jnp.exp2(s - jnp.where(ok, m_new, BIG).astype(_MXU)))
        return m_new, acc

    units = _round_robin([[(job, g, c) for g in range(len(groups)) for c in range(n)]
                          for job, (n, _, groups) in enumerate(jobs)])
    lookahead = max(LOOKAHEAD, len(jobs))
    states = [[_softmax_init(t) for _ in range(n)] for n, _, _ in jobs]
    pending = {k: scores_of(*units[k]) for k in range(min(lookahead, len(units)))}
    for k, (job, g, c) in enumerate(units):
        if k + lookahead < len(units):
            pending[k + lookahead] = scores_of(*units[k + lookahead])
        states[job][c] = update(states[job][c], job, g, c, pending.pop(k))
        yield
    return states


def _pairs(items):
    return [items[p:p + 2] for p in range(0, len(items), 2)]


class _Mixer(NamedTuple):
    emit: Callable[..., Any]
    in_specs: Sequence[Any]
    operands: Sequence[Any]
    scratch: Sequence[Any]
    stride: int


def _fused_mixers_kernel(*refs, mixers):
    n_in = sum(len(m.in_specs) for m in mixers)
    in_refs, out_refs, scratch_refs = refs[:n_in], refs[n_in:n_in + len(mixers)], refs[n_in + len(mixers):]
    active = []
    for m, o_ref in zip(mixers, out_refs):
        ins, in_refs = in_refs[:len(m.in_specs)], in_refs[len(m.in_specs):]
        scr, scratch_refs = scratch_refs[:len(m.scratch)], scratch_refs[len(m.scratch):]
        active.append((m.emit(*ins, o_ref, *scr), m.stride))
    while active:
        for item in list(active):
            emitter, stride = item
            for _ in range(stride):
                if next(emitter, StopIteration) is StopIteration:
                    active.remove(item)
                    break


def _run_mixers(mixers, bsz, s, name):
    return pl.pallas_call(
        functools.partial(_fused_mixers_kernel, mixers=tuple(mixers)),
        grid=(bsz,),
        in_specs=[spec for m in mixers for spec in m.in_specs],
        out_specs=[pl.BlockSpec((1, s, GROUP_WIDTH), lambda b: (b, 0, 0))] * len(mixers),
        out_shape=[jax.ShapeDtypeStruct((bsz, s, GROUP_WIDTH), _MXU)] * len(mixers),
        scratch_shapes=[shape for m in mixers for shape in m.scratch],
        compiler_params=_params("parallel"),
        name=name,
    )(*[op for m in mixers for op in m.operands])


def _softmax_out(state):
    acc = state[1]
    return acc[:HEAD_DIM] / jnp.maximum(acc[HEAD_DIM:HEAD_DIM + 1], TINY)


def _top_k_rows(score, row_f, k):
    sel = jnp.zeros(score.shape, jnp.float32)
    for _ in range(k):
        mx = jnp.max(score, axis=0, keepdims=True)
        idx = jnp.min(jnp.where(score == mx, row_f, float(score.shape[0])), axis=0, keepdims=True)
        pick = row_f == idx
        sel = jnp.where(pick, 1.0, sel)
        score = jnp.where(pick, PICKED, score)
    return sel


def _now(fn):
    fn()


def _tile_iotas(t):
    return (lax.broadcasted_iota(jnp.int32, (t, t), 0), lax.broadcasted_iota(jnp.int32, (t, t), 1))


def _key_rows(ref, j, t):
    if isinstance(j, int):
        return ref[j * t:(j + 1) * t, :]
    return ref[pl.ds(pl.multiple_of(j * t, t), t), :]


def _transposed(ref_block):
    return ref_block.astype(jnp.float32).T


def _queries_t(q_ref, width, t):
    qt = _transposed(q_ref[0]).astype(_MXU)
    return [[qt[c * width:(c + 1) * width, r * t:(r + 1) * t] for c in range(GROUP_WIDTH // width)]
            for r in range(Q_TILES)]


def _fill_values_t(vt_ref, v_ref, col0, t):
    n_tiles, rows, _ = vt_ref.shape
    lane_block = (col0 // LANES) * LANES
    for c in range(n_tiles):
        blk = _transposed(v_ref[0, c * t:(c + 1) * t, lane_block:lane_block + LANES])
        vt_ref[c, 0:HEAD_DIM, :] = blk[col0 - lane_block:col0 - lane_block + HEAD_DIM].astype(vt_ref.dtype)
        if rows == ACC_ROWS:
            first = lax.broadcasted_iota(jnp.int32, (rows - HEAD_DIM, t), 0) == 0
            vt_ref[c, HEAD_DIM:rows, :] = jnp.where(first, 1.0, 0.0).astype(vt_ref.dtype)


def _group_specs(s, *column_blocks):
    return [pl.BlockSpec((1, s, GROUP_WIDTH), functools.partial(lambda cb, b: (b, 0, cb), cb))
            for cb in column_blocks]


def _bias_specs(t, head_group):
    spec = pl.BlockSpec((GROUP_HEADS, t, t), lambda b: (head_group, 0, 0))
    return [spec, spec]


def _store_heads(o_ref, outs_t):
    tiles = [jnp.concatenate(heads, axis=0) for heads in outs_t]
    o_ref[0] = jnp.concatenate(tiles, axis=1).T.astype(o_ref.dtype)


def _sb_emit(q_ref, k_ref, v_ref, o_ref, kb_ref, vt_ref):
    t = q_ref.shape[1] // Q_TILES
    key, qry = _tile_iotas(t)
    strict = key < qry
    later = jnp.where(qry > key, 1.0, 0.0).astype(_MXU)

    @_now
    def _():
        for h in _HEADS:
            kb_ref[h] = k_ref[0, :, h * HEAD_DIM:(h + 1) * HEAD_DIM].astype(kb_ref.dtype)
            _fill_values_t(vt_ref.at[h], v_ref, h * HEAD_DIM, t)

    def query_step(step):
        qts = _queries_t(q_ref, HEAD_DIM, t)
        units = _round_robin([[(r, step * Q_TILES + r, pair, h)
                               for pair in _pairs(list(range(step * Q_TILES + r, -1, -1))) for h in _HEADS]
                              for r in range(Q_TILES)])
        zero = (jnp.zeros((HEAD_DIM, t), jnp.float32), jnp.zeros((1, t), jnp.float32))
        carry = [[zero] * GROUP_HEADS for _ in range(Q_TILES)]
        zs, log_keeps, suffixes = {}, {}, {}

        def scores(u):
            r, _, pair, h = units[u]
            zs[u] = [_dot(_key_rows(kb_ref.at[h], j, t), qts[r][h]) for j in pair]

        def keeps(u):
            _, k, pair, _ = units[u]
            log_keeps[u], suffixes[u] = [], []
            for j, z in zip(pair, zs[u]):
                drop = jnp.maximum(jnp.log2(1.0 + jnp.exp2(jnp.minimum(z, SOFTPLUS_CLAMP))), z)
                if j == k:
                    drop = jnp.where(strict, drop, 0.0)
                log_keeps[u].append(drop)
                suffixes[u].append(_dot(later, drop.astype(_MXU)))

        def values(u):
            r, k, pair, h = units[u]
            acc, run = carry[r][h]
            weights = []
            for j, z, drop, suffix in zip(pair, zs.pop(u), log_keeps.pop(u), suffixes.pop(u)):
                a = jnp.exp2(z - drop - suffix + run)
                if j == k:
                    a = jnp.where(strict, a, 0.0)
                weights.append(a.astype(_MXU))
                run = run - (suffix[0:1] + drop[0:1])
            for j, w in zip(pair, weights):
                acc = acc + _dot(vt_ref[h, j], w)
            carry[r][h] = (acc, run)

        stages = (scores, keeps, values)
        for tick in range(len(units) + len(stages) - 1):
            for lag, stage in enumerate(stages):
                if 0 <= tick - lag < len(units):
                    stage(tick - lag)
            yield
        _store_heads(o_ref, [[c[0] for c in tile_carry] for tile_carry in carry])

    yield from query_step(0)


def _stick_breaking(proj, t=TILE):
    s = proj.shape[1]
    scratch = [pltpu.VMEM((GROUP_HEADS, s, HEAD_DIM), _MXU), pltpu.VMEM((GROUP_HEADS, s // t, HEAD_DIM, t), _MXU)]
    return _Mixer(_sb_emit, _group_specs(s, CB_SB_Q, CB_SB_K, CB_SB_V), [proj] * 3, scratch, stride=1)


def _moba_emit(q_ref, k_ref, v_ref, bd_ref, bs_ref, o_ref, kb_ref, vt_ref, km_ref):
    t = q_ref.shape[1] // Q_TILES
    n_blk = k_ref.shape[1] // MOBA_BLOCK
    tiles_per_blk = MOBA_BLOCK // t
    blk_shift = int(math.log2(tiles_per_blk))

    @_now
    def _():
        km_ref[...] = jnp.zeros_like(km_ref)
        for h in _HEADS:
            lo, hi = h * HEAD_DIM, (h + 1) * HEAD_DIM
            kb_ref[h] = k_ref[0, :, lo:hi].astype(kb_ref.dtype)
            _fill_values_t(vt_ref.at[h], v_ref, lo, t)
            for n in range(n_blk):
                blk = k_ref[0, n * MOBA_BLOCK:(n + 1) * MOBA_BLOCK, lo:hi]
                km_ref[h, n:n + 1, :] = jnp.mean(blk.astype(jnp.float32), axis=0, keepdims=True)

    def query_step(step):
        key, qry = _tile_iotas(t)
        causal = key <= qry
        blk_row = lax.broadcasted_iota(jnp.int32, (km_ref.shape[1], t), 0)
        qt = _transposed(q_ref[0])
        jobs = []
        for r in range(Q_TILES):
            k = step * Q_TILES + r
            own = k >> blk_shift
            qts, sels = [], []
            for h in _HEADS:
                qf = qt[h * HEAD_DIM:(h + 1) * HEAD_DIM, r * t:(r + 1) * t]
                qts.append(qf.astype(_MXU))
                gate = _dot(km_ref[h], qf, precision=lax.Precision.HIGHEST)
                gate = jnp.where(blk_row < own, gate, NEG_INF)
                sel = _top_k_rows(gate, blk_row.astype(jnp.float32), min(MOBA_TOPK, n_blk - 1))
                sels.append(jnp.where(blk_row < own, sel, 0.0))

            def tile(own, sels, j, bias_ref=None, emask=None):
                n = j >> blk_shift
                return _KeyTile([_key_rows(kb_ref.at[h], j, t) for h in _HEADS], [vt_ref[h, j] for h in _HEADS],
                                None if bias_ref is None else [bias_ref[h] for h in _HEADS],
                                None if emask is None else [emask] * GROUP_HEADS,
                                None if n == own else [sels[h][n:n + 1] > 0.5 for h in _HEADS])

            tile = functools.partial(tile, own, sels)
            groups = _pairs([functools.partial(tile, j) for j in range(k - 1)])
            groups.append(([functools.partial(tile, k - 1, bs_ref)] if k else [])
                          + [functools.partial(tile, k, bd_ref, causal)])
            jobs.append((GROUP_HEADS, qts, groups))
        all_states = yield from _softmax_jobs(t, jobs)
        _store_heads(o_ref, [[_softmax_out(st) for st in states] for states in all_states])

    yield from query_step(0)


def _kv_scratch(s, t, key_dim=HEAD_DIM, n_keys=GROUP_HEADS):
    return [pltpu.VMEM((n_keys, s, key_dim), _MXU), pltpu.VMEM((GROUP_HEADS, s // t, ACC_ROWS, t), _MXU)]


def _moba(proj, bias_diag, bias_sub, t=TILE):
    s = proj.shape[1]
    n_blk_pad = -(-(s // MOBA_BLOCK) // SUBLANES) * SUBLANES
    scratch = _kv_scratch(s, t) + [pltpu.VMEM((GROUP_HEADS, n_blk_pad, HEAD_DIM), jnp.float32)]
    return _Mixer(_moba_emit, _group_specs(s, CB_MB_Q, CB_MB_K, CB_MB_V) + _bias_specs(t, 0),
                  [proj] * 3 + [bias_diag, bias_sub], scratch, stride=2)


def _diff_emit(lam_ref, g_ref, q_ref, k_ref, v_ref, bd_ref, bs_ref, o_ref, kb_ref, vt_ref, *, lambda_init):
    t = q_ref.shape[1] // Q_TILES
    key, qry = _tile_iotas(t)
    causal = key <= qry
    lv = lam_ref[...]
    lam = (jnp.exp(jnp.sum(lv[0:1] * lv[1:2], keepdims=True))
           - jnp.exp(jnp.sum(lv[2:3] * lv[3:4], keepdims=True)) + lambda_init)
    halves = range(2 * GROUP_HEADS)

    @_now
    def _():
        for c in halves:
            kb_ref[c] = k_ref[0, :, c * DIFF_HALF:(c + 1) * DIFF_HALF].astype(kb_ref.dtype)
        for h in _HEADS:
            _fill_values_t(vt_ref.at[h], v_ref, h * HEAD_DIM, t)

    def tile(j, bias_ref=None, emask=None):
        n = len(halves)
        return _KeyTile([_key_rows(kb_ref.at[c], j, t) for c in halves], [vt_ref[c // 2, j] for c in halves],
                        None if bias_ref is None else [bias_ref[c // 2] for c in halves],
                        None if emask is None else [emask] * n)

    def query_step(step):
        qts = _queries_t(q_ref, DIFF_HALF, t)
        jobs = []
        for r in range(Q_TILES):
            k = step * Q_TILES + r
            groups = _pairs([functools.partial(tile, j) for j in range(k - 1)])
            groups.append(([functools.partial(tile, k - 1, bs_ref)] if k else [])
                          + [functools.partial(tile, k, bd_ref, causal)])
            jobs.append((len(halves), qts[r], groups))
        all_states = yield from _softmax_jobs(t, jobs)
        outs = []
        for states in all_states:
            heads = []
            for h in _HEADS:
                o = _softmax_out(states[2 * h]) - lam * _softmax_out(states[2 * h + 1])
                o = o * lax.rsqrt(jnp.mean(o * o, axis=0, keepdims=True) + NORM_EPS) * g_ref[...]
                heads.append(o * (1.0 - lambda_init))
            outs.append(heads)
        _store_heads(o_ref, outs)

    yield from query_step(0)


def _diff(proj, lam_params, subln, bias_diag, bias_sub, lambda_init, t=TILE):
    s = proj.shape[1]
    in_specs = ([pl.BlockSpec(lam_params.shape, lambda b: (0, 0)), pl.BlockSpec(subln.shape, lambda b: (0, 0))]
                + _group_specs(s, CB_DF_Q, CB_DF_K, CB_DF_V) + _bias_specs(t, 2))
    return _Mixer(functools.partial(_diff_emit, lambda_init=lambda_init), in_specs,
                  [lam_params, subln] + [proj] * 3 + [bias_diag, bias_sub],
                  _kv_scratch(s, t, key_dim=DIFF_HALF, n_keys=2 * GROUP_HEADS), stride=2)


def _compress_kernel(kcv_ref, pk_ref, pv_ref, wk1_ref, wk2_ref, wv1_ref, wv2t_ref, kc_ref, vct_ref):
    n_chunk = kc_ref.shape[1]

    branches = ((0, pk_ref, wk1_ref), (HEAD_DIM, pv_ref, wv1_ref))
    tops = [jnp.zeros((n_chunk, w1_ref.shape[1]), jnp.float32) for _, _, w1_ref in branches]
    bots = list(tops)
    pack = 2 * LANES // HEAD_DIM
    for l0 in range(0, CMP_STRIDE, pack):
        tokens = [kcv_ref[0, pl.ds(l, n_chunk, stride=CMP_STRIDE), :] for l in range(l0, l0 + pack)]
        for n, (col0, p_ref, w1_ref) in enumerate(branches):
            for half, acc in ((0, tops), (CMP_STRIDE, bots)):
                x = jnp.concatenate([(tok[:, col0:col0 + HEAD_DIM] + p_ref[half + l:half + l + 1, :]).astype(_MXU)
                                     for l, tok in zip(range(l0, l0 + pack), tokens)], axis=1)
                acc[n] = acc[n] + _dot(x, w1_ref[(half + l0) * HEAD_DIM:(half + l0 + pack) * HEAD_DIM, :])
    hidden = [jax.nn.gelu(top + pltpu.roll(bot, n_chunk - 1, axis=0)).astype(_MXU) for top, bot in zip(tops, bots)]
    kc_ref[0] = _dot(hidden[0], wk2_ref[...])
    vct_ref[0] = _dot_nt(wv2t_ref[...], hidden[1])


def _compress(kcv, pos_k, pos_v, wk1, wk2, wv1, wv2t, layer):
    bsz, s, width = kcv.shape
    n_chunk = s // CMP_STRIDE

    def full(a):
        return pl.BlockSpec((None,) + a.shape[1:], lambda b: (layer, 0, 0))

    return pl.pallas_call(
        _compress_kernel,
        grid=(bsz,),
        in_specs=[pl.BlockSpec((1, s, width), lambda b: (b, 0, 0)),
                  full(pos_k), full(pos_v), full(wk1), full(wk2), full(wv1), full(wv2t)],
        out_specs=[pl.BlockSpec((1, n_chunk, HEAD_DIM), lambda b: (b, 0, 0)),
                   pl.BlockSpec((1, HEAD_DIM, n_chunk), lambda b: (b, 0, 0))],
        out_shape=[jax.ShapeDtypeStruct((bsz, n_chunk, HEAD_DIM), jnp.float32),
                   jax.ShapeDtypeStruct((bsz, HEAD_DIM, n_chunk), jnp.float32)],
        compiler_params=_params("parallel"),
        name="nsa_compress",
    )(kcv, pos_k, pos_v, wk1, wk2, wv1, wv2t)


def _nsa_emit(q_ref, kva_ref, kvb_ref, gq_ref, kc_ref, vct_ref, bd_ref, bs_ref, bc_ref, cover_ref, e_ref,
                o_ref, ks_ref, vst_ref, kw_ref, vwt_ref):
    t = q_ref.shape[1] // Q_TILES
    key, qry = _tile_iotas(t)
    causal = key <= qry
    ks_col, vs_col, kw_col, vw_col, gate_col = 2 * HEAD_DIM, 3 * HEAD_DIM, 0, HEAD_DIM, 2 * HEAD_DIM

    @_now
    def _():
        ks_ref[:, :HEAD_DIM] = kva_ref[0, :, ks_col:ks_col + HEAD_DIM].astype(ks_ref.dtype)
        ks_ref[:, HEAD_DIM:] = e_ref[...]
        kw_ref[...] = kvb_ref[0, :, kw_col:kw_col + HEAD_DIM].astype(kw_ref.dtype)
        _fill_values_t(vst_ref, kva_ref, vs_col, t)
        _fill_values_t(vwt_ref, kvb_ref, vw_col, t)

    def tile(k_ref, vt_ref, j, bias_ref=None, emask=None):
        n = GROUP_HEADS
        return _KeyTile([_key_rows(k_ref, j, t)] * n, [vt_ref[j]] * n,
                        None if bias_ref is None else [bias_ref[h] for h in _HEADS],
                        None if emask is None else [emask] * n)

    def query_step(step):
        qts = _queries_t(q_ref, HEAD_DIM, t)
        n_cmp = kc_ref.shape[1]
        n_slc = cover_ref.shape[0]
        kc = kc_ref[0].astype(_MXU)
        vct = vct_ref[0].astype(_MXU)
        c_row = lax.broadcasted_iota(jnp.int32, (n_cmp, t), 0)
        c_col = lax.broadcasted_iota(jnp.int32, (n_cmp, t), 1)
        s_row = lax.broadcasted_iota(jnp.int32, (n_slc, t), 0)
        s_col = lax.broadcasted_iota(jnp.int32, (n_slc, t), 1)
        tiles = [step * Q_TILES + r for r in range(Q_TILES)]

        o_cmp, importance = [], []
        for r, k in enumerate(tiles):
            visible = c_col + k * t >= c_row * CMP_STRIDE + (CMP_LEN - 1)
            cmp_scores = [_dot(kc, qts[r][h]) for h in _HEADS]
            cmp_probs = []
            p_sum = jnp.zeros((n_cmp, t), jnp.float32)
            for h in _HEADS:
                sc = jnp.where(visible, cmp_scores[h] + bc_ref[h, :, r * t:(r + 1) * t], NEG_INF)
                e = jnp.where(visible, jnp.exp2(sc - jnp.max(sc, axis=0, keepdims=True)), 0.0)
                p = e / jnp.maximum(jnp.sum(e, axis=0, keepdims=True), TINY)
                cmp_probs.append(p.astype(_MXU))
                p_sum = p_sum + p
            o_cmp.append([_dot(vct, cmp_probs[h]) for h in _HEADS])
            importance.append(_dot(cover_ref[...], p_sum, precision=lax.Precision.HIGHEST))

        n_back = WINDOW // t
        jobs = []
        for r, k in enumerate(tiles):
            window = [functools.partial(tile, kw_ref, vwt_ref, k - n_back, None, qry < key)] if k >= n_back else []
            for back in range(min(n_back - 1, k), 0, -1):
                window.append(functools.partial(tile, kw_ref, vwt_ref, k - back, bs_ref if back == 1 else None))
            window.append(functools.partial(tile, kw_ref, vwt_ref, k, bd_ref, causal))
            jobs.append((GROUP_HEADS, qts[r], [window]))
        win_states = yield from _softmax_jobs(t, jobs)
        o_win = [[_softmax_out(st) for st in states] for states in win_states]

        jobs = []
        for r, k in enumerate(tiles):
            own = jnp.right_shift(s_col + k * t, int(math.log2(SLC_LEN)))
            score = jnp.where(s_row == own, FORCE, jnp.where(s_row < own, importance[r], NEG_INF))
            sel = _top_k_rows(score, s_row.astype(jnp.float32), min(SLC_TOPN, n_slc))
            penalty = jnp.where(sel > 0.5, 0.0, NEG_INF).astype(_MXU)
            q_aug = [jnp.concatenate([qts[r][h], penalty], axis=0) for h in _HEADS]
            groups = _pairs([functools.partial(tile, ks_ref, vst_ref, j) for j in range(k - 1)])
            groups.append(([functools.partial(tile, ks_ref, vst_ref, k - 1, bs_ref)] if k else [])
                          + [functools.partial(tile, ks_ref, vst_ref, k, bd_ref, causal)])
            jobs.append((GROUP_HEADS, q_aug, groups))
        slc_states = yield from _softmax_jobs(t, jobs)
        o_slc = [[_softmax_out(st) for st in states] for states in slc_states]

        gates = _transposed(gq_ref[0, :, (gate_col // LANES) * LANES:(gate_col // LANES + 1) * LANES])
        gates = 1.0 / (1.0 + jnp.exp(-gates[gate_col % LANES:gate_col % LANES + N_GATES + 4]))
        outs = []
        for r in range(Q_TILES):
            heads = []
            for h in _HEADS:
                g = [gates[br * GROUP_HEADS + h:br * GROUP_HEADS + h + 1, r * t:(r + 1) * t] for br in range(3)]
                heads.append(g[0] * o_cmp[r][h] + g[1] * o_slc[r][h] + g[2] * o_win[r][h])
            outs.append(heads)
        _store_heads(o_ref, outs)

    yield from query_step(0)


def _nsa(proj, kc, vc_t, bias_diag, bias_sub, bias_cmp, cover_t, expand, t=TILE):
    s = proj.shape[1]
    n_cmp = kc.shape[1]
    values_t = pltpu.VMEM((s // t, ACC_ROWS, t), _MXU)
    in_specs = (_group_specs(s, CB_NS_Q, CB_NS_A, CB_NS_B, CB_NS_B)
                + [pl.BlockSpec((1, n_cmp, HEAD_DIM), lambda b: (b, 0, 0)),
                   pl.BlockSpec((1, HEAD_DIM, n_cmp), lambda b: (b, 0, 0))]
                + _bias_specs(t, 1)
                + [pl.BlockSpec(bias_cmp.shape, lambda b: (0, 0, 0)),
                   pl.BlockSpec(cover_t.shape, lambda b: (0, 0)),
                   pl.BlockSpec(expand.shape, lambda b: (0, 0))])
    scratch = [pltpu.VMEM((s, HEAD_DIM + expand.shape[1]), _MXU), values_t, pltpu.VMEM((s, HEAD_DIM), _MXU), values_t]
    return _Mixer(_nsa_emit, in_specs, [proj] * 4 + [kc, vc_t, bias_diag, bias_sub, bias_cmp, cover_t, expand],
                  scratch, stride=3)


def _nsa_constants(s):
    n_cmp = (s - CMP_LEN) // CMP_STRIDE + 1
    n_slc = s // SLC_LEN
    assert n_cmp + 1 == s // CMP_STRIDE and n_slc % SUBLANES == 0
    c_start = np.arange(n_cmp) * CMP_STRIDE
    s_start = np.arange(n_slc) * SLC_LEN
    cover = np.clip(np.minimum((c_start + CMP_LEN - 1)[:, None], (s_start + SLC_LEN - 1)[None, :])
                    - np.maximum(c_start[:, None], s_start[None, :]) + 1, 0, None) / CMP_LEN
    cover_t = np.zeros((n_slc, n_cmp + 1), np.float32)
    cover_t[:, :n_cmp] = cover.T
    expand = (np.arange(s)[:, None] // SLC_LEN == np.arange(n_slc)[None, :]).astype(np.float32)
    return jnp.asarray(cover_t), jnp.asarray(expand, _MXU)


def kernel(x, w_in, w_out, w_up, w_down, norm_attn, norm_mlp, cmp_pos_k, cmp_pos_v, cmp_k_w1, cmp_k_w2,
           cmp_v_w1, cmp_v_w2, diff_lambda, diff_subln, rel_bias, final_norm):
    bsz, s, d = x.shape
    depth = w_in.shape[0]
    t = TILE
    n_chunk = s // CMP_STRIDE
    assert s == Q_TILES * t and MOBA_BLOCK % t == 0 and WINDOW % t == 0 and t >= MAX_DISTANCE

    assert w_in.shape[2] + PAD_COLS == D_IN_PAD
    w_in_c = jnp.pad(w_in.astype(_MXU), ((0, 0), (0, 0), (0, PAD_COLS)))
    w_out_c, w_up_c, w_down_c = (w.astype(_MXU) for w in (w_out, w_up, w_down))
    wk1, wk2 = cmp_k_w1.astype(_MXU), cmp_k_w2.astype(_MXU)
    wv1, wv2t = cmp_v_w1.astype(_MXU), jnp.swapaxes(cmp_v_w2, 1, 2).astype(_MXU)

    table_t = jnp.pad(rel_bias.T, ((0, 0), (0, LANES - N_BUCKETS)))
    tiles = dict(n_heads=rel_bias.shape[1], head0=0, rows=t, cols=t, col_tile=t, row_stride=-1, col_stride=1)
    bias_diag = _bias_table(table_t, offset=0, **tiles)
    bias_sub = _bias_table(table_t, offset=t, **tiles)
    bias_cmp = _bias_table(table_t, n_heads=GROUP_HEADS, head0=GROUP_HEADS, rows=n_chunk, cols=s,
                           col_tile=t, row_stride=-CMP_STRIDE, col_stride=1, offset=-(CMP_LEN - 1))
    cover_t, expand = _nsa_constants(s)
    col_scale = np.ones((1, D_IN_PAD), np.float32)
    for cb, width in ((CB_SB_Q, HEAD_DIM), (CB_MB_Q, HEAD_DIM), (CB_NS_Q, HEAD_DIM), (CB_DF_Q, DIFF_HALF)):
        col_scale[:, cb * GROUP_WIDTH:(cb + 1) * GROUP_WIDTH] = width ** -0.5 * LOG2E
    col_scale = jnp.asarray(col_scale)

    x2 = x.reshape(bsz * s, d)
    for layer in range(depth):
        proj, kcv = _norm_matmul(x2, norm_attn[layer][None], w_in_c, layer, col_scale,
                                 gap=(COLS_BEFORE_PAD, COLS_BEFORE_PAD + PAD_COLS),
                                 side_col=CB_NS_A * GROUP_WIDTH, side_width=2 * HEAD_DIM)
        proj = proj.reshape(bsz, s, D_IN_PAD)
        kc, vc_t = _compress(kcv.reshape(bsz, s, 2 * HEAD_DIM), cmp_pos_k, cmp_pos_v, wk1, wk2, wv1, wv2t, layer)
        lambda_init = 0.8 - 0.6 * math.exp(-0.3 * layer)
        o_sb, o_df = _run_mixers(
            [_stick_breaking(proj),
             _diff(proj, diff_lambda[layer], diff_subln[layer][:, None], bias_diag, bias_sub, lambda_init)],
            bsz, s, "stick_breaking_diff")
        o_mb, o_ns = _run_mixers(
            [_moba(proj, bias_diag, bias_sub),
             _nsa(proj, kc, vc_t, bias_diag, bias_sub, bias_cmp, cover_t, expand)],
            bsz, s, "moba_nsa")
        groups = [o.reshape(bsz * s, GROUP_WIDTH) for o in (o_sb, o_mb, o_ns, o_df)]
        x2 = _out_mlp(x2, groups, w_out_c, norm_mlp[layer][None], w_up_c, w_down_c, layer,
                      final_norm[None], final_norm=(layer == depth - 1))
    return x2.reshape(bsz, s, d)
```

```python
import functools
import math
from typing import Any, Callable, NamedTuple, Optional, Sequence

import numpy as np
import jax
import jax.numpy as jnp
from jax import lax
from jax.experimental import pallas as pl
from jax.experimental.pallas import tpu as pltpu

HEAD_DIM = 64
GROUP_HEADS = 4
GROUP_WIDTH = GROUP_HEADS * HEAD_DIM
NORM_EPS = 1e-6
NEG_INF = -1e30
BIG = 1e30
FORCE = 1e30
TINY = 1e-30
SOFTPLUS_CLAMP = 64.0
PICKED = -3e38
LOG2E = math.log2(math.e)
N_BUCKETS = 32
MAX_DISTANCE = 128
MOBA_BLOCK = 256
MOBA_TOPK = 3
CMP_LEN = 32
CMP_STRIDE = 16
SLC_LEN = 64
SLC_TOPN = 4
WINDOW = 512
DIFF_HALF = HEAD_DIM // 2
LANES = 128
SUBLANES = 8
TILE = 256
BF16_ROWS = 16
ACC_ROWS = HEAD_DIM + BF16_ROWS
N_GATES = 3 * GROUP_HEADS
COLS_BEFORE_PAD = 9 * GROUP_WIDTH - 2 * HEAD_DIM + N_GATES
PAD_COLS = 2 * HEAD_DIM - N_GATES
CB_SB_Q, CB_SB_K, CB_SB_V, CB_MB_Q, CB_MB_K, CB_MB_V, CB_NS_Q, CB_NS_A, CB_NS_B, CB_DF_Q, CB_DF_K, CB_DF_V = range(12)
D_IN_PAD = 12 * GROUP_WIDTH

_MXU = jnp.bfloat16
_VMEM_LIMIT = 56 * 1024 * 1024
_HEADS = range(GROUP_HEADS)
Q_TILES = 8
LOOKAHEAD = 2


def _dot(a, b, precision=None):
    return jnp.dot(a, b, precision=precision, preferred_element_type=jnp.float32)


def _dot_nt(a, b, precision=None):
    return lax.dot_general(a, b, (((1,), (1,)), ((), ())), precision=precision,
                           preferred_element_type=jnp.float32)


def _rms(x, g):
    return x * lax.rsqrt(jnp.mean(x * x, axis=-1, keepdims=True) + NORM_EPS) * g


def _params(*sem):
    return pltpu.CompilerParams(dimension_semantics=sem, vmem_limit_bytes=_VMEM_LIMIT)


def _norm_matmul_kernel(x_ref, g_ref, w_ref, scale_ref, o_ref, side_ref, wp_ref, *, tn, side_col, gap):
    @pl.when(pl.program_id(0) == 0)
    def _():
        lo, hi = gap
        wp_ref[:, :lo] = w_ref[:, :lo]
        wp_ref[:, lo:hi] = jnp.zeros((wp_ref.shape[0], hi - lo), wp_ref.dtype)
        wp_ref[:, hi:] = w_ref[:, lo:lo + wp_ref.shape[1] - hi]

    h = _rms(x_ref[...], g_ref[...]).astype(_MXU)
    for j in range(wp_ref.shape[1] // tn):
        cols = slice(j * tn, (j + 1) * tn)
        acc = _dot(h, wp_ref[:, cols])
        o_ref[:, cols] = (acc * scale_ref[:, cols]).astype(o_ref.dtype)
        if j * tn <= side_col < (j + 1) * tn:
            side_ref[...] = acc[:, side_col - j * tn:side_col - j * tn + side_ref.shape[1]]


def _norm_matmul(x, g, w, layer, col_scale, *, gap, side_col, side_width, tm=512, tn=1024):
    m, d = x.shape
    n = w.shape[2]
    assert side_col % LANES == 0 and side_col // tn == (side_col + side_width - 1) // tn
    return pl.pallas_call(
        functools.partial(_norm_matmul_kernel, tn=tn, side_col=side_col, gap=gap),
        grid=(m // tm,),
        in_specs=[pl.BlockSpec((tm, d), lambda i: (i, 0)),
                  pl.BlockSpec((1, d), lambda i: (0, 0)),
                  pl.BlockSpec((None, d, w.shape[2]), lambda i: (layer, 0, 0), pipeline_mode=pl.Buffered(1)),
                  pl.BlockSpec((1, n), lambda i: (0, 0))],
        out_specs=[pl.BlockSpec((tm, n), lambda i: (i, 0)),
                   pl.BlockSpec((tm, side_width), lambda i: (i, 0))],
        out_shape=[jax.ShapeDtypeStruct((m, n), _MXU),
                   jax.ShapeDtypeStruct((m, side_width), jnp.float32)],
        scratch_shapes=[pltpu.VMEM((d, n), _MXU)],
        compiler_params=_params("arbitrary"),
        name="norm_in_proj",
    )(x, g, w, col_scale)


def _out_mlp_kernel(x_ref, a_ref, b_ref, c_ref, d_ref, wo_ref, g_ref, wu_ref, wd_ref, gf_ref, o_ref,
                    *, final_norm, tf):
    mixed = jnp.concatenate([a_ref[...], b_ref[...], c_ref[...], d_ref[...]], axis=1)
    y = x_ref[...] + _dot(mixed, wo_ref[...])
    h = _rms(y, g_ref[...]).astype(_MXU)
    for c in range(wu_ref.shape[1] // tf):
        u = jnp.square(jnp.maximum(_dot(h, wu_ref[:, c * tf:(c + 1) * tf]), 0.0))
        y = y + _dot(u.astype(_MXU), wd_ref[c * tf:(c + 1) * tf, :])
    if final_norm:
        y = _rms(y, gf_ref[...])
    o_ref[...] = y


def _out_mlp(x, groups, w_out, g, w_up, w_down, layer, g_final, *, final_norm, tm=512, tf=2048):
    m, d = x.shape
    f = w_up.shape[2]
    gspec = pl.BlockSpec((tm, GROUP_WIDTH), lambda i: (i, 0))
    row = pl.BlockSpec((1, d), lambda i: (0, 0))

    def resident(rows, cols):
        return pl.BlockSpec((None, rows, cols), lambda i: (layer, 0, 0), pipeline_mode=pl.Buffered(1))

    return pl.pallas_call(
        functools.partial(_out_mlp_kernel, final_norm=final_norm, tf=tf),
        grid=(m // tm,),
        in_specs=[pl.BlockSpec((tm, d), lambda i: (i, 0)), gspec, gspec, gspec, gspec,
                  resident(d, d), row, resident(d, f), resident(f, d), row],
        out_specs=pl.BlockSpec((tm, d), lambda i: (i, 0)),
        out_shape=jax.ShapeDtypeStruct((m, d), jnp.float32),
        compiler_params=_params("parallel"),
        name="out_proj_mlp_residual",
    )(x, *groups, w_out, g, w_up, w_down, g_final)


def _t5_bucket(dist):
    n = jnp.maximum(dist, 0)
    max_exact = N_BUCKETS // 2
    nf = jnp.maximum(n, 1).astype(jnp.float32)
    large = max_exact + (jnp.log(nf / max_exact) / math.log(MAX_DISTANCE / max_exact)
                         * (N_BUCKETS - max_exact)).astype(jnp.int32)
    large = jnp.minimum(large, N_BUCKETS - 1)
    return jnp.where(n < max_exact, n, large)


def _bias_kernel(tab_ref, o_ref, *, row_stride, col_stride, offset, head0):
    nh, tr, tc = o_ref.shape
    for blk in range(tc // LANES):
        rows = lax.broadcasted_iota(jnp.int32, (tr, LANES), 0)
        cols = lax.broadcasted_iota(jnp.int32, (tr, LANES), 1) + (pl.program_id(0) * tc + blk * LANES)
        bucket = _t5_bucket(rows * row_stride + cols * col_stride + offset)
        for h in range(nh):
            row = tab_ref[head0 + h:head0 + h + 1, :]
            row = (row - row[:, N_BUCKETS - 1:N_BUCKETS]) * LOG2E
            o_ref[h, :, blk * LANES:(blk + 1) * LANES] = jnp.take_along_axis(
                jnp.broadcast_to(row, (tr, LANES)), bucket, axis=1, mode="promise_in_bounds")


def _bias_table(table_t, *, n_heads, head0, rows, cols, col_tile, row_stride, col_stride, offset):
    return pl.pallas_call(
        functools.partial(_bias_kernel, row_stride=row_stride, col_stride=col_stride,
                          offset=offset, head0=head0),
        grid=(cols // col_tile,),
        in_specs=[pl.BlockSpec(table_t.shape, lambda i: (0, 0))],
        out_specs=pl.BlockSpec((n_heads, rows, col_tile), lambda i: (0, 0, i)),
        out_shape=jax.ShapeDtypeStruct((n_heads, rows, cols), jnp.float32),
        compiler_params=_params("parallel"),
        name="t5_bias_tiles",
    )(table_t)


def _softmax_init(t):
    return (jnp.full((1, t), NEG_INF, jnp.float32), jnp.zeros((ACC_ROWS, t), jnp.float32))


class _KeyTile(NamedTuple):
    kts: Sequence[Any]
    vts: Sequence[Any]
    biases: Optional[Sequence[Any]] = None
    emasks: Optional[Sequence[Any]] = None
    qmasks: Optional[Sequence[Any]] = None


def _round_robin(lists):
    out = []
    for rank in range(max(map(len, lists), default=0)):
        out.extend(items[rank] for items in lists if rank < len(items))
    return out


def _softmax_jobs(t, jobs):
    built = {}

    def tiles_of(job, g):
        if (job, g) not in built:
            built[job, g] = [make() for make in jobs[job][2][g]]
        return built[job, g]

    def scores_of(job, g, c):
        qts = jobs[job][1]
        row = []
        for tile in tiles_of(job, g):
            s = _dot(tile.kts[c], qts[c])
            if tile.biases is not None:
                s = s + tile.biases[c]
            if tile.emasks is not None:
                s = jnp.where(tile.emasks[c], s, NEG_INF)
            row.append(s.astype(_MXU))
        return row

    def update(state, job, g, c, scores):
        m, acc = state
        m_new = m
        for tile, s in zip(tiles_of(job, g), scores):
            tile_max = jnp.max(s, axis=0, keepdims=True).astype(jnp.float32)
            if tile.qmasks is not None:
                tile_max = jnp.where(tile.qmasks[c], tile_max, NEG_INF)
            m_new = jnp.maximum(m_new, tile_max)
        seen = m_new > 0.5 * NEG_INF
        acc = jnp.exp2(m - m_new) * acc
        for tile, s in zip(tiles_of(job, g), scores):
            ok = seen if tile.qmasks is None else jnp.logical_and(seen, tile.qmasks[c])
            acc = acc + _dot(tile.vts[c], jnp.exp2(s - jnp.where(ok, m_new, BIG).astype(_MXU)))
        return m_new, acc

    units = _round_robin([[(job, g, c) for g in range(len(groups)) for c in range(n)]
                          for job, (n, _, groups) in enumerate(jobs)])
    lookahead = max(LOOKAHEAD, len(jobs))
    states = [[_softmax_init(t) for _ in range(n)] for n, _, _ in jobs]
    pending = {k: scores_of(*units[k]) for k in range(min(lookahead, len(units)))}
    for k, (job, g, c) in enumerate(units):
        if k + lookahead < len(units):
            pending[k + lookahead] = scores_of(*units[k + lookahead])
        states[job][c] = update(states[job][c], job, g, c, pending.pop(k))
        yield
    return states


def _pairs(items):
    return [items[p:p + 2] for p in range(0, len(items), 2)]


class _Mixer(NamedTuple):
    emit: Callable[..., Any]
    in_specs: Sequence[Any]
    operands: Sequence[Any]
    scratch: Sequence[Any]
    stride: int


def _fused_mixers_kernel(*refs, mixers):
    n_in = sum(len(m.in_specs) for m in mixers)
    in_refs, out_refs, scratch_refs = refs[:n_in], refs[n_in:n_in + len(mixers)], refs[n_in + len(mixers):]
    active = []
    for m, o_ref in zip(mixers, out_refs):
        ins, in_refs = in_refs[:len(m.in_specs)], in_refs[len(m.in_specs):]
        scr, scratch_refs = scratch_refs[:len(m.scratch)], scratch_refs[len(m.scratch):]
        active.append((m.emit(*ins, o_ref, *scr), m.stride))
    while active:
        for item in list(active):
            emitter, stride = item
            for _ in range(stride):
                if next(emitter, StopIteration) is StopIteration:
                    active.remove(item)
                    break


def _run_mixers(mixers, bsz, s, name):
    return pl.pallas_call(
        functools.partial(_fused_mixers_kernel, mixers=tuple(mixers)),
        grid=(bsz,),
        in_specs=[spec for m in mixers for spec in m.in_specs],
        out_specs=[pl.BlockSpec((1, s, GROUP_WIDTH), lambda b: (b, 0, 0))] * len(mixers),
        out_shape=[jax.ShapeDtypeStruct((bsz, s, GROUP_WIDTH), _MXU)] * len(mixers),
        scratch_shapes=[shape for m in mixers for shape in m.scratch],
        compiler_params=_params("parallel"),
        name=name,
    )(*[op for m in mixers for op in m.operands])


def _softmax_out(state):
    acc = state[1]
    return acc[:HEAD_DIM] / jnp.maximum(acc[HEAD_DIM:HEAD_DIM + 1], TINY)


def _top_k_rows(score, row_f, k):
    sel = jnp.zeros(score.shape, jnp.float32)
    for _ in range(k):
        mx = jnp.max(score, axis=0, keepdims=True)
        idx = jnp.min(jnp.where(score == mx, row_f, float(score.shape[0])), axis=0, keepdims=True)
        pick = row_f == idx
        sel = jnp.where(pick, 1.0, sel)
        score = jnp.where(pick, PICKED, score)
    return sel


def _now(fn):
    fn()


def _tile_iotas(t):
    return (lax.broadcasted_iota(jnp.int32, (t, t), 0), lax.broadcasted_iota(jnp.int32, (t, t), 1))


def _key_rows(ref, j, t):
    if isinstance(j, int):
        return ref[j * t:(j + 1) * t, :]
    return ref[pl.ds(pl.multiple_of(j * t, t), t), :]


def _transposed(ref_block):
    return ref_block.astype(jnp.float32).T


def _queries_t(q_ref, width, t):
    qt = _transposed(q_ref[0]).astype(_MXU)
    return [[qt[c * width:(c + 1) * width, r * t:(r + 1) * t] for c in range(GROUP_WIDTH // width)]
            for r in range(Q_TILES)]


def _fill_values_t(vt_ref, v_ref, col0, t):
    n_tiles, rows, _ = vt_ref.shape
    lane_block = (col0 // LANES) * LANES
    for c in range(n_tiles):
        blk = _transposed(v_ref[0, c * t:(c + 1) * t, lane_block:lane_block + LANES])
        vt_ref[c, 0:HEAD_DIM, :] = blk[col0 - lane_block:col0 - lane_block + HEAD_DIM].astype(vt_ref.dtype)
        if rows == ACC_ROWS:
            first = lax.broadcasted_iota(jnp.int32, (rows - HEAD_DIM, t), 0) == 0
            vt_ref[c, HEAD_DIM:rows, :] = jnp.where(first, 1.0, 0.0).astype(vt_ref.dtype)


def _group_specs(s, *column_blocks):
    return [pl.BlockSpec((1, s, GROUP_WIDTH), functools.partial(lambda cb, b: (b, 0, cb), cb))
            for cb in column_blocks]


def _bias_specs(t, head_group):
    spec = pl.BlockSpec((GROUP_HEADS, t, t), lambda b: (head_group, 0, 0))
    return [spec, spec]


def _store_heads(o_ref, outs_t):
    tiles = [jnp.concatenate(heads, axis=0) for heads in outs_t]
    o_ref[0] = jnp.concatenate(tiles, axis=1).T.astype(o_ref.dtype)


def _sb_emit(q_ref, k_ref, v_ref, o_ref, kb_ref, vt_ref):
    t = q_ref.shape[1] // Q_TILES
    key, qry = _tile_iotas(t)
    strict = key < qry
    later = jnp.where(qry > key, 1.0, 0.0).astype(_MXU)

    @_now
    def _():
        for h in _HEADS:
            kb_ref[h] = k_ref[0, :, h * HEAD_DIM:(h + 1) * HEAD_DIM].astype(kb_ref.dtype)
            _fill_values_t(vt_ref.at[h], v_ref, h * HEAD_DIM, t)

    def query_step(step):
        qts = _queries_t(q_ref, HEAD_DIM, t)
        units = _round_robin([[(r, step * Q_TILES + r, pair, h)
                               for pair in _pairs(list(range(step * Q_TILES + r, -1, -1))) for h in _HEADS]
                              for r in range(Q_TILES)])
        zero = (jnp.zeros((HEAD_DIM, t), jnp.float32), jnp.zeros((1, t), jnp.float32))
        carry = [[zero] * GROUP_HEADS for _ in range(Q_TILES)]
        zs, log_keeps, suffixes = {}, {}, {}

        def scores(u):
            r, _, pair, h = units[u]
            zs[u] = [_dot(_key_rows(kb_ref.at[h], j, t), qts[r][h]) for j in pair]

        def keeps(u):
            _, k, pair, _ = units[u]
            log_keeps[u], suffixes[u] = [], []
            for j, z in zip(pair, zs[u]):
                drop = jnp.maximum(jnp.log2(1.0 + jnp.exp2(jnp.minimum(z, SOFTPLUS_CLAMP))), z)
                if j == k:
                    drop = jnp.where(strict, drop, 0.0)
                log_keeps[u].append(drop)
                suffixes[u].append(_dot(later, drop.astype(_MXU)))

        def values(u):
            r, k, pair, h = units[u]
            acc, run = carry[r][h]
            weights = []
            for j, z, drop, suffix in zip(pair, zs.pop(u), log_keeps.pop(u), suffixes.pop(u)):
                a = jnp.exp2(z - drop - suffix + run)
                if j == k:
                    a = jnp.where(strict, a, 0.0)
                weights.append(a.astype(_MXU))
                run = run - (suffix[0:1] + drop[0:1])
            for j, w in zip(pair, weights):
                acc = acc + _dot(vt_ref[h, j], w)
            carry[r][h] = (acc, run)

        stages = (scores, keeps, values)
        for tick in range(len(units) + len(stages) - 1):
            for lag, stage in enumerate(stages):
                if 0 <= tick - lag < len(units):
                    stage(tick - lag)
            yield
        _store_heads(o_ref, [[c[0] for c in tile_carry] for tile_carry in carry])

    yield from query_step(0)


def _stick_breaking(proj, t=TILE):
    s = proj.shape[1]
    scratch = [pltpu.VMEM((GROUP_HEADS, s, HEAD_DIM), _MXU), pltpu.VMEM((GROUP_HEADS, s // t, HEAD_DIM, t), _MXU)]
    return _Mixer(_sb_emit, _group_specs(s, CB_SB_Q, CB_SB_K, CB_SB_V), [proj] * 3, scratch, stride=1)


def _moba_emit(q_ref, k_ref, v_ref, bd_ref, bs_ref, o_ref, kb_ref, vt_ref, km_ref):
    t = q_ref.shape[1] // Q_TILES
    n_blk = k_ref.shape[1] // MOBA_BLOCK
    tiles_per_blk = MOBA_BLOCK // t
    blk_shift = int(math.log2(tiles_per_blk))

    @_now
    def _():
        km_ref[...] = jnp.zeros_like(km_ref)
        for h in _HEADS:
            lo, hi = h * HEAD_DIM, (h + 1) * HEAD_DIM
            kb_ref[h] = k_ref[0, :, lo:hi].astype(kb_ref.dtype)
            _fill_values_t(vt_ref.at[h], v_ref, lo, t)
            for n in range(n_blk):
                blk = k_ref[0, n * MOBA_BLOCK:(n + 1) * MOBA_BLOCK, lo:hi]
                km_ref[h, n:n + 1, :] = jnp.mean(blk.astype(jnp.float32), axis=0, keepdims=True)

    def query_step(step):
        key, qry = _tile_iotas(t)
        causal = key <= qry
        blk_row = lax.broadcasted_iota(jnp.int32, (km_ref.shape[1], t), 0)
        qt = _transposed(q_ref[0])
        jobs = []
        for r in range(Q_TILES):
            k = step * Q_TILES + r
            own = k >> blk_shift
            qts, sels = [], []
            for h in _HEADS:
                qf = qt[h * HEAD_DIM:(h + 1) * HEAD_DIM, r * t:(r + 1) * t]
                qts.append(qf.astype(_MXU))
                gate = _dot(km_ref[h], qf, precision=lax.Precision.HIGHEST)
                gate = jnp.where(blk_row < own, gate, NEG_INF)
                sel = _top_k_rows(gate, blk_row.astype(jnp.float32), min(MOBA_TOPK, n_blk - 1))
                sels.append(jnp.where(blk_row < own, sel, 0.0))

            def tile(own, sels, j, bias_ref=None, emask=None):
                n = j >> blk_shift
                return _KeyTile([_key_rows(kb_ref.at[h], j, t) for h in _HEADS], [vt_ref[h, j] for h in _HEADS],
                                None if bias_ref is None else [bias_ref[h] for h in _HEADS],
                                None if emask is None else [emask] * GROUP_HEADS,
                                None if n == own else [sels[h][n:n + 1] > 0.5 for h in _HEADS])

            tile = functools.partial(tile, own, sels)
            groups = _pairs([functools.partial(tile, j) for j in range(k - 1)])
            groups.append(([functools.partial(tile, k - 1, bs_ref)] if k else [])
                          + [functools.partial(tile, k, bd_ref, causal)])
            jobs.append((GROUP_HEADS, qts, groups))
        all_states = yield from _softmax_jobs(t, jobs)
        _store_heads(o_ref, [[_softmax_out(st) for st in states] for states in all_states])

    yield from query_step(0)


def _kv_scratch(s, t, key_dim=HEAD_DIM, n_keys=GROUP_HEADS):
    return [pltpu.VMEM((n_keys, s, key_dim), _MXU), pltpu.VMEM((GROUP_HEADS, s // t, ACC_ROWS, t), _MXU)]


def _moba(proj, bias_diag, bias_sub, t=TILE):
    s = proj.shape[1]
    n_blk_pad = -(-(s // MOBA_BLOCK) // SUBLANES) * SUBLANES
    scratch = _kv_scratch(s, t) + [pltpu.VMEM((GROUP_HEADS, n_blk_pad, HEAD_DIM), jnp.float32)]
    return _Mixer(_moba_emit, _group_specs(s, CB_MB_Q, CB_MB_K, CB_MB_V) + _bias_specs(t, 0),
                  [proj] * 3 + [bias_diag, bias_sub], scratch, stride=2)


def _diff_emit(lam_ref, g_ref, q_ref, k_ref, v_ref, bd_ref, bs_ref, o_ref, kb_ref, vt_ref, *, lambda_init):
    t = q_ref.shape[1] // Q_TILES
    key, qry = _tile_iotas(t)
    causal = key <= qry
    lv = lam_ref[...]
    lam = (jnp.exp(jnp.sum(lv[0:1] * lv[1:2], keepdims=True))
           - jnp.exp(jnp.sum(lv[2:3] * lv[3:4], keepdims=True)) + lambda_init)
    halves = range(2 * GROUP_HEADS)

    @_now
    def _():
        for c in halves:
            kb_ref[c] = k_ref[0, :, c * DIFF_HALF:(c + 1) * DIFF_HALF].astype(kb_ref.dtype)
        for h in _HEADS:
            _fill_values_t(vt_ref.at[h], v_ref, h * HEAD_DIM, t)

    def tile(j, bias_ref=None, emask=None):
        n = len(halves)
        return _KeyTile([_key_rows(kb_ref.at[c], j, t) for c in halves], [vt_ref[c // 2, j] for c in halves],
                        None if bias_ref is None else [bias_ref[c // 2] for c in halves],
                        None if emask is None else [emask] * n)

    def query_step(step):
        qts = _queries_t(q_ref, DIFF_HALF, t)
        jobs = []
        for r in range(Q_TILES):
            k = step * Q_TILES + r
            groups = _pairs([functools.partial(tile, j) for j in range(k - 1)])
            groups.append(([functools.partial(tile, k - 1, bs_ref)] if k else [])
                          + [functools.partial(tile, k, bd_ref, causal)])
            jobs.append((len(halves), qts[r], groups))
        all_states = yield from _softmax_jobs(t, jobs)
        outs = []
        for states in all_states:
            heads = []
            for h in _HEADS:
                o = _softmax_out(states[2 * h]) - lam * _softmax_out(states[2 * h + 1])
                o = o * lax.rsqrt(jnp.mean(o * o, axis=0, keepdims=True) + NORM_EPS) * g_ref[...]
                heads.append(o * (1.0 - lambda_init))
            outs.append(heads)
        _store_heads(o_ref, outs)

    yield from query_step(0)


def _diff(proj, lam_params, subln, bias_diag, bias_sub, lambda_init, t=TILE):
    s = proj.shape[1]
    in_specs = ([pl.BlockSpec(lam_params.shape, lambda b: (0, 0)), pl.BlockSpec(subln.shape, lambda b: (0, 0))]
                + _group_specs(s, CB_DF_Q, CB_DF_K, CB_DF_V) + _bias_specs(t, 2))
    return _Mixer(functools.partial(_diff_emit, lambda_init=lambda_init), in_specs,
                  [lam_params, subln] + [proj] * 3 + [bias_diag, bias_sub],
                  _kv_scratch(s, t, key_dim=DIFF_HALF, n_keys=2 * GROUP_HEADS), stride=2)


def _compress_kernel(kcv_ref, pk_ref, pv_ref, wk1_ref, wk2_ref, wv1_ref, wv2t_ref, kc_ref, vct_ref):
    n_chunk = kc_ref.shape[1]

    branches = ((0, pk_ref, wk1_ref), (HEAD_DIM, pv_ref, wv1_ref))
    tops = [jnp.zeros((n_chunk, w1_ref.shape[1]), jnp.float32) for _, _, w1_ref in branches]
    bots = list(tops)
    pack = 2 * LANES // HEAD_DIM
    for l0 in range(0, CMP_STRIDE, pack):
        tokens = [kcv_ref[0, pl.ds(l, n_chunk, stride=CMP_STRIDE), :] for l in range(l0, l0 + pack)]
        for n, (col0, p_ref, w1_ref) in enumerate(branches):
            for half, acc in ((0, tops), (CMP_STRIDE, bots)):
                x = jnp.concatenate([(tok[:, col0:col0 + HEAD_DIM] + p_ref[half + l:half + l + 1, :]).astype(_MXU)
                                     for l, tok in zip(range(l0, l0 + pack), tokens)], axis=1)
                acc[n] = acc[n] + _dot(x, w1_ref[(half + l0) * HEAD_DIM:(half + l0 + pack) * HEAD_DIM, :])
    hidden = [jax.nn.gelu(top + pltpu.roll(bot, n_chunk - 1, axis=0)).astype(_MXU) for top, bot in zip(tops, bots)]
    kc_ref[0] = _dot(hidden[0], wk2_ref[...])
    vct_ref[0] = _dot_nt(wv2t_ref[...], hidden[1])


def _compress(kcv, pos_k, pos_v, wk1, wk2, wv1, wv2t, layer):
    bsz, s, width = kcv.shape
    n_chunk = s // CMP_STRIDE

    def full(a):
        return pl.BlockSpec((None,) + a.shape[1:], lambda b: (layer, 0, 0))

    return pl.pallas_call(
        _compress_kernel,
        grid=(bsz,),
        in_specs=[pl.BlockSpec((1, s, width), lambda b: (b, 0, 0)),
                  full(pos_k), full(pos_v), full(wk1), full(wk2), full(wv1), full(wv2t)],
        out_specs=[pl.BlockSpec((1, n_chunk, HEAD_DIM), lambda b: (b, 0, 0)),
                   pl.BlockSpec((1, HEAD_DIM, n_chunk), lambda b: (b, 0, 0))],
        out_shape=[jax.ShapeDtypeStruct((bsz, n_chunk, HEAD_DIM), jnp.float32),
                   jax.ShapeDtypeStruct((bsz, HEAD_DIM, n_chunk), jnp.float32)],
        compiler_params=_params("parallel"),
        name="nsa_compress",
    )(kcv, pos_k, pos_v, wk1, wk2, wv1, wv2t)


def _nsa_emit(q_ref, kva_ref, kvb_ref, kc_ref, vct_ref, bd_ref, bs_ref, bc_ref, cover_ref, e_ref,
                o_ref, ks_ref, vst_ref, kw_ref, vwt_ref):
    t = q_ref.shape[1] // Q_TILES
    key, qry = _tile_iotas(t)
    causal = key <= qry
    ks_col, vs_col, kw_col, vw_col, gate_col = 2 * HEAD_DIM, 3 * HEAD_DIM, 0, HEAD_DIM, 2 * HEAD_DIM

    @_now
    def _():
        ks_ref[:, :HEAD_DIM] = kva_ref[0, :, ks_col:ks_col + HEAD_DIM].astype(ks_ref.dtype)
        ks_ref[:, HEAD_DIM:] = e_ref[...]
        kw_ref[...] = kvb_ref[0, :, kw_col:kw_col + HEAD_DIM].astype(kw_ref.dtype)
        _fill_values_t(vst_ref, kva_ref, vs_col, t)
        _fill_values_t(vwt_ref, kvb_ref, vw_col, t)

    def tile(k_ref, vt_ref, j, bias_ref=None, emask=None):
        n = GROUP_HEADS
        return _KeyTile([_key_rows(k_ref, j, t)] * n, [vt_ref[j]] * n,
                        None if bias_ref is None else [bias_ref[h] for h in _HEADS],
                        None if emask is None else [emask] * n)

    def query_step(step):
        qts = _queries_t(q_ref, HEAD_DIM, t)
        n_cmp = kc_ref.shape[1]
        n_slc = cover_ref.shape[0]
        kc = kc_ref[0].astype(_MXU)
        vct = vct_ref[0].astype(_MXU)
        c_row = lax.broadcasted_iota(jnp.int32, (n_cmp, t), 0)
        c_col = lax.broadcasted_iota(jnp.int32, (n_cmp, t), 1)
        s_row = lax.broadcasted_iota(jnp.int32, (n_slc, t), 0)
        s_col = lax.broadcasted_iota(jnp.int32, (n_slc, t), 1)
        tiles = [step * Q_TILES + r for r in range(Q_TILES)]

        o_cmp, importance = [], []
        for r, k in enumerate(tiles):
            visible = c_col + k * t >= c_row * CMP_STRIDE + (CMP_LEN - 1)
            cmp_scores = [_dot(kc, qts[r][h]) for h in _HEADS]
            cmp_probs = []
            p_sum = jnp.zeros((n_cmp, t), jnp.float32)
            for h in _HEADS:
                sc = jnp.where(visible, cmp_scores[h] + bc_ref[h, :, r * t:(r + 1) * t], NEG_INF)
                e = jnp.where(visible, jnp.exp2(sc - jnp.max(sc, axis=0, keepdims=True)), 0.0)
                p = e / jnp.maximum(jnp.sum(e, axis=0, keepdims=True), TINY)
                cmp_probs.append(p.astype(_MXU))
                p_sum = p_sum + p
            o_cmp.append([_dot(vct, cmp_probs[h]) for h in _HEADS])
            importance.append(_dot(cover_ref[...], p_sum, precision=lax.Precision.HIGHEST))

        n_back = WINDOW // t
        jobs = []
        for r, k in enumerate(tiles):
            window = [functools.partial(tile, kw_ref, vwt_ref, k - n_back, None, qry < key)] if k >= n_back else []
            for back in range(min(n_back - 1, k), 0, -1):
                window.append(functools.partial(tile, kw_ref, vwt_ref, k - back, bs_ref if back == 1 else None))
            window.append(functools.partial(tile, kw_ref, vwt_ref, k, bd_ref, causal))
            jobs.append((GROUP_HEADS, qts[r], [window]))
        win_states = yield from _softmax_jobs(t, jobs)
        o_win = [[_softmax_out(st) for st in states] for states in win_states]

        jobs = []
        for r, k in enumerate(tiles):
            own = jnp.right_shift(s_col + k * t, int(math.log2(SLC_LEN)))
            score = jnp.where(s_row == own, FORCE, jnp.where(s_row < own, importance[r], NEG_INF))
            sel = _top_k_rows(score, s_row.astype(jnp.float32), min(SLC_TOPN, n_slc))
            penalty = jnp.where(sel > 0.5, 0.0, NEG_INF).astype(_MXU)
            q_aug = [jnp.concatenate([qts[r][h], penalty], axis=0) for h in _HEADS]
            groups = _pairs([functools.partial(tile, ks_ref, vst_ref, j) for j in range(k - 1)])
            groups.append(([functools.partial(tile, ks_ref, vst_ref, k - 1, bs_ref)] if k else [])
                          + [functools.partial(tile, ks_ref, vst_ref, k, bd_ref, causal)])
            jobs.append((GROUP_HEADS, q_aug, groups))
        slc_states = yield from _softmax_jobs(t, jobs)
        o_slc = [[_softmax_out(st) for st in states] for states in slc_states]

        gates = _transposed(kvb_ref[0, :, (gate_col // LANES) * LANES:(gate_col // LANES + 1) * LANES])
        gates = 1.0 / (1.0 + jnp.exp(-gates[gate_col % LANES:gate_col % LANES + N_GATES + 4]))
        outs = []
        for r in range(Q_TILES):
            heads = []
            for h in _HEADS:
                g = [gates[br * GROUP_HEADS + h:br * GROUP_HEADS + h + 1, r * t:(r + 1) * t] for br in range(3)]
                heads.append(g[0] * o_cmp[r][h] + g[1] * o_slc[r][h] + g[2] * o_win[r][h])
            outs.append(heads)
        _store_heads(o_ref, outs)

    yield from query_step(0)


def _nsa(proj, kc, vc_t, bias_diag, bias_sub, bias_cmp, cover_t, expand, t=TILE):
    s = proj.shape[1]
    n_cmp = kc.shape[1]
    values_t = pltpu.VMEM((s // t, ACC_ROWS, t), _MXU)
    in_specs = (_group_specs(s, CB_NS_Q, CB_NS_A, CB_NS_B)
                + [pl.BlockSpec((1, n_cmp, HEAD_DIM), lambda b: (b, 0, 0)),
                   pl.BlockSpec((1, HEAD_DIM, n_cmp), lambda b: (b, 0, 0))]
                + _bias_specs(t, 1)
                + [pl.BlockSpec(bias_cmp.shape, lambda b: (0, 0, 0)),
                   pl.BlockSpec(cover_t.shape, lambda b: (0, 0)),
                   pl.BlockSpec(expand.shape, lambda b: (0, 0))])
    scratch = [pltpu.VMEM((s, HEAD_DIM + expand.shape[1]), _MXU), values_t, pltpu.VMEM((s, HEAD_DIM), _MXU), values_t]
    return _Mixer(_nsa_emit, in_specs, [proj] * 3 + [kc, vc_t, bias_diag, bias_sub, bias_cmp, cover_t, expand],
                  scratch, stride=3)


def _nsa_constants(s):
    n_cmp = (s - CMP_LEN) // CMP_STRIDE + 1
    n_slc = s // SLC_LEN
    assert n_cmp + 1 == s // CMP_STRIDE and n_slc % SUBLANES == 0
    c_start = np.arange(n_cmp) * CMP_STRIDE
    s_start = np.arange(n_slc) * SLC_LEN
    cover = np.clip(np.minimum((c_start + CMP_LEN - 1)[:, None], (s_start + SLC_LEN - 1)[None, :])
                    - np.maximum(c_start[:, None], s_start[None, :]) + 1, 0, None) / CMP_LEN
    cover_t = np.zeros((n_slc, n_cmp + 1), np.float32)
    cover_t[:, :n_cmp] = cover.T
    expand = (np.arange(s)[:, None] // SLC_LEN == np.arange(n_slc)[None, :]).astype(np.float32)
    return jnp.asarray(cover_t), jnp.asarray(expand, _MXU)


def kernel(x, w_in, w_out, w_up, w_down, norm_attn, norm_mlp, cmp_pos_k, cmp_pos_v, cmp_k_w1, cmp_k_w2,
           cmp_v_w1, cmp_v_w2, diff_lambda, diff_subln, rel_bias, final_norm):
    bsz, s, d = x.shape
    depth = w_in.shape[0]
    t = TILE
    n_chunk = s // CMP_STRIDE
    assert s == Q_TILES * t and MOBA_BLOCK % t == 0 and WINDOW % t == 0 and t >= MAX_DISTANCE

    assert w_in.shape[2] + PAD_COLS == D_IN_PAD
    w_in_c = jnp.pad(w_in.astype(_MXU), ((0, 0), (0, 0), (0, PAD_COLS)))
    w_out_c, w_up_c, w_down_c = (w.astype(_MXU) for w in (w_out, w_up, w_down))
    wk1, wk2 = cmp_k_w1.astype(_MXU), cmp_k_w2.astype(_MXU)
    wv1, wv2t = cmp_v_w1.astype(_MXU), jnp.swapaxes(cmp_v_w2, 1, 2).astype(_MXU)

    table_t = jnp.pad(rel_bias.T, ((0, 0), (0, LANES - N_BUCKETS)))
    tiles = dict(n_heads=rel_bias.shape[1], head0=0, rows=t, cols=t, col_tile=t, row_stride=-1, col_stride=1)
    bias_diag = _bias_table(table_t, offset=0, **tiles)
    bias_sub = _bias_table(table_t, offset=t, **tiles)
    bias_cmp = _bias_table(table_t, n_heads=GROUP_HEADS, head0=GROUP_HEADS, rows=n_chunk, cols=s,
                           col_tile=t, row_stride=-CMP_STRIDE, col_stride=1, offset=-(CMP_LEN - 1))
    cover_t, expand = _nsa_constants(s)
    col_scale = np.ones((1, D_IN_PAD), np.float32)
    for cb, width in ((CB_SB_Q, HEAD_DIM), (CB_MB_Q, HEAD_DIM), (CB_NS_Q, HEAD_DIM), (CB_DF_Q, DIFF_HALF)):
        col_scale[:, cb * GROUP_WIDTH:(cb + 1) * GROUP_WIDTH] = width ** -0.5 * LOG2E
    col_scale = jnp.asarray(col_scale)

    x2 = x.reshape(bsz * s, d)
    for layer in range(depth):
        proj, kcv = _norm_matmul(x2, norm_attn[layer][None], w_in_c, layer, col_scale,
                                 gap=(COLS_BEFORE_PAD, COLS_BEFORE_PAD + PAD_COLS),
                                 side_col=CB_NS_A * GROUP_WIDTH, side_width=2 * HEAD_DIM)
        proj = proj.reshape(bsz, s, D_IN_PAD)
        kc, vc_t = _compress(kcv.reshape(bsz, s, 2 * HEAD_DIM), cmp_pos_k, cmp_pos_v, wk1, wk2, wv1, wv2t, layer)
        lambda_init = 0.8 - 0.6 * math.exp(-0.3 * layer)
        (o_sb,) = _run_mixers([_stick_breaking(proj)], bsz, s, "stick_breaking")
        (o_mb,) = _run_mixers([_moba(proj, bias_diag, bias_sub)], bsz, s, "moba")
        (o_ns,) = _run_mixers([_nsa(proj, kc, vc_t, bias_diag, bias_sub, bias_cmp, cover_t, expand)], bsz, s, "nsa")
        (o_df,) = _run_mixers([_diff(proj, diff_lambda[layer], diff_subln[layer][:, None], bias_diag, bias_sub,
                                     lambda_init)], bsz, s, "diff_attention")
        groups = [o.reshape(bsz * s, GROUP_WIDTH) for o in (o_sb, o_mb, o_ns, o_df)]
        x2 = _out_mlp(x2, groups, w_out_c, norm_mlp[layer][None], w_up_c, w_down_c, layer,
                      final_norm[None], final_norm=(layer == depth - 1))
    return x2.reshape(bsz, s, d)
```

```python
import functools
import math
from typing import Any, Callable, NamedTuple, Optional, Sequence

import numpy as np
import jax
import jax.numpy as jnp
from jax import lax
from jax.experimental import pallas as pl
from jax.experimental.pallas import tpu as pltpu

HEAD_DIM = 64
GROUP_HEADS = 4
GROUP_WIDTH = GROUP_HEADS * HEAD_DIM
NORM_EPS = 1e-6
NEG_INF = -1e30
BIG = 1e30
FORCE = 1e30
TINY = 1e-30
SOFTPLUS_CLAMP = 64.0
PICKED = -3e38
LOG2E = math.log2(math.e)
N_BUCKETS = 32
MAX_DISTANCE = 128
MOBA_BLOCK = 256
MOBA_TOPK = 3
CMP_LEN = 32
CMP_STRIDE = 16
SLC_LEN = 64
SLC_TOPN = 4
WINDOW = 512
DIFF_HALF = HEAD_DIM // 2
LANES = 128
SUBLANES = 8
TILE = 256
BF16_ROWS = 16
ACC_ROWS = HEAD_DIM + BF16_ROWS
N_GATES = 3 * GROUP_HEADS
COLS_BEFORE_PAD = 9 * GROUP_WIDTH - 2 * HEAD_DIM + N_GATES
PAD_COLS = 2 * HEAD_DIM - N_GATES
CB_SB_Q, CB_SB_K, CB_SB_V, CB_MB_Q, CB_MB_K, CB_MB_V, CB_NS_Q, CB_NS_A, CB_NS_B, CB_DF_Q, CB_DF_K, CB_DF_V = range(12)
D_IN_PAD = 12 * GROUP_WIDTH

_MXU = jnp.bfloat16
_VMEM_LIMIT = 56 * 1024 * 1024
_HEADS = range(GROUP_HEADS)
Q_TILES = 8
LOOKAHEAD = 2


def _dot(a, b, precision=None):
    return jnp.dot(a, b, precision=precision, preferred_element_type=jnp.float32)


def _dot_nt(a, b, precision=None):
    return lax.dot_general(a, b, (((1,), (1,)), ((), ())), precision=precision,
                           preferred_element_type=jnp.float32)


def _rms(x, g):
    return x * lax.rsqrt(jnp.mean(x * x, axis=-1, keepdims=True) + NORM_EPS) * g


def _params(*sem):
    return pltpu.CompilerParams(dimension_semantics=sem, vmem_limit_bytes=_VMEM_LIMIT)


def _norm_matmul_kernel(x_ref, g_ref, w_ref, scale_ref, o_ref, side_ref, wp_ref, *, tn, side_col, gap):
    @pl.when(pl.program_id(0) == 0)
    def _():
        lo, hi = gap
        wp_ref[:, :lo] = w_ref[:, :lo]
        wp_ref[:, lo:hi] = jnp.zeros((wp_ref.shape[0], hi - lo), wp_ref.dtype)
        wp_ref[:, hi:] = w_ref[:, lo:lo + wp_ref.shape[1] - hi]

    h = _rms(x_ref[...], g_ref[...]).astype(_MXU)
    for j in range(wp_ref.shape[1] // tn):
        cols = slice(j * tn, (j + 1) * tn)
        acc = _dot(h, wp_ref[:, cols])
        o_ref[:, cols] = (acc * scale_ref[:, cols]).astype(o_ref.dtype)
        if j * tn <= side_col < (j + 1) * tn:
            side_ref[...] = acc[:, side_col - j * tn:side_col - j * tn + side_ref.shape[1]]


def _norm_matmul(x, g, w, layer, col_scale, *, gap, side_col, side_width, tm=512, tn=1024):
    m, d = x.shape
    n = w.shape[2]
    assert side_col % LANES == 0 and side_col // tn == (side_col + side_width - 1) // tn
    return pl.pallas_call(
        functools.partial(_norm_matmul_kernel, tn=tn, side_col=side_col, gap=gap),
        grid=(m // tm,),
        in_specs=[pl.BlockSpec((tm, d), lambda i: (i, 0)),
                  pl.BlockSpec((1, d), lambda i: (0, 0)),
                  pl.BlockSpec((None, d, w.shape[2]), lambda i: (layer, 0, 0), pipeline_mode=pl.Buffered(1)),
                  pl.BlockSpec((1, n), lambda i: (0, 0))],
        out_specs=[pl.BlockSpec((tm, n), lambda i: (i, 0)),
                   pl.BlockSpec((tm, side_width), lambda i: (i, 0))],
        out_shape=[jax.ShapeDtypeStruct((m, n), _MXU),
                   jax.ShapeDtypeStruct((m, side_width), jnp.float32)],
        scratch_shapes=[pltpu.VMEM((d, n), _MXU)],
        compiler_params=_params("arbitrary"),
        name="norm_in_proj",
    )(x, g, w, col_scale)


def _out_mlp_kernel(x_ref, a_ref, b_ref, c_ref, d_ref, wo_ref, g_ref, wu_ref, wd_ref, gf_ref, o_ref,
                    *, final_norm, tf):
    mixed = jnp.concatenate([a_ref[...], b_ref[...], c_ref[...], d_ref[...]], axis=1)
    y = x_ref[...] + _dot(mixed, wo_ref[...])
    h = _rms(y, g_ref[...]).astype(_MXU)
    for c in range(wu_ref.shape[1] // tf):
        u = jnp.square(jnp.maximum(_dot(h, wu_ref[:, c * tf:(c + 1) * tf]), 0.0))
        y = y + _dot(u.astype(_MXU), wd_ref[c * tf:(c + 1) * tf, :])
    if final_norm:
        y = _rms(y, gf_ref[...])
    o_ref[...] = y


def _out_mlp(x, groups, w_out, g, w_up, w_down, layer, g_final, *, final_norm, tm=512, tf=2048):
    m, d = x.shape
    f = w_up.shape[2]
    gspec = pl.BlockSpec((tm, GROUP_WIDTH), lambda i: (i, 0))
    row = pl.BlockSpec((1, d), lambda i: (0, 0))

    def resident(rows, cols):
        return pl.BlockSpec((None, rows, cols), lambda i: (layer, 0, 0), pipeline_mode=pl.Buffered(1))

    return pl.pallas_call(
        functools.partial(_out_mlp_kernel, final_norm=final_norm, tf=tf),
        grid=(m // tm,),
        in_specs=[pl.BlockSpec((tm, d), lambda i: (i, 0)), gspec, gspec, gspec, gspec,
                  resident(d, d), row, resident(d, f), resident(f, d), row],
        out_specs=pl.BlockSpec((tm, d), lambda i: (i, 0)),
        out_shape=jax.ShapeDtypeStruct((m, d), jnp.float32),
        compiler_params=_params("parallel"),
        name="out_proj_mlp_residual",
    )(x, *groups, w_out, g, w_up, w_down, g_final)


def _t5_bucket(dist):
    n = jnp.maximum(dist, 0)
    max_exact = N_BUCKETS // 2
    nf = jnp.maximum(n, 1).astype(jnp.float32)
    large = max_exact + (jnp.log(nf / max_exact) / math.log(MAX_DISTANCE / max_exact)
                         * (N_BUCKETS - max_exact)).astype(jnp.int32)
    large = jnp.minimum(large, N_BUCKETS - 1)
    return jnp.where(n < max_exact, n, large)


def _bias_kernel(tab_ref, o_ref, *, row_stride, col_stride, offset, head0):
    nh, tr, tc = o_ref.shape
    for blk in range(tc // LANES):
        rows = lax.broadcasted_iota(jnp.int32, (tr, LANES), 0)
        cols = lax.broadcasted_iota(jnp.int32, (tr, LANES), 1) + (pl.program_id(0) * tc + blk * LANES)
        bucket = _t5_bucket(rows * row_stride + cols * col_stride + offset)
        for h in range(nh):
            row = tab_ref[head0 + h:head0 + h + 1, :]
            row = (row - row[:, N_BUCKETS - 1:N_BUCKETS]) * LOG2E
            o_ref[h, :, blk * LANES:(blk + 1) * LANES] = jnp.take_along_axis(
                jnp.broadcast_to(row, (tr, LANES)), bucket, axis=1, mode="promise_in_bounds")


def _bias_table(table_t, *, n_heads, head0, rows, cols, col_tile, row_stride, col_stride, offset):
    return pl.pallas_call(
        functools.partial(_bias_kernel, row_stride=row_stride, col_stride=col_stride,
                          offset=offset, head0=head0),
        grid=(cols // col_tile,),
        in_specs=[pl.BlockSpec(table_t.shape, lambda i: (0, 0))],
        out_specs=pl.BlockSpec((n_heads, rows, col_tile), lambda i: (0, 0, i)),
        out_shape=jax.ShapeDtypeStruct((n_heads, rows, cols), jnp.float32),
        compiler_params=_params("parallel"),
        name="t5_bias_tiles",
    )(table_t)


def _softmax_init(t):
    return (jnp.full((1, t), NEG_INF, jnp.float32), jnp.zeros((ACC_ROWS, t), jnp.float32))


class _KeyTile(NamedTuple):
    kts: Sequence[Any]
    vts: Sequence[Any]
    biases: Optional[Sequence[Any]] = None
    emasks: Optional[Sequence[Any]] = None
    qmasks: Optional[Sequence[Any]] = None


def _round_robin(lists):
    out = []
    for rank in range(max(map(len, lists), default=0)):
        out.extend(items[rank] for items in lists if rank < len(items))
    return out


def _softmax_jobs(t, jobs):
    built = {}

    def tiles_of(job, g):
        if (job, g) not in built:
            built[job, g] = [make() for make in jobs[job][2][g]]
        return built[job, g]

    def scores_of(job, g, c):
        qts = jobs[job][1]
        row = []
        for tile in tiles_of(job, g):
            s = _dot(tile.kts[c], qts[c])
            if tile.biases is not None:
                s = s + tile.biases[c]
            if tile.emasks is not None:
                s = jnp.where(tile.emasks[c], s, NEG_INF)
            row.append(s.astype(_MXU))
        return row

    def update(state, job, g, c, scores):
        m, acc = state
        m_new = m
        for tile, s in zip(tiles_of(job, g), scores):
            tile_max = jnp.max(s, axis=0, keepdims=True).astype(jnp.float32)
            if tile.qmasks is not None:
                tile_max = jnp.where(tile.qmasks[c], tile_max, NEG_INF)
            m_new = jnp.maximum(m_new, tile_max)
        seen = m_new > 0.5 * NEG_INF
        acc = jnp.exp2(m - m_new) * acc
        for tile, s in zip(tiles_of(job, g), scores):
            ok = seen if tile.qmasks is None else jnp.logical_and(seen, tile.qmasks[c])
            acc = acc + _dot(tile.vts[c], jnp.exp2(s - jnp.where(ok, m_new, BIG).astype(_MXU)))
        return m_new, acc

    units = _round_robin([[(job, g, c) for g in range(len(groups)) for c in range(n)]
                          for job, (n, _, groups) in enumerate(jobs)])
    lookahead = max(LOOKAHEAD, len(jobs))
    states = [[_softmax_init(t) for _ in range(n)] for n, _, _ in jobs]
    pending = {k: scores_of(*units[k]) for k in range(min(lookahead, len(units)))}
    for k, (job, g, c) in enumerate(units):
        if k + lookahead < len(units):
            pending[k + lookahead] = scores_of(*units[k + lookahead])
        states[job][c] = update(states[job][c], job, g, c, pending.pop(k))
        yield
    return states


def _pairs(items):
    return [items[p:p + 2] for p in range(0, len(items), 2)]


class _Mixer(NamedTuple):
    emit: Callable[..., Any]
    in_specs: Sequence[Any]
    operands: Sequence[Any]
    scratch: Sequence[Any]


def _mixer_kernel(*refs, emit):
    for _ in emit(*refs):
        pass


def _run_mixer(mixer, bsz, s, name):
    return pl.pallas_call(
        functools.partial(_mixer_kernel, emit=mixer.emit),
        grid=(bsz,),
        in_specs=list(mixer.in_specs),
        out_specs=pl.BlockSpec((1, s, GROUP_WIDTH), lambda b: (b, 0, 0)),
        out_shape=jax.ShapeDtypeStruct((bsz, s, GROUP_WIDTH), _MXU),
        scratch_shapes=list(mixer.scratch),
        compiler_params=_params("parallel"),
        name=name,
    )(*mixer.operands)


def _softmax_out(state):
    acc = state[1]
    return acc[:HEAD_DIM] / jnp.maximum(acc[HEAD_DIM:HEAD_DIM + 1], TINY)


def _top_k_rows(score, row_f, k):
    sel = jnp.zeros(score.shape, jnp.float32)
    for _ in range(k):
        mx = jnp.max(score, axis=0, keepdims=True)
        idx = jnp.min(jnp.where(score == mx, row_f, float(score.shape[0])), axis=0, keepdims=True)
        pick = row_f == idx
        sel = jnp.where(pick, 1.0, sel)
        score = jnp.where(pick, PICKED, score)
    return sel


def _now(fn):
    fn()


def _tile_iotas(t):
    return (lax.broadcasted_iota(jnp.int32, (t, t), 0), lax.broadcasted_iota(jnp.int32, (t, t), 1))


def _key_rows(ref, j, t):
    if isinstance(j, int):
        return ref[j * t:(j + 1) * t, :]
    return ref[pl.ds(pl.multiple_of(j * t, t), t), :]


def _transposed(ref_block):
    return ref_block.astype(jnp.float32).T


def _queries_t(q_ref, width, t):
    qt = _transposed(q_ref[0]).astype(_MXU)
    return [[qt[c * width:(c + 1) * width, r * t:(r + 1) * t] for c in range(GROUP_WIDTH // width)]
            for r in range(Q_TILES)]


def _fill_values_t(vt_ref, v_ref, col0, t):
    n_tiles, rows, _ = vt_ref.shape
    lane_block = (col0 // LANES) * LANES
    for c in range(n_tiles):
        blk = _transposed(v_ref[0, c * t:(c + 1) * t, lane_block:lane_block + LANES])
        vt_ref[c, 0:HEAD_DIM, :] = blk[col0 - lane_block:col0 - lane_block + HEAD_DIM].astype(vt_ref.dtype)
        if rows == ACC_ROWS:
            first = lax.broadcasted_iota(jnp.int32, (rows - HEAD_DIM, t), 0) == 0
            vt_ref[c, HEAD_DIM:rows, :] = jnp.where(first, 1.0, 0.0).astype(vt_ref.dtype)


def _group_specs(s, *column_blocks):
    return [pl.BlockSpec((1, s, GROUP_WIDTH), functools.partial(lambda cb, b: (b, 0, cb), cb))
            for cb in column_blocks]


def _bias_specs(t, head_group):
    spec = pl.BlockSpec((GROUP_HEADS, t, t), lambda b: (head_group, 0, 0))
    return [spec, spec]


def _store_heads(o_ref, outs_t):
    tiles = [jnp.concatenate(heads, axis=0) for heads in outs_t]
    o_ref[0] = jnp.concatenate(tiles, axis=1).T.astype(o_ref.dtype)


def _sb_emit(q_ref, k_ref, v_ref, o_ref, kb_ref, vt_ref):
    t = q_ref.shape[1] // Q_TILES
    key, qry = _tile_iotas(t)
    strict = key < qry
    later = jnp.where(qry > key, 1.0, 0.0).astype(_MXU)

    @_now
    def _():
        for h in _HEADS:
            kb_ref[h] = k_ref[0, :, h * HEAD_DIM:(h + 1) * HEAD_DIM].astype(kb_ref.dtype)
            _fill_values_t(vt_ref.at[h], v_ref, h * HEAD_DIM, t)

    def query_step(step):
        qts = _queries_t(q_ref, HEAD_DIM, t)
        units = _round_robin([[(r, step * Q_TILES + r, pair, h)
                               for pair in _pairs(list(range(step * Q_TILES + r, -1, -1))) for h in _HEADS]
                              for r in range(Q_TILES)])
        zero = (jnp.zeros((HEAD_DIM, t), jnp.float32), jnp.zeros((1, t), jnp.float32))
        carry = [[zero] * GROUP_HEADS for _ in range(Q_TILES)]
        zs, log_keeps, suffixes = {}, {}, {}

        def scores(u):
            r, _, pair, h = units[u]
            zs[u] = [_dot(_key_rows(kb_ref.at[h], j, t), qts[r][h]) for j in pair]

        def keeps(u):
            _, k, pair, _ = units[u]
            log_keeps[u], suffixes[u] = [], []
            for j, z in zip(pair, zs[u]):
                drop = jnp.maximum(jnp.log2(1.0 + jnp.exp2(jnp.minimum(z, SOFTPLUS_CLAMP))), z)
                if j == k:
                    drop = jnp.where(strict, drop, 0.0)
                log_keeps[u].append(drop)
                suffixes[u].append(_dot(later, drop.astype(_MXU)))

        def values(u):
            r, k, pair, h = units[u]
            acc, run = carry[r][h]
            weights = []
            for j, z, drop, suffix in zip(pair, zs.pop(u), log_keeps.pop(u), suffixes.pop(u)):
                a = jnp.exp2(z - drop - suffix + run)
                if j == k:
                    a = jnp.where(strict, a, 0.0)
                weights.append(a.astype(_MXU))
                run = run - (suffix[0:1] + drop[0:1])
            for j, w in zip(pair, weights):
                acc = acc + _dot(vt_ref[h, j], w)
            carry[r][h] = (acc, run)

        stages = (scores, keeps, values)
        for tick in range(len(units) + len(stages) - 1):
            for lag, stage in enumerate(stages):
                if 0 <= tick - lag < len(units):
                    stage(tick - lag)
            yield
        _store_heads(o_ref, [[c[0] for c in tile_carry] for tile_carry in carry])

    yield from query_step(0)


def _stick_breaking(proj, t=TILE):
    s = proj.shape[1]
    scratch = [pltpu.VMEM((GROUP_HEADS, s, HEAD_DIM), _MXU), pltpu.VMEM((GROUP_HEADS, s // t, HEAD_DIM, t), _MXU)]
    return _Mixer(_sb_emit, _group_specs(s, CB_SB_Q, CB_SB_K, CB_SB_V), [proj] * 3, scratch)


def _moba_emit(q_ref, k_ref, v_ref, bd_ref, bs_ref, o_ref, kb_ref, vt_ref, km_ref):
    t = q_ref.shape[1] // Q_TILES
    n_blk = k_ref.shape[1] // MOBA_BLOCK
    tiles_per_blk = MOBA_BLOCK // t
    blk_shift = int(math.log2(tiles_per_blk))

    @_now
    def _():
        km_ref[...] = jnp.zeros_like(km_ref)
        for h in _HEADS:
            lo, hi = h * HEAD_DIM, (h + 1) * HEAD_DIM
            kb_ref[h] = k_ref[0, :, lo:hi].astype(kb_ref.dtype)
            _fill_values_t(vt_ref.at[h], v_ref, lo, t)
            for n in range(n_blk):
                blk = k_ref[0, n * MOBA_BLOCK:(n + 1) * MOBA_BLOCK, lo:hi]
                km_ref[h, n:n + 1, :] = jnp.mean(blk.astype(jnp.float32), axis=0, keepdims=True)

    def query_step(step):
        key, qry = _tile_iotas(t)
        causal = key <= qry
        blk_row = lax.broadcasted_iota(jnp.int32, (km_ref.shape[1], t), 0)
        qt = _transposed(q_ref[0])
        jobs = []
        for r in range(Q_TILES):
            k = step * Q_TILES + r
            own = k >> blk_shift
            qts, sels = [], []
            for h in _HEADS:
                qf = qt[h * HEAD_DIM:(h + 1) * HEAD_DIM, r * t:(r + 1) * t]
                qts.append(qf.astype(_MXU))
                gate = _dot(km_ref[h], qf, precision=lax.Precision.HIGHEST)
                gate = jnp.where(blk_row < own, gate, NEG_INF)
                sel = _top_k_rows(gate, blk_row.astype(jnp.float32), min(MOBA_TOPK, n_blk - 1))
                sels.append(jnp.where(blk_row < own, sel, 0.0))

            def tile(own, sels, j, bias_ref=None, emask=None):
                n = j >> blk_shift
                return _KeyTile([_key_rows(kb_ref.at[h], j, t) for h in _HEADS], [vt_ref[h, j] for h in _HEADS],
                                None if bias_ref is None else [bias_ref[h] for h in _HEADS],
                                None if emask is None else [emask] * GROUP_HEADS,
                                None if n == own else [sels[h][n:n + 1] > 0.5 for h in _HEADS])

            tile = functools.partial(tile, own, sels)
            groups = _pairs([functools.partial(tile, j) for j in range(k - 1)])
            groups.append(([functools.partial(tile, k - 1, bs_ref)] if k else [])
                          + [functools.partial(tile, k, bd_ref, causal)])
            jobs.append((GROUP_HEADS, qts, groups))
        all_states = yield from _softmax_jobs(t, jobs)
        _store_heads(o_ref, [[_softmax_out(st) for st in states] for states in all_states])

    yield from query_step(0)


def _kv_scratch(s, t, key_dim=HEAD_DIM, n_keys=GROUP_HEADS):
    return [pltpu.VMEM((n_keys, s, key_dim), _MXU), pltpu.VMEM((GROUP_HEADS, s // t, ACC_ROWS, t), _MXU)]


def _moba(proj, bias_diag, bias_sub, t=TILE):
    s = proj.shape[1]
    n_blk_pad = -(-(s // MOBA_BLOCK) // SUBLANES) * SUBLANES
    scratch = _kv_scratch(s, t) + [pltpu.VMEM((GROUP_HEADS, n_blk_pad, HEAD_DIM), jnp.float32)]
    return _Mixer(_moba_emit, _group_specs(s, CB_MB_Q, CB_MB_K, CB_MB_V) + _bias_specs(t, 0),
                  [proj] * 3 + [bias_diag, bias_sub], scratch)


def _diff_emit(lam_ref, g_ref, q_ref, k_ref, v_ref, bd_ref, bs_ref, o_ref, kb_ref, vt_ref, *, lambda_init):
    t = q_ref.shape[1] // Q_TILES
    key, qry = _tile_iotas(t)
    causal = key <= qry
    lv = lam_ref[...]
    lam = (jnp.exp(jnp.sum(lv[0:1] * lv[1:2], keepdims=True))
           - jnp.exp(jnp.sum(lv[2:3] * lv[3:4], keepdims=True)) + lambda_init)
    halves = range(2 * GROUP_HEADS)

    @_now
    def _():
        for c in halves:
            kb_ref[c] = k_ref[0, :, c * DIFF_HALF:(c + 1) * DIFF_HALF].astype(kb_ref.dtype)
        for h in _HEADS:
            _fill_values_t(vt_ref.at[h], v_ref, h * HEAD_DIM, t)

    def tile(j, bias_ref=None, emask=None):
        n = len(halves)
        return _KeyTile([_key_rows(kb_ref.at[c], j, t) for c in halves], [vt_ref[c // 2, j] for c in halves],
                        None if bias_ref is None else [bias_ref[c // 2] for c in halves],
                        None if emask is None else [emask] * n)

    def query_step(step):
        qts = _queries_t(q_ref, DIFF_HALF, t)
        jobs = []
        for r in range(Q_TILES):
            k = step * Q_TILES + r
            groups = _pairs([functools.partial(tile, j) for j in range(k - 1)])
            groups.append(([functools.partial(tile, k - 1, bs_ref)] if k else [])
                          + [functools.partial(tile, k, bd_ref, causal)])
            jobs.append((len(halves), qts[r], groups))
        all_states = yield from _softmax_jobs(t, jobs)
        outs = []
        for states in all_states:
            heads = []
            for h in _HEADS:
                o = _softmax_out(states[2 * h]) - lam * _softmax_out(states[2 * h + 1])
                o = o * lax.rsqrt(jnp.mean(o * o, axis=0, keepdims=True) + NORM_EPS) * g_ref[...]
                heads.append(o * (1.0 - lambda_init))
            outs.append(heads)
        _store_heads(o_ref, outs)

    yield from query_step(0)


def _diff(proj, lam_params, subln, bias_diag, bias_sub, lambda_init, t=TILE):
    s = proj.shape[1]
    in_specs = ([pl.BlockSpec(lam_params.shape, lambda b: (0, 0)), pl.BlockSpec(subln.shape, lambda b: (0, 0))]
                + _group_specs(s, CB_DF_Q, CB_DF_K, CB_DF_V) + _bias_specs(t, 2))
    return _Mixer(functools.partial(_diff_emit, lambda_init=lambda_init), in_specs,
                  [lam_params, subln] + [proj] * 3 + [bias_diag, bias_sub],
                  _kv_scratch(s, t, key_dim=DIFF_HALF, n_keys=2 * GROUP_HEADS))


def _compress_kernel(kcv_ref, pk_ref, pv_ref, wk1_ref, wk2_ref, wv1_ref, wv2t_ref, kc_ref, vct_ref):
    n_chunk = kc_ref.shape[1]

    branches = ((0, pk_ref, wk1_ref), (HEAD_DIM, pv_ref, wv1_ref))
    tops = [jnp.zeros((n_chunk, w1_ref.shape[1]), jnp.float32) for _, _, w1_ref in branches]
    bots = list(tops)
    pack = 2 * LANES // HEAD_DIM
    for l0 in range(0, CMP_STRIDE, pack):
        tokens = [kcv_ref[0, pl.ds(l, n_chunk, stride=CMP_STRIDE), :] for l in range(l0, l0 + pack)]
        for n, (col0, p_ref, w1_ref) in enumerate(branches):
            for half, acc in ((0, tops), (CMP_STRIDE, bots)):
                x = jnp.concatenate([(tok[:, col0:col0 + HEAD_DIM] + p_ref[half + l:half + l + 1, :]).astype(_MXU)
                                     for l, tok in zip(range(l0, l0 + pack), tokens)], axis=1)
                acc[n] = acc[n] + _dot(x, w1_ref[(half + l0) * HEAD_DIM:(half + l0 + pack) * HEAD_DIM, :])
    hidden = [jax.nn.gelu(top + pltpu.roll(bot, n_chunk - 1, axis=0)).astype(_MXU) for top, bot in zip(tops, bots)]
    kc_ref[0] = _dot(hidden[0], wk2_ref[...])
    vct_ref[0] = _dot_nt(wv2t_ref[...], hidden[1])


def _compress(kcv, pos_k, pos_v, wk1, wk2, wv1, wv2t, layer):
    bsz, s, width = kcv.shape
    n_chunk = s // CMP_STRIDE

    def full(a):
        return pl.BlockSpec((None,) + a.shape[1:], lambda b: (layer, 0, 0))

    return pl.pallas_call(
        _compress_kernel,
        grid=(bsz,),
        in_specs=[pl.BlockSpec((1, s, width), lambda b: (b, 0, 0)),
                  full(pos_k), full(pos_v), full(wk1), full(wk2), full(wv1), full(wv2t)],
        out_specs=[pl.BlockSpec((1, n_chunk, HEAD_DIM), lambda b: (b, 0, 0)),
                   pl.BlockSpec((1, HEAD_DIM, n_chunk), lambda b: (b, 0, 0))],
        out_shape=[jax.ShapeDtypeStruct((bsz, n_chunk, HEAD_DIM), jnp.float32),
                   jax.ShapeDtypeStruct((bsz, HEAD_DIM, n_chunk), jnp.float32)],
        compiler_params=_params("parallel"),
        name="nsa_compress",
    )(kcv, pos_k, pos_v, wk1, wk2, wv1, wv2t)


def _nsa_emit(q_ref, kva_ref, kvb_ref, kc_ref, vct_ref, bd_ref, bs_ref, bc_ref, cover_ref, e_ref,
                o_ref, ks_ref, vst_ref, kw_ref, vwt_ref):
    t = q_ref.shape[1] // Q_TILES
    key, qry = _tile_iotas(t)
    causal = key <= qry
    ks_col, vs_col, kw_col, vw_col, gate_col = 2 * HEAD_DIM, 3 * HEAD_DIM, 0, HEAD_DIM, 2 * HEAD_DIM

    @_now
    def _():
        ks_ref[:, :HEAD_DIM] = kva_ref[0, :, ks_col:ks_col + HEAD_DIM].astype(ks_ref.dtype)
        ks_ref[:, HEAD_DIM:] = e_ref[...]
        kw_ref[...] = kvb_ref[0, :, kw_col:kw_col + HEAD_DIM].astype(kw_ref.dtype)
        _fill_values_t(vst_ref, kva_ref, vs_col, t)
        _fill_values_t(vwt_ref, kvb_ref, vw_col, t)

    def tile(k_ref, vt_ref, j, bias_ref=None, emask=None):
        n = GROUP_HEADS
        return _KeyTile([_key_rows(k_ref, j, t)] * n, [vt_ref[j]] * n,
                        None if bias_ref is None else [bias_ref[h] for h in _HEADS],
                        None if emask is None else [emask] * n)

    def query_step(step):
        qts = _queries_t(q_ref, HEAD_DIM, t)
        n_cmp = kc_ref.shape[1]
        n_slc = cover_ref.shape[0]
        kc = kc_ref[0].astype(_MXU)
        vct = vct_ref[0].astype(_MXU)
        c_row = lax.broadcasted_iota(jnp.int32, (n_cmp, t), 0)
        c_col = lax.broadcasted_iota(jnp.int32, (n_cmp, t), 1)
        s_row = lax.broadcasted_iota(jnp.int32, (n_slc, t), 0)
        s_col = lax.broadcasted_iota(jnp.int32, (n_slc, t), 1)
        tiles = [step * Q_TILES + r for r in range(Q_TILES)]

        o_cmp, importance = [], []
        for r, k in enumerate(tiles):
            visible = c_col + k * t >= c_row * CMP_STRIDE + (CMP_LEN - 1)
            cmp_scores = [_dot(kc, qts[r][h]) for h in _HEADS]
            cmp_probs = []
            p_sum = jnp.zeros((n_cmp, t), jnp.float32)
            for h in _HEADS:
                sc = jnp.where(visible, cmp_scores[h] + bc_ref[h, :, r * t:(r + 1) * t], NEG_INF)
                e = jnp.where(visible, jnp.exp2(sc - jnp.max(sc, axis=0, keepdims=True)), 0.0)
                p = e / jnp.maximum(jnp.sum(e, axis=0, keepdims=True), TINY)
                cmp_probs.append(p.astype(_MXU))
                p_sum = p_sum + p
            o_cmp.append([_dot(vct, cmp_probs[h]) for h in _HEADS])
            importance.append(_dot(cover_ref[...], p_sum, precision=lax.Precision.HIGHEST))

        n_back = WINDOW // t
        jobs = []
        for r, k in enumerate(tiles):
            window = [functools.partial(tile, kw_ref, vwt_ref, k - n_back, None, qry < key)] if k >= n_back else []
            for back in range(min(n_back - 1, k), 0, -1):
                window.append(functools.partial(tile, kw_ref, vwt_ref, k - back, bs_ref if back == 1 else None))
            window.append(functools.partial(tile, kw_ref, vwt_ref, k, bd_ref, causal))
            jobs.append((GROUP_HEADS, qts[r], [window]))
        win_states = yield from _softmax_jobs(t, jobs)
        o_win = [[_softmax_out(st) for st in states] for states in win_states]

        jobs = []
        for r, k in enumerate(tiles):
            own = jnp.right_shift(s_col + k * t, int(math.log2(SLC_LEN)))
            score = jnp.where(s_row == own, FORCE, jnp.where(s_row < own, importance[r], NEG_INF))
            sel = _top_k_rows(score, s_row.astype(jnp.float32), min(SLC_TOPN, n_slc))
            penalty = jnp.where(sel > 0.5, 0.0, NEG_INF).astype(_MXU)
            q_aug = [jnp.concatenate([qts[r][h], penalty], axis=0) for h in _HEADS]
            groups = _pairs([functools.partial(tile, ks_ref, vst_ref, j) for j in range(k - 1)])
            groups.append(([functools.partial(tile, ks_ref, vst_ref, k - 1, bs_ref)] if k else [])
                          + [functools.partial(tile, ks_ref, vst_ref, k, bd_ref, causal)])
            jobs.append((GROUP_HEADS, q_aug, groups))
        slc_states = yield from _softmax_jobs(t, jobs)
        o_slc = [[_softmax_out(st) for st in states] for states in slc_states]

        gates = _transposed(kvb_ref[0, :, (gate_col // LANES) * LANES:(gate_col // LANES + 1) * LANES])
        gates = 1.0 / (1.0 + jnp.exp(-gates[gate_col % LANES:gate_col % LANES + N_GATES + 4]))
        outs = []
        for r in range(Q_TILES):
            heads = []
            for h in _HEADS:
                g = [gates[br * GROUP_HEADS + h:br * GROUP_HEADS + h + 1, r * t:(r + 1) * t] for br in range(3)]
                heads.append(g[0] * o_cmp[r][h] + g[1] * o_slc[r][h] + g[2] * o_win[r][h])
            outs.append(heads)
        _store_heads(o_ref, outs)

    yield from query_step(0)


def _nsa(proj, kc, vc_t, bias_diag, bias_sub, bias_cmp, cover_t, expand, t=TILE):
    s = proj.shape[1]
    n_cmp = kc.shape[1]
    values_t = pltpu.VMEM((s // t, ACC_ROWS, t), _MXU)
    in_specs = (_group_specs(s, CB_NS_Q, CB_NS_A, CB_NS_B)
                + [pl.BlockSpec((1, n_cmp, HEAD_DIM), lambda b: (b, 0, 0)),
                   pl.BlockSpec((1, HEAD_DIM, n_cmp), lambda b: (b, 0, 0))]
                + _bias_specs(t, 1)
                + [pl.BlockSpec(bias_cmp.shape, lambda b: (0, 0, 0)),
                   pl.BlockSpec(cover_t.shape, lambda b: (0, 0)),
                   pl.BlockSpec(expand.shape, lambda b: (0, 0))])
    scratch = [pltpu.VMEM((s, HEAD_DIM + expand.shape[1]), _MXU), values_t, pltpu.VMEM((s, HEAD_DIM), _MXU), values_t]
    return _Mixer(_nsa_emit, in_specs, [proj] * 3 + [kc, vc_t, bias_diag, bias_sub, bias_cmp, cover_t, expand],
                  scratch)


def _nsa_constants(s):
    n_cmp = (s - CMP_LEN) // CMP_STRIDE + 1
    n_slc = s // SLC_LEN
    assert n_cmp + 1 == s // CMP_STRIDE and n_slc % SUBLANES == 0
    c_start = np.arange(n_cmp) * CMP_STRIDE
    s_start = np.arange(n_slc) * SLC_LEN
    cover = np.clip(np.minimum((c_start + CMP_LEN - 1)[:, None], (s_start + SLC_LEN - 1)[None, :])
                    - np.maximum(c_start[:, None], s_start[None, :]) + 1, 0, None) / CMP_LEN
    cover_t = np.zeros((n_slc, n_cmp + 1), np.float32)
    cover_t[:, :n_cmp] = cover.T
    expand = (np.arange(s)[:, None] // SLC_LEN == np.arange(n_slc)[None, :]).astype(np.float32)
    return jnp.asarray(cover_t), jnp.asarray(expand, _MXU)


def kernel(x, w_in, w_out, w_up, w_down, norm_attn, norm_mlp, cmp_pos_k, cmp_pos_v, cmp_k_w1, cmp_k_w2,
           cmp_v_w1, cmp_v_w2, diff_lambda, diff_subln, rel_bias, final_norm):
    bsz, s, d = x.shape
    depth = w_in.shape[0]
    t = TILE
    n_chunk = s // CMP_STRIDE
    assert s == Q_TILES * t and MOBA_BLOCK % t == 0 and WINDOW % t == 0 and t >= MAX_DISTANCE

    assert w_in.shape[2] + PAD_COLS == D_IN_PAD
    w_in_c = jnp.pad(w_in.astype(_MXU), ((0, 0), (0, 0), (0, PAD_COLS)))
    w_out_c, w_up_c, w_down_c = (w.astype(_MXU) for w in (w_out, w_up, w_down))
    wk1, wk2 = cmp_k_w1.astype(_MXU), cmp_k_w2.astype(_MXU)
    wv1, wv2t = cmp_v_w1.astype(_MXU), jnp.swapaxes(cmp_v_w2, 1, 2).astype(_MXU)

    table_t = jnp.pad(rel_bias.T, ((0, 0), (0, LANES - N_BUCKETS)))
    tiles = dict(n_heads=rel_bias.shape[1], head0=0, rows=t, cols=t, col_tile=t, row_stride=-1, col_stride=1)
    bias_diag = _bias_table(table_t, offset=0, **tiles)
    bias_sub = _bias_table(table_t, offset=t, **tiles)
    bias_cmp = _bias_table(table_t, n_heads=GROUP_HEADS, head0=GROUP_HEADS, rows=n_chunk, cols=s,
                           col_tile=t, row_stride=-CMP_STRIDE, col_stride=1, offset=-(CMP_LEN - 1))
    cover_t, expand = _nsa_constants(s)
    col_scale = np.ones((1, D_IN_PAD), np.float32)
    for cb, width in ((CB_SB_Q, HEAD_DIM), (CB_MB_Q, HEAD_DIM), (CB_NS_Q, HEAD_DIM), (CB_DF_Q, DIFF_HALF)):
        col_scale[:, cb * GROUP_WIDTH:(cb + 1) * GROUP_WIDTH] = width ** -0.5 * LOG2E
    col_scale = jnp.asarray(col_scale)

    x2 = x.reshape(bsz * s, d)
    for layer in range(depth):
        proj, kcv = _norm_matmul(x2, norm_attn[layer][None], w_in_c, layer, col_scale,
                                 gap=(COLS_BEFORE_PAD, COLS_BEFORE_PAD + PAD_COLS),
                                 side_col=CB_NS_A * GROUP_WIDTH, side_width=2 * HEAD_DIM)
        proj = proj.reshape(bsz, s, D_IN_PAD)
        kc, vc_t = _compress(kcv.reshape(bsz, s, 2 * HEAD_DIM), cmp_pos_k, cmp_pos_v, wk1, wk2, wv1, wv2t, layer)
        lambda_init = 0.8 - 0.6 * math.exp(-0.3 * layer)
        o_sb = _run_mixer(_stick_breaking(proj), bsz, s, "stick_breaking")
        o_mb = _run_mixer(_moba(proj, bias_diag, bias_sub), bsz, s, "moba")
        o_ns = _run_mixer(_nsa(proj, kc, vc_t, bias_diag, bias_sub, bias_cmp, cover_t, expand), bsz, s, "nsa")
        o_df = _run_mixer(_diff(proj, diff_lambda[layer], diff_subln[layer][:, None], bias_diag, bias_sub,
                                lambda_init), bsz, s, "diff_attention")
        groups = [o.reshape(bsz * s, GROUP_WIDTH) for o in (o_sb, o_mb, o_ns, o_df)]
        x2 = _out_mlp(x2, groups, w_out_c, norm_mlp[layer][None], w_up_c, w_down_c, layer,
                      final_norm[None], final_norm=(layer == depth - 1))
    return x2.reshape(bsz, s, d)
```

```python
import functools
import math
from typing import Any, Callable, NamedTuple, Optional, Sequence

import numpy as np
import jax
import jax.numpy as jnp
from jax import lax
from jax.experimental import pallas as pl
from jax.experimental.pallas import tpu as pltpu

HEAD_DIM = 64
GROUP_HEADS = 4
GROUP_WIDTH = GROUP_HEADS * HEAD_DIM
NORM_EPS = 1e-6
NEG_INF = -1e30
BIG = 1e30
FORCE = 1e30
TINY = 1e-30
SOFTPLUS_CLAMP = 64.0
PICKED = -3e38
LOG2E = math.log2(math.e)
N_BUCKETS = 32
MAX_DISTANCE = 128
MOBA_BLOCK = 256
MOBA_TOPK = 3
CMP_LEN = 32
CMP_STRIDE = 16
SLC_LEN = 64
SLC_TOPN = 4
WINDOW = 512
DIFF_HALF = HEAD_DIM // 2
LANES = 128
SUBLANES = 8
TILE = 256
BF16_ROWS = 16
ACC_ROWS = HEAD_DIM + BF16_ROWS
N_GATES = 3 * GROUP_HEADS
COLS_BEFORE_PAD = 9 * GROUP_WIDTH - 2 * HEAD_DIM + N_GATES
PAD_COLS = 2 * HEAD_DIM - N_GATES
CB_SB_Q, CB_SB_K, CB_SB_V, CB_MB_Q, CB_MB_K, CB_MB_V, CB_NS_Q, CB_NS_A, CB_NS_B, CB_DF_Q, CB_DF_K, CB_DF_V = range(12)
D_IN_PAD = 12 * GROUP_WIDTH

_MXU = jnp.bfloat16
_VMEM_LIMIT = 56 * 1024 * 1024
_HEADS = range(GROUP_HEADS)
Q_TILES = 8
FAR_GROUP = 3
LOOKAHEAD = 2


def _dot(a, b, precision=None):
    return jnp.dot(a, b, precision=precision, preferred_element_type=jnp.float32)


def _dot_nt(a, b, precision=None):
    return lax.dot_general(a, b, (((1,), (1,)), ((), ())), precision=precision,
                           preferred_element_type=jnp.float32)


def _rms(x, g):
    return x * lax.rsqrt(jnp.mean(x * x, axis=-1, keepdims=True) + NORM_EPS) * g


def _params(*sem):
    return pltpu.CompilerParams(dimension_semantics=sem, vmem_limit_bytes=_VMEM_LIMIT)


def _norm_matmul_kernel(x_ref, g_ref, w_ref, scale_ref, o_ref, side_ref, wp_ref, *, tn, side_col, gap):
    @pl.when(pl.program_id(0) == 0)
    def _():
        lo, hi = gap
        wp_ref[:, :lo] = w_ref[:, :lo]
        wp_ref[:, lo:hi] = jnp.zeros((wp_ref.shape[0], hi - lo), wp_ref.dtype)
        wp_ref[:, hi:] = w_ref[:, lo:lo + wp_ref.shape[1] - hi]

    h = _rms(x_ref[...], g_ref[...]).astype(_MXU)
    for j in range(wp_ref.shape[1] // tn):
        cols = slice(j * tn, (j + 1) * tn)
        acc = _dot(h, wp_ref[:, cols])
        o_ref[:, cols] = (acc * scale_ref[:, cols]).astype(o_ref.dtype)
        if j * tn <= side_col < (j + 1) * tn:
            side_ref[...] = acc[:, side_col - j * tn:side_col - j * tn + side_ref.shape[1]]


def _norm_matmul(x, g, w, layer, col_scale, *, gap, side_col, side_width, tm=512, tn=1024):
    m, d = x.shape
    n = w.shape[2]
    assert side_col % LANES == 0 and side_col // tn == (side_col + side_width - 1) // tn
    return pl.pallas_call(
        functools.partial(_norm_matmul_kernel, tn=tn, side_col=side_col, gap=gap),
        grid=(m // tm,),
        in_specs=[pl.BlockSpec((tm, d), lambda i: (i, 0)),
                  pl.BlockSpec((1, d), lambda i: (0, 0)),
                  pl.BlockSpec((None, d, w.shape[2]), lambda i: (layer, 0, 0), pipeline_mode=pl.Buffered(1)),
                  pl.BlockSpec((1, n), lambda i: (0, 0))],
        out_specs=[pl.BlockSpec((tm, n), lambda i: (i, 0)),
                   pl.BlockSpec((tm, side_width), lambda i: (i, 0))],
        out_shape=[jax.ShapeDtypeStruct((m, n), _MXU),
                   jax.ShapeDtypeStruct((m, side_width), jnp.float32)],
        scratch_shapes=[pltpu.VMEM((d, n), _MXU)],
        compiler_params=_params("arbitrary"),
        name="norm_in_proj",
    )(x, g, w, col_scale)


def _out_mlp_kernel(x_ref, a_ref, b_ref, c_ref, d_ref, wo_ref, g_ref, wu_ref, wd_ref, gf_ref, o_ref,
                    *, final_norm, tf):
    mixed = jnp.concatenate([a_ref[...], b_ref[...], c_ref[...], d_ref[...]], axis=1)
    y = x_ref[...] + _dot(mixed, wo_ref[...])
    h = _rms(y, g_ref[...]).astype(_MXU)
    for c in range(wu_ref.shape[1] // tf):
        u = jnp.square(jnp.maximum(_dot(h, wu_ref[:, c * tf:(c + 1) * tf]), 0.0))
        y = y + _dot(u.astype(_MXU), wd_ref[c * tf:(c + 1) * tf, :])
    if final_norm:
        y = _rms(y, gf_ref[...])
    o_ref[...] = y


def _out_mlp(x, groups, w_out, g, w_up, w_down, layer, g_final, *, final_norm, tm=512, tf=2048):
    m, d = x.shape
    f = w_up.shape[2]
    gspec = pl.BlockSpec((tm, GROUP_WIDTH), lambda i: (i, 0))
    row = pl.BlockSpec((1, d), lambda i: (0, 0))

    def resident(rows, cols):
        return pl.BlockSpec((None, rows, cols), lambda i: (layer, 0, 0), pipeline_mode=pl.Buffered(1))

    return pl.pallas_call(
        functools.partial(_out_mlp_kernel, final_norm=final_norm, tf=tf),
        grid=(m // tm,),
        in_specs=[pl.BlockSpec((tm, d), lambda i: (i, 0)), gspec, gspec, gspec, gspec,
                  resident(d, d), row, resident(d, f), resident(f, d), row],
        out_specs=pl.BlockSpec((tm, d), lambda i: (i, 0)),
        out_shape=jax.ShapeDtypeStruct((m, d), jnp.float32),
        compiler_params=_params("parallel"),
        name="out_proj_mlp_residual",
    )(x, *groups, w_out, g, w_up, w_down, g_final)


def _t5_bucket(dist):
    n = jnp.maximum(dist, 0)
    max_exact = N_BUCKETS // 2
    nf = jnp.maximum(n, 1).astype(jnp.float32)
    large = max_exact + (jnp.log(nf / max_exact) / math.log(MAX_DISTANCE / max_exact)
                         * (N_BUCKETS - max_exact)).astype(jnp.int32)
    large = jnp.minimum(large, N_BUCKETS - 1)
    return jnp.where(n < max_exact, n, large)


def _bias_kernel(tab_ref, o_ref, *, row_stride, col_stride, offset, head0):
    nh, tr, tc = o_ref.shape
    for blk in range(tc // LANES):
        rows = lax.broadcasted_iota(jnp.int32, (tr, LANES), 0)
        cols = lax.broadcasted_iota(jnp.int32, (tr, LANES), 1) + (pl.program_id(0) * tc + blk * LANES)
        bucket = _t5_bucket(rows * row_stride + cols * col_stride + offset)
        for h in range(nh):
            row = tab_ref[head0 + h:head0 + h + 1, :]
            row = (row - row[:, N_BUCKETS - 1:N_BUCKETS]) * LOG2E
            o_ref[h, :, blk * LANES:(blk + 1) * LANES] = jnp.take_along_axis(
                jnp.broadcast_to(row, (tr, LANES)), bucket, axis=1, mode="promise_in_bounds")


def _bias_table(table_t, *, n_heads, head0, rows, cols, col_tile, row_stride, col_stride, offset):
    return pl.pallas_call(
        functools.partial(_bias_kernel, row_stride=row_stride, col_stride=col_stride,
                          offset=offset, head0=head0),
        grid=(cols // col_tile,),
        in_specs=[pl.BlockSpec(table_t.shape, lambda i: (0, 0))],
        out_specs=pl.BlockSpec((n_heads, rows, col_tile), lambda i: (0, 0, i)),
        out_shape=jax.ShapeDtypeStruct((n_heads, rows, cols), jnp.float32),
        compiler_params=_params("parallel"),
        name="t5_bias_tiles",
    )(table_t)


def _softmax_init(t):
    return (jnp.full((1, t), NEG_INF, jnp.float32), jnp.zeros((ACC_ROWS, t), jnp.float32))


class _KeyTile(NamedTuple):
    kts: Sequence[Any]
    vts: Sequence[Any]
    biases: Optional[Sequence[Any]] = None
    emasks: Optional[Sequence[Any]] = None
    qmasks: Optional[Sequence[Any]] = None


def _round_robin(lists):
    out = []
    for rank in range(max(map(len, lists), default=0)):
        out.extend(items[rank] for items in lists if rank < len(items))
    return out


def _softmax_jobs(t, jobs):
    built = {}

    def tiles_of(job, g):
        if (job, g) not in built:
            built[job, g] = [make() for make in jobs[job][2][g]]
        return built[job, g]

    def scores_of(job, g, c):
        qts = jobs[job][1]
        row = []
        for tile in tiles_of(job, g):
            s = _dot(tile.kts[c], qts[c])
            if tile.biases is not None:
                s = s + tile.biases[c]
            if tile.emasks is not None:
                s = jnp.where(tile.emasks[c], s, NEG_INF)
            row.append(s.astype(_MXU))
        return row

    def update(state, job, g, c, scores):
        m, acc = state
        m_new = m
        for tile, s in zip(tiles_of(job, g), scores):
            tile_max = jnp.max(s, axis=0, keepdims=True).astype(jnp.float32)
            if tile.qmasks is not None:
                tile_max = jnp.where(tile.qmasks[c], tile_max, NEG_INF)
            m_new = jnp.maximum(m_new, tile_max)
        seen = m_new > 0.5 * NEG_INF
        acc = jnp.exp2(m - m_new) * acc
        for tile, s in zip(tiles_of(job, g), scores):
            ok = seen if tile.qmasks is None else jnp.logical_and(seen, tile.qmasks[c])
            acc = acc + _dot(tile.vts[c], jnp.exp2(s - jnp.where(ok, m_new, BIG).astype(_MXU)))
        return m_new, acc

    units = _round_robin([[(job, g, c) for g in range(len(groups)) for c in range(n)]
                          for job, (n, _, groups) in enumerate(jobs)])
    lookahead = max(LOOKAHEAD, len(jobs))
    states = [[_softmax_init(t) for _ in range(n)] for n, _, _ in jobs]
    pending = {k: scores_of(*units[k]) for k in range(min(lookahead, len(units)))}
    for k, (job, g, c) in enumerate(units):
        if k + lookahead < len(units):
            pending[k + lookahead] = scores_of(*units[k + lookahead])
        states[job][c] = update(states[job][c], job, g, c, pending.pop(k))
        yield
    return states


def _pairs(items, size=2):
    return [items[p:p + size] for p in range(0, len(items), size)]


class _Mixer(NamedTuple):
    emit: Callable[..., Any]
    in_specs: Sequence[Any]
    operands: Sequence[Any]
    scratch: Sequence[Any]


def _mixer_kernel(*refs, emit):
    for _ in emit(*refs):
        pass


def _run_mixer(mixer, bsz, s, name):
    return pl.pallas_call(
        functools.partial(_mixer_kernel, emit=mixer.emit),
        grid=(bsz,),
        in_specs=list(mixer.in_specs),
        out_specs=pl.BlockSpec((1, s, GROUP_WIDTH), lambda b: (b, 0, 0)),
        out_shape=jax.ShapeDtypeStruct((bsz, s, GROUP_WIDTH), _MXU),
        scratch_shapes=list(mixer.scratch),
        compiler_params=_params("parallel"),
        name=name,
    )(*mixer.operands)


def _softmax_out(state):
    acc = state[1]
    return acc[:HEAD_DIM] / jnp.maximum(acc[HEAD_DIM:HEAD_DIM + 1], TINY)


def _top_k_rows(score, row_f, k):
    sel = jnp.zeros(score.shape, jnp.float32)
    for _ in range(k):
        mx = jnp.max(score, axis=0, keepdims=True)
        idx = jnp.min(jnp.where(score == mx, row_f, float(score.shape[0])), axis=0, keepdims=True)
        pick = row_f == idx
        sel = jnp.where(pick, 1.0, sel)
        score = jnp.where(pick, PICKED, score)
    return sel


def _now(fn):
    fn()


def _tile_iotas(t):
    return (lax.broadcasted_iota(jnp.int32, (t, t), 0), lax.broadcasted_iota(jnp.int32, (t, t), 1))


def _key_rows(ref, j, t):
    if isinstance(j, int):
        return ref[j * t:(j + 1) * t, :]
    return ref[pl.ds(pl.multiple_of(j * t, t), t), :]


def _transposed(ref_block):
    return ref_block.astype(jnp.float32).T


def _queries_t(q_ref, width, t):
    qt = _transposed(q_ref[0]).astype(_MXU)
    return [[qt[c * width:(c + 1) * width, r * t:(r + 1) * t] for c in range(GROUP_WIDTH // width)]
            for r in range(Q_TILES)]


def _fill_values_t(vt_ref, v_ref, col0, t):
    n_tiles, rows, _ = vt_ref.shape
    lane_block = (col0 // LANES) * LANES
    for c in range(n_tiles):
        blk = _transposed(v_ref[0, c * t:(c + 1) * t, lane_block:lane_block + LANES])
        vt_ref[c, 0:HEAD_DIM, :] = blk[col0 - lane_block:col0 - lane_block + HEAD_DIM].astype(vt_ref.dtype)
        if rows == ACC_ROWS:
            first = lax.broadcasted_iota(jnp.int32, (rows - HEAD_DIM, t), 0) == 0
            vt_ref[c, HEAD_DIM:rows, :] = jnp.where(first, 1.0, 0.0).astype(vt_ref.dtype)


def _group_specs(s, *column_blocks):
    return [pl.BlockSpec((1, s, GROUP_WIDTH), functools.partial(lambda cb, b: (b, 0, cb), cb))
            for cb in column_blocks]


def _bias_specs(t, head_group):
    spec = pl.BlockSpec((GROUP_HEADS, t, t), lambda b: (head_group, 0, 0))
    return [spec, spec]


def _store_heads(o_ref, outs_t):
    tiles = [jnp.concatenate(heads, axis=0) for heads in outs_t]
    o_ref[0] = jnp.concatenate(tiles, axis=1).T.astype(o_ref.dtype)


def _sb_emit(q_ref, k_ref, v_ref, o_ref, kb_ref, vt_ref):
    t = q_ref.shape[1] // Q_TILES
    key, qry = _tile_iotas(t)
    strict = key < qry
    later = jnp.where(qry > key, 1.0, 0.0).astype(_MXU)

    @_now
    def _():
        for h in _HEADS:
            kb_ref[h] = k_ref[0, :, h * HEAD_DIM:(h + 1) * HEAD_DIM].astype(kb_ref.dtype)
            _fill_values_t(vt_ref.at[h], v_ref, h * HEAD_DIM, t)

    def query_step(step):
        qts = _queries_t(q_ref, HEAD_DIM, t)
        units = _round_robin([[(r, step * Q_TILES + r, pair, h)
                               for pair in _pairs(list(range(step * Q_TILES + r, -1, -1))) for h in _HEADS]
                              for r in range(Q_TILES)])
        zero = (jnp.zeros((HEAD_DIM, t), jnp.float32), jnp.zeros((1, t), jnp.float32))
        carry = [[zero] * GROUP_HEADS for _ in range(Q_TILES)]
        zs, log_keeps, suffixes = {}, {}, {}

        def scores(u):
            r, _, pair, h = units[u]
            zs[u] = [_dot(_key_rows(kb_ref.at[h], j, t), qts[r][h]) for j in pair]

        def keeps(u):
            _, k, pair, _ = units[u]
            log_keeps[u], suffixes[u] = [], []
            for j, z in zip(pair, zs[u]):
                drop = jnp.maximum(jnp.log2(1.0 + jnp.exp2(jnp.minimum(z, SOFTPLUS_CLAMP))), z)
                if j == k:
                    drop = jnp.where(strict, drop, 0.0)
                log_keeps[u].append(drop)
                suffixes[u].append(_dot(later, drop.astype(_MXU)))

        def values(u):
            r, k, pair, h = units[u]
            acc, run = carry[r][h]
            weights = []
            for j, z, drop, suffix in zip(pair, zs.pop(u), log_keeps.pop(u), suffixes.pop(u)):
                a = jnp.exp2(z - drop - suffix + run)
                if j == k:
                    a = jnp.where(strict, a, 0.0)
                weights.append(a.astype(_MXU))
                run = run - (suffix[0:1] + drop[0:1])
            for j, w in zip(pair, weights):
                acc = acc + _dot(vt_ref[h, j], w)
            carry[r][h] = (acc, run)

        stages = (scores, keeps, values)
        for tick in range(len(units) + len(stages) - 1):
            for lag, stage in enumerate(stages):
                if 0 <= tick - lag < len(units):
                    stage(tick - lag)
            yield
        _store_heads(o_ref, [[c[0] for c in tile_carry] for tile_carry in carry])

    yield from query_step(0)


def _stick_breaking(proj, t=TILE):
    s = proj.shape[1]
    scratch = [pltpu.VMEM((GROUP_HEADS, s, HEAD_DIM), _MXU), pltpu.VMEM((GROUP_HEADS, s // t, HEAD_DIM, t), _MXU)]
    return _Mixer(_sb_emit, _group_specs(s, CB_SB_Q, CB_SB_K, CB_SB_V), [proj] * 3, scratch)


def _moba_emit(q_ref, k_ref, v_ref, bd_ref, bs_ref, o_ref, kb_ref, vt_ref, km_ref):
    t = q_ref.shape[1] // Q_TILES
    n_blk = k_ref.shape[1] // MOBA_BLOCK
    tiles_per_blk = MOBA_BLOCK // t
    blk_shift = int(math.log2(tiles_per_blk))

    @_now
    def _():
        km_ref[...] = jnp.zeros_like(km_ref)
        for h in _HEADS:
            lo, hi = h * HEAD_DIM, (h + 1) * HEAD_DIM
            kb_ref[h] = k_ref[0, :, lo:hi].astype(kb_ref.dtype)
            _fill_values_t(vt_ref.at[h], v_ref, lo, t)
            for n in range(n_blk):
                blk = k_ref[0, n * MOBA_BLOCK:(n + 1) * MOBA_BLOCK, lo:hi]
                km_ref[h, n:n + 1, :] = jnp.mean(blk.astype(jnp.float32), axis=0, keepdims=True)

    def query_step(step):
        key, qry = _tile_iotas(t)
        causal = key <= qry
        blk_row = lax.broadcasted_iota(jnp.int32, (km_ref.shape[1], t), 0)
        qt = _transposed(q_ref[0])
        jobs = []
        for r in range(Q_TILES):
            k = step * Q_TILES + r
            own = k >> blk_shift
            qts, sels = [], []
            for h in _HEADS:
                qf = qt[h * HEAD_DIM:(h + 1) * HEAD_DIM, r * t:(r + 1) * t]
                qts.append(qf.astype(_MXU))
                gate = _dot(km_ref[h], qf, precision=lax.Precision.HIGHEST)
                gate = jnp.where(blk_row < own, gate, NEG_INF)
                sel = _top_k_rows(gate, blk_row.astype(jnp.float32), min(MOBA_TOPK, n_blk - 1))
                sels.append(jnp.where(blk_row < own, sel, 0.0))

            def tile(own, sels, j, bias_ref=None, emask=None):
                n = j >> blk_shift
                return _KeyTile([_key_rows(kb_ref.at[h], j, t) for h in _HEADS], [vt_ref[h, j] for h in _HEADS],
                                None if bias_ref is None else [bias_ref[h] for h in _HEADS],
                                None if emask is None else [emask] * GROUP_HEADS,
                                None if n == own else [sels[h][n:n + 1] > 0.5 for h in _HEADS])

            tile = functools.partial(tile, own, sels)
            groups = _pairs([functools.partial(tile, j) for j in range(k - 1)], FAR_GROUP)
            groups.append(([functools.partial(tile, k - 1, bs_ref)] if k else [])
                          + [functools.partial(tile, k, bd_ref, causal)])
            jobs.append((GROUP_HEADS, qts, groups))
        all_states = yield from _softmax_jobs(t, jobs)
        _store_heads(o_ref, [[_softmax_out(st) for st in states] for states in all_states])

    yield from query_step(0)


def _kv_scratch(s, t, key_dim=HEAD_DIM, n_keys=GROUP_HEADS):
    return [pltpu.VMEM((n_keys, s, key_dim), _MXU), pltpu.VMEM((GROUP_HEADS, s // t, ACC_ROWS, t), _MXU)]


def _moba(proj, bias_diag, bias_sub, t=TILE):
    s = proj.shape[1]
    n_blk_pad = -(-(s // MOBA_BLOCK) // SUBLANES) * SUBLANES
    scratch = _kv_scratch(s, t) + [pltpu.VMEM((GROUP_HEADS, n_blk_pad, HEAD_DIM), jnp.float32)]
    return _Mixer(_moba_emit, _group_specs(s, CB_MB_Q, CB_MB_K, CB_MB_V) + _bias_specs(t, 0),
                  [proj] * 3 + [bias_diag, bias_sub], scratch)


def _diff_emit(lam_ref, g_ref, q_ref, k_ref, v_ref, bd_ref, bs_ref, o_ref, kb_ref, vt_ref, *, lambda_init):
    t = q_ref.shape[1] // Q_TILES
    key, qry = _tile_iotas(t)
    causal = key <= qry
    lv = lam_ref[...]
    lam = (jnp.exp(jnp.sum(lv[0:1] * lv[1:2], keepdims=True))
           - jnp.exp(jnp.sum(lv[2:3] * lv[3:4], keepdims=True)) + lambda_init)
    halves = range(2 * GROUP_HEADS)

    @_now
    def _():
        for c in halves:
            kb_ref[c] = k_ref[0, :, c * DIFF_HALF:(c + 1) * DIFF_HALF].astype(kb_ref.dtype)
        for h in _HEADS:
            _fill_values_t(vt_ref.at[h], v_ref, h * HEAD_DIM, t)

    def tile(j, bias_ref=None, emask=None):
        n = len(halves)
        return _KeyTile([_key_rows(kb_ref.at[c], j, t) for c in halves], [vt_ref[c // 2, j] for c in halves],
                        None if bias_ref is None else [bias_ref[c // 2] for c in halves],
                        None if emask is None else [emask] * n)

    def query_step(step):
        qts = _queries_t(q_ref, DIFF_HALF, t)
        jobs = []
        for r in range(Q_TILES):
            k = step * Q_TILES + r
            groups = _pairs([functools.partial(tile, j) for j in range(k - 1)], FAR_GROUP)
            groups.append(([functools.partial(tile, k - 1, bs_ref)] if k else [])
                          + [functools.partial(tile, k, bd_ref, causal)])
            jobs.append((len(halves), qts[r], groups))
        all_states = yield from _softmax_jobs(t, jobs)
        outs = []
        for states in all_states:
            heads = []
            for h in _HEADS:
                o = _softmax_out(states[2 * h]) - lam * _softmax_out(states[2 * h + 1])
                o = o * lax.rsqrt(jnp.mean(o * o, axis=0, keepdims=True) + NORM_EPS) * g_ref[...]
                heads.append(o * (1.0 - lambda_init))
            outs.append(heads)
        _store_heads(o_ref, outs)

    yield from query_step(0)


def _diff(proj, lam_params, subln, bias_diag, bias_sub, lambda_init, t=TILE):
    s = proj.shape[1]
    in_specs = ([pl.BlockSpec(lam_params.shape, lambda b: (0, 0)), pl.BlockSpec(subln.shape, lambda b: (0, 0))]
                + _group_specs(s, CB_DF_Q, CB_DF_K, CB_DF_V) + _bias_specs(t, 2))
    return _Mixer(functools.partial(_diff_emit, lambda_init=lambda_init), in_specs,
                  [lam_params, subln] + [proj] * 3 + [bias_diag, bias_sub],
                  _kv_scratch(s, t, key_dim=DIFF_HALF, n_keys=2 * GROUP_HEADS))


def _compress_kernel(kcv_ref, pk_ref, pv_ref, wk1_ref, wk2_ref, wv1_ref, wv2t_ref, kc_ref, vct_ref):
    n_chunk = kc_ref.shape[1]

    branches = ((0, pk_ref, wk1_ref), (HEAD_DIM, pv_ref, wv1_ref))
    tops = [jnp.zeros((n_chunk, w1_ref.shape[1]), jnp.float32) for _, _, w1_ref in branches]
    bots = list(tops)
    pack = 2 * LANES // HEAD_DIM
    for l0 in range(0, CMP_STRIDE, pack):
        tokens = [kcv_ref[0, pl.ds(l, n_chunk, stride=CMP_STRIDE), :] for l in range(l0, l0 + pack)]
        for n, (col0, p_ref, w1_ref) in enumerate(branches):
            for half, acc in ((0, tops), (CMP_STRIDE, bots)):
                x = jnp.concatenate([(tok[:, col0:col0 + HEAD_DIM] + p_ref[half + l:half + l + 1, :]).astype(_MXU)
                                     for l, tok in zip(range(l0, l0 + pack), tokens)], axis=1)
                acc[n] = acc[n] + _dot(x, w1_ref[(half + l0) * HEAD_DIM:(half + l0 + pack) * HEAD_DIM, :])
    hidden = [jax.nn.gelu(top + pltpu.roll(bot, n_chunk - 1, axis=0)).astype(_MXU) for top, bot in zip(tops, bots)]
    kc_ref[0] = _dot(hidden[0], wk2_ref[...])
    vct_ref[0] = _dot_nt(wv2t_ref[...], hidden[1])


def _compress(kcv, pos_k, pos_v, wk1, wk2, wv1, wv2t, layer):
    bsz, s, width = kcv.shape
    n_chunk = s // CMP_STRIDE

    def full(a):
        return pl.BlockSpec((None,) + a.shape[1:], lambda b: (layer, 0, 0))

    return pl.pallas_call(
        _compress_kernel,
        grid=(bsz,),
        in_specs=[pl.BlockSpec((1, s, width), lambda b: (b, 0, 0)),
                  full(pos_k), full(pos_v), full(wk1), full(wk2), full(wv1), full(wv2t)],
        out_specs=[pl.BlockSpec((1, n_chunk, HEAD_DIM), lambda b: (b, 0, 0)),
                   pl.BlockSpec((1, HEAD_DIM, n_chunk), lambda b: (b, 0, 0))],
        out_shape=[jax.ShapeDtypeStruct((bsz, n_chunk, HEAD_DIM), jnp.float32),
                   jax.ShapeDtypeStruct((bsz, HEAD_DIM, n_chunk), jnp.float32)],
        compiler_params=_params("parallel"),
        name="nsa_compress",
    )(kcv, pos_k, pos_v, wk1, wk2, wv1, wv2t)


def _nsa_emit(q_ref, kva_ref, kvb_ref, kc_ref, vct_ref, bd_ref, bs_ref, bc_ref, cover_ref, e_ref,
                o_ref, ks_ref, vst_ref, kw_ref, vwt_ref):
    t = q_ref.shape[1] // Q_TILES
    key, qry = _tile_iotas(t)
    causal = key <= qry
    ks_col, vs_col, kw_col, vw_col, gate_col = 2 * HEAD_DIM, 3 * HEAD_DIM, 0, HEAD_DIM, 2 * HEAD_DIM

    @_now
    def _():
        ks_ref[:, :HEAD_DIM] = kva_ref[0, :, ks_col:ks_col + HEAD_DIM].astype(ks_ref.dtype)
        ks_ref[:, HEAD_DIM:] = e_ref[...]
        kw_ref[...] = kvb_ref[0, :, kw_col:kw_col + HEAD_DIM].astype(kw_ref.dtype)
        _fill_values_t(vst_ref, kva_ref, vs_col, t)
        _fill_values_t(vwt_ref, kvb_ref, vw_col, t)

    def tile(k_ref, vt_ref, j, bias_ref=None, emask=None):
        n = GROUP_HEADS
        return _KeyTile([_key_rows(k_ref, j, t)] * n, [vt_ref[j]] * n,
                        None if bias_ref is None else [bias_ref[h] for h in _HEADS],
                        None if emask is None else [emask] * n)

    def query_step(step):
        qts = _queries_t(q_ref, HEAD_DIM, t)
        n_cmp = kc_ref.shape[1]
        n_slc = cover_ref.shape[0]
        kc = kc_ref[0].astype(_MXU)
        vct = vct_ref[0].astype(_MXU)
        c_row = lax.broadcasted_iota(jnp.int32, (n_cmp, t), 0)
        c_col = lax.broadcasted_iota(jnp.int32, (n_cmp, t), 1)
        s_row = lax.broadcasted_iota(jnp.int32, (n_slc, t), 0)
        s_col = lax.broadcasted_iota(jnp.int32, (n_slc, t), 1)
        tiles = [step * Q_TILES + r for r in range(Q_TILES)]

        o_cmp, importance = [], []
        for r, k in enumerate(tiles):
            visible = c_col + k * t >= c_row * CMP_STRIDE + (CMP_LEN - 1)
            cmp_scores = [_dot(kc, qts[r][h]) for h in _HEADS]
            cmp_probs = []
            p_sum = jnp.zeros((n_cmp, t), jnp.float32)
            for h in _HEADS:
                sc = jnp.where(visible, cmp_scores[h] + bc_ref[h, :, r * t:(r + 1) * t], NEG_INF)
                e = jnp.where(visible, jnp.exp2(sc - jnp.max(sc, axis=0, keepdims=True)), 0.0)
                p = e / jnp.maximum(jnp.sum(e, axis=0, keepdims=True), TINY)
                cmp_probs.append(p.astype(_MXU))
                p_sum = p_sum + p
            o_cmp.append([_dot(vct, cmp_probs[h]) for h in _HEADS])
            importance.append(_dot(cover_ref[...], p_sum, precision=lax.Precision.HIGHEST))

        n_back = WINDOW // t
        jobs = []
        for r, k in enumerate(tiles):
            window = [functools.partial(tile, kw_ref, vwt_ref, k - n_back, None, qry < key)] if k >= n_back else []
            for back in range(min(n_back - 1, k), 0, -1):
                window.append(functools.partial(tile, kw_ref, vwt_ref, k - back, bs_ref if back == 1 else None))
            window.append(functools.partial(tile, kw_ref, vwt_ref, k, bd_ref, causal))
            jobs.append((GROUP_HEADS, qts[r], [window]))
        win_states = yield from _softmax_jobs(t, jobs)
        o_win = [[_softmax_out(st) for st in states] for states in win_states]

        jobs = []
        for r, k in enumerate(tiles):
            own = jnp.right_shift(s_col + k * t, int(math.log2(SLC_LEN)))
            score = jnp.where(s_row == own, FORCE, jnp.where(s_row < own, importance[r], NEG_INF))
            sel = _top_k_rows(score, s_row.astype(jnp.float32), min(SLC_TOPN, n_slc))
            penalty = jnp.where(sel > 0.5, 0.0, NEG_INF).astype(_MXU)
            q_aug = [jnp.concatenate([qts[r][h], penalty], axis=0) for h in _HEADS]
            groups = _pairs([functools.partial(tile, ks_ref, vst_ref, j) for j in range(k - 1)], FAR_GROUP)
            groups.append(([functools.partial(tile, ks_ref, vst_ref, k - 1, bs_ref)] if k else [])
                          + [functools.partial(tile, ks_ref, vst_ref, k, bd_ref, causal)])
            jobs.append((GROUP_HEADS, q_aug, groups))
        slc_states = yield from _softmax_jobs(t, jobs)
        o_slc = [[_softmax_out(st) for st in states] for states in slc_states]

        gates = _transposed(kvb_ref[0, :, (gate_col // LANES) * LANES:(gate_col // LANES + 1) * LANES])
        gates = 1.0 / (1.0 + jnp.exp(-gates[gate_col % LANES:gate_col % LANES + N_GATES + 4]))
        outs = []
        for r in range(Q_TILES):
            heads = []
            for h in _HEADS:
                g = [gates[br * GROUP_HEADS + h:br * GROUP_HEADS + h + 1, r * t:(r + 1) * t] for br in range(3)]
                heads.append(g[0] * o_cmp[r][h] + g[1] * o_slc[r][h] + g[2] * o_win[r][h])
            outs.append(heads)
        _store_heads(o_ref, outs)

    yield from query_step(0)


def _nsa(proj, kc, vc_t, bias_diag, bias_sub, bias_cmp, cover_t, expand, t=TILE):
    s = proj.shape[1]
    n_cmp = kc.shape[1]
    values_t = pltpu.VMEM((s // t, ACC_ROWS, t), _MXU)
    in_specs = (_group_specs(s, CB_NS_Q, CB_NS_A, CB_NS_B)
                + [pl.BlockSpec((1, n_cmp, HEAD_DIM), lambda b: (b, 0, 0)),
                   pl.BlockSpec((1, HEAD_DIM, n_cmp), lambda b: (b, 0, 0))]
                + _bias_specs(t, 1)
                + [pl.BlockSpec(bias_cmp.shape, lambda b: (0, 0, 0)),
                   pl.BlockSpec(cover_t.shape, lambda b: (0, 0)),
                   pl.BlockSpec(expand.shape, lambda b: (0, 0))])
    scratch = [pltpu.VMEM((s, HEAD_DIM + expand.shape[1]), _MXU), values_t, pltpu.VMEM((s, HEAD_DIM), _MXU), values_t]
    return _Mixer(_nsa_emit, in_specs, [proj] * 3 + [kc, vc_t, bias_diag, bias_sub, bias_cmp, cover_t, expand],
                  scratch)


def _nsa_constants(s):
    n_cmp = (s - CMP_LEN) // CMP_STRIDE + 1
    n_slc = s // SLC_LEN
    assert n_cmp + 1 == s // CMP_STRIDE and n_slc % SUBLANES == 0
    c_start = np.arange(n_cmp) * CMP_STRIDE
    s_start = np.arange(n_slc) * SLC_LEN
    cover = np.clip(np.minimum((c_start + CMP_LEN - 1)[:, None], (s_start + SLC_LEN - 1)[None, :])
                    - np.maximum(c_start[:, None], s_start[None, :]) + 1, 0, None) / CMP_LEN
    cover_t = np.zeros((n_slc, n_cmp + 1), np.float32)
    cover_t[:, :n_cmp] = cover.T
    expand = (np.arange(s)[:, None] // SLC_LEN == np.arange(n_slc)[None, :]).astype(np.float32)
    return jnp.asarray(cover_t), jnp.asarray(expand, _MXU)


def kernel(x, w_in, w_out, w_up, w_down, norm_attn, norm_mlp, cmp_pos_k, cmp_pos_v, cmp_k_w1, cmp_k_w2,
           cmp_v_w1, cmp_v_w2, diff_lambda, diff_subln, rel_bias, final_norm):
    bsz, s, d = x.shape
    depth = w_in.shape[0]
    t = TILE
    n_chunk = s // CMP_STRIDE
    assert s == Q_TILES * t and MOBA_BLOCK % t == 0 and WINDOW % t == 0 and t >= MAX_DISTANCE

    assert w_in.shape[2] + PAD_COLS == D_IN_PAD
    w_in_c = jnp.pad(w_in.astype(_MXU), ((0, 0), (0, 0), (0, PAD_COLS)))
    w_out_c, w_up_c, w_down_c = (w.astype(_MXU) for w in (w_out, w_up, w_down))
    wk1, wk2 = cmp_k_w1.astype(_MXU), cmp_k_w2.astype(_MXU)
    wv1, wv2t = cmp_v_w1.astype(_MXU), jnp.swapaxes(cmp_v_w2, 1, 2).astype(_MXU)

    table_t = jnp.pad(rel_bias.T, ((0, 0), (0, LANES - N_BUCKETS)))
    tiles = dict(n_heads=rel_bias.shape[1], head0=0, rows=t, cols=t, col_tile=t, row_stride=-1, col_stride=1)
    bias_diag = _bias_table(table_t, offset=0, **tiles)
    bias_sub = _bias_table(table_t, offset=t, **tiles)
    bias_cmp = _bias_table(table_t, n_heads=GROUP_HEADS, head0=GROUP_HEADS, rows=n_chunk, cols=s,
                           col_tile=t, row_stride=-CMP_STRIDE, col_stride=1, offset=-(CMP_LEN - 1))
    cover_t, expand = _nsa_constants(s)
    col_scale = np.ones((1, D_IN_PAD), np.float32)
    for cb, width in ((CB_SB_Q, HEAD_DIM), (CB_MB_Q, HEAD_DIM), (CB_NS_Q, HEAD_DIM), (CB_DF_Q, DIFF_HALF)):
        col_scale[:, cb * GROUP_WIDTH:(cb + 1) * GROUP_WIDTH] = width ** -0.5 * LOG2E
    col_scale = jnp.asarray(col_scale)

    x2 = x.reshape(bsz * s, d)
    for layer in range(depth):
        proj, kcv = _norm_matmul(x2, norm_attn[layer][None], w_in_c, layer, col_scale,
                                 gap=(COLS_BEFORE_PAD, COLS_BEFORE_PAD + PAD_COLS),
                                 side_col=CB_NS_A * GROUP_WIDTH, side_width=2 * HEAD_DIM)
        proj = proj.reshape(bsz, s, D_IN_PAD)
        kc, vc_t = _compress(kcv.reshape(bsz, s, 2 * HEAD_DIM), cmp_pos_k, cmp_pos_v, wk1, wk2, wv1, wv2t, layer)
        lambda_init = 0.8 - 0.6 * math.exp(-0.3 * layer)
        o_sb = _run_mixer(_stick_breaking(proj), bsz, s, "stick_breaking")
        o_mb = _run_mixer(_moba(proj, bias_diag, bias_sub), bsz, s, "moba")
        o_ns = _run_mixer(_nsa(proj, kc, vc_t, bias_diag, bias_sub, bias_cmp, cover_t, expand), bsz, s, "nsa")
        o_df = _run_mixer(_diff(proj, diff_lambda[layer], diff_subln[layer][:, None], bias_diag, bias_sub,
                                lambda_init), bsz, s, "diff_attention")
        groups = [o.reshape(bsz * s, GROUP_WIDTH) for o in (o_sb, o_mb, o_ns, o_df)]
        x2 = _out_mlp(x2, groups, w_out_c, norm_mlp[layer][None], w_up_c, w_down_c, layer,
                      final_norm[None], final_norm=(layer == depth - 1))
    return x2.reshape(bsz, s, d)
```

```python
import functools
import math
from typing import Any, Callable, NamedTuple, Optional, Sequence

import numpy as np
import jax
import jax.numpy as jnp
from jax import lax
from jax.experimental import pallas as pl
from jax.experimental.pallas import tpu as pltpu

HEAD_DIM = 64
GROUP_HEADS = 4
GROUP_WIDTH = GROUP_HEADS * HEAD_DIM
NORM_EPS = 1e-6
NEG_INF = -1e30
BIG = 1e30
FORCE = 1e30
TINY = 1e-30
SOFTPLUS_CLAMP = 64.0
PICKED = -3e38
LOG2E = math.log2(math.e)
N_BUCKETS = 32
MAX_DISTANCE = 128
MOBA_BLOCK = 256
MOBA_TOPK = 3
CMP_LEN = 32
CMP_STRIDE = 16
SLC_LEN = 64
SLC_TOPN = 4
WINDOW = 512
DIFF_HALF = HEAD_DIM // 2
LANES = 128
SUBLANES = 8
TILE = 256
BF16_ROWS = 16
ACC_ROWS = HEAD_DIM + BF16_ROWS
N_GATES = 3 * GROUP_HEADS
COLS_BEFORE_PAD = 9 * GROUP_WIDTH - 2 * HEAD_DIM + N_GATES
PAD_COLS = 2 * HEAD_DIM - N_GATES
CB_SB_Q, CB_SB_K, CB_SB_V, CB_MB_Q, CB_MB_K, CB_MB_V, CB_NS_Q, CB_NS_A, CB_NS_B, CB_DF_Q, CB_DF_K, CB_DF_V = range(12)
D_IN_PAD = 12 * GROUP_WIDTH

_MXU = jnp.bfloat16
_VMEM_LIMIT = 56 * 1024 * 1024
_HEADS = range(GROUP_HEADS)
Q_TILES = 8
LOOKAHEAD = 2


def _dot(a, b, precision=None):
    return jnp.dot(a, b, precision=precision, preferred_element_type=jnp.float32)


def _dot_nt(a, b, precision=None):
    return lax.dot_general(a, b, (((1,), (1,)), ((), ())), precision=precision,
                           preferred_element_type=jnp.float32)


def _rms(x, g):
    return x * lax.rsqrt(jnp.mean(x * x, axis=-1, keepdims=True) + NORM_EPS) * g


def _params(*sem):
    return pltpu.CompilerParams(dimension_semantics=sem, vmem_limit_bytes=_VMEM_LIMIT)


def _norm_matmul_kernel(x_ref, g_ref, w_ref, scale_ref, o_ref, side_ref, wp_ref, *, tn, side_col, gap):
    @pl.when(pl.program_id(0) == 0)
    def _():
        lo, hi = gap
        wp_ref[:, :lo] = w_ref[:, :lo]
        wp_ref[:, lo:hi] = jnp.zeros((wp_ref.shape[0], hi - lo), wp_ref.dtype)
        wp_ref[:, hi:] = w_ref[:, lo:lo + wp_ref.shape[1] - hi]

    h = _rms(x_ref[...], g_ref[...]).astype(_MXU)
    for j in range(wp_ref.shape[1] // tn):
        cols = slice(j * tn, (j + 1) * tn)
        acc = _dot(h, wp_ref[:, cols])
        o_ref[:, cols] = (acc * scale_ref[:, cols]).astype(o_ref.dtype)
        if j * tn <= side_col < (j + 1) * tn:
            side_ref[...] = acc[:, side_col - j * tn:side_col - j * tn + side_ref.shape[1]]


def _norm_matmul(x, g, w, layer, col_scale, *, gap, side_col, side_width, tm=512, tn=1024):
    m, d = x.shape
    n = w.shape[2]
    assert side_col % LANES == 0 and side_col // tn == (side_col + side_width - 1) // tn
    return pl.pallas_call(
        functools.partial(_norm_matmul_kernel, tn=tn, side_col=side_col, gap=gap),
        grid=(m // tm,),
        in_specs=[pl.BlockSpec((tm, d), lambda i: (i, 0)),
                  pl.BlockSpec((1, d), lambda i: (0, 0)),
                  pl.BlockSpec((None, d, w.shape[2]), lambda i: (layer, 0, 0), pipeline_mode=pl.Buffered(1)),
                  pl.BlockSpec((1, n), lambda i: (0, 0))],
        out_specs=[pl.BlockSpec((tm, n), lambda i: (i, 0)),
                   pl.BlockSpec((tm, side_width), lambda i: (i, 0))],
        out_shape=[jax.ShapeDtypeStruct((m, n), _MXU),
                   jax.ShapeDtypeStruct((m, side_width), jnp.float32)],
        scratch_shapes=[pltpu.VMEM((d, n), _MXU)],
        compiler_params=_params("arbitrary"),
        name="norm_in_proj",
    )(x, g, w, col_scale)


def _out_mlp_kernel(x_ref, a_ref, b_ref, c_ref, d_ref, wo_ref, g_ref, wu_ref, wd_ref, gf_ref, o_ref,
                    *, final_norm, tf):
    mixed = jnp.concatenate([a_ref[...], b_ref[...], c_ref[...], d_ref[...]], axis=1)
    y = x_ref[...] + _dot(mixed, wo_ref[...])
    h = _rms(y, g_ref[...]).astype(_MXU)
    for c in range(wu_ref.shape[1] // tf):
        u = jnp.square(jnp.maximum(_dot(h, wu_ref[:, c * tf:(c + 1) * tf]), 0.0))
        y = y + _dot(u.astype(_MXU), wd_ref[c * tf:(c + 1) * tf, :])
    if final_norm:
        y = _rms(y, gf_ref[...])
    o_ref[...] = y


def _out_mlp(x, groups, w_out, g, w_up, w_down, layer, g_final, *, final_norm, tm=512, tf=2048):
    m, d = x.shape
    f = w_up.shape[2]
    gspec = pl.BlockSpec((tm, GROUP_WIDTH), lambda i: (i, 0))
    row = pl.BlockSpec((1, d), lambda i: (0, 0))

    def resident(rows, cols):
        return pl.BlockSpec((None, rows, cols), lambda i: (layer, 0, 0), pipeline_mode=pl.Buffered(1))

    return pl.pallas_call(
        functools.partial(_out_mlp_kernel, final_norm=final_norm, tf=tf),
        grid=(m // tm,),
        in_specs=[pl.BlockSpec((tm, d), lambda i: (i, 0)), gspec, gspec, gspec, gspec,
                  resident(d, d), row, resident(d, f), resident(f, d), row],
        out_specs=pl.BlockSpec((tm, d), lambda i: (i, 0)),
        out_shape=jax.ShapeDtypeStruct((m, d), jnp.float32),
        compiler_params=_params("parallel"),
        name="out_proj_mlp_residual",
    )(x, *groups, w_out, g, w_up, w_down, g_final)


def _t5_bucket(dist):
    n = jnp.maximum(dist, 0)
    max_exact = N_BUCKETS // 2
    nf = jnp.maximum(n, 1).astype(jnp.float32)
    large = max_exact + (jnp.log(nf / max_exact) / math.log(MAX_DISTANCE / max_exact)
                         * (N_BUCKETS - max_exact)).astype(jnp.int32)
    large = jnp.minimum(large, N_BUCKETS - 1)
    return jnp.where(n < max_exact, n, large)


def _bias_kernel(tab_ref, o_ref, *, row_stride, col_stride, offset, head0):
    nh, tr, tc = o_ref.shape
    for blk in range(tc // LANES):
        rows = lax.broadcasted_iota(jnp.int32, (tr, LANES), 0)
        cols = lax.broadcasted_iota(jnp.int32, (tr, LANES), 1) + (pl.program_id(0) * tc + blk * LANES)
        bucket = _t5_bucket(rows * row_stride + cols * col_stride + offset)
        for h in range(nh):
            row = tab_ref[head0 + h:head0 + h + 1, :]
            row = (row - row[:, N_BUCKETS - 1:N_BUCKETS]) * LOG2E
            o_ref[h, :, blk * LANES:(blk + 1) * LANES] = jnp.take_along_axis(
                jnp.broadcast_to(row, (tr, LANES)), bucket, axis=1, mode="promise_in_bounds")


def _bias_table(table_t, *, n_heads, head0, rows, cols, col_tile, row_stride, col_stride, offset):
    return pl.pallas_call(
        functools.partial(_bias_kernel, row_stride=row_stride, col_stride=col_stride,
                          offset=offset, head0=head0),
        grid=(cols // col_tile,),
        in_specs=[pl.BlockSpec(table_t.shape, lambda i: (0, 0))],
        out_specs=pl.BlockSpec((n_heads, rows, col_tile), lambda i: (0, 0, i)),
        out_shape=jax.ShapeDtypeStruct((n_heads, rows, cols), jnp.float32),
        compiler_params=_params("parallel"),
        name="t5_bias_tiles",
    )(table_t)


def _softmax_init(t):
    return (jnp.full((1, t), NEG_INF, jnp.float32), jnp.zeros((ACC_ROWS, t), jnp.float32))


class _KeyTile(NamedTuple):
    kts: Sequence[Any]
    vts: Sequence[Any]
    biases: Optional[Sequence[Any]] = None
    emasks: Optional[Sequence[Any]] = None
    qmasks: Optional[Sequence[Any]] = None


def _round_robin(lists):
    out = []
    for rank in range(max(map(len, lists), default=0)):
        out.extend(items[rank] for items in lists if rank < len(items))
    return out


def _softmax_jobs(t, jobs):
    built = {}

    def tiles_of(job, g):
        if (job, g) not in built:
            built[job, g] = [make() for make in jobs[job][2][g]]
        return built[job, g]

    def scores_of(job, g, c):
        qts = jobs[job][1]
        row = []
        for tile in tiles_of(job, g):
            s = _dot(tile.kts[c], qts[c])
            if tile.biases is not None:
                s = s + tile.biases[c]
            if tile.emasks is not None:
                s = jnp.where(tile.emasks[c], s, NEG_INF)
            row.append(s.astype(_MXU))
        return row

    def update(state, job, g, c, scores):
        m, acc = state
        m_new = m
        for tile, s in zip(tiles_of(job, g), scores):
            tile_max = jnp.max(s, axis=0, keepdims=True).astype(jnp.float32)
            if tile.qmasks is not None:
                tile_max = jnp.where(tile.qmasks[c], tile_max, NEG_INF)
            m_new = jnp.maximum(m_new, tile_max)
        seen = m_new > 0.5 * NEG_INF
        acc = jnp.exp2(m - m_new) * acc
        for tile, s in zip(tiles_of(job, g), scores):
            ok = seen if tile.qmasks is None else jnp.logical_and(seen, tile.qmasks[c])
            acc = acc + _dot(tile.vts[c], jnp.exp2(s - jnp.where(ok, m_new, BIG).astype(_MXU)))
        return m_new, acc

    units = _round_robin([[(job, g, c) for g in range(len(groups)) for c in range(n)]
                          for job, (n, _, groups) in enumerate(jobs)])
    lookahead = max(LOOKAHEAD, len(jobs))
    states = [[_softmax_init(t) for _ in range(n)] for n, _, _ in jobs]
    pending = {k: scores_of(*units[k]) for k in range(min(lookahead, len(units)))}
    for k, (job, g, c) in enumerate(units):
        if k + lookahead < len(units):
            pending[k + lookahead] = scores_of(*units[k + lookahead])
        states[job][c] = update(states[job][c], job, g, c, pending.pop(k))
        yield
    return states


def _pairs(items):
    return [items[p:p + 2] for p in range(0, len(items), 2)]


class _Mixer(NamedTuple):
    emit: Callable[..., Any]
    in_specs: Sequence[Any]
    operands: Sequence[Any]
    scratch: Sequence[Any]


def _mixer_kernel(*refs, emit):
    for _ in emit(*refs):
        pass


def _run_mixer(mixer, bsz, s, name):
    return pl.pallas_call(
        functools.partial(_mixer_kernel, emit=mixer.emit),
        grid=(bsz,),
        in_specs=list(mixer.in_specs),
        out_specs=pl.BlockSpec((1, s, GROUP_WIDTH), lambda b: (b, 0, 0)),
        out_shape=jax.ShapeDtypeStruct((bsz, s, GROUP_WIDTH), _MXU),
        scratch_shapes=list(mixer.scratch),
        compiler_params=_params("parallel"),
        name=name,
    )(*mixer.operands)


def _softmax_out(state):
    acc = state[1]
    return acc[:HEAD_DIM] / jnp.maximum(acc[HEAD_DIM:HEAD_DIM + 1], TINY)


def _top_k_rows(score, row_f, k):
    sel = jnp.zeros(score.shape, jnp.float32)
    for _ in range(k):
        mx = jnp.max(score, axis=0, keepdims=True)
        idx = jnp.min(jnp.where(score == mx, row_f, float(score.shape[0])), axis=0, keepdims=True)
        pick = row_f == idx
        sel = jnp.where(pick, 1.0, sel)
        score = jnp.where(pick, PICKED, score)
    return sel


def _now(fn):
    fn()


def _tile_iotas(t):
    return (lax.broadcasted_iota(jnp.int32, (t, t), 0), lax.broadcasted_iota(jnp.int32, (t, t), 1))


def _key_rows(ref, j, t):
    if isinstance(j, int):
        return ref[j * t:(j + 1) * t, :]
    return ref[pl.ds(pl.multiple_of(j * t, t), t), :]


def _transposed(ref_block):
    return ref_block.astype(jnp.float32).T


def _queries_t(q_ref, width, t):
    qt = _transposed(q_ref[0]).astype(_MXU)
    return [[qt[c * width:(c + 1) * width, r * t:(r + 1) * t] for c in range(GROUP_WIDTH // width)]
            for r in range(Q_TILES)]


def _fill_values_t(vt_ref, v_ref, col0, t):
    n_tiles, rows, _ = vt_ref.shape
    lane_block = (col0 // LANES) * LANES
    for c in range(n_tiles):
        blk = _transposed(v_ref[0, c * t:(c + 1) * t, lane_block:lane_block + LANES])
        vt_ref[c, 0:HEAD_DIM, :] = blk[col0 - lane_block:col0 - lane_block + HEAD_DIM].astype(vt_ref.dtype)
        if rows == ACC_ROWS:
            first = lax.broadcasted_iota(jnp.int32, (rows - HEAD_DIM, t), 0) == 0
            vt_ref[c, HEAD_DIM:rows, :] = jnp.where(first, 1.0, 0.0).astype(vt_ref.dtype)


def _group_specs(s, *column_blocks):
    return [pl.BlockSpec((1, s, GROUP_WIDTH), functools.partial(lambda cb, b: (b, 0, cb), cb))
            for cb in column_blocks]


def _bias_specs(t, head_group):
    spec = pl.BlockSpec((GROUP_HEADS, t, t), lambda b: (head_group, 0, 0))
    return [spec, spec]


def _store_heads(o_ref, outs_t):
    tiles = [jnp.concatenate(heads, axis=0) for heads in outs_t]
    o_ref[0] = jnp.concatenate(tiles, axis=1).T.astype(o_ref.dtype)


def _sb_emit(q_ref, k_ref, v_ref, o_ref, kb_ref, vt_ref):
    t = q_ref.shape[1] // Q_TILES
    key, qry = _tile_iotas(t)
    strict = key < qry
    later = jnp.where(qry > key, 1.0, 0.0).astype(_MXU)

    @_now
    def _():
        for h in _HEADS:
            kb_ref[h] = k_ref[0, :, h * HEAD_DIM:(h + 1) * HEAD_DIM].astype(kb_ref.dtype)
            _fill_values_t(vt_ref.at[h], v_ref, h * HEAD_DIM, t)

    def query_step(step):
        qts = _queries_t(q_ref, HEAD_DIM, t)
        units = _round_robin([[(r, step * Q_TILES + r, pair, h)
                               for pair in _pairs(list(range(step * Q_TILES + r, -1, -1))) for h in _HEADS]
                              for r in range(Q_TILES)])
        zero = (jnp.zeros((HEAD_DIM, t), jnp.float32), jnp.zeros((1, t), jnp.float32))
        carry = [[zero] * GROUP_HEADS for _ in range(Q_TILES)]
        zs, log_keeps, suffixes = {}, {}, {}

        def scores(u):
            r, _, pair, h = units[u]
            zs[u] = [_dot(_key_rows(kb_ref.at[h], j, t), qts[r][h]) for j in pair]

        def keeps(u):
            _, k, pair, _ = units[u]
            log_keeps[u], suffixes[u] = [], []
            for j, z in zip(pair, zs[u]):
                drop = jnp.maximum(jnp.log2(1.0 + jnp.exp2(jnp.minimum(z, SOFTPLUS_CLAMP))), z)
                if j == k:
                    drop = jnp.where(strict, drop, 0.0)
                log_keeps[u].append(drop)
                suffixes[u].append(_dot(later, drop.astype(_MXU)))

        def values(u):
            r, k, pair, h = units[u]
            acc, run = carry[r][h]
            weights = []
            for j, z, drop, suffix in zip(pair, zs.pop(u), log_keeps.pop(u), suffixes.pop(u)):
                a = jnp.exp2(z - drop - suffix + run)
                if j == k:
                    a = jnp.where(strict, a, 0.0)
                weights.append(a.astype(_MXU))
                run = run - (suffix[0:1] + drop[0:1])
            for j, w in zip(pair, weights):
                acc = acc + _dot(vt_ref[h, j], w)
            carry[r][h] = (acc, run)

        stages = (scores, keeps, values)
        for tick in range(len(units) + len(stages) - 1):
            for lag, stage in enumerate(stages):
                if 0 <= tick - lag < len(units):
                    stage(tick - lag)
            yield
        _store_heads(o_ref, [[c[0] for c in tile_carry] for tile_carry in carry])

    yield from query_step(0)


def _stick_breaking(proj, t=TILE):
    s = proj.shape[1]
    scratch = [pltpu.VMEM((GROUP_HEADS, s, HEAD_DIM), _MXU), pltpu.VMEM((GROUP_HEADS, s // t, HEAD_DIM, t), _MXU)]
    return _Mixer(_sb_emit, _group_specs(s, CB_SB_Q, CB_SB_K, CB_SB_V), [proj] * 3, scratch)


def _moba_emit(q_ref, k_ref, v_ref, bd_ref, bs_ref, o_ref, kb_ref, vt_ref, km_ref):
    t = q_ref.shape[1] // Q_TILES
    n_blk = k_ref.shape[1] // MOBA_BLOCK
    tiles_per_blk = MOBA_BLOCK // t
    blk_shift = int(math.log2(tiles_per_blk))

    @_now
    def _():
        km_ref[...] = jnp.zeros_like(km_ref)
        for h in _HEADS:
            lo, hi = h * HEAD_DIM, (h + 1) * HEAD_DIM
            kb_ref[h] = k_ref[0, :, lo:hi].astype(kb_ref.dtype)
            _fill_values_t(vt_ref.at[h], v_ref, lo, t)
            for n in range(n_blk):
                blk = k_ref[0, n * MOBA_BLOCK:(n + 1) * MOBA_BLOCK, lo:hi]
                km_ref[h, n:n + 1, :] = jnp.mean(blk.astype(jnp.float32), axis=0, keepdims=True)

    def query_step(step):
        key, qry = _tile_iotas(t)
        causal = key <= qry
        blk_row = lax.broadcasted_iota(jnp.int32, (km_ref.shape[1], t), 0)
        qt = _transposed(q_ref[0])
        jobs = []
        for r in range(Q_TILES):
            k = step * Q_TILES + r
            own = k >> blk_shift
            qts, sels = [], []
            for h in _HEADS:
                qf = qt[h * HEAD_DIM:(h + 1) * HEAD_DIM, r * t:(r + 1) * t]
                qts.append(qf.astype(_MXU))
                gate = _dot(km_ref[h], qf, precision=lax.Precision.HIGHEST)
                gate = jnp.where(blk_row < own, gate, NEG_INF)
                sel = _top_k_rows(gate, blk_row.astype(jnp.float32), min(MOBA_TOPK, n_blk - 1))
                sels.append(jnp.where(blk_row < own, sel, 0.0))

            def tile(own, sels, j, bias_ref=None, emask=None):
                n = j >> blk_shift
                return _KeyTile([_key_rows(kb_ref.at[h], j, t) for h in _HEADS], [vt_ref[h, j] for h in _HEADS],
                                None if bias_ref is None else [bias_ref[h] for h in _HEADS],
                                None if emask is None else [emask] * GROUP_HEADS,
                                None if n == own else [sels[h][n:n + 1] > 0.5 for h in _HEADS])

            tile = functools.partial(tile, own, sels)
            groups = _pairs([functools.partial(tile, j) for j in range(k - 1)])
            groups.append(([functools.partial(tile, k - 1, bs_ref)] if k else [])
                          + [functools.partial(tile, k, bd_ref, causal)])
            jobs.append((GROUP_HEADS, qts, groups))
        all_states = yield from _softmax_jobs(t, jobs)
        _store_heads(o_ref, [[_softmax_out(st) for st in states] for states in all_states])

    yield from query_step(0)


def _kv_scratch(s, t, key_dim=HEAD_DIM, n_keys=GROUP_HEADS):
    return [pltpu.VMEM((n_keys, s, key_dim), _MXU), pltpu.VMEM((GROUP_HEADS, s // t, ACC_ROWS, t), _MXU)]


def _moba(proj, bias_diag, bias_sub, t=TILE):
    s = proj.shape[1]
    n_blk_pad = -(-(s // MOBA_BLOCK) // SUBLANES) * SUBLANES
    scratch = _kv_scratch(s, t) + [pltpu.VMEM((GROUP_HEADS, n_blk_pad, HEAD_DIM), jnp.float32)]
    return _Mixer(_moba_emit, _group_specs(s, CB_MB_Q, CB_MB_K, CB_MB_V) + _bias_specs(t, 0),
                  [proj] * 3 + [bias_diag, bias_sub], scratch)


def _diff_emit(lam_ref, g_ref, q_ref, k_ref, v_ref, bd_ref, bs_ref, o_ref, kb_ref, vt_ref, *, lambda_init):
    t = q_ref.shape[1] // Q_TILES
    key, qry = _tile_iotas(t)
    causal = key <= qry
    lv = lam_ref[...]
    lam = (jnp.exp(jnp.sum(lv[0:1] * lv[1:2], keepdims=True))
           - jnp.exp(jnp.sum(lv[2:3] * lv[3:4], keepdims=True)) + lambda_init)
    halves = range(2 * GROUP_HEADS)

    @_now
    def _():
        for c in halves:
            kb_ref[c] = k_ref[0, :, c * DIFF_HALF:(c + 1) * DIFF_HALF].astype(kb_ref.dtype)
        for h in _HEADS:
            _fill_values_t(vt_ref.at[h], v_ref, h * HEAD_DIM, t)

    def tile(j, bias_ref=None, emask=None):
        n = len(halves)
        return _KeyTile([_key_rows(kb_ref.at[c], j, t) for c in halves], [vt_ref[c // 2, j] for c in halves],
                        None if bias_ref is None else [bias_ref[c // 2] for c in halves],
                        None if emask is None else [emask] * n)

    def query_step(step):
        qts = _queries_t(q_ref, DIFF_HALF, t)
        jobs = []
        for r in range(Q_TILES):
            k = step * Q_TILES + r
            groups = _pairs([functools.partial(tile, j) for j in range(k - 1)])
            groups.append(([functools.partial(tile, k - 1, bs_ref)] if k else [])
                          + [functools.partial(tile, k, bd_ref, causal)])
            jobs.append((len(halves), qts[r], groups))
        all_states = yield from _softmax_jobs(t, jobs)
        outs = []
        for states in all_states:
            heads = []
            for h in _HEADS:
                o = _softmax_out(states[2 * h]) - lam * _softmax_out(states[2 * h + 1])
                o = o * lax.rsqrt(jnp.mean(o * o, axis=0, keepdims=True) + NORM_EPS) * g_ref[...]
                heads.append(o * (1.0 - lambda_init))
            outs.append(heads)
        _store_heads(o_ref, outs)

    yield from query_step(0)


def _diff(proj, lam_params, subln, bias_diag, bias_sub, lambda_init, t=TILE):
    s = proj.shape[1]
    in_specs = ([pl.BlockSpec(lam_params.shape, lambda b: (0, 0)), pl.BlockSpec(subln.shape, lambda b: (0, 0))]
                + _group_specs(s, CB_DF_Q, CB_DF_K, CB_DF_V) + _bias_specs(t, 2))
    return _Mixer(functools.partial(_diff_emit, lambda_init=lambda_init), in_specs,
                  [lam_params, subln] + [proj] * 3 + [bias_diag, bias_sub],
                  _kv_scratch(s, t, key_dim=DIFF_HALF, n_keys=2 * GROUP_HEADS))


def _compress_blocks(kcv_ref, pk_ref, pv_ref, wk1_ref, wk2_ref, wv1_ref, wv2t_ref):
    n_chunk = kcv_ref.shape[1] // CMP_STRIDE

    branches = ((0, pk_ref, wk1_ref), (HEAD_DIM, pv_ref, wv1_ref))
    tops = [jnp.zeros((n_chunk, w1_ref.shape[1]), jnp.float32) for _, _, w1_ref in branches]
    bots = list(tops)
    pack = 2 * LANES // HEAD_DIM
    for l0 in range(0, CMP_STRIDE, pack):
        tokens = [kcv_ref[0, pl.ds(l, n_chunk, stride=CMP_STRIDE), :] for l in range(l0, l0 + pack)]
        for n, (col0, p_ref, w1_ref) in enumerate(branches):
            for half, acc in ((0, tops), (CMP_STRIDE, bots)):
                x = jnp.concatenate([(tok[:, col0:col0 + HEAD_DIM] + p_ref[half + l:half + l + 1, :]).astype(_MXU)
                                     for l, tok in zip(range(l0, l0 + pack), tokens)], axis=1)
                acc[n] = acc[n] + _dot(x, w1_ref[(half + l0) * HEAD_DIM:(half + l0 + pack) * HEAD_DIM, :])
    hidden = [jax.nn.gelu(top + pltpu.roll(bot, n_chunk - 1, axis=0)).astype(_MXU) for top, bot in zip(tops, bots)]
    return _dot(hidden[0], wk2_ref[...]), _dot_nt(wv2t_ref[...], hidden[1])


def _nsa_emit(q_ref, kva_ref, kvb_ref, kcv_ref, pk_ref, pv_ref, wk1_ref, wk2_ref, wv1_ref, wv2t_ref,
              bd_ref, bs_ref, bc_ref, cover_ref, e_ref, o_ref, ks_ref, vst_ref, kw_ref, vwt_ref):
    t = q_ref.shape[1] // Q_TILES
    key, qry = _tile_iotas(t)
    causal = key <= qry
    ks_col, vs_col, kw_col, vw_col, gate_col = 2 * HEAD_DIM, 3 * HEAD_DIM, 0, HEAD_DIM, 2 * HEAD_DIM

    @_now
    def _():
        ks_ref[:, :HEAD_DIM] = kva_ref[0, :, ks_col:ks_col + HEAD_DIM].astype(ks_ref.dtype)
        ks_ref[:, HEAD_DIM:] = e_ref[...]
        kw_ref[...] = kvb_ref[0, :, kw_col:kw_col + HEAD_DIM].astype(kw_ref.dtype)
        _fill_values_t(vst_ref, kva_ref, vs_col, t)
        _fill_values_t(vwt_ref, kvb_ref, vw_col, t)

    def tile(k_ref, vt_ref, j, bias_ref=None, emask=None):
        n = GROUP_HEADS
        return _KeyTile([_key_rows(k_ref, j, t)] * n, [vt_ref[j]] * n,
                        None if bias_ref is None else [bias_ref[h] for h in _HEADS],
                        None if emask is None else [emask] * n)

    def query_step(step):
        qts = _queries_t(q_ref, HEAD_DIM, t)
        kc, vct = (a.astype(_MXU) for a in _compress_blocks(kcv_ref, pk_ref, pv_ref, wk1_ref, wk2_ref,
                                                            wv1_ref, wv2t_ref))
        n_cmp = kc.shape[0]
        n_slc = cover_ref.shape[0]
        c_row = lax.broadcasted_iota(jnp.int32, (n_cmp, t), 0)
        c_col = lax.broadcasted_iota(jnp.int32, (n_cmp, t), 1)
        s_row = lax.broadcasted_iota(jnp.int32, (n_slc, t), 0)
        s_col = lax.broadcasted_iota(jnp.int32, (n_slc, t), 1)
        tiles = [step * Q_TILES + r for r in range(Q_TILES)]

        o_cmp, importance = [], []
        for r, k in enumerate(tiles):
            visible = c_col + k * t >= c_row * CMP_STRIDE + (CMP_LEN - 1)
            cmp_scores = [_dot(kc, qts[r][h]) for h in _HEADS]
            cmp_probs = []
            p_sum = jnp.zeros((n_cmp, t), jnp.float32)
            for h in _HEADS:
                sc = jnp.where(visible, cmp_scores[h] + bc_ref[h, :, r * t:(r + 1) * t], NEG_INF)
                e = jnp.where(visible, jnp.exp2(sc - jnp.max(sc, axis=0, keepdims=True)), 0.0)
                p = e / jnp.maximum(jnp.sum(e, axis=0, keepdims=True), TINY)
                cmp_probs.append(p.astype(_MXU))
                p_sum = p_sum + p
            o_cmp.append([_dot(vct, cmp_probs[h]) for h in _HEADS])
            importance.append(_dot(cover_ref[...], p_sum, precision=lax.Precision.HIGHEST))

        n_back = WINDOW // t
        jobs = []
        for r, k in enumerate(tiles):
            window = [functools.partial(tile, kw_ref, vwt_ref, k - n_back, None, qry < key)] if k >= n_back else []
            for back in range(min(n_back - 1, k), 0, -1):
                window.append(functools.partial(tile, kw_ref, vwt_ref, k - back, bs_ref if back == 1 else None))
            window.append(functools.partial(tile, kw_ref, vwt_ref, k, bd_ref, causal))
            jobs.append((GROUP_HEADS, qts[r], [window]))
        win_states = yield from _softmax_jobs(t, jobs)
        o_win = [[_softmax_out(st) for st in states] for states in win_states]

        jobs = []
        for r, k in enumerate(tiles):
            own = jnp.right_shift(s_col + k * t, int(math.log2(SLC_LEN)))
            score = jnp.where(s_row == own, FORCE, jnp.where(s_row < own, importance[r], NEG_INF))
            sel = _top_k_rows(score, s_row.astype(jnp.float32), min(SLC_TOPN, n_slc))
            penalty = jnp.where(sel > 0.5, 0.0, NEG_INF).astype(_MXU)
            q_aug = [jnp.concatenate([qts[r][h], penalty], axis=0) for h in _HEADS]
            groups = _pairs([functools.partial(tile, ks_ref, vst_ref, j) for j in range(k - 1)])
            groups.append(([functools.partial(tile, ks_ref, vst_ref, k - 1, bs_ref)] if k else [])
                          + [functools.partial(tile, ks_ref, vst_ref, k, bd_ref, causal)])
            jobs.append((GROUP_HEADS, q_aug, groups))
        slc_states = yield from _softmax_jobs(t, jobs)
        o_slc = [[_softmax_out(st) for st in states] for states in slc_states]

        gates = _transposed(kvb_ref[0, :, (gate_col // LANES) * LANES:(gate_col // LANES + 1) * LANES])
        gates = 1.0 / (1.0 + jnp.exp(-gates[gate_col % LANES:gate_col % LANES + N_GATES + 4]))
        outs = []
        for r in range(Q_TILES):
            heads = []
            for h in _HEADS:
                g = [gates[br * GROUP_HEADS + h:br * GROUP_HEADS + h + 1, r * t:(r + 1) * t] for br in range(3)]
                heads.append(g[0] * o_cmp[r][h] + g[1] * o_slc[r][h] + g[2] * o_win[r][h])
            outs.append(heads)
        _store_heads(o_ref, outs)

    yield from query_step(0)


def _nsa(proj, kcv, cmp_params, layer, bias_diag, bias_sub, bias_cmp, cover_t, expand, t=TILE):
    s = proj.shape[1]
    values_t = pltpu.VMEM((s // t, ACC_ROWS, t), _MXU)
    in_specs = (_group_specs(s, CB_NS_Q, CB_NS_A, CB_NS_B)
                + [pl.BlockSpec((1, s, kcv.shape[2]), lambda b: (b, 0, 0))]
                + [pl.BlockSpec((None,) + a.shape[1:], lambda b: (layer, 0, 0)) for a in cmp_params]
                + _bias_specs(t, 1)
                + [pl.BlockSpec(bias_cmp.shape, lambda b: (0, 0, 0)),
                   pl.BlockSpec(cover_t.shape, lambda b: (0, 0)),
                   pl.BlockSpec(expand.shape, lambda b: (0, 0))])
    scratch = [pltpu.VMEM((s, HEAD_DIM + expand.shape[1]), _MXU), values_t, pltpu.VMEM((s, HEAD_DIM), _MXU), values_t]
    return _Mixer(_nsa_emit, in_specs,
                  [proj] * 3 + [kcv, *cmp_params, bias_diag, bias_sub, bias_cmp, cover_t, expand], scratch)


def _nsa_constants(s):
    n_cmp = (s - CMP_LEN) // CMP_STRIDE + 1
    n_slc = s // SLC_LEN
    assert n_cmp + 1 == s // CMP_STRIDE and n_slc % SUBLANES == 0
    c_start = np.arange(n_cmp) * CMP_STRIDE
    s_start = np.arange(n_slc) * SLC_LEN
    cover = np.clip(np.minimum((c_start + CMP_LEN - 1)[:, None], (s_start + SLC_LEN - 1)[None, :])
                    - np.maximum(c_start[:, None], s_start[None, :]) + 1, 0, None) / CMP_LEN
    cover_t = np.zeros((n_slc, n_cmp + 1), np.float32)
    cover_t[:, :n_cmp] = cover.T
    expand = (np.arange(s)[:, None] // SLC_LEN == np.arange(n_slc)[None, :]).astype(np.float32)
    return jnp.asarray(cover_t), jnp.asarray(expand, _MXU)


def kernel(x, w_in, w_out, w_up, w_down, norm_attn, norm_mlp, cmp_pos_k, cmp_pos_v, cmp_k_w1, cmp_k_w2,
           cmp_v_w1, cmp_v_w2, diff_lambda, diff_subln, rel_bias, final_norm):
    bsz, s, d = x.shape
    depth = w_in.shape[0]
    t = TILE
    n_chunk = s // CMP_STRIDE
    assert s == Q_TILES * t and MOBA_BLOCK % t == 0 and WINDOW % t == 0 and t >= MAX_DISTANCE

    assert w_in.shape[2] + PAD_COLS == D_IN_PAD
    w_in_c = jnp.pad(w_in.astype(_MXU), ((0, 0), (0, 0), (0, PAD_COLS)))
    w_out_c, w_up_c, w_down_c = (w.astype(_MXU) for w in (w_out, w_up, w_down))
    wk1, wk2 = cmp_k_w1.astype(_MXU), cmp_k_w2.astype(_MXU)
    wv1, wv2t = cmp_v_w1.astype(_MXU), jnp.swapaxes(cmp_v_w2, 1, 2).astype(_MXU)

    table_t = jnp.pad(rel_bias.T, ((0, 0), (0, LANES - N_BUCKETS)))
    tiles = dict(n_heads=rel_bias.shape[1], head0=0, rows=t, cols=t, col_tile=t, row_stride=-1, col_stride=1)
    bias_diag = _bias_table(table_t, offset=0, **tiles)
    bias_sub = _bias_table(table_t, offset=t, **tiles)
    bias_cmp = _bias_table(table_t, n_heads=GROUP_HEADS, head0=GROUP_HEADS, rows=n_chunk, cols=s,
                           col_tile=t, row_stride=-CMP_STRIDE, col_stride=1, offset=-(CMP_LEN - 1))
    cover_t, expand = _nsa_constants(s)
    col_scale = np.ones((1, D_IN_PAD), np.float32)
    for cb, width in ((CB_SB_Q, HEAD_DIM), (CB_MB_Q, HEAD_DIM), (CB_NS_Q, HEAD_DIM), (CB_DF_Q, DIFF_HALF)):
        col_scale[:, cb * GROUP_WIDTH:(cb + 1) * GROUP_WIDTH] = width ** -0.5 * LOG2E
    col_scale = jnp.asarray(col_scale)

    x2 = x.reshape(bsz * s, d)
    for layer in range(depth):
        proj, kcv = _norm_matmul(x2, norm_attn[layer][None], w_in_c, layer, col_scale,
                                 gap=(COLS_BEFORE_PAD, COLS_BEFORE_PAD + PAD_COLS),
                                 side_col=CB_NS_A * GROUP_WIDTH, side_width=2 * HEAD_DIM)
        proj = proj.reshape(bsz, s, D_IN_PAD)
        lambda_init = 0.8 - 0.6 * math.exp(-0.3 * layer)
        o_sb = _run_mixer(_stick_breaking(proj), bsz, s, "stick_breaking")
        o_mb = _run_mixer(_moba(proj, bias_diag, bias_sub), bsz, s, "moba")
        o_ns = _run_mixer(_nsa(proj, kcv.reshape(bsz, s, 2 * HEAD_DIM), (cmp_pos_k, cmp_pos_v, wk1, wk2, wv1, wv2t),
                               layer, bias_diag, bias_sub, bias_cmp, cover_t, expand), bsz, s, "nsa")
        o_df = _run_mixer(_diff(proj, diff_lambda[layer], diff_subln[layer][:, None], bias_diag, bias_sub,
                                lambda_init), bsz, s, "diff_attention")
        groups = [o.reshape(bsz * s, GROUP_WIDTH) for o in (o_sb, o_mb, o_ns, o_df)]
        x2 = _out_mlp(x2, groups, w_out_c, norm_mlp[layer][None], w_up_c, w_down_c, layer,
                      final_norm[None], final_norm=(layer == depth - 1))
    return x2.reshape(bsz, s, d)
```

```python
import functools
import math
from typing import Any, Callable, NamedTuple, Optional, Sequence

import numpy as np
import jax
import jax.numpy as jnp
from jax import lax
from jax.experimental import pallas as pl
from jax.experimental.pallas import tpu as pltpu

HEAD_DIM = 64
GROUP_HEADS = 4
GROUP_WIDTH = GROUP_HEADS * HEAD_DIM
NORM_EPS = 1e-6
NEG_INF = -1e30
BIG = 1e30
FORCE = 1e30
TINY = 1e-30
SOFTPLUS_CLAMP = 64.0
PICKED = -3e38
LOG2E = math.log2(math.e)
N_BUCKETS = 32
MAX_DISTANCE = 128
MOBA_BLOCK = 256
MOBA_TOPK = 3
CMP_LEN = 32
CMP_STRIDE = 16
SLC_LEN = 64
SLC_TOPN = 4
WINDOW = 512
DIFF_HALF = HEAD_DIM // 2
LANES = 128
SUBLANES = 8
TILE = 256
BF16_ROWS = 16
ACC_ROWS = HEAD_DIM + BF16_ROWS
N_GATES = 3 * GROUP_HEADS
COLS_BEFORE_PAD = 9 * GROUP_WIDTH - 2 * HEAD_DIM + N_GATES
PAD_COLS = 2 * HEAD_DIM - N_GATES
CB_SB_Q, CB_SB_K, CB_SB_V, CB_MB_Q, CB_MB_K, CB_MB_V, CB_NS_Q, CB_NS_A, CB_NS_B, CB_DF_Q, CB_DF_K, CB_DF_V = range(12)
D_IN_PAD = 12 * GROUP_WIDTH

_MXU = jnp.bfloat16
_VMEM_LIMIT = 56 * 1024 * 1024
_HEADS = range(GROUP_HEADS)
Q_TILES = 8
LOOKAHEAD = 2


def _dot(a, b, precision=None):
    return jnp.dot(a, b, precision=precision, preferred_element_type=jnp.float32)


def _dot_nt(a, b, precision=None):
    return lax.dot_general(a, b, (((1,), (1,)), ((), ())), precision=precision,
                           preferred_element_type=jnp.float32)


def _rms(x, g):
    return x * lax.rsqrt(jnp.mean(x * x, axis=-1, keepdims=True) + NORM_EPS) * g


def _params(*sem):
    return pltpu.CompilerParams(dimension_semantics=sem, vmem_limit_bytes=_VMEM_LIMIT)


def _norm_matmul_kernel(x_ref, g_ref, w_ref, scale_ref, o_ref, side_ref, wp_ref, *, tn, side_col, gap):
    @pl.when(pl.program_id(0) == 0)
    def _():
        lo, hi = gap
        wp_ref[:, :lo] = w_ref[:, :lo]
        wp_ref[:, lo:hi] = jnp.zeros((wp_ref.shape[0], hi - lo), wp_ref.dtype)
        wp_ref[:, hi:] = w_ref[:, lo:lo + wp_ref.shape[1] - hi]

    h = _rms(x_ref[...], g_ref[...]).astype(_MXU)
    for j in range(wp_ref.shape[1] // tn):
        cols = slice(j * tn, (j + 1) * tn)
        acc = _dot(h, wp_ref[:, cols])
        o_ref[:, cols] = (acc * scale_ref[:, cols]).astype(o_ref.dtype)
        if j * tn <= side_col < (j + 1) * tn:
            side_ref[...] = acc[:, side_col - j * tn:side_col - j * tn + side_ref.shape[1]]


def _norm_matmul(x, g, w, layer, col_scale, *, gap, side_col, side_width, tm=1024, tn=1024):
    m, d = x.shape
    n = w.shape[2]
    assert side_col % LANES == 0 and side_col // tn == (side_col + side_width - 1) // tn
    return pl.pallas_call(
        functools.partial(_norm_matmul_kernel, tn=tn, side_col=side_col, gap=gap),
        grid=(m // tm,),
        in_specs=[pl.BlockSpec((tm, d), lambda i: (i, 0)),
                  pl.BlockSpec((1, d), lambda i: (0, 0)),
                  pl.BlockSpec((None, d, w.shape[2]), lambda i: (layer, 0, 0), pipeline_mode=pl.Buffered(1)),
                  pl.BlockSpec((1, n), lambda i: (0, 0))],
        out_specs=[pl.BlockSpec((tm, n), lambda i: (i, 0)),
                   pl.BlockSpec((tm, side_width), lambda i: (i, 0))],
        out_shape=[jax.ShapeDtypeStruct((m, n), _MXU),
                   jax.ShapeDtypeStruct((m, side_width), jnp.float32)],
        scratch_shapes=[pltpu.VMEM((d, n), _MXU)],
        compiler_params=_params("arbitrary"),
        name="norm_in_proj",
    )(x, g, w, col_scale)


def _out_mlp_kernel(x_ref, a_ref, b_ref, c_ref, d_ref, wo_ref, g_ref, wu_ref, wd_ref, gf_ref, o_ref,
                    *, final_norm, tf):
    mixed = jnp.concatenate([a_ref[...], b_ref[...], c_ref[...], d_ref[...]], axis=1)
    y = x_ref[...] + _dot(mixed, wo_ref[...])
    h = _rms(y, g_ref[...]).astype(_MXU)
    for c in range(wu_ref.shape[1] // tf):
        u = jnp.square(jnp.maximum(_dot(h, wu_ref[:, c * tf:(c + 1) * tf]), 0.0))
        y = y + _dot(u.astype(_MXU), wd_ref[c * tf:(c + 1) * tf, :])
    if final_norm:
        y = _rms(y, gf_ref[...])
    o_ref[...] = y


def _out_mlp(x, groups, w_out, g, w_up, w_down, layer, g_final, *, final_norm, tm=512, tf=2048):
    m, d = x.shape
    f = w_up.shape[2]
    gspec = pl.BlockSpec((tm, GROUP_WIDTH), lambda i: (i, 0))
    row = pl.BlockSpec((1, d), lambda i: (0, 0))

    def resident(rows, cols):
        return pl.BlockSpec((None, rows, cols), lambda i: (layer, 0, 0), pipeline_mode=pl.Buffered(1))

    return pl.pallas_call(
        functools.partial(_out_mlp_kernel, final_norm=final_norm, tf=tf),
        grid=(m // tm,),
        in_specs=[pl.BlockSpec((tm, d), lambda i: (i, 0)), gspec, gspec, gspec, gspec,
                  resident(d, d), row, resident(d, f), resident(f, d), row],
        out_specs=pl.BlockSpec((tm, d), lambda i: (i, 0)),
        out_shape=jax.ShapeDtypeStruct((m, d), jnp.float32),
        compiler_params=_params("parallel"),
        name="out_proj_mlp_residual",
    )(x, *groups, w_out, g, w_up, w_down, g_final)


def _t5_bucket(dist):
    n = jnp.maximum(dist, 0)
    max_exact = N_BUCKETS // 2
    nf = jnp.maximum(n, 1).astype(jnp.float32)
    large = max_exact + (jnp.log(nf / max_exact) / math.log(MAX_DISTANCE / max_exact)
                         * (N_BUCKETS - max_exact)).astype(jnp.int32)
    large = jnp.minimum(large, N_BUCKETS - 1)
    return jnp.where(n < max_exact, n, large)


def _bias_kernel(tab_ref, o_ref, *, row_stride, col_stride, offset, head0):
    nh, tr, tc = o_ref.shape
    for blk in range(tc // LANES):
        rows = lax.broadcasted_iota(jnp.int32, (tr, LANES), 0)
        cols = lax.broadcasted_iota(jnp.int32, (tr, LANES), 1) + (pl.program_id(0) * tc + blk * LANES)
        bucket = _t5_bucket(rows * row_stride + cols * col_stride + offset)
        for h in range(nh):
            row = tab_ref[head0 + h:head0 + h + 1, :]
            row = (row - row[:, N_BUCKETS - 1:N_BUCKETS]) * LOG2E
            o_ref[h, :, blk * LANES:(blk + 1) * LANES] = jnp.take_along_axis(
                jnp.broadcast_to(row, (tr, LANES)), bucket, axis=1, mode="promise_in_bounds")


def _bias_table(table_t, *, n_heads, head0, rows, cols, col_tile, row_stride, col_stride, offset):
    return pl.pallas_call(
        functools.partial(_bias_kernel, row_stride=row_stride, col_stride=col_stride,
                          offset=offset, head0=head0),
        grid=(cols // col_tile,),
        in_specs=[pl.BlockSpec(table_t.shape, lambda i: (0, 0))],
        out_specs=pl.BlockSpec((n_heads, rows, col_tile), lambda i: (0, 0, i)),
        out_shape=jax.ShapeDtypeStruct((n_heads, rows, cols), jnp.float32),
        compiler_params=_params("parallel"),
        name="t5_bias_tiles",
    )(table_t)


def _softmax_init(t):
    return (jnp.full((1, t), NEG_INF, jnp.float32), jnp.zeros((ACC_ROWS, t), jnp.float32))


class _KeyTile(NamedTuple):
    kts: Sequence[Any]
    vts: Sequence[Any]
    biases: Optional[Sequence[Any]] = None
    emasks: Optional[Sequence[Any]] = None
    qmasks: Optional[Sequence[Any]] = None


def _round_robin(lists):
    out = []
    for rank in range(max(map(len, lists), default=0)):
        out.extend(items[rank] for items in lists if rank < len(items))
    return out


def _softmax_jobs(t, jobs):
    built = {}

    def tiles_of(job, g):
        if (job, g) not in built:
            built[job, g] = [make() for make in jobs[job][2][g]]
        return built[job, g]

    def scores_of(job, g, c):
        qts = jobs[job][1]
        row = []
        for tile in tiles_of(job, g):
            s = _dot(tile.kts[c], qts[c])
            if tile.biases is not None:
                s = s + tile.biases[c]
            if tile.emasks is not None:
                s = jnp.where(tile.emasks[c], s, NEG_INF)
            row.append(s.astype(_MXU))
        return row

    def update(state, job, g, c, scores):
        m, acc = state
        m_new = m
        for tile, s in zip(tiles_of(job, g), scores):
            tile_max = jnp.max(s, axis=0, keepdims=True).astype(jnp.float32)
            if tile.qmasks is not None:
                tile_max = jnp.where(tile.qmasks[c], tile_max, NEG_INF)
            m_new = jnp.maximum(m_new, tile_max)
        seen = m_new > 0.5 * NEG_INF
        acc = jnp.exp2(m - m_new) * acc
        for tile, s in zip(tiles_of(job, g), scores):
            ok = seen if tile.qmasks is None else jnp.logical_and(seen, tile.qmasks[c])
            acc = acc + _dot(tile.vts[c], jnp.exp2(s - jnp.where(ok, m_new, BIG).astype(_MXU)))
        return m_new, acc

    units = _round_robin([[(job, g, c) for g in range(len(groups)) for c in range(n)]
                          for job, (n, _, groups) in enumerate(jobs)])
    lookahead = max(LOOKAHEAD, len(jobs))
    states = [[_softmax_init(t) for _ in range(n)] for n, _, _ in jobs]
    pending = {k: scores_of(*units[k]) for k in range(min(lookahead, len(units)))}
    for k, (job, g, c) in enumerate(units):
        if k + lookahead < len(units):
            pending[k + lookahead] = scores_of(*units[k + lookahead])
        states[job][c] = update(states[job][c], job, g, c, pending.pop(k))
        yield
    return states


def _pairs(items):
    return [items[p:p + 2] for p in range(0, len(items), 2)]


class _Mixer(NamedTuple):
    emit: Callable[..., Any]
    in_specs: Sequence[Any]
    operands: Sequence[Any]
    scratch: Sequence[Any]


def _mixer_kernel(*refs, emit):
    for _ in emit(*refs):
        pass


def _run_mixer(mixer, bsz, s, name):
    return pl.pallas_call(
        functools.partial(_mixer_kernel, emit=mixer.emit),
        grid=(bsz,),
        in_specs=list(mixer.in_specs),
        out_specs=pl.BlockSpec((1, s, GROUP_WIDTH), lambda b: (b, 0, 0)),
        out_shape=jax.ShapeDtypeStruct((bsz, s, GROUP_WIDTH), _MXU),
        scratch_shapes=list(mixer.scratch),
        compiler_params=_params("parallel"),
        name=name,
    )(*mixer.operands)


def _softmax_out(state):
    acc = state[1]
    return acc[:HEAD_DIM] / jnp.maximum(acc[HEAD_DIM:HEAD_DIM + 1], TINY)


def _top_k_rows(score, row_f, k):
    sel = jnp.zeros(score.shape, jnp.float32)
    for _ in range(k):
        mx = jnp.max(score, axis=0, keepdims=True)
        idx = jnp.min(jnp.where(score == mx, row_f, float(score.shape[0])), axis=0, keepdims=True)
        pick = row_f == idx
        sel = jnp.where(pick, 1.0, sel)
        score = jnp.where(pick, PICKED, score)
    return sel


def _now(fn):
    fn()


def _tile_iotas(t):
    return (lax.broadcasted_iota(jnp.int32, (t, t), 0), lax.broadcasted_iota(jnp.int32, (t, t), 1))


def _key_rows(ref, j, t):
    if isinstance(j, int):
        return ref[j * t:(j + 1) * t, :]
    return ref[pl.ds(pl.multiple_of(j * t, t), t), :]


def _transposed(ref_block):
    return ref_block.astype(jnp.float32).T


def _queries_t(q_ref, width, t):
    qt = _transposed(q_ref[0]).astype(_MXU)
    return [[qt[c * width:(c + 1) * width, r * t:(r + 1) * t] for c in range(GROUP_WIDTH // width)]
            for r in range(Q_TILES)]


def _fill_values_t(vt_ref, v_ref, col0, t):
    n_tiles, rows, _ = vt_ref.shape
    lane_block = (col0 // LANES) * LANES
    for c in range(n_tiles):
        blk = _transposed(v_ref[0, c * t:(c + 1) * t, lane_block:lane_block + LANES])
        vt_ref[c, 0:HEAD_DIM, :] = blk[col0 - lane_block:col0 - lane_block + HEAD_DIM].astype(vt_ref.dtype)
        if rows == ACC_ROWS:
            first = lax.broadcasted_iota(jnp.int32, (rows - HEAD_DIM, t), 0) == 0
            vt_ref[c, HEAD_DIM:rows, :] = jnp.where(first, 1.0, 0.0).astype(vt_ref.dtype)


def _group_specs(s, *column_blocks):
    return [pl.BlockSpec((1, s, GROUP_WIDTH), functools.partial(lambda cb, b: (b, 0, cb), cb))
            for cb in column_blocks]


def _bias_specs(t, head_group):
    spec = pl.BlockSpec((GROUP_HEADS, t, t), lambda b: (head_group, 0, 0))
    return [spec, spec]


def _store_heads(o_ref, outs_t):
    tiles = [jnp.concatenate(heads, axis=0) for heads in outs_t]
    o_ref[0] = jnp.concatenate(tiles, axis=1).T.astype(o_ref.dtype)


def _sb_emit(q_ref, k_ref, v_ref, o_ref, kb_ref, vt_ref):
    t = q_ref.shape[1] // Q_TILES
    key, qry = _tile_iotas(t)
    strict = key < qry
    later = jnp.where(qry > key, 1.0, 0.0).astype(_MXU)

    @_now
    def _():
        for h in _HEADS:
            kb_ref[h] = k_ref[0, :, h * HEAD_DIM:(h + 1) * HEAD_DIM].astype(kb_ref.dtype)
            _fill_values_t(vt_ref.at[h], v_ref, h * HEAD_DIM, t)

    def query_step(step):
        qts = _queries_t(q_ref, HEAD_DIM, t)
        units = _round_robin([[(r, step * Q_TILES + r, pair, h)
                               for pair in _pairs(list(range(step * Q_TILES + r, -1, -1))) for h in _HEADS]
                              for r in range(Q_TILES)])
        zero = (jnp.zeros((HEAD_DIM, t), jnp.float32), jnp.zeros((1, t), jnp.float32))
        carry = [[zero] * GROUP_HEADS for _ in range(Q_TILES)]
        zs, log_keeps, suffixes = {}, {}, {}

        def scores(u):
            r, _, pair, h = units[u]
            zs[u] = [_dot(_key_rows(kb_ref.at[h], j, t), qts[r][h]) for j in pair]

        def keeps(u):
            _, k, pair, _ = units[u]
            log_keeps[u], suffixes[u] = [], []
            for j, z in zip(pair, zs[u]):
                drop = jnp.maximum(jnp.log2(1.0 + jnp.exp2(jnp.minimum(z, SOFTPLUS_CLAMP))), z)
                if j == k:
                    drop = jnp.where(strict, drop, 0.0)
                log_keeps[u].append(drop)
                suffixes[u].append(_dot(later, drop.astype(_MXU)))

        def values(u):
            r, k, pair, h = units[u]
            acc, run = carry[r][h]
            weights = []
            for j, z, drop, suffix in zip(pair, zs.pop(u), log_keeps.pop(u), suffixes.pop(u)):
                a = jnp.exp2((z - drop - suffix + run).astype(_MXU))
                if j == k:
                    a = jnp.where(strict, a, 0.0)
                weights.append(a)
                run = run - (suffix[0:1] + drop[0:1])
            for j, w in zip(pair, weights):
                acc = acc + _dot(vt_ref[h, j], w)
            carry[r][h] = (acc, run)

        stages = (scores, keeps, values)
        for tick in range(len(units) + len(stages) - 1):
            for lag, stage in enumerate(stages):
                if 0 <= tick - lag < len(units):
                    stage(tick - lag)
            yield
        _store_heads(o_ref, [[c[0] for c in tile_carry] for tile_carry in carry])

    yield from query_step(0)


def _stick_breaking(proj, t=TILE):
    s = proj.shape[1]
    scratch = [pltpu.VMEM((GROUP_HEADS, s, HEAD_DIM), _MXU), pltpu.VMEM((GROUP_HEADS, s // t, HEAD_DIM, t), _MXU)]
    return _Mixer(_sb_emit, _group_specs(s, CB_SB_Q, CB_SB_K, CB_SB_V), [proj] * 3, scratch)


def _moba_emit(q_ref, k_ref, v_ref, bd_ref, bs_ref, o_ref, kb_ref, vt_ref, km_ref):
    t = q_ref.shape[1] // Q_TILES
    n_blk = k_ref.shape[1] // MOBA_BLOCK
    tiles_per_blk = MOBA_BLOCK // t
    blk_shift = int(math.log2(tiles_per_blk))

    @_now
    def _():
        km_ref[...] = jnp.zeros_like(km_ref)
        for h in _HEADS:
            lo, hi = h * HEAD_DIM, (h + 1) * HEAD_DIM
            kb_ref[h] = k_ref[0, :, lo:hi].astype(kb_ref.dtype)
            _fill_values_t(vt_ref.at[h], v_ref, lo, t)
            for n in range(n_blk):
                blk = k_ref[0, n * MOBA_BLOCK:(n + 1) * MOBA_BLOCK, lo:hi]
                km_ref[h, n:n + 1, :] = jnp.mean(blk.astype(jnp.float32), axis=0, keepdims=True)

    def query_step(step):
        key, qry = _tile_iotas(t)
        causal = key <= qry
        blk_row = lax.broadcasted_iota(jnp.int32, (km_ref.shape[1], t), 0)
        qt = _transposed(q_ref[0])
        jobs = []
        for r in range(Q_TILES):
            k = step * Q_TILES + r
            own = k >> blk_shift
            qts, sels = [], []
            for h in _HEADS:
                qf = qt[h * HEAD_DIM:(h + 1) * HEAD_DIM, r * t:(r + 1) * t]
                qts.append(qf.astype(_MXU))
                gate = _dot(km_ref[h], qf, precision=lax.Precision.HIGHEST)
                gate = jnp.where(blk_row < own, gate, NEG_INF)
                sel = _top_k_rows(gate, blk_row.astype(jnp.float32), min(MOBA_TOPK, n_blk - 1))
                sels.append(jnp.where(blk_row < own, sel, 0.0))

            def tile(own, sels, j, bias_ref=None, emask=None):
                n = j >> blk_shift
                return _KeyTile([_key_rows(kb_ref.at[h], j, t) for h in _HEADS], [vt_ref[h, j] for h in _HEADS],
                                None if bias_ref is None else [bias_ref[h] for h in _HEADS],
                                None if emask is None else [emask] * GROUP_HEADS,
                                None if n == own else [sels[h][n:n + 1] > 0.5 for h in _HEADS])

            tile = functools.partial(tile, own, sels)
            groups = _pairs([functools.partial(tile, j) for j in range(k - 1)])
            groups.append(([functools.partial(tile, k - 1, bs_ref)] if k else [])
                          + [functools.partial(tile, k, bd_ref, causal)])
            jobs.append((GROUP_HEADS, qts, groups))
        all_states = yield from _softmax_jobs(t, jobs)
        _store_heads(o_ref, [[_softmax_out(st) for st in states] for states in all_states])

    yield from query_step(0)


def _kv_scratch(s, t, key_dim=HEAD_DIM, n_keys=GROUP_HEADS):
    return [pltpu.VMEM((n_keys, s, key_dim), _MXU), pltpu.VMEM((GROUP_HEADS, s // t, ACC_ROWS, t), _MXU)]


def _moba(proj, bias_diag, bias_sub, t=TILE):
    s = proj.shape[1]
    n_blk_pad = -(-(s // MOBA_BLOCK) // SUBLANES) * SUBLANES
    scratch = _kv_scratch(s, t) + [pltpu.VMEM((GROUP_HEADS, n_blk_pad, HEAD_DIM), jnp.float32)]
    return _Mixer(_moba_emit, _group_specs(s, CB_MB_Q, CB_MB_K, CB_MB_V) + _bias_specs(t, 0),
                  [proj] * 3 + [bias_diag, bias_sub], scratch)


def _diff_emit(lam_ref, g_ref, q_ref, k_ref, v_ref, bd_ref, bs_ref, o_ref, kb_ref, vt_ref, *, lambda_init):
    t = q_ref.shape[1] // Q_TILES
    key, qry = _tile_iotas(t)
    causal = key <= qry
    lv = lam_ref[...]
    lam = (jnp.exp(jnp.sum(lv[0:1] * lv[1:2], keepdims=True))
           - jnp.exp(jnp.sum(lv[2:3] * lv[3:4], keepdims=True)) + lambda_init)
    halves = range(2 * GROUP_HEADS)

    @_now
    def _():
        for c in halves:
            kb_ref[c] = k_ref[0, :, c * DIFF_HALF:(c + 1) * DIFF_HALF].astype(kb_ref.dtype)
        for h in _HEADS:
            _fill_values_t(vt_ref.at[h], v_ref, h * HEAD_DIM, t)

    def tile(j, bias_ref=None, emask=None):
        n = len(halves)
        return _KeyTile([_key_rows(kb_ref.at[c], j, t) for c in halves], [vt_ref[c // 2, j] for c in halves],
                        None if bias_ref is None else [bias_ref[c // 2] for c in halves],
                        None if emask is None else [emask] * n)

    def query_step(step):
        qts = _queries_t(q_ref, DIFF_HALF, t)
        jobs = []
        for r in range(Q_TILES):
            k = step * Q_TILES + r
            groups = _pairs([functools.partial(tile, j) for j in range(k - 1)])
            groups.append(([functools.partial(tile, k - 1, bs_ref)] if k else [])
                          + [functools.partial(tile, k, bd_ref, causal)])
            jobs.append((len(halves), qts[r], groups))
        all_states = yield from _softmax_jobs(t, jobs)
        outs = []
        for states in all_states:
            heads = []
            for h in _HEADS:
                o = _softmax_out(states[2 * h]) - lam * _softmax_out(states[2 * h + 1])
                o = o * lax.rsqrt(jnp.mean(o * o, axis=0, keepdims=True) + NORM_EPS) * g_ref[...]
                heads.append(o * (1.0 - lambda_init))
            outs.append(heads)
        _store_heads(o_ref, outs)

    yield from query_step(0)


def _diff(proj, lam_params, subln, bias_diag, bias_sub, lambda_init, t=TILE):
    s = proj.shape[1]
    in_specs = ([pl.BlockSpec(lam_params.shape, lambda b: (0, 0)), pl.BlockSpec(subln.shape, lambda b: (0, 0))]
                + _group_specs(s, CB_DF_Q, CB_DF_K, CB_DF_V) + _bias_specs(t, 2))
    return _Mixer(functools.partial(_diff_emit, lambda_init=lambda_init), in_specs,
                  [lam_params, subln] + [proj] * 3 + [bias_diag, bias_sub],
                  _kv_scratch(s, t, key_dim=DIFF_HALF, n_keys=2 * GROUP_HEADS))


def _compress_blocks(kcv_ref, pk_ref, pv_ref, wk1_ref, wk2_ref, wv1_ref, wv2t_ref):
    n_chunk = kcv_ref.shape[1] // CMP_STRIDE

    branches = ((0, pk_ref, wk1_ref), (HEAD_DIM, pv_ref, wv1_ref))
    tops = [jnp.zeros((n_chunk, w1_ref.shape[1]), jnp.float32) for _, _, w1_ref in branches]
    bots = list(tops)
    pack = 2 * LANES // HEAD_DIM
    for l0 in range(0, CMP_STRIDE, pack):
        tokens = [kcv_ref[0, pl.ds(l, n_chunk, stride=CMP_STRIDE), :] for l in range(l0, l0 + pack)]
        for n, (col0, p_ref, w1_ref) in enumerate(branches):
            for half, acc in ((0, tops), (CMP_STRIDE, bots)):
                x = jnp.concatenate([(tok[:, col0:col0 + HEAD_DIM] + p_ref[half + l:half + l + 1, :]).astype(_MXU)
                                     for l, tok in zip(range(l0, l0 + pack), tokens)], axis=1)
                acc[n] = acc[n] + _dot(x, w1_ref[(half + l0) * HEAD_DIM:(half + l0 + pack) * HEAD_DIM, :])
    hidden = [jax.nn.gelu(top + pltpu.roll(bot, n_chunk - 1, axis=0)).astype(_MXU) for top, bot in zip(tops, bots)]
    return _dot(hidden[0], wk2_ref[...]), _dot_nt(wv2t_ref[...], hidden[1])


def _nsa_emit(q_ref, kva_ref, kvb_ref, kcv_ref, pk_ref, pv_ref, wk1_ref, wk2_ref, wv1_ref, wv2t_ref,
              bd_ref, bs_ref, bc_ref, cover_ref, e_ref, o_ref, ks_ref, vst_ref, kw_ref, vwt_ref):
    t = q_ref.shape[1] // Q_TILES
    key, qry = _tile_iotas(t)
    causal = key <= qry
    ks_col, vs_col, kw_col, vw_col, gate_col = 2 * HEAD_DIM, 3 * HEAD_DIM, 0, HEAD_DIM, 2 * HEAD_DIM

    @_now
    def _():
        ks_ref[:, :HEAD_DIM] = kva_ref[0, :, ks_col:ks_col + HEAD_DIM].astype(ks_ref.dtype)
        ks_ref[:, HEAD_DIM:] = e_ref[...]
        kw_ref[...] = kvb_ref[0, :, kw_col:kw_col + HEAD_DIM].astype(kw_ref.dtype)
        _fill_values_t(vst_ref, kva_ref, vs_col, t)
        _fill_values_t(vwt_ref, kvb_ref, vw_col, t)

    def tile(k_ref, vt_ref, j, bias_ref=None, emask=None):
        n = GROUP_HEADS
        return _KeyTile([_key_rows(k_ref, j, t)] * n, [vt_ref[j]] * n,
                        None if bias_ref is None else [bias_ref[h] for h in _HEADS],
                        None if emask is None else [emask] * n)

    def query_step(step):
        qts = _queries_t(q_ref, HEAD_DIM, t)
        kc, vct = (a.astype(_MXU) for a in _compress_blocks(kcv_ref, pk_ref, pv_ref, wk1_ref, wk2_ref,
                                                            wv1_ref, wv2t_ref))
        n_cmp = kc.shape[0]
        n_slc = cover_ref.shape[0]
        c_row = lax.broadcasted_iota(jnp.int32, (n_cmp, t), 0)
        c_col = lax.broadcasted_iota(jnp.int32, (n_cmp, t), 1)
        s_row = lax.broadcasted_iota(jnp.int32, (n_slc, t), 0)
        s_col = lax.broadcasted_iota(jnp.int32, (n_slc, t), 1)
        tiles = [step * Q_TILES + r for r in range(Q_TILES)]

        o_cmp, importance = [], []
        for r, k in enumerate(tiles):
            visible = c_col + k * t >= c_row * CMP_STRIDE + (CMP_LEN - 1)
            cmp_scores = [_dot(kc, qts[r][h]) for h in _HEADS]
            cmp_probs = []
            p_sum = jnp.zeros((n_cmp, t), jnp.float32)
            for h in _HEADS:
                sc = jnp.where(visible, cmp_scores[h] + bc_ref[h, :, r * t:(r + 1) * t], NEG_INF)
                e = jnp.where(visible, jnp.exp2(sc - jnp.max(sc, axis=0, keepdims=True)), 0.0)
                p = e / jnp.maximum(jnp.sum(e, axis=0, keepdims=True), TINY)
                cmp_probs.append(p.astype(_MXU))
                p_sum = p_sum + p
            o_cmp.append([_dot(vct, cmp_probs[h]) for h in _HEADS])
            importance.append(_dot(cover_ref[...], p_sum, precision=lax.Precision.HIGHEST))

        n_back = WINDOW // t
        jobs = []
        for r, k in enumerate(tiles):
            window = [functools.partial(tile, kw_ref, vwt_ref, k - n_back, None, qry < key)] if k >= n_back else []
            for back in range(min(n_back - 1, k), 0, -1):
                window.append(functools.partial(tile, kw_ref, vwt_ref, k - back, bs_ref if back == 1 else None))
            window.append(functools.partial(tile, kw_ref, vwt_ref, k, bd_ref, causal))
            jobs.append((GROUP_HEADS, qts[r], [window]))
        win_states = yield from _softmax_jobs(t, jobs)
        o_win = [[_softmax_out(st) for st in states] for states in win_states]

        jobs = []
        for r, k in enumerate(tiles):
            own = jnp.right_shift(s_col + k * t, int(math.log2(SLC_LEN)))
            score = jnp.where(s_row == own, FORCE, jnp.where(s_row < own, importance[r], NEG_INF))
            sel = _top_k_rows(score, s_row.astype(jnp.float32), min(SLC_TOPN, n_slc))
            penalty = jnp.where(sel > 0.5, 0.0, NEG_INF).astype(_MXU)
            q_aug = [jnp.concatenate([qts[r][h], penalty], axis=0) for h in _HEADS]
            groups = _pairs([functools.partial(tile, ks_ref, vst_ref, j) for j in range(k - 1)])
            groups.append(([functools.partial(tile, ks_ref, vst_ref, k - 1, bs_ref)] if k else [])
                          + [functools.partial(tile, ks_ref, vst_ref, k, bd_ref, causal)])
            jobs.append((GROUP_HEADS, q_aug, groups))
        slc_states = yield from _softmax_jobs(t, jobs)
        o_slc = [[_softmax_out(st) for st in states] for states in slc_states]

        gates = _transposed(kvb_ref[0, :, (gate_col // LANES) * LANES:(gate_col // LANES + 1) * LANES])
        gates = 1.0 / (1.0 + jnp.exp(-gates[gate_col % LANES:gate_col % LANES + N_GATES + 4]))
        outs = []
        for r in range(Q_TILES):
            heads = []
            for h in _HEADS:
                g = [gates[br * GROUP_HEADS + h:br * GROUP_HEADS + h + 1, r * t:(r + 1) * t] for br in range(3)]
                heads.append(g[0] * o_cmp[r][h] + g[1] * o_slc[r][h] + g[2] * o_win[r][h])
            outs.append(heads)
        _store_heads(o_ref, outs)

    yield from query_step(0)


def _nsa(proj, kcv, cmp_params, layer, bias_diag, bias_sub, bias_cmp, cover_t, expand, t=TILE):
    s = proj.shape[1]
    values_t = pltpu.VMEM((s // t, ACC_ROWS, t), _MXU)
    in_specs = (_group_specs(s, CB_NS_Q, CB_NS_A, CB_NS_B)
                + [pl.BlockSpec((1, s, kcv.shape[2]), lambda b: (b, 0, 0))]
                + [pl.BlockSpec((None,) + a.shape[1:], lambda b: (layer, 0, 0)) for a in cmp_params]
                + _bias_specs(t, 1)
                + [pl.BlockSpec(bias_cmp.shape, lambda b: (0, 0, 0)),
                   pl.BlockSpec(cover_t.shape, lambda b: (0, 0)),
                   pl.BlockSpec(expand.shape, lambda b: (0, 0))])
    scratch = [pltpu.VMEM((s, HEAD_DIM + expand.shape[1]), _MXU), values_t, pltpu.VMEM((s, HEAD_DIM), _MXU), values_t]
    return _Mixer(_nsa_emit, in_specs,
                  [proj] * 3 + [kcv, *cmp_params, bias_diag, bias_sub, bias_cmp, cover_t, expand], scratch)


def _nsa_constants(s):
    n_cmp = (s - CMP_LEN) // CMP_STRIDE + 1
    n_slc = s // SLC_LEN
    assert n_cmp + 1 == s // CMP_STRIDE and n_slc % SUBLANES == 0
    c_start = np.arange(n_cmp) * CMP_STRIDE
    s_start = np.arange(n_slc) * SLC_LEN
    cover = np.clip(np.minimum((c_start + CMP_LEN - 1)[:, None], (s_start + SLC_LEN - 1)[None, :])
                    - np.maximum(c_start[:, None], s_start[None, :]) + 1, 0, None) / CMP_LEN
    cover_t = np.zeros((n_slc, n_cmp + 1), np.float32)
    cover_t[:, :n_cmp] = cover.T
    expand = (np.arange(s)[:, None] // SLC_LEN == np.arange(n_slc)[None, :]).astype(np.float32)
    return jnp.asarray(cover_t), jnp.asarray(expand, _MXU)


def kernel(x, w_in, w_out, w_up, w_down, norm_attn, norm_mlp, cmp_pos_k, cmp_pos_v, cmp_k_w1, cmp_k_w2,
           cmp_v_w1, cmp_v_w2, diff_lambda, diff_subln, rel_bias, final_norm):
    bsz, s, d = x.shape
    depth = w_in.shape[0]
    t = TILE
    n_chunk = s // CMP_STRIDE
    assert s == Q_TILES * t and MOBA_BLOCK % t == 0 and WINDOW % t == 0 and t >= MAX_DISTANCE

    assert w_in.shape[2] + PAD_COLS == D_IN_PAD
    w_in_c = jnp.pad(w_in.astype(_MXU), ((0, 0), (0, 0), (0, PAD_COLS)))
    w_out_c, w_up_c, w_down_c = (w.astype(_MXU) for w in (w_out, w_up, w_down))
    wk1, wk2 = cmp_k_w1.astype(_MXU), cmp_k_w2.astype(_MXU)
    wv1, wv2t = cmp_v_w1.astype(_MXU), jnp.swapaxes(cmp_v_w2, 1, 2).astype(_MXU)

    table_t = jnp.pad(rel_bias.T, ((0, 0), (0, LANES - N_BUCKETS)))
    tiles = dict(n_heads=rel_bias.shape[1], head0=0, rows=t, cols=t, col_tile=t, row_stride=-1, col_stride=1)
    bias_diag = _bias_table(table_t, offset=0, **tiles)
    bias_sub = _bias_table(table_t, offset=t, **tiles)
    bias_cmp = _bias_table(table_t, n_heads=GROUP_HEADS, head0=GROUP_HEADS, rows=n_chunk, cols=s,
                           col_tile=t, row_stride=-CMP_STRIDE, col_stride=1, offset=-(CMP_LEN - 1))
    cover_t, expand = _nsa_constants(s)
    col_scale = np.ones((1, D_IN_PAD), np.float32)
    for cb, width in ((CB_SB_Q, HEAD_DIM), (CB_MB_Q, HEAD_DIM), (CB_NS_Q, HEAD_DIM), (CB_DF_Q, DIFF_HALF)):
        col_scale[:, cb * GROUP_WIDTH:(cb + 1) * GROUP_WIDTH] = width ** -0.5 * LOG2E
    col_scale = jnp.asarray(col_scale)

    x2 = x.reshape(bsz * s, d)
    for layer in range(depth):
        proj, kcv = _norm_matmul(x2, norm_attn[layer][None], w_in_c, layer, col_scale,
                                 gap=(COLS_BEFORE_PAD, COLS_BEFORE_PAD + PAD_COLS),
                                 side_col=CB_NS_A * GROUP_WIDTH, side_width=2 * HEAD_DIM)
        proj = proj.reshape(bsz, s, D_IN_PAD)
        lambda_init = 0.8 - 0.6 * math.exp(-0.3 * layer)
        o_sb = _run_mixer(_stick_breaking(proj), bsz, s, "stick_breaking")
        o_mb = _run_mixer(_moba(proj, bias_diag, bias_sub), bsz, s, "moba")
        o_ns = _run_mixer(_nsa(proj, kcv.reshape(bsz, s, 2 * HEAD_DIM), (cmp_pos_k, cmp_pos_v, wk1, wk2, wv1, wv2t),
                               layer, bias_diag, bias_sub, bias_cmp, cover_t, expand), bsz, s, "nsa")
        o_df = _run_mixer(_diff(proj, diff_lambda[layer], diff_subln[layer][:, None], bias_diag, bias_sub,
                                lambda_init), bsz, s, "diff_attention")
        groups = [o.reshape(bsz * s, GROUP_WIDTH) for o in (o_sb, o_mb, o_ns, o_df)]
        x2 = _out_mlp(x2, groups, w_out_c, norm_mlp[layer][None], w_up_c, w_down_c, layer,
                      final_norm[None], final_norm=(layer == depth - 1))
    return x2.reshape(bsz, s, d)
```

```python
import functools
import math
from typing import Any, Callable, NamedTuple, Optional, Sequence

import numpy as np
import jax
import jax.numpy as jnp
from jax import lax
from jax.experimental import pallas as pl
from jax.experimental.pallas import tpu as pltpu

HEAD_DIM = 64
GROUP_HEADS = 4
GROUP_WIDTH = GROUP_HEADS * HEAD_DIM
NORM_EPS = 1e-6
NEG_INF = -1e30
BIG = 1e30
FORCE = 1e30
TINY = 1e-30
SOFTPLUS_CLAMP = 64.0
PICKED = -3e38
LOG2E = math.log2(math.e)
N_BUCKETS = 32
MAX_DISTANCE = 128
MOBA_BLOCK = 256
MOBA_TOPK = 3
CMP_LEN = 32
CMP_STRIDE = 16
SLC_LEN = 64
SLC_TOPN = 4
WINDOW = 512
DIFF_HALF = HEAD_DIM // 2
LANES = 128
SUBLANES = 8
TILE = 256
BF16_ROWS = 16
ACC_ROWS = HEAD_DIM + BF16_ROWS
N_GATES = 3 * GROUP_HEADS
COLS_BEFORE_PAD = 9 * GROUP_WIDTH - 2 * HEAD_DIM + N_GATES
PAD_COLS = 2 * HEAD_DIM - N_GATES
CB_SB_Q, CB_SB_K, CB_SB_V, CB_MB_Q, CB_MB_K, CB_MB_V, CB_NS_Q, CB_NS_A, CB_NS_B, CB_DF_Q, CB_DF_K, CB_DF_V = range(12)
D_IN_PAD = 12 * GROUP_WIDTH

_MXU = jnp.bfloat16
_VMEM_LIMIT = 56 * 1024 * 1024
_HEADS = range(GROUP_HEADS)
Q_TILES = 8
LOOKAHEAD = 2


def _dot(a, b, precision=None):
    return jnp.dot(a, b, precision=precision, preferred_element_type=jnp.float32)


def _dot_nt(a, b, precision=None):
    return lax.dot_general(a, b, (((1,), (1,)), ((), ())), precision=precision,
                           preferred_element_type=jnp.float32)


def _rms(x, g):
    return x * lax.rsqrt(jnp.mean(x * x, axis=-1, keepdims=True) + NORM_EPS) * g


def _params(*sem):
    return pltpu.CompilerParams(dimension_semantics=sem, vmem_limit_bytes=_VMEM_LIMIT)


def _norm_matmul_kernel(x_ref, g_ref, w_ref, scale_ref, o_ref, side_ref, wp_ref, *, tn, side_col, gap):
    @pl.when(pl.program_id(0) == 0)
    def _():
        lo, hi = gap
        wp_ref[:, :lo] = w_ref[:, :lo]
        wp_ref[:, lo:hi] = jnp.zeros((wp_ref.shape[0], hi - lo), wp_ref.dtype)
        wp_ref[:, hi:] = w_ref[:, lo:lo + wp_ref.shape[1] - hi]

    h = _rms(x_ref[...], g_ref[...]).astype(_MXU)
    for j in range(wp_ref.shape[1] // tn):
        cols = slice(j * tn, (j + 1) * tn)
        acc = _dot(h, wp_ref[:, cols])
        o_ref[:, cols] = (acc * scale_ref[:, cols]).astype(o_ref.dtype)
        if j * tn <= side_col < (j + 1) * tn:
            side_ref[...] = acc[:, side_col - j * tn:side_col - j * tn + side_ref.shape[1]]


def _norm_matmul(x, g, w, layer, col_scale, *, gap, side_col, side_width, tm=1024, tn=1024):
    m, d = x.shape
    n = w.shape[2]
    assert side_col % LANES == 0 and side_col // tn == (side_col + side_width - 1) // tn
    return pl.pallas_call(
        functools.partial(_norm_matmul_kernel, tn=tn, side_col=side_col, gap=gap),
        grid=(m // tm,),
        in_specs=[pl.BlockSpec((tm, d), lambda i: (i, 0)),
                  pl.BlockSpec((1, d), lambda i: (0, 0)),
                  pl.BlockSpec((None, d, w.shape[2]), lambda i: (layer, 0, 0), pipeline_mode=pl.Buffered(1)),
                  pl.BlockSpec((1, n), lambda i: (0, 0))],
        out_specs=[pl.BlockSpec((tm, n), lambda i: (i, 0)),
                   pl.BlockSpec((tm, side_width), lambda i: (i, 0))],
        out_shape=[jax.ShapeDtypeStruct((m, n), _MXU),
                   jax.ShapeDtypeStruct((m, side_width), jnp.float32)],
        scratch_shapes=[pltpu.VMEM((d, n), _MXU)],
        compiler_params=_params("arbitrary"),
        name="norm_in_proj",
    )(x, g, w, col_scale)


def _out_mlp_kernel(x_ref, a_ref, b_ref, c_ref, d_ref, wo_ref, g_ref, wu_ref, wd_ref, gf_ref, o_ref,
                    *, final_norm, tf):
    mixed = jnp.concatenate([a_ref[...], b_ref[...], c_ref[...], d_ref[...]], axis=1)
    y = x_ref[...] + _dot(mixed, wo_ref[...])
    h = _rms(y, g_ref[...]).astype(_MXU)
    for c in range(wu_ref.shape[1] // tf):
        u = jnp.square(jnp.maximum(_dot(h, wu_ref[:, c * tf:(c + 1) * tf]), 0.0))
        y = y + _dot(u.astype(_MXU), wd_ref[c * tf:(c + 1) * tf, :])
    if final_norm:
        y = _rms(y, gf_ref[...])
    o_ref[...] = y


def _out_mlp(x, groups, w_out, g, w_up, w_down, layer, g_final, *, final_norm, tm=512, tf=2048):
    m, d = x.shape
    f = w_up.shape[2]
    gspec = pl.BlockSpec((tm, GROUP_WIDTH), lambda i: (i, 0))
    row = pl.BlockSpec((1, d), lambda i: (0, 0))

    def resident(rows, cols):
        return pl.BlockSpec((None, rows, cols), lambda i: (layer, 0, 0), pipeline_mode=pl.Buffered(1))

    return pl.pallas_call(
        functools.partial(_out_mlp_kernel, final_norm=final_norm, tf=tf),
        grid=(m // tm,),
        in_specs=[pl.BlockSpec((tm, d), lambda i: (i, 0)), gspec, gspec, gspec, gspec,
                  resident(d, d), row, resident(d, f), resident(f, d), row],
        out_specs=pl.BlockSpec((tm, d), lambda i: (i, 0)),
        out_shape=jax.ShapeDtypeStruct((m, d), jnp.float32),
        compiler_params=_params("parallel"),
        name="out_proj_mlp_residual",
    )(x, *groups, w_out, g, w_up, w_down, g_final)


def _t5_bucket(dist):
    n = jnp.maximum(dist, 0)
    max_exact = N_BUCKETS // 2
    nf = jnp.maximum(n, 1).astype(jnp.float32)
    large = max_exact + (jnp.log(nf / max_exact) / math.log(MAX_DISTANCE / max_exact)
                         * (N_BUCKETS - max_exact)).astype(jnp.int32)
    large = jnp.minimum(large, N_BUCKETS - 1)
    return jnp.where(n < max_exact, n, large)


def _bias_kernel(tab_ref, o_ref, *, row_stride, col_stride, offset, head0):
    nh, tr, tc = o_ref.shape
    for blk in range(tc // LANES):
        rows = lax.broadcasted_iota(jnp.int32, (tr, LANES), 0)
        cols = lax.broadcasted_iota(jnp.int32, (tr, LANES), 1) + (pl.program_id(0) * tc + blk * LANES)
        bucket = _t5_bucket(rows * row_stride + cols * col_stride + offset)
        for h in range(nh):
            row = tab_ref[head0 + h:head0 + h + 1, :]
            row = (row - row[:, N_BUCKETS - 1:N_BUCKETS]) * LOG2E
            o_ref[h, :, blk * LANES:(blk + 1) * LANES] = jnp.take_along_axis(
                jnp.broadcast_to(row, (tr, LANES)), bucket, axis=1, mode="promise_in_bounds")


def _bias_table(table_t, *, n_heads, head0, rows, cols, col_tile, row_stride, col_stride, offset):
    return pl.pallas_call(
        functools.partial(_bias_kernel, row_stride=row_stride, col_stride=col_stride,
                          offset=offset, head0=head0),
        grid=(cols // col_tile,),
        in_specs=[pl.BlockSpec(table_t.shape, lambda i: (0, 0))],
        out_specs=pl.BlockSpec((n_heads, rows, col_tile), lambda i: (0, 0, i)),
        out_shape=jax.ShapeDtypeStruct((n_heads, rows, cols), jnp.float32),
        compiler_params=_params("parallel"),
        name="t5_bias_tiles",
    )(table_t)


def _softmax_init(t):
    return (jnp.full((1, t), NEG_INF, jnp.float32), jnp.zeros((ACC_ROWS, t), jnp.float32))


class _KeyTile(NamedTuple):
    kts: Sequence[Any]
    vts: Sequence[Any]
    biases: Optional[Sequence[Any]] = None
    emasks: Optional[Sequence[Any]] = None
    qmasks: Optional[Sequence[Any]] = None


def _round_robin(lists):
    out = []
    for rank in range(max(map(len, lists), default=0)):
        out.extend(items[rank] for items in lists if rank < len(items))
    return out


def _softmax_jobs(t, jobs):
    built = {}

    def tiles_of(job, g):
        if (job, g) not in built:
            built[job, g] = [make() for make in jobs[job][2][g]]
        return built[job, g]

    def scores_of(job, g, c):
        qts = jobs[job][1]
        row = []
        for tile in tiles_of(job, g):
            s = _dot(tile.kts[c], qts[c])
            if tile.biases is not None:
                s = s + tile.biases[c]
            if tile.emasks is not None:
                s = jnp.where(tile.emasks[c], s, NEG_INF)
            row.append(s.astype(_MXU))
        return row

    def update(state, job, g, c, scores):
        m, acc = state
        m_new = m
        for tile, s in zip(tiles_of(job, g), scores):
            tile_max = jnp.max(s, axis=0, keepdims=True).astype(jnp.float32)
            if tile.qmasks is not None:
                tile_max = jnp.where(tile.qmasks[c], tile_max, NEG_INF)
            m_new = jnp.maximum(m_new, tile_max)
        seen = m_new > 0.5 * NEG_INF
        acc = jnp.exp2(m - m_new) * acc
        for tile, s in zip(tiles_of(job, g), scores):
            ok = seen if tile.qmasks is None else jnp.logical_and(seen, tile.qmasks[c])
            acc = acc + _dot(tile.vts[c], jnp.exp2(s - jnp.where(ok, m_new, BIG).astype(_MXU)))
        return m_new, acc

    units = _round_robin([[(job, g, c) for g in range(len(groups)) for c in range(n)]
                          for job, (n, _, groups) in enumerate(jobs)])
    lookahead = max(LOOKAHEAD, len(jobs))
    states = [[_softmax_init(t) for _ in range(n)] for n, _, _ in jobs]
    pending = {k: scores_of(*units[k]) for k in range(min(lookahead, len(units)))}
    for k, (job, g, c) in enumerate(units):
        if k + lookahead < len(units):
            pending[k + lookahead] = scores_of(*units[k + lookahead])
        states[job][c] = update(states[job][c], job, g, c, pending.pop(k))
        yield
    return states


def _pairs(items):
    return [items[p:p + 2] for p in range(0, len(items), 2)]


class _Mixer(NamedTuple):
    emit: Callable[..., Any]
    in_specs: Sequence[Any]
    operands: Sequence[Any]
    scratch: Sequence[Any]


def _mixer_kernel(*refs, emit):
    for _ in emit(*refs):
        pass


def _run_mixer(mixer, bsz, s, name):
    return pl.pallas_call(
        functools.partial(_mixer_kernel, emit=mixer.emit),
        grid=(bsz,),
        in_specs=list(mixer.in_specs),
        out_specs=pl.BlockSpec((1, s, GROUP_WIDTH), lambda b: (b, 0, 0)),
        out_shape=jax.ShapeDtypeStruct((bsz, s, GROUP_WIDTH), _MXU),
        scratch_shapes=list(mixer.scratch),
        compiler_params=_params("parallel"),
        name=name,
    )(*mixer.operands)


def _softmax_out(state):
    acc = state[1]
    return acc[:HEAD_DIM] / jnp.maximum(acc[HEAD_DIM:HEAD_DIM + 1], TINY)


def _top_k_rows(score, row_f, k):
    sel = jnp.zeros(score.shape, jnp.float32)
    for _ in range(k):
        mx = jnp.max(score, axis=0, keepdims=True)
        idx = jnp.min(jnp.where(score == mx, row_f, float(score.shape[0])), axis=0, keepdims=True)
        pick = row_f == idx
        sel = jnp.where(pick, 1.0, sel)
        score = jnp.where(pick, PICKED, score)
    return sel


def _now(fn):
    fn()


def _tile_iotas(t):
    return (lax.broadcasted_iota(jnp.int32, (t, t), 0), lax.broadcasted_iota(jnp.int32, (t, t), 1))


def _key_rows(ref, j, t):
    if isinstance(j, int):
        return ref[j * t:(j + 1) * t, :]
    return ref[pl.ds(pl.multiple_of(j * t, t), t), :]


def _transposed(ref_block):
    return ref_block.astype(jnp.float32).T


def _queries_t(q_ref, width, t):
    qt = _transposed(q_ref[0]).astype(_MXU)
    return [[qt[c * width:(c + 1) * width, r * t:(r + 1) * t] for c in range(GROUP_WIDTH // width)]
            for r in range(Q_TILES)]


def _fill_values_t(vt_ref, v_ref, col0, t):
    n_tiles, rows, _ = vt_ref.shape
    lane_block = (col0 // LANES) * LANES
    for c in range(n_tiles):
        blk = _transposed(v_ref[0, c * t:(c + 1) * t, lane_block:lane_block + LANES])
        vt_ref[c, 0:HEAD_DIM, :] = blk[col0 - lane_block:col0 - lane_block + HEAD_DIM].astype(vt_ref.dtype)
        if rows == ACC_ROWS:
            first = lax.broadcasted_iota(jnp.int32, (rows - HEAD_DIM, t), 0) == 0
            vt_ref[c, HEAD_DIM:rows, :] = jnp.where(first, 1.0, 0.0).astype(vt_ref.dtype)


def _group_specs(s, *column_blocks):
    return [pl.BlockSpec((1, s, GROUP_WIDTH), functools.partial(lambda cb, b: (b, 0, cb), cb))
            for cb in column_blocks]


def _bias_specs(t, head_group):
    spec = pl.BlockSpec((GROUP_HEADS, t, t), lambda b: (head_group, 0, 0))
    return [spec, spec]


def _store_heads(o_ref, outs_t):
    tiles = [jnp.concatenate(heads, axis=0) for heads in outs_t]
    o_ref[0] = jnp.concatenate(tiles, axis=1).T.astype(o_ref.dtype)


def _sb_emit(q_ref, k_ref, v_ref, o_ref, kb_ref, vt_ref):
    t = q_ref.shape[1] // Q_TILES
    key, qry = _tile_iotas(t)
    strict = key < qry
    later = jnp.where(qry > key, 1.0, 0.0).astype(_MXU)

    @_now
    def _():
        for h in _HEADS:
            kb_ref[h] = k_ref[0, :, h * HEAD_DIM:(h + 1) * HEAD_DIM].astype(kb_ref.dtype)
            _fill_values_t(vt_ref.at[h], v_ref, h * HEAD_DIM, t)

    def query_step(step):
        qts = _queries_t(q_ref, HEAD_DIM, t)
        units = _round_robin([[(r, step * Q_TILES + r, pair, h)
                               for pair in _pairs(list(range(step * Q_TILES + r, -1, -1))) for h in _HEADS]
                              for r in range(Q_TILES)])
        zero = (jnp.zeros((HEAD_DIM, t), jnp.float32), jnp.zeros((1, t), jnp.float32))
        carry = [[zero] * GROUP_HEADS for _ in range(Q_TILES)]
        zs, log_keeps, suffixes = {}, {}, {}

        def scores(u):
            r, _, pair, h = units[u]
            zs[u] = [_dot(_key_rows(kb_ref.at[h], j, t), qts[r][h]) for j in pair]

        def keeps(u):
            _, k, pair, _ = units[u]
            log_keeps[u], suffixes[u] = [], []
            for j, z in zip(pair, zs[u]):
                drop = jnp.maximum(jnp.log2(1.0 + jnp.exp2(jnp.minimum(z, SOFTPLUS_CLAMP))), z)
                if j == k:
                    drop = jnp.where(strict, drop, 0.0)
                log_keeps[u].append(drop)
                suffixes[u].append(_dot(later, drop.astype(_MXU)))

        def values(u):
            r, k, pair, h = units[u]
            acc, run = carry[r][h]
            weights = []
            for j, z, drop, suffix in zip(pair, zs.pop(u), log_keeps.pop(u), suffixes.pop(u)):
                a = jnp.exp2(z - drop - suffix + run)
                if j == k:
                    a = jnp.where(strict, a, 0.0)
                weights.append(a.astype(_MXU))
                run = run - (suffix[0:1] + drop[0:1])
            for j, w in zip(pair, weights):
                acc = acc + _dot(vt_ref[h, j], w)
            carry[r][h] = (acc, run)

        stages = (scores, keeps, values)
        for tick in range(len(units) + len(stages) - 1):
            for lag, stage in enumerate(stages):
                if 0 <= tick - lag < len(units):
                    stage(tick - lag)
            yield
        _store_heads(o_ref, [[c[0] for c in tile_carry] for tile_carry in carry])

    yield from query_step(0)


def _stick_breaking(proj, t=TILE):
    s = proj.shape[1]
    scratch = [pltpu.VMEM((GROUP_HEADS, s, HEAD_DIM), _MXU), pltpu.VMEM((GROUP_HEADS, s // t, HEAD_DIM, t), _MXU)]
    return _Mixer(_sb_emit, _group_specs(s, CB_SB_Q, CB_SB_K, CB_SB_V), [proj] * 3, scratch)


def _moba_emit(q_ref, k_ref, v_ref, bd_ref, bs_ref, o_ref, kb_ref, vt_ref, km_ref):
    t = q_ref.shape[1] // Q_TILES
    n_blk = k_ref.shape[1] // MOBA_BLOCK
    tiles_per_blk = MOBA_BLOCK // t
    blk_shift = int(math.log2(tiles_per_blk))

    @_now
    def _():
        km_ref[...] = jnp.zeros_like(km_ref)
        for h in _HEADS:
            lo, hi = h * HEAD_DIM, (h + 1) * HEAD_DIM
            kb_ref[h] = k_ref[0, :, lo:hi].astype(kb_ref.dtype)
            _fill_values_t(vt_ref.at[h], v_ref, lo, t)
            for n in range(n_blk):
                blk = k_ref[0, n * MOBA_BLOCK:(n + 1) * MOBA_BLOCK, lo:hi]
                km_ref[h, n:n + 1, :] = jnp.mean(blk.astype(jnp.float32), axis=0, keepdims=True)

    def query_step(step):
        key, qry = _tile_iotas(t)
        causal = key <= qry
        blk_row = lax.broadcasted_iota(jnp.int32, (km_ref.shape[1], t), 0)
        qt = _transposed(q_ref[0])
        jobs = []
        for r in range(Q_TILES):
            k = step * Q_TILES + r
            own = k >> blk_shift
            qts, sels = [], []
            for h in _HEADS:
                qf = qt[h * HEAD_DIM:(h + 1) * HEAD_DIM, r * t:(r + 1) * t]
                qts.append(qf.astype(_MXU))
                gate = _dot(km_ref[h], qf, precision=lax.Precision.HIGHEST)
                gate = jnp.where(blk_row < own, gate, NEG_INF)
                sel = _top_k_rows(gate, blk_row.astype(jnp.float32), min(MOBA_TOPK, n_blk - 1))
                sels.append(jnp.where(blk_row < own, sel, 0.0))

            def tile(own, sels, j, bias_ref=None, emask=None):
                n = j >> blk_shift
                return _KeyTile([_key_rows(kb_ref.at[h], j, t) for h in _HEADS], [vt_ref[h, j] for h in _HEADS],
                                None if bias_ref is None else [bias_ref[h] for h in _HEADS],
                                None if emask is None else [emask] * GROUP_HEADS,
                                None if n == own else [sels[h][n:n + 1] > 0.5 for h in _HEADS])

            tile = functools.partial(tile, own, sels)
            groups = _pairs([functools.partial(tile, j) for j in range(k - 1)])
            groups.append(([functools.partial(tile, k - 1, bs_ref)] if k else [])
                          + [functools.partial(tile, k, bd_ref, causal)])
            jobs.append((GROUP_HEADS, qts, groups))
        all_states = yield from _softmax_jobs(t, jobs)
        _store_heads(o_ref, [[_softmax_out(st) for st in states] for states in all_states])

    yield from query_step(0)


def _kv_scratch(s, t, key_dim=HEAD_DIM, n_keys=GROUP_HEADS):
    return [pltpu.VMEM((n_keys, s, key_dim), _MXU), pltpu.VMEM((GROUP_HEADS, s // t, ACC_ROWS, t), _MXU)]


def _moba(proj, bias_diag, bias_sub, t=TILE):
    s = proj.shape[1]
    n_blk_pad = -(-(s // MOBA_BLOCK) // SUBLANES) * SUBLANES
    scratch = _kv_scratch(s, t) + [pltpu.VMEM((GROUP_HEADS, n_blk_pad, HEAD_DIM), jnp.float32)]
    return _Mixer(_moba_emit, _group_specs(s, CB_MB_Q, CB_MB_K, CB_MB_V) + _bias_specs(t, 0),
                  [proj] * 3 + [bias_diag, bias_sub], scratch)


def _diff_emit(lam_ref, g_ref, q_ref, k_ref, v_ref, bd_ref, bs_ref, o_ref, kb_ref, vt_ref, *, lambda_init):
    t = q_ref.shape[1] // Q_TILES
    key, qry = _tile_iotas(t)
    causal = key <= qry
    lv = lam_ref[...]
    lam = (jnp.exp(jnp.sum(lv[0:1] * lv[1:2], keepdims=True))
           - jnp.exp(jnp.sum(lv[2:3] * lv[3:4], keepdims=True)) + lambda_init)
    halves = range(2 * GROUP_HEADS)

    @_now
    def _():
        for c in halves:
            kb_ref[c] = k_ref[0, :, c * DIFF_HALF:(c + 1) * DIFF_HALF].astype(kb_ref.dtype)
        for h in _HEADS:
            _fill_values_t(vt_ref.at[h], v_ref, h * HEAD_DIM, t)

    def tile(j, bias_ref=None, emask=None):
        n = len(halves)
        return _KeyTile([_key_rows(kb_ref.at[c], j, t) for c in halves], [vt_ref[c // 2, j] for c in halves],
                        None if bias_ref is None else [bias_ref[c // 2] for c in halves],
                        None if emask is None else [emask] * n)

    def query_step(step):
        qts = _queries_t(q_ref, DIFF_HALF, t)
        jobs = []
        for r in range(Q_TILES):
            k = step * Q_TILES + r
            groups = _pairs([functools.partial(tile, j) for j in range(k - 1)])
            groups.append(([functools.partial(tile, k - 1, bs_ref)] if k else [])
                          + [functools.partial(tile, k, bd_ref, causal)])
            jobs.append((len(halves), qts[r], groups))
        all_states = yield from _softmax_jobs(t, jobs)
        outs = []
        for states in all_states:
            heads = []
            for h in _HEADS:
                o = _softmax_out(states[2 * h]) - lam * _softmax_out(states[2 * h + 1])
                o = o * lax.rsqrt(jnp.mean(o * o, axis=0, keepdims=True) + NORM_EPS) * g_ref[...]
                heads.append(o * (1.0 - lambda_init))
            outs.append(heads)
        _store_heads(o_ref, outs)

    yield from query_step(0)


def _diff(proj, lam_params, subln, bias_diag, bias_sub, lambda_init, t=TILE):
    s = proj.shape[1]
    in_specs = ([pl.BlockSpec(lam_params.shape, lambda b: (0, 0)), pl.BlockSpec(subln.shape, lambda b: (0, 0))]
                + _group_specs(s, CB_DF_Q, CB_DF_K, CB_DF_V) + _bias_specs(t, 2))
    return _Mixer(functools.partial(_diff_emit, lambda_init=lambda_init), in_specs,
                  [lam_params, subln] + [proj] * 3 + [bias_diag, bias_sub],
                  _kv_scratch(s, t, key_dim=DIFF_HALF, n_keys=2 * GROUP_HEADS))


def _compress_blocks(kcv_ref, pk_ref, pv_ref, wk1_ref, wk2_ref, wv1_ref, wv2t_ref):
    n_chunk = kcv_ref.shape[1] // CMP_STRIDE

    branches = ((0, pk_ref, wk1_ref), (HEAD_DIM, pv_ref, wv1_ref))
    tops = [jnp.zeros((n_chunk, w1_ref.shape[1]), jnp.float32) for _, _, w1_ref in branches]
    bots = list(tops)
    pack = 2 * LANES // HEAD_DIM
    for l0 in range(0, CMP_STRIDE, pack):
        tokens = [kcv_ref[0, pl.ds(l, n_chunk, stride=CMP_STRIDE), :] for l in range(l0, l0 + pack)]
        for n, (col0, p_ref, w1_ref) in enumerate(branches):
            for half, acc in ((0, tops), (CMP_STRIDE, bots)):
                x = jnp.concatenate([(tok[:, col0:col0 + HEAD_DIM] + p_ref[half + l:half + l + 1, :]).astype(_MXU)
                                     for l, tok in zip(range(l0, l0 + pack), tokens)], axis=1)
                acc[n] = acc[n] + _dot(x, w1_ref[(half + l0) * HEAD_DIM:(half + l0 + pack) * HEAD_DIM, :])
    hidden = [jax.nn.gelu(top + pltpu.roll(bot, n_chunk - 1, axis=0)).astype(_MXU) for top, bot in zip(tops, bots)]
    return _dot(hidden[0], wk2_ref[...]), _dot_nt(wv2t_ref[...], hidden[1])


def _nsa_emit(q_ref, kva_ref, kvb_ref, kcv_ref, pk_ref, pv_ref, wk1_ref, wk2_ref, wv1_ref, wv2t_ref,
              bd_ref, bs_ref, bc_ref, cover_ref, e_ref, o_ref, ks_ref, vst_ref, kw_ref, vwt_ref):
    t = q_ref.shape[1] // Q_TILES
    key, qry = _tile_iotas(t)
    causal = key <= qry
    ks_col, vs_col, kw_col, vw_col, gate_col = 2 * HEAD_DIM, 3 * HEAD_DIM, 0, HEAD_DIM, 2 * HEAD_DIM

    @_now
    def _():
        ks_ref[:, :HEAD_DIM] = kva_ref[0, :, ks_col:ks_col + HEAD_DIM].astype(ks_ref.dtype)
        ks_ref[:, HEAD_DIM:] = e_ref[...]
        kw_ref[...] = kvb_ref[0, :, kw_col:kw_col + HEAD_DIM].astype(kw_ref.dtype)
        _fill_values_t(vst_ref, kva_ref, vs_col, t)
        _fill_values_t(vwt_ref, kvb_ref, vw_col, t)

    def tile(k_ref, vt_ref, j, bias_ref=None, emask=None):
        n = GROUP_HEADS
        return _KeyTile([_key_rows(k_ref, j, t)] * n, [vt_ref[j]] * n,
                        None if bias_ref is None else [bias_ref[h] for h in _HEADS],
                        None if emask is None else [emask] * n)

    def query_step(step):
        qts = _queries_t(q_ref, HEAD_DIM, t)
        kc, vct = (a.astype(_MXU) for a in _compress_blocks(kcv_ref, pk_ref, pv_ref, wk1_ref, wk2_ref,
                                                            wv1_ref, wv2t_ref))
        n_cmp = kc.shape[0]
        n_slc = cover_ref.shape[0]
        c_row = lax.broadcasted_iota(jnp.int32, (n_cmp, t), 0)
        c_col = lax.broadcasted_iota(jnp.int32, (n_cmp, t), 1)
        s_row = lax.broadcasted_iota(jnp.int32, (n_slc, t), 0)
        s_col = lax.broadcasted_iota(jnp.int32, (n_slc, t), 1)
        tiles = [step * Q_TILES + r for r in range(Q_TILES)]

        o_cmp, importance = [], []
        for r, k in enumerate(tiles):
            visible = c_col + k * t >= c_row * CMP_STRIDE + (CMP_LEN - 1)
            cmp_scores = [_dot(kc, qts[r][h]) for h in _HEADS]
            cmp_probs = []
            p_sum = jnp.zeros((n_cmp, t), jnp.float32)
            for h in _HEADS:
                sc = jnp.where(visible, cmp_scores[h] + bc_ref[h, :, r * t:(r + 1) * t], NEG_INF)
                e = jnp.where(visible, jnp.exp2(sc - jnp.max(sc, axis=0, keepdims=True)), 0.0)
                p = e / jnp.maximum(jnp.sum(e, axis=0, keepdims=True), TINY)
                cmp_probs.append(p.astype(_MXU))
                p_sum = p_sum + p
            o_cmp.append([_dot(vct, cmp_probs[h]) for h in _HEADS])
            importance.append(_dot(cover_ref[...], p_sum, precision=lax.Precision.HIGHEST))

        n_back = WINDOW // t
        jobs = []
        for r, k in enumerate(tiles):
            window = [functools.partial(tile, kw_ref, vwt_ref, k - n_back, None, qry < key)] if k >= n_back else []
            for back in range(min(n_back - 1, k), 0, -1):
                window.append(functools.partial(tile, kw_ref, vwt_ref, k - back, bs_ref if back == 1 else None))
            window.append(functools.partial(tile, kw_ref, vwt_ref, k, bd_ref, causal))
            jobs.append((GROUP_HEADS, qts[r], [window]))
        win_states = yield from _softmax_jobs(t, jobs)
        o_win = [[_softmax_out(st) for st in states] for states in win_states]

        jobs = []
        for r, k in enumerate(tiles):
            own = jnp.right_shift(s_col + k * t, int(math.log2(SLC_LEN)))
            score = jnp.where(s_row == own, FORCE, jnp.where(s_row < own, importance[r], NEG_INF))
            sel = _top_k_rows(score, s_row.astype(jnp.float32), min(SLC_TOPN, n_slc))
            penalty = jnp.where(sel > 0.5, 0.0, NEG_INF).astype(_MXU)
            q_aug = [jnp.concatenate([qts[r][h], penalty], axis=0) for h in _HEADS]
            groups = _pairs([functools.partial(tile, ks_ref, vst_ref, j) for j in range(k - 1)])
            groups.append(([functools.partial(tile, ks_ref, vst_ref, k - 1, bs_ref)] if k else [])
                          + [functools.partial(tile, ks_ref, vst_ref, k, bd_ref, causal)])
            jobs.append((GROUP_HEADS, q_aug, groups))
        slc_states = yield from _softmax_jobs(t, jobs)
        o_slc = [[_softmax_out(st) for st in states] for states in slc_states]

        gates = _transposed(kvb_ref[0, :, (gate_col // LANES) * LANES:(gate_col // LANES + 1) * LANES])
        gates = 1.0 / (1.0 + jnp.exp(-gates[gate_col % LANES:gate_col % LANES + N_GATES + 4]))
        outs = []
        for r in range(Q_TILES):
            heads = []
            for h in _HEADS:
                g = [gates[br * GROUP_HEADS + h:br * GROUP_HEADS + h + 1, r * t:(r + 1) * t] for br in range(3)]
                heads.append(g[0] * o_cmp[r][h] + g[1] * o_slc[r][h] + g[2] * o_win[r][h])
            outs.append(heads)
        _store_heads(o_ref, outs)

    yield from query_step(0)


def _nsa(proj, kcv, cmp_params, layer, bias_diag, bias_sub, bias_cmp, cover_t, expand, t=TILE):
    s = proj.shape[1]
    values_t = pltpu.VMEM((s // t, ACC_ROWS, t), _MXU)
    in_specs = (_group_specs(s, CB_NS_Q, CB_NS_A, CB_NS_B)
                + [pl.BlockSpec((1, s, kcv.shape[2]), lambda b: (b, 0, 0))]
                + [pl.BlockSpec((None,) + a.shape[1:], lambda b: (layer, 0, 0)) for a in cmp_params]
                + _bias_specs(t, 1)
                + [pl.BlockSpec(bias_cmp.shape, lambda b: (0, 0, 0)),
                   pl.BlockSpec(cover_t.shape, lambda b: (0, 0)),
                   pl.BlockSpec(expand.shape, lambda b: (0, 0))])
    scratch = [pltpu.VMEM((s, HEAD_DIM + expand.shape[1]), _MXU), values_t, pltpu.VMEM((s, HEAD_DIM), _MXU), values_t]
    return _Mixer(_nsa_emit, in_specs,
                  [proj] * 3 + [kcv, *cmp_params, bias_diag, bias_sub, bias_cmp, cover_t, expand], scratch)


def _nsa_constants(s):
    n_cmp = (s - CMP_LEN) // CMP_STRIDE + 1
    n_slc = s // SLC_LEN
    assert n_cmp + 1 == s // CMP_STRIDE and n_slc % SUBLANES == 0
    c_start = np.arange(n_cmp) * CMP_STRIDE
    s_start = np.arange(n_slc) * SLC_LEN
    cover = np.clip(np.minimum((c_start + CMP_LEN - 1)[:, None], (s_start + SLC_LEN - 1)[None, :])
                    - np.maximum(c_start[:, None], s_start[None, :]) + 1, 0, None) / CMP_LEN
    cover_t = np.zeros((n_slc, n_cmp + 1), np.float32)
    cover_t[:, :n_cmp] = cover.T
    expand = (np.arange(s)[:, None] // SLC_LEN == np.arange(n_slc)[None, :]).astype(np.float32)
    return jnp.asarray(cover_t), jnp.asarray(expand, _MXU)


def kernel(x, w_in, w_out, w_up, w_down, norm_attn, norm_mlp, cmp_pos_k, cmp_pos_v, cmp_k_w1, cmp_k_w2,
           cmp_v_w1, cmp_v_w2, diff_lambda, diff_subln, rel_bias, final_norm):
    bsz, s, d = x.shape
    depth = w_in.shape[0]
    t = TILE
    n_chunk = s // CMP_STRIDE
    assert s == Q_TILES * t and MOBA_BLOCK % t == 0 and WINDOW % t == 0 and t >= MAX_DISTANCE

    assert w_in.shape[2] + PAD_COLS == D_IN_PAD
    w_in_c = jnp.pad(w_in.astype(_MXU), ((0, 0), (0, 0), (0, PAD_COLS)))
    w_out_c, w_up_c, w_down_c = (w.astype(_MXU) for w in (w_out, w_up, w_down))
    wk1, wk2 = cmp_k_w1.astype(_MXU), cmp_k_w2.astype(_MXU)
    wv1, wv2t = cmp_v_w1.astype(_MXU), jnp.swapaxes(cmp_v_w2, 1, 2).astype(_MXU)

    table_t = jnp.pad(rel_bias.T, ((0, 0), (0, LANES - N_BUCKETS)))
    tiles = dict(n_heads=rel_bias.shape[1], head0=0, rows=t, cols=t, col_tile=t, row_stride=-1, col_stride=1)
    bias_diag = _bias_table(table_t, offset=0, **tiles)
    bias_sub = _bias_table(table_t, offset=t, **tiles)
    bias_cmp = _bias_table(table_t, n_heads=GROUP_HEADS, head0=GROUP_HEADS, rows=n_chunk, cols=s,
                           col_tile=t, row_stride=-CMP_STRIDE, col_stride=1, offset=-(CMP_LEN - 1))
    cover_t, expand = _nsa_constants(s)
    col_scale = np.ones((1, D_IN_PAD), np.float32)
    for cb, width in ((CB_SB_Q, HEAD_DIM), (CB_MB_Q, HEAD_DIM), (CB_NS_Q, HEAD_DIM), (CB_DF_Q, DIFF_HALF)):
        col_scale[:, cb * GROUP_WIDTH:(cb + 1) * GROUP_WIDTH] = width ** -0.5 * LOG2E
    col_scale = jnp.asarray(col_scale)

    x2 = x.reshape(bsz * s, d)
    for layer in range(depth):
        proj, kcv = _norm_matmul(x2, norm_attn[layer][None], w_in_c, layer, col_scale,
                                 gap=(COLS_BEFORE_PAD, COLS_BEFORE_PAD + PAD_COLS),
                                 side_col=CB_NS_A * GROUP_WIDTH, side_width=2 * HEAD_DIM)
        proj = proj.reshape(bsz, s, D_IN_PAD)
        lambda_init = 0.8 - 0.6 * math.exp(-0.3 * layer)
        o_sb = _run_mixer(_stick_breaking(proj), bsz, s, "stick_breaking")
        o_mb = _run_mixer(_moba(proj, bias_diag, bias_sub), bsz, s, "moba")
        o_ns = _run_mixer(_nsa(proj, kcv.reshape(bsz, s, 2 * HEAD_DIM), (cmp_pos_k, cmp_pos_v, wk1, wk2, wv1, wv2t),
                               layer, bias_diag, bias_sub, bias_cmp, cover_t, expand), bsz, s, "nsa")
        o_df = _run_mixer(_diff(proj, diff_lambda[layer], diff_subln[layer][:, None], bias_diag, bias_sub,
                                lambda_init), bsz, s, "diff_attention")
        groups = [o.reshape(bsz * s, GROUP_WIDTH) for o in (o_sb, o_mb, o_ns, o_df)]
        x2 = _out_mlp(x2, groups, w_out_c, norm_mlp[layer][None], w_up_c, w_down_c, layer,
                      final_norm[None], final_norm=(layer == depth - 1))
    return x2.reshape(bsz, s, d)
```

```python
import functools
import math
from typing import Any, Callable, NamedTuple, Optional, Sequence

import numpy as np
import jax
import jax.numpy as jnp
from jax import lax
from jax.experimental import pallas as pl
from jax.experimental.pallas import tpu as pltpu

HEAD_DIM = 64
GROUP_HEADS = 4
GROUP_WIDTH = GROUP_HEADS * HEAD_DIM
NORM_EPS = 1e-6
NEG_INF = -1e30
BIG = 1e30
FORCE = 1e30
TINY = 1e-30
SOFTPLUS_CLAMP = 64.0
PICKED = -3e38
LOG2E = math.log2(math.e)
N_BUCKETS = 32
MAX_DISTANCE = 128
MOBA_BLOCK = 256
MOBA_TOPK = 3
CMP_LEN = 32
CMP_STRIDE = 16
SLC_LEN = 64
SLC_TOPN = 4
WINDOW = 512
DIFF_HALF = HEAD_DIM // 2
LANES = 128
SUBLANES = 8
TILE = 256
BF16_ROWS = 16
ACC_ROWS = HEAD_DIM + BF16_ROWS
N_GATES = 3 * GROUP_HEADS
COLS_BEFORE_PAD = 9 * GROUP_WIDTH - 2 * HEAD_DIM + N_GATES
PAD_COLS = 2 * HEAD_DIM - N_GATES
CB_SB_Q, CB_SB_K, CB_SB_V, CB_MB_Q, CB_MB_K, CB_MB_V, CB_NS_Q, CB_NS_A, CB_NS_B, CB_DF_Q, CB_DF_K, CB_DF_V = range(12)
D_IN_PAD = 12 * GROUP_WIDTH

_MXU = jnp.bfloat16
_VMEM_LIMIT = 56 * 1024 * 1024
_HEADS = range(GROUP_HEADS)
Q_TILES = 8
LOOKAHEAD = 2


def _dot(a, b, precision=None):
    return jnp.dot(a, b, precision=precision, preferred_element_type=jnp.float32)


def _dot_nt(a, b, precision=None):
    return lax.dot_general(a, b, (((1,), (1,)), ((), ())), precision=precision,
                           preferred_element_type=jnp.float32)


def _rms(x, g):
    return x * lax.rsqrt(jnp.mean(x * x, axis=-1, keepdims=True) + NORM_EPS) * g


def _params(*sem):
    return pltpu.CompilerParams(dimension_semantics=sem, vmem_limit_bytes=_VMEM_LIMIT)


def _norm_matmul_kernel(x_ref, g_ref, w_ref, scale_ref, o_ref, side_ref, wp_ref, *, tn, side_col, gap):
    @pl.when(pl.program_id(0) == 0)
    def _():
        lo, hi = gap
        wp_ref[:, :lo] = w_ref[:, :lo]
        wp_ref[:, lo:hi] = jnp.zeros((wp_ref.shape[0], hi - lo), wp_ref.dtype)
        wp_ref[:, hi:] = w_ref[:, lo:lo + wp_ref.shape[1] - hi]

    h = _rms(x_ref[...], g_ref[...]).astype(_MXU)
    for j in range(wp_ref.shape[1] // tn):
        cols = slice(j * tn, (j + 1) * tn)
        acc = _dot(h, wp_ref[:, cols])
        o_ref[:, cols] = (acc * scale_ref[:, cols]).astype(o_ref.dtype)
        if j * tn <= side_col < (j + 1) * tn:
            side_ref[...] = acc[:, side_col - j * tn:side_col - j * tn + side_ref.shape[1]]


def _norm_matmul(x, g, w, layer, col_scale, *, gap, side_col, side_width, tm=1024, tn=1024):
    m, d = x.shape
    n = w.shape[2]
    assert side_col % LANES == 0 and side_col // tn == (side_col + side_width - 1) // tn
    return pl.pallas_call(
        functools.partial(_norm_matmul_kernel, tn=tn, side_col=side_col, gap=gap),
        grid=(m // tm,),
        in_specs=[pl.BlockSpec((tm, d), lambda i: (i, 0)),
                  pl.BlockSpec((1, d), lambda i: (0, 0)),
                  pl.BlockSpec((None, d, w.shape[2]), lambda i: (layer, 0, 0), pipeline_mode=pl.Buffered(1)),
                  pl.BlockSpec((1, n), lambda i: (0, 0))],
        out_specs=[pl.BlockSpec((tm, n), lambda i: (i, 0)),
                   pl.BlockSpec((tm, side_width), lambda i: (i, 0))],
        out_shape=[jax.ShapeDtypeStruct((m, n), _MXU),
                   jax.ShapeDtypeStruct((m, side_width), jnp.float32)],
        scratch_shapes=[pltpu.VMEM((d, n), _MXU)],
        compiler_params=_params("arbitrary"),
        name="norm_in_proj",
    )(x, g, w, col_scale)


def _out_mlp_kernel(x_ref, a_ref, b_ref, c_ref, d_ref, wo_ref, g_ref, wu_ref, wd_ref, gf_ref, o_ref,
                    *, final_norm, tf):
    mixed = jnp.concatenate([a_ref[...], b_ref[...], c_ref[...], d_ref[...]], axis=1)
    y = x_ref[...] + _dot(mixed, wo_ref[...])
    h = _rms(y, g_ref[...]).astype(_MXU)
    for c in range(wu_ref.shape[1] // tf):
        u = jnp.square(jnp.maximum(_dot(h, wu_ref[:, c * tf:(c + 1) * tf]), 0.0))
        y = y + _dot(u.astype(_MXU), wd_ref[c * tf:(c + 1) * tf, :])
    if final_norm:
        y = _rms(y, gf_ref[...])
    o_ref[...] = y


def _out_mlp(x, groups, w_out, g, w_up, w_down, layer, g_final, *, final_norm, tm=1024, tf=1024):
    m, d = x.shape
    f = w_up.shape[2]
    gspec = pl.BlockSpec((tm, GROUP_WIDTH), lambda i: (i, 0))
    row = pl.BlockSpec((1, d), lambda i: (0, 0))

    def resident(rows, cols):
        return pl.BlockSpec((None, rows, cols), lambda i: (layer, 0, 0), pipeline_mode=pl.Buffered(1))

    return pl.pallas_call(
        functools.partial(_out_mlp_kernel, final_norm=final_norm, tf=tf),
        grid=(m // tm,),
        in_specs=[pl.BlockSpec((tm, d), lambda i: (i, 0)), gspec, gspec, gspec, gspec,
                  resident(d, d), row, resident(d, f), resident(f, d), row],
        out_specs=pl.BlockSpec((tm, d), lambda i: (i, 0)),
        out_shape=jax.ShapeDtypeStruct((m, d), jnp.float32),
        compiler_params=_params("parallel"),
        name="out_proj_mlp_residual",
    )(x, *groups, w_out, g, w_up, w_down, g_final)


def _t5_bucket(dist):
    n = jnp.maximum(dist, 0)
    max_exact = N_BUCKETS // 2
    nf = jnp.maximum(n, 1).astype(jnp.float32)
    large = max_exact + (jnp.log(nf / max_exact) / math.log(MAX_DISTANCE / max_exact)
                         * (N_BUCKETS - max_exact)).astype(jnp.int32)
    large = jnp.minimum(large, N_BUCKETS - 1)
    return jnp.where(n < max_exact, n, large)


def _bias_kernel(tab_ref, o_ref, *, row_stride, col_stride, offset, head0):
    nh, tr, tc = o_ref.shape
    for blk in range(tc // LANES):
        rows = lax.broadcasted_iota(jnp.int32, (tr, LANES), 0)
        cols = lax.broadcasted_iota(jnp.int32, (tr, LANES), 1) + (pl.program_id(0) * tc + blk * LANES)
        bucket = _t5_bucket(rows * row_stride + cols * col_stride + offset)
        for h in range(nh):
            row = tab_ref[head0 + h:head0 + h + 1, :]
            row = (row - row[:, N_BUCKETS - 1:N_BUCKETS]) * LOG2E
            o_ref[h, :, blk * LANES:(blk + 1) * LANES] = jnp.take_along_axis(
                jnp.broadcast_to(row, (tr, LANES)), bucket, axis=1, mode="promise_in_bounds")


def _bias_table(table_t, *, n_heads, head0, rows, cols, col_tile, row_stride, col_stride, offset):
    return pl.pallas_call(
        functools.partial(_bias_kernel, row_stride=row_stride, col_stride=col_stride,
                          offset=offset, head0=head0),
        grid=(cols // col_tile,),
        in_specs=[pl.BlockSpec(table_t.shape, lambda i: (0, 0))],
        out_specs=pl.BlockSpec((n_heads, rows, col_tile), lambda i: (0, 0, i)),
        out_shape=jax.ShapeDtypeStruct((n_heads, rows, cols), jnp.float32),
        compiler_params=_params("parallel"),
        name="t5_bias_tiles",
    )(table_t)


def _softmax_init(t):
    return (jnp.full((1, t), NEG_INF, jnp.float32), jnp.zeros((ACC_ROWS, t), jnp.float32))


class _KeyTile(NamedTuple):
    kts: Sequence[Any]
    vts: Sequence[Any]
    biases: Optional[Sequence[Any]] = None
    emasks: Optional[Sequence[Any]] = None
    qmasks: Optional[Sequence[Any]] = None


def _round_robin(lists):
    out = []
    for rank in range(max(map(len, lists), default=0)):
        out.extend(items[rank] for items in lists if rank < len(items))
    return out


def _softmax_jobs(t, jobs):
    built = {}

    def tiles_of(job, g):
        if (job, g) not in built:
            built[job, g] = [make() for make in jobs[job][2][g]]
        return built[job, g]

    def scores_of(job, g, c):
        qts = jobs[job][1]
        row = []
        for tile in tiles_of(job, g):
            s = _dot(tile.kts[c], qts[c])
            if tile.biases is not None:
                s = s + tile.biases[c]
            if tile.emasks is not None:
                s = jnp.where(tile.emasks[c], s, NEG_INF)
            row.append(s.astype(_MXU))
        return row

    def update(state, job, g, c, scores):
        m, acc = state
        m_new = m
        for tile, s in zip(tiles_of(job, g), scores):
            tile_max = jnp.max(s, axis=0, keepdims=True).astype(jnp.float32)
            if tile.qmasks is not None:
                tile_max = jnp.where(tile.qmasks[c], tile_max, NEG_INF)
            m_new = jnp.maximum(m_new, tile_max)
        seen = m_new > 0.5 * NEG_INF
        acc = jnp.exp2(m - m_new) * acc
        for tile, s in zip(tiles_of(job, g), scores):
            ok = seen if tile.qmasks is None else jnp.logical_and(seen, tile.qmasks[c])
            acc = acc + _dot(tile.vts[c], jnp.exp2(s - jnp.where(ok, m_new, BIG).astype(_MXU)))
        return m_new, acc

    units = _round_robin([[(job, g, c) for g in range(len(groups)) for c in range(n)]
                          for job, (n, _, groups) in enumerate(jobs)])
    lookahead = max(LOOKAHEAD, len(jobs))
    states = [[_softmax_init(t) for _ in range(n)] for n, _, _ in jobs]
    pending = {k: scores_of(*units[k]) for k in range(min(lookahead, len(units)))}
    for k, (job, g, c) in enumerate(units):
        if k + lookahead < len(units):
            pending[k + lookahead] = scores_of(*units[k + lookahead])
        states[job][c] = update(states[job][c], job, g, c, pending.pop(k))
        yield
    return states


def _pairs(items):
    return [items[p:p + 2] for p in range(0, len(items), 2)]


class _Mixer(NamedTuple):
    emit: Callable[..., Any]
    in_specs: Sequence[Any]
    operands: Sequence[Any]
    scratch: Sequence[Any]


def _mixer_kernel(*refs, emit):
    for _ in emit(*refs):
        pass


def _run_mixer(mixer, bsz, s, name):
    return pl.pallas_call(
        functools.partial(_mixer_kernel, emit=mixer.emit),
        grid=(bsz,),
        in_specs=list(mixer.in_specs),
        out_specs=pl.BlockSpec((1, s, GROUP_WIDTH), lambda b: (b, 0, 0)),
        out_shape=jax.ShapeDtypeStruct((bsz, s, GROUP_WIDTH), _MXU),
        scratch_shapes=list(mixer.scratch),
        compiler_params=_params("parallel"),
        name=name,
    )(*mixer.operands)


def _softmax_out(state):
    acc = state[1]
    return acc[:HEAD_DIM] / jnp.maximum(acc[HEAD_DIM:HEAD_DIM + 1], TINY)


def _top_k_rows(score, row_f, k):
    sel = jnp.zeros(score.shape, jnp.float32)
    for _ in range(k):
        mx = jnp.max(score, axis=0, keepdims=True)
        idx = jnp.min(jnp.where(score == mx, row_f, float(score.shape[0])), axis=0, keepdims=True)
        pick = row_f == idx
        sel = jnp.where(pick, 1.0, sel)
        score = jnp.where(pick, PICKED, score)
    return sel


def _now(fn):
    fn()


def _tile_iotas(t):
    return (lax.broadcasted_iota(jnp.int32, (t, t), 0), lax.broadcasted_iota(jnp.int32, (t, t), 1))


def _key_rows(ref, j, t):
    if isinstance(j, int):
        return ref[j * t:(j + 1) * t, :]
    return ref[pl.ds(pl.multiple_of(j * t, t), t), :]


def _transposed(ref_block):
    return ref_block.astype(jnp.float32).T


def _queries_t(q_ref, width, t):
    qt = _transposed(q_ref[0]).astype(_MXU)
    return [[qt[c * width:(c + 1) * width, r * t:(r + 1) * t] for c in range(GROUP_WIDTH // width)]
            for r in range(Q_TILES)]


def _fill_values_t(vt_ref, v_ref, col0, t):
    n_tiles, rows, _ = vt_ref.shape
    lane_block = (col0 // LANES) * LANES
    for c in range(n_tiles):
        blk = _transposed(v_ref[0, c * t:(c + 1) * t, lane_block:lane_block + LANES])
        vt_ref[c, 0:HEAD_DIM, :] = blk[col0 - lane_block:col0 - lane_block + HEAD_DIM].astype(vt_ref.dtype)
        if rows == ACC_ROWS:
            first = lax.broadcasted_iota(jnp.int32, (rows - HEAD_DIM, t), 0) == 0
            vt_ref[c, HEAD_DIM:rows, :] = jnp.where(first, 1.0, 0.0).astype(vt_ref.dtype)


def _group_specs(s, *column_blocks):
    return [pl.BlockSpec((1, s, GROUP_WIDTH), functools.partial(lambda cb, b: (b, 0, cb), cb))
            for cb in column_blocks]


def _bias_specs(t, head_group):
    spec = pl.BlockSpec((GROUP_HEADS, t, t), lambda b: (head_group, 0, 0))
    return [spec, spec]


def _store_heads(o_ref, outs_t):
    tiles = [jnp.concatenate(heads, axis=0) for heads in outs_t]
    o_ref[0] = jnp.concatenate(tiles, axis=1).T.astype(o_ref.dtype)


def _sb_emit(q_ref, k_ref, v_ref, o_ref, kb_ref, vt_ref):
    t = q_ref.shape[1] // Q_TILES
    key, qry = _tile_iotas(t)
    strict = key < qry
    later = jnp.where(qry > key, 1.0, 0.0).astype(_MXU)

    @_now
    def _():
        for h in _HEADS:
            kb_ref[h] = k_ref[0, :, h * HEAD_DIM:(h + 1) * HEAD_DIM].astype(kb_ref.dtype)
            _fill_values_t(vt_ref.at[h], v_ref, h * HEAD_DIM, t)

    def query_step(step):
        qts = _queries_t(q_ref, HEAD_DIM, t)
        units = _round_robin([[(r, step * Q_TILES + r, pair, h)
                               for pair in _pairs(list(range(step * Q_TILES + r, -1, -1))) for h in _HEADS]
                              for r in range(Q_TILES)])
        zero = (jnp.zeros((HEAD_DIM, t), jnp.float32), jnp.zeros((1, t), jnp.float32))
        carry = [[zero] * GROUP_HEADS for _ in range(Q_TILES)]
        zs, log_keeps, suffixes = {}, {}, {}

        def scores(u):
            r, _, pair, h = units[u]
            zs[u] = [_dot(_key_rows(kb_ref.at[h], j, t), qts[r][h]) for j in pair]

        def keeps(u):
            _, k, pair, _ = units[u]
            log_keeps[u], suffixes[u] = [], []
            for j, z in zip(pair, zs[u]):
                drop = jnp.maximum(jnp.log2(1.0 + jnp.exp2(jnp.minimum(z, SOFTPLUS_CLAMP))), z)
                if j == k:
                    drop = jnp.where(strict, drop, 0.0)
                log_keeps[u].append(drop)
                suffixes[u].append(_dot(later, drop.astype(_MXU)))

        def values(u):
            r, k, pair, h = units[u]
            acc, run = carry[r][h]
            weights = []
            for j, z, drop, suffix in zip(pair, zs.pop(u), log_keeps.pop(u), suffixes.pop(u)):
                a = jnp.exp2(z - drop - suffix + run)
                if j == k:
                    a = jnp.where(strict, a, 0.0)
                weights.append(a.astype(_MXU))
                run = run - (suffix[0:1] + drop[0:1])
            for j, w in zip(pair, weights):
                acc = acc + _dot(vt_ref[h, j], w)
            carry[r][h] = (acc, run)

        stages = (scores, keeps, values)
        for tick in range(len(units) + len(stages) - 1):
            for lag, stage in enumerate(stages):
                if 0 <= tick - lag < len(units):
                    stage(tick - lag)
            yield
        _store_heads(o_ref, [[c[0] for c in tile_carry] for tile_carry in carry])

    yield from query_step(0)


def _stick_breaking(proj, t=TILE):
    s = proj.shape[1]
    scratch = [pltpu.VMEM((GROUP_HEADS, s, HEAD_DIM), _MXU), pltpu.VMEM((GROUP_HEADS, s // t, HEAD_DIM, t), _MXU)]
    return _Mixer(_sb_emit, _group_specs(s, CB_SB_Q, CB_SB_K, CB_SB_V), [proj] * 3, scratch)


def _moba_emit(q_ref, k_ref, v_ref, bd_ref, bs_ref, o_ref, kb_ref, vt_ref, km_ref):
    t = q_ref.shape[1] // Q_TILES
    n_blk = k_ref.shape[1] // MOBA_BLOCK
    tiles_per_blk = MOBA_BLOCK // t
    blk_shift = int(math.log2(tiles_per_blk))

    @_now
    def _():
        km_ref[...] = jnp.zeros_like(km_ref)
        for h in _HEADS:
            lo, hi = h * HEAD_DIM, (h + 1) * HEAD_DIM
            kb_ref[h] = k_ref[0, :, lo:hi].astype(kb_ref.dtype)
            _fill_values_t(vt_ref.at[h], v_ref, lo, t)
            for n in range(n_blk):
                blk = k_ref[0, n * MOBA_BLOCK:(n + 1) * MOBA_BLOCK, lo:hi]
                km_ref[h, n:n + 1, :] = jnp.mean(blk.astype(jnp.float32), axis=0, keepdims=True)

    def query_step(step):
        key, qry = _tile_iotas(t)
        causal = key <= qry
        blk_row = lax.broadcasted_iota(jnp.int32, (km_ref.shape[1], t), 0)
        qt = _transposed(q_ref[0])
        jobs = []
        for r in range(Q_TILES):
            k = step * Q_TILES + r
            own = k >> blk_shift
            qts, sels = [], []
            for h in _HEADS:
                qf = qt[h * HEAD_DIM:(h + 1) * HEAD_DIM, r * t:(r + 1) * t]
                qts.append(qf.astype(_MXU))
                gate = _dot(km_ref[h], qf, precision=lax.Precision.HIGHEST)
                gate = jnp.where(blk_row < own, gate, NEG_INF)
                sel = _top_k_rows(gate, blk_row.astype(jnp.float32), min(MOBA_TOPK, n_blk - 1))
                sels.append(jnp.where(blk_row < own, sel, 0.0))

            def tile(own, sels, j, bias_ref=None, emask=None):
                n = j >> blk_shift
                return _KeyTile([_key_rows(kb_ref.at[h], j, t) for h in _HEADS], [vt_ref[h, j] for h in _HEADS],
                                None if bias_ref is None else [bias_ref[h] for h in _HEADS],
                                None if emask is None else [emask] * GROUP_HEADS,
                                None if n == own else [sels[h][n:n + 1] > 0.5 for h in _HEADS])

            tile = functools.partial(tile, own, sels)
            groups = _pairs([functools.partial(tile, j) for j in range(k - 1)])
            groups.append(([functools.partial(tile, k - 1, bs_ref)] if k else [])
                          + [functools.partial(tile, k, bd_ref, causal)])
            jobs.append((GROUP_HEADS, qts, groups))
        all_states = yield from _softmax_jobs(t, jobs)
        _store_heads(o_ref, [[_softmax_out(st) for st in states] for states in all_states])

    yield from query_step(0)


def _kv_scratch(s, t, key_dim=HEAD_DIM, n_keys=GROUP_HEADS):
    return [pltpu.VMEM((n_keys, s, key_dim), _MXU), pltpu.VMEM((GROUP_HEADS, s // t, ACC_ROWS, t), _MXU)]


def _moba(proj, bias_diag, bias_sub, t=TILE):
    s = proj.shape[1]
    n_blk_pad = -(-(s // MOBA_BLOCK) // SUBLANES) * SUBLANES
    scratch = _kv_scratch(s, t) + [pltpu.VMEM((GROUP_HEADS, n_blk_pad, HEAD_DIM), jnp.float32)]
    return _Mixer(_moba_emit, _group_specs(s, CB_MB_Q, CB_MB_K, CB_MB_V) + _bias_specs(t, 0),
                  [proj] * 3 + [bias_diag, bias_sub], scratch)


def _diff_emit(lam_ref, g_ref, q_ref, k_ref, v_ref, bd_ref, bs_ref, o_ref, kb_ref, vt_ref, *, lambda_init):
    t = q_ref.shape[1] // Q_TILES
    key, qry = _tile_iotas(t)
    causal = key <= qry
    lv = lam_ref[...]
    lam = (jnp.exp(jnp.sum(lv[0:1] * lv[1:2], keepdims=True))
           - jnp.exp(jnp.sum(lv[2:3] * lv[3:4], keepdims=True)) + lambda_init)
    halves = range(2 * GROUP_HEADS)

    @_now
    def _():
        for c in halves:
            kb_ref[c] = k_ref[0, :, c * DIFF_HALF:(c + 1) * DIFF_HALF].astype(kb_ref.dtype)
        for h in _HEADS:
            _fill_values_t(vt_ref.at[h], v_ref, h * HEAD_DIM, t)

    def tile(j, bias_ref=None, emask=None):
        n = len(halves)
        return _KeyTile([_key_rows(kb_ref.at[c], j, t) for c in halves], [vt_ref[c // 2, j] for c in halves],
                        None if bias_ref is None else [bias_ref[c // 2] for c in halves],
                        None if emask is None else [emask] * n)

    def query_step(step):
        qts = _queries_t(q_ref, DIFF_HALF, t)
        jobs = []
        for r in range(Q_TILES):
            k = step * Q_TILES + r
            groups = _pairs([functools.partial(tile, j) for j in range(k - 1)])
            groups.append(([functools.partial(tile, k - 1, bs_ref)] if k else [])
                          + [functools.partial(tile, k, bd_ref, causal)])
            jobs.append((len(halves), qts[r], groups))
        all_states = yield from _softmax_jobs(t, jobs)
        outs = []
        for states in all_states:
            heads = []
            for h in _HEADS:
                o = _softmax_out(states[2 * h]) - lam * _softmax_out(states[2 * h + 1])
                o = o * lax.rsqrt(jnp.mean(o * o, axis=0, keepdims=True) + NORM_EPS) * g_ref[...]
                heads.append(o * (1.0 - lambda_init))
            outs.append(heads)
        _store_heads(o_ref, outs)

    yield from query_step(0)


def _diff(proj, lam_params, subln, bias_diag, bias_sub, lambda_init, t=TILE):
    s = proj.shape[1]
    in_specs = ([pl.BlockSpec(lam_params.shape, lambda b: (0, 0)), pl.BlockSpec(subln.shape, lambda b: (0, 0))]
                + _group_specs(s, CB_DF_Q, CB_DF_K, CB_DF_V) + _bias_specs(t, 2))
    return _Mixer(functools.partial(_diff_emit, lambda_init=lambda_init), in_specs,
                  [lam_params, subln] + [proj] * 3 + [bias_diag, bias_sub],
                  _kv_scratch(s, t, key_dim=DIFF_HALF, n_keys=2 * GROUP_HEADS))


def _compress_blocks(kcv_ref, pk_ref, pv_ref, wk1_ref, wk2_ref, wv1_ref, wv2t_ref):
    n_chunk = kcv_ref.shape[1] // CMP_STRIDE

    branches = ((0, pk_ref, wk1_ref), (HEAD_DIM, pv_ref, wv1_ref))
    tops = [jnp.zeros((n_chunk, w1_ref.shape[1]), jnp.float32) for _, _, w1_ref in branches]
    bots = list(tops)
    pack = 2 * LANES // HEAD_DIM
    for l0 in range(0, CMP_STRIDE, pack):
        tokens = [kcv_ref[0, pl.ds(l, n_chunk, stride=CMP_STRIDE), :] for l in range(l0, l0 + pack)]
        for n, (col0, p_ref, w1_ref) in enumerate(branches):
            for half, acc in ((0, tops), (CMP_STRIDE, bots)):
                x = jnp.concatenate([(tok[:, col0:col0 + HEAD_DIM] + p_ref[half + l:half + l + 1, :]).astype(_MXU)
                                     for l, tok in zip(range(l0, l0 + pack), tokens)], axis=1)
                acc[n] = acc[n] + _dot(x, w1_ref[(half + l0) * HEAD_DIM:(half + l0 + pack) * HEAD_DIM, :])
    hidden = [jax.nn.gelu(top + pltpu.roll(bot, n_chunk - 1, axis=0)).astype(_MXU) for top, bot in zip(tops, bots)]
    return _dot(hidden[0], wk2_ref[...]), _dot_nt(wv2t_ref[...], hidden[1])


def _nsa_emit(q_ref, kva_ref, kvb_ref, kcv_ref, pk_ref, pv_ref, wk1_ref, wk2_ref, wv1_ref, wv2t_ref,
              bd_ref, bs_ref, bc_ref, cover_ref, e_ref, o_ref, ks_ref, vst_ref, kw_ref, vwt_ref):
    t = q_ref.shape[1] // Q_TILES
    key, qry = _tile_iotas(t)
    causal = key <= qry
    ks_col, vs_col, kw_col, vw_col, gate_col = 2 * HEAD_DIM, 3 * HEAD_DIM, 0, HEAD_DIM, 2 * HEAD_DIM

    @_now
    def _():
        ks_ref[:, :HEAD_DIM] = kva_ref[0, :, ks_col:ks_col + HEAD_DIM].astype(ks_ref.dtype)
        ks_ref[:, HEAD_DIM:] = e_ref[...]
        kw_ref[...] = kvb_ref[0, :, kw_col:kw_col + HEAD_DIM].astype(kw_ref.dtype)
        _fill_values_t(vst_ref, kva_ref, vs_col, t)
        _fill_values_t(vwt_ref, kvb_ref, vw_col, t)

    def tile(k_ref, vt_ref, j, bias_ref=None, emask=None):
        n = GROUP_HEADS
        return _KeyTile([_key_rows(k_ref, j, t)] * n, [vt_ref[j]] * n,
                        None if bias_ref is None else [bias_ref[h] for h in _HEADS],
                        None if emask is None else [emask] * n)

    def query_step(step):
        qts = _queries_t(q_ref, HEAD_DIM, t)
        kc, vct = (a.astype(_MXU) for a in _compress_blocks(kcv_ref, pk_ref, pv_ref, wk1_ref, wk2_ref,
                                                            wv1_ref, wv2t_ref))
        n_cmp = kc.shape[0]
        n_slc = cover_ref.shape[0]
        c_row = lax.broadcasted_iota(jnp.int32, (n_cmp, t), 0)
        c_col = lax.broadcasted_iota(jnp.int32, (n_cmp, t), 1)
        s_row = lax.broadcasted_iota(jnp.int32, (n_slc, t), 0)
        s_col = lax.broadcasted_iota(jnp.int32, (n_slc, t), 1)
        tiles = [step * Q_TILES + r for r in range(Q_TILES)]

        o_cmp, importance = [], []
        for r, k in enumerate(tiles):
            visible = c_col + k * t >= c_row * CMP_STRIDE + (CMP_LEN - 1)
            cmp_scores = [_dot(kc, qts[r][h]) for h in _HEADS]
            cmp_probs = []
            p_sum = jnp.zeros((n_cmp, t), jnp.float32)
            for h in _HEADS:
                sc = jnp.where(visible, cmp_scores[h] + bc_ref[h, :, r * t:(r + 1) * t], NEG_INF)
                e = jnp.where(visible, jnp.exp2(sc - jnp.max(sc, axis=0, keepdims=True)), 0.0)
                p = e / jnp.maximum(jnp.sum(e, axis=0, keepdims=True), TINY)
                cmp_probs.append(p.astype(_MXU))
                p_sum = p_sum + p
            o_cmp.append([_dot(vct, cmp_probs[h]) for h in _HEADS])
            importance.append(_dot(cover_ref[...], p_sum, precision=lax.Precision.HIGHEST))

        n_back = WINDOW // t
        jobs = []
        for r, k in enumerate(tiles):
            window = [functools.partial(tile, kw_ref, vwt_ref, k - n_back, None, qry < key)] if k >= n_back else []
            for back in range(min(n_back - 1, k), 0, -1):
                window.append(functools.partial(tile, kw_ref, vwt_ref, k - back, bs_ref if back == 1 else None))
            window.append(functools.partial(tile, kw_ref, vwt_ref, k, bd_ref, causal))
            jobs.append((GROUP_HEADS, qts[r], [window]))
        win_states = yield from _softmax_jobs(t, jobs)
        o_win = [[_softmax_out(st) for st in states] for states in win_states]

        jobs = []
        for r, k in enumerate(tiles):
            own = jnp.right_shift(s_col + k * t, int(math.log2(SLC_LEN)))
            score = jnp.where(s_row == own, FORCE, jnp.where(s_row < own, importance[r], NEG_INF))
            sel = _top_k_rows(score, s_row.astype(jnp.float32), min(SLC_TOPN, n_slc))
            penalty = jnp.where(sel > 0.5, 0.0, NEG_INF).astype(_MXU)
            q_aug = [jnp.concatenate([qts[r][h], penalty], axis=0) for h in _HEADS]
            groups = _pairs([functools.partial(tile, ks_ref, vst_ref, j) for j in range(k - 1)])
            groups.append(([functools.partial(tile, ks_ref, vst_ref, k - 1, bs_ref)] if k else [])
                          + [functools.partial(tile, ks_ref, vst_ref, k, bd_ref, causal)])
            jobs.append((GROUP_HEADS, q_aug, groups))
        slc_states = yield from _softmax_jobs(t, jobs)
        o_slc = [[_softmax_out(st) for st in states] for states in slc_states]

        gates = _transposed(kvb_ref[0, :, (gate_col // LANES) * LANES:(gate_col // LANES + 1) * LANES])
        gates = 1.0 / (1.0 + jnp.exp(-gates[gate_col % LANES:gate_col % LANES + N_GATES + 4]))
        outs = []
        for r in range(Q_TILES):
            heads = []
            for h in _HEADS:
                g = [gates[br * GROUP_HEADS + h:br * GROUP_HEADS + h + 1, r * t:(r + 1) * t] for br in range(3)]
                heads.append(g[0] * o_cmp[r][h] + g[1] * o_slc[r][h] + g[2] * o_win[r][h])
            outs.append(heads)
        _store_heads(o_ref, outs)

    yield from query_step(0)


def _nsa(proj, kcv, cmp_params, layer, bias_diag, bias_sub, bias_cmp, cover_t, expand, t=TILE):
    s = proj.shape[1]
    values_t = pltpu.VMEM((s // t, ACC_ROWS, t), _MXU)
    in_specs = (_group_specs(s, CB_NS_Q, CB_NS_A, CB_NS_B)
                + [pl.BlockSpec((1, s, kcv.shape[2]), lambda b: (b, 0, 0))]
                + [pl.BlockSpec((None,) + a.shape[1:], lambda b: (layer, 0, 0)) for a in cmp_params]
                + _bias_specs(t, 1)
                + [pl.BlockSpec(bias_cmp.shape, lambda b: (0, 0, 0)),
                   pl.BlockSpec(cover_t.shape, lambda b: (0, 0)),
                   pl.BlockSpec(expand.shape, lambda b: (0, 0))])
    scratch = [pltpu.VMEM((s, HEAD_DIM + expand.shape[1]), _MXU), values_t, pltpu.VMEM((s, HEAD_DIM), _MXU), values_t]
    return _Mixer(_nsa_emit, in_specs,
                  [proj] * 3 + [kcv, *cmp_params, bias_diag, bias_sub, bias_cmp, cover_t, expand], scratch)


def _nsa_constants(s):
    n_cmp = (s - CMP_LEN) // CMP_STRIDE + 1
    n_slc = s // SLC_LEN
    assert n_cmp + 1 == s // CMP_STRIDE and n_slc % SUBLANES == 0
    c_start = np.arange(n_cmp) * CMP_STRIDE
    s_start = np.arange(n_slc) * SLC_LEN
    cover = np.clip(np.minimum((c_start + CMP_LEN - 1)[:, None], (s_start + SLC_LEN - 1)[None, :])
                    - np.maximum(c_start[:, None], s_start[None, :]) + 1, 0, None) / CMP_LEN
    cover_t = np.zeros((n_slc, n_cmp + 1), np.float32)
    cover_t[:, :n_cmp] = cover.T
    expand = (np.arange(s)[:, None] // SLC_LEN == np.arange(n_slc)[None, :]).astype(np.float32)
    return jnp.asarray(cover_t), jnp.asarray(expand, _MXU)


def kernel(x, w_in, w_out, w_up, w_down, norm_attn, norm_mlp, cmp_pos_k, cmp_pos_v, cmp_k_w1, cmp_k_w2,
           cmp_v_w1, cmp_v_w2, diff_lambda, diff_subln, rel_bias, final_norm):
    bsz, s, d = x.shape
    depth = w_in.shape[0]
    t = TILE
    n_chunk = s // CMP_STRIDE
    assert s == Q_TILES * t and MOBA_BLOCK % t == 0 and WINDOW % t == 0 and t >= MAX_DISTANCE

    assert w_in.shape[2] + PAD_COLS == D_IN_PAD
    w_in_c = jnp.pad(w_in.astype(_MXU), ((0, 0), (0, 0), (0, PAD_COLS)))
    w_out_c, w_up_c, w_down_c = (w.astype(_MXU) for w in (w_out, w_up, w_down))
    wk1, wk2 = cmp_k_w1.astype(_MXU), cmp_k_w2.astype(_MXU)
    wv1, wv2t = cmp_v_w1.astype(_MXU), jnp.swapaxes(cmp_v_w2, 1, 2).astype(_MXU)

    table_t = jnp.pad(rel_bias.T, ((0, 0), (0, LANES - N_BUCKETS)))
    tiles = dict(n_heads=rel_bias.shape[1], head0=0, rows=t, cols=t, col_tile=t, row_stride=-1, col_stride=1)
    bias_diag = _bias_table(table_t, offset=0, **tiles)
    bias_sub = _bias_table(table_t, offset=t, **tiles)
    bias_cmp = _bias_table(table_t, n_heads=GROUP_HEADS, head0=GROUP_HEADS, rows=n_chunk, cols=s,
                           col_tile=t, row_stride=-CMP_STRIDE, col_stride=1, offset=-(CMP_LEN - 1))
    cover_t, expand = _nsa_constants(s)
    col_scale = np.ones((1, D_IN_PAD), np.float32)
    for cb, width in ((CB_SB_Q, HEAD_DIM), (CB_MB_Q, HEAD_DIM), (CB_NS_Q, HEAD_DIM), (CB_DF_Q, DIFF_HALF)):
        col_scale[:, cb * GROUP_WIDTH:(cb + 1) * GROUP_WIDTH] = width ** -0.5 * LOG2E
    col_scale = jnp.asarray(col_scale)

    x2 = x.reshape(bsz * s, d)
    for layer in range(depth):
        proj, kcv = _norm_matmul(x2, norm_attn[layer][None], w_in_c, layer, col_scale,
                                 gap=(COLS_BEFORE_PAD, COLS_BEFORE_PAD + PAD_COLS),
                                 side_col=CB_NS_A * GROUP_WIDTH, side_width=2 * HEAD_DIM)
        proj = proj.reshape(bsz, s, D_IN_PAD)
        lambda_init = 0.8 - 0.6 * math.exp(-0.3 * layer)
        o_sb = _run_mixer(_stick_breaking(proj), bsz, s, "stick_breaking")
        o_mb = _run_mixer(_moba(proj, bias_diag, bias_sub), bsz, s, "moba")
        o_ns = _run_mixer(_nsa(proj, kcv.reshape(bsz, s, 2 * HEAD_DIM), (cmp_pos_k, cmp_pos_v, wk1, wk2, wv1, wv2t),
                               layer, bias_diag, bias_sub, bias_cmp, cover_t, expand), bsz, s, "nsa")
        o_df = _run_mixer(_diff(proj, diff_lambda[layer], diff_subln[layer][:, None], bias_diag, bias_sub,
                                lambda_init), bsz, s, "diff_attention")
        groups = [o.reshape(bsz * s, GROUP_WIDTH) for o in (o_sb, o_mb, o_ns, o_df)]
        x2 = _out_mlp(x2, groups, w_out_c, norm_mlp[layer][None], w_up_c, w_down_c, layer,
                      final_norm[None], final_norm=(layer == depth - 1))
    return x2.reshape(bsz, s, d)
```

```python
import functools
import math
from typing import Any, Callable, NamedTuple, Optional, Sequence

import numpy as np
import jax
import jax.numpy as jnp
from jax import lax
from jax.experimental import pallas as pl
from jax.experimental.pallas import tpu as pltpu

HEAD_DIM = 64
GROUP_HEADS = 4
GROUP_WIDTH = GROUP_HEADS * HEAD_DIM
NORM_EPS = 1e-6
NEG_INF = -1e30
BIG = 1e30
FORCE = 1e30
TINY = 1e-30
SOFTPLUS_CLAMP = 64.0
PICKED = -3e38
LOG2E = math.log2(math.e)
N_BUCKETS = 32
MAX_DISTANCE = 128
MOBA_BLOCK = 256
MOBA_TOPK = 3
CMP_LEN = 32
CMP_STRIDE = 16
SLC_LEN = 64
SLC_TOPN = 4
WINDOW = 512
DIFF_HALF = HEAD_DIM // 2
LANES = 128
SUBLANES = 8
TILE = 256
BF16_ROWS = 16
ACC_ROWS = HEAD_DIM + BF16_ROWS
N_GATES = 3 * GROUP_HEADS
COLS_BEFORE_PAD = 9 * GROUP_WIDTH - 2 * HEAD_DIM + N_GATES
PAD_COLS = 2 * HEAD_DIM - N_GATES
CB_SB_Q, CB_SB_K, CB_SB_V, CB_MB_Q, CB_MB_K, CB_MB_V, CB_NS_Q, CB_NS_A, CB_NS_B, CB_DF_Q, CB_DF_K, CB_DF_V = range(12)
D_IN_PAD = 12 * GROUP_WIDTH

_MXU = jnp.bfloat16
_VMEM_LIMIT = 56 * 1024 * 1024
_HEADS = range(GROUP_HEADS)
Q_TILES = 8
LOOKAHEAD = 2


def _dot(a, b, precision=None):
    return jnp.dot(a, b, precision=precision, preferred_element_type=jnp.float32)


def _dot_nt(a, b, precision=None):
    return lax.dot_general(a, b, (((1,), (1,)), ((), ())), precision=precision,
                           preferred_element_type=jnp.float32)


def _rms(x, g):
    return x * lax.rsqrt(jnp.mean(x * x, axis=-1, keepdims=True) + NORM_EPS) * g


def _params(*sem):
    return pltpu.CompilerParams(dimension_semantics=sem, vmem_limit_bytes=_VMEM_LIMIT)


def _norm_matmul_kernel(x_ref, g_ref, w_ref, scale_ref, o_ref, side_ref, wp_ref, *, tn, side_col, gap):
    @pl.when(pl.program_id(0) == 0)
    def _():
        lo, hi = gap
        wp_ref[:, :lo] = w_ref[:, :lo]
        wp_ref[:, lo:hi] = jnp.zeros((wp_ref.shape[0], hi - lo), wp_ref.dtype)
        wp_ref[:, hi:] = w_ref[:, lo:lo + wp_ref.shape[1] - hi]

    h = _rms(x_ref[...], g_ref[...]).astype(_MXU)
    for j in range(wp_ref.shape[1] // tn):
        cols = slice(j * tn, (j + 1) * tn)
        acc = _dot(h, wp_ref[:, cols])
        o_ref[:, cols] = (acc * scale_ref[:, cols]).astype(o_ref.dtype)
        if j * tn <= side_col < (j + 1) * tn:
            side_ref[...] = acc[:, side_col - j * tn:side_col - j * tn + side_ref.shape[1]]


def _norm_matmul(x, g, w, layer, col_scale, *, gap, side_col, side_width, tm=1024, tn=1024):
    m, d = x.shape
    n = w.shape[2]
    assert side_col % LANES == 0 and side_col // tn == (side_col + side_width - 1) // tn
    return pl.pallas_call(
        functools.partial(_norm_matmul_kernel, tn=tn, side_col=side_col, gap=gap),
        grid=(m // tm,),
        in_specs=[pl.BlockSpec((tm, d), lambda i: (i, 0)),
                  pl.BlockSpec((1, d), lambda i: (0, 0)),
                  pl.BlockSpec((None, d, w.shape[2]), lambda i: (layer, 0, 0), pipeline_mode=pl.Buffered(1)),
                  pl.BlockSpec((1, n), lambda i: (0, 0))],
        out_specs=[pl.BlockSpec((tm, n), lambda i: (i, 0)),
                   pl.BlockSpec((tm, side_width), lambda i: (i, 0))],
        out_shape=[jax.ShapeDtypeStruct((m, n), _MXU),
                   jax.ShapeDtypeStruct((m, side_width), jnp.float32)],
        scratch_shapes=[pltpu.VMEM((d, n), _MXU)],
        compiler_params=_params("arbitrary"),
        name="norm_in_proj",
    )(x, g, w, col_scale)


def _out_mlp_kernel(x_ref, a_ref, b_ref, c_ref, d_ref, wo_ref, g_ref, wu_ref, wd_ref, gf_ref, o_ref,
                    *, final_norm, tf):
    mixed = jnp.concatenate([a_ref[...], b_ref[...], c_ref[...], d_ref[...]], axis=1)
    y = x_ref[...] + _dot(mixed, wo_ref[...])
    h = _rms(y, g_ref[...]).astype(_MXU)
    for c in range(wu_ref.shape[1] // tf):
        u = jnp.square(jnp.maximum(_dot(h, wu_ref[:, c * tf:(c + 1) * tf]), 0.0))
        y = y + _dot(u.astype(_MXU), wd_ref[c * tf:(c + 1) * tf, :])
    if final_norm:
        y = _rms(y, gf_ref[...])
    o_ref[...] = y


def _out_mlp(x, groups, w_out, g, w_up, w_down, layer, g_final, *, final_norm, tm=1024, tf=1024):
    m, d = x.shape
    f = w_up.shape[2]
    gspec = pl.BlockSpec((tm, GROUP_WIDTH), lambda i: (i, 0))
    row = pl.BlockSpec((1, d), lambda i: (0, 0))

    def resident(rows, cols):
        return pl.BlockSpec((None, rows, cols), lambda i: (layer, 0, 0), pipeline_mode=pl.Buffered(1))

    return pl.pallas_call(
        functools.partial(_out_mlp_kernel, final_norm=final_norm, tf=tf),
        grid=(m // tm,),
        in_specs=[pl.BlockSpec((tm, d), lambda i: (i, 0)), gspec, gspec, gspec, gspec,
                  resident(d, d), row, resident(d, f), resident(f, d), row],
        out_specs=pl.BlockSpec((tm, d), lambda i: (i, 0)),
        out_shape=jax.ShapeDtypeStruct((m, d), jnp.float32),
        compiler_params=_params("parallel"),
        name="out_proj_mlp_residual",
    )(x, *groups, w_out, g, w_up, w_down, g_final)


def _t5_bucket(dist):
    n = jnp.maximum(dist, 0)
    max_exact = N_BUCKETS // 2
    nf = jnp.maximum(n, 1).astype(jnp.float32)
    large = max_exact + (jnp.log(nf / max_exact) / math.log(MAX_DISTANCE / max_exact)
                         * (N_BUCKETS - max_exact)).astype(jnp.int32)
    large = jnp.minimum(large, N_BUCKETS - 1)
    return jnp.where(n < max_exact, n, large)


def _bias_kernel(tab_ref, o_ref, *, row_stride, col_stride, offset, head0):
    nh, tr, tc = o_ref.shape
    for blk in range(tc // LANES):
        rows = lax.broadcasted_iota(jnp.int32, (tr, LANES), 0)
        cols = lax.broadcasted_iota(jnp.int32, (tr, LANES), 1) + (pl.program_id(0) * tc + blk * LANES)
        bucket = _t5_bucket(rows * row_stride + cols * col_stride + offset)
        for h in range(nh):
            row = tab_ref[head0 + h:head0 + h + 1, :]
            row = (row - row[:, N_BUCKETS - 1:N_BUCKETS]) * LOG2E
            o_ref[h, :, blk * LANES:(blk + 1) * LANES] = jnp.take_along_axis(
                jnp.broadcast_to(row, (tr, LANES)), bucket, axis=1, mode="promise_in_bounds")


def _bias_table(table_t, *, n_heads, head0, rows, cols, col_tile, row_stride, col_stride, offset):
    return pl.pallas_call(
        functools.partial(_bias_kernel, row_stride=row_stride, col_stride=col_stride,
                          offset=offset, head0=head0),
        grid=(cols // col_tile,),
        in_specs=[pl.BlockSpec(table_t.shape, lambda i: (0, 0))],
        out_specs=pl.BlockSpec((n_heads, rows, col_tile), lambda i: (0, 0, i)),
        out_shape=jax.ShapeDtypeStruct((n_heads, rows, cols), jnp.float32),
        compiler_params=_params("parallel"),
        name="t5_bias_tiles",
    )(table_t)


def _softmax_init(t):
    return (jnp.full((1, t), NEG_INF, jnp.float32), jnp.zeros((ACC_ROWS, t), jnp.float32))


class _KeyTile(NamedTuple):
    kts: Sequence[Any]
    vts: Sequence[Any]
    biases: Optional[Sequence[Any]] = None
    emasks: Optional[Sequence[Any]] = None
    qmasks: Optional[Sequence[Any]] = None


def _round_robin(lists):
    out = []
    for rank in range(max(map(len, lists), default=0)):
        out.extend(items[rank] for items in lists if rank < len(items))
    return out


def _softmax_jobs(t, jobs):
    built = {}

    def tiles_of(job, g):
        if (job, g) not in built:
            built[job, g] = [make() for make in jobs[job][2][g]]
        return built[job, g]

    def scores_of(job, g, c):
        qts = jobs[job][1]
        row = []
        for tile in tiles_of(job, g):
            s = _dot(tile.kts[c], qts[c])
            if tile.biases is not None:
                s = s + tile.biases[c]
            if tile.emasks is not None:
                s = jnp.where(tile.emasks[c], s, NEG_INF)
            row.append(s.astype(_MXU))
        return row

    def update(state, job, g, c, scores):
        m, acc = state
        m_new = m
        for tile, s in zip(tiles_of(job, g), scores):
            tile_max = jnp.max(s, axis=0, keepdims=True).astype(jnp.float32)
            if tile.qmasks is not None:
                tile_max = jnp.where(tile.qmasks[c], tile_max, NEG_INF)
            m_new = jnp.maximum(m_new, tile_max)
        seen = m_new > 0.5 * NEG_INF
        acc = jnp.exp2(m - m_new) * acc
        for tile, s in zip(tiles_of(job, g), scores):
            ok = seen if tile.qmasks is None else jnp.logical_and(seen, tile.qmasks[c])
            acc = acc + _dot(tile.vts[c], jnp.exp2(s - jnp.where(ok, m_new, BIG).astype(_MXU)))
        return m_new, acc

    units = _round_robin([[(job, g, c) for g in range(len(groups)) for c in range(n)]
                          for job, (n, _, groups) in enumerate(jobs)])
    lookahead = max(LOOKAHEAD, len(jobs) // 2)
    states = [[_softmax_init(t) for _ in range(n)] for n, _, _ in jobs]
    pending = {k: scores_of(*units[k]) for k in range(min(lookahead, len(units)))}
    for k, (job, g, c) in enumerate(units):
        if k + lookahead < len(units):
            pending[k + lookahead] = scores_of(*units[k + lookahead])
        states[job][c] = update(states[job][c], job, g, c, pending.pop(k))
        yield
    return states


def _pairs(items):
    return [items[p:p + 2] for p in range(0, len(items), 2)]


class _Mixer(NamedTuple):
    emit: Callable[..., Any]
    in_specs: Sequence[Any]
    operands: Sequence[Any]
    scratch: Sequence[Any]


def _mixer_kernel(*refs, emit):
    for _ in emit(*refs):
        pass


def _run_mixer(mixer, bsz, s, name):
    return pl.pallas_call(
        functools.partial(_mixer_kernel, emit=mixer.emit),
        grid=(bsz,),
        in_specs=list(mixer.in_specs),
        out_specs=pl.BlockSpec((1, s, GROUP_WIDTH), lambda b: (b, 0, 0)),
        out_shape=jax.ShapeDtypeStruct((bsz, s, GROUP_WIDTH), _MXU),
        scratch_shapes=list(mixer.scratch),
        compiler_params=_params("parallel"),
        name=name,
    )(*mixer.operands)


def _softmax_out(state):
    acc = state[1]
    return acc[:HEAD_DIM] / jnp.maximum(acc[HEAD_DIM:HEAD_DIM + 1], TINY)


def _top_k_rows(score, row_f, k):
    sel = jnp.zeros(score.shape, jnp.float32)
    for _ in range(k):
        mx = jnp.max(score, axis=0, keepdims=True)
        idx = jnp.min(jnp.where(score == mx, row_f, float(score.shape[0])), axis=0, keepdims=True)
        pick = row_f == idx
        sel = jnp.where(pick, 1.0, sel)
        score = jnp.where(pick, PICKED, score)
    return sel


def _now(fn):
    fn()


def _tile_iotas(t):
    return (lax.broadcasted_iota(jnp.int32, (t, t), 0), lax.broadcasted_iota(jnp.int32, (t, t), 1))


def _key_rows(ref, j, t):
    if isinstance(j, int):
        return ref[j * t:(j + 1) * t, :]
    return ref[pl.ds(pl.multiple_of(j * t, t), t), :]


def _transposed(ref_block):
    return ref_block.astype(jnp.float32).T


def _queries_t(q_ref, width, t):
    qt = _transposed(q_ref[0]).astype(_MXU)
    return [[qt[c * width:(c + 1) * width, r * t:(r + 1) * t] for c in range(GROUP_WIDTH // width)]
            for r in range(Q_TILES)]


def _fill_values_t(vt_ref, v_ref, col0, t):
    n_tiles, rows, _ = vt_ref.shape
    lane_block = (col0 // LANES) * LANES
    for c in range(n_tiles):
        blk = _transposed(v_ref[0, c * t:(c + 1) * t, lane_block:lane_block + LANES])
        vt_ref[c, 0:HEAD_DIM, :] = blk[col0 - lane_block:col0 - lane_block + HEAD_DIM].astype(vt_ref.dtype)
        if rows == ACC_ROWS:
            first = lax.broadcasted_iota(jnp.int32, (rows - HEAD_DIM, t), 0) == 0
            vt_ref[c, HEAD_DIM:rows, :] = jnp.where(first, 1.0, 0.0).astype(vt_ref.dtype)


def _group_specs(s, *column_blocks):
    return [pl.BlockSpec((1, s, GROUP_WIDTH), functools.partial(lambda cb, b: (b, 0, cb), cb))
            for cb in column_blocks]


def _bias_specs(t, head_group):
    spec = pl.BlockSpec((GROUP_HEADS, t, t), lambda b: (head_group, 0, 0))
    return [spec, spec]


def _store_heads(o_ref, outs_t):
    tiles = [jnp.concatenate(heads, axis=0) for heads in outs_t]
    o_ref[0] = jnp.concatenate(tiles, axis=1).T.astype(o_ref.dtype)


def _sb_emit(q_ref, k_ref, v_ref, o_ref, kb_ref, vt_ref):
    t = q_ref.shape[1] // Q_TILES
    key, qry = _tile_iotas(t)
    strict = key < qry
    later = jnp.where(qry > key, 1.0, 0.0).astype(_MXU)

    @_now
    def _():
        for h in _HEADS:
            kb_ref[h] = k_ref[0, :, h * HEAD_DIM:(h + 1) * HEAD_DIM].astype(kb_ref.dtype)
            _fill_values_t(vt_ref.at[h], v_ref, h * HEAD_DIM, t)

    def query_step(step):
        qts = _queries_t(q_ref, HEAD_DIM, t)
        units = _round_robin([[(r, step * Q_TILES + r, pair, h)
                               for pair in _pairs(list(range(step * Q_TILES + r, -1, -1))) for h in _HEADS]
                              for r in range(Q_TILES)])
        zero = (jnp.zeros((HEAD_DIM, t), jnp.float32), jnp.zeros((1, t), jnp.float32))
        carry = [[zero] * GROUP_HEADS for _ in range(Q_TILES)]
        zs, log_keeps, suffixes = {}, {}, {}

        def scores(u):
            r, _, pair, h = units[u]
            zs[u] = [_dot(_key_rows(kb_ref.at[h], j, t), qts[r][h]) for j in pair]

        def keeps(u):
            _, k, pair, _ = units[u]
            log_keeps[u], suffixes[u] = [], []
            for j, z in zip(pair, zs[u]):
                drop = jnp.maximum(jnp.log2(1.0 + jnp.exp2(jnp.minimum(z, SOFTPLUS_CLAMP))), z)
                if j == k:
                    drop = jnp.where(strict, drop, 0.0)
                log_keeps[u].append(drop)
                suffixes[u].append(_dot(later, drop.astype(_MXU)))

        def values(u):
            r, k, pair, h = units[u]
            acc, run = carry[r][h]
            weights = []
            for j, z, drop, suffix in zip(pair, zs.pop(u), log_keeps.pop(u), suffixes.pop(u)):
                a = jnp.exp2(z - drop - suffix + run)
                if j == k:
                    a = jnp.where(strict, a, 0.0)
                weights.append(a.astype(_MXU))
                run = run - (suffix[0:1] + drop[0:1])
            for j, w in zip(pair, weights):
                acc = acc + _dot(vt_ref[h, j], w)
            carry[r][h] = (acc, run)

        stages = (scores, keeps, values)
        for tick in range(len(units) + len(stages) - 1):
            for lag, stage in enumerate(stages):
                if 0 <= tick - lag < len(units):
                    stage(tick - lag)
            yield
        _store_heads(o_ref, [[c[0] for c in tile_carry] for tile_carry in carry])

    yield from query_step(0)


def _stick_breaking(proj, t=TILE):
    s = proj.shape[1]
    scratch = [pltpu.VMEM((GROUP_HEADS, s, HEAD_DIM), _MXU), pltpu.VMEM((GROUP_HEADS, s // t, HEAD_DIM, t), _MXU)]
    return _Mixer(_sb_emit, _group_specs(s, CB_SB_Q, CB_SB_K, CB_SB_V), [proj] * 3, scratch)


def _moba_emit(q_ref, k_ref, v_ref, bd_ref, bs_ref, o_ref, kb_ref, vt_ref, km_ref):
    t = q_ref.shape[1] // Q_TILES
    n_blk = k_ref.shape[1] // MOBA_BLOCK
    tiles_per_blk = MOBA_BLOCK // t
    blk_shift = int(math.log2(tiles_per_blk))

    @_now
    def _():
        km_ref[...] = jnp.zeros_like(km_ref)
        for h in _HEADS:
            lo, hi = h * HEAD_DIM, (h + 1) * HEAD_DIM
            kb_ref[h] = k_ref[0, :, lo:hi].astype(kb_ref.dtype)
            _fill_values_t(vt_ref.at[h], v_ref, lo, t)
            for n in range(n_blk):
                blk = k_ref[0, n * MOBA_BLOCK:(n + 1) * MOBA_BLOCK, lo:hi]
                km_ref[h, n:n + 1, :] = jnp.mean(blk.astype(jnp.float32), axis=0, keepdims=True)

    def query_step(step):
        key, qry = _tile_iotas(t)
        causal = key <= qry
        blk_row = lax.broadcasted_iota(jnp.int32, (km_ref.shape[1], t), 0)
        qt = _transposed(q_ref[0])
        jobs = []
        for r in range(Q_TILES):
            k = step * Q_TILES + r
            own = k >> blk_shift
            qts, sels = [], []
            for h in _HEADS:
                qf = qt[h * HEAD_DIM:(h + 1) * HEAD_DIM, r * t:(r + 1) * t]
                qts.append(qf.astype(_MXU))
                gate = _dot(km_ref[h], qf, precision=lax.Precision.HIGHEST)
                gate = jnp.where(blk_row < own, gate, NEG_INF)
                sel = _top_k_rows(gate, blk_row.astype(jnp.float32), min(MOBA_TOPK, n_blk - 1))
                sels.append(jnp.where(blk_row < own, sel, 0.0))

            def tile(own, sels, j, bias_ref=None, emask=None):
                n = j >> blk_shift
                return _KeyTile([_key_rows(kb_ref.at[h], j, t) for h in _HEADS], [vt_ref[h, j] for h in _HEADS],
                                None if bias_ref is None else [bias_ref[h] for h in _HEADS],
                                None if emask is None else [emask] * GROUP_HEADS,
                                None if n == own else [sels[h][n:n + 1] > 0.5 for h in _HEADS])

            tile = functools.partial(tile, own, sels)
            groups = _pairs([functools.partial(tile, j) for j in range(k - 1)])
            groups.append(([functools.partial(tile, k - 1, bs_ref)] if k else [])
                          + [functools.partial(tile, k, bd_ref, causal)])
            jobs.append((GROUP_HEADS, qts, groups))
        all_states = yield from _softmax_jobs(t, jobs)
        _store_heads(o_ref, [[_softmax_out(st) for st in states] for states in all_states])

    yield from query_step(0)


def _kv_scratch(s, t, key_dim=HEAD_DIM, n_keys=GROUP_HEADS):
    return [pltpu.VMEM((n_keys, s, key_dim), _MXU), pltpu.VMEM((GROUP_HEADS, s // t, ACC_ROWS, t), _MXU)]


def _moba(proj, bias_diag, bias_sub, t=TILE):
    s = proj.shape[1]
    n_blk_pad = -(-(s // MOBA_BLOCK) // SUBLANES) * SUBLANES
    scratch = _kv_scratch(s, t) + [pltpu.VMEM((GROUP_HEADS, n_blk_pad, HEAD_DIM), jnp.float32)]
    return _Mixer(_moba_emit, _group_specs(s, CB_MB_Q, CB_MB_K, CB_MB_V) + _bias_specs(t, 0),
                  [proj] * 3 + [bias_diag, bias_sub], scratch)


def _diff_emit(lam_ref, g_ref, q_ref, k_ref, v_ref, bd_ref, bs_ref, o_ref, kb_ref, vt_ref, *, lambda_init):
    t = q_ref.shape[1] // Q_TILES
    key, qry = _tile_iotas(t)
    causal = key <= qry
    lv = lam_ref[...]
    lam = (jnp.exp(jnp.sum(lv[0:1] * lv[1:2], keepdims=True))
           - jnp.exp(jnp.sum(lv[2:3] * lv[3:4], keepdims=True)) + lambda_init)
    halves = range(2 * GROUP_HEADS)

    @_now
    def _():
        for c in halves:
            kb_ref[c] = k_ref[0, :, c * DIFF_HALF:(c + 1) * DIFF_HALF].astype(kb_ref.dtype)
        for h in _HEADS:
            _fill_values_t(vt_ref.at[h], v_ref, h * HEAD_DIM, t)

    def tile(j, bias_ref=None, emask=None):
        n = len(halves)
        return _KeyTile([_key_rows(kb_ref.at[c], j, t) for c in halves], [vt_ref[c // 2, j] for c in halves],
                        None if bias_ref is None else [bias_ref[c // 2] for c in halves],
                        None if emask is None else [emask] * n)

    def query_step(step):
        qts = _queries_t(q_ref, DIFF_HALF, t)
        jobs = []
        for r in range(Q_TILES):
            k = step * Q_TILES + r
            groups = _pairs([functools.partial(tile, j) for j in range(k - 1)])
            groups.append(([functools.partial(tile, k - 1, bs_ref)] if k else [])
                          + [functools.partial(tile, k, bd_ref, causal)])
            jobs.append((len(halves), qts[r], groups))
        all_states = yield from _softmax_jobs(t, jobs)
        outs = []
        for states in all_states:
            heads = []
            for h in _HEADS:
                o = _softmax_out(states[2 * h]) - lam * _softmax_out(states[2 * h + 1])
                o = o * lax.rsqrt(jnp.mean(o * o, axis=0, keepdims=True) + NORM_EPS) * g_ref[...]
                heads.append(o * (1.0 - lambda_init))
            outs.append(heads)
        _store_heads(o_ref, outs)

    yield from query_step(0)


def _diff(proj, lam_params, subln, bias_diag, bias_sub, lambda_init, t=TILE):
    s = proj.shape[1]
    in_specs = ([pl.BlockSpec(lam_params.shape, lambda b: (0, 0)), pl.BlockSpec(subln.shape, lambda b: (0, 0))]
                + _group_specs(s, CB_DF_Q, CB_DF_K, CB_DF_V) + _bias_specs(t, 2))
    return _Mixer(functools.partial(_diff_emit, lambda_init=lambda_init), in_specs,
                  [lam_params, subln] + [proj] * 3 + [bias_diag, bias_sub],
                  _kv_scratch(s, t, key_dim=DIFF_HALF, n_keys=2 * GROUP_HEADS))


def _compress_blocks(kcv_ref, pk_ref, pv_ref, wk1_ref, wk2_ref, wv1_ref, wv2t_ref):
    n_chunk = kcv_ref.shape[1] // CMP_STRIDE

    branches = ((0, pk_ref, wk1_ref), (HEAD_DIM, pv_ref, wv1_ref))
    tops = [jnp.zeros((n_chunk, w1_ref.shape[1]), jnp.float32) for _, _, w1_ref in branches]
    bots = list(tops)
    pack = 2 * LANES // HEAD_DIM
    for l0 in range(0, CMP_STRIDE, pack):
        tokens = [kcv_ref[0, pl.ds(l, n_chunk, stride=CMP_STRIDE), :] for l in range(l0, l0 + pack)]
        for n, (col0, p_ref, w1_ref) in enumerate(branches):
            for half, acc in ((0, tops), (CMP_STRIDE, bots)):
                x = jnp.concatenate([(tok[:, col0:col0 + HEAD_DIM] + p_ref[half + l:half + l + 1, :]).astype(_MXU)
                                     for l, tok in zip(range(l0, l0 + pack), tokens)], axis=1)
                acc[n] = acc[n] + _dot(x, w1_ref[(half + l0) * HEAD_DIM:(half + l0 + pack) * HEAD_DIM, :])
    hidden = [jax.nn.gelu(top + pltpu.roll(bot, n_chunk - 1, axis=0)).astype(_MXU) for top, bot in zip(tops, bots)]
    return _dot(hidden[0], wk2_ref[...]), _dot_nt(wv2t_ref[...], hidden[1])


def _nsa_emit(q_ref, kva_ref, kvb_ref, kcv_ref, pk_ref, pv_ref, wk1_ref, wk2_ref, wv1_ref, wv2t_ref,
              bd_ref, bs_ref, bc_ref, cover_ref, e_ref, o_ref, ks_ref, vst_ref, kw_ref, vwt_ref):
    t = q_ref.shape[1] // Q_TILES
    key, qry = _tile_iotas(t)
    causal = key <= qry
    ks_col, vs_col, kw_col, vw_col, gate_col = 2 * HEAD_DIM, 3 * HEAD_DIM, 0, HEAD_DIM, 2 * HEAD_DIM

    @_now
    def _():
        ks_ref[:, :HEAD_DIM] = kva_ref[0, :, ks_col:ks_col + HEAD_DIM].astype(ks_ref.dtype)
        ks_ref[:, HEAD_DIM:] = e_ref[...]
        kw_ref[...] = kvb_ref[0, :, kw_col:kw_col + HEAD_DIM].astype(kw_ref.dtype)
        _fill_values_t(vst_ref, kva_ref, vs_col, t)
        _fill_values_t(vwt_ref, kvb_ref, vw_col, t)

    def tile(k_ref, vt_ref, j, bias_ref=None, emask=None):
        n = GROUP_HEADS
        return _KeyTile([_key_rows(k_ref, j, t)] * n, [vt_ref[j]] * n,
                        None if bias_ref is None else [bias_ref[h] for h in _HEADS],
                        None if emask is None else [emask] * n)

    def query_step(step):
        qts = _queries_t(q_ref, HEAD_DIM, t)
        kc, vct = (a.astype(_MXU) for a in _compress_blocks(kcv_ref, pk_ref, pv_ref, wk1_ref, wk2_ref,
                                                            wv1_ref, wv2t_ref))
        n_cmp = kc.shape[0]
        n_slc = cover_ref.shape[0]
        c_row = lax.broadcasted_iota(jnp.int32, (n_cmp, t), 0)
        c_col = lax.broadcasted_iota(jnp.int32, (n_cmp, t), 1)
        s_row = lax.broadcasted_iota(jnp.int32, (n_slc, t), 0)
        s_col = lax.broadcasted_iota(jnp.int32, (n_slc, t), 1)
        tiles = [step * Q_TILES + r for r in range(Q_TILES)]

        o_cmp, importance = [], []
        for r, k in enumerate(tiles):
            visible = c_col + k * t >= c_row * CMP_STRIDE + (CMP_LEN - 1)
            cmp_scores = [_dot(kc, qts[r][h]) for h in _HEADS]
            cmp_probs = []
            p_sum = jnp.zeros((n_cmp, t), jnp.float32)
            for h in _HEADS:
                sc = jnp.where(visible, cmp_scores[h] + bc_ref[h, :, r * t:(r + 1) * t], NEG_INF)
                e = jnp.where(visible, jnp.exp2(sc - jnp.max(sc, axis=0, keepdims=True)), 0.0)
                p = e / jnp.maximum(jnp.sum(e, axis=0, keepdims=True), TINY)
                cmp_probs.append(p.astype(_MXU))
                p_sum = p_sum + p
            o_cmp.append([_dot(vct, cmp_probs[h]) for h in _HEADS])
            importance.append(_dot(cover_ref[...], p_sum, precision=lax.Precision.HIGHEST))

        n_back = WINDOW // t
        jobs = []
        for r, k in enumerate(tiles):
            window = [functools.partial(tile, kw_ref, vwt_ref, k - n_back, None, qry < key)] if k >= n_back else []
            for back in range(min(n_back - 1, k), 0, -1):
                window.append(functools.partial(tile, kw_ref, vwt_ref, k - back, bs_ref if back == 1 else None))
            window.append(functools.partial(tile, kw_ref, vwt_ref, k, bd_ref, causal))
            jobs.append((GROUP_HEADS, qts[r], [window]))
        win_states = yield from _softmax_jobs(t, jobs)
        o_win = [[_softmax_out(st) for st in states] for states in win_states]

        jobs = []
        for r, k in enumerate(tiles):
            own = jnp.right_shift(s_col + k * t, int(math.log2(SLC_LEN)))
            score = jnp.where(s_row == own, FORCE, jnp.where(s_row < own, importance[r], NEG_INF))
            sel = _top_k_rows(score, s_row.astype(jnp.float32), min(SLC_TOPN, n_slc))
            penalty = jnp.where(sel > 0.5, 0.0, NEG_INF).astype(_MXU)
            q_aug = [jnp.concatenate([qts[r][h], penalty], axis=0) for h in _HEADS]
            groups = _pairs([functools.partial(tile, ks_ref, vst_ref, j) for j in range(k - 1)])
            groups.append(([functools.partial(tile, ks_ref, vst_ref, k - 1, bs_ref)] if k else [])
                          + [functools.partial(tile, ks_ref, vst_ref, k, bd_ref, causal)])
            jobs.append((GROUP_HEADS, q_aug, groups))
        slc_states = yield from _softmax_jobs(t, jobs)
        o_slc = [[_softmax_out(st) for st in states] for states in slc_states]

        gates = _transposed(kvb_ref[0, :, (gate_col // LANES) * LANES:(gate_col // LANES + 1) * LANES])
        gates = 1.0 / (1.0 + jnp.exp(-gates[gate_col % LANES:gate_col % LANES + N_GATES + 4]))
        outs = []
        for r in range(Q_TILES):
            heads = []
            for h in _HEADS:
                g = [gates[br * GROUP_HEADS + h:br * GROUP_HEADS + h + 1, r * t:(r + 1) * t] for br in range(3)]
                heads.append(g[0] * o_cmp[r][h] + g[1] * o_slc[r][h] + g[2] * o_win[r][h])
            outs.append(heads)
        _store_heads(o_ref, outs)

    yield from query_step(0)


def _nsa(proj, kcv, cmp_params, layer, bias_diag, bias_sub, bias_cmp, cover_t, expand, t=TILE):
    s = proj.shape[1]
    values_t = pltpu.VMEM((s // t, ACC_ROWS, t), _MXU)
    in_specs = (_group_specs(s, CB_NS_Q, CB_NS_A, CB_NS_B)
                + [pl.BlockSpec((1, s, kcv.shape[2]), lambda b: (b, 0, 0))]
                + [pl.BlockSpec((None,) + a.shape[1:], lambda b: (layer, 0, 0)) for a in cmp_params]
                + _bias_specs(t, 1)
                + [pl.BlockSpec(bias_cmp.shape, lambda b: (0, 0, 0)),
                   pl.BlockSpec(cover_t.shape, lambda b: (0, 0)),
                   pl.BlockSpec(expand.shape, lambda b: (0, 0))])
    scratch = [pltpu.VMEM((s, HEAD_DIM + expand.shape[1]), _MXU), values_t, pltpu.VMEM((s, HEAD_DIM), _MXU), values_t]
    return _Mixer(_nsa_emit, in_specs,
                  [proj] * 3 + [kcv, *cmp_params, bias_diag, bias_sub, bias_cmp, cover_t, expand], scratch)


def _nsa_constants(s):
    n_cmp = (s - CMP_LEN) // CMP_STRIDE + 1
    n_slc = s // SLC_LEN
    assert n_cmp + 1 == s // CMP_STRIDE and n_slc % SUBLANES == 0
    c_start = np.arange(n_cmp) * CMP_STRIDE
    s_start = np.arange(n_slc) * SLC_LEN
    cover = np.clip(np.minimum((c_start + CMP_LEN - 1)[:, None], (s_start + SLC_LEN - 1)[None, :])
                    - np.maximum(c_start[:, None], s_start[None, :]) + 1, 0, None) / CMP_LEN
    cover_t = np.zeros((n_slc, n_cmp + 1), np.float32)
    cover_t[:, :n_cmp] = cover.T
    expand = (np.arange(s)[:, None] // SLC_LEN == np.arange(n_slc)[None, :]).astype(np.float32)
    return jnp.asarray(cover_t), jnp.asarray(expand, _MXU)


def kernel(x, w_in, w_out, w_up, w_down, norm_attn, norm_mlp, cmp_pos_k, cmp_pos_v, cmp_k_w1, cmp_k_w2,
           cmp_v_w1, cmp_v_w2, diff_lambda, diff_subln, rel_bias, final_norm):
    bsz, s, d = x.shape
    depth = w_in.shape[0]
    t = TILE
    n_chunk = s // CMP_STRIDE
    assert s == Q_TILES * t and MOBA_BLOCK % t == 0 and WINDOW % t == 0 and t >= MAX_DISTANCE

    assert w_in.shape[2] + PAD_COLS == D_IN_PAD
    w_in_c = jnp.pad(w_in.astype(_MXU), ((0, 0), (0, 0), (0, PAD_COLS)))
    w_out_c, w_up_c, w_down_c = (w.astype(_MXU) for w in (w_out, w_up, w_down))
    wk1, wk2 = cmp_k_w1.astype(_MXU), cmp_k_w2.astype(_MXU)
    wv1, wv2t = cmp_v_w1.astype(_MXU), jnp.swapaxes(cmp_v_w2, 1, 2).astype(_MXU)

    table_t = jnp.pad(rel_bias.T, ((0, 0), (0, LANES - N_BUCKETS)))
    tiles = dict(n_heads=rel_bias.shape[1], head0=0, rows=t, cols=t, col_tile=t, row_stride=-1, col_stride=1)
    bias_diag = _bias_table(table_t, offset=0, **tiles)
    bias_sub = _bias_table(table_t, offset=t, **tiles)
    bias_cmp = _bias_table(table_t, n_heads=GROUP_HEADS, head0=GROUP_HEADS, rows=n_chunk, cols=s,
                           col_tile=t, row_stride=-CMP_STRIDE, col_stride=1, offset=-(CMP_LEN - 1))
    cover_t, expand = _nsa_constants(s)
    col_scale = np.ones((1, D_IN_PAD), np.float32)
    for cb, width in ((CB_SB_Q, HEAD_DIM), (CB_MB_Q, HEAD_DIM), (CB_NS_Q, HEAD_DIM), (CB_DF_Q, DIFF_HALF)):
        col_scale[:, cb * GROUP_WIDTH:(cb + 1) * GROUP_WIDTH] = width ** -0.5 * LOG2E
    col_scale = jnp.asarray(col_scale)

    x2 = x.reshape(bsz * s, d)
    for layer in range(depth):
        proj, kcv = _norm_matmul(x2, norm_attn[layer][None], w_in_c, layer, col_scale,
                                 gap=(COLS_BEFORE_PAD, COLS_BEFORE_PAD + PAD_COLS),
                                 side_col=CB_NS_A * GROUP_WIDTH, side_width=2 * HEAD_DIM)
        proj = proj.reshape(bsz, s, D_IN_PAD)
        lambda_init = 0.8 - 0.6 * math.exp(-0.3 * layer)
        o_sb = _run_mixer(_stick_breaking(proj), bsz, s, "stick_breaking")
        o_mb = _run_mixer(_moba(proj, bias_diag, bias_sub), bsz, s, "moba")
        o_ns = _run_mixer(_nsa(proj, kcv.reshape(bsz, s, 2 * HEAD_DIM), (cmp_pos_k, cmp_pos_v, wk1, wk2, wv1, wv2t),
                               layer, bias_diag, bias_sub, bias_cmp, cover_t, expand), bsz, s, "nsa")
        o_df = _run_mixer(_diff(proj, diff_lambda[layer], diff_subln[layer][:, None], bias_diag, bias_sub,
                                lambda_init), bsz, s, "diff_attention")
        groups = [o.reshape(bsz * s, GROUP_WIDTH) for o in (o_sb, o_mb, o_ns, o_df)]
        x2 = _out_mlp(x2, groups, w_out_c, norm_mlp[layer][None], w_up_c, w_down_c, layer,
                      final_norm[None], final_norm=(layer == depth - 1))
    return x2.reshape(bsz, s, d)
```

```python
import functools
import math
from typing import Any, Callable, NamedTuple, Optional, Sequence

import numpy as np
import jax
import jax.numpy as jnp
from jax import lax
from jax.experimental import pallas as pl
from jax.experimental.pallas import tpu as pltpu

HEAD_DIM = 64
GROUP_HEADS = 4
GROUP_WIDTH = GROUP_HEADS * HEAD_DIM
NORM_EPS = 1e-6
NEG_INF = -1e30
BIG = 1e30
FORCE = 1e30
TINY = 1e-30
SOFTPLUS_CLAMP = 64.0
PICKED = -3e38
LOG2E = math.log2(math.e)
N_BUCKETS = 32
MAX_DISTANCE = 128
MOBA_BLOCK = 256
MOBA_TOPK = 3
CMP_LEN = 32
CMP_STRIDE = 16
SLC_LEN = 64
SLC_TOPN = 4
WINDOW = 512
DIFF_HALF = HEAD_DIM // 2
LANES = 128
SUBLANES = 8
TILE = 256
BF16_ROWS = 16
ACC_ROWS = HEAD_DIM + BF16_ROWS
N_GATES = 3 * GROUP_HEADS
COLS_BEFORE_PAD = 9 * GROUP_WIDTH - 2 * HEAD_DIM + N_GATES
PAD_COLS = 2 * HEAD_DIM - N_GATES
CB_SB_Q, CB_SB_K, CB_SB_V, CB_MB_Q, CB_MB_K, CB_MB_V, CB_NS_Q, CB_NS_A, CB_NS_B, CB_DF_Q, CB_DF_K, CB_DF_V = range(12)
D_IN_PAD = 12 * GROUP_WIDTH

_MXU = jnp.bfloat16
_VMEM_LIMIT = 56 * 1024 * 1024
_HEADS = range(GROUP_HEADS)
Q_TILES = 8
LOOKAHEAD = 2


def _dot(a, b, precision=None):
    return jnp.dot(a, b, precision=precision, preferred_element_type=jnp.float32)


def _dot_nt(a, b, precision=None):
    return lax.dot_general(a, b, (((1,), (1,)), ((), ())), precision=precision,
                           preferred_element_type=jnp.float32)


def _rms(x, g):
    return x * lax.rsqrt(jnp.mean(x * x, axis=-1, keepdims=True) + NORM_EPS) * g


def _params(*sem):
    return pltpu.CompilerParams(dimension_semantics=sem, vmem_limit_bytes=_VMEM_LIMIT)


def _norm_matmul_kernel(x_ref, g_ref, w_ref, scale_ref, o_ref, side_ref, wp_ref, *, tn, side_col, gap):
    @pl.when(pl.program_id(0) == 0)
    def _():
        lo, hi = gap
        wp_ref[:, :lo] = w_ref[:, :lo]
        wp_ref[:, lo:hi] = jnp.zeros((wp_ref.shape[0], hi - lo), wp_ref.dtype)
        wp_ref[:, hi:] = w_ref[:, lo:lo + wp_ref.shape[1] - hi]

    h = _rms(x_ref[...], g_ref[...]).astype(_MXU)
    for j in range(wp_ref.shape[1] // tn):
        cols = slice(j * tn, (j + 1) * tn)
        acc = _dot(h, wp_ref[:, cols])
        o_ref[:, cols] = (acc * scale_ref[:, cols]).astype(o_ref.dtype)
        if j * tn <= side_col < (j + 1) * tn:
            side_ref[...] = acc[:, side_col - j * tn:side_col - j * tn + side_ref.shape[1]]


def _norm_matmul(x, g, w, layer, col_scale, *, gap, side_col, side_width, tm=1024, tn=1024):
    m, d = x.shape
    n = w.shape[2]
    assert side_col % LANES == 0 and side_col // tn == (side_col + side_width - 1) // tn
    return pl.pallas_call(
        functools.partial(_norm_matmul_kernel, tn=tn, side_col=side_col, gap=gap),
        grid=(m // tm,),
        in_specs=[pl.BlockSpec((tm, d), lambda i: (i, 0)),
                  pl.BlockSpec((1, d), lambda i: (0, 0)),
                  pl.BlockSpec((None, d, w.shape[2]), lambda i: (layer, 0, 0), pipeline_mode=pl.Buffered(1)),
                  pl.BlockSpec((1, n), lambda i: (0, 0))],
        out_specs=[pl.BlockSpec((tm, n), lambda i: (i, 0)),
                   pl.BlockSpec((tm, side_width), lambda i: (i, 0))],
        out_shape=[jax.ShapeDtypeStruct((m, n), _MXU),
                   jax.ShapeDtypeStruct((m, side_width), jnp.float32)],
        scratch_shapes=[pltpu.VMEM((d, n), _MXU)],
        compiler_params=_params("arbitrary"),
        name="norm_in_proj",
    )(x, g, w, col_scale)


def _out_mlp_kernel(x_ref, a_ref, b_ref, c_ref, d_ref, wo_ref, g_ref, wu_ref, wd_ref, gf_ref, o_ref,
                    *, final_norm, tf):
    mixed = jnp.concatenate([a_ref[...], b_ref[...], c_ref[...], d_ref[...]], axis=1)
    y = x_ref[...] + _dot(mixed, wo_ref[...])
    h = _rms(y, g_ref[...]).astype(_MXU)
    for c in range(wu_ref.shape[1] // tf):
        u = jnp.square(jnp.maximum(_dot(h, wu_ref[:, c * tf:(c + 1) * tf]), 0.0))
        y = y + _dot(u.astype(_MXU), wd_ref[c * tf:(c + 1) * tf, :])
    if final_norm:
        y = _rms(y, gf_ref[...])
    o_ref[...] = y


def _out_mlp(x, groups, w_out, g, w_up, w_down, layer, g_final, *, final_norm, tm=1024, tf=1024):
    m, d = x.shape
    f = w_up.shape[2]
    gspec = pl.BlockSpec((tm, GROUP_WIDTH), lambda i: (i, 0))
    row = pl.BlockSpec((1, d), lambda i: (0, 0))

    def resident(rows, cols):
        return pl.BlockSpec((None, rows, cols), lambda i: (layer, 0, 0), pipeline_mode=pl.Buffered(1))

    return pl.pallas_call(
        functools.partial(_out_mlp_kernel, final_norm=final_norm, tf=tf),
        grid=(m // tm,),
        in_specs=[pl.BlockSpec((tm, d), lambda i: (i, 0)), gspec, gspec, gspec, gspec,
                  resident(d, d), row, resident(d, f), resident(f, d), row],
        out_specs=pl.BlockSpec((tm, d), lambda i: (i, 0)),
        out_shape=jax.ShapeDtypeStruct((m, d), jnp.float32),
        compiler_params=_params("parallel"),
        name="out_proj_mlp_residual",
    )(x, *groups, w_out, g, w_up, w_down, g_final)


def _t5_bucket(dist):
    n = jnp.maximum(dist, 0)
    max_exact = N_BUCKETS // 2
    nf = jnp.maximum(n, 1).astype(jnp.float32)
    large = max_exact + (jnp.log(nf / max_exact) / math.log(MAX_DISTANCE / max_exact)
                         * (N_BUCKETS - max_exact)).astype(jnp.int32)
    large = jnp.minimum(large, N_BUCKETS - 1)
    return jnp.where(n < max_exact, n, large)


def _bias_kernel(tab_ref, o_ref, *, row_stride, col_stride, offset, head0):
    nh, tr, tc = o_ref.shape
    for blk in range(tc // LANES):
        rows = lax.broadcasted_iota(jnp.int32, (tr, LANES), 0)
        cols = lax.broadcasted_iota(jnp.int32, (tr, LANES), 1) + (pl.program_id(0) * tc + blk * LANES)
        bucket = _t5_bucket(rows * row_stride + cols * col_stride + offset)
        for h in range(nh):
            row = tab_ref[head0 + h:head0 + h + 1, :]
            row = (row - row[:, N_BUCKETS - 1:N_BUCKETS]) * LOG2E
            o_ref[h, :, blk * LANES:(blk + 1) * LANES] = jnp.take_along_axis(
                jnp.broadcast_to(row, (tr, LANES)), bucket, axis=1, mode="promise_in_bounds")


def _bias_table(table_t, *, n_heads, head0, rows, cols, col_tile, row_stride, col_stride, offset):
    return pl.pallas_call(
        functools.partial(_bias_kernel, row_stride=row_stride, col_stride=col_stride,
                          offset=offset, head0=head0),
        grid=(cols // col_tile,),
        in_specs=[pl.BlockSpec(table_t.shape, lambda i: (0, 0))],
        out_specs=pl.BlockSpec((n_heads, rows, col_tile), lambda i: (0, 0, i)),
        out_shape=jax.ShapeDtypeStruct((n_heads, rows, cols), jnp.float32),
        compiler_params=_params("parallel"),
        name="t5_bias_tiles",
    )(table_t)


def _softmax_init(t):
    return (jnp.full((1, t), NEG_INF, jnp.float32), jnp.zeros((ACC_ROWS, t), jnp.float32))


class _KeyTile(NamedTuple):
    kts: Sequence[Any]
    vts: Sequence[Any]
    biases: Optional[Sequence[Any]] = None
    emasks: Optional[Sequence[Any]] = None
    qmasks: Optional[Sequence[Any]] = None


def _round_robin(lists):
    out = []
    for rank in range(max(map(len, lists), default=0)):
        out.extend(items[rank] for items in lists if rank < len(items))
    return out


def _softmax_jobs(t, jobs):
    built = {}

    def tiles_of(job, g):
        if (job, g) not in built:
            built[job, g] = [make() for make in jobs[job][2][g]]
        return built[job, g]

    def scores_of(job, g, c):
        qts = jobs[job][1]
        row = []
        for tile in tiles_of(job, g):
            s = _dot(tile.kts[c], qts[c])
            if tile.biases is not None:
                s = s + tile.biases[c]
            if tile.emasks is not None:
                s = jnp.where(tile.emasks[c], s, NEG_INF)
            row.append(s.astype(_MXU))
        return row

    def update(state, job, g, c, scores):
        m, acc = state
        m_new = m
        for tile, s in zip(tiles_of(job, g), scores):
            tile_max = jnp.max(s, axis=0, keepdims=True).astype(jnp.float32)
            if tile.qmasks is not None:
                tile_max = jnp.where(tile.qmasks[c], tile_max, NEG_INF)
            m_new = jnp.maximum(m_new, tile_max)
        seen = m_new > 0.5 * NEG_INF
        acc = jnp.exp2(m - m_new) * acc
        for tile, s in zip(tiles_of(job, g), scores):
            ok = seen if tile.qmasks is None else jnp.logical_and(seen, tile.qmasks[c])
            acc = acc + _dot(tile.vts[c], jnp.exp2(s - jnp.where(ok, m_new, BIG).astype(_MXU)))
        return m_new, acc

    units = _round_robin([[(job, g, c) for g in range(len(groups)) for c in range(n)]
                          for job, (n, _, groups) in enumerate(jobs)])
    lookahead = LOOKAHEAD
    states = [[_softmax_init(t) for _ in range(n)] for n, _, _ in jobs]
    pending = {k: scores_of(*units[k]) for k in range(min(lookahead, len(units)))}
    for k, (job, g, c) in enumerate(units):
        if k + lookahead < len(units):
            pending[k + lookahead] = scores_of(*units[k + lookahead])
        states[job][c] = update(states[job][c], job, g, c, pending.pop(k))
        yield
    return states


def _pairs(items):
    return [items[p:p + 2] for p in range(0, len(items), 2)]


class _Mixer(NamedTuple):
    emit: Callable[..., Any]
    in_specs: Sequence[Any]
    operands: Sequence[Any]
    scratch: Sequence[Any]


def _mixer_kernel(*refs, emit):
    for _ in emit(*refs):
        pass


def _run_mixer(mixer, bsz, s, name):
    return pl.pallas_call(
        functools.partial(_mixer_kernel, emit=mixer.emit),
        grid=(bsz,),
        in_specs=list(mixer.in_specs),
        out_specs=pl.BlockSpec((1, s, GROUP_WIDTH), lambda b: (b, 0, 0)),
        out_shape=jax.ShapeDtypeStruct((bsz, s, GROUP_WIDTH), _MXU),
        scratch_shapes=list(mixer.scratch),
        compiler_params=_params("parallel"),
        name=name,
    )(*mixer.operands)


def _softmax_out(state):
    acc = state[1]
    return acc[:HEAD_DIM] / jnp.maximum(acc[HEAD_DIM:HEAD_DIM + 1], TINY)


def _top_k_rows(score, row_f, k):
    sel = jnp.zeros(score.shape, jnp.float32)
    for _ in range(k):
        mx = jnp.max(score, axis=0, keepdims=True)
        idx = jnp.min(jnp.where(score == mx, row_f, float(score.shape[0])), axis=0, keepdims=True)
        pick = row_f == idx
        sel = jnp.where(pick, 1.0, sel)
        score = jnp.where(pick, PICKED, score)
    return sel


def _now(fn):
    fn()


def _tile_iotas(t):
    return (lax.broadcasted_iota(jnp.int32, (t, t), 0), lax.broadcasted_iota(jnp.int32, (t, t), 1))


def _key_rows(ref, j, t):
    if isinstance(j, int):
        return ref[j * t:(j + 1) * t, :]
    return ref[pl.ds(pl.multiple_of(j * t, t), t), :]


def _transposed(ref_block):
    return ref_block.astype(jnp.float32).T


def _queries_t(q_ref, width, t):
    qt = _transposed(q_ref[0]).astype(_MXU)
    return [[qt[c * width:(c + 1) * width, r * t:(r + 1) * t] for c in range(GROUP_WIDTH // width)]
            for r in range(Q_TILES)]


def _fill_values_t(vt_ref, v_ref, col0, t):
    n_tiles, rows, _ = vt_ref.shape
    lane_block = (col0 // LANES) * LANES
    for c in range(n_tiles):
        blk = _transposed(v_ref[0, c * t:(c + 1) * t, lane_block:lane_block + LANES])
        vt_ref[c, 0:HEAD_DIM, :] = blk[col0 - lane_block:col0 - lane_block + HEAD_DIM].astype(vt_ref.dtype)
        if rows == ACC_ROWS:
            first = lax.broadcasted_iota(jnp.int32, (rows - HEAD_DIM, t), 0) == 0
            vt_ref[c, HEAD_DIM:rows, :] = jnp.where(first, 1.0, 0.0).astype(vt_ref.dtype)


def _group_specs(s, *column_blocks):
    return [pl.BlockSpec((1, s, GROUP_WIDTH), functools.partial(lambda cb, b: (b, 0, cb), cb))
            for cb in column_blocks]


def _bias_specs(t, head_group):
    spec = pl.BlockSpec((GROUP_HEADS, t, t), lambda b: (head_group, 0, 0))
    return [spec, spec]


def _store_heads(o_ref, outs_t):
    tiles = [jnp.concatenate(heads, axis=0) for heads in outs_t]
    o_ref[0] = jnp.concatenate(tiles, axis=1).T.astype(o_ref.dtype)


def _sb_emit(q_ref, k_ref, v_ref, o_ref, kb_ref, vt_ref):
    t = q_ref.shape[1] // Q_TILES
    key, qry = _tile_iotas(t)
    strict = key < qry
    later = jnp.where(qry > key, 1.0, 0.0).astype(_MXU)

    @_now
    def _():
        for h in _HEADS:
            kb_ref[h] = k_ref[0, :, h * HEAD_DIM:(h + 1) * HEAD_DIM].astype(kb_ref.dtype)
            _fill_values_t(vt_ref.at[h], v_ref, h * HEAD_DIM, t)

    def query_step(step):
        qts = _queries_t(q_ref, HEAD_DIM, t)
        units = _round_robin([[(r, step * Q_TILES + r, pair, h)
                               for pair in _pairs(list(range(step * Q_TILES + r, -1, -1))) for h in _HEADS]
                              for r in range(Q_TILES)])
        zero = (jnp.zeros((HEAD_DIM, t), jnp.float32), jnp.zeros((1, t), jnp.float32))
        carry = [[zero] * GROUP_HEADS for _ in range(Q_TILES)]
        zs, log_keeps, suffixes = {}, {}, {}

        def scores(u):
            r, _, pair, h = units[u]
            zs[u] = [_dot(_key_rows(kb_ref.at[h], j, t), qts[r][h]) for j in pair]

        def keeps(u):
            _, k, pair, _ = units[u]
            log_keeps[u], suffixes[u] = [], []
            for j, z in zip(pair, zs[u]):
                drop = jnp.maximum(jnp.log2(1.0 + jnp.exp2(jnp.minimum(z, SOFTPLUS_CLAMP))), z)
                if j == k:
                    drop = jnp.where(strict, drop, 0.0)
                log_keeps[u].append(drop)
                suffixes[u].append(_dot(later, drop.astype(_MXU)))

        def values(u):
            r, k, pair, h = units[u]
            acc, run = carry[r][h]
            weights = []
            for j, z, drop, suffix in zip(pair, zs.pop(u), log_keeps.pop(u), suffixes.pop(u)):
                a = jnp.exp2(z - drop - suffix + run)
                if j == k:
                    a = jnp.where(strict, a, 0.0)
                weights.append(a.astype(_MXU))
                run = run - (suffix[0:1] + drop[0:1])
            for j, w in zip(pair, weights):
                acc = acc + _dot(vt_ref[h, j], w)
            carry[r][h] = (acc, run)

        stages = (scores, keeps, values)
        for tick in range(len(units) + len(stages) - 1):
            for lag, stage in enumerate(stages):
                if 0 <= tick - lag < len(units):
                    stage(tick - lag)
            yield
        _store_heads(o_ref, [[c[0] for c in tile_carry] for tile_carry in carry])

    yield from query_step(0)


def _stick_breaking(proj, t=TILE):
    s = proj.shape[1]
    scratch = [pltpu.VMEM((GROUP_HEADS, s, HEAD_DIM), _MXU), pltpu.VMEM((GROUP_HEADS, s // t, HEAD_DIM, t), _MXU)]
    return _Mixer(_sb_emit, _group_specs(s, CB_SB_Q, CB_SB_K, CB_SB_V), [proj] * 3, scratch)


def _moba_emit(q_ref, k_ref, v_ref, bd_ref, bs_ref, o_ref, kb_ref, vt_ref, km_ref):
    t = q_ref.shape[1] // Q_TILES
    n_blk = k_ref.shape[1] // MOBA_BLOCK
    tiles_per_blk = MOBA_BLOCK // t
    blk_shift = int(math.log2(tiles_per_blk))

    @_now
    def _():
        km_ref[...] = jnp.zeros_like(km_ref)
        for h in _HEADS:
            lo, hi = h * HEAD_DIM, (h + 1) * HEAD_DIM
            kb_ref[h] = k_ref[0, :, lo:hi].astype(kb_ref.dtype)
            _fill_values_t(vt_ref.at[h], v_ref, lo, t)
            for n in range(n_blk):
                blk = k_ref[0, n * MOBA_BLOCK:(n + 1) * MOBA_BLOCK, lo:hi]
                km_ref[h, n:n + 1, :] = jnp.mean(blk.astype(jnp.float32), axis=0, keepdims=True)

    def query_step(step):
        key, qry = _tile_iotas(t)
        causal = key <= qry
        blk_row = lax.broadcasted_iota(jnp.int32, (km_ref.shape[1], t), 0)
        qt = _transposed(q_ref[0])
        jobs = []
        for r in range(Q_TILES):
            k = step * Q_TILES + r
            own = k >> blk_shift
            qts, sels = [], []
            for h in _HEADS:
                qf = qt[h * HEAD_DIM:(h + 1) * HEAD_DIM, r * t:(r + 1) * t]
                qts.append(qf.astype(_MXU))
                gate = _dot(km_ref[h], qf, precision=lax.Precision.HIGHEST)
                gate = jnp.where(blk_row < own, gate, NEG_INF)
                sel = _top_k_rows(gate, blk_row.astype(jnp.float32), min(MOBA_TOPK, n_blk - 1))
                sels.append(jnp.where(blk_row < own, sel, 0.0))

            def tile(own, sels, j, bias_ref=None, emask=None):
                n = j >> blk_shift
                return _KeyTile([_key_rows(kb_ref.at[h], j, t) for h in _HEADS], [vt_ref[h, j] for h in _HEADS],
                                None if bias_ref is None else [bias_ref[h] for h in _HEADS],
                                None if emask is None else [emask] * GROUP_HEADS,
                                None if n == own else [sels[h][n:n + 1] > 0.5 for h in _HEADS])

            tile = functools.partial(tile, own, sels)
            groups = _pairs([functools.partial(tile, j) for j in range(k - 1)])
            groups.append(([functools.partial(tile, k - 1, bs_ref)] if k else [])
                          + [functools.partial(tile, k, bd_ref, causal)])
            jobs.append((GROUP_HEADS, qts, groups))
        all_states = yield from _softmax_jobs(t, jobs)
        _store_heads(o_ref, [[_softmax_out(st) for st in states] for states in all_states])

    yield from query_step(0)


def _kv_scratch(s, t, key_dim=HEAD_DIM, n_keys=GROUP_HEADS):
    return [pltpu.VMEM((n_keys, s, key_dim), _MXU), pltpu.VMEM((GROUP_HEADS, s // t, ACC_ROWS, t), _MXU)]


def _moba(proj, bias_diag, bias_sub, t=TILE):
    s = proj.shape[1]
    n_blk_pad = -(-(s // MOBA_BLOCK) // SUBLANES) * SUBLANES
    scratch = _kv_scratch(s, t) + [pltpu.VMEM((GROUP_HEADS, n_blk_pad, HEAD_DIM), jnp.float32)]
    return _Mixer(_moba_emit, _group_specs(s, CB_MB_Q, CB_MB_K, CB_MB_V) + _bias_specs(t, 0),
                  [proj] * 3 + [bias_diag, bias_sub], scratch)


def _diff_emit(lam_ref, g_ref, q_ref, k_ref, v_ref, bd_ref, bs_ref, o_ref, kb_ref, vt_ref, *, lambda_init):
    t = q_ref.shape[1] // Q_TILES
    key, qry = _tile_iotas(t)
    causal = key <= qry
    lv = lam_ref[...]
    lam = (jnp.exp(jnp.sum(lv[0:1] * lv[1:2], keepdims=True))
           - jnp.exp(jnp.sum(lv[2:3] * lv[3:4], keepdims=True)) + lambda_init)
    halves = range(2 * GROUP_HEADS)

    @_now
    def _():
        for c in halves:
            kb_ref[c] = k_ref[0, :, c * DIFF_HALF:(c + 1) * DIFF_HALF].astype(kb_ref.dtype)
        for h in _HEADS:
            _fill_values_t(vt_ref.at[h], v_ref, h * HEAD_DIM, t)

    def tile(j, bias_ref=None, emask=None):
        n = len(halves)
        return _KeyTile([_key_rows(kb_ref.at[c], j, t) for c in halves], [vt_ref[c // 2, j] for c in halves],
                        None if bias_ref is None else [bias_ref[c // 2] for c in halves],
                        None if emask is None else [emask] * n)

    def query_step(step):
        qts = _queries_t(q_ref, DIFF_HALF, t)
        jobs = []
        for r in range(Q_TILES):
            k = step * Q_TILES + r
            groups = _pairs([functools.partial(tile, j) for j in range(k - 1)])
            groups.append(([functools.partial(tile, k - 1, bs_ref)] if k else [])
                          + [functools.partial(tile, k, bd_ref, causal)])
            jobs.append((len(halves), qts[r], groups))
        all_states = yield from _softmax_jobs(t, jobs)
        outs = []
        for states in all_states:
            heads = []
            for h in _HEADS:
                o = _softmax_out(states[2 * h]) - lam * _softmax_out(states[2 * h + 1])
                o = o * lax.rsqrt(jnp.mean(o * o, axis=0, keepdims=True) + NORM_EPS) * g_ref[...]
                heads.append(o * (1.0 - lambda_init))
            outs.append(heads)
        _store_heads(o_ref, outs)

    yield from query_step(0)


def _diff(proj, lam_params, subln, bias_diag, bias_sub, lambda_init, t=TILE):
    s = proj.shape[1]
    in_specs = ([pl.BlockSpec(lam_params.shape, lambda b: (0, 0)), pl.BlockSpec(subln.shape, lambda b: (0, 0))]
                + _group_specs(s, CB_DF_Q, CB_DF_K, CB_DF_V) + _bias_specs(t, 2))
    return _Mixer(functools.partial(_diff_emit, lambda_init=lambda_init), in_specs,
                  [lam_params, subln] + [proj] * 3 + [bias_diag, bias_sub],
                  _kv_scratch(s, t, key_dim=DIFF_HALF, n_keys=2 * GROUP_HEADS))


def _compress_blocks(kcv_ref, pk_ref, pv_ref, wk1_ref, wk2_ref, wv1_ref, wv2t_ref):
    n_chunk = kcv_ref.shape[1] // CMP_STRIDE

    branches = ((0, pk_ref, wk1_ref), (HEAD_DIM, pv_ref, wv1_ref))
    tops = [jnp.zeros((n_chunk, w1_ref.shape[1]), jnp.float32) for _, _, w1_ref in branches]
    bots = list(tops)
    pack = 2 * LANES // HEAD_DIM
    for l0 in range(0, CMP_STRIDE, pack):
        tokens = [kcv_ref[0, pl.ds(l, n_chunk, stride=CMP_STRIDE), :] for l in range(l0, l0 + pack)]
        for n, (col0, p_ref, w1_ref) in enumerate(branches):
            for half, acc in ((0, tops), (CMP_STRIDE, bots)):
                x = jnp.concatenate([(tok[:, col0:col0 + HEAD_DIM] + p_ref[half + l:half + l + 1, :]).astype(_MXU)
                                     for l, tok in zip(range(l0, l0 + pack), tokens)], axis=1)
                acc[n] = acc[n] + _dot(x, w1_ref[(half + l0) * HEAD_DIM:(half + l0 + pack) * HEAD_DIM, :])
    hidden = [jax.nn.gelu(top + pltpu.roll(bot, n_chunk - 1, axis=0)).astype(_MXU) for top, bot in zip(tops, bots)]
    return _dot(hidden[0], wk2_ref[...]), _dot_nt(wv2t_ref[...], hidden[1])


def _nsa_emit(q_ref, kva_ref, kvb_ref, kcv_ref, pk_ref, pv_ref, wk1_ref, wk2_ref, wv1_ref, wv2t_ref,
              bd_ref, bs_ref, bc_ref, cover_ref, e_ref, o_ref, ks_ref, vst_ref, kw_ref, vwt_ref):
    t = q_ref.shape[1] // Q_TILES
    key, qry = _tile_iotas(t)
    causal = key <= qry
    ks_col, vs_col, kw_col, vw_col, gate_col = 2 * HEAD_DIM, 3 * HEAD_DIM, 0, HEAD_DIM, 2 * HEAD_DIM

    @_now
    def _():
        ks_ref[:, :HEAD_DIM] = kva_ref[0, :, ks_col:ks_col + HEAD_DIM].astype(ks_ref.dtype)
        ks_ref[:, HEAD_DIM:] = e_ref[...]
        kw_ref[...] = kvb_ref[0, :, kw_col:kw_col + HEAD_DIM].astype(kw_ref.dtype)
        _fill_values_t(vst_ref, kva_ref, vs_col, t)
        _fill_values_t(vwt_ref, kvb_ref, vw_col, t)

    def tile(k_ref, vt_ref, j, bias_ref=None, emask=None):
        n = GROUP_HEADS
        return _KeyTile([_key_rows(k_ref, j, t)] * n, [vt_ref[j]] * n,
                        None if bias_ref is None else [bias_ref[h] for h in _HEADS],
                        None if emask is None else [emask] * n)

    def query_step(step):
        qts = _queries_t(q_ref, HEAD_DIM, t)
        kc, vct = (a.astype(_MXU) for a in _compress_blocks(kcv_ref, pk_ref, pv_ref, wk1_ref, wk2_ref,
                                                            wv1_ref, wv2t_ref))
        n_cmp = kc.shape[0]
        n_slc = cover_ref.shape[0]
        c_row = lax.broadcasted_iota(jnp.int32, (n_cmp, t), 0)
        c_col = lax.broadcasted_iota(jnp.int32, (n_cmp, t), 1)
        s_row = lax.broadcasted_iota(jnp.int32, (n_slc, t), 0)
        s_col = lax.broadcasted_iota(jnp.int32, (n_slc, t), 1)
        tiles = [step * Q_TILES + r for r in range(Q_TILES)]

        o_cmp, importance = [], []
        for r, k in enumerate(tiles):
            visible = c_col + k * t >= c_row * CMP_STRIDE + (CMP_LEN - 1)
            cmp_scores = [_dot(kc, qts[r][h]) for h in _HEADS]
            cmp_probs = []
            p_sum = jnp.zeros((n_cmp, t), jnp.float32)
            for h in _HEADS:
                sc = jnp.where(visible, cmp_scores[h] + bc_ref[h, :, r * t:(r + 1) * t], NEG_INF)
                e = jnp.where(visible, jnp.exp2(sc - jnp.max(sc, axis=0, keepdims=True)), 0.0)
                p = e / jnp.maximum(jnp.sum(e, axis=0, keepdims=True), TINY)
                cmp_probs.append(p.astype(_MXU))
                p_sum = p_sum + p
            o_cmp.append([_dot(vct, cmp_probs[h]) for h in _HEADS])
            importance.append(_dot(cover_ref[...], p_sum, precision=lax.Precision.HIGHEST))

        n_back = WINDOW // t
        jobs = []
        for r, k in enumerate(tiles):
            window = [functools.partial(tile, kw_ref, vwt_ref, k - n_back, None, qry < key)] if k >= n_back else []
            for back in range(min(n_back - 1, k), 0, -1):
                window.append(functools.partial(tile, kw_ref, vwt_ref, k - back, bs_ref if back == 1 else None))
            window.append(functools.partial(tile, kw_ref, vwt_ref, k, bd_ref, causal))
            jobs.append((GROUP_HEADS, qts[r], [window]))
        win_states = yield from _softmax_jobs(t, jobs)
        o_win = [[_softmax_out(st) for st in states] for states in win_states]

        jobs = []
        for r, k in enumerate(tiles):
            own = jnp.right_shift(s_col + k * t, int(math.log2(SLC_LEN)))
            score = jnp.where(s_row == own, FORCE, jnp.where(s_row < own, importance[r], NEG_INF))
            sel = _top_k_rows(score, s_row.astype(jnp.float32), min(SLC_TOPN, n_slc))
            penalty = jnp.where(sel > 0.5, 0.0, NEG_INF).astype(_MXU)
            q_aug = [jnp.concatenate([qts[r][h], penalty], axis=0) for h in _HEADS]
            groups = _pairs([functools.partial(tile, ks_ref, vst_ref, j) for j in range(k - 1)])
            groups.append(([functools.partial(tile, ks_ref, vst_ref, k - 1, bs_ref)] if k else [])
                          + [functools.partial(tile, ks_ref, vst_ref, k, bd_ref, causal)])
            jobs.append((GROUP_HEADS, q_aug, groups))
        slc_states = yield from _softmax_jobs(t, jobs)
        o_slc = [[_softmax_out(st) for st in states] for states in slc_states]

        gates = _transposed(kvb_ref[0, :, (gate_col // LANES) * LANES:(gate_col // LANES + 1) * LANES])
        gates = 1.0 / (1.0 + jnp.exp(-gates[gate_col % LANES:gate_col % LANES + N_GATES + 4]))
        outs = []
        for r in range(Q_TILES):
            heads = []
            for h in _HEADS:
                g = [gates[br * GROUP_HEADS + h:br * GROUP_HEADS + h + 1, r * t:(r + 1) * t] for br in range(3)]
                heads.append(g[0] * o_cmp[r][h] + g[1] * o_slc[r][h] + g[2] * o_win[r][h])
            outs.append(heads)
        _store_heads(o_ref, outs)

    yield from query_step(0)


def _nsa(proj, kcv, cmp_params, layer, bias_diag, bias_sub, bias_cmp, cover_t, expand, t=TILE):
    s = proj.shape[1]
    values_t = pltpu.VMEM((s // t, ACC_ROWS, t), _MXU)
    in_specs = (_group_specs(s, CB_NS_Q, CB_NS_A, CB_NS_B)
                + [pl.BlockSpec((1, s, kcv.shape[2]), lambda b: (b, 0, 0))]
                + [pl.BlockSpec((None,) + a.shape[1:], lambda b: (layer, 0, 0)) for a in cmp_params]
                + _bias_specs(t, 1)
                + [pl.BlockSpec(bias_cmp.shape, lambda b: (0, 0, 0)),
                   pl.BlockSpec(cover_t.shape, lambda b: (0, 0)),
                   pl.BlockSpec(expand.shape, lambda b: (0, 0))])
    scratch = [pltpu.VMEM((s, HEAD_DIM + expand.shape[1]), _MXU), values_t, pltpu.VMEM((s, HEAD_DIM), _MXU), values_t]
    return _Mixer(_nsa_emit, in_specs,
                  [proj] * 3 + [kcv, *cmp_params, bias_diag, bias_sub, bias_cmp, cover_t, expand], scratch)


def _nsa_constants(s):
    n_cmp = (s - CMP_LEN) // CMP_STRIDE + 1
    n_slc = s // SLC_LEN
    assert n_cmp + 1 == s // CMP_STRIDE and n_slc % SUBLANES == 0
    c_start = np.arange(n_cmp) * CMP_STRIDE
    s_start = np.arange(n_slc) * SLC_LEN
    cover = np.clip(np.minimum((c_start + CMP_LEN - 1)[:, None], (s_start + SLC_LEN - 1)[None, :])
                    - np.maximum(c_start[:, None], s_start[None, :]) + 1, 0, None) / CMP_LEN
    cover_t = np.zeros((n_slc, n_cmp + 1), np.float32)
    cover_t[:, :n_cmp] = cover.T
    expand = (np.arange(s)[:, None] // SLC_LEN == np.arange(n_slc)[None, :]).astype(np.float32)
    return jnp.asarray(cover_t), jnp.asarray(expand, _MXU)


def kernel(x, w_in, w_out, w_up, w_down, norm_attn, norm_mlp, cmp_pos_k, cmp_pos_v, cmp_k_w1, cmp_k_w2,
           cmp_v_w1, cmp_v_w2, diff_lambda, diff_subln, rel_bias, final_norm):
    bsz, s, d = x.shape
    depth = w_in.shape[0]
    t = TILE
    n_chunk = s // CMP_STRIDE
    assert s == Q_TILES * t and MOBA_BLOCK % t == 0 and WINDOW % t == 0 and t >= MAX_DISTANCE

    assert w_in.shape[2] + PAD_COLS == D_IN_PAD
    w_in_c = jnp.pad(w_in.astype(_MXU), ((0, 0), (0, 0), (0, PAD_COLS)))
    w_out_c, w_up_c, w_down_c = (w.astype(_MXU) for w in (w_out, w_up, w_down))
    wk1, wk2 = cmp_k_w1.astype(_MXU), cmp_k_w2.astype(_MXU)
    wv1, wv2t = cmp_v_w1.astype(_MXU), jnp.swapaxes(cmp_v_w2, 1, 2).astype(_MXU)

    table_t = jnp.pad(rel_bias.T, ((0, 0), (0, LANES - N_BUCKETS)))
    tiles = dict(n_heads=rel_bias.shape[1], head0=0, rows=t, cols=t, col_tile=t, row_stride=-1, col_stride=1)
    bias_diag = _bias_table(table_t, offset=0, **tiles)
    bias_sub = _bias_table(table_t, offset=t, **tiles)
    bias_cmp = _bias_table(table_t, n_heads=GROUP_HEADS, head0=GROUP_HEADS, rows=n_chunk, cols=s,
                           col_tile=t, row_stride=-CMP_STRIDE, col_stride=1, offset=-(CMP_LEN - 1))
    cover_t, expand = _nsa_constants(s)
    col_scale = np.ones((1, D_IN_PAD), np.float32)
    for cb, width in ((CB_SB_Q, HEAD_DIM), (CB_MB_Q, HEAD_DIM), (CB_NS_Q, HEAD_DIM), (CB_DF_Q, DIFF_HALF)):
        col_scale[:, cb * GROUP_WIDTH:(cb + 1) * GROUP_WIDTH] = width ** -0.5 * LOG2E
    col_scale = jnp.asarray(col_scale)

    x2 = x.reshape(bsz * s, d)
    for layer in range(depth):
        proj, kcv = _norm_matmul(x2, norm_attn[layer][None], w_in_c, layer, col_scale,
                                 gap=(COLS_BEFORE_PAD, COLS_BEFORE_PAD + PAD_COLS),
                                 side_col=CB_NS_A * GROUP_WIDTH, side_width=2 * HEAD_DIM)
        proj = proj.reshape(bsz, s, D_IN_PAD)
        lambda_init = 0.8 - 0.6 * math.exp(-0.3 * layer)
        o_sb = _run_mixer(_stick_breaking(proj), bsz, s, "stick_breaking")
        o_mb = _run_mixer(_moba(proj, bias_diag, bias_sub), bsz, s, "moba")
        o_ns = _run_mixer(_nsa(proj, kcv.reshape(bsz, s, 2 * HEAD_DIM), (cmp_pos_k, cmp_pos_v, wk1, wk2, wv1, wv2t),
                               layer, bias_diag, bias_sub, bias_cmp, cover_t, expand), bsz, s, "nsa")
        o_df = _run_mixer(_diff(proj, diff_lambda[layer], diff_subln[layer][:, None], bias_diag, bias_sub,
                                lambda_init), bsz, s, "diff_attention")
        groups = [o.reshape(bsz * s, GROUP_WIDTH) for o in (o_sb, o_mb, o_ns, o_df)]
        x2 = _out_mlp(x2, groups, w_out_c, norm_mlp[layer][None], w_up_c, w_down_c, layer,
                      final_norm[None], final_norm=(layer == depth - 1))
    return x2.reshape(bsz, s, d)
```

```python
import functools
import math
from typing import Any, Callable, NamedTuple, Optional, Sequence

import numpy as np
import jax
import jax.numpy as jnp
from jax import lax
from jax.experimental import pallas as pl
from jax.experimental.pallas import tpu as pltpu

HEAD_DIM = 64
GROUP_HEADS = 4
GROUP_WIDTH = GROUP_HEADS * HEAD_DIM
NORM_EPS = 1e-6
NEG_INF = -1e30
BIG = 1e30
FORCE = 1e30
TINY = 1e-30
SOFTPLUS_CLAMP = 64.0
PICKED = -3e38
LOG2E = math.log2(math.e)
N_BUCKETS = 32
MAX_DISTANCE = 128
MOBA_BLOCK = 256
MOBA_TOPK = 3
CMP_LEN = 32
CMP_STRIDE = 16
SLC_LEN = 64
SLC_TOPN = 4
WINDOW = 512
DIFF_HALF = HEAD_DIM // 2
LANES = 128
SUBLANES = 8
TILE = 256
BF16_ROWS = 16
ACC_ROWS = HEAD_DIM + BF16_ROWS
N_GATES = 3 * GROUP_HEADS
COLS_BEFORE_PAD = 9 * GROUP_WIDTH - 2 * HEAD_DIM + N_GATES
PAD_COLS = 2 * HEAD_DIM - N_GATES
CB_SB_Q, CB_SB_K, CB_SB_V, CB_MB_Q, CB_MB_K, CB_MB_V, CB_NS_Q, CB_NS_A, CB_NS_B, CB_DF_Q, CB_DF_K, CB_DF_V = range(12)
D_IN_PAD = 12 * GROUP_WIDTH

_MXU = jnp.bfloat16
_VMEM_LIMIT = 56 * 1024 * 1024
_HEADS = range(GROUP_HEADS)
Q_TILES = 8
SB_STAGE_LAG = 2
LOOKAHEAD = 2


def _dot(a, b, precision=None):
    return jnp.dot(a, b, precision=precision, preferred_element_type=jnp.float32)


def _dot_nt(a, b, precision=None):
    return lax.dot_general(a, b, (((1,), (1,)), ((), ())), precision=precision,
                           preferred_element_type=jnp.float32)


def _rms(x, g):
    return x * lax.rsqrt(jnp.mean(x * x, axis=-1, keepdims=True) + NORM_EPS) * g


def _params(*sem):
    return pltpu.CompilerParams(dimension_semantics=sem, vmem_limit_bytes=_VMEM_LIMIT)


def _norm_matmul_kernel(x_ref, g_ref, w_ref, scale_ref, o_ref, side_ref, wp_ref, *, tn, side_col, gap):
    @pl.when(pl.program_id(0) == 0)
    def _():
        lo, hi = gap
        wp_ref[:, :lo] = w_ref[:, :lo]
        wp_ref[:, lo:hi] = jnp.zeros((wp_ref.shape[0], hi - lo), wp_ref.dtype)
        wp_ref[:, hi:] = w_ref[:, lo:lo + wp_ref.shape[1] - hi]

    h = _rms(x_ref[...], g_ref[...]).astype(_MXU)
    for j in range(wp_ref.shape[1] // tn):
        cols = slice(j * tn, (j + 1) * tn)
        acc = _dot(h, wp_ref[:, cols])
        o_ref[:, cols] = (acc * scale_ref[:, cols]).astype(o_ref.dtype)
        if j * tn <= side_col < (j + 1) * tn:
            side_ref[...] = acc[:, side_col - j * tn:side_col - j * tn + side_ref.shape[1]]


def _norm_matmul(x, g, w, layer, col_scale, *, gap, side_col, side_width, tm=1024, tn=1024):
    m, d = x.shape
    n = w.shape[2]
    assert side_col % LANES == 0 and side_col // tn == (side_col + side_width - 1) // tn
    return pl.pallas_call(
        functools.partial(_norm_matmul_kernel, tn=tn, side_col=side_col, gap=gap),
        grid=(m // tm,),
        in_specs=[pl.BlockSpec((tm, d), lambda i: (i, 0)),
                  pl.BlockSpec((1, d), lambda i: (0, 0)),
                  pl.BlockSpec((None, d, w.shape[2]), lambda i: (layer, 0, 0), pipeline_mode=pl.Buffered(1)),
                  pl.BlockSpec((1, n), lambda i: (0, 0))],
        out_specs=[pl.BlockSpec((tm, n), lambda i: (i, 0)),
                   pl.BlockSpec((tm, side_width), lambda i: (i, 0))],
        out_shape=[jax.ShapeDtypeStruct((m, n), _MXU),
                   jax.ShapeDtypeStruct((m, side_width), jnp.float32)],
        scratch_shapes=[pltpu.VMEM((d, n), _MXU)],
        compiler_params=_params("arbitrary"),
        name="norm_in_proj",
    )(x, g, w, col_scale)


def _out_mlp_kernel(x_ref, a_ref, b_ref, c_ref, d_ref, wo_ref, g_ref, wu_ref, wd_ref, gf_ref, o_ref,
                    *, final_norm, tf):
    mixed = jnp.concatenate([a_ref[...], b_ref[...], c_ref[...], d_ref[...]], axis=1)
    y = x_ref[...] + _dot(mixed, wo_ref[...])
    h = _rms(y, g_ref[...]).astype(_MXU)
    for c in range(wu_ref.shape[1] // tf):
        u = jnp.square(jnp.maximum(_dot(h, wu_ref[:, c * tf:(c + 1) * tf]), 0.0))
        y = y + _dot(u.astype(_MXU), wd_ref[c * tf:(c + 1) * tf, :])
    if final_norm:
        y = _rms(y, gf_ref[...])
    o_ref[...] = y


def _out_mlp(x, groups, w_out, g, w_up, w_down, layer, g_final, *, final_norm, tm=1024, tf=1024):
    m, d = x.shape
    f = w_up.shape[2]
    gspec = pl.BlockSpec((tm, GROUP_WIDTH), lambda i: (i, 0))
    row = pl.BlockSpec((1, d), lambda i: (0, 0))

    def resident(rows, cols):
        return pl.BlockSpec((None, rows, cols), lambda i: (layer, 0, 0), pipeline_mode=pl.Buffered(1))

    return pl.pallas_call(
        functools.partial(_out_mlp_kernel, final_norm=final_norm, tf=tf),
        grid=(m // tm,),
        in_specs=[pl.BlockSpec((tm, d), lambda i: (i, 0)), gspec, gspec, gspec, gspec,
                  resident(d, d), row, resident(d, f), resident(f, d), row],
        out_specs=pl.BlockSpec((tm, d), lambda i: (i, 0)),
        out_shape=jax.ShapeDtypeStruct((m, d), jnp.float32),
        compiler_params=_params("parallel"),
        name="out_proj_mlp_residual",
    )(x, *groups, w_out, g, w_up, w_down, g_final)


def _t5_bucket(dist):
    n = jnp.maximum(dist, 0)
    max_exact = N_BUCKETS // 2
    nf = jnp.maximum(n, 1).astype(jnp.float32)
    large = max_exact + (jnp.log(nf / max_exact) / math.log(MAX_DISTANCE / max_exact)
                         * (N_BUCKETS - max_exact)).astype(jnp.int32)
    large = jnp.minimum(large, N_BUCKETS - 1)
    return jnp.where(n < max_exact, n, large)


def _bias_kernel(tab_ref, o_ref, *, row_stride, col_stride, offset, head0):
    nh, tr, tc = o_ref.shape
    for blk in range(tc // LANES):
        rows = lax.broadcasted_iota(jnp.int32, (tr, LANES), 0)
        cols = lax.broadcasted_iota(jnp.int32, (tr, LANES), 1) + (pl.program_id(0) * tc + blk * LANES)
        bucket = _t5_bucket(rows * row_stride + cols * col_stride + offset)
        for h in range(nh):
            row = tab_ref[head0 + h:head0 + h + 1, :]
            row = (row - row[:, N_BUCKETS - 1:N_BUCKETS]) * LOG2E
            o_ref[h, :, blk * LANES:(blk + 1) * LANES] = jnp.take_along_axis(
                jnp.broadcast_to(row, (tr, LANES)), bucket, axis=1, mode="promise_in_bounds")


def _bias_table(table_t, *, n_heads, head0, rows, cols, col_tile, row_stride, col_stride, offset):
    return pl.pallas_call(
        functools.partial(_bias_kernel, row_stride=row_stride, col_stride=col_stride,
                          offset=offset, head0=head0),
        grid=(cols // col_tile,),
        in_specs=[pl.BlockSpec(table_t.shape, lambda i: (0, 0))],
        out_specs=pl.BlockSpec((n_heads, rows, col_tile), lambda i: (0, 0, i)),
        out_shape=jax.ShapeDtypeStruct((n_heads, rows, cols), jnp.float32),
        compiler_params=_params("parallel"),
        name="t5_bias_tiles",
    )(table_t)


def _softmax_init(t):
    return (jnp.full((1, t), NEG_INF, jnp.float32), jnp.zeros((ACC_ROWS, t), jnp.float32))


class _KeyTile(NamedTuple):
    kts: Sequence[Any]
    vts: Sequence[Any]
    biases: Optional[Sequence[Any]] = None
    emasks: Optional[Sequence[Any]] = None
    qmasks: Optional[Sequence[Any]] = None


def _round_robin(lists):
    out = []
    for rank in range(max(map(len, lists), default=0)):
        out.extend(items[rank] for items in lists if rank < len(items))
    return out


def _softmax_jobs(t, jobs):
    built = {}

    def tiles_of(job, g):
        if (job, g) not in built:
            built[job, g] = [make() for make in jobs[job][2][g]]
        return built[job, g]

    def scores_of(job, g, c):
        qts = jobs[job][1]
        row = []
        for tile in tiles_of(job, g):
            s = _dot(tile.kts[c], qts[c])
            if tile.biases is not None:
                s = s + tile.biases[c]
            if tile.emasks is not None:
                s = jnp.where(tile.emasks[c], s, NEG_INF)
            row.append(s.astype(_MXU))
        return row

    def update(state, job, g, c, scores):
        m, acc = state
        m_new = m
        for tile, s in zip(tiles_of(job, g), scores):
            tile_max = jnp.max(s, axis=0, keepdims=True).astype(jnp.float32)
            if tile.qmasks is not None:
                tile_max = jnp.where(tile.qmasks[c], tile_max, NEG_INF)
            m_new = jnp.maximum(m_new, tile_max)
        seen = m_new > 0.5 * NEG_INF
        acc = jnp.exp2(m - m_new) * acc
        for tile, s in zip(tiles_of(job, g), scores):
            ok = seen if tile.qmasks is None else jnp.logical_and(seen, tile.qmasks[c])
            acc = acc + _dot(tile.vts[c], jnp.exp2(s - jnp.where(ok, m_new, BIG).astype(_MXU)))
        return m_new, acc

    units = _round_robin([[(job, g, c) for g in range(len(groups)) for c in range(n)]
                          for job, (n, _, groups) in enumerate(jobs)])
    lookahead = max(LOOKAHEAD, len(jobs) // 2)
    states = [[_softmax_init(t) for _ in range(n)] for n, _, _ in jobs]
    pending = {k: scores_of(*units[k]) for k in range(min(lookahead, len(units)))}
    for k, (job, g, c) in enumerate(units):
        if k + lookahead < len(units):
            pending[k + lookahead] = scores_of(*units[k + lookahead])
        states[job][c] = update(states[job][c], job, g, c, pending.pop(k))
        yield
    return states


def _pairs(items):
    return [items[p:p + 2] for p in range(0, len(items), 2)]


class _Mixer(NamedTuple):
    emit: Callable[..., Any]
    in_specs: Sequence[Any]
    operands: Sequence[Any]
    scratch: Sequence[Any]


def _mixer_kernel(*refs, emit):
    for _ in emit(*refs):
        pass


def _run_mixer(mixer, bsz, s, name):
    return pl.pallas_call(
        functools.partial(_mixer_kernel, emit=mixer.emit),
        grid=(bsz,),
        in_specs=list(mixer.in_specs),
        out_specs=pl.BlockSpec((1, s, GROUP_WIDTH), lambda b: (b, 0, 0)),
        out_shape=jax.ShapeDtypeStruct((bsz, s, GROUP_WIDTH), _MXU),
        scratch_shapes=list(mixer.scratch),
        compiler_params=_params("parallel"),
        name=name,
    )(*mixer.operands)


def _softmax_out(state):
    acc = state[1]
    return acc[:HEAD_DIM] / jnp.maximum(acc[HEAD_DIM:HEAD_DIM + 1], TINY)


def _top_k_rows(score, row_f, k):
    sel = jnp.zeros(score.shape, jnp.float32)
    for _ in range(k):
        mx = jnp.max(score, axis=0, keepdims=True)
        idx = jnp.min(jnp.where(score == mx, row_f, float(score.shape[0])), axis=0, keepdims=True)
        pick = row_f == idx
        sel = jnp.where(pick, 1.0, sel)
        score = jnp.where(pick, PICKED, score)
    return sel


def _now(fn):
    fn()


def _tile_iotas(t):
    return (lax.broadcasted_iota(jnp.int32, (t, t), 0), lax.broadcasted_iota(jnp.int32, (t, t), 1))


def _key_rows(ref, j, t):
    if isinstance(j, int):
        return ref[j * t:(j + 1) * t, :]
    return ref[pl.ds(pl.multiple_of(j * t, t), t), :]


def _transposed(ref_block):
    return ref_block.astype(jnp.float32).T


def _queries_t(q_ref, width, t):
    qt = _transposed(q_ref[0]).astype(_MXU)
    return [[qt[c * width:(c + 1) * width, r * t:(r + 1) * t] for c in range(GROUP_WIDTH // width)]
            for r in range(Q_TILES)]


def _fill_values_t(vt_ref, v_ref, col0, t):
    n_tiles, rows, _ = vt_ref.shape
    lane_block = (col0 // LANES) * LANES
    for c in range(n_tiles):
        blk = _transposed(v_ref[0, c * t:(c + 1) * t, lane_block:lane_block + LANES])
        vt_ref[c, 0:HEAD_DIM, :] = blk[col0 - lane_block:col0 - lane_block + HEAD_DIM].astype(vt_ref.dtype)
        if rows == ACC_ROWS:
            first = lax.broadcasted_iota(jnp.int32, (rows - HEAD_DIM, t), 0) == 0
            vt_ref[c, HEAD_DIM:rows, :] = jnp.where(first, 1.0, 0.0).astype(vt_ref.dtype)


def _group_specs(s, *column_blocks):
    return [pl.BlockSpec((1, s, GROUP_WIDTH), functools.partial(lambda cb, b: (b, 0, cb), cb))
            for cb in column_blocks]


def _bias_specs(t, head_group):
    spec = pl.BlockSpec((GROUP_HEADS, t, t), lambda b: (head_group, 0, 0))
    return [spec, spec]


def _store_heads(o_ref, outs_t):
    tiles = [jnp.concatenate(heads, axis=0) for heads in outs_t]
    o_ref[0] = jnp.concatenate(tiles, axis=1).T.astype(o_ref.dtype)


def _sb_emit(q_ref, k_ref, v_ref, o_ref, kb_ref, vt_ref):
    t = q_ref.shape[1] // Q_TILES
    key, qry = _tile_iotas(t)
    strict = key < qry
    later = jnp.where(qry > key, 1.0, 0.0).astype(_MXU)

    @_now
    def _():
        for h in _HEADS:
            kb_ref[h] = k_ref[0, :, h * HEAD_DIM:(h + 1) * HEAD_DIM].astype(kb_ref.dtype)
            _fill_values_t(vt_ref.at[h], v_ref, h * HEAD_DIM, t)

    def query_step(step):
        qts = _queries_t(q_ref, HEAD_DIM, t)
        units = _round_robin([[(r, step * Q_TILES + r, pair, h)
                               for pair in _pairs(list(range(step * Q_TILES + r, -1, -1))) for h in _HEADS]
                              for r in range(Q_TILES)])
        zero = (jnp.zeros((HEAD_DIM, t), jnp.float32), jnp.zeros((1, t), jnp.float32))
        carry = [[zero] * GROUP_HEADS for _ in range(Q_TILES)]
        zs, log_keeps, suffixes = {}, {}, {}

        def scores(u):
            r, _, pair, h = units[u]
            zs[u] = [_dot(_key_rows(kb_ref.at[h], j, t), qts[r][h]) for j in pair]

        def keeps(u):
            _, k, pair, _ = units[u]
            log_keeps[u], suffixes[u] = [], []
            for j, z in zip(pair, zs[u]):
                drop = jnp.maximum(jnp.log2(1.0 + jnp.exp2(jnp.minimum(z, SOFTPLUS_CLAMP))), z)
                if j == k:
                    drop = jnp.where(strict, drop, 0.0)
                log_keeps[u].append(drop)
                suffixes[u].append(_dot(later, drop.astype(_MXU)))

        def values(u):
            r, k, pair, h = units[u]
            acc, run = carry[r][h]
            weights = []
            for j, z, drop, suffix in zip(pair, zs.pop(u), log_keeps.pop(u), suffixes.pop(u)):
                a = jnp.exp2(z - drop - suffix + run)
                if j == k:
                    a = jnp.where(strict, a, 0.0)
                weights.append(a.astype(_MXU))
                run = run - (suffix[0:1] + drop[0:1])
            for j, w in zip(pair, weights):
                acc = acc + _dot(vt_ref[h, j], w)
            carry[r][h] = (acc, run)

        stages = (scores, keeps, values)
        for tick in range(len(units) + SB_STAGE_LAG * (len(stages) - 1)):
            for n, stage in enumerate(stages):
                if 0 <= tick - n * SB_STAGE_LAG < len(units):
                    stage(tick - n * SB_STAGE_LAG)
            yield
        _store_heads(o_ref, [[c[0] for c in tile_carry] for tile_carry in carry])

    yield from query_step(0)


def _stick_breaking(proj, t=TILE):
    s = proj.shape[1]
    scratch = [pltpu.VMEM((GROUP_HEADS, s, HEAD_DIM), _MXU), pltpu.VMEM((GROUP_HEADS, s // t, HEAD_DIM, t), _MXU)]
    return _Mixer(_sb_emit, _group_specs(s, CB_SB_Q, CB_SB_K, CB_SB_V), [proj] * 3, scratch)


def _moba_emit(q_ref, k_ref, v_ref, bd_ref, bs_ref, o_ref, kb_ref, vt_ref, km_ref):
    t = q_ref.shape[1] // Q_TILES
    n_blk = k_ref.shape[1] // MOBA_BLOCK
    tiles_per_blk = MOBA_BLOCK // t
    blk_shift = int(math.log2(tiles_per_blk))

    @_now
    def _():
        km_ref[...] = jnp.zeros_like(km_ref)
        for h in _HEADS:
            lo, hi = h * HEAD_DIM, (h + 1) * HEAD_DIM
            kb_ref[h] = k_ref[0, :, lo:hi].astype(kb_ref.dtype)
            _fill_values_t(vt_ref.at[h], v_ref, lo, t)
            for n in range(n_blk):
                blk = k_ref[0, n * MOBA_BLOCK:(n + 1) * MOBA_BLOCK, lo:hi]
                km_ref[h, n:n + 1, :] = jnp.mean(blk.astype(jnp.float32), axis=0, keepdims=True)

    def query_step(step):
        key, qry = _tile_iotas(t)
        causal = key <= qry
        blk_row = lax.broadcasted_iota(jnp.int32, (km_ref.shape[1], t), 0)
        qt = _transposed(q_ref[0])
        jobs = []
        for r in range(Q_TILES):
            k = step * Q_TILES + r
            own = k >> blk_shift
            qts, sels = [], []
            for h in _HEADS:
                qf = qt[h * HEAD_DIM:(h + 1) * HEAD_DIM, r * t:(r + 1) * t]
                qts.append(qf.astype(_MXU))
                gate = _dot(km_ref[h], qf, precision=lax.Precision.HIGHEST)
                gate = jnp.where(blk_row < own, gate, NEG_INF)
                sel = _top_k_rows(gate, blk_row.astype(jnp.float32), min(MOBA_TOPK, n_blk - 1))
                sels.append(jnp.where(blk_row < own, sel, 0.0))

            def tile(own, sels, j, bias_ref=None, emask=None):
                n = j >> blk_shift
                return _KeyTile([_key_rows(kb_ref.at[h], j, t) for h in _HEADS], [vt_ref[h, j] for h in _HEADS],
                                None if bias_ref is None else [bias_ref[h] for h in _HEADS],
                                None if emask is None else [emask] * GROUP_HEADS,
                                None if n == own else [sels[h][n:n + 1] > 0.5 for h in _HEADS])

            tile = functools.partial(tile, own, sels)
            groups = _pairs([functools.partial(tile, j) for j in range(k - 1)])
            groups.append(([functools.partial(tile, k - 1, bs_ref)] if k else [])
                          + [functools.partial(tile, k, bd_ref, causal)])
            jobs.append((GROUP_HEADS, qts, groups))
        all_states = yield from _softmax_jobs(t, jobs)
        _store_heads(o_ref, [[_softmax_out(st) for st in states] for states in all_states])

    yield from query_step(0)


def _kv_scratch(s, t, key_dim=HEAD_DIM, n_keys=GROUP_HEADS):
    return [pltpu.VMEM((n_keys, s, key_dim), _MXU), pltpu.VMEM((GROUP_HEADS, s // t, ACC_ROWS, t), _MXU)]


def _moba(proj, bias_diag, bias_sub, t=TILE):
    s = proj.shape[1]
    n_blk_pad = -(-(s // MOBA_BLOCK) // SUBLANES) * SUBLANES
    scratch = _kv_scratch(s, t) + [pltpu.VMEM((GROUP_HEADS, n_blk_pad, HEAD_DIM), jnp.float32)]
    return _Mixer(_moba_emit, _group_specs(s, CB_MB_Q, CB_MB_K, CB_MB_V) + _bias_specs(t, 0),
                  [proj] * 3 + [bias_diag, bias_sub], scratch)


def _diff_emit(lam_ref, g_ref, q_ref, k_ref, v_ref, bd_ref, bs_ref, o_ref, kb_ref, vt_ref, *, lambda_init):
    t = q_ref.shape[1] // Q_TILES
    key, qry = _tile_iotas(t)
    causal = key <= qry
    lv = lam_ref[...]
    lam = (jnp.exp(jnp.sum(lv[0:1] * lv[1:2], keepdims=True))
           - jnp.exp(jnp.sum(lv[2:3] * lv[3:4], keepdims=True)) + lambda_init)
    halves = range(2 * GROUP_HEADS)

    @_now
    def _():
        for c in halves:
            kb_ref[c] = k_ref[0, :, c * DIFF_HALF:(c + 1) * DIFF_HALF].astype(kb_ref.dtype)
        for h in _HEADS:
            _fill_values_t(vt_ref.at[h], v_ref, h * HEAD_DIM, t)

    def tile(j, bias_ref=None, emask=None):
        n = len(halves)
        return _KeyTile([_key_rows(kb_ref.at[c], j, t) for c in halves], [vt_ref[c // 2, j] for c in halves],
                        None if bias_ref is None else [bias_ref[c // 2] for c in halves],
                        None if emask is None else [emask] * n)

    def query_step(step):
        qts = _queries_t(q_ref, DIFF_HALF, t)
        jobs = []
        for r in range(Q_TILES):
            k = step * Q_TILES + r
            groups = _pairs([functools.partial(tile, j) for j in range(k - 1)])
            groups.append(([functools.partial(tile, k - 1, bs_ref)] if k else [])
                          + [functools.partial(tile, k, bd_ref, causal)])
            jobs.append((len(halves), qts[r], groups))
        all_states = yield from _softmax_jobs(t, jobs)
        outs = []
        for states in all_states:
            heads = []
            for h in _HEADS:
                o = _softmax_out(states[2 * h]) - lam * _softmax_out(states[2 * h + 1])
                o = o * lax.rsqrt(jnp.mean(o * o, axis=0, keepdims=True) + NORM_EPS) * g_ref[...]
                heads.append(o * (1.0 - lambda_init))
            outs.append(heads)
        _store_heads(o_ref, outs)

    yield from query_step(0)


def _diff(proj, lam_params, subln, bias_diag, bias_sub, lambda_init, t=TILE):
    s = proj.shape[1]
    in_specs = ([pl.BlockSpec(lam_params.shape, lambda b: (0, 0)), pl.BlockSpec(subln.shape, lambda b: (0, 0))]
                + _group_specs(s, CB_DF_Q, CB_DF_K, CB_DF_V) + _bias_specs(t, 2))
    return _Mixer(functools.partial(_diff_emit, lambda_init=lambda_init), in_specs,
                  [lam_params, subln] + [proj] * 3 + [bias_diag, bias_sub],
                  _kv_scratch(s, t, key_dim=DIFF_HALF, n_keys=2 * GROUP_HEADS))


def _compress_blocks(kcv_ref, pk_ref, pv_ref, wk1_ref, wk2_ref, wv1_ref, wv2t_ref):
    n_chunk = kcv_ref.shape[1] // CMP_STRIDE

    branches = ((0, pk_ref, wk1_ref), (HEAD_DIM, pv_ref, wv1_ref))
    tops = [jnp.zeros((n_chunk, w1_ref.shape[1]), jnp.float32) for _, _, w1_ref in branches]
    bots = list(tops)
    pack = 2 * LANES // HEAD_DIM
    for l0 in range(0, CMP_STRIDE, pack):
        tokens = [kcv_ref[0, pl.ds(l, n_chunk, stride=CMP_STRIDE), :] for l in range(l0, l0 + pack)]
        for n, (col0, p_ref, w1_ref) in enumerate(branches):
            for half, acc in ((0, tops), (CMP_STRIDE, bots)):
                x = jnp.concatenate([(tok[:, col0:col0 + HEAD_DIM] + p_ref[half + l:half + l + 1, :]).astype(_MXU)
                                     for l, tok in zip(range(l0, l0 + pack), tokens)], axis=1)
                acc[n] = acc[n] + _dot(x, w1_ref[(half + l0) * HEAD_DIM:(half + l0 + pack) * HEAD_DIM, :])
    hidden = [jax.nn.gelu(top + pltpu.roll(bot, n_chunk - 1, axis=0)).astype(_MXU) for top, bot in zip(tops, bots)]
    return _dot(hidden[0], wk2_ref[...]), _dot_nt(wv2t_ref[...], hidden[1])


def _nsa_emit(q_ref, kva_ref, kvb_ref, kcv_ref, pk_ref, pv_ref, wk1_ref, wk2_ref, wv1_ref, wv2t_ref,
              bd_ref, bs_ref, bc_ref, cover_ref, e_ref, o_ref, ks_ref, vst_ref, kw_ref, vwt_ref):
    t = q_ref.shape[1] // Q_TILES
    key, qry = _tile_iotas(t)
    causal = key <= qry
    ks_col, vs_col, kw_col, vw_col, gate_col = 2 * HEAD_DIM, 3 * HEAD_DIM, 0, HEAD_DIM, 2 * HEAD_DIM

    @_now
    def _():
        ks_ref[:, :HEAD_DIM] = kva_ref[0, :, ks_col:ks_col + HEAD_DIM].astype(ks_ref.dtype)
        ks_ref[:, HEAD_DIM:] = e_ref[...]
        kw_ref[...] = kvb_ref[0, :, kw_col:kw_col + HEAD_DIM].astype(kw_ref.dtype)
        _fill_values_t(vst_ref, kva_ref, vs_col, t)
        _fill_values_t(vwt_ref, kvb_ref, vw_col, t)

    def tile(k_ref, vt_ref, j, bias_ref=None, emask=None):
        n = GROUP_HEADS
        return _KeyTile([_key_rows(k_ref, j, t)] * n, [vt_ref[j]] * n,
                        None if bias_ref is None else [bias_ref[h] for h in _HEADS],
                        None if emask is None else [emask] * n)

    def query_step(step):
        qts = _queries_t(q_ref, HEAD_DIM, t)
        kc, vct = (a.astype(_MXU) for a in _compress_blocks(kcv_ref, pk_ref, pv_ref, wk1_ref, wk2_ref,
                                                            wv1_ref, wv2t_ref))
        n_cmp = kc.shape[0]
        n_slc = cover_ref.shape[0]
        c_row = lax.broadcasted_iota(jnp.int32, (n_cmp, t), 0)
        c_col = lax.broadcasted_iota(jnp.int32, (n_cmp, t), 1)
        s_row = lax.broadcasted_iota(jnp.int32, (n_slc, t), 0)
        s_col = lax.broadcasted_iota(jnp.int32, (n_slc, t), 1)
        tiles = [step * Q_TILES + r for r in range(Q_TILES)]

        o_cmp, importance = [], []
        for r, k in enumerate(tiles):
            visible = c_col + k * t >= c_row * CMP_STRIDE + (CMP_LEN - 1)
            cmp_scores = [_dot(kc, qts[r][h]) for h in _HEADS]
            cmp_probs = []
            p_sum = jnp.zeros((n_cmp, t), jnp.float32)
            for h in _HEADS:
                sc = jnp.where(visible, cmp_scores[h] + bc_ref[h, :, r * t:(r + 1) * t], NEG_INF)
                e = jnp.where(visible, jnp.exp2(sc - jnp.max(sc, axis=0, keepdims=True)), 0.0)
                p = e / jnp.maximum(jnp.sum(e, axis=0, keepdims=True), TINY)
                cmp_probs.append(p.astype(_MXU))
                p_sum = p_sum + p
            o_cmp.append([_dot(vct, cmp_probs[h]) for h in _HEADS])
            importance.append(_dot(cover_ref[...], p_sum, precision=lax.Precision.HIGHEST))

        n_back = WINDOW // t
        jobs = []
        for r, k in enumerate(tiles):
            window = [functools.partial(tile, kw_ref, vwt_ref, k - n_back, None, qry < key)] if k >= n_back else []
            for back in range(min(n_back - 1, k), 0, -1):
                window.append(functools.partial(tile, kw_ref, vwt_ref, k - back, bs_ref if back == 1 else None))
            window.append(functools.partial(tile, kw_ref, vwt_ref, k, bd_ref, causal))
            jobs.append((GROUP_HEADS, qts[r], [window]))
        win_states = yield from _softmax_jobs(t, jobs)
        o_win = [[_softmax_out(st) for st in states] for states in win_states]

        jobs = []
        for r, k in enumerate(tiles):
            own = jnp.right_shift(s_col + k * t, int(math.log2(SLC_LEN)))
            score = jnp.where(s_row == own, FORCE, jnp.where(s_row < own, importance[r], NEG_INF))
            sel = _top_k_rows(score, s_row.astype(jnp.float32), min(SLC_TOPN, n_slc))
            penalty = jnp.where(sel > 0.5, 0.0, NEG_INF).astype(_MXU)
            q_aug = [jnp.concatenate([qts[r][h], penalty], axis=0) for h in _HEADS]
            groups = _pairs([functools.partial(tile, ks_ref, vst_ref, j) for j in range(k - 1)])
            groups.append(([functools.partial(tile, ks_ref, vst_ref, k - 1, bs_ref)] if k else [])
                          + [functools.partial(tile, ks_ref, vst_ref, k, bd_ref, causal)])
            jobs.append((GROUP_HEADS, q_aug, groups))
        slc_states = yield from _softmax_jobs(t, jobs)
        o_slc = [[_softmax_out(st) for st in states] for states in slc_states]

        gates = _transposed(kvb_ref[0, :, (gate_col // LANES) * LANES:(gate_col // LANES + 1) * LANES])
        gates = 1.0 / (1.0 + jnp.exp(-gates[gate_col % LANES:gate_col % LANES + N_GATES + 4]))
        outs = []
        for r in range(Q_TILES):
            heads = []
            for h in _HEADS:
                g = [gates[br * GROUP_HEADS + h:br * GROUP_HEADS + h + 1, r * t:(r + 1) * t] for br in range(3)]
                heads.append(g[0] * o_cmp[r][h] + g[1] * o_slc[r][h] + g[2] * o_win[r][h])
            outs.append(heads)
        _store_heads(o_ref, outs)

    yield from query_step(0)


def _nsa(proj, kcv, cmp_params, layer, bias_diag, bias_sub, bias_cmp, cover_t, expand, t=TILE):
    s = proj.shape[1]
    values_t = pltpu.VMEM((s // t, ACC_ROWS, t), _MXU)
    in_specs = (_group_specs(s, CB_NS_Q, CB_NS_A, CB_NS_B)
                + [pl.BlockSpec((1, s, kcv.shape[2]), lambda b: (b, 0, 0))]
                + [pl.BlockSpec((None,) + a.shape[1:], lambda b: (layer, 0, 0)) for a in cmp_params]
                + _bias_specs(t, 1)
                + [pl.BlockSpec(bias_cmp.shape, lambda b: (0, 0, 0)),
                   pl.BlockSpec(cover_t.shape, lambda b: (0, 0)),
                   pl.BlockSpec(expand.shape, lambda b: (0, 0))])
    scratch = [pltpu.VMEM((s, HEAD_DIM + expand.shape[1]), _MXU), values_t, pltpu.VMEM((s, HEAD_DIM), _MXU), values_t]
    return _Mixer(_nsa_emit, in_specs,
                  [proj] * 3 + [kcv, *cmp_params, bias_diag, bias_sub, bias_cmp, cover_t, expand], scratch)


def _nsa_constants(s):
    n_cmp = (s - CMP_LEN) // CMP_STRIDE + 1
    n_slc = s // SLC_LEN
    assert n_cmp + 1 == s // CMP_STRIDE and n_slc % SUBLANES == 0
    c_start = np.arange(n_cmp) * CMP_STRIDE
    s_start = np.arange(n_slc) * SLC_LEN
    cover = np.clip(np.minimum((c_start + CMP_LEN - 1)[:, None], (s_start + SLC_LEN - 1)[None, :])
                    - np.maximum(c_start[:, None], s_start[None, :]) + 1, 0, None) / CMP_LEN
    cover_t = np.zeros((n_slc, n_cmp + 1), np.float32)
    cover_t[:, :n_cmp] = cover.T
    expand = (np.arange(s)[:, None] // SLC_LEN == np.arange(n_slc)[None, :]).astype(np.float32)
    return jnp.asarray(cover_t), jnp.asarray(expand, _MXU)


def kernel(x, w_in, w_out, w_up, w_down, norm_attn, norm_mlp, cmp_pos_k, cmp_pos_v, cmp_k_w1, cmp_k_w2,
           cmp_v_w1, cmp_v_w2, diff_lambda, diff_subln, rel_bias, final_norm):
    bsz, s, d = x.shape
    depth = w_in.shape[0]
    t = TILE
    n_chunk = s // CMP_STRIDE
    assert s == Q_TILES * t and MOBA_BLOCK % t == 0 and WINDOW % t == 0 and t >= MAX_DISTANCE

    assert w_in.shape[2] + PAD_COLS == D_IN_PAD
    w_in_c = jnp.pad(w_in.astype(_MXU), ((0, 0), (0, 0), (0, PAD_COLS)))
    w_out_c, w_up_c, w_down_c = (w.astype(_MXU) for w in (w_out, w_up, w_down))
    wk1, wk2 = cmp_k_w1.astype(_MXU), cmp_k_w2.astype(_MXU)
    wv1, wv2t = cmp_v_w1.astype(_MXU), jnp.swapaxes(cmp_v_w2, 1, 2).astype(_MXU)

    table_t = jnp.pad(rel_bias.T, ((0, 0), (0, LANES - N_BUCKETS)))
    tiles = dict(n_heads=rel_bias.shape[1], head0=0, rows=t, cols=t, col_tile=t, row_stride=-1, col_stride=1)
    bias_diag = _bias_table(table_t, offset=0, **tiles)
    bias_sub = _bias_table(table_t, offset=t, **tiles)
    bias_cmp = _bias_table(table_t, n_heads=GROUP_HEADS, head0=GROUP_HEADS, rows=n_chunk, cols=s,
                           col_tile=t, row_stride=-CMP_STRIDE, col_stride=1, offset=-(CMP_LEN - 1))
    cover_t, expand = _nsa_constants(s)
    col_scale = np.ones((1, D_IN_PAD), np.float32)
    for cb, width in ((CB_SB_Q, HEAD_DIM), (CB_MB_Q, HEAD_DIM), (CB_NS_Q, HEAD_DIM), (CB_DF_Q, DIFF_HALF)):
        col_scale[:, cb * GROUP_WIDTH:(cb + 1) * GROUP_WIDTH] = width ** -0.5 * LOG2E
    col_scale = jnp.asarray(col_scale)

    x2 = x.reshape(bsz * s, d)
    for layer in range(depth):
        proj, kcv = _norm_matmul(x2, norm_attn[layer][None], w_in_c, layer, col_scale,
                                 gap=(COLS_BEFORE_PAD, COLS_BEFORE_PAD + PAD_COLS),
                                 side_col=CB_NS_A * GROUP_WIDTH, side_width=2 * HEAD_DIM)
        proj = proj.reshape(bsz, s, D_IN_PAD)
        lambda_init = 0.8 - 0.6 * math.exp(-0.3 * layer)
        o_sb = _run_mixer(_stick_breaking(proj), bsz, s, "stick_breaking")
        o_mb = _run_mixer(_moba(proj, bias_diag, bias_sub), bsz, s, "moba")
        o_ns = _run_mixer(_nsa(proj, kcv.reshape(bsz, s, 2 * HEAD_DIM), (cmp_pos_k, cmp_pos_v, wk1, wk2, wv1, wv2t),
                               layer, bias_diag, bias_sub, bias_cmp, cover_t, expand), bsz, s, "nsa")
        o_df = _run_mixer(_diff(proj, diff_lambda[layer], diff_subln[layer][:, None], bias_diag, bias_sub,
                                lambda_init), bsz, s, "diff_attention")
        groups = [o.reshape(bsz * s, GROUP_WIDTH) for o in (o_sb, o_mb, o_ns, o_df)]
        x2 = _out_mlp(x2, groups, w_out_c, norm_mlp[layer][None], w_up_c, w_down_c, layer,
                      final_norm[None], final_norm=(layer == depth - 1))
    return x2.reshape(bsz, s, d)
```

```python
import functools
import math
from typing import Any, Callable, NamedTuple, Optional, Sequence

import numpy as np
import jax
import jax.numpy as jnp
from jax import lax
from jax.experimental import pallas as pl
from jax.experimental.pallas import tpu as pltpu

HEAD_DIM = 64
GROUP_HEADS = 4
GROUP_WIDTH = GROUP_HEADS * HEAD_DIM
NORM_EPS = 1e-6
NEG_INF = -1e30
BIG = 1e30
FORCE = 1e30
TINY = 1e-30
SOFTPLUS_CLAMP = 64.0
PICKED = -3e38
LOG2E = math.log2(math.e)
N_BUCKETS = 32
MAX_DISTANCE = 128
MOBA_BLOCK = 256
MOBA_TOPK = 3
CMP_LEN = 32
CMP_STRIDE = 16
SLC_LEN = 64
SLC_TOPN = 4
WINDOW = 512
DIFF_HALF = HEAD_DIM // 2
LANES = 128
SUBLANES = 8
TILE = 256
BF16_ROWS = 16
ACC_ROWS = HEAD_DIM + BF16_ROWS
N_GATES = 3 * GROUP_HEADS
COLS_BEFORE_PAD = 9 * GROUP_WIDTH - 2 * HEAD_DIM + N_GATES
PAD_COLS = 2 * HEAD_DIM - N_GATES
CB_SB_Q, CB_SB_K, CB_SB_V, CB_MB_Q, CB_MB_K, CB_MB_V, CB_NS_Q, CB_NS_A, CB_NS_B, CB_DF_Q, CB_DF_K, CB_DF_V = range(12)
D_IN_PAD = 12 * GROUP_WIDTH

_MXU = jnp.bfloat16
_VMEM_LIMIT = 56 * 1024 * 1024
_HEADS = range(GROUP_HEADS)
Q_TILES = 8
LOOKAHEAD = 2


def _dot(a, b, precision=None):
    return jnp.dot(a, b, precision=precision, preferred_element_type=jnp.float32)


def _dot_nt(a, b, precision=None):
    return lax.dot_general(a, b, (((1,), (1,)), ((), ())), precision=precision,
                           preferred_element_type=jnp.float32)


def _rms(x, g):
    return x * lax.rsqrt(jnp.mean(x * x, axis=-1, keepdims=True) + NORM_EPS) * g


def _params(*sem):
    return pltpu.CompilerParams(dimension_semantics=sem, vmem_limit_bytes=_VMEM_LIMIT)


def _norm_matmul_kernel(x_ref, g_ref, w_ref, scale_ref, o_ref, side_ref, wp_ref, *, tn, side_col, gap):
    @pl.when(pl.program_id(0) == 0)
    def _():
        lo, hi = gap
        wp_ref[:, :lo] = w_ref[:, :lo]
        wp_ref[:, lo:hi] = jnp.zeros((wp_ref.shape[0], hi - lo), wp_ref.dtype)
        wp_ref[:, hi:] = w_ref[:, lo:lo + wp_ref.shape[1] - hi]

    h = _rms(x_ref[...], g_ref[...]).astype(_MXU)
    for j in range(wp_ref.shape[1] // tn):
        cols = slice(j * tn, (j + 1) * tn)
        acc = _dot(h, wp_ref[:, cols])
        o_ref[:, cols] = (acc * scale_ref[:, cols]).astype(o_ref.dtype)
        if j * tn <= side_col < (j + 1) * tn:
            side_ref[...] = acc[:, side_col - j * tn:side_col - j * tn + side_ref.shape[1]]


def _norm_matmul(x, g, w, layer, col_scale, *, gap, side_col, side_width, tm=1024, tn=1024):
    m, d = x.shape
    n = w.shape[2]
    assert side_col % LANES == 0 and side_col // tn == (side_col + side_width - 1) // tn
    return pl.pallas_call(
        functools.partial(_norm_matmul_kernel, tn=tn, side_col=side_col, gap=gap),
        grid=(m // tm,),
        in_specs=[pl.BlockSpec((tm, d), lambda i: (i, 0)),
                  pl.BlockSpec((1, d), lambda i: (0, 0)),
                  pl.BlockSpec((None, d, w.shape[2]), lambda i: (layer, 0, 0), pipeline_mode=pl.Buffered(1)),
                  pl.BlockSpec((1, n), lambda i: (0, 0))],
        out_specs=[pl.BlockSpec((tm, n), lambda i: (i, 0)),
                   pl.BlockSpec((tm, side_width), lambda i: (i, 0))],
        out_shape=[jax.ShapeDtypeStruct((m, n), _MXU),
                   jax.ShapeDtypeStruct((m, side_width), jnp.float32)],
        scratch_shapes=[pltpu.VMEM((d, n), _MXU)],
        compiler_params=_params("arbitrary"),
        name="norm_in_proj",
    )(x, g, w, col_scale)


def _out_mlp_kernel(x_ref, a_ref, b_ref, c_ref, d_ref, wo_ref, g_ref, wu_ref, wd_ref, gf_ref, o_ref,
                    *, final_norm, tf):
    mixed = jnp.concatenate([a_ref[...], b_ref[...], c_ref[...], d_ref[...]], axis=1)
    y = x_ref[...] + _dot(mixed, wo_ref[...])
    h = _rms(y, g_ref[...]).astype(_MXU)
    for c in range(wu_ref.shape[1] // tf):
        u = jnp.square(jnp.maximum(_dot(h, wu_ref[:, c * tf:(c + 1) * tf]), 0.0))
        y = y + _dot(u.astype(_MXU), wd_ref[c * tf:(c + 1) * tf, :])
    if final_norm:
        y = _rms(y, gf_ref[...])
    o_ref[...] = y


def _out_mlp(x, groups, w_out, g, w_up, w_down, layer, g_final, *, final_norm, tm=1024, tf=1024):
    m, d = x.shape
    f = w_up.shape[2]
    gspec = pl.BlockSpec((tm, GROUP_WIDTH), lambda i: (i, 0))
    row = pl.BlockSpec((1, d), lambda i: (0, 0))

    def resident(rows, cols):
        return pl.BlockSpec((None, rows, cols), lambda i: (layer, 0, 0), pipeline_mode=pl.Buffered(1))

    return pl.pallas_call(
        functools.partial(_out_mlp_kernel, final_norm=final_norm, tf=tf),
        grid=(m // tm,),
        in_specs=[pl.BlockSpec((tm, d), lambda i: (i, 0)), gspec, gspec, gspec, gspec,
                  resident(d, d), row, resident(d, f), resident(f, d), row],
        out_specs=pl.BlockSpec((tm, d), lambda i: (i, 0)),
        out_shape=jax.ShapeDtypeStruct((m, d), jnp.float32),
        compiler_params=_params("parallel"),
        name="out_proj_mlp_residual",
    )(x, *groups, w_out, g, w_up, w_down, g_final)


def _t5_bucket(dist):
    n = jnp.maximum(dist, 0)
    max_exact = N_BUCKETS // 2
    nf = jnp.maximum(n, 1).astype(jnp.float32)
    large = max_exact + (jnp.log(nf / max_exact) / math.log(MAX_DISTANCE / max_exact)
                         * (N_BUCKETS - max_exact)).astype(jnp.int32)
    large = jnp.minimum(large, N_BUCKETS - 1)
    return jnp.where(n < max_exact, n, large)


def _bias_kernel(tab_ref, o_ref, *, row_stride, col_stride, offset, head0):
    nh, tr, tc = o_ref.shape
    for blk in range(tc // LANES):
        rows = lax.broadcasted_iota(jnp.int32, (tr, LANES), 0)
        cols = lax.broadcasted_iota(jnp.int32, (tr, LANES), 1) + (pl.program_id(0) * tc + blk * LANES)
        bucket = _t5_bucket(rows * row_stride + cols * col_stride + offset)
        for h in range(nh):
            row = tab_ref[head0 + h:head0 + h + 1, :]
            row = (row - row[:, N_BUCKETS - 1:N_BUCKETS]) * LOG2E
            o_ref[h, :, blk * LANES:(blk + 1) * LANES] = jnp.take_along_axis(
                jnp.broadcast_to(row, (tr, LANES)), bucket, axis=1, mode="promise_in_bounds")


def _bias_table(table_t, *, n_heads, head0, rows, cols, col_tile, row_stride, col_stride, offset):
    return pl.pallas_call(
        functools.partial(_bias_kernel, row_stride=row_stride, col_stride=col_stride,
                          offset=offset, head0=head0),
        grid=(cols // col_tile,),
        in_specs=[pl.BlockSpec(table_t.shape, lambda i: (0, 0))],
        out_specs=pl.BlockSpec((n_heads, rows, col_tile), lambda i: (0, 0, i)),
        out_shape=jax.ShapeDtypeStruct((n_heads, rows, cols), jnp.float32),
        compiler_params=_params("parallel"),
        name="t5_bias_tiles",
    )(table_t)


def _softmax_init(t):
    return (jnp.full((1, t), NEG_INF, jnp.float32), jnp.zeros((ACC_ROWS, t), jnp.float32))


class _KeyTile(NamedTuple):
    kts: Sequence[Any]
    vts: Sequence[Any]
    biases: Optional[Sequence[Any]] = None
    emasks: Optional[Sequence[Any]] = None
    qmasks: Optional[Sequence[Any]] = None


def _round_robin(lists):
    out = []
    for rank in range(max(map(len, lists), default=0)):
        out.extend(items[rank] for items in lists if rank < len(items))
    return out


def _softmax_jobs(t, jobs):
    built = {}

    def tiles_of(job, g):
        if (job, g) not in built:
            built[job, g] = [make() for make in jobs[job][2][g]]
        return built[job, g]

    def scores_of(job, g, c):
        qts = jobs[job][1]
        row = []
        for tile in tiles_of(job, g):
            s = _dot(tile.kts[c], qts[c])
            if tile.biases is not None:
                s = s + tile.biases[c]
            if tile.emasks is not None:
                s = jnp.where(tile.emasks[c], s, NEG_INF)
            row.append(s.astype(_MXU))
        return row

    def update(state, job, g, c, scores):
        m, acc = state
        m_new = m
        for tile, s in zip(tiles_of(job, g), scores):
            tile_max = jnp.max(s, axis=0, keepdims=True).astype(jnp.float32)
            if tile.qmasks is not None:
                tile_max = jnp.where(tile.qmasks[c], tile_max, NEG_INF)
            m_new = jnp.maximum(m_new, tile_max)
        seen = m_new > 0.5 * NEG_INF
        acc = jnp.exp2(m - m_new) * acc
        for tile, s in zip(tiles_of(job, g), scores):
            ok = seen if tile.qmasks is None else jnp.logical_and(seen, tile.qmasks[c])
            acc = acc + _dot(tile.vts[c], jnp.exp2(s - jnp.where(ok, m_new, BIG).astype(_MXU)))
        return m_new, acc

    units = _round_robin([[(job, g, c) for g in range(len(groups)) for c in range(n)]
                          for job, (n, _, groups) in enumerate(jobs)])
    lookahead = max(LOOKAHEAD, len(jobs) // 2)
    states = [[_softmax_init(t) for _ in range(n)] for n, _, _ in jobs]
    pending = {k: scores_of(*units[k]) for k in range(min(lookahead, len(units)))}
    for k, (job, g, c) in enumerate(units):
        if k + lookahead < len(units):
            pending[k + lookahead] = scores_of(*units[k + lookahead])
        states[job][c] = update(states[job][c], job, g, c, pending.pop(k))
        yield
    return states


def _pairs(items):
    return [items[p:p + 2] for p in range(0, len(items), 2)]


class _Mixer(NamedTuple):
    emit: Callable[..., Any]
    in_specs: Sequence[Any]
    operands: Sequence[Any]
    scratch: Sequence[Any]


def _mixer_kernel(*refs, emit):
    for _ in emit(*refs):
        pass


def _run_mixer(mixer, bsz, s, name):
    return pl.pallas_call(
        functools.partial(_mixer_kernel, emit=mixer.emit),
        grid=(bsz,),
        in_specs=list(mixer.in_specs),
        out_specs=pl.BlockSpec((1, s, GROUP_WIDTH), lambda b: (b, 0, 0)),
        out_shape=jax.ShapeDtypeStruct((bsz, s, GROUP_WIDTH), _MXU),
        scratch_shapes=list(mixer.scratch),
        compiler_params=_params("parallel"),
        name=name,
    )(*mixer.operands)


def _softmax_out(state):
    acc = state[1]
    return acc[:HEAD_DIM] / jnp.maximum(acc[HEAD_DIM:HEAD_DIM + 1], TINY)


def _top_k_rows(score, row_f, k):
    sel = jnp.zeros(score.shape, jnp.float32)
    for _ in range(k):
        mx = jnp.max(score, axis=0, keepdims=True)
        idx = jnp.min(jnp.where(score == mx, row_f, float(score.shape[0])), axis=0, keepdims=True)
        pick = row_f == idx
        sel = jnp.where(pick, 1.0, sel)
        score = jnp.where(pick, PICKED, score)
    return sel


def _now(fn):
    fn()


def _tile_iotas(t):
    return (lax.broadcasted_iota(jnp.int32, (t, t), 0), lax.broadcasted_iota(jnp.int32, (t, t), 1))


def _key_rows(ref, j, t):
    if isinstance(j, int):
        return ref[j * t:(j + 1) * t, :]
    return ref[pl.ds(pl.multiple_of(j * t, t), t), :]


def _transposed(ref_block):
    return ref_block.astype(jnp.float32).T


def _queries_t(q_ref, width, t):
    qt = _transposed(q_ref[0]).astype(_MXU)
    return [[qt[c * width:(c + 1) * width, r * t:(r + 1) * t] for c in range(GROUP_WIDTH // width)]
            for r in range(Q_TILES)]


def _fill_values_t(vt_ref, v_ref, col0, t):
    n_tiles, rows, _ = vt_ref.shape
    lane_block = (col0 // LANES) * LANES
    for c in range(n_tiles):
        blk = _transposed(v_ref[0, c * t:(c + 1) * t, lane_block:lane_block + LANES])
        vt_ref[c, 0:HEAD_DIM, :] = blk[col0 - lane_block:col0 - lane_block + HEAD_DIM].astype(vt_ref.dtype)
        if rows == ACC_ROWS:
            first = lax.broadcasted_iota(jnp.int32, (rows - HEAD_DIM, t), 0) == 0
            vt_ref[c, HEAD_DIM:rows, :] = jnp.where(first, 1.0, 0.0).astype(vt_ref.dtype)


def _group_specs(s, *column_blocks):
    return [pl.BlockSpec((1, s, GROUP_WIDTH), functools.partial(lambda cb, b: (b, 0, cb), cb))
            for cb in column_blocks]


def _bias_specs(t, head_group):
    spec = pl.BlockSpec((GROUP_HEADS, t, t), lambda b: (head_group, 0, 0))
    return [spec, spec]


def _store_heads(o_ref, outs_t):
    tiles = [jnp.concatenate(heads, axis=0) for heads in outs_t]
    o_ref[0] = jnp.concatenate(tiles, axis=1).T.astype(o_ref.dtype)


def _sb_emit(q_ref, k_ref, v_ref, o_ref, kb_ref, vt_ref):
    t = q_ref.shape[1] // Q_TILES
    key, qry = _tile_iotas(t)
    strict = key < qry
    later = jnp.where(qry > key, 1.0, 0.0).astype(_MXU)

    @_now
    def _():
        for h in _HEADS:
            kb_ref[h] = k_ref[0, :, h * HEAD_DIM:(h + 1) * HEAD_DIM].astype(kb_ref.dtype)
            _fill_values_t(vt_ref.at[h], v_ref, h * HEAD_DIM, t)

    def query_step(step):
        qts = _queries_t(q_ref, HEAD_DIM, t)
        units = _round_robin([[(r, step * Q_TILES + r, pair, h)
                               for pair in ([j] for j in range(step * Q_TILES + r, -1, -1)) for h in _HEADS]
                              for r in range(Q_TILES)])
        zero = (jnp.zeros((HEAD_DIM, t), jnp.float32), jnp.zeros((1, t), jnp.float32))
        carry = [[zero] * GROUP_HEADS for _ in range(Q_TILES)]
        zs, log_keeps, suffixes = {}, {}, {}

        def scores(u):
            r, _, pair, h = units[u]
            zs[u] = [_dot(_key_rows(kb_ref.at[h], j, t), qts[r][h]) for j in pair]

        def keeps(u):
            _, k, pair, _ = units[u]
            log_keeps[u], suffixes[u] = [], []
            for j, z in zip(pair, zs[u]):
                drop = jnp.maximum(jnp.log2(1.0 + jnp.exp2(jnp.minimum(z, SOFTPLUS_CLAMP))), z)
                if j == k:
                    drop = jnp.where(strict, drop, 0.0)
                log_keeps[u].append(drop)
                suffixes[u].append(_dot(later, drop.astype(_MXU)))

        def values(u):
            r, k, pair, h = units[u]
            acc, run = carry[r][h]
            weights = []
            for j, z, drop, suffix in zip(pair, zs.pop(u), log_keeps.pop(u), suffixes.pop(u)):
                a = jnp.exp2(z - drop - suffix + run)
                if j == k:
                    a = jnp.where(strict, a, 0.0)
                weights.append(a.astype(_MXU))
                run = run - (suffix[0:1] + drop[0:1])
            for j, w in zip(pair, weights):
                acc = acc + _dot(vt_ref[h, j], w)
            carry[r][h] = (acc, run)

        stages = (scores, keeps, values)
        for tick in range(len(units) + len(stages) - 1):
            for lag, stage in enumerate(stages):
                if 0 <= tick - lag < len(units):
                    stage(tick - lag)
            yield
        _store_heads(o_ref, [[c[0] for c in tile_carry] for tile_carry in carry])

    yield from query_step(0)


def _stick_breaking(proj, t=TILE):
    s = proj.shape[1]
    scratch = [pltpu.VMEM((GROUP_HEADS, s, HEAD_DIM), _MXU), pltpu.VMEM((GROUP_HEADS, s // t, HEAD_DIM, t), _MXU)]
    return _Mixer(_sb_emit, _group_specs(s, CB_SB_Q, CB_SB_K, CB_SB_V), [proj] * 3, scratch)


def _moba_emit(q_ref, k_ref, v_ref, bd_ref, bs_ref, o_ref, kb_ref, vt_ref, km_ref):
    t = q_ref.shape[1] // Q_TILES
    n_blk = k_ref.shape[1] // MOBA_BLOCK
    tiles_per_blk = MOBA_BLOCK // t
    blk_shift = int(math.log2(tiles_per_blk))

    @_now
    def _():
        km_ref[...] = jnp.zeros_like(km_ref)
        for h in _HEADS:
            lo, hi = h * HEAD_DIM, (h + 1) * HEAD_DIM
            kb_ref[h] = k_ref[0, :, lo:hi].astype(kb_ref.dtype)
            _fill_values_t(vt_ref.at[h], v_ref, lo, t)
            for n in range(n_blk):
                blk = k_ref[0, n * MOBA_BLOCK:(n + 1) * MOBA_BLOCK, lo:hi]
                km_ref[h, n:n + 1, :] = jnp.mean(blk.astype(jnp.float32), axis=0, keepdims=True)

    def query_step(step):
        key, qry = _tile_iotas(t)
        causal = key <= qry
        blk_row = lax.broadcasted_iota(jnp.int32, (km_ref.shape[1], t), 0)
        qt = _transposed(q_ref[0])
        jobs = []
        for r in range(Q_TILES):
            k = step * Q_TILES + r
            own = k >> blk_shift
            qts, sels = [], []
            for h in _HEADS:
                qf = qt[h * HEAD_DIM:(h + 1) * HEAD_DIM, r * t:(r + 1) * t]
                qts.append(qf.astype(_MXU))
                gate = _dot(km_ref[h], qf, precision=lax.Precision.HIGHEST)
                gate = jnp.where(blk_row < own, gate, NEG_INF)
                sel = _top_k_rows(gate, blk_row.astype(jnp.float32), min(MOBA_TOPK, n_blk - 1))
                sels.append(jnp.where(blk_row < own, sel, 0.0))

            def tile(own, sels, j, bias_ref=None, emask=None):
                n = j >> blk_shift
                return _KeyTile([_key_rows(kb_ref.at[h], j, t) for h in _HEADS], [vt_ref[h, j] for h in _HEADS],
                                None if bias_ref is None else [bias_ref[h] for h in _HEADS],
                                None if emask is None else [emask] * GROUP_HEADS,
                                None if n == own else [sels[h][n:n + 1] > 0.5 for h in _HEADS])

            tile = functools.partial(tile, own, sels)
            groups = _pairs([functools.partial(tile, j) for j in range(k - 1)])
            groups.append(([functools.partial(tile, k - 1, bs_ref)] if k else [])
                          + [functools.partial(tile, k, bd_ref, causal)])
            jobs.append((GROUP_HEADS, qts, groups))
        all_states = yield from _softmax_jobs(t, jobs)
        _store_heads(o_ref, [[_softmax_out(st) for st in states] for states in all_states])

    yield from query_step(0)


def _kv_scratch(s, t, key_dim=HEAD_DIM, n_keys=GROUP_HEADS):
    return [pltpu.VMEM((n_keys, s, key_dim), _MXU), pltpu.VMEM((GROUP_HEADS, s // t, ACC_ROWS, t), _MXU)]


def _moba(proj, bias_diag, bias_sub, t=TILE):
    s = proj.shape[1]
    n_blk_pad = -(-(s // MOBA_BLOCK) // SUBLANES) * SUBLANES
    scratch = _kv_scratch(s, t) + [pltpu.VMEM((GROUP_HEADS, n_blk_pad, HEAD_DIM), jnp.float32)]
    return _Mixer(_moba_emit, _group_specs(s, CB_MB_Q, CB_MB_K, CB_MB_V) + _bias_specs(t, 0),
                  [proj] * 3 + [bias_diag, bias_sub], scratch)


def _diff_emit(lam_ref, g_ref, q_ref, k_ref, v_ref, bd_ref, bs_ref, o_ref, kb_ref, vt_ref, *, lambda_init):
    t = q_ref.shape[1] // Q_TILES
    key, qry = _tile_iotas(t)
    causal = key <= qry
    lv = lam_ref[...]
    lam = (jnp.exp(jnp.sum(lv[0:1] * lv[1:2], keepdims=True))
           - jnp.exp(jnp.sum(lv[2:3] * lv[3:4], keepdims=True)) + lambda_init)
    halves = range(2 * GROUP_HEADS)

    @_now
    def _():
        for c in halves:
            kb_ref[c] = k_ref[0, :, c * DIFF_HALF:(c + 1) * DIFF_HALF].astype(kb_ref.dtype)
        for h in _HEADS:
            _fill_values_t(vt_ref.at[h], v_ref, h * HEAD_DIM, t)

    def tile(j, bias_ref=None, emask=None):
        n = len(halves)
        return _KeyTile([_key_rows(kb_ref.at[c], j, t) for c in halves], [vt_ref[c // 2, j] for c in halves],
                        None if bias_ref is None else [bias_ref[c // 2] for c in halves],
                        None if emask is None else [emask] * n)

    def query_step(step):
        qts = _queries_t(q_ref, DIFF_HALF, t)
        jobs = []
        for r in range(Q_TILES):
            k = step * Q_TILES + r
            groups = _pairs([functools.partial(tile, j) for j in range(k - 1)])
            groups.append(([functools.partial(tile, k - 1, bs_ref)] if k else [])
                          + [functools.partial(tile, k, bd_ref, causal)])
            jobs.append((len(halves), qts[r], groups))
        all_states = yield from _softmax_jobs(t, jobs)
        outs = []
        for states in all_states:
            heads = []
            for h in _HEADS:
                o = _softmax_out(states[2 * h]) - lam * _softmax_out(states[2 * h + 1])
                o = o * lax.rsqrt(jnp.mean(o * o, axis=0, keepdims=True) + NORM_EPS) * g_ref[...]
                heads.append(o * (1.0 - lambda_init))
            outs.append(heads)
        _store_heads(o_ref, outs)

    yield from query_step(0)


def _diff(proj, lam_params, subln, bias_diag, bias_sub, lambda_init, t=TILE):
    s = proj.shape[1]
    in_specs = ([pl.BlockSpec(lam_params.shape, lambda b: (0, 0)), pl.BlockSpec(subln.shape, lambda b: (0, 0))]
                + _group_specs(s, CB_DF_Q, CB_DF_K, CB_DF_V) + _bias_specs(t, 2))
    return _Mixer(functools.partial(_diff_emit, lambda_init=lambda_init), in_specs,
                  [lam_params, subln] + [proj] * 3 + [bias_diag, bias_sub],
                  _kv_scratch(s, t, key_dim=DIFF_HALF, n_keys=2 * GROUP_HEADS))


def _compress_blocks(kcv_ref, pk_ref, pv_ref, wk1_ref, wk2_ref, wv1_ref, wv2t_ref):
    n_chunk = kcv_ref.shape[1] // CMP_STRIDE

    branches = ((0, pk_ref, wk1_ref), (HEAD_DIM, pv_ref, wv1_ref))
    tops = [jnp.zeros((n_chunk, w1_ref.shape[1]), jnp.float32) for _, _, w1_ref in branches]
    bots = list(tops)
    pack = 2 * LANES // HEAD_DIM
    for l0 in range(0, CMP_STRIDE, pack):
        tokens = [kcv_ref[0, pl.ds(l, n_chunk, stride=CMP_STRIDE), :] for l in range(l0, l0 + pack)]
        for n, (col0, p_ref, w1_ref) in enumerate(branches):
            for half, acc in ((0, tops), (CMP_STRIDE, bots)):
                x = jnp.concatenate([(tok[:, col0:col0 + HEAD_DIM] + p_ref[half + l:half + l + 1, :]).astype(_MXU)
                                     for l, tok in zip(range(l0, l0 + pack), tokens)], axis=1)
                acc[n] = acc[n] + _dot(x, w1_ref[(half + l0) * HEAD_DIM:(half + l0 + pack) * HEAD_DIM, :])
    hidden = [jax.nn.gelu(top + pltpu.roll(bot, n_chunk - 1, axis=0)).astype(_MXU) for top, bot in zip(tops, bots)]
    return _dot(hidden[0], wk2_ref[...]), _dot_nt(wv2t_ref[...], hidden[1])


def _nsa_emit(q_ref, kva_ref, kvb_ref, kcv_ref, pk_ref, pv_ref, wk1_ref, wk2_ref, wv1_ref, wv2t_ref,
              bd_ref, bs_ref, bc_ref, cover_ref, e_ref, o_ref, ks_ref, vst_ref, kw_ref, vwt_ref):
    t = q_ref.shape[1] // Q_TILES
    key, qry = _tile_iotas(t)
    causal = key <= qry
    ks_col, vs_col, kw_col, vw_col, gate_col = 2 * HEAD_DIM, 3 * HEAD_DIM, 0, HEAD_DIM, 2 * HEAD_DIM

    @_now
    def _():
        ks_ref[:, :HEAD_DIM] = kva_ref[0, :, ks_col:ks_col + HEAD_DIM].astype(ks_ref.dtype)
        ks_ref[:, HEAD_DIM:] = e_ref[...]
        kw_ref[...] = kvb_ref[0, :, kw_col:kw_col + HEAD_DIM].astype(kw_ref.dtype)
        _fill_values_t(vst_ref, kva_ref, vs_col, t)
        _fill_values_t(vwt_ref, kvb_ref, vw_col, t)

    def tile(k_ref, vt_ref, j, bias_ref=None, emask=None):
        n = GROUP_HEADS
        return _KeyTile([_key_rows(k_ref, j, t)] * n, [vt_ref[j]] * n,
                        None if bias_ref is None else [bias_ref[h] for h in _HEADS],
                        None if emask is None else [emask] * n)

    def query_step(step):
        qts = _queries_t(q_ref, HEAD_DIM, t)
        kc, vct = (a.astype(_MXU) for a in _compress_blocks(kcv_ref, pk_ref, pv_ref, wk1_ref, wk2_ref,
                                                            wv1_ref, wv2t_ref))
        n_cmp = kc.shape[0]
        n_slc = cover_ref.shape[0]
        c_row = lax.broadcasted_iota(jnp.int32, (n_cmp, t), 0)
        c_col = lax.broadcasted_iota(jnp.int32, (n_cmp, t), 1)
        s_row = lax.broadcasted_iota(jnp.int32, (n_slc, t), 0)
        s_col = lax.broadcasted_iota(jnp.int32, (n_slc, t), 1)
        tiles = [step * Q_TILES + r for r in range(Q_TILES)]

        o_cmp, importance = [], []
        for r, k in enumerate(tiles):
            visible = c_col + k * t >= c_row * CMP_STRIDE + (CMP_LEN - 1)
            cmp_scores = [_dot(kc, qts[r][h]) for h in _HEADS]
            cmp_probs = []
            p_sum = jnp.zeros((n_cmp, t), jnp.float32)
            for h in _HEADS:
                sc = jnp.where(visible, cmp_scores[h] + bc_ref[h, :, r * t:(r + 1) * t], NEG_INF)
                e = jnp.where(visible, jnp.exp2(sc - jnp.max(sc, axis=0, keepdims=True)), 0.0)
                p = e / jnp.maximum(jnp.sum(e, axis=0, keepdims=True), TINY)
                cmp_probs.append(p.astype(_MXU))
                p_sum = p_sum + p
            o_cmp.append([_dot(vct, cmp_probs[h]) for h in _HEADS])
            importance.append(_dot(cover_ref[...], p_sum, precision=lax.Precision.HIGHEST))

        n_back = WINDOW // t
        jobs = []
        for r, k in enumerate(tiles):
            window = [functools.partial(tile, kw_ref, vwt_ref, k - n_back, None, qry < key)] if k >= n_back else []
            for back in range(min(n_back - 1, k), 0, -1):
                window.append(functools.partial(tile, kw_ref, vwt_ref, k - back, bs_ref if back == 1 else None))
            window.append(functools.partial(tile, kw_ref, vwt_ref, k, bd_ref, causal))
            jobs.append((GROUP_HEADS, qts[r], [window]))
        win_states = yield from _softmax_jobs(t, jobs)
        o_win = [[_softmax_out(st) for st in states] for states in win_states]

        jobs = []
        for r, k in enumerate(tiles):
            own = jnp.right_shift(s_col + k * t, int(math.log2(SLC_LEN)))
            score = jnp.where(s_row == own, FORCE, jnp.where(s_row < own, importance[r], NEG_INF))
            sel = _top_k_rows(score, s_row.astype(jnp.float32), min(SLC_TOPN, n_slc))
            penalty = jnp.where(sel > 0.5, 0.0, NEG_INF).astype(_MXU)
            q_aug = [jnp.concatenate([qts[r][h], penalty], axis=0) for h in _HEADS]
            groups = _pairs([functools.partial(tile, ks_ref, vst_ref, j) for j in range(k - 1)])
            groups.append(([functools.partial(tile, ks_ref, vst_ref, k - 1, bs_ref)] if k else [])
                          + [functools.partial(tile, ks_ref, vst_ref, k, bd_ref, causal)])
            jobs.append((GROUP_HEADS, q_aug, groups))
        slc_states = yield from _softmax_jobs(t, jobs)
        o_slc = [[_softmax_out(st) for st in states] for states in slc_states]

        gates = _transposed(kvb_ref[0, :, (gate_col // LANES) * LANES:(gate_col // LANES + 1) * LANES])
        gates = 1.0 / (1.0 + jnp.exp(-gates[gate_col % LANES:gate_col % LANES + N_GATES + 4]))
        outs = []
        for r in range(Q_TILES):
            heads = []
            for h in _HEADS:
                g = [gates[br * GROUP_HEADS + h:br * GROUP_HEADS + h + 1, r * t:(r + 1) * t] for br in range(3)]
                heads.append(g[0] * o_cmp[r][h] + g[1] * o_slc[r][h] + g[2] * o_win[r][h])
            outs.append(heads)
        _store_heads(o_ref, outs)

    yield from query_step(0)


def _nsa(proj, kcv, cmp_params, layer, bias_diag, bias_sub, bias_cmp, cover_t, expand, t=TILE):
    s = proj.shape[1]
    values_t = pltpu.VMEM((s // t, ACC_ROWS, t), _MXU)
    in_specs = (_group_specs(s, CB_NS_Q, CB_NS_A, CB_NS_B)
                + [pl.BlockSpec((1, s, kcv.shape[2]), lambda b: (b, 0, 0))]
                + [pl.BlockSpec((None,) + a.shape[1:], lambda b: (layer, 0, 0)) for a in cmp_params]
                + _bias_specs(t, 1)
                + [pl.BlockSpec(bias_cmp.shape, lambda b: (0, 0, 0)),
                   pl.BlockSpec(cover_t.shape, lambda b: (0, 0)),
                   pl.BlockSpec(expand.shape, lambda b: (0, 0))])
    scratch = [pltpu.VMEM((s, HEAD_DIM + expand.shape[1]), _MXU), values_t, pltpu.VMEM((s, HEAD_DIM), _MXU), values_t]
    return _Mixer(_nsa_emit, in_specs,
                  [proj] * 3 + [kcv, *cmp_params, bias_diag, bias_sub, bias_cmp, cover_t, expand], scratch)


def _nsa_constants(s):
    n_cmp = (s - CMP_LEN) // CMP_STRIDE + 1
    n_slc = s // SLC_LEN
    assert n_cmp + 1 == s // CMP_STRIDE and n_slc % SUBLANES == 0
    c_start = np.arange(n_cmp) * CMP_STRIDE
    s_start = np.arange(n_slc) * SLC_LEN
    cover = np.clip(np.minimum((c_start + CMP_LEN - 1)[:, None], (s_start + SLC_LEN - 1)[None, :])
                    - np.maximum(c_start[:, None], s_start[None, :]) + 1, 0, None) / CMP_LEN
    cover_t = np.zeros((n_slc, n_cmp + 1), np.float32)
    cover_t[:, :n_cmp] = cover.T
    expand = (np.arange(s)[:, None] // SLC_LEN == np.arange(n_slc)[None, :]).astype(np.float32)
    return jnp.asarray(cover_t), jnp.asarray(expand, _MXU)


def kernel(x, w_in, w_out, w_up, w_down, norm_attn, norm_mlp, cmp_pos_k, cmp_pos_v, cmp_k_w1, cmp_k_w2,
           cmp_v_w1, cmp_v_w2, diff_lambda, diff_subln, rel_bias, final_norm):
    bsz, s, d = x.shape
    depth = w_in.shape[0]
    t = TILE
    n_chunk = s // CMP_STRIDE
    assert s == Q_TILES * t and MOBA_BLOCK % t == 0 and WINDOW % t == 0 and t >= MAX_DISTANCE

    assert w_in.shape[2] + PAD_COLS == D_IN_PAD
    w_in_c = jnp.pad(w_in.astype(_MXU), ((0, 0), (0, 0), (0, PAD_COLS)))
    w_out_c, w_up_c, w_down_c = (w.astype(_MXU) for w in (w_out, w_up, w_down))
    wk1, wk2 = cmp_k_w1.astype(_MXU), cmp_k_w2.astype(_MXU)
    wv1, wv2t = cmp_v_w1.astype(_MXU), jnp.swapaxes(cmp_v_w2, 1, 2).astype(_MXU)

    table_t = jnp.pad(rel_bias.T, ((0, 0), (0, LANES - N_BUCKETS)))
    tiles = dict(n_heads=rel_bias.shape[1], head0=0, rows=t, cols=t, col_tile=t, row_stride=-1, col_stride=1)
    bias_diag = _bias_table(table_t, offset=0, **tiles)
    bias_sub = _bias_table(table_t, offset=t, **tiles)
    bias_cmp = _bias_table(table_t, n_heads=GROUP_HEADS, head0=GROUP_HEADS, rows=n_chunk, cols=s,
                           col_tile=t, row_stride=-CMP_STRIDE, col_stride=1, offset=-(CMP_LEN - 1))
    cover_t, expand = _nsa_constants(s)
    col_scale = np.ones((1, D_IN_PAD), np.float32)
    for cb, width in ((CB_SB_Q, HEAD_DIM), (CB_MB_Q, HEAD_DIM), (CB_NS_Q, HEAD_DIM), (CB_DF_Q, DIFF_HALF)):
        col_scale[:, cb * GROUP_WIDTH:(cb + 1) * GROUP_WIDTH] = width ** -0.5 * LOG2E
    col_scale = jnp.asarray(col_scale)

    x2 = x.reshape(bsz * s, d)
    for layer in range(depth):
        proj, kcv = _norm_matmul(x2, norm_attn[layer][None], w_in_c, layer, col_scale,
                                 gap=(COLS_BEFORE_PAD, COLS_BEFORE_PAD + PAD_COLS),
                                 side_col=CB_NS_A * GROUP_WIDTH, side_width=2 * HEAD_DIM)
        proj = proj.reshape(bsz, s, D_IN_PAD)
        lambda_init = 0.8 - 0.6 * math.exp(-0.3 * layer)
        o_sb = _run_mixer(_stick_breaking(proj), bsz, s, "stick_breaking")
        o_mb = _run_mixer(_moba(proj, bias_diag, bias_sub), bsz, s, "moba")
        o_ns = _run_mixer(_nsa(proj, kcv.reshape(bsz, s, 2 * HEAD_DIM), (cmp_pos_k, cmp_pos_v, wk1, wk2, wv1, wv2t),
                               layer, bias_diag, bias_sub, bias_cmp, cover_t, expand), bsz, s, "nsa")
        o_df = _run_mixer(_diff(proj, diff_lambda[layer], diff_subln[layer][:, None], bias_diag, bias_sub,
                                lambda_init), bsz, s, "diff_attention")
        groups = [o.reshape(bsz * s, GROUP_WIDTH) for o in (o_sb, o_mb, o_ns, o_df)]
        x2 = _out_mlp(x2, groups, w_out_c, norm_mlp[layer][None], w_up_c, w_down_c, layer,
                      final_norm[None], final_norm=(layer == depth - 1))
    return x2.reshape(bsz, s, d)
```

```python
import functools
import math
from typing import Any, Callable, NamedTuple, Optional, Sequence

import numpy as np
import jax
import jax.numpy as jnp
from jax import lax
from jax.experimental import pallas as pl
from jax.experimental.pallas import tpu as pltpu

HEAD_DIM = 64
GROUP_HEADS = 4
GROUP_WIDTH = GROUP_HEADS * HEAD_DIM
NORM_EPS = 1e-6
NEG_INF = -1e30
BIG = 1e30
FORCE = 1e30
TINY = 1e-30
SOFTPLUS_CLAMP = 64.0
PICKED = -3e38
LOG2E = math.log2(math.e)
N_BUCKETS = 32
MAX_DISTANCE = 128
MOBA_BLOCK = 256
MOBA_TOPK = 3
CMP_LEN = 32
CMP_STRIDE = 16
SLC_LEN = 64
SLC_TOPN = 4
WINDOW = 512
DIFF_HALF = HEAD_DIM // 2
LANES = 128
SUBLANES = 8
TILE = 256
BF16_ROWS = 16
ACC_ROWS = HEAD_DIM + BF16_ROWS
N_GATES = 3 * GROUP_HEADS
COLS_BEFORE_PAD = 9 * GROUP_WIDTH - 2 * HEAD_DIM + N_GATES
PAD_COLS = 2 * HEAD_DIM - N_GATES
CB_SB_Q, CB_SB_K, CB_SB_V, CB_MB_Q, CB_MB_K, CB_MB_V, CB_NS_Q, CB_NS_A, CB_NS_B, CB_DF_Q, CB_DF_K, CB_DF_V = range(12)
D_IN_PAD = 12 * GROUP_WIDTH

_MXU = jnp.bfloat16
_VMEM_LIMIT = 56 * 1024 * 1024
_HEADS = range(GROUP_HEADS)
Q_TILES = 8
LOOKAHEAD = 2


def _dot(a, b, precision=None):
    return jnp.dot(a, b, precision=precision, preferred_element_type=jnp.float32)


def _dot_nt(a, b, precision=None):
    return lax.dot_general(a, b, (((1,), (1,)), ((), ())), precision=precision,
                           preferred_element_type=jnp.float32)


def _rms(x, g):
    return x * lax.rsqrt(jnp.mean(x * x, axis=-1, keepdims=True) + NORM_EPS) * g


def _params(*sem):
    return pltpu.CompilerParams(dimension_semantics=sem, vmem_limit_bytes=_VMEM_LIMIT)


def _norm_matmul_kernel(x_ref, g_ref, w_ref, scale_ref, o_ref, side_ref, wp_ref, *, tn, side_col, gap):
    @pl.when(pl.program_id(0) == 0)
    def _():
        lo, hi = gap
        wp_ref[:, :lo] = w_ref[:, :lo]
        wp_ref[:, lo:hi] = jnp.zeros((wp_ref.shape[0], hi - lo), wp_ref.dtype)
        wp_ref[:, hi:] = w_ref[:, lo:lo + wp_ref.shape[1] - hi]

    h = _rms(x_ref[...], g_ref[...]).astype(_MXU)
    for j in range(wp_ref.shape[1] // tn):
        cols = slice(j * tn, (j + 1) * tn)
        acc = _dot(h, wp_ref[:, cols])
        o_ref[:, cols] = (acc * scale_ref[:, cols]).astype(o_ref.dtype)
        if j * tn <= side_col < (j + 1) * tn:
            side_ref[...] = acc[:, side_col - j * tn:side_col - j * tn + side_ref.shape[1]]


def _norm_matmul(x, g, w, layer, col_scale, *, gap, side_col, side_width, tm=1024, tn=1024):
    m, d = x.shape
    n = w.shape[2]
    assert side_col % LANES == 0 and side_col // tn == (side_col + side_width - 1) // tn
    return pl.pallas_call(
        functools.partial(_norm_matmul_kernel, tn=tn, side_col=side_col, gap=gap),
        grid=(m // tm,),
        in_specs=[pl.BlockSpec((tm, d), lambda i: (i, 0)),
                  pl.BlockSpec((1, d), lambda i: (0, 0)),
                  pl.BlockSpec((None, d, w.shape[2]), lambda i: (layer, 0, 0), pipeline_mode=pl.Buffered(1)),
                  pl.BlockSpec((1, n), lambda i: (0, 0))],
        out_specs=[pl.BlockSpec((tm, n), lambda i: (i, 0)),
                   pl.BlockSpec((tm, side_width), lambda i: (i, 0))],
        out_shape=[jax.ShapeDtypeStruct((m, n), _MXU),
                   jax.ShapeDtypeStruct((m, side_width), jnp.float32)],
        scratch_shapes=[pltpu.VMEM((d, n), _MXU)],
        compiler_params=_params("arbitrary"),
        name="norm_in_proj",
    )(x, g, w, col_scale)


def _out_mlp_kernel(x_ref, a_ref, b_ref, c_ref, d_ref, wo_ref, g_ref, wu_ref, wd_ref, gf_ref, o_ref,
                    *, final_norm, tf):
    mixed = jnp.concatenate([a_ref[...], b_ref[...], c_ref[...], d_ref[...]], axis=1)
    y = x_ref[...] + _dot(mixed, wo_ref[...])
    h = _rms(y, g_ref[...]).astype(_MXU)
    for c in range(wu_ref.shape[1] // tf):
        u = jnp.square(jnp.maximum(_dot(h, wu_ref[:, c * tf:(c + 1) * tf]), 0.0))
        y = y + _dot(u.astype(_MXU), wd_ref[c * tf:(c + 1) * tf, :])
    if final_norm:
        y = _rms(y, gf_ref[...])
    o_ref[...] = y


def _out_mlp(x, groups, w_out, g, w_up, w_down, layer, g_final, *, final_norm, tm=1024, tf=1024):
    m, d = x.shape
    f = w_up.shape[2]
    gspec = pl.BlockSpec((tm, GROUP_WIDTH), lambda i: (i, 0))
    row = pl.BlockSpec((1, d), lambda i: (0, 0))

    def resident(rows, cols):
        return pl.BlockSpec((None, rows, cols), lambda i: (layer, 0, 0), pipeline_mode=pl.Buffered(1))

    return pl.pallas_call(
        functools.partial(_out_mlp_kernel, final_norm=final_norm, tf=tf),
        grid=(m // tm,),
        in_specs=[pl.BlockSpec((tm, d), lambda i: (i, 0)), gspec, gspec, gspec, gspec,
                  resident(d, d), row, resident(d, f), resident(f, d), row],
        out_specs=pl.BlockSpec((tm, d), lambda i: (i, 0)),
        out_shape=jax.ShapeDtypeStruct((m, d), jnp.float32),
        compiler_params=_params("parallel"),
        name="out_proj_mlp_residual",
    )(x, *groups, w_out, g, w_up, w_down, g_final)


def _t5_bucket(dist):
    n = jnp.maximum(dist, 0)
    max_exact = N_BUCKETS // 2
    nf = jnp.maximum(n, 1).astype(jnp.float32)
    large = max_exact + (jnp.log(nf / max_exact) / math.log(MAX_DISTANCE / max_exact)
                         * (N_BUCKETS - max_exact)).astype(jnp.int32)
    large = jnp.minimum(large, N_BUCKETS - 1)
    return jnp.where(n < max_exact, n, large)


def _bias_kernel(tab_ref, o_ref, *, row_stride, col_stride, offset, head0):
    nh, tr, tc = o_ref.shape
    for blk in range(tc // LANES):
        rows = lax.broadcasted_iota(jnp.int32, (tr, LANES), 0)
        cols = lax.broadcasted_iota(jnp.int32, (tr, LANES), 1) + (pl.program_id(0) * tc + blk * LANES)
        bucket = _t5_bucket(rows * row_stride + cols * col_stride + offset)
        for h in range(nh):
            row = tab_ref[head0 + h:head0 + h + 1, :]
            row = (row - row[:, N_BUCKETS - 1:N_BUCKETS]) * LOG2E
            o_ref[h, :, blk * LANES:(blk + 1) * LANES] = jnp.take_along_axis(
                jnp.broadcast_to(row, (tr, LANES)), bucket, axis=1, mode="promise_in_bounds")


def _bias_table(table_t, *, n_heads, head0, rows, cols, col_tile, row_stride, col_stride, offset):
    return pl.pallas_call(
        functools.partial(_bias_kernel, row_stride=row_stride, col_stride=col_stride,
                          offset=offset, head0=head0),
        grid=(cols // col_tile,),
        in_specs=[pl.BlockSpec(table_t.shape, lambda i: (0, 0))],
        out_specs=pl.BlockSpec((n_heads, rows, col_tile), lambda i: (0, 0, i)),
        out_shape=jax.ShapeDtypeStruct((n_heads, rows, cols), jnp.float32),
        compiler_params=_params("parallel"),
        name="t5_bias_tiles",
    )(table_t)


def _softmax_init(t):
    return (jnp.full((1, t), NEG_INF, jnp.float32), jnp.zeros((ACC_ROWS, t), jnp.float32))


class _KeyTile(NamedTuple):
    kts: Sequence[Any]
    vts: Sequence[Any]
    biases: Optional[Sequence[Any]] = None
    emasks: Optional[Sequence[Any]] = None
    qmasks: Optional[Sequence[Any]] = None


def _round_robin(lists):
    out = []
    for rank in range(max(map(len, lists), default=0)):
        out.extend(items[rank] for items in lists if rank < len(items))
    return out


def _softmax_jobs(t, jobs):
    built = {}

    def tiles_of(job, g):
        if (job, g) not in built:
            built[job, g] = [make() for make in jobs[job][2][g]]
        return built[job, g]

    def scores_of(job, g, c):
        qts = jobs[job][1]
        row = []
        for tile in tiles_of(job, g):
            s = _dot(tile.kts[c], qts[c])
            if tile.biases is not None:
                s = s + tile.biases[c]
            if tile.emasks is not None:
                s = jnp.where(tile.emasks[c], s, NEG_INF)
            row.append(s.astype(_MXU))
        return row

    def update(state, job, g, c, scores):
        m, acc = state
        m_new = m
        for tile, s in zip(tiles_of(job, g), scores):
            tile_max = jnp.max(s, axis=0, keepdims=True).astype(jnp.float32)
            if tile.qmasks is not None:
                tile_max = jnp.where(tile.qmasks[c], tile_max, NEG_INF)
            m_new = jnp.maximum(m_new, tile_max)
        seen = m_new > 0.5 * NEG_INF
        acc = jnp.exp2(m - m_new) * acc
        for tile, s in zip(tiles_of(job, g), scores):
            ok = seen if tile.qmasks is None else jnp.logical_and(seen, tile.qmasks[c])
            acc = acc + _dot(tile.vts[c], jnp.exp2(s - jnp.where(ok, m_new, BIG).astype(_MXU)))
        return m_new, acc

    units = _round_robin([[(job, g, c) for g in range(len(groups)) for c in range(n)]
                          for job, (n, _, groups) in enumerate(jobs)])
    lookahead = max(LOOKAHEAD, len(jobs) // 2)
    states = [[_softmax_init(t) for _ in range(n)] for n, _, _ in jobs]
    pending = {k: scores_of(*units[k]) for k in range(min(lookahead, len(units)))}
    for k, (job, g, c) in enumerate(units):
        if k + lookahead < len(units):
            pending[k + lookahead] = scores_of(*units[k + lookahead])
        states[job][c] = update(states[job][c], job, g, c, pending.pop(k))
        yield
    return states


def _pairs(items, size=2):
    return [items[p:p + size] for p in range(0, len(items), size)]


class _Mixer(NamedTuple):
    emit: Callable[..., Any]
    in_specs: Sequence[Any]
    operands: Sequence[Any]
    scratch: Sequence[Any]


def _mixer_kernel(*refs, emit):
    for _ in emit(*refs):
        pass


def _run_mixer(mixer, bsz, s, name):
    return pl.pallas_call(
        functools.partial(_mixer_kernel, emit=mixer.emit),
        grid=(bsz,),
        in_specs=list(mixer.in_specs),
        out_specs=pl.BlockSpec((1, s, GROUP_WIDTH), lambda b: (b, 0, 0)),
        out_shape=jax.ShapeDtypeStruct((bsz, s, GROUP_WIDTH), _MXU),
        scratch_shapes=list(mixer.scratch),
        compiler_params=_params("parallel"),
        name=name,
    )(*mixer.operands)


def _softmax_out(state):
    acc = state[1]
    return acc[:HEAD_DIM] / jnp.maximum(acc[HEAD_DIM:HEAD_DIM + 1], TINY)


def _top_k_rows(score, row_f, k):
    sel = jnp.zeros(score.shape, jnp.float32)
    for _ in range(k):
        mx = jnp.max(score, axis=0, keepdims=True)
        idx = jnp.min(jnp.where(score == mx, row_f, float(score.shape[0])), axis=0, keepdims=True)
        pick = row_f == idx
        sel = jnp.where(pick, 1.0, sel)
        score = jnp.where(pick, PICKED, score)
    return sel


def _now(fn):
    fn()


def _tile_iotas(t):
    return (lax.broadcasted_iota(jnp.int32, (t, t), 0), lax.broadcasted_iota(jnp.int32, (t, t), 1))


def _key_rows(ref, j, t):
    if isinstance(j, int):
        return ref[j * t:(j + 1) * t, :]
    return ref[pl.ds(pl.multiple_of(j * t, t), t), :]


def _transposed(ref_block):
    return ref_block.astype(jnp.float32).T


def _queries_t(q_ref, width, t):
    qt = _transposed(q_ref[0]).astype(_MXU)
    return [[qt[c * width:(c + 1) * width, r * t:(r + 1) * t] for c in range(GROUP_WIDTH // width)]
            for r in range(Q_TILES)]


def _fill_values_t(vt_ref, v_ref, col0, t):
    n_tiles, rows, _ = vt_ref.shape
    lane_block = (col0 // LANES) * LANES
    for c in range(n_tiles):
        blk = _transposed(v_ref[0, c * t:(c + 1) * t, lane_block:lane_block + LANES])
        vt_ref[c, 0:HEAD_DIM, :] = blk[col0 - lane_block:col0 - lane_block + HEAD_DIM].astype(vt_ref.dtype)
        if rows == ACC_ROWS:
            first = lax.broadcasted_iota(jnp.int32, (rows - HEAD_DIM, t), 0) == 0
            vt_ref[c, HEAD_DIM:rows, :] = jnp.where(first, 1.0, 0.0).astype(vt_ref.dtype)


def _group_specs(s, *column_blocks):
    return [pl.BlockSpec((1, s, GROUP_WIDTH), functools.partial(lambda cb, b: (b, 0, cb), cb))
            for cb in column_blocks]


def _bias_specs(t, head_group):
    spec = pl.BlockSpec((GROUP_HEADS, t, t), lambda b: (head_group, 0, 0))
    return [spec, spec]


def _store_heads(o_ref, outs_t):
    tiles = [jnp.concatenate(heads, axis=0) for heads in outs_t]
    o_ref[0] = jnp.concatenate(tiles, axis=1).T.astype(o_ref.dtype)


def _sb_emit(q_ref, k_ref, v_ref, o_ref, kb_ref, vt_ref):
    t = q_ref.shape[1] // Q_TILES
    key, qry = _tile_iotas(t)
    strict = key < qry
    later = jnp.where(qry > key, 1.0, 0.0).astype(_MXU)

    @_now
    def _():
        for h in _HEADS:
            kb_ref[h] = k_ref[0, :, h * HEAD_DIM:(h + 1) * HEAD_DIM].astype(kb_ref.dtype)
            _fill_values_t(vt_ref.at[h], v_ref, h * HEAD_DIM, t)

    def query_step(step):
        qts = _queries_t(q_ref, HEAD_DIM, t)
        units = _round_robin([[(r, step * Q_TILES + r, pair, h)
                               for pair in _pairs(list(range(step * Q_TILES + r, -1, -1)), 3) for h in _HEADS]
                              for r in range(Q_TILES)])
        zero = (jnp.zeros((HEAD_DIM, t), jnp.float32), jnp.zeros((1, t), jnp.float32))
        carry = [[zero] * GROUP_HEADS for _ in range(Q_TILES)]
        zs, log_keeps, suffixes = {}, {}, {}

        def scores(u):
            r, _, pair, h = units[u]
            zs[u] = [_dot(_key_rows(kb_ref.at[h], j, t), qts[r][h]) for j in pair]

        def keeps(u):
            _, k, pair, _ = units[u]
            log_keeps[u], suffixes[u] = [], []
            for j, z in zip(pair, zs[u]):
                drop = jnp.maximum(jnp.log2(1.0 + jnp.exp2(jnp.minimum(z, SOFTPLUS_CLAMP))), z)
                if j == k:
                    drop = jnp.where(strict, drop, 0.0)
                log_keeps[u].append(drop)
                suffixes[u].append(_dot(later, drop.astype(_MXU)))

        def values(u):
            r, k, pair, h = units[u]
            acc, run = carry[r][h]
            weights = []
            for j, z, drop, suffix in zip(pair, zs.pop(u), log_keeps.pop(u), suffixes.pop(u)):
                a = jnp.exp2(z - drop - suffix + run)
                if j == k:
                    a = jnp.where(strict, a, 0.0)
                weights.append(a.astype(_MXU))
                run = run - (suffix[0:1] + drop[0:1])
            for j, w in zip(pair, weights):
                acc = acc + _dot(vt_ref[h, j], w)
            carry[r][h] = (acc, run)

        stages = (scores, keeps, values)
        for tick in range(len(units) + len(stages) - 1):
            for lag, stage in enumerate(stages):
                if 0 <= tick - lag < len(units):
                    stage(tick - lag)
            yield
        _store_heads(o_ref, [[c[0] for c in tile_carry] for tile_carry in carry])

    yield from query_step(0)


def _stick_breaking(proj, t=TILE):
    s = proj.shape[1]
    scratch = [pltpu.VMEM((GROUP_HEADS, s, HEAD_DIM), _MXU), pltpu.VMEM((GROUP_HEADS, s // t, HEAD_DIM, t), _MXU)]
    return _Mixer(_sb_emit, _group_specs(s, CB_SB_Q, CB_SB_K, CB_SB_V), [proj] * 3, scratch)


def _moba_emit(q_ref, k_ref, v_ref, bd_ref, bs_ref, o_ref, kb_ref, vt_ref, km_ref):
    t = q_ref.shape[1] // Q_TILES
    n_blk = k_ref.shape[1] // MOBA_BLOCK
    tiles_per_blk = MOBA_BLOCK // t
    blk_shift = int(math.log2(tiles_per_blk))

    @_now
    def _():
        km_ref[...] = jnp.zeros_like(km_ref)
        for h in _HEADS:
            lo, hi = h * HEAD_DIM, (h + 1) * HEAD_DIM
            kb_ref[h] = k_ref[0, :, lo:hi].astype(kb_ref.dtype)
            _fill_values_t(vt_ref.at[h], v_ref, lo, t)
            for n in range(n_blk):
                blk = k_ref[0, n * MOBA_BLOCK:(n + 1) * MOBA_BLOCK, lo:hi]
                km_ref[h, n:n + 1, :] = jnp.mean(blk.astype(jnp.float32), axis=0, keepdims=True)

    def query_step(step):
        key, qry = _tile_iotas(t)
        causal = key <= qry
        blk_row = lax.broadcasted_iota(jnp.int32, (km_ref.shape[1], t), 0)
        qt = _transposed(q_ref[0])
        jobs = []
        for r in range(Q_TILES):
            k = step * Q_TILES + r
            own = k >> blk_shift
            qts, sels = [], []
            for h in _HEADS:
                qf = qt[h * HEAD_DIM:(h + 1) * HEAD_DIM, r * t:(r + 1) * t]
                qts.append(qf.astype(_MXU))
                gate = _dot(km_ref[h], qf, precision=lax.Precision.HIGHEST)
                gate = jnp.where(blk_row < own, gate, NEG_INF)
                sel = _top_k_rows(gate, blk_row.astype(jnp.float32), min(MOBA_TOPK, n_blk - 1))
                sels.append(jnp.where(blk_row < own, sel, 0.0))

            def tile(own, sels, j, bias_ref=None, emask=None):
                n = j >> blk_shift
                return _KeyTile([_key_rows(kb_ref.at[h], j, t) for h in _HEADS], [vt_ref[h, j] for h in _HEADS],
                                None if bias_ref is None else [bias_ref[h] for h in _HEADS],
                                None if emask is None else [emask] * GROUP_HEADS,
                                None if n == own else [sels[h][n:n + 1] > 0.5 for h in _HEADS])

            tile = functools.partial(tile, own, sels)
            groups = _pairs([functools.partial(tile, j) for j in range(k - 1)])
            groups.append(([functools.partial(tile, k - 1, bs_ref)] if k else [])
                          + [functools.partial(tile, k, bd_ref, causal)])
            jobs.append((GROUP_HEADS, qts, groups))
        all_states = yield from _softmax_jobs(t, jobs)
        _store_heads(o_ref, [[_softmax_out(st) for st in states] for states in all_states])

    yield from query_step(0)


def _kv_scratch(s, t, key_dim=HEAD_DIM, n_keys=GROUP_HEADS):
    return [pltpu.VMEM((n_keys, s, key_dim), _MXU), pltpu.VMEM((GROUP_HEADS, s // t, ACC_ROWS, t), _MXU)]


def _moba(proj, bias_diag, bias_sub, t=TILE):
    s = proj.shape[1]
    n_blk_pad = -(-(s // MOBA_BLOCK) // SUBLANES) * SUBLANES
    scratch = _kv_scratch(s, t) + [pltpu.VMEM((GROUP_HEADS, n_blk_pad, HEAD_DIM), jnp.float32)]
    return _Mixer(_moba_emit, _group_specs(s, CB_MB_Q, CB_MB_K, CB_MB_V) + _bias_specs(t, 0),
                  [proj] * 3 + [bias_diag, bias_sub], scratch)


def _diff_emit(lam_ref, g_ref, q_ref, k_ref, v_ref, bd_ref, bs_ref, o_ref, kb_ref, vt_ref, *, lambda_init):
    t = q_ref.shape[1] // Q_TILES
    key, qry = _tile_iotas(t)
    causal = key <= qry
    lv = lam_ref[...]
    lam = (jnp.exp(jnp.sum(lv[0:1] * lv[1:2], keepdims=True))
           - jnp.exp(jnp.sum(lv[2:3] * lv[3:4], keepdims=True)) + lambda_init)
    halves = range(2 * GROUP_HEADS)

    @_now
    def _():
        for c in halves:
            kb_ref[c] = k_ref[0, :, c * DIFF_HALF:(c + 1) * DIFF_HALF].astype(kb_ref.dtype)
        for h in _HEADS:
            _fill_values_t(vt_ref.at[h], v_ref, h * HEAD_DIM, t)

    def tile(j, bias_ref=None, emask=None):
        n = len(halves)
        return _KeyTile([_key_rows(kb_ref.at[c], j, t) for c in halves], [vt_ref[c // 2, j] for c in halves],
                        None if bias_ref is None else [bias_ref[c // 2] for c in halves],
                        None if emask is None else [emask] * n)

    def query_step(step):
        qts = _queries_t(q_ref, DIFF_HALF, t)
        jobs = []
        for r in range(Q_TILES):
            k = step * Q_TILES + r
            groups = _pairs([functools.partial(tile, j) for j in range(k - 1)])
            groups.append(([functools.partial(tile, k - 1, bs_ref)] if k else [])
                          + [functools.partial(tile, k, bd_ref, causal)])
            jobs.append((len(halves), qts[r], groups))
        all_states = yield from _softmax_jobs(t, jobs)
        outs = []
        for states in all_states:
            heads = []
            for h in _HEADS:
                o = _softmax_out(states[2 * h]) - lam * _softmax_out(states[2 * h + 1])
                o = o * lax.rsqrt(jnp.mean(o * o, axis=0, keepdims=True) + NORM_EPS) * g_ref[...]
                heads.append(o * (1.0 - lambda_init))
            outs.append(heads)
        _store_heads(o_ref, outs)

    yield from query_step(0)


def _diff(proj, lam_params, subln, bias_diag, bias_sub, lambda_init, t=TILE):
    s = proj.shape[1]
    in_specs = ([pl.BlockSpec(lam_params.shape, lambda b: (0, 0)), pl.BlockSpec(subln.shape, lambda b: (0, 0))]
                + _group_specs(s, CB_DF_Q, CB_DF_K, CB_DF_V) + _bias_specs(t, 2))
    return _Mixer(functools.partial(_diff_emit, lambda_init=lambda_init), in_specs,
                  [lam_params, subln] + [proj] * 3 + [bias_diag, bias_sub],
                  _kv_scratch(s, t, key_dim=DIFF_HALF, n_keys=2 * GROUP_HEADS))


def _compress_blocks(kcv_ref, pk_ref, pv_ref, wk1_ref, wk2_ref, wv1_ref, wv2t_ref):
    n_chunk = kcv_ref.shape[1] // CMP_STRIDE

    branches = ((0, pk_ref, wk1_ref), (HEAD_DIM, pv_ref, wv1_ref))
    tops = [jnp.zeros((n_chunk, w1_ref.shape[1]), jnp.float32) for _, _, w1_ref in branches]
    bots = list(tops)
    pack = 2 * LANES // HEAD_DIM
    for l0 in range(0, CMP_STRIDE, pack):
        tokens = [kcv_ref[0, pl.ds(l, n_chunk, stride=CMP_STRIDE), :] for l in range(l0, l0 + pack)]
        for n, (col0, p_ref, w1_ref) in enumerate(branches):
            for half, acc in ((0, tops), (CMP_STRIDE, bots)):
                x = jnp.concatenate([(tok[:, col0:col0 + HEAD_DIM] + p_ref[half + l:half + l + 1, :]).astype(_MXU)
                                     for l, tok in zip(range(l0, l0 + pack), tokens)], axis=1)
                acc[n] = acc[n] + _dot(x, w1_ref[(half + l0) * HEAD_DIM:(half + l0 + pack) * HEAD_DIM, :])
    hidden = [jax.nn.gelu(top + pltpu.roll(bot, n_chunk - 1, axis=0)).astype(_MXU) for top, bot in zip(tops, bots)]
    return _dot(hidden[0], wk2_ref[...]), _dot_nt(wv2t_ref[...], hidden[1])


def _nsa_emit(q_ref, kva_ref, kvb_ref, kcv_ref, pk_ref, pv_ref, wk1_ref, wk2_ref, wv1_ref, wv2t_ref,
              bd_ref, bs_ref, bc_ref, cover_ref, e_ref, o_ref, ks_ref, vst_ref, kw_ref, vwt_ref):
    t = q_ref.shape[1] // Q_TILES
    key, qry = _tile_iotas(t)
    causal = key <= qry
    ks_col, vs_col, kw_col, vw_col, gate_col = 2 * HEAD_DIM, 3 * HEAD_DIM, 0, HEAD_DIM, 2 * HEAD_DIM

    @_now
    def _():
        ks_ref[:, :HEAD_DIM] = kva_ref[0, :, ks_col:ks_col + HEAD_DIM].astype(ks_ref.dtype)
        ks_ref[:, HEAD_DIM:] = e_ref[...]
        kw_ref[...] = kvb_ref[0, :, kw_col:kw_col + HEAD_DIM].astype(kw_ref.dtype)
        _fill_values_t(vst_ref, kva_ref, vs_col, t)
        _fill_values_t(vwt_ref, kvb_ref, vw_col, t)

    def tile(k_ref, vt_ref, j, bias_ref=None, emask=None):
        n = GROUP_HEADS
        return _KeyTile([_key_rows(k_ref, j, t)] * n, [vt_ref[j]] * n,
                        None if bias_ref is None else [bias_ref[h] for h in _HEADS],
                        None if emask is None else [emask] * n)

    def query_step(step):
        qts = _queries_t(q_ref, HEAD_DIM, t)
        kc, vct = (a.astype(_MXU) for a in _compress_blocks(kcv_ref, pk_ref, pv_ref, wk1_ref, wk2_ref,
                                                            wv1_ref, wv2t_ref))
        n_cmp = kc.shape[0]
        n_slc = cover_ref.shape[0]
        c_row = lax.broadcasted_iota(jnp.int32, (n_cmp, t), 0)
        c_col = lax.broadcasted_iota(jnp.int32, (n_cmp, t), 1)
        s_row = lax.broadcasted_iota(jnp.int32, (n_slc, t), 0)
        s_col = lax.broadcasted_iota(jnp.int32, (n_slc, t), 1)
        tiles = [step * Q_TILES + r for r in range(Q_TILES)]

        o_cmp, importance = [], []
        for r, k in enumerate(tiles):
            visible = c_col + k * t >= c_row * CMP_STRIDE + (CMP_LEN - 1)
            cmp_scores = [_dot(kc, qts[r][h]) for h in _HEADS]
            cmp_probs = []
            p_sum = jnp.zeros((n_cmp, t), jnp.float32)
            for h in _HEADS:
                sc = jnp.where(visible, cmp_scores[h] + bc_ref[h, :, r * t:(r + 1) * t], NEG_INF)
                e = jnp.where(visible, jnp.exp2(sc - jnp.max(sc, axis=0, keepdims=True)), 0.0)
                p = e / jnp.maximum(jnp.sum(e, axis=0, keepdims=True), TINY)
                cmp_probs.append(p.astype(_MXU))
                p_sum = p_sum + p
            o_cmp.append([_dot(vct, cmp_probs[h]) for h in _HEADS])
            importance.append(_dot(cover_ref[...], p_sum, precision=lax.Precision.HIGHEST))

        n_back = WINDOW // t
        jobs = []
        for r, k in enumerate(tiles):
            window = [functools.partial(tile, kw_ref, vwt_ref, k - n_back, None, qry < key)] if k >= n_back else []
            for back in range(min(n_back - 1, k), 0, -1):
                window.append(functools.partial(tile, kw_ref, vwt_ref, k - back, bs_ref if back == 1 else None))
            window.append(functools.partial(tile, kw_ref, vwt_ref, k, bd_ref, causal))
            jobs.append((GROUP_HEADS, qts[r], [window]))
        win_states = yield from _softmax_jobs(t, jobs)
        o_win = [[_softmax_out(st) for st in states] for states in win_states]

        jobs = []
        for r, k in enumerate(tiles):
            own = jnp.right_shift(s_col + k * t, int(math.log2(SLC_LEN)))
            score = jnp.where(s_row == own, FORCE, jnp.where(s_row < own, importance[r], NEG_INF))
            sel = _top_k_rows(score, s_row.astype(jnp.float32), min(SLC_TOPN, n_slc))
            penalty = jnp.where(sel > 0.5, 0.0, NEG_INF).astype(_MXU)
            q_aug = [jnp.concatenate([qts[r][h], penalty], axis=0) for h in _HEADS]
            groups = _pairs([functools.partial(tile, ks_ref, vst_ref, j) for j in range(k - 1)])
            groups.append(([functools.partial(tile, ks_ref, vst_ref, k - 1, bs_ref)] if k else [])
                          + [functools.partial(tile, ks_ref, vst_ref, k, bd_ref, causal)])
            jobs.append((GROUP_HEADS, q_aug, groups))
        slc_states = yield from _softmax_jobs(t, jobs)
        o_slc = [[_softmax_out(st) for st in states] for states in slc_states]

        gates = _transposed(kvb_ref[0, :, (gate_col // LANES) * LANES:(gate_col // LANES + 1) * LANES])
        gates = 1.0 / (1.0 + jnp.exp(-gates[gate_col % LANES:gate_col % LANES + N_GATES + 4]))
        outs = []
        for r in range(Q_TILES):
            heads = []
            for h in _HEADS:
                g = [gates[br * GROUP_HEADS + h:br * GROUP_HEADS + h + 1, r * t:(r + 1) * t] for br in range(3)]
                heads.append(g[0] * o_cmp[r][h] + g[1] * o_slc[r][h] + g[2] * o_win[r][h])
            outs.append(heads)
        _store_heads(o_ref, outs)

    yield from query_step(0)


def _nsa(proj, kcv, cmp_params, layer, bias_diag, bias_sub, bias_cmp, cover_t, expand, t=TILE):
    s = proj.shape[1]
    values_t = pltpu.VMEM((s // t, ACC_ROWS, t), _MXU)
    in_specs = (_group_specs(s, CB_NS_Q, CB_NS_A, CB_NS_B)
                + [pl.BlockSpec((1, s, kcv.shape[2]), lambda b: (b, 0, 0))]
                + [pl.BlockSpec((None,) + a.shape[1:], lambda b: (layer, 0, 0)) for a in cmp_params]
                + _bias_specs(t, 1)
                + [pl.BlockSpec(bias_cmp.shape, lambda b: (0, 0, 0)),
                   pl.BlockSpec(cover_t.shape, lambda b: (0, 0)),
                   pl.BlockSpec(expand.shape, lambda b: (0, 0))])
    scratch = [pltpu.VMEM((s, HEAD_DIM + expand.shape[1]), _MXU), values_t, pltpu.VMEM((s, HEAD_DIM), _MXU), values_t]
    return _Mixer(_nsa_emit, in_specs,
                  [proj] * 3 + [kcv, *cmp_params, bias_diag, bias_sub, bias_cmp, cover_t, expand], scratch)


def _nsa_constants(s):
    n_cmp = (s - CMP_LEN) // CMP_STRIDE + 1
    n_slc = s // SLC_LEN
    assert n_cmp + 1 == s // CMP_STRIDE and n_slc % SUBLANES == 0
    c_start = np.arange(n_cmp) * CMP_STRIDE
    s_start = np.arange(n_slc) * SLC_LEN
    cover = np.clip(np.minimum((c_start + CMP_LEN - 1)[:, None], (s_start + SLC_LEN - 1)[None, :])
                    - np.maximum(c_start[:, None], s_start[None, :]) + 1, 0, None) / CMP_LEN
    cover_t = np.zeros((n_slc, n_cmp + 1), np.float32)
    cover_t[:, :n_cmp] = cover.T
    expand = (np.arange(s)[:, None] // SLC_LEN == np.arange(n_slc)[None, :]).astype(np.float32)
    return jnp.asarray(cover_t), jnp.asarray(expand, _MXU)


def kernel(x, w_in, w_out, w_up, w_down, norm_attn, norm_mlp, cmp_pos_k, cmp_pos_v, cmp_k_w1, cmp_k_w2,
           cmp_v_w1, cmp_v_w2, diff_lambda, diff_subln, rel_bias, final_norm):
    bsz, s, d = x.shape
    depth = w_in.shape[0]
    t = TILE
    n_chunk = s // CMP_STRIDE
    assert s == Q_TILES * t and MOBA_BLOCK % t == 0 and WINDOW % t == 0 and t >= MAX_DISTANCE

    assert w_in.shape[2] + PAD_COLS == D_IN_PAD
    w_in_c = jnp.pad(w_in.astype(_MXU), ((0, 0), (0, 0), (0, PAD_COLS)))
    w_out_c, w_up_c, w_down_c = (w.astype(_MXU) for w in (w_out, w_up, w_down))
    wk1, wk2 = cmp_k_w1.astype(_MXU), cmp_k_w2.astype(_MXU)
    wv1, wv2t = cmp_v_w1.astype(_MXU), jnp.swapaxes(cmp_v_w2, 1, 2).astype(_MXU)

    table_t = jnp.pad(rel_bias.T, ((0, 0), (0, LANES - N_BUCKETS)))
    tiles = dict(n_heads=rel_bias.shape[1], head0=0, rows=t, cols=t, col_tile=t, row_stride=-1, col_stride=1)
    bias_diag = _bias_table(table_t, offset=0, **tiles)
    bias_sub = _bias_table(table_t, offset=t, **tiles)
    bias_cmp = _bias_table(table_t, n_heads=GROUP_HEADS, head0=GROUP_HEADS, rows=n_chunk, cols=s,
                           col_tile=t, row_stride=-CMP_STRIDE, col_stride=1, offset=-(CMP_LEN - 1))
    cover_t, expand = _nsa_constants(s)
    col_scale = np.ones((1, D_IN_PAD), np.float32)
    for cb, width in ((CB_SB_Q, HEAD_DIM), (CB_MB_Q, HEAD_DIM), (CB_NS_Q, HEAD_DIM), (CB_DF_Q, DIFF_HALF)):
        col_scale[:, cb * GROUP_WIDTH:(cb + 1) * GROUP_WIDTH] = width ** -0.5 * LOG2E
    col_scale = jnp.asarray(col_scale)

    x2 = x.reshape(bsz * s, d)
    for layer in range(depth):
        proj, kcv = _norm_matmul(x2, norm_attn[layer][None], w_in_c, layer, col_scale,
                                 gap=(COLS_BEFORE_PAD, COLS_BEFORE_PAD + PAD_COLS),
                                 side_col=CB_NS_A * GROUP_WIDTH, side_width=2 * HEAD_DIM)
        proj = proj.reshape(bsz, s, D_IN_PAD)
        lambda_init = 0.8 - 0.6 * math.exp(-0.3 * layer)
        o_sb = _run_mixer(_stick_breaking(proj), bsz, s, "stick_breaking")
        o_mb = _run_mixer(_moba(proj, bias_diag, bias_sub), bsz, s, "moba")
        o_ns = _run_mixer(_nsa(proj, kcv.reshape(bsz, s, 2 * HEAD_DIM), (cmp_pos_k, cmp_pos_v, wk1, wk2, wv1, wv2t),
                               layer, bias_diag, bias_sub, bias_cmp, cover_t, expand), bsz, s, "nsa")
        o_df = _run_mixer(_diff(proj, diff_lambda[layer], diff_subln[layer][:, None], bias_diag, bias_sub,
                                lambda_init), bsz, s, "diff_attention")
        groups = [o.reshape(bsz * s, GROUP_WIDTH) for o in (o_sb, o_mb, o_ns, o_df)]
        x2 = _out_mlp(x2, groups, w_out_c, norm_mlp[layer][None], w_up_c, w_down_c, layer,
                      final_norm[None], final_norm=(layer == depth - 1))
    return x2.reshape(bsz, s, d)
```

```python
import functools
import math
from typing import Any, Callable, NamedTuple, Optional, Sequence

import numpy as np
import jax
import jax.numpy as jnp
from jax import lax
from jax.experimental import pallas as pl
from jax.experimental.pallas import tpu as pltpu

HEAD_DIM = 64
GROUP_HEADS = 4
GROUP_WIDTH = GROUP_HEADS * HEAD_DIM
NORM_EPS = 1e-6
NEG_INF = -1e30
BIG = 1e30
FORCE = 1e30
TINY = 1e-30
SOFTPLUS_CLAMP = 64.0
PICKED = -3e38
LOG2E = math.log2(math.e)
N_BUCKETS = 32
MAX_DISTANCE = 128
MOBA_BLOCK = 256
MOBA_TOPK = 3
CMP_LEN = 32
CMP_STRIDE = 16
SLC_LEN = 64
SLC_TOPN = 4
WINDOW = 512
DIFF_HALF = HEAD_DIM // 2
LANES = 128
SUBLANES = 8
TILE = 256
BF16_ROWS = 16
ACC_ROWS = HEAD_DIM + BF16_ROWS
N_GATES = 3 * GROUP_HEADS
COLS_BEFORE_PAD = 9 * GROUP_WIDTH - 2 * HEAD_DIM + N_GATES
PAD_COLS = 2 * HEAD_DIM - N_GATES
CB_SB_Q, CB_SB_K, CB_SB_V, CB_MB_Q, CB_MB_K, CB_MB_V, CB_NS_Q, CB_NS_A, CB_NS_B, CB_DF_Q, CB_DF_K, CB_DF_V = range(12)
D_IN_PAD = 12 * GROUP_WIDTH

_MXU = jnp.bfloat16
_VMEM_LIMIT = 56 * 1024 * 1024
_HEADS = range(GROUP_HEADS)
Q_TILES = 8
LOOKAHEAD = 2


def _dot(a, b, precision=None):
    return jnp.dot(a, b, precision=precision, preferred_element_type=jnp.float32)


def _dot_nt(a, b, precision=None):
    return lax.dot_general(a, b, (((1,), (1,)), ((), ())), precision=precision,
                           preferred_element_type=jnp.float32)


def _rms(x, g):
    return x * lax.rsqrt(jnp.mean(x * x, axis=-1, keepdims=True) + NORM_EPS) * g


def _params(*sem):
    return pltpu.CompilerParams(dimension_semantics=sem, vmem_limit_bytes=_VMEM_LIMIT)


def _norm_matmul_kernel(x_ref, g_ref, w_ref, scale_ref, o_ref, side_ref, wp_ref, *, tn, side_col, gap):
    @pl.when(pl.program_id(0) == 0)
    def _():
        lo, hi = gap
        wp_ref[:, :lo] = w_ref[:, :lo]
        wp_ref[:, lo:hi] = jnp.zeros((wp_ref.shape[0], hi - lo), wp_ref.dtype)
        wp_ref[:, hi:] = w_ref[:, lo:lo + wp_ref.shape[1] - hi]

    h = _rms(x_ref[...], g_ref[...]).astype(_MXU)
    for j in range(wp_ref.shape[1] // tn):
        cols = slice(j * tn, (j + 1) * tn)
        acc = _dot(h, wp_ref[:, cols])
        o_ref[:, cols] = (acc * scale_ref[:, cols]).astype(o_ref.dtype)
        if j * tn <= side_col < (j + 1) * tn:
            side_ref[...] = acc[:, side_col - j * tn:side_col - j * tn + side_ref.shape[1]]


def _norm_matmul(x, g, w, layer, col_scale, *, gap, side_col, side_width, tm=1024, tn=1024):
    m, d = x.shape
    n = w.shape[2]
    assert side_col % LANES == 0 and side_col // tn == (side_col + side_width - 1) // tn
    return pl.pallas_call(
        functools.partial(_norm_matmul_kernel, tn=tn, side_col=side_col, gap=gap),
        grid=(m // tm,),
        in_specs=[pl.BlockSpec((tm, d), lambda i: (i, 0)),
                  pl.BlockSpec((1, d), lambda i: (0, 0)),
                  pl.BlockSpec((None, d, w.shape[2]), lambda i: (layer, 0, 0), pipeline_mode=pl.Buffered(1)),
                  pl.BlockSpec((1, n), lambda i: (0, 0))],
        out_specs=[pl.BlockSpec((tm, n), lambda i: (i, 0)),
                   pl.BlockSpec((tm, side_width), lambda i: (i, 0))],
        out_shape=[jax.ShapeDtypeStruct((m, n), _MXU),
                   jax.ShapeDtypeStruct((m, side_width), jnp.float32)],
        scratch_shapes=[pltpu.VMEM((d, n), _MXU)],
        compiler_params=_params("arbitrary"),
        name="norm_in_proj",
    )(x, g, w, col_scale)


def _out_mlp_kernel(x_ref, a_ref, b_ref, c_ref, d_ref, wo_ref, g_ref, wu_ref, wd_ref, gf_ref, o_ref,
                    *, final_norm, tf):
    mixed = jnp.concatenate([a_ref[...], b_ref[...], c_ref[...], d_ref[...]], axis=1)
    y = x_ref[...] + _dot(mixed, wo_ref[...])
    h = _rms(y, g_ref[...]).astype(_MXU)
    for c in range(wu_ref.shape[1] // tf):
        u = jnp.square(jnp.maximum(_dot(h, wu_ref[:, c * tf:(c + 1) * tf]), 0.0))
        y = y + _dot(u.astype(_MXU), wd_ref[c * tf:(c + 1) * tf, :])
    if final_norm:
        y = _rms(y, gf_ref[...])
    o_ref[...] = y


def _out_mlp(x, groups, w_out, g, w_up, w_down, layer, g_final, *, final_norm, tm=1024, tf=1024):
    m, d = x.shape
    f = w_up.shape[2]
    gspec = pl.BlockSpec((tm, GROUP_WIDTH), lambda i: (i, 0))
    row = pl.BlockSpec((1, d), lambda i: (0, 0))

    def resident(rows, cols):
        return pl.BlockSpec((None, rows, cols), lambda i: (layer, 0, 0), pipeline_mode=pl.Buffered(1))

    return pl.pallas_call(
        functools.partial(_out_mlp_kernel, final_norm=final_norm, tf=tf),
        grid=(m // tm,),
        in_specs=[pl.BlockSpec((tm, d), lambda i: (i, 0)), gspec, gspec, gspec, gspec,
                  resident(d, d), row, resident(d, f), resident(f, d), row],
        out_specs=pl.BlockSpec((tm, d), lambda i: (i, 0)),
        out_shape=jax.ShapeDtypeStruct((m, d), jnp.float32),
        compiler_params=_params("parallel"),
        name="out_proj_mlp_residual",
    )(x, *groups, w_out, g, w_up, w_down, g_final)


def _t5_bucket(dist):
    n = jnp.maximum(dist, 0)
    max_exact = N_BUCKETS // 2
    nf = jnp.maximum(n, 1).astype(jnp.float32)
    large = max_exact + (jnp.log(nf / max_exact) / math.log(MAX_DISTANCE / max_exact)
                         * (N_BUCKETS - max_exact)).astype(jnp.int32)
    large = jnp.minimum(large, N_BUCKETS - 1)
    return jnp.where(n < max_exact, n, large)


def _bias_kernel(tab_ref, o_ref, *, row_stride, col_stride, offset, head0):
    nh, tr, tc = o_ref.shape
    for blk in range(tc // LANES):
        rows = lax.broadcasted_iota(jnp.int32, (tr, LANES), 0)
        cols = lax.broadcasted_iota(jnp.int32, (tr, LANES), 1) + (pl.program_id(0) * tc + blk * LANES)
        bucket = _t5_bucket(rows * row_stride + cols * col_stride + offset)
        for h in range(nh):
            row = tab_ref[head0 + h:head0 + h + 1, :]
            row = (row - row[:, N_BUCKETS - 1:N_BUCKETS]) * LOG2E
            o_ref[h, :, blk * LANES:(blk + 1) * LANES] = jnp.take_along_axis(
                jnp.broadcast_to(row, (tr, LANES)), bucket, axis=1, mode="promise_in_bounds")


def _bias_table(table_t, *, n_heads, head0, rows, cols, col_tile, row_stride, col_stride, offset):
    return pl.pallas_call(
        functools.partial(_bias_kernel, row_stride=row_stride, col_stride=col_stride,
                          offset=offset, head0=head0),
        grid=(cols // col_tile,),
        in_specs=[pl.BlockSpec(table_t.shape, lambda i: (0, 0))],
        out_specs=pl.BlockSpec((n_heads, rows, col_tile), lambda i: (0, 0, i)),
        out_shape=jax.ShapeDtypeStruct((n_heads, rows, cols), jnp.float32),
        compiler_params=_params("parallel"),
        name="t5_bias_tiles",
    )(table_t)


def _softmax_init(t):
    return (jnp.full((1, t), NEG_INF, jnp.float32), jnp.zeros((ACC_ROWS, t), jnp.float32))


class _KeyTile(NamedTuple):
    kts: Sequence[Any]
    vts: Sequence[Any]
    biases: Optional[Sequence[Any]] = None
    emasks: Optional[Sequence[Any]] = None
    qmasks: Optional[Sequence[Any]] = None


def _round_robin(lists):
    out = []
    for rank in range(max(map(len, lists), default=0)):
        out.extend(items[rank] for items in lists if rank < len(items))
    return out


def _softmax_jobs(t, jobs):
    built = {}

    def tiles_of(job, g):
        if (job, g) not in built:
            built[job, g] = [make() for make in jobs[job][2][g]]
        return built[job, g]

    def scores_of(job, g, c):
        qts = jobs[job][1]
        row = []
        for tile in tiles_of(job, g):
            s = _dot(tile.kts[c], qts[c])
            if tile.biases is not None:
                s = s + tile.biases[c]
            if tile.emasks is not None:
                s = jnp.where(tile.emasks[c], s, NEG_INF)
            row.append(s.astype(_MXU))
        return row

    def update(state, job, g, c, scores):
        m, acc = state
        m_new = m
        for tile, s in zip(tiles_of(job, g), scores):
            tile_max = jnp.max(s, axis=0, keepdims=True).astype(jnp.float32)
            if tile.qmasks is not None:
                tile_max = jnp.where(tile.qmasks[c], tile_max, NEG_INF)
            m_new = jnp.maximum(m_new, tile_max)
        seen = m_new > 0.5 * NEG_INF
        acc = jnp.exp2(m - m_new) * acc
        for tile, s in zip(tiles_of(job, g), scores):
            ok = seen if tile.qmasks is None else jnp.logical_and(seen, tile.qmasks[c])
            acc = acc + _dot(tile.vts[c], jnp.exp2(s - jnp.where(ok, m_new, BIG).astype(_MXU)))
        return m_new, acc

    units = _round_robin([[(job, g, c) for g in range(len(groups)) for c in range(n)]
                          for job, (n, _, groups) in enumerate(jobs)])
    lookahead = max(LOOKAHEAD, 3 * len(jobs) // 4)
    states = [[_softmax_init(t) for _ in range(n)] for n, _, _ in jobs]
    pending = {k: scores_of(*units[k]) for k in range(min(lookahead, len(units)))}
    for k, (job, g, c) in enumerate(units):
        if k + lookahead < len(units):
            pending[k + lookahead] = scores_of(*units[k + lookahead])
        states[job][c] = update(states[job][c], job, g, c, pending.pop(k))
        yield
    return states


def _pairs(items):
    return [items[p:p + 2] for p in range(0, len(items), 2)]


class _Mixer(NamedTuple):
    emit: Callable[..., Any]
    in_specs: Sequence[Any]
    operands: Sequence[Any]
    scratch: Sequence[Any]


def _mixer_kernel(*refs, emit):
    for _ in emit(*refs):
        pass


def _run_mixer(mixer, bsz, s, name):
    return pl.pallas_call(
        functools.partial(_mixer_kernel, emit=mixer.emit),
        grid=(bsz,),
        in_specs=list(mixer.in_specs),
        out_specs=pl.BlockSpec((1, s, GROUP_WIDTH), lambda b: (b, 0, 0)),
        out_shape=jax.ShapeDtypeStruct((bsz, s, GROUP_WIDTH), _MXU),
        scratch_shapes=list(mixer.scratch),
        compiler_params=_params("parallel"),
        name=name,
    )(*mixer.operands)


def _softmax_out(state):
    acc = state[1]
    return acc[:HEAD_DIM] / jnp.maximum(acc[HEAD_DIM:HEAD_DIM + 1], TINY)


def _top_k_rows(score, row_f, k):
    sel = jnp.zeros(score.shape, jnp.float32)
    for _ in range(k):
        mx = jnp.max(score, axis=0, keepdims=True)
        idx = jnp.min(jnp.where(score == mx, row_f, float(score.shape[0])), axis=0, keepdims=True)
        pick = row_f == idx
        sel = jnp.where(pick, 1.0, sel)
        score = jnp.where(pick, PICKED, score)
    return sel


def _now(fn):
    fn()


def _tile_iotas(t):
    return (lax.broadcasted_iota(jnp.int32, (t, t), 0), lax.broadcasted_iota(jnp.int32, (t, t), 1))


def _key_rows(ref, j, t):
    if isinstance(j, int):
        return ref[j * t:(j + 1) * t, :]
    return ref[pl.ds(pl.multiple_of(j * t, t), t), :]


def _transposed(ref_block):
    return ref_block.astype(jnp.float32).T


def _queries_t(q_ref, width, t):
    qt = _transposed(q_ref[0]).astype(_MXU)
    return [[qt[c * width:(c + 1) * width, r * t:(r + 1) * t] for c in range(GROUP_WIDTH // width)]
            for r in range(Q_TILES)]


def _fill_values_t(vt_ref, v_ref, col0, t):
    n_tiles, rows, _ = vt_ref.shape
    lane_block = (col0 // LANES) * LANES
    for c in range(n_tiles):
        blk = _transposed(v_ref[0, c * t:(c + 1) * t, lane_block:lane_block + LANES])
        vt_ref[c, 0:HEAD_DIM, :] = blk[col0 - lane_block:col0 - lane_block + HEAD_DIM].astype(vt_ref.dtype)
        if rows == ACC_ROWS:
            first = lax.broadcasted_iota(jnp.int32, (rows - HEAD_DIM, t), 0) == 0
            vt_ref[c, HEAD_DIM:rows, :] = jnp.where(first, 1.0, 0.0).astype(vt_ref.dtype)


def _group_specs(s, *column_blocks):
    return [pl.BlockSpec((1, s, GROUP_WIDTH), functools.partial(lambda cb, b: (b, 0, cb), cb))
            for cb in column_blocks]


def _bias_specs(t, head_group):
    spec = pl.BlockSpec((GROUP_HEADS, t, t), lambda b: (head_group, 0, 0))
    return [spec, spec]


def _store_heads(o_ref, outs_t):
    tiles = [jnp.concatenate(heads, axis=0) for heads in outs_t]
    o_ref[0] = jnp.concatenate(tiles, axis=1).T.astype(o_ref.dtype)


def _sb_emit(q_ref, k_ref, v_ref, o_ref, kb_ref, vt_ref):
    t = q_ref.shape[1] // Q_TILES
    key, qry = _tile_iotas(t)
    strict = key < qry
    later = jnp.where(qry > key, 1.0, 0.0).astype(_MXU)

    @_now
    def _():
        for h in _HEADS:
            kb_ref[h] = k_ref[0, :, h * HEAD_DIM:(h + 1) * HEAD_DIM].astype(kb_ref.dtype)
            _fill_values_t(vt_ref.at[h], v_ref, h * HEAD_DIM, t)

    def query_step(step):
        qts = _queries_t(q_ref, HEAD_DIM, t)
        units = _round_robin([[(r, step * Q_TILES + r, pair, h)
                               for pair in _pairs(list(range(step * Q_TILES + r, -1, -1))) for h in _HEADS]
                              for r in range(Q_TILES)])
        zero = (jnp.zeros((HEAD_DIM, t), jnp.float32), jnp.zeros((1, t), jnp.float32))
        carry = [[zero] * GROUP_HEADS for _ in range(Q_TILES)]
        zs, log_keeps, suffixes = {}, {}, {}

        def scores(u):
            r, _, pair, h = units[u]
            zs[u] = [_dot(_key_rows(kb_ref.at[h], j, t), qts[r][h]) for j in pair]

        def keeps(u):
            _, k, pair, _ = units[u]
            log_keeps[u], suffixes[u] = [], []
            for j, z in zip(pair, zs[u]):
                drop = jnp.maximum(jnp.log2(1.0 + jnp.exp2(jnp.minimum(z, SOFTPLUS_CLAMP))), z)
                if j == k:
                    drop = jnp.where(strict, drop, 0.0)
                log_keeps[u].append(drop)
                suffixes[u].append(_dot(later, drop.astype(_MXU)))

        def values(u):
            r, k, pair, h = units[u]
            acc, run = carry[r][h]
            weights = []
            for j, z, drop, suffix in zip(pair, zs.pop(u), log_keeps.pop(u), suffixes.pop(u)):
                a = jnp.exp2(z - drop - suffix + run)
                if j == k:
                    a = jnp.where(strict, a, 0.0)
                weights.append(a.astype(_MXU))
                run = run - (suffix[0:1] + drop[0:1])
            for j, w in zip(pair, weights):
                acc = acc + _dot(vt_ref[h, j], w)
            carry[r][h] = (acc, run)

        stages = (scores, keeps, values)
        for tick in range(len(units) + len(stages) - 1):
            for lag, stage in enumerate(stages):
                if 0 <= tick - lag < len(units):
                    stage(tick - lag)
            yield
        _store_heads(o_ref, [[c[0] for c in tile_carry] for tile_carry in carry])

    yield from query_step(0)


def _stick_breaking(proj, t=TILE):
    s = proj.shape[1]
    scratch = [pltpu.VMEM((GROUP_HEADS, s, HEAD_DIM), _MXU), pltpu.VMEM((GROUP_HEADS, s // t, HEAD_DIM, t), _MXU)]
    return _Mixer(_sb_emit, _group_specs(s, CB_SB_Q, CB_SB_K, CB_SB_V), [proj] * 3, scratch)


def _moba_emit(q_ref, k_ref, v_ref, bd_ref, bs_ref, o_ref, kb_ref, vt_ref, km_ref):
    t = q_ref.shape[1] // Q_TILES
    n_blk = k_ref.shape[1] // MOBA_BLOCK
    tiles_per_blk = MOBA_BLOCK // t
    blk_shift = int(math.log2(tiles_per_blk))

    @_now
    def _():
        km_ref[...] = jnp.zeros_like(km_ref)
        for h in _HEADS:
            lo, hi = h * HEAD_DIM, (h + 1) * HEAD_DIM
            kb_ref[h] = k_ref[0, :, lo:hi].astype(kb_ref.dtype)
            _fill_values_t(vt_ref.at[h], v_ref, lo, t)
            for n in range(n_blk):
                blk = k_ref[0, n * MOBA_BLOCK:(n + 1) * MOBA_BLOCK, lo:hi]
                km_ref[h, n:n + 1, :] = jnp.mean(blk.astype(jnp.float32), axis=0, keepdims=True)

    def query_step(step):
        key, qry = _tile_iotas(t)
        causal = key <= qry
        blk_row = lax.broadcasted_iota(jnp.int32, (km_ref.shape[1], t), 0)
        qt = _transposed(q_ref[0])
        jobs = []
        for r in range(Q_TILES):
            k = step * Q_TILES + r
            own = k >> blk_shift
            qts, sels = [], []
            for h in _HEADS:
                qf = qt[h * HEAD_DIM:(h + 1) * HEAD_DIM, r * t:(r + 1) * t]
                qts.append(qf.astype(_MXU))
                gate = _dot(km_ref[h], qf, precision=lax.Precision.HIGHEST)
                gate = jnp.where(blk_row < own, gate, NEG_INF)
                sel = _top_k_rows(gate, blk_row.astype(jnp.float32), min(MOBA_TOPK, n_blk - 1))
                sels.append(jnp.where(blk_row < own, sel, 0.0))

            def tile(own, sels, j, bias_ref=None, emask=None):
                n = j >> blk_shift
                return _KeyTile([_key_rows(kb_ref.at[h], j, t) for h in _HEADS], [vt_ref[h, j] for h in _HEADS],
                                None if bias_ref is None else [bias_ref[h] for h in _HEADS],
                                None if emask is None else [emask] * GROUP_HEADS,
                                None if n == own else [sels[h][n:n + 1] > 0.5 for h in _HEADS])

            tile = functools.partial(tile, own, sels)
            groups = _pairs([functools.partial(tile, j) for j in range(k - 1)])
            groups.append(([functools.partial(tile, k - 1, bs_ref)] if k else [])
                          + [functools.partial(tile, k, bd_ref, causal)])
            jobs.append((GROUP_HEADS, qts, groups))
        all_states = yield from _softmax_jobs(t, jobs)
        _store_heads(o_ref, [[_softmax_out(st) for st in states] for states in all_states])

    yield from query_step(0)


def _kv_scratch(s, t, key_dim=HEAD_DIM, n_keys=GROUP_HEADS):
    return [pltpu.VMEM((n_keys, s, key_dim), _MXU), pltpu.VMEM((GROUP_HEADS, s // t, ACC_ROWS, t), _MXU)]


def _moba(proj, bias_diag, bias_sub, t=TILE):
    s = proj.shape[1]
    n_blk_pad = -(-(s // MOBA_BLOCK) // SUBLANES) * SUBLANES
    scratch = _kv_scratch(s, t) + [pltpu.VMEM((GROUP_HEADS, n_blk_pad, HEAD_DIM), jnp.float32)]
    return _Mixer(_moba_emit, _group_specs(s, CB_MB_Q, CB_MB_K, CB_MB_V) + _bias_specs(t, 0),
                  [proj] * 3 + [bias_diag, bias_sub], scratch)


def _diff_emit(lam_ref, g_ref, q_ref, k_ref, v_ref, bd_ref, bs_ref, o_ref, kb_ref, vt_ref, *, lambda_init):
    t = q_ref.shape[1] // Q_TILES
    key, qry = _tile_iotas(t)
    causal = key <= qry
    lv = lam_ref[...]
    lam = (jnp.exp(jnp.sum(lv[0:1] * lv[1:2], keepdims=True))
           - jnp.exp(jnp.sum(lv[2:3] * lv[3:4], keepdims=True)) + lambda_init)
    halves = range(2 * GROUP_HEADS)

    @_now
    def _():
        for c in halves:
            kb_ref[c] = k_ref[0, :, c * DIFF_HALF:(c + 1) * DIFF_HALF].astype(kb_ref.dtype)
        for h in _HEADS:
            _fill_values_t(vt_ref.at[h], v_ref, h * HEAD_DIM, t)

    def tile(j, bias_ref=None, emask=None):
        n = len(halves)
        return _KeyTile([_key_rows(kb_ref.at[c], j, t) for c in halves], [vt_ref[c // 2, j] for c in halves],
                        None if bias_ref is None else [bias_ref[c // 2] for c in halves],
                        None if emask is None else [emask] * n)

    def query_step(step):
        qts = _queries_t(q_ref, DIFF_HALF, t)
        jobs = []
        for r in range(Q_TILES):
            k = step * Q_TILES + r
            groups = _pairs([functools.partial(tile, j) for j in range(k - 1)])
            groups.append(([functools.partial(tile, k - 1, bs_ref)] if k else [])
                          + [functools.partial(tile, k, bd_ref, causal)])
            jobs.append((len(halves), qts[r], groups))
        all_states = yield from _softmax_jobs(t, jobs)
        outs = []
        for states in all_states:
            heads = []
            for h in _HEADS:
                o = _softmax_out(states[2 * h]) - lam * _softmax_out(states[2 * h + 1])
                o = o * lax.rsqrt(jnp.mean(o * o, axis=0, keepdims=True) + NORM_EPS) * g_ref[...]
                heads.append(o * (1.0 - lambda_init))
            outs.append(heads)
        _store_heads(o_ref, outs)

    yield from query_step(0)


def _diff(proj, lam_params, subln, bias_diag, bias_sub, lambda_init, t=TILE):
    s = proj.shape[1]
    in_specs = ([pl.BlockSpec(lam_params.shape, lambda b: (0, 0)), pl.BlockSpec(subln.shape, lambda b: (0, 0))]
                + _group_specs(s, CB_DF_Q, CB_DF_K, CB_DF_V) + _bias_specs(t, 2))
    return _Mixer(functools.partial(_diff_emit, lambda_init=lambda_init), in_specs,
                  [lam_params, subln] + [proj] * 3 + [bias_diag, bias_sub],
                  _kv_scratch(s, t, key_dim=DIFF_HALF, n_keys=2 * GROUP_HEADS))


def _compress_blocks(kcv_ref, pk_ref, pv_ref, wk1_ref, wk2_ref, wv1_ref, wv2t_ref):
    n_chunk = kcv_ref.shape[1] // CMP_STRIDE

    branches = ((0, pk_ref, wk1_ref), (HEAD_DIM, pv_ref, wv1_ref))
    tops = [jnp.zeros((n_chunk, w1_ref.shape[1]), jnp.float32) for _, _, w1_ref in branches]
    bots = list(tops)
    pack = 2 * LANES // HEAD_DIM
    for l0 in range(0, CMP_STRIDE, pack):
        tokens = [kcv_ref[0, pl.ds(l, n_chunk, stride=CMP_STRIDE), :] for l in range(l0, l0 + pack)]
        for n, (col0, p_ref, w1_ref) in enumerate(branches):
            for half, acc in ((0, tops), (CMP_STRIDE, bots)):
                x = jnp.concatenate([(tok[:, col0:col0 + HEAD_DIM] + p_ref[half + l:half + l + 1, :]).astype(_MXU)
                                     for l, tok in zip(range(l0, l0 + pack), tokens)], axis=1)
                acc[n] = acc[n] + _dot(x, w1_ref[(half + l0) * HEAD_DIM:(half + l0 + pack) * HEAD_DIM, :])
    hidden = [jax.nn.gelu(top + pltpu.roll(bot, n_chunk - 1, axis=0)).astype(_MXU) for top, bot in zip(tops, bots)]
    return _dot(hidden[0], wk2_ref[...]), _dot_nt(wv2t_ref[...], hidden[1])


def _nsa_emit(q_ref, kva_ref, kvb_ref, kcv_ref, pk_ref, pv_ref, wk1_ref, wk2_ref, wv1_ref, wv2t_ref,
              bd_ref, bs_ref, bc_ref, cover_ref, e_ref, o_ref, ks_ref, vst_ref, kw_ref, vwt_ref):
    t = q_ref.shape[1] // Q_TILES
    key, qry = _tile_iotas(t)
    causal = key <= qry
    ks_col, vs_col, kw_col, vw_col, gate_col = 2 * HEAD_DIM, 3 * HEAD_DIM, 0, HEAD_DIM, 2 * HEAD_DIM

    @_now
    def _():
        ks_ref[:, :HEAD_DIM] = kva_ref[0, :, ks_col:ks_col + HEAD_DIM].astype(ks_ref.dtype)
        ks_ref[:, HEAD_DIM:] = e_ref[...]
        kw_ref[...] = kvb_ref[0, :, kw_col:kw_col + HEAD_DIM].astype(kw_ref.dtype)
        _fill_values_t(vst_ref, kva_ref, vs_col, t)
        _fill_values_t(vwt_ref, kvb_ref, vw_col, t)

    def tile(k_ref, vt_ref, j, bias_ref=None, emask=None):
        n = GROUP_HEADS
        return _KeyTile([_key_rows(k_ref, j, t)] * n, [vt_ref[j]] * n,
                        None if bias_ref is None else [bias_ref[h] for h in _HEADS],
                        None if emask is None else [emask] * n)

    def query_step(step):
        qts = _queries_t(q_ref, HEAD_DIM, t)
        kc, vct = (a.astype(_MXU) for a in _compress_blocks(kcv_ref, pk_ref, pv_ref, wk1_ref, wk2_ref,
                                                            wv1_ref, wv2t_ref))
        n_cmp = kc.shape[0]
        n_slc = cover_ref.shape[0]
        c_row = lax.broadcasted_iota(jnp.int32, (n_cmp, t), 0)
        c_col = lax.broadcasted_iota(jnp.int32, (n_cmp, t), 1)
        s_row = lax.broadcasted_iota(jnp.int32, (n_slc, t), 0)
        s_col = lax.broadcasted_iota(jnp.int32, (n_slc, t), 1)
        tiles = [step * Q_TILES + r for r in range(Q_TILES)]

        o_cmp, importance = [], []
        for r, k in enumerate(tiles):
            visible = c_col + k * t >= c_row * CMP_STRIDE + (CMP_LEN - 1)
            cmp_scores = [_dot(kc, qts[r][h]) for h in _HEADS]
            cmp_probs = []
            p_sum = jnp.zeros((n_cmp, t), jnp.float32)
            for h in _HEADS:
                sc = jnp.where(visible, cmp_scores[h] + bc_ref[h, :, r * t:(r + 1) * t], NEG_INF)
                e = jnp.where(visible, jnp.exp2(sc - jnp.max(sc, axis=0, keepdims=True)), 0.0)
                p = e / jnp.maximum(jnp.sum(e, axis=0, keepdims=True), TINY)
                cmp_probs.append(p.astype(_MXU))
                p_sum = p_sum + p
            o_cmp.append([_dot(vct, cmp_probs[h]) for h in _HEADS])
            importance.append(_dot(cover_ref[...], p_sum, precision=lax.Precision.HIGHEST))

        n_back = WINDOW // t
        jobs = []
        for r, k in enumerate(tiles):
            window = [functools.partial(tile, kw_ref, vwt_ref, k - n_back, None, qry < key)] if k >= n_back else []
            for back in range(min(n_back - 1, k), 0, -1):
                window.append(functools.partial(tile, kw_ref, vwt_ref, k - back, bs_ref if back == 1 else None))
            window.append(functools.partial(tile, kw_ref, vwt_ref, k, bd_ref, causal))
            jobs.append((GROUP_HEADS, qts[r], [window]))
        win_states = yield from _softmax_jobs(t, jobs)
        o_win = [[_softmax_out(st) for st in states] for states in win_states]

        jobs = []
        for r, k in enumerate(tiles):
            own = jnp.right_shift(s_col + k * t, int(math.log2(SLC_LEN)))
            score = jnp.where(s_row == own, FORCE, jnp.where(s_row < own, importance[r], NEG_INF))
            sel = _top_k_rows(score, s_row.astype(jnp.float32), min(SLC_TOPN, n_slc))
            penalty = jnp.where(sel > 0.5, 0.0, NEG_INF).astype(_MXU)
            q_aug = [jnp.concatenate([qts[r][h], penalty], axis=0) for h in _HEADS]
            groups = _pairs([functools.partial(tile, ks_ref, vst_ref, j) for j in range(k - 1)])
            groups.append(([functools.partial(tile, ks_ref, vst_ref, k - 1, bs_ref)] if k else [])
                          + [functools.partial(tile, ks_ref, vst_ref, k, bd_ref, causal)])
            jobs.append((GROUP_HEADS, q_aug, groups))
        slc_states = yield from _softmax_jobs(t, jobs)
        o_slc = [[_softmax_out(st) for st in states] for states in slc_states]

        gates = _transposed(kvb_ref[0, :, (gate_col // LANES) * LANES:(gate_col // LANES + 1) * LANES])
        gates = 1.0 / (1.0 + jnp.exp(-gates[gate_col % LANES:gate_col % LANES + N_GATES + 4]))
        outs = []
        for r in range(Q_TILES):
            heads = []
            for h in _HEADS:
                g = [gates[br * GROUP_HEADS + h:br * GROUP_HEADS + h + 1, r * t:(r + 1) * t] for br in range(3)]
                heads.append(g[0] * o_cmp[r][h] + g[1] * o_slc[r][h] + g[2] * o_win[r][h])
            outs.append(heads)
        _store_heads(o_ref, outs)

    yield from query_step(0)


def _nsa(proj, kcv, cmp_params, layer, bias_diag, bias_sub, bias_cmp, cover_t, expand, t=TILE):
    s = proj.shape[1]
    values_t = pltpu.VMEM((s // t, ACC_ROWS, t), _MXU)
    in_specs = (_group_specs(s, CB_NS_Q, CB_NS_A, CB_NS_B)
                + [pl.BlockSpec((1, s, kcv.shape[2]), lambda b: (b, 0, 0))]
                + [pl.BlockSpec((None,) + a.shape[1:], lambda b: (layer, 0, 0)) for a in cmp_params]
                + _bias_specs(t, 1)
                + [pl.BlockSpec(bias_cmp.shape, lambda b: (0, 0, 0)),
                   pl.BlockSpec(cover_t.shape, lambda b: (0, 0)),
                   pl.BlockSpec(expand.shape, lambda b: (0, 0))])
    scratch = [pltpu.VMEM((s, HEAD_DIM + expand.shape[1]), _MXU), values_t, pltpu.VMEM((s, HEAD_DIM), _MXU), values_t]
    return _Mixer(_nsa_emit, in_specs,
                  [proj] * 3 + [kcv, *cmp_params, bias_diag, bias_sub, bias_cmp, cover_t, expand], scratch)


def _nsa_constants(s):
    n_cmp = (s - CMP_LEN) // CMP_STRIDE + 1
    n_slc = s // SLC_LEN
    assert n_cmp + 1 == s // CMP_STRIDE and n_slc % SUBLANES == 0
    c_start = np.arange(n_cmp) * CMP_STRIDE
    s_start = np.arange(n_slc) * SLC_LEN
    cover = np.clip(np.minimum((c_start + CMP_LEN - 1)[:, None], (s_start + SLC_LEN - 1)[None, :])
                    - np.maximum(c_start[:, None], s_start[None, :]) + 1, 0, None) / CMP_LEN
    cover_t = np.zeros((n_slc, n_cmp + 1), np.float32)
    cover_t[:, :n_cmp] = cover.T
    expand = (np.arange(s)[:, None] // SLC_LEN == np.arange(n_slc)[None, :]).astype(np.float32)
    return jnp.asarray(cover_t), jnp.asarray(expand, _MXU)


def kernel(x, w_in, w_out, w_up, w_down, norm_attn, norm_mlp, cmp_pos_k, cmp_pos_v, cmp_k_w1, cmp_k_w2,
           cmp_v_w1, cmp_v_w2, diff_lambda, diff_subln, rel_bias, final_norm):
    bsz, s, d = x.shape
    depth = w_in.shape[0]
    t = TILE
    n_chunk = s // CMP_STRIDE
    assert s == Q_TILES * t and MOBA_BLOCK % t == 0 and WINDOW % t == 0 and t >= MAX_DISTANCE

    assert w_in.shape[2] + PAD_COLS == D_IN_PAD
    w_in_c = jnp.pad(w_in.astype(_MXU), ((0, 0), (0, 0), (0, PAD_COLS)))
    w_out_c, w_up_c, w_down_c = (w.astype(_MXU) for w in (w_out, w_up, w_down))
    wk1, wk2 = cmp_k_w1.astype(_MXU), cmp_k_w2.astype(_MXU)
    wv1, wv2t = cmp_v_w1.astype(_MXU), jnp.swapaxes(cmp_v_w2, 1, 2).astype(_MXU)

    table_t = jnp.pad(rel_bias.T, ((0, 0), (0, LANES - N_BUCKETS)))
    tiles = dict(n_heads=rel_bias.shape[1], head0=0, rows=t, cols=t, col_tile=t, row_stride=-1, col_stride=1)
    bias_diag = _bias_table(table_t, offset=0, **tiles)
    bias_sub = _bias_table(table_t, offset=t, **tiles)
    bias_cmp = _bias_table(table_t, n_heads=GROUP_HEADS, head0=GROUP_HEADS, rows=n_chunk, cols=s,
                           col_tile=t, row_stride=-CMP_STRIDE, col_stride=1, offset=-(CMP_LEN - 1))
    cover_t, expand = _nsa_constants(s)
    col_scale = np.ones((1, D_IN_PAD), np.float32)
    for cb, width in ((CB_SB_Q, HEAD_DIM), (CB_MB_Q, HEAD_DIM), (CB_NS_Q, HEAD_DIM), (CB_DF_Q, DIFF_HALF)):
        col_scale[:, cb * GROUP_WIDTH:(cb + 1) * GROUP_WIDTH] = width ** -0.5 * LOG2E
    col_scale = jnp.asarray(col_scale)

    x2 = x.reshape(bsz * s, d)
    for layer in range(depth):
        proj, kcv = _norm_matmul(x2, norm_attn[layer][None], w_in_c, layer, col_scale,
                                 gap=(COLS_BEFORE_PAD, COLS_BEFORE_PAD + PAD_COLS),
                                 side_col=CB_NS_A * GROUP_WIDTH, side_width=2 * HEAD_DIM)
        proj = proj.reshape(bsz, s, D_IN_PAD)
        lambda_init = 0.8 - 0.6 * math.exp(-0.3 * layer)
        o_sb = _run_mixer(_stick_breaking(proj), bsz, s, "stick_breaking")
        o_mb = _run_mixer(_moba(proj, bias_diag, bias_sub), bsz, s, "moba")
        o_ns = _run_mixer(_nsa(proj, kcv.reshape(bsz, s, 2 * HEAD_DIM), (cmp_pos_k, cmp_pos_v, wk1, wk2, wv1, wv2t),
                               layer, bias_diag, bias_sub, bias_cmp, cover_t, expand), bsz, s, "nsa")
        o_df = _run_mixer(_diff(proj, diff_lambda[layer], diff_subln[layer][:, None], bias_diag, bias_sub,
                                lambda_init), bsz, s, "diff_attention")
        groups = [o.reshape(bsz * s, GROUP_WIDTH) for o in (o_sb, o_mb, o_ns, o_df)]
        x2 = _out_mlp(x2, groups, w_out_c, norm_mlp[layer][None], w_up_c, w_down_c, layer,
                      final_norm[None], final_norm=(layer == depth - 1))
    return x2.reshape(bsz, s, d)
```

```python
import functools
import math
from typing import Any, Callable, NamedTuple, Optional, Sequence

import numpy as np
import jax
import jax.numpy as jnp
from jax import lax
from jax.experimental import pallas as pl
from jax.experimental.pallas import tpu as pltpu

HEAD_DIM = 64
GROUP_HEADS = 4
GROUP_WIDTH = GROUP_HEADS * HEAD_DIM
NORM_EPS = 1e-6
NEG_INF = -1e30
BIG = 1e30
FORCE = 1e30
TINY = 1e-30
SOFTPLUS_CLAMP = 64.0
PICKED = -3e38
LOG2E = math.log2(math.e)
N_BUCKETS = 32
MAX_DISTANCE = 128
MOBA_BLOCK = 256
MOBA_TOPK = 3
CMP_LEN = 32
CMP_STRIDE = 16
SLC_LEN = 64
SLC_TOPN = 4
WINDOW = 512
DIFF_HALF = HEAD_DIM // 2
LANES = 128
SUBLANES = 8
TILE = 256
BF16_ROWS = 16
ACC_ROWS = HEAD_DIM + BF16_ROWS
N_GATES = 3 * GROUP_HEADS
COLS_BEFORE_PAD = 9 * GROUP_WIDTH - 2 * HEAD_DIM + N_GATES
PAD_COLS = 2 * HEAD_DIM - N_GATES
CB_SB_Q, CB_SB_K, CB_SB_V, CB_MB_Q, CB_MB_K, CB_MB_V, CB_NS_Q, CB_NS_A, CB_NS_B, CB_DF_Q, CB_DF_K, CB_DF_V = range(12)
D_IN_PAD = 12 * GROUP_WIDTH

_MXU = jnp.bfloat16
_VMEM_LIMIT = 56 * 1024 * 1024
_HEADS = range(GROUP_HEADS)
Q_TILES = 8
LOOKAHEAD = 2


def _dot(a, b, precision=None):
    return jnp.dot(a, b, precision=precision, preferred_element_type=jnp.float32)


def _dot_nt(a, b, precision=None):
    return lax.dot_general(a, b, (((1,), (1,)), ((), ())), precision=precision,
                           preferred_element_type=jnp.float32)


def _rms(x, g):
    return x * lax.rsqrt(jnp.mean(x * x, axis=-1, keepdims=True) + NORM_EPS) * g


def _params(*sem):
    return pltpu.CompilerParams(dimension_semantics=sem, vmem_limit_bytes=_VMEM_LIMIT)


def _norm_matmul_kernel(x_ref, g_ref, w_ref, scale_ref, o_ref, side_ref, wp_ref, *, tn, side_col, gap):
    @pl.when(pl.program_id(0) == 0)
    def _():
        lo, hi = gap
        wp_ref[:, :lo] = w_ref[:, :lo]
        wp_ref[:, lo:hi] = jnp.zeros((wp_ref.shape[0], hi - lo), wp_ref.dtype)
        wp_ref[:, hi:] = w_ref[:, lo:lo + wp_ref.shape[1] - hi]

    h = _rms(x_ref[...], g_ref[...]).astype(_MXU)
    for j in range(wp_ref.shape[1] // tn):
        cols = slice(j * tn, (j + 1) * tn)
        acc = _dot(h, wp_ref[:, cols])
        o_ref[:, cols] = (acc * scale_ref[:, cols]).astype(o_ref.dtype)
        if j * tn <= side_col < (j + 1) * tn:
            side_ref[...] = acc[:, side_col - j * tn:side_col - j * tn + side_ref.shape[1]]


def _norm_matmul(x, g, w, layer, col_scale, *, gap, side_col, side_width, tm=1024, tn=1024):
    m, d = x.shape
    n = w.shape[2]
    assert side_col % LANES == 0 and side_col // tn == (side_col + side_width - 1) // tn
    return pl.pallas_call(
        functools.partial(_norm_matmul_kernel, tn=tn, side_col=side_col, gap=gap),
        grid=(m // tm,),
        in_specs=[pl.BlockSpec((tm, d), lambda i: (i, 0)),
                  pl.BlockSpec((1, d), lambda i: (0, 0)),
                  pl.BlockSpec((None, d, w.shape[2]), lambda i: (layer, 0, 0), pipeline_mode=pl.Buffered(1)),
                  pl.BlockSpec((1, n), lambda i: (0, 0))],
        out_specs=[pl.BlockSpec((tm, n), lambda i: (i, 0)),
                   pl.BlockSpec((tm, side_width), lambda i: (i, 0))],
        out_shape=[jax.ShapeDtypeStruct((m, n), _MXU),
                   jax.ShapeDtypeStruct((m, side_width), jnp.float32)],
        scratch_shapes=[pltpu.VMEM((d, n), _MXU)],
        compiler_params=_params("arbitrary"),
        name="norm_in_proj",
    )(x, g, w, col_scale)


def _out_mlp_kernel(x_ref, a_ref, b_ref, c_ref, d_ref, wo_ref, g_ref, wu_ref, wd_ref, gf_ref, o_ref,
                    *, final_norm, tf):
    mixed = jnp.concatenate([a_ref[...], b_ref[...], c_ref[...], d_ref[...]], axis=1)
    y = x_ref[...] + _dot(mixed, wo_ref[...])
    h = _rms(y, g_ref[...]).astype(_MXU)
    for c in range(wu_ref.shape[1] // tf):
        u = jnp.square(jnp.maximum(_dot(h, wu_ref[:, c * tf:(c + 1) * tf]), 0.0))
        y = y + _dot(u.astype(_MXU), wd_ref[c * tf:(c + 1) * tf, :])
    if final_norm:
        y = _rms(y, gf_ref[...])
    o_ref[...] = y


def _out_mlp(x, groups, w_out, g, w_up, w_down, layer, g_final, *, final_norm, tm=1024, tf=2048):
    m, d = x.shape
    f = w_up.shape[2]
    gspec = pl.BlockSpec((tm, GROUP_WIDTH), lambda i: (i, 0))
    row = pl.BlockSpec((1, d), lambda i: (0, 0))

    def resident(rows, cols):
        return pl.BlockSpec((None, rows, cols), lambda i: (layer, 0, 0), pipeline_mode=pl.Buffered(1))

    return pl.pallas_call(
        functools.partial(_out_mlp_kernel, final_norm=final_norm, tf=tf),
        grid=(m // tm,),
        in_specs=[pl.BlockSpec((tm, d), lambda i: (i, 0)), gspec, gspec, gspec, gspec,
                  resident(d, d), row, resident(d, f), resident(f, d), row],
        out_specs=pl.BlockSpec((tm, d), lambda i: (i, 0)),
        out_shape=jax.ShapeDtypeStruct((m, d), jnp.float32),
        compiler_params=_params("parallel"),
        name="out_proj_mlp_residual",
    )(x, *groups, w_out, g, w_up, w_down, g_final)


def _t5_bucket(dist):
    n = jnp.maximum(dist, 0)
    max_exact = N_BUCKETS // 2
    nf = jnp.maximum(n, 1).astype(jnp.float32)
    large = max_exact + (jnp.log(nf / max_exact) / math.log(MAX_DISTANCE / max_exact)
                         * (N_BUCKETS - max_exact)).astype(jnp.int32)
    large = jnp.minimum(large, N_BUCKETS - 1)
    return jnp.where(n < max_exact, n, large)


def _bias_kernel(tab_ref, o_ref, *, row_stride, col_stride, offset, head0):
    nh, tr, tc = o_ref.shape
    for blk in range(tc // LANES):
        rows = lax.broadcasted_iota(jnp.int32, (tr, LANES), 0)
        cols = lax.broadcasted_iota(jnp.int32, (tr, LANES), 1) + (pl.program_id(0) * tc + blk * LANES)
        bucket = _t5_bucket(rows * row_stride + cols * col_stride + offset)
        for h in range(nh):
            row = tab_ref[head0 + h:head0 + h + 1, :]
            row = (row - row[:, N_BUCKETS - 1:N_BUCKETS]) * LOG2E
            o_ref[h, :, blk * LANES:(blk + 1) * LANES] = jnp.take_along_axis(
                jnp.broadcast_to(row, (tr, LANES)), bucket, axis=1, mode="promise_in_bounds")


def _bias_table(table_t, *, n_heads, head0, rows, cols, col_tile, row_stride, col_stride, offset):
    return pl.pallas_call(
        functools.partial(_bias_kernel, row_stride=row_stride, col_stride=col_stride,
                          offset=offset, head0=head0),
        grid=(cols // col_tile,),
        in_specs=[pl.BlockSpec(table_t.shape, lambda i: (0, 0))],
        out_specs=pl.BlockSpec((n_heads, rows, col_tile), lambda i: (0, 0, i)),
        out_shape=jax.ShapeDtypeStruct((n_heads, rows, cols), jnp.float32),
        compiler_params=_params("parallel"),
        name="t5_bias_tiles",
    )(table_t)


def _softmax_init(t):
    return (jnp.full((1, t), NEG_INF, jnp.float32), jnp.zeros((ACC_ROWS, t), jnp.float32))


class _KeyTile(NamedTuple):
    kts: Sequence[Any]
    vts: Sequence[Any]
    biases: Optional[Sequence[Any]] = None
    emasks: Optional[Sequence[Any]] = None
    qmasks: Optional[Sequence[Any]] = None


def _round_robin(lists):
    out = []
    for rank in range(max(map(len, lists), default=0)):
        out.extend(items[rank] for items in lists if rank < len(items))
    return out


def _softmax_jobs(t, jobs):
    built = {}

    def tiles_of(job, g):
        if (job, g) not in built:
            built[job, g] = [make() for make in jobs[job][2][g]]
        return built[job, g]

    def scores_of(job, g, c):
        qts = jobs[job][1]
        row = []
        for tile in tiles_of(job, g):
            s = _dot(tile.kts[c], qts[c])
            if tile.biases is not None:
                s = s + tile.biases[c]
            if tile.emasks is not None:
                s = jnp.where(tile.emasks[c], s, NEG_INF)
            row.append(s.astype(_MXU))
        return row

    def update(state, job, g, c, scores):
        m, acc = state
        m_new = m
        for tile, s in zip(tiles_of(job, g), scores):
            tile_max = jnp.max(s, axis=0, keepdims=True).astype(jnp.float32)
            if tile.qmasks is not None:
                tile_max = jnp.where(tile.qmasks[c], tile_max, NEG_INF)
            m_new = jnp.maximum(m_new, tile_max)
        seen = m_new > 0.5 * NEG_INF
        acc = jnp.exp2(m - m_new) * acc
        for tile, s in zip(tiles_of(job, g), scores):
            ok = seen if tile.qmasks is None else jnp.logical_and(seen, tile.qmasks[c])
            acc = acc + _dot(tile.vts[c], jnp.exp2(s - jnp.where(ok, m_new, BIG).astype(_MXU)))
        return m_new, acc

    units = _round_robin([[(job, g, c) for g in range(len(groups)) for c in range(n)]
                          for job, (n, _, groups) in enumerate(jobs)])
    lookahead = max(LOOKAHEAD, len(jobs) // 2)
    states = [[_softmax_init(t) for _ in range(n)] for n, _, _ in jobs]
    pending = {k: scores_of(*units[k]) for k in range(min(lookahead, len(units)))}
    for k, (job, g, c) in enumerate(units):
        if k + lookahead < len(units):
            pending[k + lookahead] = scores_of(*units[k + lookahead])
        states[job][c] = update(states[job][c], job, g, c, pending.pop(k))
        yield
    return states


def _pairs(items):
    return [items[p:p + 2] for p in range(0, len(items), 2)]


class _Mixer(NamedTuple):
    emit: Callable[..., Any]
    in_specs: Sequence[Any]
    operands: Sequence[Any]
    scratch: Sequence[Any]


def _mixer_kernel(*refs, emit):
    for _ in emit(*refs):
        pass


def _run_mixer(mixer, bsz, s, name):
    return pl.pallas_call(
        functools.partial(_mixer_kernel, emit=mixer.emit),
        grid=(bsz,),
        in_specs=list(mixer.in_specs),
        out_specs=pl.BlockSpec((1, s, GROUP_WIDTH), lambda b: (b, 0, 0)),
        out_shape=jax.ShapeDtypeStruct((bsz, s, GROUP_WIDTH), _MXU),
        scratch_shapes=list(mixer.scratch),
        compiler_params=_params("parallel"),
        name=name,
    )(*mixer.operands)


def _softmax_out(state):
    acc = state[1]
    return acc[:HEAD_DIM] / jnp.maximum(acc[HEAD_DIM:HEAD_DIM + 1], TINY)


def _top_k_rows(score, row_f, k):
    sel = jnp.zeros(score.shape, jnp.float32)
    for _ in range(k):
        mx = jnp.max(score, axis=0, keepdims=True)
        idx = jnp.min(jnp.where(score == mx, row_f, float(score.shape[0])), axis=0, keepdims=True)
        pick = row_f == idx
        sel = jnp.where(pick, 1.0, sel)
        score = jnp.where(pick, PICKED, score)
    return sel


def _now(fn):
    fn()


def _tile_iotas(t):
    return (lax.broadcasted_iota(jnp.int32, (t, t), 0), lax.broadcasted_iota(jnp.int32, (t, t), 1))


def _key_rows(ref, j, t):
    if isinstance(j, int):
        return ref[j * t:(j + 1) * t, :]
    return ref[pl.ds(pl.multiple_of(j * t, t), t), :]


def _transposed(ref_block):
    return ref_block.astype(jnp.float32).T


def _queries_t(q_ref, width, t):
    qt = _transposed(q_ref[0]).astype(_MXU)
    return [[qt[c * width:(c + 1) * width, r * t:(r + 1) * t] for c in range(GROUP_WIDTH // width)]
            for r in range(Q_TILES)]


def _fill_values_t(vt_ref, v_ref, col0, t):
    n_tiles, rows, _ = vt_ref.shape
    lane_block = (col0 // LANES) * LANES
    for c in range(n_tiles):
        blk = _transposed(v_ref[0, c * t:(c + 1) * t, lane_block:lane_block + LANES])
        vt_ref[c, 0:HEAD_DIM, :] = blk[col0 - lane_block:col0 - lane_block + HEAD_DIM].astype(vt_ref.dtype)
        if rows == ACC_ROWS:
            first = lax.broadcasted_iota(jnp.int32, (rows - HEAD_DIM, t), 0) == 0
            vt_ref[c, HEAD_DIM:rows, :] = jnp.where(first, 1.0, 0.0).astype(vt_ref.dtype)


def _group_specs(s, *column_blocks):
    return [pl.BlockSpec((1, s, GROUP_WIDTH), functools.partial(lambda cb, b: (b, 0, cb), cb))
            for cb in column_blocks]


def _bias_specs(t, head_group):
    spec = pl.BlockSpec((GROUP_HEADS, t, t), lambda b: (head_group, 0, 0))
    return [spec, spec]


def _store_heads(o_ref, outs_t):
    tiles = [jnp.concatenate(heads, axis=0) for heads in outs_t]
    o_ref[0] = jnp.concatenate(tiles, axis=1).T.astype(o_ref.dtype)


def _sb_emit(q_ref, k_ref, v_ref, o_ref, kb_ref, vt_ref):
    t = q_ref.shape[1] // Q_TILES
    key, qry = _tile_iotas(t)
    strict = key < qry
    later = jnp.where(qry > key, 1.0, 0.0).astype(_MXU)

    @_now
    def _():
        for h in _HEADS:
            kb_ref[h] = k_ref[0, :, h * HEAD_DIM:(h + 1) * HEAD_DIM].astype(kb_ref.dtype)
            _fill_values_t(vt_ref.at[h], v_ref, h * HEAD_DIM, t)

    def query_step(step):
        qts = _queries_t(q_ref, HEAD_DIM, t)
        units = _round_robin([[(r, step * Q_TILES + r, pair, h)
                               for pair in _pairs(list(range(step * Q_TILES + r, -1, -1))) for h in _HEADS]
                              for r in range(Q_TILES)])
        zero = (jnp.zeros((HEAD_DIM, t), jnp.float32), jnp.zeros((1, t), jnp.float32))
        carry = [[zero] * GROUP_HEADS for _ in range(Q_TILES)]
        zs, log_keeps, suffixes = {}, {}, {}

        def scores(u):
            r, _, pair, h = units[u]
            zs[u] = [_dot(_key_rows(kb_ref.at[h], j, t), qts[r][h]) for j in pair]

        def keeps(u):
            _, k, pair, _ = units[u]
            log_keeps[u], suffixes[u] = [], []
            for j, z in zip(pair, zs[u]):
                drop = jnp.maximum(jnp.log2(1.0 + jnp.exp2(jnp.minimum(z, SOFTPLUS_CLAMP))), z)
                if j == k:
                    drop = jnp.where(strict, drop, 0.0)
                log_keeps[u].append(drop)
                suffixes[u].append(_dot(later, drop.astype(_MXU)))

        def values(u):
            r, k, pair, h = units[u]
            acc, run = carry[r][h]
            weights = []
            for j, z, drop, suffix in zip(pair, zs.pop(u), log_keeps.pop(u), suffixes.pop(u)):
                a = jnp.exp2(z - drop - suffix + run)
                if j == k:
                    a = jnp.where(strict, a, 0.0)
                weights.append(a.astype(_MXU))
                run = run - (suffix[0:1] + drop[0:1])
            for j, w in zip(pair, weights):
                acc = acc + _dot(vt_ref[h, j], w)
            carry[r][h] = (acc, run)

        stages = (scores, keeps, values)
        for tick in range(len(units) + len(stages) - 1):
            for lag, stage in enumerate(stages):
                if 0 <= tick - lag < len(units):
                    stage(tick - lag)
            yield
        _store_heads(o_ref, [[c[0] for c in tile_carry] for tile_carry in carry])

    yield from query_step(0)


def _stick_breaking(proj, t=TILE):
    s = proj.shape[1]
    scratch = [pltpu.VMEM((GROUP_HEADS, s, HEAD_DIM), _MXU), pltpu.VMEM((GROUP_HEADS, s // t, HEAD_DIM, t), _MXU)]
    return _Mixer(_sb_emit, _group_specs(s, CB_SB_Q, CB_SB_K, CB_SB_V), [proj] * 3, scratch)


def _moba_emit(q_ref, k_ref, v_ref, bd_ref, bs_ref, o_ref, kb_ref, vt_ref, km_ref):
    t = q_ref.shape[1] // Q_TILES
    n_blk = k_ref.shape[1] // MOBA_BLOCK
    tiles_per_blk = MOBA_BLOCK // t
    blk_shift = int(math.log2(tiles_per_blk))

    @_now
    def _():
        km_ref[...] = jnp.zeros_like(km_ref)
        for h in _HEADS:
            lo, hi = h * HEAD_DIM, (h + 1) * HEAD_DIM
            kb_ref[h] = k_ref[0, :, lo:hi].astype(kb_ref.dtype)
            _fill_values_t(vt_ref.at[h], v_ref, lo, t)
            for n in range(n_blk):
                blk = k_ref[0, n * MOBA_BLOCK:(n + 1) * MOBA_BLOCK, lo:hi]
                km_ref[h, n:n + 1, :] = jnp.mean(blk.astype(jnp.float32), axis=0, keepdims=True)

    def query_step(step):
        key, qry = _tile_iotas(t)
        causal = key <= qry
        blk_row = lax.broadcasted_iota(jnp.int32, (km_ref.shape[1], t), 0)
        qt = _transposed(q_ref[0])
        jobs = []
        for r in range(Q_TILES):
            k = step * Q_TILES + r
            own = k >> blk_shift
            qts, sels = [], []
            for h in _HEADS:
                qf = qt[h * HEAD_DIM:(h + 1) * HEAD_DIM, r * t:(r + 1) * t]
                qts.append(qf.astype(_MXU))
                gate = _dot(km_ref[h], qf, precision=lax.Precision.HIGHEST)
                gate = jnp.where(blk_row < own, gate, NEG_INF)
                sel = _top_k_rows(gate, blk_row.astype(jnp.float32), min(MOBA_TOPK, n_blk - 1))
                sels.append(jnp.where(blk_row < own, sel, 0.0))

            def tile(own, sels, j, bias_ref=None, emask=None):
                n = j >> blk_shift
                return _KeyTile([_key_rows(kb_ref.at[h], j, t) for h in _HEADS], [vt_ref[h, j] for h in _HEADS],
                                None if bias_ref is None else [bias_ref[h] for h in _HEADS],
                                None if emask is None else [emask] * GROUP_HEADS,
                                None if n == own else [sels[h][n:n + 1] > 0.5 for h in _HEADS])

            tile = functools.partial(tile, own, sels)
            groups = _pairs([functools.partial(tile, j) for j in range(k - 1)])
            groups.append(([functools.partial(tile, k - 1, bs_ref)] if k else [])
                          + [functools.partial(tile, k, bd_ref, causal)])
            jobs.append((GROUP_HEADS, qts, groups))
        all_states = yield from _softmax_jobs(t, jobs)
        _store_heads(o_ref, [[_softmax_out(st) for st in states] for states in all_states])

    yield from query_step(0)


def _kv_scratch(s, t, key_dim=HEAD_DIM, n_keys=GROUP_HEADS):
    return [pltpu.VMEM((n_keys, s, key_dim), _MXU), pltpu.VMEM((GROUP_HEADS, s // t, ACC_ROWS, t), _MXU)]


def _moba(proj, bias_diag, bias_sub, t=TILE):
    s = proj.shape[1]
    n_blk_pad = -(-(s // MOBA_BLOCK) // SUBLANES) * SUBLANES
    scratch = _kv_scratch(s, t) + [pltpu.VMEM((GROUP_HEADS, n_blk_pad, HEAD_DIM), jnp.float32)]
    return _Mixer(_moba_emit, _group_specs(s, CB_MB_Q, CB_MB_K, CB_MB_V) + _bias_specs(t, 0),
                  [proj] * 3 + [bias_diag, bias_sub], scratch)


def _diff_emit(lam_ref, g_ref, q_ref, k_ref, v_ref, bd_ref, bs_ref, o_ref, kb_ref, vt_ref, *, lambda_init):
    t = q_ref.shape[1] // Q_TILES
    key, qry = _tile_iotas(t)
    causal = key <= qry
    lv = lam_ref[...]
    lam = (jnp.exp(jnp.sum(lv[0:1] * lv[1:2], keepdims=True))
           - jnp.exp(jnp.sum(lv[2:3] * lv[3:4], keepdims=True)) + lambda_init)
    halves = range(2 * GROUP_HEADS)

    @_now
    def _():
        for c in halves:
            kb_ref[c] = k_ref[0, :, c * DIFF_HALF:(c + 1) * DIFF_HALF].astype(kb_ref.dtype)
        for h in _HEADS:
            _fill_values_t(vt_ref.at[h], v_ref, h * HEAD_DIM, t)

    def tile(j, bias_ref=None, emask=None):
        n = len(halves)
        return _KeyTile([_key_rows(kb_ref.at[c], j, t) for c in halves], [vt_ref[c // 2, j] for c in halves],
                        None if bias_ref is None else [bias_ref[c // 2] for c in halves],
                        None if emask is None else [emask] * n)

    def query_step(step):
        qts = _queries_t(q_ref, DIFF_HALF, t)
        jobs = []
        for r in range(Q_TILES):
            k = step * Q_TILES + r
            groups = _pairs([functools.partial(tile, j) for j in range(k - 1)])
            groups.append(([functools.partial(tile, k - 1, bs_ref)] if k else [])
                          + [functools.partial(tile, k, bd_ref, causal)])
            jobs.append((len(halves), qts[r], groups))
        all_states = yield from _softmax_jobs(t, jobs)
        outs = []
        for states in all_states:
            heads = []
            for h in _HEADS:
                o = _softmax_out(states[2 * h]) - lam * _softmax_out(states[2 * h + 1])
                o = o * lax.rsqrt(jnp.mean(o * o, axis=0, keepdims=True) + NORM_EPS) * g_ref[...]
                heads.append(o * (1.0 - lambda_init))
            outs.append(heads)
        _store_heads(o_ref, outs)

    yield from query_step(0)


def _diff(proj, lam_params, subln, bias_diag, bias_sub, lambda_init, t=TILE):
    s = proj.shape[1]
    in_specs = ([pl.BlockSpec(lam_params.shape, lambda b: (0, 0)), pl.BlockSpec(subln.shape, lambda b: (0, 0))]
                + _group_specs(s, CB_DF_Q, CB_DF_K, CB_DF_V) + _bias_specs(t, 2))
    return _Mixer(functools.partial(_diff_emit, lambda_init=lambda_init), in_specs,
                  [lam_params, subln] + [proj] * 3 + [bias_diag, bias_sub],
                  _kv_scratch(s, t, key_dim=DIFF_HALF, n_keys=2 * GROUP_HEADS))


def _compress_blocks(kcv_ref, pk_ref, pv_ref, wk1_ref, wk2_ref, wv1_ref, wv2t_ref):
    n_chunk = kcv_ref.shape[1] // CMP_STRIDE

    branches = ((0, pk_ref, wk1_ref), (HEAD_DIM, pv_ref, wv1_ref))
    tops = [jnp.zeros((n_chunk, w1_ref.shape[1]), jnp.float32) for _, _, w1_ref in branches]
    bots = list(tops)
    pack = 2 * LANES // HEAD_DIM
    for l0 in range(0, CMP_STRIDE, pack):
        tokens = [kcv_ref[0, pl.ds(l, n_chunk, stride=CMP_STRIDE), :] for l in range(l0, l0 + pack)]
        for n, (col0, p_ref, w1_ref) in enumerate(branches):
            for half, acc in ((0, tops), (CMP_STRIDE, bots)):
                x = jnp.concatenate([(tok[:, col0:col0 + HEAD_DIM] + p_ref[half + l:half + l + 1, :]).astype(_MXU)
                                     for l, tok in zip(range(l0, l0 + pack), tokens)], axis=1)
                acc[n] = acc[n] + _dot(x, w1_ref[(half + l0) * HEAD_DIM:(half + l0 + pack) * HEAD_DIM, :])
    hidden = [jax.nn.gelu(top + pltpu.roll(bot, n_chunk - 1, axis=0)).astype(_MXU) for top, bot in zip(tops, bots)]
    return _dot(hidden[0], wk2_ref[...]), _dot_nt(wv2t_ref[...], hidden[1])


def _nsa_emit(q_ref, kva_ref, kvb_ref, kcv_ref, pk_ref, pv_ref, wk1_ref, wk2_ref, wv1_ref, wv2t_ref,
              bd_ref, bs_ref, bc_ref, cover_ref, e_ref, o_ref, ks_ref, vst_ref, kw_ref, vwt_ref):
    t = q_ref.shape[1] // Q_TILES
    key, qry = _tile_iotas(t)
    causal = key <= qry
    ks_col, vs_col, kw_col, vw_col, gate_col = 2 * HEAD_DIM, 3 * HEAD_DIM, 0, HEAD_DIM, 2 * HEAD_DIM

    @_now
    def _():
        ks_ref[:, :HEAD_DIM] = kva_ref[0, :, ks_col:ks_col + HEAD_DIM].astype(ks_ref.dtype)
        ks_ref[:, HEAD_DIM:] = e_ref[...]
        kw_ref[...] = kvb_ref[0, :, kw_col:kw_col + HEAD_DIM].astype(kw_ref.dtype)
        _fill_values_t(vst_ref, kva_ref, vs_col, t)
        _fill_values_t(vwt_ref, kvb_ref, vw_col, t)

    def tile(k_ref, vt_ref, j, bias_ref=None, emask=None):
        n = GROUP_HEADS
        return _KeyTile([_key_rows(k_ref, j, t)] * n, [vt_ref[j]] * n,
                        None if bias_ref is None else [bias_ref[h] for h in _HEADS],
                        None if emask is None else [emask] * n)

    def query_step(step):
        qts = _queries_t(q_ref, HEAD_DIM, t)
        kc, vct = (a.astype(_MXU) for a in _compress_blocks(kcv_ref, pk_ref, pv_ref, wk1_ref, wk2_ref,
                                                            wv1_ref, wv2t_ref))
        n_cmp = kc.shape[0]
        n_slc = cover_ref.shape[0]
        c_row = lax.broadcasted_iota(jnp.int32, (n_cmp, t), 0)
        c_col = lax.broadcasted_iota(jnp.int32, (n_cmp, t), 1)
        s_row = lax.broadcasted_iota(jnp.int32, (n_slc, t), 0)
        s_col = lax.broadcasted_iota(jnp.int32, (n_slc, t), 1)
        tiles = [step * Q_TILES + r for r in range(Q_TILES)]

        o_cmp, importance = [], []
        for r, k in enumerate(tiles):
            visible = c_col + k * t >= c_row * CMP_STRIDE + (CMP_LEN - 1)
            cmp_scores = [_dot(kc, qts[r][h]) for h in _HEADS]
            cmp_probs = []
            p_sum = jnp.zeros((n_cmp, t), jnp.float32)
            for h in _HEADS:
                sc = jnp.where(visible, cmp_scores[h] + bc_ref[h, :, r * t:(r + 1) * t], NEG_INF)
                e = jnp.where(visible, jnp.exp2(sc - jnp.max(sc, axis=0, keepdims=True)), 0.0)
                p = e / jnp.maximum(jnp.sum(e, axis=0, keepdims=True), TINY)
                cmp_probs.append(p.astype(_MXU))
                p_sum = p_sum + p
            o_cmp.append([_dot(vct, cmp_probs[h]) for h in _HEADS])
            importance.append(_dot(cover_ref[...], p_sum, precision=lax.Precision.HIGHEST))

        n_back = WINDOW // t
        jobs = []
        for r, k in enumerate(tiles):
            window = [functools.partial(tile, kw_ref, vwt_ref, k - n_back, None, qry < key)] if k >= n_back else []
            for back in range(min(n_back - 1, k), 0, -1):
                window.append(functools.partial(tile, kw_ref, vwt_ref, k - back, bs_ref if back == 1 else None))
            window.append(functools.partial(tile, kw_ref, vwt_ref, k, bd_ref, causal))
            jobs.append((GROUP_HEADS, qts[r], [window]))
        win_states = yield from _softmax_jobs(t, jobs)
        o_win = [[_softmax_out(st) for st in states] for states in win_states]

        jobs = []
        for r, k in enumerate(tiles):
            own = jnp.right_shift(s_col + k * t, int(math.log2(SLC_LEN)))
            score = jnp.where(s_row == own, FORCE, jnp.where(s_row < own, importance[r], NEG_INF))
            sel = _top_k_rows(score, s_row.astype(jnp.float32), min(SLC_TOPN, n_slc))
            penalty = jnp.where(sel > 0.5, 0.0, NEG_INF).astype(_MXU)
            q_aug = [jnp.concatenate([qts[r][h], penalty], axis=0) for h in _HEADS]
            groups = _pairs([functools.partial(tile, ks_ref, vst_ref, j) for j in range(k - 1)])
            groups.append(([functools.partial(tile, ks_ref, vst_ref, k - 1, bs_ref)] if k else [])
                          + [functools.partial(tile, ks_ref, vst_ref, k, bd_ref, causal)])
            jobs.append((GROUP_HEADS, q_aug, groups))
        slc_states = yield from _softmax_jobs(t, jobs)
        o_slc = [[_softmax_out(st) for st in states] for states in slc_states]

        gates = _transposed(kvb_ref[0, :, (gate_col // LANES) * LANES:(gate_col // LANES + 1) * LANES])
        gates = 1.0 / (1.0 + jnp.exp(-gates[gate_col % LANES:gate_col % LANES + N_GATES + 4]))
        outs = []
        for r in range(Q_TILES):
            heads = []
            for h in _HEADS:
                g = [gates[br * GROUP_HEADS + h:br * GROUP_HEADS + h + 1, r * t:(r + 1) * t] for br in range(3)]
                heads.append(g[0] * o_cmp[r][h] + g[1] * o_slc[r][h] + g[2] * o_win[r][h])
            outs.append(heads)
        _store_heads(o_ref, outs)

    yield from query_step(0)


def _nsa(proj, kcv, cmp_params, layer, bias_diag, bias_sub, bias_cmp, cover_t, expand, t=TILE):
    s = proj.shape[1]
    values_t = pltpu.VMEM((s // t, ACC_ROWS, t), _MXU)
    in_specs = (_group_specs(s, CB_NS_Q, CB_NS_A, CB_NS_B)
                + [pl.BlockSpec((1, s, kcv.shape[2]), lambda b: (b, 0, 0))]
                + [pl.BlockSpec((None,) + a.shape[1:], lambda b: (layer, 0, 0)) for a in cmp_params]
                + _bias_specs(t, 1)
                + [pl.BlockSpec(bias_cmp.shape, lambda b: (0, 0, 0)),
                   pl.BlockSpec(cover_t.shape, lambda b: (0, 0)),
                   pl.BlockSpec(expand.shape, lambda b: (0, 0))])
    scratch = [pltpu.VMEM((s, HEAD_DIM + expand.shape[1]), _MXU), values_t, pltpu.VMEM((s, HEAD_DIM), _MXU), values_t]
    return _Mixer(_nsa_emit, in_specs,
                  [proj] * 3 + [kcv, *cmp_params, bias_diag, bias_sub, bias_cmp, cover_t, expand], scratch)


def _nsa_constants(s):
    n_cmp = (s - CMP_LEN) // CMP_STRIDE + 1
    n_slc = s // SLC_LEN
    assert n_cmp + 1 == s // CMP_STRIDE and n_slc % SUBLANES == 0
    c_start = np.arange(n_cmp) * CMP_STRIDE
    s_start = np.arange(n_slc) * SLC_LEN
    cover = np.clip(np.minimum((c_start + CMP_LEN - 1)[:, None], (s_start + SLC_LEN - 1)[None, :])
                    - np.maximum(c_start[:, None], s_start[None, :]) + 1, 0, None) / CMP_LEN
    cover_t = np.zeros((n_slc, n_cmp + 1), np.float32)
    cover_t[:, :n_cmp] = cover.T
    expand = (np.arange(s)[:, None] // SLC_LEN == np.arange(n_slc)[None, :]).astype(np.float32)
    return jnp.asarray(cover_t), jnp.asarray(expand, _MXU)


def kernel(x, w_in, w_out, w_up, w_down, norm_attn, norm_mlp, cmp_pos_k, cmp_pos_v, cmp_k_w1, cmp_k_w2,
           cmp_v_w1, cmp_v_w2, diff_lambda, diff_subln, rel_bias, final_norm):
    bsz, s, d = x.shape
    depth = w_in.shape[0]
    t = TILE
    n_chunk = s // CMP_STRIDE
    assert s == Q_TILES * t and MOBA_BLOCK % t == 0 and WINDOW % t == 0 and t >= MAX_DISTANCE

    assert w_in.shape[2] + PAD_COLS == D_IN_PAD
    w_in_c = jnp.pad(w_in.astype(_MXU), ((0, 0), (0, 0), (0, PAD_COLS)))
    w_out_c, w_up_c, w_down_c = (w.astype(_MXU) for w in (w_out, w_up, w_down))
    wk1, wk2 = cmp_k_w1.astype(_MXU), cmp_k_w2.astype(_MXU)
    wv1, wv2t = cmp_v_w1.astype(_MXU), jnp.swapaxes(cmp_v_w2, 1, 2).astype(_MXU)

    table_t = jnp.pad(rel_bias.T, ((0, 0), (0, LANES - N_BUCKETS)))
    tiles = dict(n_heads=rel_bias.shape[1], head0=0, rows=t, cols=t, col_tile=t, row_stride=-1, col_stride=1)
    bias_diag = _bias_table(table_t, offset=0, **tiles)
    bias_sub = _bias_table(table_t, offset=t, **tiles)
    bias_cmp = _bias_table(table_t, n_heads=GROUP_HEADS, head0=GROUP_HEADS, rows=n_chunk, cols=s,
                           col_tile=t, row_stride=-CMP_STRIDE, col_stride=1, offset=-(CMP_LEN - 1))
    cover_t, expand = _nsa_constants(s)
    col_scale = np.ones((1, D_IN_PAD), np.float32)
    for cb, width in ((CB_SB_Q, HEAD_DIM), (CB_MB_Q, HEAD_DIM), (CB_NS_Q, HEAD_DIM), (CB_DF_Q, DIFF_HALF)):
        col_scale[:, cb * GROUP_WIDTH:(cb + 1) * GROUP_WIDTH] = width ** -0.5 * LOG2E
    col_scale = jnp.asarray(col_scale)

    x2 = x.reshape(bsz * s, d)
    for layer in range(depth):
        proj, kcv = _norm_matmul(x2, norm_attn[layer][None], w_in_c, layer, col_scale,
                                 gap=(COLS_BEFORE_PAD, COLS_BEFORE_PAD + PAD_COLS),
                                 side_col=CB_NS_A * GROUP_WIDTH, side_width=2 * HEAD_DIM)
        proj = proj.reshape(bsz, s, D_IN_PAD)
        lambda_init = 0.8 - 0.6 * math.exp(-0.3 * layer)
        o_sb = _run_mixer(_stick_breaking(proj), bsz, s, "stick_breaking")
        o_mb = _run_mixer(_moba(proj, bias_diag, bias_sub), bsz, s, "moba")
        o_ns = _run_mixer(_nsa(proj, kcv.reshape(bsz, s, 2 * HEAD_DIM), (cmp_pos_k, cmp_pos_v, wk1, wk2, wv1, wv2t),
                               layer, bias_diag, bias_sub, bias_cmp, cover_t, expand), bsz, s, "nsa")
        o_df = _run_mixer(_diff(proj, diff_lambda[layer], diff_subln[layer][:, None], bias_diag, bias_sub,
                                lambda_init), bsz, s, "diff_attention")
        groups = [o.reshape(bsz * s, GROUP_WIDTH) for o in (o_sb, o_mb, o_ns, o_df)]
        x2 = _out_mlp(x2, groups, w_out_c, norm_mlp[layer][None], w_up_c, w_down_c, layer,
                      final_norm[None], final_norm=(layer == depth - 1))
    return x2.reshape(bsz, s, d)
```

```python
import functools
import math
from typing import Any, Callable, NamedTuple, Optional, Sequence

import numpy as np
import jax
import jax.numpy as jnp
from jax import lax
from jax.experimental import pallas as pl
from jax.experimental.pallas import tpu as pltpu

HEAD_DIM = 64
GROUP_HEADS = 4
GROUP_WIDTH = GROUP_HEADS * HEAD_DIM
NORM_EPS = 1e-6
NEG_INF = -1e30
BIG = 1e30
FORCE = 1e30
TINY = 1e-30
SOFTPLUS_CLAMP = 64.0
PICKED = -3e38
LOG2E = math.log2(math.e)
N_BUCKETS = 32
MAX_DISTANCE = 128
MOBA_BLOCK = 256
MOBA_TOPK = 3
CMP_LEN = 32
CMP_STRIDE = 16
SLC_LEN = 64
SLC_TOPN = 4
WINDOW = 512
DIFF_HALF = HEAD_DIM // 2
LANES = 128
SUBLANES = 8
TILE = 256
BF16_ROWS = 16
ACC_ROWS = HEAD_DIM + BF16_ROWS
N_GATES = 3 * GROUP_HEADS
COLS_BEFORE_PAD = 9 * GROUP_WIDTH - 2 * HEAD_DIM + N_GATES
PAD_COLS = 2 * HEAD_DIM - N_GATES
CB_SB_Q, CB_SB_K, CB_SB_V, CB_MB_Q, CB_MB_K, CB_MB_V, CB_NS_Q, CB_NS_A, CB_NS_B, CB_DF_Q, CB_DF_K, CB_DF_V = range(12)
D_IN_PAD = 12 * GROUP_WIDTH

_MXU = jnp.bfloat16
_VMEM_LIMIT = 56 * 1024 * 1024
_HEADS = range(GROUP_HEADS)
Q_TILES = 8
LOOKAHEAD = 2


def _dot(a, b, precision=None):
    return jnp.dot(a, b, precision=precision, preferred_element_type=jnp.float32)


def _dot_nt(a, b, precision=None):
    return lax.dot_general(a, b, (((1,), (1,)), ((), ())), precision=precision,
                           preferred_element_type=jnp.float32)


def _rms(x, g):
    return x * lax.rsqrt(jnp.mean(x * x, axis=-1, keepdims=True) + NORM_EPS) * g


def _params(*sem, fuse_inputs=None):
    return pltpu.CompilerParams(dimension_semantics=sem, vmem_limit_bytes=_VMEM_LIMIT,
                                allow_input_fusion=fuse_inputs)


def _norm_matmul_kernel(x_ref, g_ref, w_ref, scale_ref, o_ref, side_ref, wp_ref, *, tn, side_col, gap):
    @pl.when(pl.program_id(0) == 0)
    def _():
        lo, hi = gap
        wp_ref[:, :lo] = w_ref[:, :lo]
        wp_ref[:, lo:hi] = jnp.zeros((wp_ref.shape[0], hi - lo), wp_ref.dtype)
        wp_ref[:, hi:] = w_ref[:, lo:lo + wp_ref.shape[1] - hi]

    h = _rms(x_ref[...], g_ref[...]).astype(_MXU)
    for j in range(wp_ref.shape[1] // tn):
        cols = slice(j * tn, (j + 1) * tn)
        acc = _dot(h, wp_ref[:, cols])
        o_ref[:, cols] = (acc * scale_ref[:, cols]).astype(o_ref.dtype)
        if j * tn <= side_col < (j + 1) * tn:
            side_ref[...] = acc[:, side_col - j * tn:side_col - j * tn + side_ref.shape[1]]


def _norm_matmul(x, g, w, layer, col_scale, *, gap, side_col, side_width, tm=1024, tn=1024):
    m, d = x.shape
    n = w.shape[2]
    assert side_col % LANES == 0 and side_col // tn == (side_col + side_width - 1) // tn
    return pl.pallas_call(
        functools.partial(_norm_matmul_kernel, tn=tn, side_col=side_col, gap=gap),
        grid=(m // tm,),
        in_specs=[pl.BlockSpec((tm, d), lambda i: (i, 0)),
                  pl.BlockSpec((1, d), lambda i: (0, 0)),
                  pl.BlockSpec((None, d, w.shape[2]), lambda i: (layer, 0, 0), pipeline_mode=pl.Buffered(1)),
                  pl.BlockSpec((1, n), lambda i: (0, 0))],
        out_specs=[pl.BlockSpec((tm, n), lambda i: (i, 0)),
                   pl.BlockSpec((tm, side_width), lambda i: (i, 0))],
        out_shape=[jax.ShapeDtypeStruct((m, n), _MXU),
                   jax.ShapeDtypeStruct((m, side_width), jnp.float32)],
        scratch_shapes=[pltpu.VMEM((d, n), _MXU)],
        compiler_params=_params("arbitrary", fuse_inputs=[False, False, True, False]),
        name="norm_in_proj",
    )(x, g, w, col_scale)


def _out_mlp_kernel(x_ref, a_ref, b_ref, c_ref, d_ref, wo_ref, g_ref, wu_ref, wd_ref, gf_ref, o_ref,
                    *, final_norm, tf):
    mixed = jnp.concatenate([a_ref[...], b_ref[...], c_ref[...], d_ref[...]], axis=1)
    y = x_ref[...] + _dot(mixed, wo_ref[...])
    h = _rms(y, g_ref[...]).astype(_MXU)
    for c in range(wu_ref.shape[1] // tf):
        u = jnp.square(jnp.maximum(_dot(h, wu_ref[:, c * tf:(c + 1) * tf]), 0.0))
        y = y + _dot(u.astype(_MXU), wd_ref[c * tf:(c + 1) * tf, :])
    if final_norm:
        y = _rms(y, gf_ref[...])
    o_ref[...] = y


def _out_mlp(x, groups, w_out, g, w_up, w_down, layer, g_final, *, final_norm, tm=1024, tf=2048):
    m, d = x.shape
    f = w_up.shape[2]
    gspec = pl.BlockSpec((tm, GROUP_WIDTH), lambda i: (i, 0))
    row = pl.BlockSpec((1, d), lambda i: (0, 0))

    def resident(rows, cols):
        return pl.BlockSpec((None, rows, cols), lambda i: (layer, 0, 0), pipeline_mode=pl.Buffered(1))

    return pl.pallas_call(
        functools.partial(_out_mlp_kernel, final_norm=final_norm, tf=tf),
        grid=(m // tm,),
        in_specs=[pl.BlockSpec((tm, d), lambda i: (i, 0)), gspec, gspec, gspec, gspec,
                  resident(d, d), row, resident(d, f), resident(f, d), row],
        out_specs=pl.BlockSpec((tm, d), lambda i: (i, 0)),
        out_shape=jax.ShapeDtypeStruct((m, d), jnp.float32),
        compiler_params=_params("parallel", fuse_inputs=[False] * 5 + [True, False, True, True, False]),
        name="out_proj_mlp_residual",
    )(x, *groups, w_out, g, w_up, w_down, g_final)


def _t5_bucket(dist):
    n = jnp.maximum(dist, 0)
    max_exact = N_BUCKETS // 2
    nf = jnp.maximum(n, 1).astype(jnp.float32)
    large = max_exact + (jnp.log(nf / max_exact) / math.log(MAX_DISTANCE / max_exact)
                         * (N_BUCKETS - max_exact)).astype(jnp.int32)
    large = jnp.minimum(large, N_BUCKETS - 1)
    return jnp.where(n < max_exact, n, large)


def _bias_kernel(tab_ref, o_ref, *, row_stride, col_stride, offset, head0):
    nh, tr, tc = o_ref.shape
    for blk in range(tc // LANES):
        rows = lax.broadcasted_iota(jnp.int32, (tr, LANES), 0)
        cols = lax.broadcasted_iota(jnp.int32, (tr, LANES), 1) + (pl.program_id(0) * tc + blk * LANES)
        bucket = _t5_bucket(rows * row_stride + cols * col_stride + offset)
        for h in range(nh):
            row = tab_ref[head0 + h:head0 + h + 1, :]
            row = (row - row[:, N_BUCKETS - 1:N_BUCKETS]) * LOG2E
            o_ref[h, :, blk * LANES:(blk + 1) * LANES] = jnp.take_along_axis(
                jnp.broadcast_to(row, (tr, LANES)), bucket, axis=1, mode="promise_in_bounds")


def _bias_table(table_t, *, n_heads, head0, rows, cols, col_tile, row_stride, col_stride, offset):
    return pl.pallas_call(
        functools.partial(_bias_kernel, row_stride=row_stride, col_stride=col_stride,
                          offset=offset, head0=head0),
        grid=(cols // col_tile,),
        in_specs=[pl.BlockSpec(table_t.shape, lambda i: (0, 0))],
        out_specs=pl.BlockSpec((n_heads, rows, col_tile), lambda i: (0, 0, i)),
        out_shape=jax.ShapeDtypeStruct((n_heads, rows, cols), jnp.float32),
        compiler_params=_params("parallel"),
        name="t5_bias_tiles",
    )(table_t)


def _softmax_init(t):
    return (jnp.full((1, t), NEG_INF, jnp.float32), jnp.zeros((ACC_ROWS, t), jnp.float32))


class _KeyTile(NamedTuple):
    kts: Sequence[Any]
    vts: Sequence[Any]
    biases: Optional[Sequence[Any]] = None
    emasks: Optional[Sequence[Any]] = None
    qmasks: Optional[Sequence[Any]] = None


def _round_robin(lists):
    out = []
    for rank in range(max(map(len, lists), default=0)):
        out.extend(items[rank] for items in lists if rank < len(items))
    return out


def _softmax_jobs(t, jobs):
    built = {}

    def tiles_of(job, g):
        if (job, g) not in built:
            built[job, g] = [make() for make in jobs[job][2][g]]
        return built[job, g]

    def scores_of(job, g, c):
        qts = jobs[job][1]
        row = []
        for tile in tiles_of(job, g):
            s = _dot(tile.kts[c], qts[c])
            if tile.biases is not None:
                s = s + tile.biases[c]
            if tile.emasks is not None:
                s = jnp.where(tile.emasks[c], s, NEG_INF)
            row.append(s.astype(_MXU))
        return row

    def update(state, job, g, c, scores):
        m, acc = state
        m_new = m
        for tile, s in zip(tiles_of(job, g), scores):
            tile_max = jnp.max(s, axis=0, keepdims=True).astype(jnp.float32)
            if tile.qmasks is not None:
                tile_max = jnp.where(tile.qmasks[c], tile_max, NEG_INF)
            m_new = jnp.maximum(m_new, tile_max)
        seen = m_new > 0.5 * NEG_INF
        acc = jnp.exp2(m - m_new) * acc
        for tile, s in zip(tiles_of(job, g), scores):
            ok = seen if tile.qmasks is None else jnp.logical_and(seen, tile.qmasks[c])
            acc = acc + _dot(tile.vts[c], jnp.exp2(s - jnp.where(ok, m_new, BIG).astype(_MXU)))
        return m_new, acc

    units = _round_robin([[(job, g, c) for g in range(len(groups)) for c in range(n)]
                          for job, (n, _, groups) in enumerate(jobs)])
    lookahead = max(LOOKAHEAD, len(jobs) // 2)
    states = [[_softmax_init(t) for _ in range(n)] for n, _, _ in jobs]
    pending = {k: scores_of(*units[k]) for k in range(min(lookahead, len(units)))}
    for k, (job, g, c) in enumerate(units):
        if k + lookahead < len(units):
            pending[k + lookahead] = scores_of(*units[k + lookahead])
        states[job][c] = update(states[job][c], job, g, c, pending.pop(k))
        yield
    return states


def _pairs(items):
    return [items[p:p + 2] for p in range(0, len(items), 2)]


class _Mixer(NamedTuple):
    emit: Callable[..., Any]
    in_specs: Sequence[Any]
    operands: Sequence[Any]
    scratch: Sequence[Any]


def _mixer_kernel(*refs, emit):
    for _ in emit(*refs):
        pass


def _run_mixer(mixer, bsz, s, name):
    return pl.pallas_call(
        functools.partial(_mixer_kernel, emit=mixer.emit),
        grid=(bsz,),
        in_specs=list(mixer.in_specs),
        out_specs=pl.BlockSpec((1, s, GROUP_WIDTH), lambda b: (b, 0, 0)),
        out_shape=jax.ShapeDtypeStruct((bsz, s, GROUP_WIDTH), _MXU),
        scratch_shapes=list(mixer.scratch),
        compiler_params=_params("parallel"),
        name=name,
    )(*mixer.operands)


def _softmax_out(state):
    acc = state[1]
    return acc[:HEAD_DIM] / jnp.maximum(acc[HEAD_DIM:HEAD_DIM + 1], TINY)


def _top_k_rows(score, row_f, k):
    sel = jnp.zeros(score.shape, jnp.float32)
    for _ in range(k):
        mx = jnp.max(score, axis=0, keepdims=True)
        idx = jnp.min(jnp.where(score == mx, row_f, float(score.shape[0])), axis=0, keepdims=True)
        pick = row_f == idx
        sel = jnp.where(pick, 1.0, sel)
        score = jnp.where(pick, PICKED, score)
    return sel


def _now(fn):
    fn()


def _tile_iotas(t):
    return (lax.broadcasted_iota(jnp.int32, (t, t), 0), lax.broadcasted_iota(jnp.int32, (t, t), 1))


def _key_rows(ref, j, t):
    if isinstance(j, int):
        return ref[j * t:(j + 1) * t, :]
    return ref[pl.ds(pl.multiple_of(j * t, t), t), :]


def _transposed(ref_block):
    return ref_block.astype(jnp.float32).T


def _queries_t(q_ref, width, t):
    qt = _transposed(q_ref[0]).astype(_MXU)
    return [[qt[c * width:(c + 1) * width, r * t:(r + 1) * t] for c in range(GROUP_WIDTH // width)]
            for r in range(Q_TILES)]


def _fill_values_t(vt_ref, v_ref, col0, t):
    n_tiles, rows, _ = vt_ref.shape
    lane_block = (col0 // LANES) * LANES
    for c in range(n_tiles):
        blk = _transposed(v_ref[0, c * t:(c + 1) * t, lane_block:lane_block + LANES])
        vt_ref[c, 0:HEAD_DIM, :] = blk[col0 - lane_block:col0 - lane_block + HEAD_DIM].astype(vt_ref.dtype)
        if rows == ACC_ROWS:
            first = lax.broadcasted_iota(jnp.int32, (rows - HEAD_DIM, t), 0) == 0
            vt_ref[c, HEAD_DIM:rows, :] = jnp.where(first, 1.0, 0.0).astype(vt_ref.dtype)


def _group_specs(s, *column_blocks):
    return [pl.BlockSpec((1, s, GROUP_WIDTH), functools.partial(lambda cb, b: (b, 0, cb), cb))
            for cb in column_blocks]


def _bias_specs(t, head_group):
    spec = pl.BlockSpec((GROUP_HEADS, t, t), lambda b: (head_group, 0, 0))
    return [spec, spec]


def _store_heads(o_ref, outs_t):
    tiles = [jnp.concatenate(heads, axis=0) for heads in outs_t]
    o_ref[0] = jnp.concatenate(tiles, axis=1).T.astype(o_ref.dtype)


def _sb_emit(q_ref, k_ref, v_ref, o_ref, kb_ref, vt_ref):
    t = q_ref.shape[1] // Q_TILES
    key, qry = _tile_iotas(t)
    strict = key < qry
    later = jnp.where(qry > key, 1.0, 0.0).astype(_MXU)

    @_now
    def _():
        for h in _HEADS:
            kb_ref[h] = k_ref[0, :, h * HEAD_DIM:(h + 1) * HEAD_DIM].astype(kb_ref.dtype)
            _fill_values_t(vt_ref.at[h], v_ref, h * HEAD_DIM, t)

    def query_step(step):
        qts = _queries_t(q_ref, HEAD_DIM, t)
        units = _round_robin([[(r, step * Q_TILES + r, pair, h)
                               for pair in _pairs(list(range(step * Q_TILES + r, -1, -1))) for h in _HEADS]
                              for r in range(Q_TILES)])
        zero = (jnp.zeros((HEAD_DIM, t), jnp.float32), jnp.zeros((1, t), jnp.float32))
        carry = [[zero] * GROUP_HEADS for _ in range(Q_TILES)]
        zs, log_keeps, suffixes = {}, {}, {}

        def scores(u):
            r, _, pair, h = units[u]
            zs[u] = [_dot(_key_rows(kb_ref.at[h], j, t), qts[r][h]) for j in pair]

        def keeps(u):
            _, k, pair, _ = units[u]
            log_keeps[u], suffixes[u] = [], []
            for j, z in zip(pair, zs[u]):
                drop = jnp.maximum(jnp.log2(1.0 + jnp.exp2(jnp.minimum(z, SOFTPLUS_CLAMP))), z)
                if j == k:
                    drop = jnp.where(strict, drop, 0.0)
                log_keeps[u].append(drop)
                suffixes[u].append(_dot(later, drop.astype(_MXU)))

        def values(u):
            r, k, pair, h = units[u]
            acc, run = carry[r][h]
            weights = []
            for j, z, drop, suffix in zip(pair, zs.pop(u), log_keeps.pop(u), suffixes.pop(u)):
                a = jnp.exp2(z - drop - suffix + run)
                if j == k:
                    a = jnp.where(strict, a, 0.0)
                weights.append(a.astype(_MXU))
                run = run - (suffix[0:1] + drop[0:1])
            for j, w in zip(pair, weights):
                acc = acc + _dot(vt_ref[h, j], w)
            carry[r][h] = (acc, run)

        stages = (scores, keeps, values)
        for tick in range(len(units) + len(stages) - 1):
            for lag, stage in enumerate(stages):
                if 0 <= tick - lag < len(units):
                    stage(tick - lag)
            yield
        _store_heads(o_ref, [[c[0] for c in tile_carry] for tile_carry in carry])

    yield from query_step(0)


def _stick_breaking(proj, t=TILE):
    s = proj.shape[1]
    scratch = [pltpu.VMEM((GROUP_HEADS, s, HEAD_DIM), _MXU), pltpu.VMEM((GROUP_HEADS, s // t, HEAD_DIM, t), _MXU)]
    return _Mixer(_sb_emit, _group_specs(s, CB_SB_Q, CB_SB_K, CB_SB_V), [proj] * 3, scratch)


def _moba_emit(q_ref, k_ref, v_ref, bd_ref, bs_ref, o_ref, kb_ref, vt_ref, km_ref):
    t = q_ref.shape[1] // Q_TILES
    n_blk = k_ref.shape[1] // MOBA_BLOCK
    tiles_per_blk = MOBA_BLOCK // t
    blk_shift = int(math.log2(tiles_per_blk))

    @_now
    def _():
        km_ref[...] = jnp.zeros_like(km_ref)
        for h in _HEADS:
            lo, hi = h * HEAD_DIM, (h + 1) * HEAD_DIM
            kb_ref[h] = k_ref[0, :, lo:hi].astype(kb_ref.dtype)
            _fill_values_t(vt_ref.at[h], v_ref, lo, t)
            for n in range(n_blk):
                blk = k_ref[0, n * MOBA_BLOCK:(n + 1) * MOBA_BLOCK, lo:hi]
                km_ref[h, n:n + 1, :] = jnp.mean(blk.astype(jnp.float32), axis=0, keepdims=True)

    def query_step(step):
        key, qry = _tile_iotas(t)
        causal = key <= qry
        blk_row = lax.broadcasted_iota(jnp.int32, (km_ref.shape[1], t), 0)
        qt = _transposed(q_ref[0])
        jobs = []
        for r in range(Q_TILES):
            k = step * Q_TILES + r
            own = k >> blk_shift
            qts, sels = [], []
            for h in _HEADS:
                qf = qt[h * HEAD_DIM:(h + 1) * HEAD_DIM, r * t:(r + 1) * t]
                qts.append(qf.astype(_MXU))
                gate = _dot(km_ref[h], qf, precision=lax.Precision.HIGHEST)
                gate = jnp.where(blk_row < own, gate, NEG_INF)
                sel = _top_k_rows(gate, blk_row.astype(jnp.float32), min(MOBA_TOPK, n_blk - 1))
                sels.append(jnp.where(blk_row < own, sel, 0.0))

            def tile(own, sels, j, bias_ref=None, emask=None):
                n = j >> blk_shift
                return _KeyTile([_key_rows(kb_ref.at[h], j, t) for h in _HEADS], [vt_ref[h, j] for h in _HEADS],
                                None if bias_ref is None else [bias_ref[h] for h in _HEADS],
                                None if emask is None else [emask] * GROUP_HEADS,
                                None if n == own else [sels[h][n:n + 1] > 0.5 for h in _HEADS])

            tile = functools.partial(tile, own, sels)
            groups = _pairs([functools.partial(tile, j) for j in range(k - 1)])
            groups.append(([functools.partial(tile, k - 1, bs_ref)] if k else [])
                          + [functools.partial(tile, k, bd_ref, causal)])
            jobs.append((GROUP_HEADS, qts, groups))
        all_states = yield from _softmax_jobs(t, jobs)
        _store_heads(o_ref, [[_softmax_out(st) for st in states] for states in all_states])

    yield from query_step(0)


def _kv_scratch(s, t, key_dim=HEAD_DIM, n_keys=GROUP_HEADS):
    return [pltpu.VMEM((n_keys, s, key_dim), _MXU), pltpu.VMEM((GROUP_HEADS, s // t, ACC_ROWS, t), _MXU)]


def _moba(proj, bias_diag, bias_sub, t=TILE):
    s = proj.shape[1]
    n_blk_pad = -(-(s // MOBA_BLOCK) // SUBLANES) * SUBLANES
    scratch = _kv_scratch(s, t) + [pltpu.VMEM((GROUP_HEADS, n_blk_pad, HEAD_DIM), jnp.float32)]
    return _Mixer(_moba_emit, _group_specs(s, CB_MB_Q, CB_MB_K, CB_MB_V) + _bias_specs(t, 0),
                  [proj] * 3 + [bias_diag, bias_sub], scratch)


def _diff_emit(lam_ref, g_ref, q_ref, k_ref, v_ref, bd_ref, bs_ref, o_ref, kb_ref, vt_ref, *, lambda_init):
    t = q_ref.shape[1] // Q_TILES
    key, qry = _tile_iotas(t)
    causal = key <= qry
    lv = lam_ref[...]
    lam = (jnp.exp(jnp.sum(lv[0:1] * lv[1:2], keepdims=True))
           - jnp.exp(jnp.sum(lv[2:3] * lv[3:4], keepdims=True)) + lambda_init)
    halves = range(2 * GROUP_HEADS)

    @_now
    def _():
        for c in halves:
            kb_ref[c] = k_ref[0, :, c * DIFF_HALF:(c + 1) * DIFF_HALF].astype(kb_ref.dtype)
        for h in _HEADS:
            _fill_values_t(vt_ref.at[h], v_ref, h * HEAD_DIM, t)

    def tile(j, bias_ref=None, emask=None):
        n = len(halves)
        return _KeyTile([_key_rows(kb_ref.at[c], j, t) for c in halves], [vt_ref[c // 2, j] for c in halves],
                        None if bias_ref is None else [bias_ref[c // 2] for c in halves],
                        None if emask is None else [emask] * n)

    def query_step(step):
        qts = _queries_t(q_ref, DIFF_HALF, t)
        jobs = []
        for r in range(Q_TILES):
            k = step * Q_TILES + r
            groups = _pairs([functools.partial(tile, j) for j in range(k - 1)])
            groups.append(([functools.partial(tile, k - 1, bs_ref)] if k else [])
                          + [functools.partial(tile, k, bd_ref, causal)])
            jobs.append((len(halves), qts[r], groups))
        all_states = yield from _softmax_jobs(t, jobs)
        outs = []
        for states in all_states:
            heads = []
            for h in _HEADS:
                o = _softmax_out(states[2 * h]) - lam * _softmax_out(states[2 * h + 1])
                o = o * lax.rsqrt(jnp.mean(o * o, axis=0, keepdims=True) + NORM_EPS) * g_ref[...]
                heads.append(o * (1.0 - lambda_init))
            outs.append(heads)
        _store_heads(o_ref, outs)

    yield from query_step(0)


def _diff(proj, lam_params, subln, bias_diag, bias_sub, lambda_init, t=TILE):
    s = proj.shape[1]
    in_specs = ([pl.BlockSpec(lam_params.shape, lambda b: (0, 0)), pl.BlockSpec(subln.shape, lambda b: (0, 0))]
                + _group_specs(s, CB_DF_Q, CB_DF_K, CB_DF_V) + _bias_specs(t, 2))
    return _Mixer(functools.partial(_diff_emit, lambda_init=lambda_init), in_specs,
                  [lam_params, subln] + [proj] * 3 + [bias_diag, bias_sub],
                  _kv_scratch(s, t, key_dim=DIFF_HALF, n_keys=2 * GROUP_HEADS))


def _compress_blocks(kcv_ref, pk_ref, pv_ref, wk1_ref, wk2_ref, wv1_ref, wv2t_ref):
    n_chunk = kcv_ref.shape[1] // CMP_STRIDE

    branches = ((0, pk_ref, wk1_ref), (HEAD_DIM, pv_ref, wv1_ref))
    tops = [jnp.zeros((n_chunk, w1_ref.shape[1]), jnp.float32) for _, _, w1_ref in branches]
    bots = list(tops)
    pack = 2 * LANES // HEAD_DIM
    for l0 in range(0, CMP_STRIDE, pack):
        tokens = [kcv_ref[0, pl.ds(l, n_chunk, stride=CMP_STRIDE), :] for l in range(l0, l0 + pack)]
        for n, (col0, p_ref, w1_ref) in enumerate(branches):
            for half, acc in ((0, tops), (CMP_STRIDE, bots)):
                x = jnp.concatenate([(tok[:, col0:col0 + HEAD_DIM] + p_ref[half + l:half + l + 1, :]).astype(_MXU)
                                     for l, tok in zip(range(l0, l0 + pack), tokens)], axis=1)
                acc[n] = acc[n] + _dot(x, w1_ref[(half + l0) * HEAD_DIM:(half + l0 + pack) * HEAD_DIM, :])
    hidden = [jax.nn.gelu(top + pltpu.roll(bot, n_chunk - 1, axis=0)).astype(_MXU) for top, bot in zip(tops, bots)]
    return _dot(hidden[0], wk2_ref[...]), _dot_nt(wv2t_ref[...], hidden[1])


def _nsa_emit(q_ref, kva_ref, kvb_ref, kcv_ref, pk_ref, pv_ref, wk1_ref, wk2_ref, wv1_ref, wv2t_ref,
              bd_ref, bs_ref, bc_ref, cover_ref, e_ref, o_ref, ks_ref, vst_ref, kw_ref, vwt_ref):
    t = q_ref.shape[1] // Q_TILES
    key, qry = _tile_iotas(t)
    causal = key <= qry
    ks_col, vs_col, kw_col, vw_col, gate_col = 2 * HEAD_DIM, 3 * HEAD_DIM, 0, HEAD_DIM, 2 * HEAD_DIM

    @_now
    def _():
        ks_ref[:, :HEAD_DIM] = kva_ref[0, :, ks_col:ks_col + HEAD_DIM].astype(ks_ref.dtype)
        ks_ref[:, HEAD_DIM:] = e_ref[...]
        kw_ref[...] = kvb_ref[0, :, kw_col:kw_col + HEAD_DIM].astype(kw_ref.dtype)
        _fill_values_t(vst_ref, kva_ref, vs_col, t)
        _fill_values_t(vwt_ref, kvb_ref, vw_col, t)

    def tile(k_ref, vt_ref, j, bias_ref=None, emask=None):
        n = GROUP_HEADS
        return _KeyTile([_key_rows(k_ref, j, t)] * n, [vt_ref[j]] * n,
                        None if bias_ref is None else [bias_ref[h] for h in _HEADS],
                        None if emask is None else [emask] * n)

    def query_step(step):
        qts = _queries_t(q_ref, HEAD_DIM, t)
        kc, vct = (a.astype(_MXU) for a in _compress_blocks(kcv_ref, pk_ref, pv_ref, wk1_ref, wk2_ref,
                                                            wv1_ref, wv2t_ref))
        n_cmp = kc.shape[0]
        n_slc = cover_ref.shape[0]
        c_row = lax.broadcasted_iota(jnp.int32, (n_cmp, t), 0)
        c_col = lax.broadcasted_iota(jnp.int32, (n_cmp, t), 1)
        s_row = lax.broadcasted_iota(jnp.int32, (n_slc, t), 0)
        s_col = lax.broadcasted_iota(jnp.int32, (n_slc, t), 1)
        tiles = [step * Q_TILES + r for r in range(Q_TILES)]

        o_cmp, importance = [], []
        for r, k in enumerate(tiles):
            visible = c_col + k * t >= c_row * CMP_STRIDE + (CMP_LEN - 1)
            cmp_scores = [_dot(kc, qts[r][h]) for h in _HEADS]
            cmp_probs = []
            p_sum = jnp.zeros((n_cmp, t), jnp.float32)
            for h in _HEADS:
                sc = jnp.where(visible, cmp_scores[h] + bc_ref[h, :, r * t:(r + 1) * t], NEG_INF)
                e = jnp.where(visible, jnp.exp2(sc - jnp.max(sc, axis=0, keepdims=True)), 0.0)
                p = e / jnp.maximum(jnp.sum(e, axis=0, keepdims=True), TINY)
                cmp_probs.append(p.astype(_MXU))
                p_sum = p_sum + p
            o_cmp.append([_dot(vct, cmp_probs[h]) for h in _HEADS])
            importance.append(_dot(cover_ref[...], p_sum, precision=lax.Precision.HIGHEST))

        n_back = WINDOW // t
        jobs = []
        for r, k in enumerate(tiles):
            window = [functools.partial(tile, kw_ref, vwt_ref, k - n_back, None, qry < key)] if k >= n_back else []
            for back in range(min(n_back - 1, k), 0, -1):
                window.append(functools.partial(tile, kw_ref, vwt_ref, k - back, bs_ref if back == 1 else None))
            window.append(functools.partial(tile, kw_ref, vwt_ref, k, bd_ref, causal))
            jobs.append((GROUP_HEADS, qts[r], [window]))
        win_states = yield from _softmax_jobs(t, jobs)
        o_win = [[_softmax_out(st) for st in states] for states in win_states]

        jobs = []
        for r, k in enumerate(tiles):
            own = jnp.right_shift(s_col + k * t, int(math.log2(SLC_LEN)))
            score = jnp.where(s_row == own, FORCE, jnp.where(s_row < own, importance[r], NEG_INF))
            sel = _top_k_rows(score, s_row.astype(jnp.float32), min(SLC_TOPN, n_slc))
            penalty = jnp.where(sel > 0.5, 0.0, NEG_INF).astype(_MXU)
            q_aug = [jnp.concatenate([qts[r][h], penalty], axis=0) for h in _HEADS]
            groups = _pairs([functools.partial(tile, ks_ref, vst_ref, j) for j in range(k - 1)])
            groups.append(([functools.partial(tile, ks_ref, vst_ref, k - 1, bs_ref)] if k else [])
                          + [functools.partial(tile, ks_ref, vst_ref, k, bd_ref, causal)])
            jobs.append((GROUP_HEADS, q_aug, groups))
        slc_states = yield from _softmax_jobs(t, jobs)
        o_slc = [[_softmax_out(st) for st in states] for states in slc_states]

        gates = _transposed(kvb_ref[0, :, (gate_col // LANES) * LANES:(gate_col // LANES + 1) * LANES])
        gates = 1.0 / (1.0 + jnp.exp(-gates[gate_col % LANES:gate_col % LANES + N_GATES + 4]))
        outs = []
        for r in range(Q_TILES):
            heads = []
            for h in _HEADS:
                g = [gates[br * GROUP_HEADS + h:br * GROUP_HEADS + h + 1, r * t:(r + 1) * t] for br in range(3)]
                heads.append(g[0] * o_cmp[r][h] + g[1] * o_slc[r][h] + g[2] * o_win[r][h])
            outs.append(heads)
        _store_heads(o_ref, outs)

    yield from query_step(0)


def _nsa(proj, kcv, cmp_params, layer, bias_diag, bias_sub, bias_cmp, cover_t, expand, t=TILE):
    s = proj.shape[1]
    values_t = pltpu.VMEM((s // t, ACC_ROWS, t), _MXU)
    in_specs = (_group_specs(s, CB_NS_Q, CB_NS_A, CB_NS_B)
                + [pl.BlockSpec((1, s, kcv.shape[2]), lambda b: (b, 0, 0))]
                + [pl.BlockSpec((None,) + a.shape[1:], lambda b: (layer, 0, 0)) for a in cmp_params]
                + _bias_specs(t, 1)
                + [pl.BlockSpec(bias_cmp.shape, lambda b: (0, 0, 0)),
                   pl.BlockSpec(cover_t.shape, lambda b: (0, 0)),
                   pl.BlockSpec(expand.shape, lambda b: (0, 0))])
    scratch = [pltpu.VMEM((s, HEAD_DIM + expand.shape[1]), _MXU), values_t, pltpu.VMEM((s, HEAD_DIM), _MXU), values_t]
    return _Mixer(_nsa_emit, in_specs,
                  [proj] * 3 + [kcv, *cmp_params, bias_diag, bias_sub, bias_cmp, cover_t, expand], scratch)


def _nsa_constants(s):
    n_cmp = (s - CMP_LEN) // CMP_STRIDE + 1
    n_slc = s // SLC_LEN
    assert n_cmp + 1 == s // CMP_STRIDE and n_slc % SUBLANES == 0
    c_start = np.arange(n_cmp) * CMP_STRIDE
    s_start = np.arange(n_slc) * SLC_LEN
    cover = np.clip(np.minimum((c_start + CMP_LEN - 1)[:, None], (s_start + SLC_LEN - 1)[None, :])
                    - np.maximum(c_start[:, None], s_start[None, :]) + 1, 0, None) / CMP_LEN
    cover_t = np.zeros((n_slc, n_cmp + 1), np.float32)
    cover_t[:, :n_cmp] = cover.T
    expand = (np.arange(s)[:, None] // SLC_LEN == np.arange(n_slc)[None, :]).astype(np.float32)
    return jnp.asarray(cover_t), jnp.asarray(expand, _MXU)


def kernel(x, w_in, w_out, w_up, w_down, norm_attn, norm_mlp, cmp_pos_k, cmp_pos_v, cmp_k_w1, cmp_k_w2,
           cmp_v_w1, cmp_v_w2, diff_lambda, diff_subln, rel_bias, final_norm):
    bsz, s, d = x.shape
    depth = w_in.shape[0]
    t = TILE
    n_chunk = s // CMP_STRIDE
    assert s == Q_TILES * t and MOBA_BLOCK % t == 0 and WINDOW % t == 0 and t >= MAX_DISTANCE

    assert w_in.shape[2] + PAD_COLS == D_IN_PAD
    w_in_c = jnp.pad(w_in.astype(_MXU), ((0, 0), (0, 0), (0, PAD_COLS)))
    w_out_c, w_up_c, w_down_c = (w.astype(_MXU) for w in (w_out, w_up, w_down))
    wk1, wk2 = cmp_k_w1.astype(_MXU), cmp_k_w2.astype(_MXU)
    wv1, wv2t = cmp_v_w1.astype(_MXU), jnp.swapaxes(cmp_v_w2, 1, 2).astype(_MXU)

    table_t = jnp.pad(rel_bias.T, ((0, 0), (0, LANES - N_BUCKETS)))
    tiles = dict(n_heads=rel_bias.shape[1], head0=0, rows=t, cols=t, col_tile=t, row_stride=-1, col_stride=1)
    bias_diag = _bias_table(table_t, offset=0, **tiles)
    bias_sub = _bias_table(table_t, offset=t, **tiles)
    bias_cmp = _bias_table(table_t, n_heads=GROUP_HEADS, head0=GROUP_HEADS, rows=n_chunk, cols=s,
                           col_tile=t, row_stride=-CMP_STRIDE, col_stride=1, offset=-(CMP_LEN - 1))
    cover_t, expand = _nsa_constants(s)
    col_scale = np.ones((1, D_IN_PAD), np.float32)
    for cb, width in ((CB_SB_Q, HEAD_DIM), (CB_MB_Q, HEAD_DIM), (CB_NS_Q, HEAD_DIM), (CB_DF_Q, DIFF_HALF)):
        col_scale[:, cb * GROUP_WIDTH:(cb + 1) * GROUP_WIDTH] = width ** -0.5 * LOG2E
    col_scale = jnp.asarray(col_scale)

    x2 = x.reshape(bsz * s, d)
    for layer in range(depth):
        proj, kcv = _norm_matmul(x2, norm_attn[layer][None], w_in_c, layer, col_scale,
                                 gap=(COLS_BEFORE_PAD, COLS_BEFORE_PAD + PAD_COLS),
                                 side_col=CB_NS_A * GROUP_WIDTH, side_width=2 * HEAD_DIM)
        proj = proj.reshape(bsz, s, D_IN_PAD)
        lambda_init = 0.8 - 0.6 * math.exp(-0.3 * layer)
        o_sb = _run_mixer(_stick_breaking(proj), bsz, s, "stick_breaking")
        o_mb = _run_mixer(_moba(proj, bias_diag, bias_sub), bsz, s, "moba")
        o_ns = _run_mixer(_nsa(proj, kcv.reshape(bsz, s, 2 * HEAD_DIM), (cmp_pos_k, cmp_pos_v, wk1, wk2, wv1, wv2t),
                               layer, bias_diag, bias_sub, bias_cmp, cover_t, expand), bsz, s, "nsa")
        o_df = _run_mixer(_diff(proj, diff_lambda[layer], diff_subln[layer][:, None], bias_diag, bias_sub,
                                lambda_init), bsz, s, "diff_attention")
        groups = [o.reshape(bsz * s, GROUP_WIDTH) for o in (o_sb, o_mb, o_ns, o_df)]
        x2 = _out_mlp(x2, groups, w_out_c, norm_mlp[layer][None], w_up_c, w_down_c, layer,
                      final_norm[None], final_norm=(layer == depth - 1))
    return x2.reshape(bsz, s, d)
```
